```python
import math
import jax, jax.numpy as jnp
from jax import lax
import numpy as np

D_MODEL = 2048
BATCH = 4
SEQ = 2048
DEPTH = 1
DEC_BATCH = 32
DEC_SEQ = 4
PAST_LEN = 8192
PAGE_SIZE = 128

MIX_WIDTH = D_MODEL
A_WIDTH = MIX_WIDTH // 2
A_GROUPS = 8
A_CH = A_WIDTH // A_GROUPS
A_CHUNK = 128
B_WIDTH = MIX_WIDTH - A_WIDTH
B_HEADS = 8
B_HEAD_DIM = B_WIDTH // B_HEADS
B_KV_HEADS = 2
B_GQA = B_HEADS // B_KV_HEADS
IDX_HEADS = 16
IDX_DIM = 64
TOPK_MAX = 256
Q_BLOCK = 128
REL_BUCKETS = 32
REL_MAX_DIST = 128
MEM_LEN = 256
MEM_HEADS = 4
MEM_HEAD_DIM = 128
MEM_WIDTH = MEM_HEADS * MEM_HEAD_DIM
N_GROUPS = 4
EXPERTS_PER_GROUP = 4
N_EXPERTS = N_GROUPS * EXPERTS_PER_GROUP
TOP_K_IN_GROUP = 2
EXPERT_FF = 512
EPS = 1e-6
IN_SIZES = (A_WIDTH, A_WIDTH, B_HEADS * B_HEAD_DIM, B_KV_HEADS * B_HEAD_DIM, B_KV_HEADS * B_HEAD_DIM, IDX_HEADS * IDX_DIM, IDX_DIM, IDX_HEADS)
IN_COLS = 2 * A_WIDTH + B_HEADS * B_HEAD_DIM + 2 * B_KV_HEADS * B_HEAD_DIM + IDX_HEADS * IDX_DIM + IDX_DIM + IDX_HEADS

kernel_name = 'hymba_gmlp_dsa_hmoe_decode_step'


def rms_norm(x, gain):
    xf = x.astype(jnp.float32)
    y = xf * lax.rsqrt(jnp.mean(xf * xf, axis=-1, keepdims=True) + EPS)
    return (y * gain.astype(jnp.float32)).astype(x.dtype)


def t5_bucket(dist):
    max_exact = REL_BUCKETS // 2
    d = jnp.maximum(dist, 0)
    ratio = jnp.log(jnp.maximum(d, 1).astype(jnp.float32) / max_exact) / math.log(REL_MAX_DIST / max_exact)
    large = jnp.minimum(max_exact + (ratio * (REL_BUCKETS - max_exact)).astype(jnp.int32), REL_BUCKETS - 1)
    return jnp.where(d < max_exact, d, large)


def gather_rows(rows, idx):
    return jax.vmap(lambda r, i: r[i])(rows, idx)


def project_mixer(n, w_in, a_vnorm, b_qnorm, b_knorm):
    B, T, _ = n.shape
    p = jnp.einsum('btd,dc->btc', n, w_in)
    offs = [int(o) for o in np.cumsum(IN_SIZES)[:-1]]
    a_u, a_v, b_q, b_k, b_v, i_q, i_k, i_w = jnp.split(p, offs, axis=-1)
    u = jax.nn.gelu(a_u, approximate=False)
    vn = rms_norm(jax.nn.gelu(a_v, approximate=False).reshape(B, T, A_GROUPS, A_CH), a_vnorm)
    q = rms_norm(b_q.reshape(B, T, B_HEADS, B_HEAD_DIM), b_qnorm)
    k = rms_norm(b_k.reshape(B, T, B_KV_HEADS, B_HEAD_DIM), b_knorm)
    v = b_v.reshape(B, T, B_KV_HEADS, B_HEAD_DIM)
    qi = i_q.reshape(B, T, IDX_HEADS, IDX_DIM)
    wi = i_w * (IDX_HEADS ** -0.5)
    return u, vn, q, k, v, qi, wi, i_k


def spatial_mix(vc, a_ws, a_b):
    P = vc.shape[2]
    ws = a_ws[:, :P, :P] * jnp.tril(jnp.ones((P, P), a_ws.dtype))
    return jnp.einsum('gts,bnsgc->bntgc', ws, vc) + a_b[:, :P].T[None, None, :, :, None]


def indexer_scores(qi, wi, ki):
    s = jnp.einsum('bqhd,bld->bqhl', qi, ki).astype(jnp.float32) * (IDX_DIM ** -0.5)
    return jnp.einsum('bqh,bqhl->bql', wi.astype(jnp.float32), jax.nn.relu(s))


def sparse_attend(q, k_sel, v_sel, rel_b, valid):
    B, Q, H, Dh = q.shape
    K = k_sel.shape[2]
    qg = q.reshape(B, Q, B_KV_HEADS, B_GQA, Dh)
    logits = jnp.einsum('bqkgd,bqskd->bqkgs', qg, k_sel).astype(jnp.float32) * (Dh ** -0.5)
    logits = logits + rel_b.astype(jnp.float32).reshape(B, Q, K, B_KV_HEADS, B_GQA).transpose(0, 1, 3, 4, 2)
    logits = jnp.where(valid[:, :, None, None, :], logits, -jnp.inf)
    p = jax.nn.softmax(logits, axis=-1).astype(v_sel.dtype)
    o = jnp.einsum('bqkgs,bqskd->bqkgd', p, v_sel)
    return o.reshape(B, Q, H * Dh)


def dsa_prompt(q, k, v, qi, wi, ki, rel_bias):
    B, T = q.shape[:2]
    topk = min(TOPK_MAX, T // 4)
    key_pos = jnp.arange(T)

    def block(i):
        s = i * Q_BLOCK
        qb = lax.dynamic_slice_in_dim(q, s, Q_BLOCK, axis=1)
        qib = lax.dynamic_slice_in_dim(qi, s, Q_BLOCK, axis=1)
        wib = lax.dynamic_slice_in_dim(wi, s, Q_BLOCK, axis=1)
        qpos = s + jnp.arange(Q_BLOCK)
        sc = indexer_scores(qib, wib, ki)
        sc = jnp.where(key_pos[None, None, :] <= qpos[None, :, None], sc, -jnp.inf)
        _, sel = lax.top_k(sc, topk)
        valid = sel <= qpos[None, :, None]
        rel_b = rel_bias[t5_bucket(qpos[None, :, None] - sel)]
        return sparse_attend(qb, gather_rows(k, sel), gather_rows(v, sel), rel_b, valid)

    out = lax.map(block, jnp.arange(T // Q_BLOCK))
    return out.transpose(1, 0, 2, 3).reshape(B, T, B_WIDTH)


def dsa_sample(q, k_new, v_new, qi, wi, ki_new, cache_k, cache_v, cache_idx_k, page_table, rel_bias):
    Bd, S = q.shape[:2]
    past = page_table.shape[1] * PAGE_SIZE
    n_keys = past + S
    topk = min(TOPK_MAX, n_keys // 4)
    ki_past = cache_idx_k[page_table].reshape(Bd, past, IDX_DIM)
    ki_all = jnp.concatenate([ki_past, ki_new.astype(ki_past.dtype)], axis=1)
    qpos = past + jnp.arange(S)
    sc = indexer_scores(qi, wi, ki_all)
    sc = jnp.where(jnp.arange(n_keys)[None, None, :] <= qpos[None, :, None], sc, -jnp.inf)
    _, sel = lax.top_k(sc, topk)
    valid = sel <= qpos[None, :, None]
    sp = jnp.minimum(sel, past - 1)
    phys = gather_rows(page_table, sp // PAGE_SIZE)
    off = sp % PAGE_SIZE
    sn = jnp.clip(sel - past, 0, S - 1)
    is_past = (sel < past)[..., None, None]
    k_sel = jnp.where(is_past, cache_k[phys, off], gather_rows(k_new, sn).astype(cache_k.dtype))
    v_sel = jnp.where(is_past, cache_v[phys, off], gather_rows(v_new, sn).astype(cache_v.dtype))
    rel_b = rel_bias[t5_bucket(qpos[None, :, None] - sel)]
    return sparse_attend(q, k_sel, v_sel, rel_b, valid)


def memory_kv(mem, mem_wk, mem_wv, mem_knorm):
    B, M, _ = mem.shape
    mk = rms_norm(jnp.einsum('bmd,dc->bmc', mem, mem_wk).reshape(B, M, MEM_HEADS, MEM_HEAD_DIM), mem_knorm)
    mv = jnp.einsum('bmd,dc->bmc', mem, mem_wv).reshape(B, M, MEM_HEADS, MEM_HEAD_DIM)
    return mk, mv


def memory_attend(n, mk, mv, mem_wq, mem_qnorm, mem_wo):
    B, T, _ = n.shape
    q = rms_norm(jnp.einsum('btd,dc->btc', n, mem_wq).reshape(B, T, MEM_HEADS, MEM_HEAD_DIM), mem_qnorm)
    logits = jnp.einsum('bthd,bmhd->bhtm', q, mk.astype(q.dtype)).astype(jnp.float32) * (MEM_HEAD_DIM ** -0.5)
    p = jax.nn.softmax(logits, axis=-1).astype(n.dtype)
    o = jnp.einsum('bhtm,bmhd->bthd', p, mv.astype(n.dtype)).reshape(B, T, MEM_WIDTH)
    return jnp.einsum('btc,cd->btd', o, mem_wo)


def hier_moe(n, w_group, b_group, w_router, b_router, w_gate, w_up, w_down):
    B, T, D = n.shape
    xf = n.reshape(B * T, D)
    g_prob = jax.nn.softmax((xf @ w_group + b_group).astype(jnp.float32), axis=-1)
    g_sel = jnp.argmax(g_prob, axis=-1)
    g_gate = jnp.take_along_axis(g_prob, g_sel[:, None], axis=1)
    e_logits = (xf @ w_router + b_router).astype(jnp.float32).reshape(-1, N_GROUPS, EXPERTS_PER_GROUP)
    e_logits = jnp.take_along_axis(e_logits, g_sel[:, None, None], axis=1)[:, 0]
    top_w, top_i = lax.top_k(jax.nn.softmax(e_logits, axis=-1), TOP_K_IN_GROUP)
    top_w = top_w / jnp.sum(top_w, axis=-1, keepdims=True) * g_gate
    expert_id = g_sel[:, None] * EXPERTS_PER_GROUP + top_i
    combine = jnp.einsum('nk,nke->ne', top_w, jax.nn.one_hot(expert_id, N_EXPERTS, dtype=jnp.float32)).astype(xf.dtype)
    y = jnp.zeros_like(xf)
    for e in range(N_EXPERTS):
        h = jax.nn.silu(xf @ w_gate[e]) * (xf @ w_up[e])
        y = y + combine[:, e:e + 1] * (h @ w_down[e])
    return y.reshape(B, T, D)


def post_mixer(h, mix, mk, mv, w_out, norm_mem, mem_wq, mem_qnorm, mem_wo, norm_ffn,
               w_group, b_group, w_router, b_router, w_gate, w_up, w_down):
    h = h + jnp.einsum('btc,cd->btd', mix, w_out)
    h = h + memory_attend(rms_norm(h, norm_mem), mk, mv, mem_wq, mem_qnorm, mem_wo)
    return h + hier_moe(rms_norm(h, norm_ffn), w_group, b_group, w_router, b_router, w_gate, w_up, w_down)


def setup_inputs(seed: int = 0) -> dict:
    key = jax.random.key(seed)
    ks = jax.random.split(key, 40)

    def nrm(i, shape, scale):
        return jax.random.normal(ks[i], shape, jnp.float32) * scale

    def gain(i, shape):
        return 1.0 + nrm(i, shape, 0.02)

    L = DEPTH
    n_pages = PAST_LEN // PAGE_SIZE
    n_pool = (DEC_BATCH * n_pages * 5 + 3) // 4
    perm = jax.random.permutation(ks[0], n_pool)
    page_table = perm[:DEC_BATCH * n_pages].reshape(DEC_BATCH, n_pages).astype(jnp.int32)
    return {
        'x_prompt': nrm(1, (BATCH, SEQ, D_MODEL), 1.0),
        'x_sample': nrm(2, (DEC_BATCH, DEC_SEQ, D_MODEL), 1.0),
        'cache_k': nrm(3, (L, n_pool, PAGE_SIZE, B_KV_HEADS, B_HEAD_DIM), 1.0),
        'cache_v': nrm(4, (L, n_pool, PAGE_SIZE, B_KV_HEADS, B_HEAD_DIM), 1.0),
        'cache_idx_k': nrm(5, (L, n_pool, PAGE_SIZE, IDX_DIM), 1.0),
        'cache_mem_k': nrm(6, (L, DEC_BATCH, MEM_LEN, MEM_HEADS, MEM_HEAD_DIM), 1.0),
        'cache_mem_v': nrm(7, (L, DEC_BATCH, MEM_LEN, MEM_HEADS, MEM_HEAD_DIM), 1.0),
        'page_table': page_table,
        'mem_prompt': nrm(8, (BATCH, MEM_LEN, D_MODEL), 1.0),
        'norm_mix': gain(9, (L, D_MODEL)),
        'w_in': nrm(10, (L, D_MODEL, IN_COLS), D_MODEL ** -0.5),
        'a_vnorm': gain(11, (L, A_GROUPS, A_CH)),
        'a_ws': nrm(12, (L, A_GROUPS, A_CHUNK, A_CHUNK), A_CHUNK ** -0.5),
        'a_b': gain(13, (L, A_GROUPS, A_CHUNK)),
        'b_qnorm': gain(14, (L, B_HEAD_DIM)),
        'b_knorm': gain(15, (L, B_HEAD_DIM)),
        'rel_bias': nrm(16, (REL_BUCKETS, B_HEADS), 0.5),
        'w_out': nrm(17, (L, MIX_WIDTH, D_MODEL), MIX_WIDTH ** -0.5),
        'norm_mem': gain(18, (L, D_MODEL)),
        'mem_wq': nrm(19, (L, D_MODEL, MEM_WIDTH), D_MODEL ** -0.5),
        'mem_wk': nrm(20, (L, D_MODEL, MEM_WIDTH), D_MODEL ** -0.5),
        'mem_wv': nrm(21, (L, D_MODEL, MEM_WIDTH), D_MODEL ** -0.5),
        'mem_qnorm': gain(22, (L, MEM_HEAD_DIM)),
        'mem_knorm': gain(23, (L, MEM_HEAD_DIM)),
        'mem_wo': nrm(24, (L, MEM_WIDTH, D_MODEL), MEM_WIDTH ** -0.5),
        'norm_ffn': gain(25, (L, D_MODEL)),
        'w_group': nrm(26, (L, D_MODEL, N_GROUPS), D_MODEL ** -0.5),
        'b_group': nrm(27, (L, N_GROUPS), 0.01),
        'w_router': nrm(28, (L, D_MODEL, N_EXPERTS), D_MODEL ** -0.5),
        'b_router': nrm(29, (L, N_EXPERTS), 0.01),
        'w_gate': nrm(30, (L, N_EXPERTS, D_MODEL, EXPERT_FF), D_MODEL ** -0.5),
        'w_up': nrm(31, (L, N_EXPERTS, D_MODEL, EXPERT_FF), D_MODEL ** -0.5),
        'w_down': nrm(32, (L, N_EXPERTS, EXPERT_FF, D_MODEL), EXPERT_FF ** -0.5),
    }


def reference(x_prompt, x_sample, cache_k, cache_v, cache_idx_k, cache_mem_k, cache_mem_v, page_table,
              mem_prompt, norm_mix, w_in, a_vnorm, a_ws, a_b, b_qnorm, b_knorm, rel_bias, w_out,
              norm_mem, mem_wq, mem_wk, mem_wv, mem_qnorm, mem_knorm, mem_wo, norm_ffn,
              w_group, b_group, w_router, b_router, w_gate, w_up, w_down):
    hp, hs = x_prompt, x_sample
    Bp, T, _ = x_prompt.shape
    Bd, S, _ = x_sample.shape
    k_p, v_p, ki_p, mk_p, mv_p, k_s, v_s, ki_s, cv_s = [], [], [], [], [], [], [], [], []
    for l in range(DEPTH):
        ffn_args = (w_out[l], norm_mem[l], mem_wq[l], mem_qnorm[l], mem_wo[l], norm_ffn[l],
                    w_group[l], b_group[l], w_router[l], b_router[l], w_gate[l], w_up[l], w_down[l])
        u, vn, q, k, v, qi, wi, ki = project_mixer(rms_norm(hp, norm_mix[l]), w_in[l], a_vnorm[l], b_qnorm[l], b_knorm[l])
        a_out = u * spatial_mix(vn.reshape(Bp, T // A_CHUNK, A_CHUNK, A_GROUPS, A_CH), a_ws[l], a_b[l]).reshape(Bp, T, A_WIDTH)
        b_out = dsa_prompt(q, k, v, qi, wi, ki, rel_bias)
        mk, mv = memory_kv(mem_prompt, mem_wk[l], mem_wv[l], mem_knorm[l])
        hp = post_mixer(hp, jnp.concatenate([a_out, b_out], axis=-1), mk, mv, *ffn_args)
        k_p.append(k); v_p.append(v); ki_p.append(ki); mk_p.append(mk); mv_p.append(mv)
        u, vn, q, k, v, qi, wi, ki = project_mixer(rms_norm(hs, norm_mix[l]), w_in[l], a_vnorm[l], b_qnorm[l], b_knorm[l])
        a_out = u * spatial_mix(vn[:, None], a_ws[l], a_b[l]).reshape(Bd, S, A_WIDTH)
        b_out = dsa_sample(q, k, v, qi, wi, ki, cache_k[l], cache_v[l], cache_idx_k[l], page_table, rel_bias)
        hs = post_mixer(hs, jnp.concatenate([a_out, b_out], axis=-1), cache_mem_k[l], cache_mem_v[l], *ffn_args)
        k_s.append(k); v_s.append(v); ki_s.append(ki); cv_s.append(vn)
    new_k_prompt = jnp.stack(k_p)
    new_v_prompt = jnp.stack(v_p)
    new_idx_k_prompt = jnp.stack(ki_p)
    new_mem_k_prompt = jnp.stack(mk_p)
    new_mem_v_prompt = jnp.stack(mv_p)
    new_k_sample = jnp.stack(k_s)
    new_v_sample = jnp.stack(v_s)
    new_idx_k_sample = jnp.stack(ki_s)
    new_chunk_v_sample = jnp.stack(cv_s)
    return (hp, hs, new_k_prompt, new_v_prompt, new_idx_k_prompt, new_mem_k_prompt, new_mem_v_prompt,
            new_k_sample, new_v_sample, new_idx_k_sample, new_chunk_v_sample)
```

```python
import functools
import math

import jax
import jax.numpy as jnp
from jax import lax
from jax.experimental import pallas as pl
from jax.experimental.pallas import tpu as pltpu

BF16 = jnp.bfloat16
F32 = jnp.float32
I32 = jnp.int32

D_MODEL = 2048
A_GROUPS = 8
A_CH = 128
A_WIDTH = A_GROUPS * A_CH
A_CHUNK = 128
B_HEADS = 8
B_HEAD_DIM = 128
B_KV_HEADS = 2
B_GQA = B_HEADS // B_KV_HEADS
B_WIDTH = B_HEADS * B_HEAD_DIM
KV_WIDTH = B_KV_HEADS * B_HEAD_DIM
IDX_HEADS = 16
IDX_DIM = 64
TOPK_MAX = 256
Q_BLOCK = 128
REL_BUCKETS = 32
REL_MAX_DIST = 128
MEM_HEADS = 4
MEM_HEAD_DIM = 128
MEM_WIDTH = MEM_HEADS * MEM_HEAD_DIM
N_GROUPS = 4
EXPERTS_PER_GROUP = 4
N_EXPERTS = N_GROUPS * EXPERTS_PER_GROUP
EXPERT_FF = 512
PAGE_SIZE = 128
EPS = 1e-6

COL_U = 0
COL_V = COL_U + A_WIDTH
COL_Q = COL_V + A_WIDTH
COL_K = COL_Q + B_WIDTH
COL_VV = COL_K + KV_WIDTH
COL_QI = COL_VV + KV_WIDTH
COL_MAIN = COL_QI + IDX_HEADS * IDX_DIM
TAIL_COLS = IDX_DIM + IDX_HEADS
LANES = 128
SUBLANES = 8
PROJ_TN = 512
KEY_CHUNK = 256
NEG_INIT = -1e30
VMEM_LIMIT = 56 * 1024 * 1024


def _cparams(sem):
    return pltpu.CompilerParams(dimension_semantics=sem, vmem_limit_bytes=VMEM_LIMIT)


def _gelu(x):
    return 0.5 * x * (1.0 + lax.erf(x * (2.0 ** -0.5)))


def _head_norm(a, gain, width=LANES):
    outs = []
    for c in range(a.shape[1] // width):
        blk = a[:, c * width:(c + 1) * width]
        ms = jnp.mean(blk * blk, axis=-1, keepdims=True)
        outs.append(blk * lax.rsqrt(ms + EPS) * gain[:, c * width:(c + 1) * width])
    return outs[0] if len(outs) == 1 else jnp.concatenate(outs, axis=-1)


def _row_norm(x, gain):
    ms = jnp.mean(x * x, axis=-1, keepdims=True)
    return x * lax.rsqrt(ms + EPS) * gain


def _dot_nt(a, b):
    return lax.dot_general(a, b, (((1,), (1,)), ((), ())), preferred_element_type=F32)


def _proj_kernel(x_ref, g_ref, w_ref, wt_ref, avn_ref, qn_ref, kn_ref, *rest, want_vn):
    if want_vn:
        p_ref, k_ref, v_ref, tail_ref, vn_ref, xn_scr = rest
    else:
        p_ref, k_ref, v_ref, tail_ref, xn_scr = rest
        vn_ref = None
    j = pl.program_id(1)
    ju = COL_V // PROJ_TN
    jv = COL_Q // PROJ_TN
    jq = COL_K // PROJ_TN
    jkv = COL_QI // PROJ_TN

    @pl.when(j == 0)
    def _():
        xn = _row_norm(x_ref[...], g_ref[...]).astype(BF16)
        xn_scr[...] = xn
        t = jnp.dot(xn, wt_ref[...].astype(BF16), preferred_element_type=F32)
        lane = lax.broadcasted_iota(I32, t.shape, 1)
        tail_ref[...] = jnp.where(lane >= IDX_DIM, t * (IDX_HEADS ** -0.5), t)

    acc = jnp.dot(xn_scr[...], w_ref[...].astype(BF16), preferred_element_type=F32)

    @pl.when(j < ju)
    def _():
        p_ref[...] = _gelu(acc).astype(BF16)

    @pl.when((j >= ju) & (j < jv))
    def _():
        vn = _head_norm(_gelu(acc), avn_ref[...])
        p_ref[...] = vn.astype(BF16)
        if vn_ref is not None:
            vn_ref[...] = vn

    @pl.when((j >= jv) & (j < jq))
    def _():
        p_ref[...] = _head_norm(acc, qn_ref[...]).astype(BF16)

    @pl.when(j == jq)
    def _():
        k = _head_norm(acc[:, :KV_WIDTH], kn_ref[...])
        v = acc[:, KV_WIDTH:]
        k_ref[...] = k
        v_ref[...] = v
        p_ref[...] = jnp.concatenate([k, v], axis=-1).astype(BF16)

    @pl.when(j >= jkv)
    def _():
        p_ref[...] = acc.astype(BF16)


def _proj(x2d, gain, w_in_l, a_vnorm_l, b_qnorm_l, b_knorm_l, *, tm, want_vn):
    n = x2d.shape[0]
    nj = COL_MAIN // PROJ_TN
    w_tail = jnp.pad(w_in_l[:, COL_MAIN:], ((0, 0), (0, LANES - TAIL_COLS)))
    avn = a_vnorm_l.reshape(1, A_WIDTH)
    qn = jnp.tile(b_qnorm_l, PROJ_TN // B_HEAD_DIM).reshape(1, PROJ_TN)
    kn = jnp.tile(b_knorm_l, B_KV_HEADS).reshape(1, KV_WIDTH)
    ju = COL_V // PROJ_TN
    nv = A_WIDTH // PROJ_TN

    def vmap_(i, j):
        return (0, jnp.clip(j - ju, 0, nv - 1))

    in_specs = [
        pl.BlockSpec((tm, D_MODEL), lambda i, j: (i, 0)),
        pl.BlockSpec((1, D_MODEL), lambda i, j: (0, 0)),
        pl.BlockSpec((D_MODEL, PROJ_TN), lambda i, j: (0, j)),
        pl.BlockSpec((D_MODEL, LANES), lambda i, j: (0, 0)),
        pl.BlockSpec((1, PROJ_TN), vmap_),
        pl.BlockSpec((1, PROJ_TN), lambda i, j: (0, 0)),
        pl.BlockSpec((1, KV_WIDTH), lambda i, j: (0, 0)),
    ]
    out_shape = [
        jax.ShapeDtypeStruct((n, COL_MAIN), BF16),
        jax.ShapeDtypeStruct((n, KV_WIDTH), F32),
        jax.ShapeDtypeStruct((n, KV_WIDTH), F32),
        jax.ShapeDtypeStruct((n, LANES), F32),
    ]
    out_specs = [
        pl.BlockSpec((tm, PROJ_TN), lambda i, j: (i, j)),
        pl.BlockSpec((tm, KV_WIDTH), lambda i, j: (i, 0)),
        pl.BlockSpec((tm, KV_WIDTH), lambda i, j: (i, 0)),
        pl.BlockSpec((tm, LANES), lambda i, j: (i, 0)),
    ]
    if want_vn:
        out_shape.append(jax.ShapeDtypeStruct((n, A_WIDTH), F32))
        out_specs.append(pl.BlockSpec((tm, PROJ_TN), lambda i, j: (i, jnp.clip(j - ju, 0, nv - 1))))
    return pl.pallas_call(
        functools.partial(_proj_kernel, want_vn=want_vn),
        grid=(n // tm, nj),
        in_specs=in_specs,
        out_specs=out_specs,
        out_shape=out_shape,
        scratch_shapes=[pltpu.VMEM((tm, D_MODEL), BF16)],
        compiler_params=_cparams(("arbitrary", "arbitrary")),
        name="proj",
    )(x2d, gain.reshape(1, D_MODEL), w_in_l, w_tail, avn, qn, kn)


def _t5_bucket(d):
    max_exact = REL_BUCKETS // 2
    d = jnp.maximum(d, 0)
    ratio = jnp.log(jnp.maximum(d, 1).astype(F32) / max_exact) / math.log(REL_MAX_DIST / max_exact)
    large = jnp.minimum(max_exact + (ratio * (REL_BUCKETS - max_exact)).astype(I32), REL_BUCKETS - 1)
    return jnp.where(d < max_exact, d, large)


def _bias_lookup(bucket, rb_ref, h):
    acc = jnp.zeros(bucket.shape, F32)
    for r in range(REL_BUCKETS):
        acc = jnp.where(bucket == r, rb_ref[r, h], acc)
    return acc


def _bias_kernel(rb_ref, pair_ref, samp_ref, sconst_ref):
    q = lax.broadcasted_iota(I32, (Q_BLOCK, Q_BLOCK), 0)
    kc = lax.broadcasted_iota(I32, (Q_BLOCK, Q_BLOCK), 1)
    nvar = pair_ref.shape[0]
    rels = range(-(nvar - 1), 2)
    buckets = {r: _t5_bucket(q - kc - r * Q_BLOCK) for r in rels}
    for h in range(B_HEADS):
        g, hh = divmod(h, B_GQA)
        tiles = {r: _bias_lookup(buckets[r], rb_ref, h) for r in rels}
        for var in range(nvar):
            r0 = var - (nvar - 1)
            for t in range(2):
                pair_ref[var, g, hh * Q_BLOCK:(hh + 1) * Q_BLOCK, t * Q_BLOCK:(t + 1) * Q_BLOCK] = tiles[r0 + t]
    rows = B_GQA * SUBLANES
    row = lax.broadcasted_iota(I32, (rows, 2 * PAGE_SIZE), 0)
    col = lax.broadcasted_iota(I32, (rows, 2 * PAGE_SIZE), 1)
    t = row % SUBLANES
    d = jnp.where(col < PAGE_SIZE, t + PAGE_SIZE - col, t - (col - PAGE_SIZE))
    bucket = _t5_bucket(d)
    far = jnp.full((rows, LANES), REL_BUCKETS - 1, I32)
    rowc = lax.broadcasted_iota(I32, (rows, LANES), 0)
    for g in range(B_KV_HEADS):
        acc = jnp.zeros((rows, 2 * PAGE_SIZE), F32)
        accc = jnp.zeros((rows, LANES), F32)
        for hh in range(B_GQA):
            h = g * B_GQA + hh
            acc = jnp.where(row // SUBLANES == hh, _bias_lookup(bucket, rb_ref, h), acc)
            accc = jnp.where(rowc // SUBLANES == hh, _bias_lookup(far, rb_ref, h), accc)
        samp_ref[g] = acc
        sconst_ref[g] = accc


N_PAIR_VARIANTS = 4


def _bias_tables(rel_bias):
    rows = B_GQA * SUBLANES
    return pl.pallas_call(
        _bias_kernel,
        in_specs=[pl.BlockSpec(memory_space=pltpu.SMEM)],
        out_shape=[
            jax.ShapeDtypeStruct((N_PAIR_VARIANTS, B_KV_HEADS, B_GQA * Q_BLOCK, KEY_CHUNK), F32),
            jax.ShapeDtypeStruct((B_KV_HEADS, rows, 2 * PAGE_SIZE), F32),
            jax.ShapeDtypeStruct((B_KV_HEADS, rows, LANES), F32),
        ],
        name="bias_tables",
    )(rel_bias)


def _key_to_f32(key):
    bits = jnp.where(key < 0, key ^ I32(-2 ** 31), ~key)
    return lax.bitcast_convert_type(bits, F32)


def _kth_largest(count_ge, shape, k):
    def body(it, prefix):
        bit = 31 - it
        cand = prefix | lax.shift_left(I32(1), bit)
        cnt = count_ge(_key_to_f32(cand))
        return jnp.where(cnt >= k, cand, prefix)

    prefix = lax.fori_loop(0, 32, body, jnp.zeros(shape, I32))
    thr = _key_to_f32(prefix)
    return jnp.where((prefix & I32(-2 ** 23)) == 0, -jnp.inf, thr)


def _dsa_prompt_kernel(q_ref, kv_ref, qia_ref, qib_ref, tailk_ref, tailq_ref, bias_ref, o_ref,
                       kil_scr, kir_scr, sc_scr, *, topk):
    i = pl.program_id(1)

    @pl.when(i == 0)
    def _():
        ki = tailk_ref[:, :IDX_DIM].astype(BF16)
        z = jnp.zeros_like(ki)
        kil_scr[...] = jnp.concatenate([ki, z], axis=-1)
        kir_scr[...] = jnp.concatenate([z, ki], axis=-1)

    wi = tailq_ref[:, IDX_DIM:IDX_DIM + IDX_HEADS]
    nk = (i + 2) // 2
    qpos = i * Q_BLOCK + lax.broadcasted_iota(I32, (Q_BLOCK, 1), 0)
    kloc = lax.broadcasted_iota(I32, (1, KEY_CHUNK), 1)
    heads_per_ref = qia_ref.shape[1] // IDX_DIM

    def score_chunk(c, carry):
        off = pl.multiple_of(c * KEY_CHUNK, KEY_CHUNK)
        kl = kil_scr[pl.ds(off, KEY_CHUNK), :]
        kr = kir_scr[pl.ds(off, KEY_CHUNK), :]
        acc = jnp.zeros((Q_BLOCK, KEY_CHUNK), F32)
        for hp in range(IDX_HEADS // 2):
            ref = qia_ref if 2 * hp < heads_per_ref else qib_ref
            base = (2 * hp) % heads_per_ref * IDX_DIM
            qp = ref[:, base:base + 2 * IDX_DIM]
            for side, kk in ((0, kl), (1, kr)):
                h = 2 * hp + side
                s = _dot_nt(qp, kk) * (IDX_DIM ** -0.5)
                acc = acc + wi[:, h:h + 1] * jnp.maximum(s, 0.0)
        kpos = c * KEY_CHUNK + kloc
        sc_scr[c] = jnp.where(kpos <= qpos, acc, -jnp.inf)
        return carry

    lax.fori_loop(0, nk, score_chunk, 0)

    def count_ge(t):
        def body(c, cnt):
            m = jnp.where(sc_scr[c] >= t, 1.0, 0.0)
            return cnt + m[:, :LANES] + m[:, LANES:]
        cnt = lax.fori_loop(0, nk, body, jnp.zeros((Q_BLOCK, LANES), F32))
        return jnp.sum(cnt, axis=-1, keepdims=True)

    thr = _kth_largest(count_ge, (Q_BLOCK, 1), topk)

    scale = B_HEAD_DIM ** -0.5
    rows = B_GQA * Q_BLOCK
    for g in range(B_KV_HEADS):
        qg = jnp.concatenate(
            [q_ref[:, (g * B_GQA + hh) * B_HEAD_DIM:(g * B_GQA + hh + 1) * B_HEAD_DIM] for hh in range(B_GQA)],
            axis=0)

        def att_chunk(c, carry, g=g, qg=qg):
            m_old, l_old, acc = carry
            off = pl.multiple_of(c * KEY_CHUNK, KEY_CHUNK)
            kc = kv_ref[pl.ds(off, KEY_CHUNK), g * B_HEAD_DIM:(g + 1) * B_HEAD_DIM]
            vc = kv_ref[pl.ds(off, KEY_CHUNK), KV_WIDTH + g * B_HEAD_DIM:KV_WIDTH + (g + 1) * B_HEAD_DIM]
            var = jnp.clip(2 * c - i, -(N_PAIR_VARIANTS - 1), 0) + (N_PAIR_VARIANTS - 1)
            lg = _dot_nt(qg, kc) * scale + bias_ref[var, g]
            kpos = c * KEY_CHUNK + kloc
            sel = jnp.where((sc_scr[c] >= thr) & (kpos <= qpos), 0.0, -jnp.inf)
            lg = lg + jnp.concatenate([sel] * B_GQA, axis=0)
            m_new = jnp.maximum(m_old, jnp.max(lg, axis=-1, keepdims=True))
            alpha = jnp.exp(m_old - m_new)
            p = jnp.exp(lg - m_new)
            l_new = alpha * l_old + jnp.sum(p, axis=-1, keepdims=True)
            acc = alpha * acc + jnp.dot(p.astype(BF16), vc, preferred_element_type=F32)
            return m_new, l_new, acc

        init = (jnp.full((rows, 1), NEG_INIT, F32), jnp.zeros((rows, 1), F32),
                jnp.zeros((rows, B_HEAD_DIM), F32))
        _, l_fin, acc = lax.fori_loop(0, nk, att_chunk, init)
        o = acc / l_fin
        for hh in range(B_GQA):
            h = g * B_GQA + hh
            o_ref[:, h * B_HEAD_DIM:(h + 1) * B_HEAD_DIM] = o[hh * Q_BLOCK:(hh + 1) * Q_BLOCK].astype(BF16)


def _dsa_prompt(p, tail, bias_pairs, *, batch, seq):
    topk = min(TOPK_MAX, seq // 4)
    nblk = seq // Q_BLOCK
    qi_w = PROJ_TN
    return pl.pallas_call(
        functools.partial(_dsa_prompt_kernel, topk=topk),
        grid=(batch, nblk),
        in_specs=[
            pl.BlockSpec((Q_BLOCK, B_WIDTH), lambda b, i: (b * nblk + i, COL_Q // B_WIDTH)),
            pl.BlockSpec((seq, 2 * KV_WIDTH), lambda b, i: (b, COL_K // (2 * KV_WIDTH))),
            pl.BlockSpec((Q_BLOCK, qi_w), lambda b, i: (b * nblk + i, COL_QI // qi_w)),
            pl.BlockSpec((Q_BLOCK, qi_w), lambda b, i: (b * nblk + i, COL_QI // qi_w + 1)),
            pl.BlockSpec((seq, LANES), lambda b, i: (b, 0)),
            pl.BlockSpec((Q_BLOCK, LANES), lambda b, i: (b * nblk + i, 0)),
            pl.BlockSpec(bias_pairs.shape, lambda b, i: (0, 0, 0, 0)),
        ],
        out_specs=pl.BlockSpec((Q_BLOCK, B_WIDTH), lambda b, i: (b * nblk + i, 0)),
        out_shape=jax.ShapeDtypeStruct((batch * seq, B_WIDTH), BF16),
        scratch_shapes=[
            pltpu.VMEM((seq, 2 * IDX_DIM), BF16),
            pltpu.VMEM((seq, 2 * IDX_DIM), BF16),
            pltpu.VMEM((seq // KEY_CHUNK, Q_BLOCK, KEY_CHUNK), F32),
        ],
        compiler_params=_cparams(("arbitrary", "arbitrary")),
        name="dsa_prompt",
    )(p, p, p, p, tail, tail, bias_pairs)


PAGES_PER_STEP = 16


def _dsa_sample_score_kernel(pt_ref, *refs, topk, dec_seq, n_steps):
    del pt_ref
    pages = refs[:PAGES_PER_STEP]
    qi_ref, wi_ref, kin_ref, sc_ref, scn_ref, thr_ref = refs[PAGES_PER_STEP:]
    s = pl.program_id(1)
    chunk = PAGES_PER_STEP * PAGE_SIZE
    qi = qi_ref[...]
    wi = wi_ref[...]

    def scores(keys):
        r = jnp.maximum(_dot_nt(qi, keys) * (IDX_DIM ** -0.5), 0.0) * wi
        acc = r[0:SUBLANES]
        for h in range(1, IDX_HEADS):
            acc = acc + r[h * SUBLANES:(h + 1) * SUBLANES]
        return acc

    keys = jnp.concatenate([pg[...].astype(BF16) for pg in pages], axis=0)
    sc_ref[s] = scores(keys)

    @pl.when(s == n_steps - 1)
    def _():
        row = lax.broadcasted_iota(I32, (8, LANES), 0)
        col = lax.broadcasted_iota(I32, (8, LANES), 1)
        scn = jnp.where((col <= row) & (col < dec_seq), scores(kin_ref[...]), -jnp.inf)
        scn_ref[...] = scn

        def count_ge(t):
            cnt = jnp.where(scn >= t, 1.0, 0.0)
            for c in range(n_steps):
                for w in range(chunk // LANES):
                    cnt = cnt + jnp.where(sc_ref[c, :, w * LANES:(w + 1) * LANES] >= t, 1.0, 0.0)
            return jnp.sum(cnt, axis=-1, keepdims=True)

        thr = _kth_largest(count_ge, (8, 1), topk)
        thr_ref[...] = jnp.broadcast_to(thr, (8, LANES))


def _dsa_sample_attend_kernel(pt_ref, *refs, dec_seq, n_steps):
    del pt_ref
    kpages = refs[:PAGES_PER_STEP]
    vpages = refs[PAGES_PER_STEP:2 * PAGES_PER_STEP]
    (q_ref, sc_ref, scn_ref, thr_ref, kvn_ref, sbias_ref, sconst_ref, o_ref,
     m_scr, l_scr, acc_scr) = refs[2 * PAGES_PER_STEP:]
    s = pl.program_id(1)
    chunk = PAGES_PER_STEP * PAGE_SIZE
    rows = B_GQA * SUBLANES
    scale = B_HEAD_DIM ** -0.5

    @pl.when(s == 0)
    def _():
        m_scr[...] = jnp.full(m_scr.shape, NEG_INIT, F32)
        l_scr[...] = jnp.zeros(l_scr.shape, F32)
        acc_scr[...] = jnp.zeros(acc_scr.shape, F32)

    thr = thr_ref[:, 0:1]
    kall = jnp.concatenate([pg[...].astype(BF16) for pg in kpages], axis=0)
    vall = jnp.concatenate([pg[...].astype(BF16) for pg in vpages], axis=0)
    sel = jnp.where(sc_ref[...] >= thr, 0.0, -jnp.inf)
    sel = jnp.concatenate([sel] * B_GQA, axis=0)
    is_last = s == n_steps - 1
    col = lax.broadcasted_iota(I32, (rows, chunk), 1)

    def update(g, lg, v):
        m_old = m_scr[g]
        l_old = l_scr[g]
        m_new = jnp.maximum(m_old, jnp.max(lg, axis=-1, keepdims=True))
        alpha = jnp.exp(m_old - m_new)
        p = jnp.exp(lg - m_new)
        l_scr[g] = alpha * l_old + jnp.sum(p, axis=-1, keepdims=True)
        acc_scr[g] = alpha * acc_scr[g] + jnp.dot(p.astype(BF16), v, preferred_element_type=F32)
        m_scr[g] = m_new

    for g in range(B_KV_HEADS):
        qg = q_ref[g]
        lg = _dot_nt(qg, kall[:, g * B_HEAD_DIM:(g + 1) * B_HEAD_DIM]) * scale
        near = jnp.concatenate(
            [jnp.zeros((rows, chunk - PAGE_SIZE), F32), sbias_ref[g, :, :PAGE_SIZE]], axis=-1)
        bias = jnp.where(is_last & (col >= chunk - PAGE_SIZE), near, sconst_ref[g, :, 0:1])
        update(g, lg + bias + sel, vall[:, g * B_HEAD_DIM:(g + 1) * B_HEAD_DIM])

    @pl.when(is_last)
    def _():
        seln = jnp.where(scn_ref[...] >= thr, 0.0, -jnp.inf)
        seln = jnp.concatenate([seln] * B_GQA, axis=0)
        for g in range(B_KV_HEADS):
            kn = kvn_ref[:, g * B_HEAD_DIM:(g + 1) * B_HEAD_DIM]
            vn = kvn_ref[:, KV_WIDTH + g * B_HEAD_DIM:KV_WIDTH + (g + 1) * B_HEAD_DIM]
            lg = _dot_nt(q_ref[g], kn) * scale + sbias_ref[g, :, PAGE_SIZE:] + seln
            update(g, lg, vn)
            o_ref[g] = (acc_scr[g] / l_scr[g]).astype(BF16)


def _dsa_sample(p_s, tail_s, k_s, v_s, cache_k_l, cache_v_l, cache_idx_l, page_table, sbias, sconst,
                *, dec_batch, dec_seq):
    n_pages = page_table.shape[1]
    past = n_pages * PAGE_SIZE
    topk = min(TOPK_MAX, (past + dec_seq) // 4)
    n_steps = n_pages // PAGES_PER_STEP
    chunk = PAGES_PER_STEP * PAGE_SIZE
    rows = B_GQA * SUBLANES
    n_pool = cache_k_l.shape[0]

    qi = p_s[:, COL_QI:COL_MAIN].reshape(dec_batch, dec_seq, IDX_HEADS, IDX_DIM).transpose(0, 2, 1, 3)
    qi = jnp.pad(qi, ((0, 0), (0, 0), (0, SUBLANES - dec_seq), (0, 0))).reshape(dec_batch, IDX_HEADS * SUBLANES, IDX_DIM)
    wi = tail_s[:, IDX_DIM:IDX_DIM + IDX_HEADS].reshape(dec_batch, dec_seq, IDX_HEADS).transpose(0, 2, 1)
    wi = jnp.pad(wi, ((0, 0), (0, 0), (0, SUBLANES - dec_seq))).reshape(dec_batch, IDX_HEADS * SUBLANES, 1)
    ki_new = tail_s[:, :IDX_DIM].astype(BF16).reshape(dec_batch, dec_seq, IDX_DIM)
    ki_new = jnp.pad(ki_new, ((0, 0), (0, LANES - dec_seq), (0, 0)))
    q = p_s[:, COL_Q:COL_K].reshape(dec_batch, dec_seq, B_KV_HEADS, B_GQA, B_HEAD_DIM)
    q = jnp.pad(q.transpose(0, 2, 3, 1, 4), ((0, 0), (0, 0), (0, 0), (0, SUBLANES - dec_seq), (0, 0)))
    q = q.reshape(dec_batch, B_KV_HEADS, rows, B_HEAD_DIM)
    kv_new = jnp.concatenate([k_s, v_s], axis=-1).astype(BF16).reshape(dec_batch, dec_seq, 2 * KV_WIDTH)
    kv_new = jnp.pad(kv_new, ((0, 0), (0, LANES - dec_seq), (0, 0)))

    def page_map(r):
        return lambda b, s, pt: (pt[b, s * PAGES_PER_STEP + r], 0, 0)

    idx_specs = [pl.BlockSpec((None, PAGE_SIZE, IDX_DIM), page_map(r)) for r in range(PAGES_PER_STEP)]
    sc, scn, thr = pl.pallas_call(
        functools.partial(_dsa_sample_score_kernel, topk=topk, dec_seq=dec_seq, n_steps=n_steps),
        grid_spec=pltpu.PrefetchScalarGridSpec(
            num_scalar_prefetch=1,
            grid=(dec_batch, n_steps),
            in_specs=idx_specs + [
                pl.BlockSpec((None, IDX_HEADS * SUBLANES, IDX_DIM), lambda b, s, pt: (b, 0, 0)),
                pl.BlockSpec((None, IDX_HEADS * SUBLANES, 1), lambda b, s, pt: (b, 0, 0)),
                pl.BlockSpec((None, LANES, IDX_DIM), lambda b, s, pt: (b, 0, 0)),
            ],
            out_specs=[
                pl.BlockSpec((None, n_steps, 8, chunk), lambda b, s, pt: (b, 0, 0, 0)),
                pl.BlockSpec((None, 8, LANES), lambda b, s, pt: (b, 0, 0)),
                pl.BlockSpec((None, 8, LANES), lambda b, s, pt: (b, 0, 0)),
            ],
        ),
        out_shape=[
            jax.ShapeDtypeStruct((dec_batch, n_steps, 8, chunk), F32),
            jax.ShapeDtypeStruct((dec_batch, 8, LANES), F32),
            jax.ShapeDtypeStruct((dec_batch, 8, LANES), F32),
        ],
        compiler_params=_cparams(("arbitrary", "arbitrary")),
        name="dsa_sample_score",
    )(page_table, *([cache_idx_l] * PAGES_PER_STEP), qi, wi, ki_new)

    ck = cache_k_l.reshape(n_pool, PAGE_SIZE, KV_WIDTH)
    cv = cache_v_l.reshape(n_pool, PAGE_SIZE, KV_WIDTH)
    kv_specs = [pl.BlockSpec((None, PAGE_SIZE, KV_WIDTH), page_map(r)) for r in range(PAGES_PER_STEP)]
    o = pl.pallas_call(
        functools.partial(_dsa_sample_attend_kernel, dec_seq=dec_seq, n_steps=n_steps),
        grid_spec=pltpu.PrefetchScalarGridSpec(
            num_scalar_prefetch=1,
            grid=(dec_batch, n_steps),
            in_specs=kv_specs + kv_specs + [
                pl.BlockSpec((None, B_KV_HEADS, rows, B_HEAD_DIM), lambda b, s, pt: (b, 0, 0, 0)),
                pl.BlockSpec((None, None, 8, chunk), lambda b, s, pt: (b, s, 0, 0)),
                pl.BlockSpec((None, 8, LANES), lambda b, s, pt: (b, 0, 0)),
                pl.BlockSpec((None, 8, LANES), lambda b, s, pt: (b, 0, 0)),
                pl.BlockSpec((None, LANES, 2 * KV_WIDTH), lambda b, s, pt: (b, 0, 0)),
                pl.BlockSpec(sbias.shape, lambda b, s, pt: (0, 0, 0)),
                pl.BlockSpec(sconst.shape, lambda b, s, pt: (0, 0, 0)),
            ],
            out_specs=pl.BlockSpec((None, B_KV_HEADS, rows, B_HEAD_DIM), lambda b, s, pt: (b, 0, 0, 0)),
            scratch_shapes=[
                pltpu.VMEM((B_KV_HEADS, rows, 1), F32),
                pltpu.VMEM((B_KV_HEADS, rows, 1), F32),
                pltpu.VMEM((B_KV_HEADS, rows, B_HEAD_DIM), F32),
            ],
        ),
        out_shape=jax.ShapeDtypeStruct((dec_batch, B_KV_HEADS, rows, B_HEAD_DIM), BF16),
        compiler_params=_cparams(("arbitrary", "arbitrary")),
        name="dsa_sample_attend",
    )(page_table, *([ck] * PAGES_PER_STEP), *([cv] * PAGES_PER_STEP), q, sc, scn, thr, kv_new, sbias, sconst)
    o = o.reshape(dec_batch, B_KV_HEADS, B_GQA, SUBLANES, B_HEAD_DIM)[:, :, :, :dec_seq].transpose(0, 3, 1, 2, 4)
    return o.reshape(dec_batch * dec_seq, B_WIDTH)


def _mixout_kernel(u_ref, vn_ref, b_ref, ws_ref, ab_ref, x_ref, w_ref, o_ref, mix_scr, *, tm):
    j = pl.program_id(1)

    @pl.when(j == 0)
    def _():
        row = lax.broadcasted_iota(I32, (A_CHUNK, A_CHUNK), 0)
        col = lax.broadcasted_iota(I32, (A_CHUNK, A_CHUNK), 1)
        for g in range(A_GROUPS):
            wsg = jnp.where(row >= col, ws_ref[g], 0.0).astype(BF16)
            bg = ab_ref[:, g:g + 1]
            cs = slice(g * A_CH, (g + 1) * A_CH)
            for r in range(tm // A_CHUNK):
                rs = slice(r * A_CHUNK, (r + 1) * A_CHUNK)
                sm = jnp.dot(wsg, vn_ref[rs, cs], preferred_element_type=F32) + bg
                mix_scr[rs, cs] = (u_ref[rs, cs].astype(F32) * sm).astype(BF16)
        mix_scr[:, A_WIDTH:] = b_ref[...]

    o_ref[...] = x_ref[...] + jnp.dot(mix_scr[...], w_ref[...].astype(BF16), preferred_element_type=F32)


def _mixout(p, b_out, ws_eff, ab_eff, x2d, w_out_l, *, tm, tn=512):
    n = x2d.shape[0]
    return pl.pallas_call(
        functools.partial(_mixout_kernel, tm=tm),
        grid=(n // tm, D_MODEL // tn),
        in_specs=[
            pl.BlockSpec((tm, A_WIDTH), lambda i, j: (i, COL_U // A_WIDTH)),
            pl.BlockSpec((tm, A_WIDTH), lambda i, j: (i, COL_V // A_WIDTH)),
            pl.BlockSpec((tm, B_WIDTH), lambda i, j: (i, 0)),
            pl.BlockSpec((A_GROUPS, A_CHUNK, A_CHUNK), lambda i, j: (0, 0, 0)),
            pl.BlockSpec((A_CHUNK, A_GROUPS), lambda i, j: (0, 0)),
            pl.BlockSpec((tm, tn), lambda i, j: (i, j)),
            pl.BlockSpec((A_WIDTH + B_WIDTH, tn), lambda i, j: (0, j)),
        ],
        out_specs=pl.BlockSpec((tm, tn), lambda i, j: (i, j)),
        out_shape=jax.ShapeDtypeStruct((n, D_MODEL), F32),
        scratch_shapes=[pltpu.VMEM((tm, A_WIDTH + B_WIDTH), BF16)],
        compiler_params=_cparams(("arbitrary", "arbitrary")),
        name="mixout",
    )(p, p, b_out, ws_eff, ab_eff, x2d, w_out_l)


def _memkv_kernel(x_ref, wk_ref, wv_ref, kn_ref, mk_ref, mv_ref):
    x = x_ref[...].astype(BF16)
    mk_ref[...] = _head_norm(jnp.dot(x, wk_ref[...].astype(BF16), preferred_element_type=F32), kn_ref[...])
    mv_ref[...] = jnp.dot(x, wv_ref[...].astype(BF16), preferred_element_type=F32)


def _memkv(mem2d, wk, wv, knorm, *, tm=256):
    n = mem2d.shape[0]
    kn = jnp.tile(knorm, MEM_HEADS).reshape(1, MEM_WIDTH)
    return pl.pallas_call(
        _memkv_kernel,
        grid=(n // tm,),
        in_specs=[
            pl.BlockSpec((tm, D_MODEL), lambda i: (i, 0)),
            pl.BlockSpec((D_MODEL, MEM_WIDTH), lambda i: (0, 0)),
            pl.BlockSpec((D_MODEL, MEM_WIDTH), lambda i: (0, 0)),
            pl.BlockSpec((1, MEM_WIDTH), lambda i: (0, 0)),
        ],
        out_specs=[pl.BlockSpec((tm, MEM_WIDTH), lambda i: (i, 0))] * 2,
        out_shape=[jax.ShapeDtypeStruct((n, MEM_WIDTH), F32)] * 2,
        compiler_params=_cparams(("arbitrary",)),
        name="memkv",
    )(mem2d, wk, wv, kn)


def _memattn_kernel(h_ref, g_ref, wq_ref, qn_ref, mk_ref, mv_ref, wo_ref, o_ref, *, rows_per_mem, mem_len):
    h = h_ref[...]
    n = _row_norm(h, g_ref[...]).astype(BF16)
    q = _head_norm(jnp.dot(n, wq_ref[...].astype(BF16), preferred_element_type=F32), qn_ref[...]).astype(BF16)
    mk = mk_ref[...].astype(BF16)
    mv = mv_ref[...].astype(BF16)
    tm, m = h.shape[0], mk.shape[0]
    if rows_per_mem is not None:
        row = lax.broadcasted_iota(I32, (tm, m), 0)
        col = lax.broadcasted_iota(I32, (tm, m), 1)
        mask = jnp.where(row // rows_per_mem == col // mem_len, 0.0, -jnp.inf)
    outs = []
    for hd in range(MEM_HEADS):
        cs = slice(hd * MEM_HEAD_DIM, (hd + 1) * MEM_HEAD_DIM)
        lg = _dot_nt(q[:, cs], mk[:, cs]) * (MEM_HEAD_DIM ** -0.5)
        if rows_per_mem is not None:
            lg = lg + mask
        e = jnp.exp(lg - jnp.max(lg, axis=-1, keepdims=True))
        l = jnp.sum(e, axis=-1, keepdims=True)
        outs.append(jnp.dot(e.astype(BF16), mv[:, cs], preferred_element_type=F32) / l)
    o = jnp.concatenate(outs, axis=-1).astype(BF16)
    o_ref[...] = h + jnp.dot(o, wo_ref[...].astype(BF16), preferred_element_type=F32)


def _memattn(h2d, gain, wq, qnorm, mk, mv, wo, *, tm, mem_rows, tiles_per_mem, rows_per_mem, mem_len):
    n = h2d.shape[0]
    qn = jnp.tile(qnorm, MEM_HEADS).reshape(1, MEM_WIDTH)
    return pl.pallas_call(
        functools.partial(_memattn_kernel, rows_per_mem=rows_per_mem, mem_len=mem_len),
        grid=(n // tm,),
        in_specs=[
            pl.BlockSpec((tm, D_MODEL), lambda i: (i, 0)),
            pl.BlockSpec((1, D_MODEL), lambda i: (0, 0)),
            pl.BlockSpec((D_MODEL, MEM_WIDTH), lambda i: (0, 0)),
            pl.BlockSpec((1, MEM_WIDTH), lambda i: (0, 0)),
            pl.BlockSpec((mem_rows, MEM_WIDTH), lambda i: (i // tiles_per_mem, 0)),
            pl.BlockSpec((mem_rows, MEM_WIDTH), lambda i: (i // tiles_per_mem, 0)),
            pl.BlockSpec((MEM_WIDTH, D_MODEL), lambda i: (0, 0)),
        ],
        out_specs=pl.BlockSpec((tm, D_MODEL), lambda i: (i, 0)),
        out_shape=jax.ShapeDtypeStruct((n, D_MODEL), F32),
        compiler_params=_cparams(("arbitrary",)),
        name="memattn",
    )(h2d, gain.reshape(1, D_MODEL), wq, qn, mk, mv, wo)


def _router_kernel(h_ref, g_ref, w_ref, b_ref, n_ref, c_ref):
    n = _row_norm(h_ref[...], g_ref[...]).astype(BF16)
    n_ref[...] = n
    lg = jnp.dot(n, w_ref[...].astype(BF16), preferred_element_type=F32) + b_ref[...]
    gl, el = lg[:, :LANES], lg[:, LANES:]
    lane = lax.broadcasted_iota(I32, gl.shape, 1)
    lanef = lane.astype(F32)

    def first_max_lane(v, vmax):
        return jnp.min(jnp.where(v == vmax, lanef, float(LANES)), axis=-1, keepdims=True).astype(I32)

    gl = jnp.where(lane < N_GROUPS, gl, -jnp.inf)
    ge = jnp.exp(gl - jnp.max(gl, axis=-1, keepdims=True))
    gp = ge / jnp.sum(ge, axis=-1, keepdims=True)
    g_gate = jnp.max(gp, axis=-1, keepdims=True)
    g_sel = first_max_lane(gp, g_gate)
    in_grp = (lane // EXPERTS_PER_GROUP == g_sel) & (lane < N_EXPERTS)
    el = jnp.where(in_grp, el, -jnp.inf)
    ee = jnp.exp(el - jnp.max(el, axis=-1, keepdims=True))
    ep = jnp.where(in_grp, ee / jnp.sum(ee, axis=-1, keepdims=True), -jnp.inf)
    w1 = jnp.max(ep, axis=-1, keepdims=True)
    i1 = first_max_lane(ep, w1)
    ep2 = jnp.where(lane == i1, -jnp.inf, ep)
    w2 = jnp.max(ep2, axis=-1, keepdims=True)
    i2 = first_max_lane(ep2, w2)
    tot = w1 + w2
    c_ref[...] = jnp.where(lane == i1, w1 / tot * g_gate, jnp.where(lane == i2, w2 / tot * g_gate, 0.0))


def _router(h2d, gain, w_cat, b_cat, *, tm):
    n = h2d.shape[0]
    return pl.pallas_call(
        _router_kernel,
        grid=(n // tm,),
        in_specs=[
            pl.BlockSpec((tm, D_MODEL), lambda i: (i, 0)),
            pl.BlockSpec((1, D_MODEL), lambda i: (0, 0)),
            pl.BlockSpec((D_MODEL, 2 * LANES), lambda i: (0, 0)),
            pl.BlockSpec((1, 2 * LANES), lambda i: (0, 0)),
        ],
        out_specs=[pl.BlockSpec((tm, D_MODEL), lambda i: (i, 0)), pl.BlockSpec((tm, LANES), lambda i: (i, 0))],
        out_shape=[jax.ShapeDtypeStruct((n, D_MODEL), BF16), jax.ShapeDtypeStruct((n, LANES), F32)],
        compiler_params=_cparams(("arbitrary",)),
        name="router",
    )(h2d, gain.reshape(1, D_MODEL), w_cat, b_cat)


def _moe_kernel(n_ref, c_ref, h_ref, wg_ref, wu_ref, wd_ref, o_ref):
    e = pl.program_id(1)

    @pl.when(e == 0)
    def _():
        o_ref[...] = h_ref[...]

    x = n_ref[...]
    a = jnp.dot(x, wg_ref[...].astype(BF16), preferred_element_type=F32)
    u = jnp.dot(x, wu_ref[...].astype(BF16), preferred_element_type=F32)
    lane = lax.broadcasted_iota(I32, c_ref.shape, 1)
    c = jnp.sum(jnp.where(lane == e, c_ref[...], 0.0), axis=-1, keepdims=True)
    hid = (a * jax.nn.sigmoid(a) * u * c).astype(BF16)
    o_ref[...] += jnp.dot(hid, wd_ref[...].astype(BF16), preferred_element_type=F32)


def _moe(n3, comb, h2d, w_gate_l, w_up_l, w_down_l, *, tm):
    n = h2d.shape[0]
    return pl.pallas_call(
        _moe_kernel,
        grid=(n // tm, N_EXPERTS),
        in_specs=[
            pl.BlockSpec((tm, D_MODEL), lambda i, e: (i, 0)),
            pl.BlockSpec((tm, LANES), lambda i, e: (i, 0)),
            pl.BlockSpec((tm, D_MODEL), lambda i, e: (i, 0)),
            pl.BlockSpec((None, D_MODEL, EXPERT_FF), lambda i, e: (e, 0, 0)),
            pl.BlockSpec((None, D_MODEL, EXPERT_FF), lambda i, e: (e, 0, 0)),
            pl.BlockSpec((None, EXPERT_FF, D_MODEL), lambda i, e: (e, 0, 0)),
        ],
        out_specs=pl.BlockSpec((tm, D_MODEL), lambda i, e: (i, 0)),
        out_shape=jax.ShapeDtypeStruct((n, D_MODEL), F32),
        compiler_params=_cparams(("arbitrary", "arbitrary")),
        name="moe",
    )(n3, comb, h2d, w_gate_l, w_up_l, w_down_l)


def _post_mixer(p, b_out, ws_eff, ab_eff, x2d, mk, mv, lw, *, tm_mix, tm_mem, mem_rows, tiles_per_mem,
                rows_per_mem, mem_len, tm_route, tm_moe):
    h1 = _mixout(p, b_out, ws_eff, ab_eff, x2d, lw["w_out"], tm=tm_mix)
    h2 = _memattn(h1, lw["norm_mem"], lw["mem_wq"], lw["mem_qnorm"], mk, mv, lw["mem_wo"], tm=tm_mem,
                  mem_rows=mem_rows, tiles_per_mem=tiles_per_mem, rows_per_mem=rows_per_mem, mem_len=mem_len)
    n3, comb = _router(h2, lw["norm_ffn"], lw["w_cat"], lw["b_cat"], tm=tm_route)
    return _moe(n3, comb, h2, lw["w_gate"], lw["w_up"], lw["w_down"], tm=tm_moe)


def kernel(x_prompt, x_sample, cache_k, cache_v, cache_idx_k, cache_mem_k, cache_mem_v, page_table,
           mem_prompt, norm_mix, w_in, a_vnorm, a_ws, a_b, b_qnorm, b_knorm, rel_bias, w_out,
           norm_mem, mem_wq, mem_wk, mem_wv, mem_qnorm, mem_knorm, mem_wo, norm_ffn,
           w_group, b_group, w_router, b_router, w_gate, w_up, w_down):
    bp, t, d = x_prompt.shape
    bd, s, _ = x_sample.shape
    depth = w_in.shape[0]
    mem_len = mem_prompt.shape[1]
    assert d == D_MODEL and t % KEY_CHUNK == 0 and s <= 8 and (bd * s) % LANES == 0

    bias_pairs, sbias, sconst = _bias_tables(rel_bias)
    hp = x_prompt.reshape(bp * t, d)
    hs = x_sample.reshape(bd * s, d)
    outs = [[] for _ in range(9)]
    for l in range(depth):
        w_cat = jnp.zeros((d, 2 * LANES), F32)
        w_cat = w_cat.at[:, :N_GROUPS].set(w_group[l]).at[:, LANES:LANES + N_EXPERTS].set(w_router[l])
        b_cat = jnp.zeros((1, 2 * LANES), F32)
        b_cat = b_cat.at[0, :N_GROUPS].set(b_group[l]).at[0, LANES:LANES + N_EXPERTS].set(b_router[l])
        lw = dict(w_out=w_out[l], norm_mem=norm_mem[l], mem_wq=mem_wq[l], mem_qnorm=mem_qnorm[l],
                  mem_wo=mem_wo[l], norm_ffn=norm_ffn[l], w_cat=w_cat, b_cat=b_cat,
                  w_gate=w_gate[l], w_up=w_up[l], w_down=w_down[l])

        p, k, v, tail = _proj(hp, norm_mix[l], w_in[l], a_vnorm[l], b_qnorm[l], b_knorm[l],
                              tm=1024, want_vn=False)
        b_out = _dsa_prompt(p, tail, bias_pairs, batch=bp, seq=t)
        mk, mv = _memkv(mem_prompt.reshape(bp * mem_len, d), mem_wk[l], mem_wv[l], mem_knorm[l])
        hp = _post_mixer(p, b_out, a_ws[l], a_b[l].T, hp, mk, mv, lw, tm_mix=1024, tm_mem=512,
                         mem_rows=mem_len, tiles_per_mem=t // 512, rows_per_mem=None, mem_len=mem_len,
                         tm_route=512, tm_moe=512)
        outs[0].append(k.reshape(bp, t, B_KV_HEADS, B_HEAD_DIM))
        outs[1].append(v.reshape(bp, t, B_KV_HEADS, B_HEAD_DIM))
        outs[2].append(tail[:, :IDX_DIM].reshape(bp, t, IDX_DIM))
        outs[3].append(mk.reshape(bp, mem_len, MEM_HEADS, MEM_HEAD_DIM))
        outs[4].append(mv.reshape(bp, mem_len, MEM_HEADS, MEM_HEAD_DIM))

        ns = bd * s
        p_s, k_s, v_s, tail_s, vn_s = _proj(hs, norm_mix[l], w_in[l], a_vnorm[l], b_qnorm[l], b_knorm[l],
                                            tm=ns, want_vn=True)
        b_out_s = _dsa_sample(p_s, tail_s, k_s, v_s, cache_k[l], cache_v[l], cache_idx_k[l], page_table,
                              sbias, sconst, dec_batch=bd, dec_seq=s)
        ws_s = jnp.einsum("ab,gts->gatbs", jnp.eye(ns // s, dtype=F32), a_ws[l][:, :s, :s])
        ws_s = ws_s.reshape(A_GROUPS, ns, ns)
        ab_s = jnp.tile(a_b[l][:, :s].T, (ns // s, 1))
        mem_tile = 32
        hs = _post_mixer(p_s, b_out_s, ws_s, ab_s, hs,
                         cache_mem_k[l].reshape(bd * mem_len, MEM_WIDTH),
                         cache_mem_v[l].reshape(bd * mem_len, MEM_WIDTH), lw,
                         tm_mix=ns, tm_mem=mem_tile, mem_rows=mem_tile // s * mem_len, tiles_per_mem=1,
                         rows_per_mem=s, mem_len=mem_len, tm_route=ns, tm_moe=ns)
        outs[5].append(k_s.reshape(bd, s, B_KV_HEADS, B_HEAD_DIM))
        outs[6].append(v_s.reshape(bd, s, B_KV_HEADS, B_HEAD_DIM))
        outs[7].append(tail_s[:, :IDX_DIM].reshape(bd, s, IDX_DIM))
        outs[8].append(vn_s.reshape(bd, s, A_GROUPS, A_CH))
    return (hp.reshape(bp, t, d), hs.reshape(bd, s, d)) + tuple(jnp.stack(o) for o in outs)
```

```python
import functools
import math

import jax
import jax.numpy as jnp
from jax import lax
from jax.experimental import pallas as pl
from jax.experimental.pallas import tpu as pltpu

BF16 = jnp.bfloat16
F32 = jnp.float32
I32 = jnp.int32

D_MODEL = 2048
A_GROUPS = 8
A_CH = 128
A_WIDTH = A_GROUPS * A_CH
A_CHUNK = 128
B_HEADS = 8
B_HEAD_DIM = 128
B_KV_HEADS = 2
B_GQA = B_HEADS // B_KV_HEADS
B_WIDTH = B_HEADS * B_HEAD_DIM
KV_WIDTH = B_KV_HEADS * B_HEAD_DIM
IDX_HEADS = 16
IDX_DIM = 64
TOPK_MAX = 256
Q_BLOCK = 128
REL_BUCKETS = 32
REL_MAX_DIST = 128
MEM_HEADS = 4
MEM_HEAD_DIM = 128
MEM_WIDTH = MEM_HEADS * MEM_HEAD_DIM
N_GROUPS = 4
EXPERTS_PER_GROUP = 4
N_EXPERTS = N_GROUPS * EXPERTS_PER_GROUP
EXPERT_FF = 512
PAGE_SIZE = 128
EPS = 1e-6

COL_U = 0
COL_V = COL_U + A_WIDTH
COL_Q = COL_V + A_WIDTH
COL_K = COL_Q + B_WIDTH
COL_VV = COL_K + KV_WIDTH
COL_QI = COL_VV + KV_WIDTH
COL_MAIN = COL_QI + IDX_HEADS * IDX_DIM
TAIL_COLS = IDX_DIM + IDX_HEADS
LANES = 128
SUBLANES = 8
PROJ_TN = 512
KEY_CHUNK = 256
NEG_INIT = -1e30
VMEM_LIMIT = 56 * 1024 * 1024


def _cparams(sem):
    return pltpu.CompilerParams(dimension_semantics=sem, vmem_limit_bytes=VMEM_LIMIT)


def _gelu(x):
    return 0.5 * x * (1.0 + lax.erf(x * (2.0 ** -0.5)))


def _head_norm(a, gain, width=LANES):
    outs = []
    for c in range(a.shape[1] // width):
        blk = a[:, c * width:(c + 1) * width]
        ms = jnp.mean(blk * blk, axis=-1, keepdims=True)
        outs.append(blk * lax.rsqrt(ms + EPS) * gain[:, c * width:(c + 1) * width])
    return outs[0] if len(outs) == 1 else jnp.concatenate(outs, axis=-1)


def _row_norm(x, gain):
    ms = jnp.mean(x * x, axis=-1, keepdims=True)
    return x * lax.rsqrt(ms + EPS) * gain


def _dot_nt(a, b):
    return lax.dot_general(a, b, (((1,), (1,)), ((), ())), preferred_element_type=F32)


def _proj_kernel(x_ref, g_ref, w_ref, wt_ref, avn_ref, qn_ref, kn_ref, *rest, want_vn):
    if want_vn:
        p_ref, k_ref, v_ref, tail_ref, vn_ref, xn_scr = rest
    else:
        p_ref, k_ref, v_ref, tail_ref, xn_scr = rest
        vn_ref = None
    j = pl.program_id(1)
    ju = COL_V // PROJ_TN
    jv = COL_Q // PROJ_TN
    jq = COL_K // PROJ_TN
    jkv = COL_QI // PROJ_TN

    @pl.when(j == 0)
    def _():
        xn = _row_norm(x_ref[...], g_ref[...]).astype(BF16)
        xn_scr[...] = xn
        wlane = lax.broadcasted_iota(I32, wt_ref.shape, 1)
        wt = jnp.where(wlane < TAIL_COLS, wt_ref[...], 0.0).astype(BF16)
        t = jnp.dot(xn, wt, preferred_element_type=F32)
        lane = lax.broadcasted_iota(I32, t.shape, 1)
        tail_ref[...] = jnp.where(lane >= IDX_DIM, t * (IDX_HEADS ** -0.5), t)

    acc = jnp.dot(xn_scr[...], w_ref[...].astype(BF16), preferred_element_type=F32)

    @pl.when(j < ju)
    def _():
        p_ref[...] = _gelu(acc).astype(BF16)

    @pl.when((j >= ju) & (j < jv))
    def _():
        vn = _head_norm(_gelu(acc), avn_ref[...])
        p_ref[...] = vn.astype(BF16)
        if vn_ref is not None:
            vn_ref[...] = vn

    @pl.when((j >= jv) & (j < jq))
    def _():
        p_ref[...] = _head_norm(acc, qn_ref[...]).astype(BF16)

    @pl.when(j == jq)
    def _():
        k = _head_norm(acc[:, :KV_WIDTH], kn_ref[...])
        v = acc[:, KV_WIDTH:]
        k_ref[...] = k
        v_ref[...] = v
        p_ref[...] = jnp.concatenate([k, v], axis=-1).astype(BF16)

    @pl.when(j >= jkv)
    def _():
        p_ref[...] = acc.astype(BF16)


def _proj(x2d, gain, w_in_l, a_vnorm_l, b_qnorm_l, b_knorm_l, *, tm, want_vn):
    n = x2d.shape[0]
    nj = COL_MAIN // PROJ_TN
    avn = a_vnorm_l.reshape(1, A_WIDTH)
    qn = jnp.tile(b_qnorm_l, PROJ_TN // B_HEAD_DIM).reshape(1, PROJ_TN)
    kn = jnp.tile(b_knorm_l, B_KV_HEADS).reshape(1, KV_WIDTH)
    ju = COL_V // PROJ_TN
    nv = A_WIDTH // PROJ_TN

    def vmap_(i, j):
        return (0, jnp.clip(j - ju, 0, nv - 1))

    in_specs = [
        pl.BlockSpec((tm, D_MODEL), lambda i, j: (i, 0)),
        pl.BlockSpec((1, D_MODEL), lambda i, j: (0, 0)),
        pl.BlockSpec((D_MODEL, PROJ_TN), lambda i, j: (0, j)),
        pl.BlockSpec((D_MODEL, LANES), lambda i, j: (0, COL_MAIN // LANES)),
        pl.BlockSpec((1, PROJ_TN), vmap_),
        pl.BlockSpec((1, PROJ_TN), lambda i, j: (0, 0)),
        pl.BlockSpec((1, KV_WIDTH), lambda i, j: (0, 0)),
    ]
    out_shape = [
        jax.ShapeDtypeStruct((n, COL_MAIN), BF16),
        jax.ShapeDtypeStruct((n, KV_WIDTH), F32),
        jax.ShapeDtypeStruct((n, KV_WIDTH), F32),
        jax.ShapeDtypeStruct((n, LANES), F32),
    ]
    out_specs = [
        pl.BlockSpec((tm, PROJ_TN), lambda i, j: (i, j)),
        pl.BlockSpec((tm, KV_WIDTH), lambda i, j: (i, 0)),
        pl.BlockSpec((tm, KV_WIDTH), lambda i, j: (i, 0)),
        pl.BlockSpec((tm, LANES), lambda i, j: (i, 0)),
    ]
    if want_vn:
        out_shape.append(jax.ShapeDtypeStruct((n, A_WIDTH), F32))
        out_specs.append(pl.BlockSpec((tm, PROJ_TN), lambda i, j: (i, jnp.clip(j - ju, 0, nv - 1))))
    return pl.pallas_call(
        functools.partial(_proj_kernel, want_vn=want_vn),
        grid=(n // tm, nj),
        in_specs=in_specs,
        out_specs=out_specs,
        out_shape=out_shape,
        scratch_shapes=[pltpu.VMEM((tm, D_MODEL), BF16)],
        compiler_params=_cparams(("arbitrary", "arbitrary")),
        name="proj",
    )(x2d, gain.reshape(1, D_MODEL), w_in_l, w_in_l, avn, qn, kn)


def _t5_bucket(d):
    max_exact = REL_BUCKETS // 2
    d = jnp.maximum(d, 0)
    ratio = jnp.log(jnp.maximum(d, 1).astype(F32) / max_exact) / math.log(REL_MAX_DIST / max_exact)
    large = jnp.minimum(max_exact + (ratio * (REL_BUCKETS - max_exact)).astype(I32), REL_BUCKETS - 1)
    return jnp.where(d < max_exact, d, large)


def _bias_lookup(bucket, rb_ref, h):
    acc = jnp.zeros(bucket.shape, F32)
    for r in range(REL_BUCKETS):
        acc = jnp.where(bucket == r, rb_ref[r, h], acc)
    return acc


def _bias_kernel(rb_ref, pair_ref, samp_ref, sconst_ref):
    q = lax.broadcasted_iota(I32, (Q_BLOCK, Q_BLOCK), 0)
    kc = lax.broadcasted_iota(I32, (Q_BLOCK, Q_BLOCK), 1)
    nvar = pair_ref.shape[0]
    rels = range(-(nvar - 1), 2)
    buckets = {r: _t5_bucket(q - kc - r * Q_BLOCK) for r in rels}
    for h in range(B_HEADS):
        g, hh = divmod(h, B_GQA)
        tiles = {r: _bias_lookup(buckets[r], rb_ref, h) for r in rels}
        for var in range(nvar):
            r0 = var - (nvar - 1)
            for t in range(2):
                pair_ref[var, g, hh * Q_BLOCK:(hh + 1) * Q_BLOCK, t * Q_BLOCK:(t + 1) * Q_BLOCK] = tiles[r0 + t]
    rows = B_GQA * SUBLANES
    row = lax.broadcasted_iota(I32, (rows, 2 * PAGE_SIZE), 0)
    col = lax.broadcasted_iota(I32, (rows, 2 * PAGE_SIZE), 1)
    t = row % SUBLANES
    d = jnp.where(col < PAGE_SIZE, t + PAGE_SIZE - col, t - (col - PAGE_SIZE))
    bucket = _t5_bucket(d)
    far = jnp.full((rows, LANES), REL_BUCKETS - 1, I32)
    rowc = lax.broadcasted_iota(I32, (rows, LANES), 0)
    for g in range(B_KV_HEADS):
        acc = jnp.zeros((rows, 2 * PAGE_SIZE), F32)
        accc = jnp.zeros((rows, LANES), F32)
        for hh in range(B_GQA):
            h = g * B_GQA + hh
            acc = jnp.where(row // SUBLANES == hh, _bias_lookup(bucket, rb_ref, h), acc)
            accc = jnp.where(rowc // SUBLANES == hh, _bias_lookup(far, rb_ref, h), accc)
        samp_ref[g] = acc
        sconst_ref[g] = accc


N_PAIR_VARIANTS = 4


def _bias_tables(rel_bias):
    rows = B_GQA * SUBLANES
    return pl.pallas_call(
        _bias_kernel,
        in_specs=[pl.BlockSpec(memory_space=pltpu.SMEM)],
        out_shape=[
            jax.ShapeDtypeStruct((N_PAIR_VARIANTS, B_KV_HEADS, B_GQA * Q_BLOCK, KEY_CHUNK), F32),
            jax.ShapeDtypeStruct((B_KV_HEADS, rows, 2 * PAGE_SIZE), F32),
            jax.ShapeDtypeStruct((B_KV_HEADS, rows, LANES), F32),
        ],
        name="bias_tables",
    )(rel_bias)


def _key_to_f32(key):
    bits = jnp.where(key < 0, key ^ I32(-2 ** 31), ~key)
    return lax.bitcast_convert_type(bits, F32)


def _kth_largest(count_ge, shape, k):
    def body(it, prefix):
        bit = 31 - it
        cand = prefix | lax.shift_left(I32(1), bit)
        cnt = count_ge(_key_to_f32(cand))
        return jnp.where(cnt >= k, cand, prefix)

    prefix = lax.fori_loop(0, 32, body, jnp.zeros(shape, I32))
    thr = _key_to_f32(prefix)
    return jnp.where((prefix & I32(-2 ** 23)) == 0, -jnp.inf, thr)


def _dsa_prompt_kernel(q_ref, kv_ref, qia_ref, qib_ref, tailk_ref, tailq_ref, bias_ref, o_ref,
                       kil_scr, kir_scr, sc_scr, *, topk):
    i = pl.program_id(1)

    @pl.when(i == 0)
    def _():
        ki = tailk_ref[:, :IDX_DIM].astype(BF16)
        z = jnp.zeros_like(ki)
        kil_scr[...] = jnp.concatenate([ki, z], axis=-1)
        kir_scr[...] = jnp.concatenate([z, ki], axis=-1)

    wi = tailq_ref[:, IDX_DIM:IDX_DIM + IDX_HEADS]
    nk = (i + 2) // 2
    qpos = i * Q_BLOCK + lax.broadcasted_iota(I32, (Q_BLOCK, 1), 0)
    kloc = lax.broadcasted_iota(I32, (1, KEY_CHUNK), 1)
    heads_per_ref = qia_ref.shape[1] // IDX_DIM

    def score_chunk(c, carry):
        off = pl.multiple_of(c * KEY_CHUNK, KEY_CHUNK)
        kl = kil_scr[pl.ds(off, KEY_CHUNK), :]
        kr = kir_scr[pl.ds(off, KEY_CHUNK), :]
        acc = jnp.zeros((Q_BLOCK, KEY_CHUNK), F32)
        for hp in range(IDX_HEADS // 2):
            ref = qia_ref if 2 * hp < heads_per_ref else qib_ref
            base = (2 * hp) % heads_per_ref * IDX_DIM
            qp = ref[:, base:base + 2 * IDX_DIM]
            for side, kk in ((0, kl), (1, kr)):
                h = 2 * hp + side
                s = _dot_nt(qp, kk) * (IDX_DIM ** -0.5)
                acc = acc + wi[:, h:h + 1] * jnp.maximum(s, 0.0)
        kpos = c * KEY_CHUNK + kloc
        sc_scr[c] = jnp.where(kpos <= qpos, acc, -jnp.inf)
        return carry

    lax.fori_loop(0, nk, score_chunk, 0)

    def count_ge(t):
        def body(c, cnt):
            m = jnp.where(sc_scr[c] >= t, 1.0, 0.0)
            return cnt + m[:, :LANES] + m[:, LANES:]
        cnt = lax.fori_loop(0, nk, body, jnp.zeros((Q_BLOCK, LANES), F32))
        return jnp.sum(cnt, axis=-1, keepdims=True)

    thr = _kth_largest(count_ge, (Q_BLOCK, 1), topk)

    scale = B_HEAD_DIM ** -0.5
    rows = B_GQA * Q_BLOCK
    for g in range(B_KV_HEADS):
        qg = jnp.concatenate(
            [q_ref[:, (g * B_GQA + hh) * B_HEAD_DIM:(g * B_GQA + hh + 1) * B_HEAD_DIM] for hh in range(B_GQA)],
            axis=0)

        def att_chunk(c, carry, g=g, qg=qg):
            m_old, l_old, acc = carry
            off = pl.multiple_of(c * KEY_CHUNK, KEY_CHUNK)
            kc = kv_ref[pl.ds(off, KEY_CHUNK), g * B_HEAD_DIM:(g + 1) * B_HEAD_DIM]
            vc = kv_ref[pl.ds(off, KEY_CHUNK), KV_WIDTH + g * B_HEAD_DIM:KV_WIDTH + (g + 1) * B_HEAD_DIM]
            var = jnp.clip(2 * c - i, -(N_PAIR_VARIANTS - 1), 0) + (N_PAIR_VARIANTS - 1)
            lg = _dot_nt(qg, kc) * scale + bias_ref[var, g]
            kpos = c * KEY_CHUNK + kloc
            sel = jnp.where((sc_scr[c] >= thr) & (kpos <= qpos), 0.0, -jnp.inf)
            lg = lg + jnp.concatenate([sel] * B_GQA, axis=0)
            m_new = jnp.maximum(m_old, jnp.max(lg, axis=-1, keepdims=True))
            alpha = jnp.exp(m_old - m_new)
            p = jnp.exp(lg - m_new)
            l_new = alpha * l_old + jnp.sum(p, axis=-1, keepdims=True)
            acc = alpha * acc + jnp.dot(p.astype(BF16), vc, preferred_element_type=F32)
            return m_new, l_new, acc

        init = (jnp.full((rows, 1), NEG_INIT, F32), jnp.zeros((rows, 1), F32),
                jnp.zeros((rows, B_HEAD_DIM), F32))
        _, l_fin, acc = lax.fori_loop(0, nk, att_chunk, init)
        o = acc / l_fin
        for hh in range(B_GQA):
            h = g * B_GQA + hh
            o_ref[:, h * B_HEAD_DIM:(h + 1) * B_HEAD_DIM] = o[hh * Q_BLOCK:(hh + 1) * Q_BLOCK].astype(BF16)


def _dsa_prompt(p, tail, bias_pairs, *, batch, seq):
    topk = min(TOPK_MAX, seq // 4)
    nblk = seq // Q_BLOCK
    qi_w = PROJ_TN
    return pl.pallas_call(
        functools.partial(_dsa_prompt_kernel, topk=topk),
        grid=(batch, nblk),
        in_specs=[
            pl.BlockSpec((Q_BLOCK, B_WIDTH), lambda b, i: (b * nblk + i, COL_Q // B_WIDTH)),
            pl.BlockSpec((seq, 2 * KV_WIDTH), lambda b, i: (b, COL_K // (2 * KV_WIDTH))),
            pl.BlockSpec((Q_BLOCK, qi_w), lambda b, i: (b * nblk + i, COL_QI // qi_w)),
            pl.BlockSpec((Q_BLOCK, qi_w), lambda b, i: (b * nblk + i, COL_QI // qi_w + 1)),
            pl.BlockSpec((seq, LANES), lambda b, i: (b, 0)),
            pl.BlockSpec((Q_BLOCK, LANES), lambda b, i: (b * nblk + i, 0)),
            pl.BlockSpec(bias_pairs.shape, lambda b, i: (0, 0, 0, 0)),
        ],
        out_specs=pl.BlockSpec((Q_BLOCK, B_WIDTH), lambda b, i: (b * nblk + i, 0)),
        out_shape=jax.ShapeDtypeStruct((batch * seq, B_WIDTH), BF16),
        scratch_shapes=[
            pltpu.VMEM((seq, 2 * IDX_DIM), BF16),
            pltpu.VMEM((seq, 2 * IDX_DIM), BF16),
            pltpu.VMEM((seq // KEY_CHUNK, Q_BLOCK, KEY_CHUNK), F32),
        ],
        compiler_params=_cparams(("arbitrary", "arbitrary")),
        name="dsa_prompt",
    )(p, p, p, p, tail, tail, bias_pairs)


PAGES_PER_STEP = 16


def _dsa_sample_score_kernel(pt_ref, *refs, topk, dec_seq, n_steps):
    del pt_ref
    pages = refs[:PAGES_PER_STEP]
    qi_ref, wi_ref, kin_ref, sc_ref, scn_ref, thr_ref = refs[PAGES_PER_STEP:]
    s = pl.program_id(1)
    chunk = PAGES_PER_STEP * PAGE_SIZE
    qi = qi_ref[...]
    wi = wi_ref[...]

    def scores(keys_t):
        r = jnp.maximum(jnp.dot(qi, keys_t, preferred_element_type=F32) * (IDX_DIM ** -0.5), 0.0) * wi
        acc = r[0:SUBLANES]
        for h in range(1, IDX_HEADS):
            acc = acc + r[h * SUBLANES:(h + 1) * SUBLANES]
        return acc

    keys_t = jnp.concatenate([pg[...].astype(BF16) for pg in pages], axis=1)
    sc_ref[s] = scores(keys_t)

    @pl.when(s == n_steps - 1)
    def _():
        row = lax.broadcasted_iota(I32, (8, LANES), 0)
        col = lax.broadcasted_iota(I32, (8, LANES), 1)
        scn = jnp.where((col <= row) & (col < dec_seq), scores(kin_ref[...]), -jnp.inf)
        scn_ref[...] = scn

        def count_ge(t):
            cnt = jnp.where(scn >= t, 1.0, 0.0)
            for c in range(n_steps):
                for w in range(chunk // LANES):
                    cnt = cnt + jnp.where(sc_ref[c, :, w * LANES:(w + 1) * LANES] >= t, 1.0, 0.0)
            return jnp.sum(cnt, axis=-1, keepdims=True)

        thr = _kth_largest(count_ge, (8, 1), topk)
        thr_ref[...] = jnp.broadcast_to(thr, (8, LANES))


def _dsa_sample_attend_kernel(pt_ref, *refs, dec_seq, n_steps):
    del pt_ref
    kpages = refs[:PAGES_PER_STEP]
    vpages = refs[PAGES_PER_STEP:2 * PAGES_PER_STEP]
    (q_ref, sc_ref, scn_ref, thr_ref, kvn_ref, sbias_ref, sconst_ref, o_ref,
     m_scr, l_scr, acc_scr) = refs[2 * PAGES_PER_STEP:]
    s = pl.program_id(1)
    chunk = PAGES_PER_STEP * PAGE_SIZE
    rows = B_GQA * SUBLANES
    scale = B_HEAD_DIM ** -0.5

    @pl.when(s == 0)
    def _():
        m_scr[...] = jnp.full(m_scr.shape, NEG_INIT, F32)
        l_scr[...] = jnp.zeros(l_scr.shape, F32)
        acc_scr[...] = jnp.zeros(acc_scr.shape, F32)

    thr = thr_ref[:, 0:1]

    def head_rows(pgs, g):
        return jnp.concatenate(
            [pg[pl.ds(g, PAGE_SIZE, stride=B_KV_HEADS), :].astype(BF16) for pg in pgs], axis=0)

    sel = jnp.where(sc_ref[...] >= thr, 0.0, -jnp.inf)
    sel = jnp.concatenate([sel] * B_GQA, axis=0)
    is_last = s == n_steps - 1
    col = lax.broadcasted_iota(I32, (rows, chunk), 1)

    def update(g, lg, v):
        m_old = m_scr[g]
        l_old = l_scr[g]
        m_new = jnp.maximum(m_old, jnp.max(lg, axis=-1, keepdims=True))
        alpha = jnp.exp(m_old - m_new)
        p = jnp.exp(lg - m_new)
        l_scr[g] = alpha * l_old + jnp.sum(p, axis=-1, keepdims=True)
        acc_scr[g] = alpha * acc_scr[g] + jnp.dot(p.astype(BF16), v, preferred_element_type=F32)
        m_scr[g] = m_new

    for g in range(B_KV_HEADS):
        qg = q_ref[g]
        lg = _dot_nt(qg, head_rows(kpages, g)) * scale
        near = jnp.concatenate(
            [jnp.zeros((rows, chunk - PAGE_SIZE), F32), sbias_ref[g, :, :PAGE_SIZE]], axis=-1)
        bias = jnp.where(is_last & (col >= chunk - PAGE_SIZE), near, sconst_ref[g, :, 0:1])
        update(g, lg + bias + sel, head_rows(vpages, g))

    @pl.when(is_last)
    def _():
        seln = jnp.where(scn_ref[...] >= thr, 0.0, -jnp.inf)
        seln = jnp.concatenate([seln] * B_GQA, axis=0)
        for g in range(B_KV_HEADS):
            kn = kvn_ref[:, g * B_HEAD_DIM:(g + 1) * B_HEAD_DIM]
            vn = kvn_ref[:, KV_WIDTH + g * B_HEAD_DIM:KV_WIDTH + (g + 1) * B_HEAD_DIM]
            lg = _dot_nt(q_ref[g], kn) * scale + sbias_ref[g, :, PAGE_SIZE:] + seln
            update(g, lg, vn)
            o_ref[g] = (acc_scr[g] / l_scr[g]).astype(BF16)


def _dsa_sample(p_s, tail_s, k_s, v_s, cache_k_l, cache_v_l, cache_idx_l, page_table, sbias, sconst,
                *, dec_batch, dec_seq):
    n_pages = page_table.shape[1]
    past = n_pages * PAGE_SIZE
    topk = min(TOPK_MAX, (past + dec_seq) // 4)
    n_steps = n_pages // PAGES_PER_STEP
    chunk = PAGES_PER_STEP * PAGE_SIZE
    rows = B_GQA * SUBLANES
    n_pool = cache_k_l.shape[0]

    qi = p_s[:, COL_QI:COL_MAIN].reshape(dec_batch, dec_seq, IDX_HEADS, IDX_DIM).transpose(0, 2, 1, 3)
    qi = jnp.pad(qi, ((0, 0), (0, 0), (0, SUBLANES - dec_seq), (0, 0))).reshape(dec_batch, IDX_HEADS * SUBLANES, IDX_DIM)
    wi = tail_s[:, IDX_DIM:IDX_DIM + IDX_HEADS].reshape(dec_batch, dec_seq, IDX_HEADS).transpose(0, 2, 1)
    wi = jnp.pad(wi, ((0, 0), (0, 0), (0, SUBLANES - dec_seq))).reshape(dec_batch, IDX_HEADS * SUBLANES, 1)
    ki_new = tail_s[:, :IDX_DIM].astype(BF16).reshape(dec_batch, dec_seq, IDX_DIM)
    ki_new = jnp.pad(ki_new, ((0, 0), (0, LANES - dec_seq), (0, 0))).transpose(0, 2, 1)
    q = p_s[:, COL_Q:COL_K].reshape(dec_batch, dec_seq, B_KV_HEADS, B_GQA, B_HEAD_DIM)
    q = jnp.pad(q.transpose(0, 2, 3, 1, 4), ((0, 0), (0, 0), (0, 0), (0, SUBLANES - dec_seq), (0, 0)))
    q = q.reshape(dec_batch, B_KV_HEADS, rows, B_HEAD_DIM)
    kv_new = jnp.concatenate([k_s, v_s], axis=-1).astype(BF16).reshape(dec_batch, dec_seq, 2 * KV_WIDTH)
    kv_new = jnp.pad(kv_new, ((0, 0), (0, LANES - dec_seq), (0, 0)))

    def page_map(r):
        return lambda b, s, pt: (pt[b, s * PAGES_PER_STEP + r], 0, 0)

    cache_idx_t = jnp.swapaxes(cache_idx_l, 1, 2)
    idx_specs = [pl.BlockSpec((None, IDX_DIM, PAGE_SIZE), page_map(r)) for r in range(PAGES_PER_STEP)]
    sc, scn, thr = pl.pallas_call(
        functools.partial(_dsa_sample_score_kernel, topk=topk, dec_seq=dec_seq, n_steps=n_steps),
        grid_spec=pltpu.PrefetchScalarGridSpec(
            num_scalar_prefetch=1,
            grid=(dec_batch, n_steps),
            in_specs=idx_specs + [
                pl.BlockSpec((None, IDX_HEADS * SUBLANES, IDX_DIM), lambda b, s, pt: (b, 0, 0)),
                pl.BlockSpec((None, IDX_HEADS * SUBLANES, 1), lambda b, s, pt: (b, 0, 0)),
                pl.BlockSpec((None, IDX_DIM, LANES), lambda b, s, pt: (b, 0, 0)),
            ],
            out_specs=[
                pl.BlockSpec((None, n_steps, 8, chunk), lambda b, s, pt: (b, 0, 0, 0)),
                pl.BlockSpec((None, 8, LANES), lambda b, s, pt: (b, 0, 0)),
                pl.BlockSpec((None, 8, LANES), lambda b, s, pt: (b, 0, 0)),
            ],
        ),
        out_shape=[
            jax.ShapeDtypeStruct((dec_batch, n_steps, 8, chunk), F32),
            jax.ShapeDtypeStruct((dec_batch, 8, LANES), F32),
            jax.ShapeDtypeStruct((dec_batch, 8, LANES), F32),
        ],
        compiler_params=_cparams(("arbitrary", "arbitrary")),
        name="dsa_sample_score",
    )(page_table, *([cache_idx_t] * PAGES_PER_STEP), qi, wi, ki_new)

    ck = cache_k_l.reshape(n_pool, PAGE_SIZE * B_KV_HEADS, B_HEAD_DIM)
    cv = cache_v_l.reshape(n_pool, PAGE_SIZE * B_KV_HEADS, B_HEAD_DIM)
    kv_specs = [pl.BlockSpec((None, PAGE_SIZE * B_KV_HEADS, B_HEAD_DIM), page_map(r))
                for r in range(PAGES_PER_STEP)]
    o = pl.pallas_call(
        functools.partial(_dsa_sample_attend_kernel, dec_seq=dec_seq, n_steps=n_steps),
        grid_spec=pltpu.PrefetchScalarGridSpec(
            num_scalar_prefetch=1,
            grid=(dec_batch, n_steps),
            in_specs=kv_specs + kv_specs + [
                pl.BlockSpec((None, B_KV_HEADS, rows, B_HEAD_DIM), lambda b, s, pt: (b, 0, 0, 0)),
                pl.BlockSpec((None, None, 8, chunk), lambda b, s, pt: (b, s, 0, 0)),
                pl.BlockSpec((None, 8, LANES), lambda b, s, pt: (b, 0, 0)),
                pl.BlockSpec((None, 8, LANES), lambda b, s, pt: (b, 0, 0)),
                pl.BlockSpec((None, LANES, 2 * KV_WIDTH), lambda b, s, pt: (b, 0, 0)),
                pl.BlockSpec(sbias.shape, lambda b, s, pt: (0, 0, 0)),
                pl.BlockSpec(sconst.shape, lambda b, s, pt: (0, 0, 0)),
            ],
            out_specs=pl.BlockSpec((None, B_KV_HEADS, rows, B_HEAD_DIM), lambda b, s, pt: (b, 0, 0, 0)),
            scratch_shapes=[
                pltpu.VMEM((B_KV_HEADS, rows, 1), F32),
                pltpu.VMEM((B_KV_HEADS, rows, 1), F32),
                pltpu.VMEM((B_KV_HEADS, rows, B_HEAD_DIM), F32),
            ],
        ),
        out_shape=jax.ShapeDtypeStruct((dec_batch, B_KV_HEADS, rows, B_HEAD_DIM), BF16),
        compiler_params=_cparams(("arbitrary", "arbitrary")),
        name="dsa_sample_attend",
    )(page_table, *([ck] * PAGES_PER_STEP), *([cv] * PAGES_PER_STEP), q, sc, scn, thr, kv_new, sbias, sconst)
    o = o.reshape(dec_batch, B_KV_HEADS, B_GQA, SUBLANES, B_HEAD_DIM)[:, :, :, :dec_seq].transpose(0, 3, 1, 2, 4)
    return o.reshape(dec_batch * dec_seq, B_WIDTH)


def _mixout_kernel(u_ref, vn_ref, b_ref, ws_ref, ab_ref, x_ref, w_ref, o_ref, mix_scr, *, tm):
    j = pl.program_id(1)

    @pl.when(j == 0)
    def _():
        row = lax.broadcasted_iota(I32, (A_CHUNK, A_CHUNK), 0)
        col = lax.broadcasted_iota(I32, (A_CHUNK, A_CHUNK), 1)
        for g in range(A_GROUPS):
            wsg = jnp.where(row >= col, ws_ref[g], 0.0).astype(BF16)
            bg = ab_ref[:, g:g + 1]
            cs = slice(g * A_CH, (g + 1) * A_CH)
            for r in range(tm // A_CHUNK):
                rs = slice(r * A_CHUNK, (r + 1) * A_CHUNK)
                sm = jnp.dot(wsg, vn_ref[rs, cs], preferred_element_type=F32) + bg
                mix_scr[rs, cs] = (u_ref[rs, cs].astype(F32) * sm).astype(BF16)
        mix_scr[:, A_WIDTH:] = b_ref[...]

    o_ref[...] = x_ref[...] + jnp.dot(mix_scr[...], w_ref[...].astype(BF16), preferred_element_type=F32)


def _mixout(p, b_out, ws_eff, ab_eff, x2d, w_out_l, *, tm, tn=512):
    n = x2d.shape[0]
    return pl.pallas_call(
        functools.partial(_mixout_kernel, tm=tm),
        grid=(n // tm, D_MODEL // tn),
        in_specs=[
            pl.BlockSpec((tm, A_WIDTH), lambda i, j: (i, COL_U // A_WIDTH)),
            pl.BlockSpec((tm, A_WIDTH), lambda i, j: (i, COL_V // A_WIDTH)),
            pl.BlockSpec((tm, B_WIDTH), lambda i, j: (i, 0)),
            pl.BlockSpec((A_GROUPS, A_CHUNK, A_CHUNK), lambda i, j: (0, 0, 0)),
            pl.BlockSpec((A_CHUNK, A_GROUPS), lambda i, j: (0, 0)),
            pl.BlockSpec((tm, tn), lambda i, j: (i, j)),
            pl.BlockSpec((A_WIDTH + B_WIDTH, tn), lambda i, j: (0, j)),
        ],
        out_specs=pl.BlockSpec((tm, tn), lambda i, j: (i, j)),
        out_shape=jax.ShapeDtypeStruct((n, D_MODEL), F32),
        scratch_shapes=[pltpu.VMEM((tm, A_WIDTH + B_WIDTH), BF16)],
        compiler_params=_cparams(("arbitrary", "arbitrary")),
        name="mixout",
    )(p, p, b_out, ws_eff, ab_eff, x2d, w_out_l)


def _memkv_kernel(x_ref, wk_ref, wv_ref, kn_ref, mk_ref, mv_ref):
    x = x_ref[...].astype(BF16)
    tm = x.shape[0]
    mk = _head_norm(jnp.dot(x, wk_ref[...].astype(BF16), preferred_element_type=F32), kn_ref[...])
    mv = jnp.dot(x, wv_ref[...].astype(BF16), preferred_element_type=F32)
    for hd in range(MEM_HEADS):
        cs = slice(hd * MEM_HEAD_DIM, (hd + 1) * MEM_HEAD_DIM)
        mk_ref[pl.ds(hd, tm, stride=MEM_HEADS), :] = mk[:, cs]
        mv_ref[pl.ds(hd, tm, stride=MEM_HEADS), :] = mv[:, cs]


def _memkv(mem2d, wk, wv, knorm, *, tm=256):
    n = mem2d.shape[0]
    kn = jnp.tile(knorm, MEM_HEADS).reshape(1, MEM_WIDTH)
    return pl.pallas_call(
        _memkv_kernel,
        grid=(n // tm,),
        in_specs=[
            pl.BlockSpec((tm, D_MODEL), lambda i: (i, 0)),
            pl.BlockSpec((D_MODEL, MEM_WIDTH), lambda i: (0, 0)),
            pl.BlockSpec((D_MODEL, MEM_WIDTH), lambda i: (0, 0)),
            pl.BlockSpec((1, MEM_WIDTH), lambda i: (0, 0)),
        ],
        out_specs=[pl.BlockSpec((tm * MEM_HEADS, MEM_HEAD_DIM), lambda i: (i, 0))] * 2,
        out_shape=[jax.ShapeDtypeStruct((n * MEM_HEADS, MEM_HEAD_DIM), F32)] * 2,
        compiler_params=_cparams(("arbitrary",)),
        name="memkv",
    )(mem2d, wk, wv, kn)


def _memattn_kernel(h_ref, g_ref, wq_ref, qn_ref, mk_ref, mv_ref, wo_ref, o_ref, *, rows_per_mem, mem_len):
    h = h_ref[...]
    n = _row_norm(h, g_ref[...]).astype(BF16)
    q = _head_norm(jnp.dot(n, wq_ref[...].astype(BF16), preferred_element_type=F32), qn_ref[...]).astype(BF16)
    tm, m = h.shape[0], mk_ref.shape[0] // MEM_HEADS
    if rows_per_mem is not None:
        row = lax.broadcasted_iota(I32, (tm, m), 0)
        col = lax.broadcasted_iota(I32, (tm, m), 1)
        mask = jnp.where(row // rows_per_mem == col // mem_len, 0.0, -jnp.inf)
    outs = []
    for hd in range(MEM_HEADS):
        cs = slice(hd * MEM_HEAD_DIM, (hd + 1) * MEM_HEAD_DIM)
        mk = mk_ref[pl.ds(hd, m, stride=MEM_HEADS), :].astype(BF16)
        mv = mv_ref[pl.ds(hd, m, stride=MEM_HEADS), :].astype(BF16)
        lg = _dot_nt(q[:, cs], mk) * (MEM_HEAD_DIM ** -0.5)
        if rows_per_mem is not None:
            lg = lg + mask
        e = jnp.exp(lg - jnp.max(lg, axis=-1, keepdims=True))
        l = jnp.sum(e, axis=-1, keepdims=True)
        outs.append(jnp.dot(e.astype(BF16), mv, preferred_element_type=F32) / l)
    o = jnp.concatenate(outs, axis=-1).astype(BF16)
    o_ref[...] = h + jnp.dot(o, wo_ref[...].astype(BF16), preferred_element_type=F32)


def _memattn(h2d, gain, wq, qnorm, mk, mv, wo, *, tm, mem_rows, tiles_per_mem, rows_per_mem, mem_len):
    n = h2d.shape[0]
    qn = jnp.tile(qnorm, MEM_HEADS).reshape(1, MEM_WIDTH)
    return pl.pallas_call(
        functools.partial(_memattn_kernel, rows_per_mem=rows_per_mem, mem_len=mem_len),
        grid=(n // tm,),
        in_specs=[
            pl.BlockSpec((tm, D_MODEL), lambda i: (i, 0)),
            pl.BlockSpec((1, D_MODEL), lambda i: (0, 0)),
            pl.BlockSpec((D_MODEL, MEM_WIDTH), lambda i: (0, 0)),
            pl.BlockSpec((1, MEM_WIDTH), lambda i: (0, 0)),
            pl.BlockSpec((mem_rows * MEM_HEADS, MEM_HEAD_DIM), lambda i: (i // tiles_per_mem, 0)),
            pl.BlockSpec((mem_rows * MEM_HEADS, MEM_HEAD_DIM), lambda i: (i // tiles_per_mem, 0)),
            pl.BlockSpec((MEM_WIDTH, D_MODEL), lambda i: (0, 0)),
        ],
        out_specs=pl.BlockSpec((tm, D_MODEL), lambda i: (i, 0)),
        out_shape=jax.ShapeDtypeStruct((n, D_MODEL), F32),
        compiler_params=_cparams(("arbitrary",)),
        name="memattn",
    )(h2d, gain.reshape(1, D_MODEL), wq, qn, mk, mv, wo)


def _router_kernel(h_ref, g_ref, w_ref, b_ref, n_ref, c_ref):
    n = _row_norm(h_ref[...], g_ref[...]).astype(BF16)
    n_ref[...] = n
    lg = jnp.dot(n, w_ref[...].astype(BF16), preferred_element_type=F32) + b_ref[...]
    gl, el = lg[:, :LANES], lg[:, LANES:]
    lane = lax.broadcasted_iota(I32, gl.shape, 1)
    lanef = lane.astype(F32)

    def first_max_lane(v, vmax):
        return jnp.min(jnp.where(v == vmax, lanef, float(LANES)), axis=-1, keepdims=True).astype(I32)

    gl = jnp.where(lane < N_GROUPS, gl, -jnp.inf)
    ge = jnp.exp(gl - jnp.max(gl, axis=-1, keepdims=True))
    gp = ge / jnp.sum(ge, axis=-1, keepdims=True)
    g_gate = jnp.max(gp, axis=-1, keepdims=True)
    g_sel = first_max_lane(gp, g_gate)
    in_grp = (lane // EXPERTS_PER_GROUP == g_sel) & (lane < N_EXPERTS)
    el = jnp.where(in_grp, el, -jnp.inf)
    ee = jnp.exp(el - jnp.max(el, axis=-1, keepdims=True))
    ep = jnp.where(in_grp, ee / jnp.sum(ee, axis=-1, keepdims=True), -jnp.inf)
    w1 = jnp.max(ep, axis=-1, keepdims=True)
    i1 = first_max_lane(ep, w1)
    ep2 = jnp.where(lane == i1, -jnp.inf, ep)
    w2 = jnp.max(ep2, axis=-1, keepdims=True)
    i2 = first_max_lane(ep2, w2)
    tot = w1 + w2
    c_ref[...] = jnp.where(lane == i1, w1 / tot * g_gate, jnp.where(lane == i2, w2 / tot * g_gate, 0.0))


def _router(h2d, gain, w_cat, b_cat, *, tm):
    n = h2d.shape[0]
    return pl.pallas_call(
        _router_kernel,
        grid=(n // tm,),
        in_specs=[
            pl.BlockSpec((tm, D_MODEL), lambda i: (i, 0)),
            pl.BlockSpec((1, D_MODEL), lambda i: (0, 0)),
            pl.BlockSpec((D_MODEL, 2 * LANES), lambda i: (0, 0)),
            pl.BlockSpec((1, 2 * LANES), lambda i: (0, 0)),
        ],
        out_specs=[pl.BlockSpec((tm, D_MODEL), lambda i: (i, 0)), pl.BlockSpec((tm, LANES), lambda i: (i, 0))],
        out_shape=[jax.ShapeDtypeStruct((n, D_MODEL), BF16), jax.ShapeDtypeStruct((n, LANES), F32)],
        compiler_params=_cparams(("arbitrary",)),
        name="router",
    )(h2d, gain.reshape(1, D_MODEL), w_cat, b_cat)


def _moe_kernel(n_ref, c_ref, h_ref, wg_ref, wu_ref, wd_ref, o_ref):
    e = pl.program_id(1)

    @pl.when(e == 0)
    def _():
        o_ref[...] = h_ref[...]

    x = n_ref[...]
    a = jnp.dot(x, wg_ref[...].astype(BF16), preferred_element_type=F32)
    u = jnp.dot(x, wu_ref[...].astype(BF16), preferred_element_type=F32)
    lane = lax.broadcasted_iota(I32, c_ref.shape, 1)
    c = jnp.sum(jnp.where(lane == e, c_ref[...], 0.0), axis=-1, keepdims=True)
    hid = (a * jax.nn.sigmoid(a) * u * c).astype(BF16)
    o_ref[...] += jnp.dot(hid, wd_ref[...].astype(BF16), preferred_element_type=F32)


def _moe(n3, comb, h2d, w_gate_l, w_up_l, w_down_l, *, tm):
    n = h2d.shape[0]
    return pl.pallas_call(
        _moe_kernel,
        grid=(n // tm, N_EXPERTS),
        in_specs=[
            pl.BlockSpec((tm, D_MODEL), lambda i, e: (i, 0)),
            pl.BlockSpec((tm, LANES), lambda i, e: (i, 0)),
            pl.BlockSpec((tm, D_MODEL), lambda i, e: (i, 0)),
            pl.BlockSpec((None, D_MODEL, EXPERT_FF), lambda i, e: (e, 0, 0)),
            pl.BlockSpec((None, D_MODEL, EXPERT_FF), lambda i, e: (e, 0, 0)),
            pl.BlockSpec((None, EXPERT_FF, D_MODEL), lambda i, e: (e, 0, 0)),
        ],
        out_specs=pl.BlockSpec((tm, D_MODEL), lambda i, e: (i, 0)),
        out_shape=jax.ShapeDtypeStruct((n, D_MODEL), F32),
        compiler_params=_cparams(("arbitrary", "arbitrary")),
        name="moe",
    )(n3, comb, h2d, w_gate_l, w_up_l, w_down_l)


def _post_mixer(p, b_out, ws_eff, ab_eff, x2d, mk, mv, lw, *, tm_mix, tm_mem, mem_rows, tiles_per_mem,
                rows_per_mem, mem_len, tm_route, tm_moe):
    h1 = _mixout(p, b_out, ws_eff, ab_eff, x2d, lw["w_out"], tm=tm_mix)
    h2 = _memattn(h1, lw["norm_mem"], lw["mem_wq"], lw["mem_qnorm"], mk, mv, lw["mem_wo"], tm=tm_mem,
                  mem_rows=mem_rows, tiles_per_mem=tiles_per_mem, rows_per_mem=rows_per_mem, mem_len=mem_len)
    n3, comb = _router(h2, lw["norm_ffn"], lw["w_cat"], lw["b_cat"], tm=tm_route)
    return _moe(n3, comb, h2, lw["w_gate"], lw["w_up"], lw["w_down"], tm=tm_moe)


def kernel(x_prompt, x_sample, cache_k, cache_v, cache_idx_k, cache_mem_k, cache_mem_v, page_table,
           mem_prompt, norm_mix, w_in, a_vnorm, a_ws, a_b, b_qnorm, b_knorm, rel_bias, w_out,
           norm_mem, mem_wq, mem_wk, mem_wv, mem_qnorm, mem_knorm, mem_wo, norm_ffn,
           w_group, b_group, w_router, b_router, w_gate, w_up, w_down):
    bp, t, d = x_prompt.shape
    bd, s, _ = x_sample.shape
    depth = w_in.shape[0]
    mem_len = mem_prompt.shape[1]
    assert d == D_MODEL and t % KEY_CHUNK == 0 and s <= 8 and (bd * s) % LANES == 0

    bias_pairs, sbias, sconst = _bias_tables(rel_bias)
    hp = x_prompt.reshape(bp * t, d)
    hs = x_sample.reshape(bd * s, d)
    outs = [[] for _ in range(9)]
    for l in range(depth):
        w_cat = jnp.zeros((d, 2 * LANES), F32)
        w_cat = w_cat.at[:, :N_GROUPS].set(w_group[l]).at[:, LANES:LANES + N_EXPERTS].set(w_router[l])
        b_cat = jnp.zeros((1, 2 * LANES), F32)
        b_cat = b_cat.at[0, :N_GROUPS].set(b_group[l]).at[0, LANES:LANES + N_EXPERTS].set(b_router[l])
        lw = dict(w_out=w_out[l], norm_mem=norm_mem[l], mem_wq=mem_wq[l], mem_qnorm=mem_qnorm[l],
                  mem_wo=mem_wo[l], norm_ffn=norm_ffn[l], w_cat=w_cat, b_cat=b_cat,
                  w_gate=w_gate[l], w_up=w_up[l], w_down=w_down[l])

        p, k, v, tail = _proj(hp, norm_mix[l], w_in[l], a_vnorm[l], b_qnorm[l], b_knorm[l],
                              tm=1024, want_vn=False)
        b_out = _dsa_prompt(p, tail, bias_pairs, batch=bp, seq=t)
        mk, mv = _memkv(mem_prompt.reshape(bp * mem_len, d), mem_wk[l], mem_wv[l], mem_knorm[l])
        hp = _post_mixer(p, b_out, a_ws[l], a_b[l].T, hp, mk, mv, lw, tm_mix=1024, tm_mem=512,
                         mem_rows=mem_len, tiles_per_mem=t // 512, rows_per_mem=None, mem_len=mem_len,
                         tm_route=512, tm_moe=512)
        outs[0].append(k.reshape(bp, t, B_KV_HEADS, B_HEAD_DIM))
        outs[1].append(v.reshape(bp, t, B_KV_HEADS, B_HEAD_DIM))
        outs[2].append(tail[:, :IDX_DIM].reshape(bp, t, IDX_DIM))
        outs[3].append(mk.reshape(bp, mem_len, MEM_HEADS, MEM_HEAD_DIM))
        outs[4].append(mv.reshape(bp, mem_len, MEM_HEADS, MEM_HEAD_DIM))

        ns = bd * s
        p_s, k_s, v_s, tail_s, vn_s = _proj(hs, norm_mix[l], w_in[l], a_vnorm[l], b_qnorm[l], b_knorm[l],
                                            tm=ns, want_vn=True)
        b_out_s = _dsa_sample(p_s, tail_s, k_s, v_s, cache_k[l], cache_v[l], cache_idx_k[l], page_table,
                              sbias, sconst, dec_batch=bd, dec_seq=s)
        ws_s = jnp.einsum("ab,gts->gatbs", jnp.eye(ns // s, dtype=F32), a_ws[l][:, :s, :s])
        ws_s = ws_s.reshape(A_GROUPS, ns, ns)
        ab_s = jnp.tile(a_b[l][:, :s].T, (ns // s, 1))
        mem_tile = 32
        hs = _post_mixer(p_s, b_out_s, ws_s, ab_s, hs,
                         cache_mem_k[l].reshape(bd * mem_len * MEM_HEADS, MEM_HEAD_DIM),
                         cache_mem_v[l].reshape(bd * mem_len * MEM_HEADS, MEM_HEAD_DIM), lw,
                         tm_mix=ns, tm_mem=mem_tile, mem_rows=mem_tile // s * mem_len, tiles_per_mem=1,
                         rows_per_mem=s, mem_len=mem_len, tm_route=ns, tm_moe=ns)
        outs[5].append(k_s.reshape(bd, s, B_KV_HEADS, B_HEAD_DIM))
        outs[6].append(v_s.reshape(bd, s, B_KV_HEADS, B_HEAD_DIM))
        outs[7].append(tail_s[:, :IDX_DIM].reshape(bd, s, IDX_DIM))
        outs[8].append(vn_s.reshape(bd, s, A_GROUPS, A_CH))
    return (hp.reshape(bp, t, d), hs.reshape(bd, s, d)) + tuple(jnp.stack(o) for o in outs)
```

```python
import functools
import math

import jax
import jax.numpy as jnp
from jax import lax
from jax.experimental import pallas as pl
from jax.experimental.pallas import tpu as pltpu

BF16 = jnp.bfloat16
F32 = jnp.float32
I32 = jnp.int32

D_MODEL = 2048
A_GROUPS = 8
A_CH = 128
A_WIDTH = A_GROUPS * A_CH
A_CHUNK = 128
B_HEADS = 8
B_HEAD_DIM = 128
B_KV_HEADS = 2
B_GQA = B_HEADS // B_KV_HEADS
B_WIDTH = B_HEADS * B_HEAD_DIM
KV_WIDTH = B_KV_HEADS * B_HEAD_DIM
IDX_HEADS = 16
IDX_DIM = 64
TOPK_MAX = 256
Q_BLOCK = 128
REL_BUCKETS = 32
REL_MAX_DIST = 128
MEM_HEADS = 4
MEM_HEAD_DIM = 128
MEM_WIDTH = MEM_HEADS * MEM_HEAD_DIM
N_GROUPS = 4
EXPERTS_PER_GROUP = 4
N_EXPERTS = N_GROUPS * EXPERTS_PER_GROUP
EXPERT_FF = 512
PAGE_SIZE = 128
EPS = 1e-6

COL_U = 0
COL_V = COL_U + A_WIDTH
COL_Q = COL_V + A_WIDTH
COL_K = COL_Q + B_WIDTH
COL_VV = COL_K + KV_WIDTH
COL_QI = COL_VV + KV_WIDTH
COL_MAIN = COL_QI + IDX_HEADS * IDX_DIM
TAIL_COLS = IDX_DIM + IDX_HEADS
LANES = 128
SUBLANES = 8
PROJ_TN = 512
KEY_CHUNK = 256
NEG_INIT = -1e30
VMEM_LIMIT = 56 * 1024 * 1024
MOE_VMEM_LIMIT = 60 * 1024 * 1024


def _cparams(sem):
    return pltpu.CompilerParams(dimension_semantics=sem, vmem_limit_bytes=VMEM_LIMIT)


def _gelu(x):
    return 0.5 * x * (1.0 + lax.erf(x * (2.0 ** -0.5)))


def _head_norm(a, gain, width=LANES):
    outs = []
    for c in range(a.shape[1] // width):
        blk = a[:, c * width:(c + 1) * width]
        ms = jnp.mean(blk * blk, axis=-1, keepdims=True)
        outs.append(blk * lax.rsqrt(ms + EPS) * gain[:, c * width:(c + 1) * width])
    return outs[0] if len(outs) == 1 else jnp.concatenate(outs, axis=-1)


def _row_norm(x, gain):
    ms = jnp.mean(x * x, axis=-1, keepdims=True)
    return x * lax.rsqrt(ms + EPS) * gain


def _dot_nt(a, b):
    return lax.dot_general(a, b, (((1,), (1,)), ((), ())), preferred_element_type=F32)


def _proj_kernel(x_ref, g_ref, w_ref, wt_ref, avn_ref, qn_ref, kn_ref, *rest, want_vn):
    if want_vn:
        p_ref, k_ref, v_ref, tail_ref, vn_ref, xn_scr = rest
    else:
        p_ref, k_ref, v_ref, tail_ref, xn_scr = rest
        vn_ref = None
    j = pl.program_id(1)
    ju = COL_V // PROJ_TN
    jv = COL_Q // PROJ_TN
    jq = COL_K // PROJ_TN
    jkv = COL_QI // PROJ_TN

    @pl.when(j == 0)
    def _():
        xn = _row_norm(x_ref[...], g_ref[...]).astype(BF16)
        xn_scr[...] = xn
        wlane = lax.broadcasted_iota(I32, wt_ref.shape, 1)
        wt = jnp.where(wlane < TAIL_COLS, wt_ref[...], 0.0).astype(BF16)
        t = jnp.dot(xn, wt, preferred_element_type=F32)
        lane = lax.broadcasted_iota(I32, t.shape, 1)
        tail_ref[...] = jnp.where(lane >= IDX_DIM, t * (IDX_HEADS ** -0.5), t)

    acc = jnp.dot(xn_scr[...], w_ref[...].astype(BF16), preferred_element_type=F32)

    @pl.when(j < ju)
    def _():
        p_ref[...] = _gelu(acc).astype(BF16)

    @pl.when((j >= ju) & (j < jv))
    def _():
        vn = _head_norm(_gelu(acc), avn_ref[...])
        p_ref[...] = vn.astype(BF16)
        if vn_ref is not None:
            vn_ref[...] = vn

    @pl.when((j >= jv) & (j < jq))
    def _():
        p_ref[...] = _head_norm(acc, qn_ref[...]).astype(BF16)

    @pl.when(j == jq)
    def _():
        k = _head_norm(acc[:, :KV_WIDTH], kn_ref[...])
        v = acc[:, KV_WIDTH:]
        k_ref[...] = k
        v_ref[...] = v
        p_ref[...] = jnp.concatenate([k, v], axis=-1).astype(BF16)

    @pl.when(j >= jkv)
    def _():
        p_ref[...] = acc.astype(BF16)


def _proj(x2d, gain, w_in_l, a_vnorm_l, b_qnorm_l, b_knorm_l, *, tm, want_vn):
    n = x2d.shape[0]
    nj = COL_MAIN // PROJ_TN
    avn = a_vnorm_l.reshape(1, A_WIDTH)
    qn = jnp.tile(b_qnorm_l, PROJ_TN // B_HEAD_DIM).reshape(1, PROJ_TN)
    kn = jnp.tile(b_knorm_l, B_KV_HEADS).reshape(1, KV_WIDTH)
    ju = COL_V // PROJ_TN
    nv = A_WIDTH // PROJ_TN

    def vmap_(i, j):
        return (0, jnp.clip(j - ju, 0, nv - 1))

    in_specs = [
        pl.BlockSpec((tm, D_MODEL), lambda i, j: (i, 0)),
        pl.BlockSpec((1, D_MODEL), lambda i, j: (0, 0)),
        pl.BlockSpec((D_MODEL, PROJ_TN), lambda i, j: (0, j)),
        pl.BlockSpec((D_MODEL, LANES), lambda i, j: (0, COL_MAIN // LANES)),
        pl.BlockSpec((1, PROJ_TN), vmap_),
        pl.BlockSpec((1, PROJ_TN), lambda i, j: (0, 0)),
        pl.BlockSpec((1, KV_WIDTH), lambda i, j: (0, 0)),
    ]
    out_shape = [
        jax.ShapeDtypeStruct((n, COL_MAIN), BF16),
        jax.ShapeDtypeStruct((n, KV_WIDTH), F32),
        jax.ShapeDtypeStruct((n, KV_WIDTH), F32),
        jax.ShapeDtypeStruct((n, LANES), F32),
    ]
    out_specs = [
        pl.BlockSpec((tm, PROJ_TN), lambda i, j: (i, j)),
        pl.BlockSpec((tm, KV_WIDTH), lambda i, j: (i, 0)),
        pl.BlockSpec((tm, KV_WIDTH), lambda i, j: (i, 0)),
        pl.BlockSpec((tm, LANES), lambda i, j: (i, 0)),
    ]
    if want_vn:
        out_shape.append(jax.ShapeDtypeStruct((n, A_WIDTH), F32))
        out_specs.append(pl.BlockSpec((tm, PROJ_TN), lambda i, j: (i, jnp.clip(j - ju, 0, nv - 1))))
    return pl.pallas_call(
        functools.partial(_proj_kernel, want_vn=want_vn),
        grid=(n // tm, nj),
        in_specs=in_specs,
        out_specs=out_specs,
        out_shape=out_shape,
        scratch_shapes=[pltpu.VMEM((tm, D_MODEL), BF16)],
        compiler_params=_cparams(("arbitrary", "arbitrary")),
        name="proj",
    )(x2d, gain.reshape(1, D_MODEL), w_in_l, w_in_l, avn, qn, kn)


def _t5_bucket(d):
    max_exact = REL_BUCKETS // 2
    d = jnp.maximum(d, 0)
    ratio = jnp.log(jnp.maximum(d, 1).astype(F32) / max_exact) / math.log(REL_MAX_DIST / max_exact)
    large = jnp.minimum(max_exact + (ratio * (REL_BUCKETS - max_exact)).astype(I32), REL_BUCKETS - 1)
    return jnp.where(d < max_exact, d, large)


def _bias_lookup(bucket, rb_ref, h):
    acc = jnp.zeros(bucket.shape, F32)
    for r in range(REL_BUCKETS):
        acc = jnp.where(bucket == r, rb_ref[r, h], acc)
    return acc


def _bias_kernel(rb_ref, pair_ref, samp_ref, sconst_ref):
    q = lax.broadcasted_iota(I32, (Q_BLOCK, Q_BLOCK), 0)
    kc = lax.broadcasted_iota(I32, (Q_BLOCK, Q_BLOCK), 1)
    nvar = pair_ref.shape[0]
    rels = range(-(nvar - 1), 2)
    buckets = {r: _t5_bucket(q - kc - r * Q_BLOCK) for r in rels}
    for h in range(B_HEADS):
        g, hh = divmod(h, B_GQA)
        tiles = {r: _bias_lookup(buckets[r], rb_ref, h) for r in rels}
        for var in range(nvar):
            r0 = var - (nvar - 1)
            for t in range(2):
                pair_ref[var, g, hh * Q_BLOCK:(hh + 1) * Q_BLOCK, t * Q_BLOCK:(t + 1) * Q_BLOCK] = tiles[r0 + t]
    rows = B_GQA * SUBLANES
    row = lax.broadcasted_iota(I32, (rows, 2 * PAGE_SIZE), 0)
    col = lax.broadcasted_iota(I32, (rows, 2 * PAGE_SIZE), 1)
    t = row % SUBLANES
    d = jnp.where(col < PAGE_SIZE, t + PAGE_SIZE - col, t - (col - PAGE_SIZE))
    bucket = _t5_bucket(d)
    far = jnp.full((rows, LANES), REL_BUCKETS - 1, I32)
    rowc = lax.broadcasted_iota(I32, (rows, LANES), 0)
    for g in range(B_KV_HEADS):
        acc = jnp.zeros((rows, 2 * PAGE_SIZE), F32)
        accc = jnp.zeros((rows, LANES), F32)
        for hh in range(B_GQA):
            h = g * B_GQA + hh
            acc = jnp.where(row // SUBLANES == hh, _bias_lookup(bucket, rb_ref, h), acc)
            accc = jnp.where(rowc // SUBLANES == hh, _bias_lookup(far, rb_ref, h), accc)
        samp_ref[g] = acc
        sconst_ref[g] = accc


N_PAIR_VARIANTS = 4


def _bias_tables(rel_bias):
    rows = B_GQA * SUBLANES
    return pl.pallas_call(
        _bias_kernel,
        in_specs=[pl.BlockSpec(memory_space=pltpu.SMEM)],
        out_shape=[
            jax.ShapeDtypeStruct((N_PAIR_VARIANTS, B_KV_HEADS, B_GQA * Q_BLOCK, KEY_CHUNK), F32),
            jax.ShapeDtypeStruct((B_KV_HEADS, rows, 2 * PAGE_SIZE), F32),
            jax.ShapeDtypeStruct((B_KV_HEADS, rows, LANES), F32),
        ],
        name="bias_tables",
    )(rel_bias)


def _key_to_f32(key):
    bits = jnp.where(key < 0, key ^ I32(-2 ** 31), ~key)
    return lax.bitcast_convert_type(bits, F32)


def _kth_largest(count_ge, shape, k):
    def body(it, prefix):
        bit = 31 - it
        cand = prefix | lax.shift_left(I32(1), bit)
        cnt = count_ge(_key_to_f32(cand))
        return jnp.where(cnt >= k, cand, prefix)

    prefix = lax.fori_loop(0, 32, body, jnp.zeros(shape, I32))
    thr = _key_to_f32(prefix)
    return jnp.where((prefix & I32(-2 ** 23)) == 0, -jnp.inf, thr)


def _dsa_prompt_kernel(q_ref, kv_ref, qia_ref, qib_ref, tailk_ref, tailq_ref, bias_ref, o_ref,
                       kil_scr, kir_scr, sc_scr, *, topk):
    i = pl.program_id(1)

    @pl.when(i == 0)
    def _():
        ki = tailk_ref[:, :IDX_DIM].astype(BF16)
        z = jnp.zeros_like(ki)
        kil_scr[...] = jnp.concatenate([ki, z], axis=-1)
        kir_scr[...] = jnp.concatenate([z, ki], axis=-1)

    wi = tailq_ref[:, IDX_DIM:IDX_DIM + IDX_HEADS]
    nk = (i + 2) // 2
    qpos = i * Q_BLOCK + lax.broadcasted_iota(I32, (Q_BLOCK, 1), 0)
    kloc = lax.broadcasted_iota(I32, (1, KEY_CHUNK), 1)
    heads_per_ref = qia_ref.shape[1] // IDX_DIM

    def score_chunk(c, carry):
        off = pl.multiple_of(c * KEY_CHUNK, KEY_CHUNK)
        kl = kil_scr[pl.ds(off, KEY_CHUNK), :]
        kr = kir_scr[pl.ds(off, KEY_CHUNK), :]
        acc = jnp.zeros((Q_BLOCK, KEY_CHUNK), F32)
        for hp in range(IDX_HEADS // 2):
            ref = qia_ref if 2 * hp < heads_per_ref else qib_ref
            base = (2 * hp) % heads_per_ref * IDX_DIM
            qp = ref[:, base:base + 2 * IDX_DIM]
            for side, kk in ((0, kl), (1, kr)):
                h = 2 * hp + side
                s = _dot_nt(qp, kk) * (IDX_DIM ** -0.5)
                acc = acc + wi[:, h:h + 1] * jnp.maximum(s, 0.0)
        kpos = c * KEY_CHUNK + kloc
        sc_scr[c] = jnp.where(kpos <= qpos, acc, -jnp.inf)
        return carry

    lax.fori_loop(0, nk, score_chunk, 0)

    def count_ge(t):
        def body(c, cnt):
            m = jnp.where(sc_scr[c] >= t, 1.0, 0.0)
            return cnt + m[:, :LANES] + m[:, LANES:]
        cnt = lax.fori_loop(0, nk, body, jnp.zeros((Q_BLOCK, LANES), F32))
        return jnp.sum(cnt, axis=-1, keepdims=True)

    thr = _kth_largest(count_ge, (Q_BLOCK, 1), topk)

    scale = B_HEAD_DIM ** -0.5
    rows = B_GQA * Q_BLOCK
    for g in range(B_KV_HEADS):
        qg = jnp.concatenate(
            [q_ref[:, (g * B_GQA + hh) * B_HEAD_DIM:(g * B_GQA + hh + 1) * B_HEAD_DIM] for hh in range(B_GQA)],
            axis=0)

        def att_chunk(c, carry, g=g, qg=qg):
            m_old, l_old, acc = carry
            off = pl.multiple_of(c * KEY_CHUNK, KEY_CHUNK)
            kc = kv_ref[pl.ds(off, KEY_CHUNK), g * B_HEAD_DIM:(g + 1) * B_HEAD_DIM]
            vc = kv_ref[pl.ds(off, KEY_CHUNK), KV_WIDTH + g * B_HEAD_DIM:KV_WIDTH + (g + 1) * B_HEAD_DIM]
            var = jnp.clip(2 * c - i, -(N_PAIR_VARIANTS - 1), 0) + (N_PAIR_VARIANTS - 1)
            lg = _dot_nt(qg, kc) * scale + bias_ref[var, g]
            kpos = c * KEY_CHUNK + kloc
            sel = jnp.where((sc_scr[c] >= thr) & (kpos <= qpos), 0.0, -jnp.inf)
            lg = lg + jnp.concatenate([sel] * B_GQA, axis=0)
            m_new = jnp.maximum(m_old, jnp.max(lg, axis=-1, keepdims=True))
            alpha = jnp.exp(m_old - m_new)
            p = jnp.exp(lg - m_new)
            l_new = alpha * l_old + jnp.sum(p, axis=-1, keepdims=True)
            acc = alpha * acc + jnp.dot(p.astype(BF16), vc, preferred_element_type=F32)
            return m_new, l_new, acc

        init = (jnp.full((rows, 1), NEG_INIT, F32), jnp.zeros((rows, 1), F32),
                jnp.zeros((rows, B_HEAD_DIM), F32))
        _, l_fin, acc = lax.fori_loop(0, nk, att_chunk, init)
        o = acc / l_fin
        for hh in range(B_GQA):
            h = g * B_GQA + hh
            o_ref[:, h * B_HEAD_DIM:(h + 1) * B_HEAD_DIM] = o[hh * Q_BLOCK:(hh + 1) * Q_BLOCK].astype(BF16)


def _dsa_prompt(p, tail, bias_pairs, *, batch, seq):
    topk = min(TOPK_MAX, seq // 4)
    nblk = seq // Q_BLOCK
    qi_w = PROJ_TN
    return pl.pallas_call(
        functools.partial(_dsa_prompt_kernel, topk=topk),
        grid=(batch, nblk),
        in_specs=[
            pl.BlockSpec((Q_BLOCK, B_WIDTH), lambda b, i: (b * nblk + i, COL_Q // B_WIDTH)),
            pl.BlockSpec((seq, 2 * KV_WIDTH), lambda b, i: (b, COL_K // (2 * KV_WIDTH))),
            pl.BlockSpec((Q_BLOCK, qi_w), lambda b, i: (b * nblk + i, COL_QI // qi_w)),
            pl.BlockSpec((Q_BLOCK, qi_w), lambda b, i: (b * nblk + i, COL_QI // qi_w + 1)),
            pl.BlockSpec((seq, LANES), lambda b, i: (b, 0)),
            pl.BlockSpec((Q_BLOCK, LANES), lambda b, i: (b * nblk + i, 0)),
            pl.BlockSpec(bias_pairs.shape, lambda b, i: (0, 0, 0, 0)),
        ],
        out_specs=pl.BlockSpec((Q_BLOCK, B_WIDTH), lambda b, i: (b * nblk + i, 0)),
        out_shape=jax.ShapeDtypeStruct((batch * seq, B_WIDTH), BF16),
        scratch_shapes=[
            pltpu.VMEM((seq, 2 * IDX_DIM), BF16),
            pltpu.VMEM((seq, 2 * IDX_DIM), BF16),
            pltpu.VMEM((seq // KEY_CHUNK, Q_BLOCK, KEY_CHUNK), F32),
        ],
        compiler_params=_cparams(("arbitrary", "arbitrary")),
        name="dsa_prompt",
    )(p, p, p, p, tail, tail, bias_pairs)


PAGES_PER_STEP = 16


def _dsa_sample_score_kernel(pt_ref, *refs, topk, dec_seq, n_steps):
    del pt_ref
    pages = refs[:PAGES_PER_STEP]
    qi_ref, wi_ref, kin_ref, sc_ref, scn_ref, thr_ref = refs[PAGES_PER_STEP:]
    s = pl.program_id(1)
    chunk = PAGES_PER_STEP * PAGE_SIZE
    qi = qi_ref[...]
    wi = wi_ref[...]

    def scores(keys_t):
        r = jnp.maximum(jnp.dot(qi, keys_t, preferred_element_type=F32) * (IDX_DIM ** -0.5), 0.0) * wi
        acc = r[0:SUBLANES]
        for h in range(1, IDX_HEADS):
            acc = acc + r[h * SUBLANES:(h + 1) * SUBLANES]
        return acc

    keys_t = jnp.concatenate([pg[...].astype(BF16) for pg in pages], axis=1)
    sc_ref[s] = scores(keys_t)

    @pl.when(s == n_steps - 1)
    def _():
        row = lax.broadcasted_iota(I32, (8, LANES), 0)
        col = lax.broadcasted_iota(I32, (8, LANES), 1)
        scn = jnp.where((col <= row) & (col < dec_seq), scores(kin_ref[...]), -jnp.inf)
        scn_ref[...] = scn

        def count_ge(t):
            cnt = jnp.where(scn >= t, 1.0, 0.0)
            for c in range(n_steps):
                for w in range(chunk // LANES):
                    cnt = cnt + jnp.where(sc_ref[c, :, w * LANES:(w + 1) * LANES] >= t, 1.0, 0.0)
            return jnp.sum(cnt, axis=-1, keepdims=True)

        thr = _kth_largest(count_ge, (8, 1), topk)
        thr_ref[...] = jnp.broadcast_to(thr, (8, LANES))


def _dsa_sample_attend_kernel(pt_ref, *refs, dec_seq, n_steps):
    del pt_ref
    kpages = refs[:PAGES_PER_STEP]
    vpages = refs[PAGES_PER_STEP:2 * PAGES_PER_STEP]
    (q_ref, sc_ref, scn_ref, thr_ref, kvn_ref, sbias_ref, sconst_ref, o_ref,
     m_scr, l_scr, acc_scr) = refs[2 * PAGES_PER_STEP:]
    s = pl.program_id(1)
    chunk = PAGES_PER_STEP * PAGE_SIZE
    rows = B_GQA * SUBLANES
    scale = B_HEAD_DIM ** -0.5

    @pl.when(s == 0)
    def _():
        m_scr[...] = jnp.full(m_scr.shape, NEG_INIT, F32)
        l_scr[...] = jnp.zeros(l_scr.shape, F32)
        acc_scr[...] = jnp.zeros(acc_scr.shape, F32)

    thr = thr_ref[:, 0:1]

    def head_rows(pgs, g):
        return jnp.concatenate(
            [pg[pl.ds(g, PAGE_SIZE, stride=B_KV_HEADS), :].astype(BF16) for pg in pgs], axis=0)

    sel = jnp.where(sc_ref[...] >= thr, 0.0, -jnp.inf)
    sel = jnp.concatenate([sel] * B_GQA, axis=0)
    is_last = s == n_steps - 1
    col = lax.broadcasted_iota(I32, (rows, chunk), 1)

    def update(g, lg, v):
        m_old = m_scr[g]
        l_old = l_scr[g]
        m_new = jnp.maximum(m_old, jnp.max(lg, axis=-1, keepdims=True))
        alpha = jnp.exp(m_old - m_new)
        p = jnp.exp(lg - m_new)
        l_scr[g] = alpha * l_old + jnp.sum(p, axis=-1, keepdims=True)
        acc_scr[g] = alpha * acc_scr[g] + jnp.dot(p.astype(BF16), v, preferred_element_type=F32)
        m_scr[g] = m_new

    for g in range(B_KV_HEADS):
        qg = q_ref[g]
        lg = _dot_nt(qg, head_rows(kpages, g)) * scale
        near = jnp.concatenate(
            [jnp.zeros((rows, chunk - PAGE_SIZE), F32), sbias_ref[g, :, :PAGE_SIZE]], axis=-1)
        bias = jnp.where(is_last & (col >= chunk - PAGE_SIZE), near, sconst_ref[g, :, 0:1])
        update(g, lg + bias + sel, head_rows(vpages, g))

    @pl.when(is_last)
    def _():
        seln = jnp.where(scn_ref[...] >= thr, 0.0, -jnp.inf)
        seln = jnp.concatenate([seln] * B_GQA, axis=0)
        for g in range(B_KV_HEADS):
            kn = kvn_ref[:, g * B_HEAD_DIM:(g + 1) * B_HEAD_DIM]
            vn = kvn_ref[:, KV_WIDTH + g * B_HEAD_DIM:KV_WIDTH + (g + 1) * B_HEAD_DIM]
            lg = _dot_nt(q_ref[g], kn) * scale + sbias_ref[g, :, PAGE_SIZE:] + seln
            update(g, lg, vn)
            o_ref[g] = (acc_scr[g] / l_scr[g]).astype(BF16)


def _dsa_sample(p_s, tail_s, k_s, v_s, cache_k_l, cache_v_l, cache_idx_l, page_table, sbias, sconst,
                *, dec_batch, dec_seq):
    n_pages = page_table.shape[1]
    past = n_pages * PAGE_SIZE
    topk = min(TOPK_MAX, (past + dec_seq) // 4)
    n_steps = n_pages // PAGES_PER_STEP
    chunk = PAGES_PER_STEP * PAGE_SIZE
    rows = B_GQA * SUBLANES
    n_pool = cache_k_l.shape[0]

    qi = p_s[:, COL_QI:COL_MAIN].reshape(dec_batch, dec_seq, IDX_HEADS, IDX_DIM).transpose(0, 2, 1, 3)
    qi = jnp.pad(qi, ((0, 0), (0, 0), (0, SUBLANES - dec_seq), (0, 0))).reshape(dec_batch, IDX_HEADS * SUBLANES, IDX_DIM)
    wi = tail_s[:, IDX_DIM:IDX_DIM + IDX_HEADS].reshape(dec_batch, dec_seq, IDX_HEADS).transpose(0, 2, 1)
    wi = jnp.pad(wi, ((0, 0), (0, 0), (0, SUBLANES - dec_seq))).reshape(dec_batch, IDX_HEADS * SUBLANES, 1)
    ki_new = tail_s[:, :IDX_DIM].astype(BF16).reshape(dec_batch, dec_seq, IDX_DIM)
    ki_new = jnp.pad(ki_new, ((0, 0), (0, LANES - dec_seq), (0, 0))).transpose(0, 2, 1)
    q = p_s[:, COL_Q:COL_K].reshape(dec_batch, dec_seq, B_KV_HEADS, B_GQA, B_HEAD_DIM)
    q = jnp.pad(q.transpose(0, 2, 3, 1, 4), ((0, 0), (0, 0), (0, 0), (0, SUBLANES - dec_seq), (0, 0)))
    q = q.reshape(dec_batch, B_KV_HEADS, rows, B_HEAD_DIM)
    kv_new = jnp.concatenate([k_s, v_s], axis=-1).astype(BF16).reshape(dec_batch, dec_seq, 2 * KV_WIDTH)
    kv_new = jnp.pad(kv_new, ((0, 0), (0, LANES - dec_seq), (0, 0)))

    def page_map(r):
        return lambda b, s, pt: (pt[b, s * PAGES_PER_STEP + r], 0, 0)

    cache_idx_t = jnp.swapaxes(cache_idx_l, 1, 2)
    idx_specs = [pl.BlockSpec((None, IDX_DIM, PAGE_SIZE), page_map(r)) for r in range(PAGES_PER_STEP)]
    sc, scn, thr = pl.pallas_call(
        functools.partial(_dsa_sample_score_kernel, topk=topk, dec_seq=dec_seq, n_steps=n_steps),
        grid_spec=pltpu.PrefetchScalarGridSpec(
            num_scalar_prefetch=1,
            grid=(dec_batch, n_steps),
            in_specs=idx_specs + [
                pl.BlockSpec((None, IDX_HEADS * SUBLANES, IDX_DIM), lambda b, s, pt: (b, 0, 0)),
                pl.BlockSpec((None, IDX_HEADS * SUBLANES, 1), lambda b, s, pt: (b, 0, 0)),
                pl.BlockSpec((None, IDX_DIM, LANES), lambda b, s, pt: (b, 0, 0)),
            ],
            out_specs=[
                pl.BlockSpec((None, n_steps, 8, chunk), lambda b, s, pt: (b, 0, 0, 0)),
                pl.BlockSpec((None, 8, LANES), lambda b, s, pt: (b, 0, 0)),
                pl.BlockSpec((None, 8, LANES), lambda b, s, pt: (b, 0, 0)),
            ],
        ),
        out_shape=[
            jax.ShapeDtypeStruct((dec_batch, n_steps, 8, chunk), F32),
            jax.ShapeDtypeStruct((dec_batch, 8, LANES), F32),
            jax.ShapeDtypeStruct((dec_batch, 8, LANES), F32),
        ],
        compiler_params=_cparams(("arbitrary", "arbitrary")),
        name="dsa_sample_score",
    )(page_table, *([cache_idx_t] * PAGES_PER_STEP), qi, wi, ki_new)

    ck = cache_k_l.reshape(n_pool, PAGE_SIZE * B_KV_HEADS, B_HEAD_DIM)
    cv = cache_v_l.reshape(n_pool, PAGE_SIZE * B_KV_HEADS, B_HEAD_DIM)
    kv_specs = [pl.BlockSpec((None, PAGE_SIZE * B_KV_HEADS, B_HEAD_DIM), page_map(r))
                for r in range(PAGES_PER_STEP)]
    o = pl.pallas_call(
        functools.partial(_dsa_sample_attend_kernel, dec_seq=dec_seq, n_steps=n_steps),
        grid_spec=pltpu.PrefetchScalarGridSpec(
            num_scalar_prefetch=1,
            grid=(dec_batch, n_steps),
            in_specs=kv_specs + kv_specs + [
                pl.BlockSpec((None, B_KV_HEADS, rows, B_HEAD_DIM), lambda b, s, pt: (b, 0, 0, 0)),
                pl.BlockSpec((None, None, 8, chunk), lambda b, s, pt: (b, s, 0, 0)),
                pl.BlockSpec((None, 8, LANES), lambda b, s, pt: (b, 0, 0)),
                pl.BlockSpec((None, 8, LANES), lambda b, s, pt: (b, 0, 0)),
                pl.BlockSpec((None, LANES, 2 * KV_WIDTH), lambda b, s, pt: (b, 0, 0)),
                pl.BlockSpec(sbias.shape, lambda b, s, pt: (0, 0, 0)),
                pl.BlockSpec(sconst.shape, lambda b, s, pt: (0, 0, 0)),
            ],
            out_specs=pl.BlockSpec((None, B_KV_HEADS, rows, B_HEAD_DIM), lambda b, s, pt: (b, 0, 0, 0)),
            scratch_shapes=[
                pltpu.VMEM((B_KV_HEADS, rows, 1), F32),
                pltpu.VMEM((B_KV_HEADS, rows, 1), F32),
                pltpu.VMEM((B_KV_HEADS, rows, B_HEAD_DIM), F32),
            ],
        ),
        out_shape=jax.ShapeDtypeStruct((dec_batch, B_KV_HEADS, rows, B_HEAD_DIM), BF16),
        compiler_params=_cparams(("arbitrary", "arbitrary")),
        name="dsa_sample_attend",
    )(page_table, *([ck] * PAGES_PER_STEP), *([cv] * PAGES_PER_STEP), q, sc, scn, thr, kv_new, sbias, sconst)
    o = o.reshape(dec_batch, B_KV_HEADS, B_GQA, SUBLANES, B_HEAD_DIM)[:, :, :, :dec_seq].transpose(0, 3, 1, 2, 4)
    return o.reshape(dec_batch * dec_seq, B_WIDTH)


def _mixout_kernel(u_ref, vn_ref, b_ref, ws_ref, ab_ref, x_ref, w_ref, o_ref, mix_scr, *, tm):
    j = pl.program_id(1)

    @pl.when(j == 0)
    def _():
        row = lax.broadcasted_iota(I32, (A_CHUNK, A_CHUNK), 0)
        col = lax.broadcasted_iota(I32, (A_CHUNK, A_CHUNK), 1)
        for g in range(A_GROUPS):
            wsg = jnp.where(row >= col, ws_ref[g], 0.0).astype(BF16)
            bg = ab_ref[:, g:g + 1]
            cs = slice(g * A_CH, (g + 1) * A_CH)
            for r in range(tm // A_CHUNK):
                rs = slice(r * A_CHUNK, (r + 1) * A_CHUNK)
                sm = jnp.dot(wsg, vn_ref[rs, cs], preferred_element_type=F32) + bg
                mix_scr[rs, cs] = (u_ref[rs, cs].astype(F32) * sm).astype(BF16)
        mix_scr[:, A_WIDTH:] = b_ref[...]

    o_ref[...] = x_ref[...] + jnp.dot(mix_scr[...], w_ref[...].astype(BF16), preferred_element_type=F32)


def _mixout(p, b_out, ws_eff, ab_eff, x2d, w_out_l, *, tm, tn=512):
    n = x2d.shape[0]
    return pl.pallas_call(
        functools.partial(_mixout_kernel, tm=tm),
        grid=(n // tm, D_MODEL // tn),
        in_specs=[
            pl.BlockSpec((tm, A_WIDTH), lambda i, j: (i, COL_U // A_WIDTH)),
            pl.BlockSpec((tm, A_WIDTH), lambda i, j: (i, COL_V // A_WIDTH)),
            pl.BlockSpec((tm, B_WIDTH), lambda i, j: (i, 0)),
            pl.BlockSpec((A_GROUPS, A_CHUNK, A_CHUNK), lambda i, j: (0, 0, 0)),
            pl.BlockSpec((A_CHUNK, A_GROUPS), lambda i, j: (0, 0)),
            pl.BlockSpec((tm, tn), lambda i, j: (i, j)),
            pl.BlockSpec((A_WIDTH + B_WIDTH, tn), lambda i, j: (0, j)),
        ],
        out_specs=pl.BlockSpec((tm, tn), lambda i, j: (i, j)),
        out_shape=jax.ShapeDtypeStruct((n, D_MODEL), F32),
        scratch_shapes=[pltpu.VMEM((tm, A_WIDTH + B_WIDTH), BF16)],
        compiler_params=_cparams(("arbitrary", "arbitrary")),
        name="mixout",
    )(p, p, b_out, ws_eff, ab_eff, x2d, w_out_l)


def _memkv_kernel(x_ref, wk_ref, wv_ref, kn_ref, mk_ref, mv_ref):
    x = x_ref[...].astype(BF16)
    tm = x.shape[0]
    mk = _head_norm(jnp.dot(x, wk_ref[...].astype(BF16), preferred_element_type=F32), kn_ref[...])
    mv = jnp.dot(x, wv_ref[...].astype(BF16), preferred_element_type=F32)
    for hd in range(MEM_HEADS):
        cs = slice(hd * MEM_HEAD_DIM, (hd + 1) * MEM_HEAD_DIM)
        mk_ref[pl.ds(hd, tm, stride=MEM_HEADS), :] = mk[:, cs]
        mv_ref[pl.ds(hd, tm, stride=MEM_HEADS), :] = mv[:, cs]


def _memkv(mem2d, wk, wv, knorm, *, tm=256):
    n = mem2d.shape[0]
    kn = jnp.tile(knorm, MEM_HEADS).reshape(1, MEM_WIDTH)
    return pl.pallas_call(
        _memkv_kernel,
        grid=(n // tm,),
        in_specs=[
            pl.BlockSpec((tm, D_MODEL), lambda i: (i, 0)),
            pl.BlockSpec((D_MODEL, MEM_WIDTH), lambda i: (0, 0)),
            pl.BlockSpec((D_MODEL, MEM_WIDTH), lambda i: (0, 0)),
            pl.BlockSpec((1, MEM_WIDTH), lambda i: (0, 0)),
        ],
        out_specs=[pl.BlockSpec((tm * MEM_HEADS, MEM_HEAD_DIM), lambda i: (i, 0))] * 2,
        out_shape=[jax.ShapeDtypeStruct((n * MEM_HEADS, MEM_HEAD_DIM), F32)] * 2,
        compiler_params=_cparams(("arbitrary",)),
        name="memkv",
    )(mem2d, wk, wv, kn)


def _memattn_kernel(h_ref, g_ref, wq_ref, qn_ref, mk_ref, mv_ref, wo_ref, o_ref, *, rows_per_mem, mem_len):
    h = h_ref[...]
    n = _row_norm(h, g_ref[...]).astype(BF16)
    q = _head_norm(jnp.dot(n, wq_ref[...].astype(BF16), preferred_element_type=F32), qn_ref[...]).astype(BF16)
    tm, m = h.shape[0], mk_ref.shape[0] // MEM_HEADS
    if rows_per_mem is not None:
        row = lax.broadcasted_iota(I32, (tm, m), 0)
        col = lax.broadcasted_iota(I32, (tm, m), 1)
        mask = jnp.where(row // rows_per_mem == col // mem_len, 0.0, -jnp.inf)
    outs = []
    for hd in range(MEM_HEADS):
        cs = slice(hd * MEM_HEAD_DIM, (hd + 1) * MEM_HEAD_DIM)
        mk = mk_ref[pl.ds(hd, m, stride=MEM_HEADS), :].astype(BF16)
        mv = mv_ref[pl.ds(hd, m, stride=MEM_HEADS), :].astype(BF16)
        lg = _dot_nt(q[:, cs], mk) * (MEM_HEAD_DIM ** -0.5)
        if rows_per_mem is not None:
            lg = lg + mask
        e = jnp.exp(lg - jnp.max(lg, axis=-1, keepdims=True))
        l = jnp.sum(e, axis=-1, keepdims=True)
        outs.append(jnp.dot(e.astype(BF16), mv, preferred_element_type=F32) / l)
    o = jnp.concatenate(outs, axis=-1).astype(BF16)
    o_ref[...] = h + jnp.dot(o, wo_ref[...].astype(BF16), preferred_element_type=F32)


def _memattn(h2d, gain, wq, qnorm, mk, mv, wo, *, tm, mem_rows, tiles_per_mem, rows_per_mem, mem_len):
    n = h2d.shape[0]
    qn = jnp.tile(qnorm, MEM_HEADS).reshape(1, MEM_WIDTH)
    return pl.pallas_call(
        functools.partial(_memattn_kernel, rows_per_mem=rows_per_mem, mem_len=mem_len),
        grid=(n // tm,),
        in_specs=[
            pl.BlockSpec((tm, D_MODEL), lambda i: (i, 0)),
            pl.BlockSpec((1, D_MODEL), lambda i: (0, 0)),
            pl.BlockSpec((D_MODEL, MEM_WIDTH), lambda i: (0, 0)),
            pl.BlockSpec((1, MEM_WIDTH), lambda i: (0, 0)),
            pl.BlockSpec((mem_rows * MEM_HEADS, MEM_HEAD_DIM), lambda i: (i // tiles_per_mem, 0)),
            pl.BlockSpec((mem_rows * MEM_HEADS, MEM_HEAD_DIM), lambda i: (i // tiles_per_mem, 0)),
            pl.BlockSpec((MEM_WIDTH, D_MODEL), lambda i: (0, 0)),
        ],
        out_specs=pl.BlockSpec((tm, D_MODEL), lambda i: (i, 0)),
        out_shape=jax.ShapeDtypeStruct((n, D_MODEL), F32),
        compiler_params=_cparams(("arbitrary",)),
        name="memattn",
    )(h2d, gain.reshape(1, D_MODEL), wq, qn, mk, mv, wo)


def _router_kernel(h_ref, g_ref, w_ref, b_ref, n_ref, c_ref):
    n = _row_norm(h_ref[...], g_ref[...]).astype(BF16)
    n_ref[...] = n
    lg = jnp.dot(n, w_ref[...].astype(BF16), preferred_element_type=F32) + b_ref[...]
    gl, el = lg[:, :LANES], lg[:, LANES:]
    lane = lax.broadcasted_iota(I32, gl.shape, 1)
    lanef = lane.astype(F32)

    def first_max_lane(v, vmax):
        return jnp.min(jnp.where(v == vmax, lanef, float(LANES)), axis=-1, keepdims=True).astype(I32)

    gl = jnp.where(lane < N_GROUPS, gl, -jnp.inf)
    ge = jnp.exp(gl - jnp.max(gl, axis=-1, keepdims=True))
    gp = ge / jnp.sum(ge, axis=-1, keepdims=True)
    g_gate = jnp.max(gp, axis=-1, keepdims=True)
    g_sel = first_max_lane(gp, g_gate)
    in_grp = (lane // EXPERTS_PER_GROUP == g_sel) & (lane < N_EXPERTS)
    el = jnp.where(in_grp, el, -jnp.inf)
    ee = jnp.exp(el - jnp.max(el, axis=-1, keepdims=True))
    ep = jnp.where(in_grp, ee / jnp.sum(ee, axis=-1, keepdims=True), -jnp.inf)
    w1 = jnp.max(ep, axis=-1, keepdims=True)
    i1 = first_max_lane(ep, w1)
    ep2 = jnp.where(lane == i1, -jnp.inf, ep)
    w2 = jnp.max(ep2, axis=-1, keepdims=True)
    i2 = first_max_lane(ep2, w2)
    tot = w1 + w2
    comb = jnp.where(lane == i1, w1 / tot * g_gate, jnp.where(lane == i2, w2 / tot * g_gate, 0.0))
    c_ref[...] = jnp.where(lane == GSEL_LANE, g_sel.astype(F32), comb)


def _router(h2d, gain, w_cat, b_cat, *, tm):
    n = h2d.shape[0]
    return pl.pallas_call(
        _router_kernel,
        grid=(n // tm,),
        in_specs=[
            pl.BlockSpec((tm, D_MODEL), lambda i: (i, 0)),
            pl.BlockSpec((1, D_MODEL), lambda i: (0, 0)),
            pl.BlockSpec((D_MODEL, 2 * LANES), lambda i: (0, 0)),
            pl.BlockSpec((1, 2 * LANES), lambda i: (0, 0)),
        ],
        out_specs=[pl.BlockSpec((tm, D_MODEL), lambda i: (i, 0)), pl.BlockSpec((tm, LANES), lambda i: (i, 0))],
        out_shape=[jax.ShapeDtypeStruct((n, D_MODEL), BF16), jax.ShapeDtypeStruct((n, LANES), F32)],
        compiler_params=_cparams(("arbitrary",)),
        name="router",
    )(h2d, gain.reshape(1, D_MODEL), w_cat, b_cat)


GSEL_LANE = N_EXPERTS
MOE_ROWS = 128
MOE_FF_SPLIT = 2


def _moe_routed_kernel(n_ref, c_ref, h_hbm, wg_ref, wu_ref, wd_ref, o_ref,
                       tri_scr, key_scr, xc_scr, yc_scr, cw_scr, nch_ref, sem, *, tm):
    i, g, e, f = (pl.program_id(k) for k in range(4))
    nchunk = tm // MOE_ROWS
    ff = EXPERT_FF // MOE_FF_SPLIT
    first_e = (e == 0) & (f == 0)
    last_e = (e == EXPERTS_PER_GROUP - 1) & (f == MOE_FF_SPLIT - 1)

    def onehot_rows(c):
        want = (c * MOE_ROWS + lax.broadcasted_iota(I32, (MOE_ROWS, 1), 0)).astype(F32)
        return jnp.where(key_scr[pl.ds(g, 1), :] == want, 1.0, 0.0).astype(BF16)

    @pl.when((i == 0) & (g == 0) & first_e)
    def _():
        r = lax.broadcasted_iota(I32, (tm, tm), 0)
        c = lax.broadcasted_iota(I32, (tm, tm), 1)
        tri_scr[...] = jnp.where(r < c, 1.0, 0.0).astype(BF16)

    @pl.when((g == 0) & first_e)
    def _():
        cp = pltpu.make_async_copy(h_hbm.at[pl.ds(pl.multiple_of(i * tm, tm), tm)], o_ref, sem)
        cp.start()
        lane = lax.broadcasted_iota(I32, (tm, LANES), 1).astype(F32)
        onehot = jnp.where(lane == c_ref[:, GSEL_LANE:GSEL_LANE + 1], 1.0, 0.0).astype(BF16)
        er = lax.broadcasted_iota(I32, (SUBLANES, LANES), 0)
        ec = lax.broadcasted_iota(I32, (SUBLANES, LANES), 1)
        eye = jnp.where(er == ec, 1.0, 0.0).astype(BF16)
        mask_t = _dot_nt(eye, onehot)
        rank_t = jnp.dot(mask_t.astype(BF16), tri_scr[...], preferred_element_type=F32)
        key_scr[...] = jnp.where(mask_t > 0.5, rank_t, -1.0)
        for gg in range(N_GROUPS):
            cnt = jnp.sum(mask_t[gg:gg + 1, :]).astype(I32)
            nch_ref[gg] = (cnt + (MOE_ROWS - 1)) // MOE_ROWS
        cp.wait()

    nch = nch_ref[g]

    @pl.when(first_e)
    def _():
        comb = c_ref[...]
        hi = comb.astype(BF16)
        r1 = comb - hi.astype(F32)
        mid = r1.astype(BF16)
        lo = (r1 - mid.astype(F32)).astype(BF16)

        def compact(c, carry):
            s = onehot_rows(c)
            xc_scr[c] = jnp.dot(s, n_ref[...], preferred_element_type=F32).astype(BF16)
            cw_scr[c] = (jnp.dot(s, hi, preferred_element_type=F32)
                         + jnp.dot(s, mid, preferred_element_type=F32)
                         + jnp.dot(s, lo, preferred_element_type=F32))
            yc_scr[c] = jnp.zeros((MOE_ROWS, D_MODEL), F32)
            return carry

        lax.fori_loop(0, nch, compact, 0)

    wg = wg_ref[...].astype(BF16)
    wu = wu_ref[...].astype(BF16)
    wd = wd_ref[...].astype(BF16)
    ex = g * EXPERTS_PER_GROUP + e
    lane = lax.broadcasted_iota(I32, (MOE_ROWS, LANES), 1)

    def expert(c, carry):
        x = xc_scr[c]
        a = jnp.dot(x, wg, preferred_element_type=F32)
        u = jnp.dot(x, wu, preferred_element_type=F32)
        cw = jnp.sum(jnp.where(lane == ex, cw_scr[c], 0.0), axis=-1, keepdims=True)
        hid = (a * jax.nn.sigmoid(a) * u * cw).astype(BF16)
        yc_scr[c] += jnp.dot(hid, wd, preferred_element_type=F32)
        return carry

    lax.fori_loop(0, nch, expert, 0)

    @pl.when(last_e)
    def _():
        def scatter(c, carry):
            s = onehot_rows(c)
            o_ref[...] += lax.dot_general(s, yc_scr[c].astype(BF16), (((0,), (0,)), ((), ())),
                                          preferred_element_type=F32)
            return carry

        lax.fori_loop(0, nch, scatter, 0)


def _moe_routed(n3, comb, h2d, w_gate_l, w_up_l, w_down_l, *, tm):
    n = h2d.shape[0]
    ff = EXPERT_FF // MOE_FF_SPLIT

    def wmap(i, g, e, f):
        return (g * EXPERTS_PER_GROUP + e, 0, f)

    return pl.pallas_call(
        functools.partial(_moe_routed_kernel, tm=tm),
        grid=(n // tm, N_GROUPS, EXPERTS_PER_GROUP, MOE_FF_SPLIT),
        in_specs=[
            pl.BlockSpec((tm, D_MODEL), lambda i, g, e, f: (i, 0), pipeline_mode=pl.Buffered(1)),
            pl.BlockSpec((tm, LANES), lambda i, g, e, f: (i, 0), pipeline_mode=pl.Buffered(1)),
            pl.BlockSpec(memory_space=pl.ANY),
            pl.BlockSpec((None, D_MODEL, ff), wmap),
            pl.BlockSpec((None, D_MODEL, ff), wmap),
            pl.BlockSpec((None, ff, D_MODEL), lambda i, g, e, f: (g * EXPERTS_PER_GROUP + e, f, 0)),
        ],
        out_specs=pl.BlockSpec((tm, D_MODEL), lambda i, g, e, f: (i, 0)),
        out_shape=jax.ShapeDtypeStruct((n, D_MODEL), F32),
        scratch_shapes=[
            pltpu.VMEM((tm, tm), BF16),
            pltpu.VMEM((SUBLANES, tm), F32),
            pltpu.VMEM((tm // MOE_ROWS, MOE_ROWS, D_MODEL), BF16),
            pltpu.VMEM((tm // MOE_ROWS, MOE_ROWS, D_MODEL), F32),
            pltpu.VMEM((tm // MOE_ROWS, MOE_ROWS, LANES), F32),
            pltpu.SMEM((N_GROUPS,), I32),
            pltpu.SemaphoreType.DMA(()),
        ],
        compiler_params=pltpu.CompilerParams(
            dimension_semantics=("arbitrary",) * 4, vmem_limit_bytes=MOE_VMEM_LIMIT),
        name="moe_routed",
    )(n3, comb, h2d, w_gate_l, w_up_l, w_down_l)


def _moe_kernel(n_ref, c_ref, h_ref, wg_ref, wu_ref, wd_ref, o_ref):
    e = pl.program_id(1)

    @pl.when(e == 0)
    def _():
        o_ref[...] = h_ref[...]

    x = n_ref[...]
    a = jnp.dot(x, wg_ref[...].astype(BF16), preferred_element_type=F32)
    u = jnp.dot(x, wu_ref[...].astype(BF16), preferred_element_type=F32)
    lane = lax.broadcasted_iota(I32, c_ref.shape, 1)
    c = jnp.sum(jnp.where(lane == e, c_ref[...], 0.0), axis=-1, keepdims=True)
    hid = (a * jax.nn.sigmoid(a) * u * c).astype(BF16)
    o_ref[...] += jnp.dot(hid, wd_ref[...].astype(BF16), preferred_element_type=F32)


def _moe(n3, comb, h2d, w_gate_l, w_up_l, w_down_l, *, tm):
    n = h2d.shape[0]
    return pl.pallas_call(
        _moe_kernel,
        grid=(n // tm, N_EXPERTS),
        in_specs=[
            pl.BlockSpec((tm, D_MODEL), lambda i, e: (i, 0)),
            pl.BlockSpec((tm, LANES), lambda i, e: (i, 0)),
            pl.BlockSpec((tm, D_MODEL), lambda i, e: (i, 0)),
            pl.BlockSpec((None, D_MODEL, EXPERT_FF), lambda i, e: (e, 0, 0)),
            pl.BlockSpec((None, D_MODEL, EXPERT_FF), lambda i, e: (e, 0, 0)),
            pl.BlockSpec((None, EXPERT_FF, D_MODEL), lambda i, e: (e, 0, 0)),
        ],
        out_specs=pl.BlockSpec((tm, D_MODEL), lambda i, e: (i, 0)),
        out_shape=jax.ShapeDtypeStruct((n, D_MODEL), F32),
        compiler_params=_cparams(("arbitrary", "arbitrary")),
        name="moe",
    )(n3, comb, h2d, w_gate_l, w_up_l, w_down_l)


def _post_mixer(p, b_out, ws_eff, ab_eff, x2d, mk, mv, lw, *, tm_mix, tm_mem, mem_rows, tiles_per_mem,
                rows_per_mem, mem_len, tm_route, tm_moe):
    h1 = _mixout(p, b_out, ws_eff, ab_eff, x2d, lw["w_out"], tm=tm_mix)
    h2 = _memattn(h1, lw["norm_mem"], lw["mem_wq"], lw["mem_qnorm"], mk, mv, lw["mem_wo"], tm=tm_mem,
                  mem_rows=mem_rows, tiles_per_mem=tiles_per_mem, rows_per_mem=rows_per_mem, mem_len=mem_len)
    n3, comb = _router(h2, lw["norm_ffn"], lw["w_cat"], lw["b_cat"], tm=tm_route)
    moe = _moe_routed if tm_moe > N_GROUPS * MOE_ROWS else _moe
    return moe(n3, comb, h2, lw["w_gate"], lw["w_up"], lw["w_down"], tm=tm_moe)


def kernel(x_prompt, x_sample, cache_k, cache_v, cache_idx_k, cache_mem_k, cache_mem_v, page_table,
           mem_prompt, norm_mix, w_in, a_vnorm, a_ws, a_b, b_qnorm, b_knorm, rel_bias, w_out,
           norm_mem, mem_wq, mem_wk, mem_wv, mem_qnorm, mem_knorm, mem_wo, norm_ffn,
           w_group, b_group, w_router, b_router, w_gate, w_up, w_down):
    bp, t, d = x_prompt.shape
    bd, s, _ = x_sample.shape
    depth = w_in.shape[0]
    mem_len = mem_prompt.shape[1]
    assert d == D_MODEL and t % KEY_CHUNK == 0 and s <= 8 and (bd * s) % LANES == 0

    bias_pairs, sbias, sconst = _bias_tables(rel_bias)
    hp = x_prompt.reshape(bp * t, d)
    hs = x_sample.reshape(bd * s, d)
    outs = [[] for _ in range(9)]
    for l in range(depth):
        w_cat = jnp.zeros((d, 2 * LANES), F32)
        w_cat = w_cat.at[:, :N_GROUPS].set(w_group[l]).at[:, LANES:LANES + N_EXPERTS].set(w_router[l])
        b_cat = jnp.zeros((1, 2 * LANES), F32)
        b_cat = b_cat.at[0, :N_GROUPS].set(b_group[l]).at[0, LANES:LANES + N_EXPERTS].set(b_router[l])
        lw = dict(w_out=w_out[l], norm_mem=norm_mem[l], mem_wq=mem_wq[l], mem_qnorm=mem_qnorm[l],
                  mem_wo=mem_wo[l], norm_ffn=norm_ffn[l], w_cat=w_cat, b_cat=b_cat,
                  w_gate=w_gate[l], w_up=w_up[l], w_down=w_down[l])

        p, k, v, tail = _proj(hp, norm_mix[l], w_in[l], a_vnorm[l], b_qnorm[l], b_knorm[l],
                              tm=1024, want_vn=False)
        b_out = _dsa_prompt(p, tail, bias_pairs, batch=bp, seq=t)
        mk, mv = _memkv(mem_prompt.reshape(bp * mem_len, d), mem_wk[l], mem_wv[l], mem_knorm[l])
        hp = _post_mixer(p, b_out, a_ws[l], a_b[l].T, hp, mk, mv, lw, tm_mix=1024, tm_mem=512,
                         mem_rows=mem_len, tiles_per_mem=t // 512, rows_per_mem=None, mem_len=mem_len,
                         tm_route=512, tm_moe=1024)
        outs[0].append(k.reshape(bp, t, B_KV_HEADS, B_HEAD_DIM))
        outs[1].append(v.reshape(bp, t, B_KV_HEADS, B_HEAD_DIM))
        outs[2].append(tail[:, :IDX_DIM].reshape(bp, t, IDX_DIM))
        outs[3].append(mk.reshape(bp, mem_len, MEM_HEADS, MEM_HEAD_DIM))
        outs[4].append(mv.reshape(bp, mem_len, MEM_HEADS, MEM_HEAD_DIM))

        ns = bd * s
        p_s, k_s, v_s, tail_s, vn_s = _proj(hs, norm_mix[l], w_in[l], a_vnorm[l], b_qnorm[l], b_knorm[l],
                                            tm=ns, want_vn=True)
        b_out_s = _dsa_sample(p_s, tail_s, k_s, v_s, cache_k[l], cache_v[l], cache_idx_k[l], page_table,
                              sbias, sconst, dec_batch=bd, dec_seq=s)
        ws_s = jnp.einsum("ab,gts->gatbs", jnp.eye(ns // s, dtype=F32), a_ws[l][:, :s, :s])
        ws_s = ws_s.reshape(A_GROUPS, ns, ns)
        ab_s = jnp.tile(a_b[l][:, :s].T, (ns // s, 1))
        mem_tile = 32
        hs = _post_mixer(p_s, b_out_s, ws_s, ab_s, hs,
                         cache_mem_k[l].reshape(bd * mem_len * MEM_HEADS, MEM_HEAD_DIM),
                         cache_mem_v[l].reshape(bd * mem_len * MEM_HEADS, MEM_HEAD_DIM), lw,
                         tm_mix=ns, tm_mem=mem_tile, mem_rows=mem_tile // s * mem_len, tiles_per_mem=1,
                         rows_per_mem=s, mem_len=mem_len, tm_route=ns, tm_moe=ns)
        outs[5].append(k_s.reshape(bd, s, B_KV_HEADS, B_HEAD_DIM))
        outs[6].append(v_s.reshape(bd, s, B_KV_HEADS, B_HEAD_DIM))
        outs[7].append(tail_s[:, :IDX_DIM].reshape(bd, s, IDX_DIM))
        outs[8].append(vn_s.reshape(bd, s, A_GROUPS, A_CH))
    return (hp.reshape(bp, t, d), hs.reshape(bd, s, d)) + tuple(jnp.stack(o) for o in outs)
```

```python
import functools
import math

import jax
import jax.numpy as jnp
from jax import lax
from jax.experimental import pallas as pl
from jax.experimental.pallas import tpu as pltpu

BF16 = jnp.bfloat16
F32 = jnp.float32
I32 = jnp.int32

D_MODEL = 2048
A_GROUPS = 8
A_CH = 128
A_WIDTH = A_GROUPS * A_CH
A_CHUNK = 128
B_HEADS = 8
B_HEAD_DIM = 128
B_KV_HEADS = 2
B_GQA = B_HEADS // B_KV_HEADS
B_WIDTH = B_HEADS * B_HEAD_DIM
KV_WIDTH = B_KV_HEADS * B_HEAD_DIM
IDX_HEADS = 16
IDX_DIM = 64
TOPK_MAX = 256
Q_BLOCK = 128
REL_BUCKETS = 32
REL_MAX_DIST = 128
MEM_HEADS = 4
MEM_HEAD_DIM = 128
MEM_WIDTH = MEM_HEADS * MEM_HEAD_DIM
N_GROUPS = 4
EXPERTS_PER_GROUP = 4
N_EXPERTS = N_GROUPS * EXPERTS_PER_GROUP
EXPERT_FF = 512
PAGE_SIZE = 128
EPS = 1e-6

COL_U = 0
COL_V = COL_U + A_WIDTH
COL_Q = COL_V + A_WIDTH
COL_K = COL_Q + B_WIDTH
COL_VV = COL_K + KV_WIDTH
COL_QI = COL_VV + KV_WIDTH
COL_MAIN = COL_QI + IDX_HEADS * IDX_DIM
TAIL_COLS = IDX_DIM + IDX_HEADS
LANES = 128
SUBLANES = 8
PROJ_TN = 512
KEY_CHUNK = 256
NEG_INIT = -1e30
VMEM_LIMIT = 56 * 1024 * 1024
MOE_VMEM_LIMIT = 60 * 1024 * 1024


def _cparams(sem):
    return pltpu.CompilerParams(dimension_semantics=sem, vmem_limit_bytes=VMEM_LIMIT)


def _gelu(x):
    return 0.5 * x * (1.0 + lax.erf(x * (2.0 ** -0.5)))


def _head_norm(a, gain, width=LANES):
    outs = []
    for c in range(a.shape[1] // width):
        blk = a[:, c * width:(c + 1) * width]
        ms = jnp.mean(blk * blk, axis=-1, keepdims=True)
        outs.append(blk * lax.rsqrt(ms + EPS) * gain[:, c * width:(c + 1) * width])
    return outs[0] if len(outs) == 1 else jnp.concatenate(outs, axis=-1)


def _row_norm(x, gain):
    ms = jnp.mean(x * x, axis=-1, keepdims=True)
    return x * lax.rsqrt(ms + EPS) * gain


def _dot_nt(a, b):
    return lax.dot_general(a, b, (((1,), (1,)), ((), ())), preferred_element_type=F32)


def _proj_kernel(x_ref, g_ref, w_ref, wt_ref, avn_ref, qn_ref, kn_ref, *rest, want_vn):
    if want_vn:
        p_ref, k_ref, v_ref, tail_ref, vn_ref, xn_scr = rest
    else:
        p_ref, k_ref, v_ref, tail_ref, xn_scr = rest
        vn_ref = None
    j = pl.program_id(1)
    ju = COL_V // PROJ_TN
    jv = COL_Q // PROJ_TN
    jq = COL_K // PROJ_TN
    jkv = COL_QI // PROJ_TN

    @pl.when(j == 0)
    def _():
        xn = _row_norm(x_ref[...], g_ref[...]).astype(BF16)
        xn_scr[...] = xn
        wlane = lax.broadcasted_iota(I32, wt_ref.shape, 1)
        wt = jnp.where(wlane < TAIL_COLS, wt_ref[...], 0.0).astype(BF16)
        t = jnp.dot(xn, wt, preferred_element_type=F32)
        lane = lax.broadcasted_iota(I32, t.shape, 1)
        tail_ref[...] = jnp.where(lane >= IDX_DIM, t * (IDX_HEADS ** -0.5), t)

    acc = jnp.dot(xn_scr[...], w_ref[...].astype(BF16), preferred_element_type=F32)

    @pl.when(j < ju)
    def _():
        p_ref[...] = _gelu(acc).astype(BF16)

    @pl.when((j >= ju) & (j < jv))
    def _():
        vn = _head_norm(_gelu(acc), avn_ref[...])
        p_ref[...] = vn.astype(BF16)
        if vn_ref is not None:
            vn_ref[...] = vn

    @pl.when((j >= jv) & (j < jq))
    def _():
        p_ref[...] = _head_norm(acc, qn_ref[...]).astype(BF16)

    @pl.when(j == jq)
    def _():
        k = _head_norm(acc[:, :KV_WIDTH], kn_ref[...])
        v = acc[:, KV_WIDTH:]
        k_ref[...] = k
        v_ref[...] = v
        p_ref[...] = jnp.concatenate([k, v], axis=-1).astype(BF16)

    @pl.when(j >= jkv)
    def _():
        p_ref[...] = acc.astype(BF16)


def _proj(x2d, gain, w_in_l, a_vnorm_l, b_qnorm_l, b_knorm_l, *, tm, want_vn):
    n = x2d.shape[0]
    nj = COL_MAIN // PROJ_TN
    avn = a_vnorm_l.reshape(1, A_WIDTH)
    qn = jnp.tile(b_qnorm_l, PROJ_TN // B_HEAD_DIM).reshape(1, PROJ_TN)
    kn = jnp.tile(b_knorm_l, B_KV_HEADS).reshape(1, KV_WIDTH)
    ju = COL_V // PROJ_TN
    nv = A_WIDTH // PROJ_TN

    def vmap_(i, j):
        return (0, jnp.clip(j - ju, 0, nv - 1))

    in_specs = [
        pl.BlockSpec((tm, D_MODEL), lambda i, j: (i, 0)),
        pl.BlockSpec((1, D_MODEL), lambda i, j: (0, 0)),
        pl.BlockSpec((D_MODEL, PROJ_TN), lambda i, j: (0, j)),
        pl.BlockSpec((D_MODEL, LANES), lambda i, j: (0, COL_MAIN // LANES)),
        pl.BlockSpec((1, PROJ_TN), vmap_),
        pl.BlockSpec((1, PROJ_TN), lambda i, j: (0, 0)),
        pl.BlockSpec((1, KV_WIDTH), lambda i, j: (0, 0)),
    ]
    out_shape = [
        jax.ShapeDtypeStruct((n, COL_MAIN), BF16),
        jax.ShapeDtypeStruct((n, KV_WIDTH), F32),
        jax.ShapeDtypeStruct((n, KV_WIDTH), F32),
        jax.ShapeDtypeStruct((n, LANES), F32),
    ]
    out_specs = [
        pl.BlockSpec((tm, PROJ_TN), lambda i, j: (i, j)),
        pl.BlockSpec((tm, KV_WIDTH), lambda i, j: (i, 0)),
        pl.BlockSpec((tm, KV_WIDTH), lambda i, j: (i, 0)),
        pl.BlockSpec((tm, LANES), lambda i, j: (i, 0)),
    ]
    if want_vn:
        out_shape.append(jax.ShapeDtypeStruct((n, A_WIDTH), F32))
        out_specs.append(pl.BlockSpec((tm, PROJ_TN), lambda i, j: (i, jnp.clip(j - ju, 0, nv - 1))))
    return pl.pallas_call(
        functools.partial(_proj_kernel, want_vn=want_vn),
        grid=(n // tm, nj),
        in_specs=in_specs,
        out_specs=out_specs,
        out_shape=out_shape,
        scratch_shapes=[pltpu.VMEM((tm, D_MODEL), BF16)],
        compiler_params=_cparams(("arbitrary", "arbitrary")),
        name="proj",
    )(x2d, gain.reshape(1, D_MODEL), w_in_l, w_in_l, avn, qn, kn)


def _t5_bucket(d):
    max_exact = REL_BUCKETS // 2
    d = jnp.maximum(d, 0)
    ratio = jnp.log(jnp.maximum(d, 1).astype(F32) / max_exact) / math.log(REL_MAX_DIST / max_exact)
    large = jnp.minimum(max_exact + jnp.floor(ratio * (REL_BUCKETS - max_exact)).astype(I32), REL_BUCKETS - 1)
    return jnp.where(d < max_exact, d, large)


def _bias_lookup(bucket, rb_ref, h):
    acc = jnp.zeros(bucket.shape, F32)
    for r in range(REL_BUCKETS):
        acc = jnp.where(bucket == r, rb_ref[r, h], acc)
    return acc


def _bias_kernel(rb_ref, pair_ref, samp_ref, sconst_ref):
    q = lax.broadcasted_iota(I32, (Q_BLOCK, Q_BLOCK), 0)
    kc = lax.broadcasted_iota(I32, (Q_BLOCK, Q_BLOCK), 1)
    nvar = pair_ref.shape[0]
    rels = range(-(nvar - 1), 2)
    buckets = {r: _t5_bucket(q - kc - r * Q_BLOCK) for r in rels}
    for h in range(B_HEADS):
        g, hh = divmod(h, B_GQA)
        tiles = {r: _bias_lookup(buckets[r], rb_ref, h) for r in rels}
        for var in range(nvar):
            r0 = var - (nvar - 1)
            for t in range(2):
                pair_ref[var, g, hh * Q_BLOCK:(hh + 1) * Q_BLOCK, t * Q_BLOCK:(t + 1) * Q_BLOCK] = tiles[r0 + t]
    rows = B_GQA * SUBLANES
    row = lax.broadcasted_iota(I32, (rows, 2 * PAGE_SIZE), 0)
    col = lax.broadcasted_iota(I32, (rows, 2 * PAGE_SIZE), 1)
    t = row % SUBLANES
    d = jnp.where(col < PAGE_SIZE, t + PAGE_SIZE - col, t - (col - PAGE_SIZE))
    bucket = _t5_bucket(d)
    far = jnp.full((rows, LANES), REL_BUCKETS - 1, I32)
    rowc = lax.broadcasted_iota(I32, (rows, LANES), 0)
    for g in range(B_KV_HEADS):
        acc = jnp.zeros((rows, 2 * PAGE_SIZE), F32)
        accc = jnp.zeros((rows, LANES), F32)
        for hh in range(B_GQA):
            h = g * B_GQA + hh
            acc = jnp.where(row // SUBLANES == hh, _bias_lookup(bucket, rb_ref, h), acc)
            accc = jnp.where(rowc // SUBLANES == hh, _bias_lookup(far, rb_ref, h), accc)
        samp_ref[g] = acc
        sconst_ref[g] = accc


N_PAIR_VARIANTS = 4


def _bias_tables(rel_bias):
    rows = B_GQA * SUBLANES
    return pl.pallas_call(
        _bias_kernel,
        in_specs=[pl.BlockSpec(memory_space=pltpu.SMEM)],
        out_shape=[
            jax.ShapeDtypeStruct((N_PAIR_VARIANTS, B_KV_HEADS, B_GQA * Q_BLOCK, KEY_CHUNK), F32),
            jax.ShapeDtypeStruct((B_KV_HEADS, rows, 2 * PAGE_SIZE), F32),
            jax.ShapeDtypeStruct((B_KV_HEADS, rows, LANES), F32),
        ],
        name="bias_tables",
    )(rel_bias)


def _key_to_f32(key):
    bits = jnp.where(key < 0, key ^ I32(-2 ** 31), ~key)
    return lax.bitcast_convert_type(bits, F32)


def _kth_largest(count_ge, shape, k):
    def body(it, prefix):
        bit = 31 - it
        cand = prefix | lax.shift_left(I32(1), bit)
        cnt = count_ge(_key_to_f32(cand))
        return jnp.where(cnt >= k, cand, prefix)

    prefix = lax.fori_loop(0, 32, body, jnp.zeros(shape, I32))
    thr = _key_to_f32(prefix)
    return jnp.where((prefix & I32(-2 ** 23)) == 0, -jnp.inf, thr)


NO_INDEX_BOUND = 2 ** 30
TIE_REPAIR_ROUNDS = 2


def _topk_select(fold_sum, fold_min, shape, k, index_bits):
    thr = _kth_largest(lambda t: fold_sum(lambda sc, kp: sc >= t), shape, k)
    c_ge = fold_sum(lambda sc, kp: sc >= thr)
    no_bound = jnp.full(shape, NO_INDEX_BOUND, I32)

    def repair(_):
        v = thr
        strict = jnp.zeros(shape, F32)

        def in_set(sc, v, strict):
            return (sc > v) | ((sc == v) & (strict < 0.5))

        for _ in range(TIE_REPAIR_ROUNDS):
            c_set = fold_sum(lambda sc, kp: in_set(sc, v, strict))
            vmin = fold_min(lambda sc, kp: jnp.where(in_set(sc, v, strict), sc, jnp.inf))
            c_gt = fold_sum(lambda sc, kp: sc > vmin)
            drop = (c_set > k) & (c_gt >= k)
            v = jnp.where(drop, vmin, v)
            strict = jnp.where(drop, 1.0, strict)
        c_set = fold_sum(lambda sc, kp: in_set(sc, v, strict))
        vt = fold_min(lambda sc, kp: jnp.where(in_set(sc, v, strict), sc, jnp.inf))
        need = k - fold_sum(lambda sc, kp: sc > vt)

        def idx_body(it, m):
            cand = m | lax.shift_left(I32(1), index_bits - 1 - it)
            below = fold_sum(lambda sc, kp: (sc == vt) & (kp < cand))
            return jnp.where(below < need, cand, m)

        m = lax.fori_loop(0, index_bits, idx_body, jnp.zeros(shape, I32))
        tie = c_set > k
        v_out = jnp.where(tie, vt, v)
        m_out = jnp.where(tie, m, jnp.where(strict > 0.5, -1, no_bound))
        return v_out, m_out

    return lax.cond(jnp.max(c_ge) > k, repair, lambda _: (thr, no_bound), 0)


def _dsa_prompt_kernel(q_ref, kv_ref, qia_ref, qib_ref, tailk_ref, tailq_ref, bias_ref, o_ref,
                       kil_scr, kir_scr, sc_scr, lg_scr, m_scr, l_scr, acc_scr, *, topk):
    i = pl.program_id(1)

    @pl.when(i == 0)
    def _():
        ki = tailk_ref[:, :IDX_DIM].astype(BF16)
        z = jnp.zeros_like(ki)
        kil_scr[...] = jnp.concatenate([ki, z], axis=-1)
        kir_scr[...] = jnp.concatenate([z, ki], axis=-1)

    wi = tailq_ref[:, IDX_DIM:IDX_DIM + IDX_HEADS] * (IDX_DIM ** -0.5)
    nk = (i + 2) // 2
    qpos = i * Q_BLOCK + lax.broadcasted_iota(I32, (Q_BLOCK, 1), 0)
    kloc = lax.broadcasted_iota(I32, (1, KEY_CHUNK), 1)
    heads_per_ref = qia_ref.shape[1] // IDX_DIM

    pairs = []
    for hp in range(IDX_HEADS // 2):
        ref = qia_ref if 2 * hp < heads_per_ref else qib_ref
        base = (2 * hp) % heads_per_ref * IDX_DIM
        pairs.append(ref[:, base:base + 2 * IDX_DIM])
    qstack = jnp.concatenate(pairs, axis=0)

    def score_chunk(c, carry):
        off = pl.multiple_of(c * KEY_CHUNK, KEY_CHUNK)
        acc = jnp.zeros((Q_BLOCK, KEY_CHUNK), F32)
        for side, k_scr in ((0, kil_scr), (1, kir_scr)):
            s = _dot_nt(qstack, k_scr[pl.ds(off, KEY_CHUNK), :])
            for hp in range(IDX_HEADS // 2):
                h = 2 * hp + side
                acc = acc + wi[:, h:h + 1] * jnp.maximum(s[hp * Q_BLOCK:(hp + 1) * Q_BLOCK], 0.0)
        kpos = c * KEY_CHUNK + kloc
        sc_scr[c] = jnp.where(kpos <= qpos, acc, -jnp.inf)
        return carry

    lax.fori_loop(0, nk, score_chunk, 0)

    def fold_sum(pred):
        def body(c, cnt):
            m = jnp.where(pred(sc_scr[c], c * KEY_CHUNK + kloc), 1.0, 0.0)
            return cnt + m[:, :LANES] + m[:, LANES:]
        cnt = lax.fori_loop(0, nk, body, jnp.zeros((Q_BLOCK, LANES), F32))
        return jnp.sum(cnt, axis=-1, keepdims=True)

    def fold_min(val):
        def body(c, acc):
            x = val(sc_scr[c], c * KEY_CHUNK + kloc)
            return jnp.minimum(acc, jnp.minimum(x[:, :LANES], x[:, LANES:]))
        acc = lax.fori_loop(0, nk, body, jnp.full((Q_BLOCK, LANES), jnp.inf, F32))
        return jnp.min(acc, axis=-1, keepdims=True)

    index_bits = (kv_ref.shape[0] - 1).bit_length()
    sel_v, sel_m = _topk_select(fold_sum, fold_min, (Q_BLOCK, 1), topk, index_bits)

    def mask_chunk(c, carry):
        kpos = c * KEY_CHUNK + kloc
        sc = sc_scr[c]
        chosen = (sc > sel_v) | ((sc == sel_v) & (kpos <= sel_m))
        sc_scr[c] = jnp.where(chosen & (kpos <= qpos), 0.0, -jnp.inf)
        return carry

    lax.fori_loop(0, nk, mask_chunk, 0)

    scale = B_HEAD_DIM ** -0.5
    rows = B_GQA * Q_BLOCK
    qgs = [jnp.concatenate(
        [q_ref[:, (g * B_GQA + hh) * B_HEAD_DIM:(g * B_GQA + hh + 1) * B_HEAD_DIM] for hh in range(B_GQA)],
        axis=0) for g in range(B_KV_HEADS)]
    m_scr[...] = jnp.full(m_scr.shape, -jnp.inf, F32)
    l_scr[...] = jnp.zeros(l_scr.shape, F32)
    acc_scr[...] = jnp.zeros(acc_scr.shape, F32)

    def logits_chunk(c, carry):
        off = pl.multiple_of(c * KEY_CHUNK, KEY_CHUNK)
        var = jnp.clip(2 * c - i, -(N_PAIR_VARIANTS - 1), 0) + (N_PAIR_VARIANTS - 1)
        sel = sc_scr[c]
        sel = jnp.concatenate([sel] * B_GQA, axis=0)
        for g in range(B_KV_HEADS):
            kc = kv_ref[pl.ds(off, KEY_CHUNK), g * B_HEAD_DIM:(g + 1) * B_HEAD_DIM]
            lg = _dot_nt(qgs[g], kc) * scale + bias_ref[var, g] + sel
            lg_scr[c, g] = lg
            m_scr[g] = jnp.maximum(m_scr[g], jnp.maximum(lg[:, :LANES], lg[:, LANES:]))
        return carry

    lax.fori_loop(0, nk, logits_chunk, 0)
    for g in range(B_KV_HEADS):
        m = jnp.max(m_scr[g], axis=-1, keepdims=True)
        m_scr[g] = jnp.broadcast_to(m, (rows, LANES))

    def pv_chunk(c, carry):
        off = pl.multiple_of(c * KEY_CHUNK, KEY_CHUNK)
        for g in range(B_KV_HEADS):
            vc = kv_ref[pl.ds(off, KEY_CHUNK), KV_WIDTH + g * B_HEAD_DIM:KV_WIDTH + (g + 1) * B_HEAD_DIM]
            mb = m_scr[g]
            p = jnp.exp(lg_scr[c, g] - jnp.concatenate([mb, mb], axis=-1))
            l_scr[g] += p[:, :LANES] + p[:, LANES:]
            acc_scr[g] += jnp.dot(p.astype(BF16), vc, preferred_element_type=F32)
        return carry

    lax.fori_loop(0, nk, pv_chunk, 0)
    for g in range(B_KV_HEADS):
        o = acc_scr[g] / jnp.sum(l_scr[g], axis=-1, keepdims=True)
        for hh in range(B_GQA):
            h = g * B_GQA + hh
            o_ref[:, h * B_HEAD_DIM:(h + 1) * B_HEAD_DIM] = o[hh * Q_BLOCK:(hh + 1) * Q_BLOCK].astype(BF16)


def _dsa_prompt(p, tail, bias_pairs, *, batch, seq):
    topk = min(TOPK_MAX, seq // 4)
    nblk = seq // Q_BLOCK
    qi_w = PROJ_TN
    return pl.pallas_call(
        functools.partial(_dsa_prompt_kernel, topk=topk),
        grid=(batch, nblk),
        in_specs=[
            pl.BlockSpec((Q_BLOCK, B_WIDTH), lambda b, i: (b * nblk + i, COL_Q // B_WIDTH)),
            pl.BlockSpec((seq, 2 * KV_WIDTH), lambda b, i: (b, COL_K // (2 * KV_WIDTH))),
            pl.BlockSpec((Q_BLOCK, qi_w), lambda b, i: (b * nblk + i, COL_QI // qi_w)),
            pl.BlockSpec((Q_BLOCK, qi_w), lambda b, i: (b * nblk + i, COL_QI // qi_w + 1)),
            pl.BlockSpec((seq, LANES), lambda b, i: (b, 0)),
            pl.BlockSpec((Q_BLOCK, LANES), lambda b, i: (b * nblk + i, 0)),
            pl.BlockSpec(bias_pairs.shape, lambda b, i: (0, 0, 0, 0)),
        ],
        out_specs=pl.BlockSpec((Q_BLOCK, B_WIDTH), lambda b, i: (b * nblk + i, 0)),
        out_shape=jax.ShapeDtypeStruct((batch * seq, B_WIDTH), BF16),
        scratch_shapes=[
            pltpu.VMEM((seq, 2 * IDX_DIM), BF16),
            pltpu.VMEM((seq, 2 * IDX_DIM), BF16),
            pltpu.VMEM((seq // KEY_CHUNK, Q_BLOCK, KEY_CHUNK), F32),
            pltpu.VMEM((seq // KEY_CHUNK, B_KV_HEADS, B_GQA * Q_BLOCK, KEY_CHUNK), F32),
            pltpu.VMEM((B_KV_HEADS, B_GQA * Q_BLOCK, LANES), F32),
            pltpu.VMEM((B_KV_HEADS, B_GQA * Q_BLOCK, LANES), F32),
            pltpu.VMEM((B_KV_HEADS, B_GQA * Q_BLOCK, B_HEAD_DIM), F32),
        ],
        compiler_params=_cparams(("arbitrary", "arbitrary")),
        name="dsa_prompt",
    )(p, p, p, p, tail, tail, bias_pairs)


PAGES_PER_STEP = 16
SCORE_KEYS_PER_DOT = 2048


def _dsa_sample_score_kernel(pt_ref, *refs, topk, dec_seq, n_pages):
    del pt_ref
    pages = refs[:n_pages]
    qi_ref, wi_ref, kin_ref, sc_ref, scn_ref, selv_ref, selm_ref = refs[n_pages:]
    past = n_pages * PAGE_SIZE
    pages_per_dot = SCORE_KEYS_PER_DOT // PAGE_SIZE
    qi = qi_ref[...]
    wi = wi_ref[...] * (IDX_DIM ** -0.5)

    def scores(keys_t):
        r = jnp.maximum(jnp.dot(qi, keys_t, preferred_element_type=F32), 0.0) * wi
        acc = r[0:SUBLANES]
        for j in range(1, IDX_HEADS * dec_seq // SUBLANES):
            acc = acc + r[j * SUBLANES:(j + 1) * SUBLANES]
        shift = dec_seq
        while shift < SUBLANES:
            acc = acc + pltpu.roll(acc, shift, axis=0)
            shift *= 2
        return acc

    for d in range(n_pages // pages_per_dot):
        keys_t = jnp.concatenate(
            [pg[...].astype(BF16) for pg in pages[d * pages_per_dot:(d + 1) * pages_per_dot]], axis=1)
        sc_ref[:, d * SCORE_KEYS_PER_DOT:(d + 1) * SCORE_KEYS_PER_DOT] = scores(keys_t)

    row = lax.broadcasted_iota(I32, (SUBLANES, LANES), 0)
    col = lax.broadcasted_iota(I32, (SUBLANES, LANES), 1)
    scn = jnp.where((col <= row % dec_seq) & (col < dec_seq), scores(kin_ref[...]), -jnp.inf)
    scn_ref[...] = scn

    def pieces():
        yield scn, past + col[0:1]
        for w in range(past // LANES):
            yield sc_ref[:, w * LANES:(w + 1) * LANES], w * LANES + col[0:1]

    def fold_sum(pred):
        cnt = jnp.zeros((SUBLANES, LANES), F32)
        for sc, kp in pieces():
            cnt = cnt + jnp.where(pred(sc, kp), 1.0, 0.0)
        return jnp.sum(cnt, axis=-1, keepdims=True)

    def fold_min(val):
        acc = jnp.full((SUBLANES, LANES), jnp.inf, F32)
        for sc, kp in pieces():
            acc = jnp.minimum(acc, val(sc, kp))
        return jnp.min(acc, axis=-1, keepdims=True)

    index_bits = (past + LANES - 1).bit_length()
    sel_v, sel_m = _topk_select(fold_sum, fold_min, (SUBLANES, 1), topk, index_bits)
    selv_ref[...] = jnp.broadcast_to(sel_v, (SUBLANES, LANES))
    selm_ref[...] = jnp.broadcast_to(sel_m, (SUBLANES, LANES))


def _dsa_sample_attend_kernel(pt_ref, *refs, dec_seq, n_steps):
    del pt_ref
    kpages = refs[:PAGES_PER_STEP]
    vpages = refs[PAGES_PER_STEP:2 * PAGES_PER_STEP]
    (q_ref, sc_ref, scn_ref, selv_ref, selm_ref, kvn_ref, sbias_ref, sconst_ref, o_ref,
     m_scr, l_scr, acc_scr) = refs[2 * PAGES_PER_STEP:]
    s = pl.program_id(1)
    chunk = PAGES_PER_STEP * PAGE_SIZE
    rows = B_GQA * SUBLANES
    scale = B_HEAD_DIM ** -0.5

    @pl.when(s == 0)
    def _():
        m_scr[...] = jnp.full(m_scr.shape, NEG_INIT, F32)
        l_scr[...] = jnp.zeros(l_scr.shape, F32)
        acc_scr[...] = jnp.zeros(acc_scr.shape, F32)

    sel_v = selv_ref[:, 0:1]
    sel_m = selm_ref[:, 0:1]

    def chosen(sc, kpos):
        return jnp.where((sc > sel_v) | ((sc == sel_v) & (kpos <= sel_m)), 0.0, -jnp.inf)

    def head_rows(pgs, g):
        return jnp.concatenate(
            [pg[pl.ds(g, PAGE_SIZE, stride=B_KV_HEADS), :].astype(BF16) for pg in pgs], axis=0)

    kcol = lax.broadcasted_iota(I32, (SUBLANES, chunk), 1)
    sel = chosen(sc_ref[...], s * chunk + kcol)
    sel = jnp.concatenate([sel] * B_GQA, axis=0)
    is_last = s == n_steps - 1
    col = lax.broadcasted_iota(I32, (rows, chunk), 1)

    def update(g, lg, v):
        m_old = m_scr[g]
        l_old = l_scr[g]
        m_new = jnp.maximum(m_old, jnp.max(lg, axis=-1, keepdims=True))
        alpha = jnp.exp(m_old - m_new)
        p = jnp.exp(lg - m_new)
        l_scr[g] = alpha * l_old + jnp.sum(p, axis=-1, keepdims=True)
        acc_scr[g] = alpha * acc_scr[g] + jnp.dot(p.astype(BF16), v, preferred_element_type=F32)
        m_scr[g] = m_new

    for g in range(B_KV_HEADS):
        qg = q_ref[g]
        lg = _dot_nt(qg, head_rows(kpages, g)) * scale
        near = jnp.concatenate(
            [jnp.zeros((rows, chunk - PAGE_SIZE), F32), sbias_ref[g, :, :PAGE_SIZE]], axis=-1)
        bias = jnp.where(is_last & (col >= chunk - PAGE_SIZE), near, sconst_ref[g, :, 0:1])
        update(g, lg + bias + sel, head_rows(vpages, g))

    @pl.when(is_last)
    def _():
        ncol = lax.broadcasted_iota(I32, (SUBLANES, LANES), 1)
        seln = chosen(scn_ref[...], n_steps * chunk + ncol)
        seln = jnp.where(scn_ref[...] > -jnp.inf, seln, -jnp.inf)
        seln = jnp.concatenate([seln] * B_GQA, axis=0)
        for g in range(B_KV_HEADS):
            kn = kvn_ref[:, g * B_HEAD_DIM:(g + 1) * B_HEAD_DIM]
            vn = kvn_ref[:, KV_WIDTH + g * B_HEAD_DIM:KV_WIDTH + (g + 1) * B_HEAD_DIM]
            lg = _dot_nt(q_ref[g], kn) * scale + sbias_ref[g, :, PAGE_SIZE:] + seln
            update(g, lg, vn)
            o_ref[g] = (acc_scr[g] / l_scr[g]).astype(BF16)


def _dsa_sample(p_s, tail_s, k_s, v_s, cache_k_l, cache_v_l, cache_idx_l, page_table, sbias, sconst,
                *, dec_batch, dec_seq):
    n_pages = page_table.shape[1]
    past = n_pages * PAGE_SIZE
    topk = min(TOPK_MAX, (past + dec_seq) // 4)
    n_steps = n_pages // PAGES_PER_STEP
    chunk = PAGES_PER_STEP * PAGE_SIZE
    rows = B_GQA * SUBLANES
    n_pool = cache_k_l.shape[0]

    assert SUBLANES % dec_seq == 0
    qi = p_s[:, COL_QI:COL_MAIN].reshape(dec_batch, dec_seq, IDX_HEADS, IDX_DIM).transpose(0, 2, 1, 3)
    qi = qi.reshape(dec_batch, IDX_HEADS * dec_seq, IDX_DIM)
    wi = tail_s[:, IDX_DIM:IDX_DIM + IDX_HEADS].reshape(dec_batch, dec_seq, IDX_HEADS).transpose(0, 2, 1)
    wi = wi.reshape(dec_batch, IDX_HEADS * dec_seq, 1)
    ki_new = tail_s[:, :IDX_DIM].astype(BF16).reshape(dec_batch, dec_seq, IDX_DIM)
    ki_new = jnp.pad(ki_new, ((0, 0), (0, LANES - dec_seq), (0, 0))).transpose(0, 2, 1)
    q = p_s[:, COL_Q:COL_K].reshape(dec_batch, dec_seq, B_KV_HEADS, B_GQA, B_HEAD_DIM)
    q = jnp.pad(q.transpose(0, 2, 3, 1, 4), ((0, 0), (0, 0), (0, 0), (0, SUBLANES - dec_seq), (0, 0)))
    q = q.reshape(dec_batch, B_KV_HEADS, rows, B_HEAD_DIM)
    kv_new = jnp.concatenate([k_s, v_s], axis=-1).astype(BF16).reshape(dec_batch, dec_seq, 2 * KV_WIDTH)
    kv_new = jnp.pad(kv_new, ((0, 0), (0, LANES - dec_seq), (0, 0)))

    def page_map(r):
        return lambda b, s, pt: (pt[b, s * PAGES_PER_STEP + r], 0, 0)

    cache_idx_t = jnp.swapaxes(cache_idx_l, 1, 2)
    idx_specs = [pl.BlockSpec((None, IDX_DIM, PAGE_SIZE), lambda b, pt, r=r: (pt[b, r], 0, 0))
                 for r in range(n_pages)]
    stat_spec = pl.BlockSpec((None, SUBLANES, LANES), lambda b, pt: (b, 0, 0))
    sc, scn, selv, selm = pl.pallas_call(
        functools.partial(_dsa_sample_score_kernel, topk=topk, dec_seq=dec_seq, n_pages=n_pages),
        grid_spec=pltpu.PrefetchScalarGridSpec(
            num_scalar_prefetch=1,
            grid=(dec_batch,),
            in_specs=idx_specs + [
                pl.BlockSpec((None, IDX_HEADS * dec_seq, IDX_DIM), lambda b, pt: (b, 0, 0)),
                pl.BlockSpec((None, IDX_HEADS * dec_seq, 1), lambda b, pt: (b, 0, 0)),
                pl.BlockSpec((None, IDX_DIM, LANES), lambda b, pt: (b, 0, 0)),
            ],
            out_specs=[pl.BlockSpec((None, SUBLANES, past), lambda b, pt: (b, 0, 0)),
                       stat_spec, stat_spec, stat_spec],
        ),
        out_shape=[
            jax.ShapeDtypeStruct((dec_batch, SUBLANES, past), F32),
            jax.ShapeDtypeStruct((dec_batch, SUBLANES, LANES), F32),
            jax.ShapeDtypeStruct((dec_batch, SUBLANES, LANES), F32),
            jax.ShapeDtypeStruct((dec_batch, SUBLANES, LANES), I32),
        ],
        compiler_params=_cparams(("arbitrary",)),
        name="dsa_sample_score",
    )(page_table, *([cache_idx_t] * n_pages), qi, wi, ki_new)

    ck = cache_k_l.reshape(n_pool, PAGE_SIZE * B_KV_HEADS, B_HEAD_DIM)
    cv = cache_v_l.reshape(n_pool, PAGE_SIZE * B_KV_HEADS, B_HEAD_DIM)
    kv_specs = [pl.BlockSpec((None, PAGE_SIZE * B_KV_HEADS, B_HEAD_DIM), page_map(r))
                for r in range(PAGES_PER_STEP)]
    o = pl.pallas_call(
        functools.partial(_dsa_sample_attend_kernel, dec_seq=dec_seq, n_steps=n_steps),
        grid_spec=pltpu.PrefetchScalarGridSpec(
            num_scalar_prefetch=1,
            grid=(dec_batch, n_steps),
            in_specs=kv_specs + kv_specs + [
                pl.BlockSpec((None, B_KV_HEADS, rows, B_HEAD_DIM), lambda b, s, pt: (b, 0, 0, 0)),
                pl.BlockSpec((None, SUBLANES, chunk), lambda b, s, pt: (b, 0, s)),
                pl.BlockSpec((None, SUBLANES, LANES), lambda b, s, pt: (b, 0, 0)),
                pl.BlockSpec((None, SUBLANES, LANES), lambda b, s, pt: (b, 0, 0)),
                pl.BlockSpec((None, SUBLANES, LANES), lambda b, s, pt: (b, 0, 0)),
                pl.BlockSpec((None, LANES, 2 * KV_WIDTH), lambda b, s, pt: (b, 0, 0)),
                pl.BlockSpec(sbias.shape, lambda b, s, pt: (0, 0, 0)),
                pl.BlockSpec(sconst.shape, lambda b, s, pt: (0, 0, 0)),
            ],
            out_specs=pl.BlockSpec((None, B_KV_HEADS, rows, B_HEAD_DIM), lambda b, s, pt: (b, 0, 0, 0)),
            scratch_shapes=[
                pltpu.VMEM((B_KV_HEADS, rows, 1), F32),
                pltpu.VMEM((B_KV_HEADS, rows, 1), F32),
                pltpu.VMEM((B_KV_HEADS, rows, B_HEAD_DIM), F32),
            ],
        ),
        out_shape=jax.ShapeDtypeStruct((dec_batch, B_KV_HEADS, rows, B_HEAD_DIM), BF16),
        compiler_params=_cparams(("arbitrary", "arbitrary")),
        name="dsa_sample_attend",
    )(page_table, *([ck] * PAGES_PER_STEP), *([cv] * PAGES_PER_STEP), q, sc, scn, selv, selm, kv_new,
      sbias, sconst)
    o = o.reshape(dec_batch, B_KV_HEADS, B_GQA, SUBLANES, B_HEAD_DIM)[:, :, :, :dec_seq].transpose(0, 3, 1, 2, 4)
    return o.reshape(dec_batch * dec_seq, B_WIDTH)


def _mixout_kernel(u_ref, vn_ref, b_ref, ws_ref, ab_ref, x_ref, w_ref, o_ref, mix_scr, *, tm):
    j = pl.program_id(1)

    @pl.when(j == 0)
    def _():
        row = lax.broadcasted_iota(I32, (A_CHUNK, A_CHUNK), 0)
        col = lax.broadcasted_iota(I32, (A_CHUNK, A_CHUNK), 1)
        for g in range(A_GROUPS):
            wsg = jnp.where(row >= col, ws_ref[g], 0.0).astype(BF16)
            bg = ab_ref[:, g:g + 1]
            cs = slice(g * A_CH, (g + 1) * A_CH)
            for r in range(tm // A_CHUNK):
                rs = slice(r * A_CHUNK, (r + 1) * A_CHUNK)
                sm = jnp.dot(wsg, vn_ref[rs, cs], preferred_element_type=F32) + bg
                mix_scr[rs, cs] = (u_ref[rs, cs].astype(F32) * sm).astype(BF16)
        mix_scr[:, A_WIDTH:] = b_ref[...]

    o_ref[...] = x_ref[...] + jnp.dot(mix_scr[...], w_ref[...].astype(BF16), preferred_element_type=F32)


def _mixout(p, b_out, ws_eff, ab_eff, x2d, w_out_l, *, tm, tn=512):
    n = x2d.shape[0]
    return pl.pallas_call(
        functools.partial(_mixout_kernel, tm=tm),
        grid=(n // tm, D_MODEL // tn),
        in_specs=[
            pl.BlockSpec((tm, A_WIDTH), lambda i, j: (i, COL_U // A_WIDTH)),
            pl.BlockSpec((tm, A_WIDTH), lambda i, j: (i, COL_V // A_WIDTH)),
            pl.BlockSpec((tm, B_WIDTH), lambda i, j: (i, 0)),
            pl.BlockSpec((A_GROUPS, A_CHUNK, A_CHUNK), lambda i, j: (0, 0, 0)),
            pl.BlockSpec((A_CHUNK, A_GROUPS), lambda i, j: (0, 0)),
            pl.BlockSpec((tm, tn), lambda i, j: (i, j)),
            pl.BlockSpec((A_WIDTH + B_WIDTH, tn), lambda i, j: (0, j)),
        ],
        out_specs=pl.BlockSpec((tm, tn), lambda i, j: (i, j)),
        out_shape=jax.ShapeDtypeStruct((n, D_MODEL), F32),
        scratch_shapes=[pltpu.VMEM((tm, A_WIDTH + B_WIDTH), BF16)],
        compiler_params=_cparams(("arbitrary", "arbitrary")),
        name="mixout",
    )(p, p, b_out, ws_eff, ab_eff, x2d, w_out_l)


def _memkv_kernel(x_ref, wk_ref, wv_ref, kn_ref, mk_ref, mv_ref):
    x = x_ref[...].astype(BF16)
    tm = x.shape[0]
    mk = _head_norm(jnp.dot(x, wk_ref[...].astype(BF16), preferred_element_type=F32), kn_ref[...])
    mv = jnp.dot(x, wv_ref[...].astype(BF16), preferred_element_type=F32)
    for hd in range(MEM_HEADS):
        cs = slice(hd * MEM_HEAD_DIM, (hd + 1) * MEM_HEAD_DIM)
        mk_ref[pl.ds(hd, tm, stride=MEM_HEADS), :] = mk[:, cs]
        mv_ref[pl.ds(hd, tm, stride=MEM_HEADS), :] = mv[:, cs]


def _memkv(mem2d, wk, wv, knorm, *, tm=256):
    n = mem2d.shape[0]
    kn = jnp.tile(knorm, MEM_HEADS).reshape(1, MEM_WIDTH)
    return pl.pallas_call(
        _memkv_kernel,
        grid=(n // tm,),
        in_specs=[
            pl.BlockSpec((tm, D_MODEL), lambda i: (i, 0)),
            pl.BlockSpec((D_MODEL, MEM_WIDTH), lambda i: (0, 0)),
            pl.BlockSpec((D_MODEL, MEM_WIDTH), lambda i: (0, 0)),
            pl.BlockSpec((1, MEM_WIDTH), lambda i: (0, 0)),
        ],
        out_specs=[pl.BlockSpec((tm * MEM_HEADS, MEM_HEAD_DIM), lambda i: (i, 0))] * 2,
        out_shape=[jax.ShapeDtypeStruct((n * MEM_HEADS, MEM_HEAD_DIM), F32)] * 2,
        compiler_params=_cparams(("arbitrary",)),
        name="memkv",
    )(mem2d, wk, wv, kn)


def _memattn_kernel(h_ref, g_ref, wq_ref, qn_ref, mk_ref, mv_ref, wo_ref, o_ref, *, rows_per_mem, mem_len):
    h = h_ref[...]
    n = _row_norm(h, g_ref[...]).astype(BF16)
    q = _head_norm(jnp.dot(n, wq_ref[...].astype(BF16), preferred_element_type=F32), qn_ref[...]).astype(BF16)
    tm, m = h.shape[0], mk_ref.shape[0] // MEM_HEADS
    if rows_per_mem is not None:
        row = lax.broadcasted_iota(I32, (tm, m), 0)
        col = lax.broadcasted_iota(I32, (tm, m), 1)
        mask = jnp.where(row // rows_per_mem == col // mem_len, 0.0, -jnp.inf)
    outs = []
    for hd in range(MEM_HEADS):
        cs = slice(hd * MEM_HEAD_DIM, (hd + 1) * MEM_HEAD_DIM)
        mk = mk_ref[pl.ds(hd, m, stride=MEM_HEADS), :].astype(BF16)
        mv = mv_ref[pl.ds(hd, m, stride=MEM_HEADS), :].astype(BF16)
        lg = _dot_nt(q[:, cs], mk) * (MEM_HEAD_DIM ** -0.5)
        if rows_per_mem is not None:
            lg = lg + mask
        e = jnp.exp(lg - jnp.max(lg, axis=-1, keepdims=True))
        l = jnp.sum(e, axis=-1, keepdims=True)
        outs.append(jnp.dot(e.astype(BF16), mv, preferred_element_type=F32) / l)
    o = jnp.concatenate(outs, axis=-1).astype(BF16)
    o_ref[...] = h + jnp.dot(o, wo_ref[...].astype(BF16), preferred_element_type=F32)


def _memattn(h2d, gain, wq, qnorm, mk, mv, wo, *, tm, mem_rows, tiles_per_mem, rows_per_mem, mem_len):
    n = h2d.shape[0]
    qn = jnp.tile(qnorm, MEM_HEADS).reshape(1, MEM_WIDTH)
    return pl.pallas_call(
        functools.partial(_memattn_kernel, rows_per_mem=rows_per_mem, mem_len=mem_len),
        grid=(n // tm,),
        in_specs=[
            pl.BlockSpec((tm, D_MODEL), lambda i: (i, 0)),
            pl.BlockSpec((1, D_MODEL), lambda i: (0, 0)),
            pl.BlockSpec((D_MODEL, MEM_WIDTH), lambda i: (0, 0)),
            pl.BlockSpec((1, MEM_WIDTH), lambda i: (0, 0)),
            pl.BlockSpec((mem_rows * MEM_HEADS, MEM_HEAD_DIM), lambda i: (i // tiles_per_mem, 0)),
            pl.BlockSpec((mem_rows * MEM_HEADS, MEM_HEAD_DIM), lambda i: (i // tiles_per_mem, 0)),
            pl.BlockSpec((MEM_WIDTH, D_MODEL), lambda i: (0, 0)),
        ],
        out_specs=pl.BlockSpec((tm, D_MODEL), lambda i: (i, 0)),
        out_shape=jax.ShapeDtypeStruct((n, D_MODEL), F32),
        compiler_params=_cparams(("arbitrary",)),
        name="memattn",
    )(h2d, gain.reshape(1, D_MODEL), wq, qn, mk, mv, wo)


def _router_kernel(h_ref, g_ref, w_ref, b_ref, n_ref, c_ref):
    n = _row_norm(h_ref[...], g_ref[...]).astype(BF16)
    n_ref[...] = n
    lg = jnp.dot(n, w_ref[...].astype(BF16), preferred_element_type=F32) + b_ref[...]
    gl, el = lg[:, :LANES], lg[:, LANES:]
    lane = lax.broadcasted_iota(I32, gl.shape, 1)
    lanef = lane.astype(F32)

    def first_max_lane(v, vmax):
        return jnp.min(jnp.where(v == vmax, lanef, float(LANES)), axis=-1, keepdims=True).astype(I32)

    gl = jnp.where(lane < N_GROUPS, gl, -jnp.inf)
    ge = jnp.exp(gl - jnp.max(gl, axis=-1, keepdims=True))
    gp = ge / jnp.sum(ge, axis=-1, keepdims=True)
    g_gate = jnp.max(gp, axis=-1, keepdims=True)
    g_sel = first_max_lane(gp, g_gate)
    in_grp = (lane // EXPERTS_PER_GROUP == g_sel) & (lane < N_EXPERTS)
    el = jnp.where(in_grp, el, -jnp.inf)
    ee = jnp.exp(el - jnp.max(el, axis=-1, keepdims=True))
    ep = jnp.where(in_grp, ee / jnp.sum(ee, axis=-1, keepdims=True), -jnp.inf)
    w1 = jnp.max(ep, axis=-1, keepdims=True)
    i1 = first_max_lane(ep, w1)
    ep2 = jnp.where(lane == i1, -jnp.inf, ep)
    w2 = jnp.max(ep2, axis=-1, keepdims=True)
    i2 = first_max_lane(ep2, w2)
    tot = w1 + w2
    comb = jnp.where(lane == i1, w1 / tot * g_gate, jnp.where(lane == i2, w2 / tot * g_gate, 0.0))
    c_ref[...] = jnp.where(lane == GSEL_LANE, g_sel.astype(F32), comb)


def _router(h2d, gain, w_cat, b_cat, *, tm):
    n = h2d.shape[0]
    return pl.pallas_call(
        _router_kernel,
        grid=(n // tm,),
        in_specs=[
            pl.BlockSpec((tm, D_MODEL), lambda i: (i, 0)),
            pl.BlockSpec((1, D_MODEL), lambda i: (0, 0)),
            pl.BlockSpec((D_MODEL, 2 * LANES), lambda i: (0, 0)),
            pl.BlockSpec((1, 2 * LANES), lambda i: (0, 0)),
        ],
        out_specs=[pl.BlockSpec((tm, D_MODEL), lambda i: (i, 0)), pl.BlockSpec((tm, LANES), lambda i: (i, 0))],
        out_shape=[jax.ShapeDtypeStruct((n, D_MODEL), BF16), jax.ShapeDtypeStruct((n, LANES), F32)],
        compiler_params=_cparams(("arbitrary",)),
        name="router",
    )(h2d, gain.reshape(1, D_MODEL), w_cat, b_cat)


def _cast_kernel(x_ref, o_ref):
    o_ref[...] = x_ref[...].astype(BF16)


def _cast_bf16(w):
    ne, r, c = w.shape
    return pl.pallas_call(
        _cast_kernel,
        grid=(ne,),
        in_specs=[pl.BlockSpec((None, r, c), lambda e: (e, 0, 0))],
        out_specs=pl.BlockSpec((None, r, c), lambda e: (e, 0, 0)),
        out_shape=jax.ShapeDtypeStruct(w.shape, BF16),
        compiler_params=_cparams(("arbitrary",)),
        name="cast_bf16",
    )(w)


GSEL_LANE = N_EXPERTS
MOE_ROWS = 128
MOE_FF_SPLIT = 1


def _moe_routed_kernel(n_ref, c_ref, h_hbm, wg_ref, wu_ref, wd_ref, o_ref,
                       tri_scr, key_scr, xc_scr, yc_scr, cw_scr, nch_ref, sem, *, tm):
    i, g, e, f = (pl.program_id(k) for k in range(4))
    main_rows = tm // N_GROUPS
    first_e = (e == 0) & (f == 0)
    last_e = (e == EXPERTS_PER_GROUP - 1) & (f == MOE_FF_SPLIT - 1)

    def onehot_rows(start, rows):
        want = (start + lax.broadcasted_iota(I32, (rows, 1), 0)).astype(F32)
        return jnp.where(key_scr[pl.ds(g, 1), :] == want, 1.0, 0.0).astype(BF16)

    def tail_start(c):
        return pl.multiple_of(main_rows + c * MOE_ROWS, MOE_ROWS)

    @pl.when((i == 0) & (g == 0) & first_e)
    def _():
        r = lax.broadcasted_iota(I32, (tm, tm), 0)
        c = lax.broadcasted_iota(I32, (tm, tm), 1)
        tri_scr[...] = jnp.where(r < c, 1.0, 0.0).astype(BF16)

    @pl.when((g == 0) & first_e)
    def _():
        cp = pltpu.make_async_copy(h_hbm.at[pl.ds(pl.multiple_of(i * tm, tm), tm)], o_ref, sem)
        cp.start()
        lane = lax.broadcasted_iota(I32, (tm, LANES), 1).astype(F32)
        onehot = jnp.where(lane == c_ref[:, GSEL_LANE:GSEL_LANE + 1], 1.0, 0.0).astype(BF16)
        er = lax.broadcasted_iota(I32, (SUBLANES, LANES), 0)
        ec = lax.broadcasted_iota(I32, (SUBLANES, LANES), 1)
        eye = jnp.where(er == ec, 1.0, 0.0).astype(BF16)
        mask_t = _dot_nt(eye, onehot)
        rank_t = jnp.dot(mask_t.astype(BF16), tri_scr[...], preferred_element_type=F32)
        key_scr[...] = jnp.where(mask_t > 0.5, rank_t, -1.0)
        for gg in range(N_GROUPS):
            cnt = jnp.sum(mask_t[gg:gg + 1, :], axis=-1, keepdims=True)
            extra = jnp.ceil(jnp.maximum(cnt - main_rows, 0.0) * (1.0 / MOE_ROWS))
            nch_ref[gg] = jnp.sum(extra).astype(I32)
        cp.wait()

    ntail = nch_ref[g]

    @pl.when(first_e)
    def _():
        comb = c_ref[...]
        hi = comb.astype(BF16)
        r1 = comb - hi.astype(F32)
        mid = r1.astype(BF16)
        lo = (r1 - mid.astype(F32)).astype(BF16)

        def compact(start, rows):
            s = onehot_rows(start, rows)
            xc_scr[pl.ds(start, rows), :] = jnp.dot(s, n_ref[...], preferred_element_type=F32).astype(BF16)
            cw_scr[pl.ds(start, rows), :] = (jnp.dot(s, hi, preferred_element_type=F32)
                                             + jnp.dot(s, mid, preferred_element_type=F32)
                                             + jnp.dot(s, lo, preferred_element_type=F32))
            yc_scr[pl.ds(start, rows), :] = jnp.zeros((rows, D_MODEL), F32)

        compact(0, main_rows)
        lax.fori_loop(0, ntail, lambda c, carry: (compact(tail_start(c), MOE_ROWS), carry)[1], 0)

    wg = wg_ref[...].astype(BF16)
    wu = wu_ref[...].astype(BF16)
    wd = wd_ref[...].astype(BF16)
    ex = g * EXPERTS_PER_GROUP + e

    def expert(start, rows):
        x = xc_scr[pl.ds(start, rows), :]
        a = jnp.dot(x, wg, preferred_element_type=F32)
        u = jnp.dot(x, wu, preferred_element_type=F32)
        lane = lax.broadcasted_iota(I32, (rows, LANES), 1)
        cw = jnp.sum(jnp.where(lane == ex, cw_scr[pl.ds(start, rows), :], 0.0), axis=-1, keepdims=True)
        hid = (a * jax.nn.sigmoid(a) * u * cw).astype(BF16)
        yc_scr[pl.ds(start, rows), :] += jnp.dot(hid, wd, preferred_element_type=F32)

    expert(0, main_rows)
    lax.fori_loop(0, ntail, lambda c, carry: (expert(tail_start(c), MOE_ROWS), carry)[1], 0)

    @pl.when(last_e)
    def _():
        def scatter(start, rows):
            s = onehot_rows(start, rows)
            y = yc_scr[pl.ds(start, rows), :].astype(BF16)
            o_ref[...] += lax.dot_general(s, y, (((0,), (0,)), ((), ())), preferred_element_type=F32)

        scatter(0, main_rows)
        lax.fori_loop(0, ntail, lambda c, carry: (scatter(tail_start(c), MOE_ROWS), carry)[1], 0)


def _moe_routed(n3, comb, h2d, w_gate_l, w_up_l, w_down_l, *, tm):
    n = h2d.shape[0]
    ff = EXPERT_FF // MOE_FF_SPLIT

    def wmap(i, g, e, f):
        return (g * EXPERTS_PER_GROUP + e, 0, f)

    return pl.pallas_call(
        functools.partial(_moe_routed_kernel, tm=tm),
        grid=(n // tm, N_GROUPS, EXPERTS_PER_GROUP, MOE_FF_SPLIT),
        in_specs=[
            pl.BlockSpec((tm, D_MODEL), lambda i, g, e, f: (i, 0), pipeline_mode=pl.Buffered(1)),
            pl.BlockSpec((tm, LANES), lambda i, g, e, f: (i, 0), pipeline_mode=pl.Buffered(1)),
            pl.BlockSpec(memory_space=pl.ANY),
            pl.BlockSpec((None, D_MODEL, ff), wmap),
            pl.BlockSpec((None, D_MODEL, ff), wmap),
            pl.BlockSpec((None, ff, D_MODEL), lambda i, g, e, f: (g * EXPERTS_PER_GROUP + e, f, 0)),
        ],
        out_specs=pl.BlockSpec((tm, D_MODEL), lambda i, g, e, f: (i, 0)),
        out_shape=jax.ShapeDtypeStruct((n, D_MODEL), F32),
        scratch_shapes=[
            pltpu.VMEM((tm, tm), BF16),
            pltpu.VMEM((SUBLANES, tm), F32),
            pltpu.VMEM((tm, D_MODEL), BF16),
            pltpu.VMEM((tm, D_MODEL), F32),
            pltpu.VMEM((tm, LANES), F32),
            pltpu.SMEM((N_GROUPS,), I32),
            pltpu.SemaphoreType.DMA(()),
        ],
        compiler_params=pltpu.CompilerParams(
            dimension_semantics=("arbitrary",) * 4, vmem_limit_bytes=MOE_VMEM_LIMIT),
        name="moe_routed",
    )(n3, comb, h2d, w_gate_l, w_up_l, w_down_l)


def _moe_kernel(n_ref, c_ref, h_ref, wg_ref, wu_ref, wd_ref, o_ref):
    e = pl.program_id(1)

    @pl.when(e == 0)
    def _():
        o_ref[...] = h_ref[...]

    x = n_ref[...]
    a = jnp.dot(x, wg_ref[...].astype(BF16), preferred_element_type=F32)
    u = jnp.dot(x, wu_ref[...].astype(BF16), preferred_element_type=F32)
    lane = lax.broadcasted_iota(I32, c_ref.shape, 1)
    c = jnp.sum(jnp.where(lane == e, c_ref[...], 0.0), axis=-1, keepdims=True)
    hid = (a * jax.nn.sigmoid(a) * u * c).astype(BF16)
    o_ref[...] += jnp.dot(hid, wd_ref[...].astype(BF16), preferred_element_type=F32)


def _moe(n3, comb, h2d, w_gate_l, w_up_l, w_down_l, *, tm):
    n = h2d.shape[0]
    return pl.pallas_call(
        _moe_kernel,
        grid=(n // tm, N_EXPERTS),
        in_specs=[
            pl.BlockSpec((tm, D_MODEL), lambda i, e: (i, 0)),
            pl.BlockSpec((tm, LANES), lambda i, e: (i, 0)),
            pl.BlockSpec((tm, D_MODEL), lambda i, e: (i, 0)),
            pl.BlockSpec((None, D_MODEL, EXPERT_FF), lambda i, e: (e, 0, 0)),
            pl.BlockSpec((None, D_MODEL, EXPERT_FF), lambda i, e: (e, 0, 0)),
            pl.BlockSpec((None, EXPERT_FF, D_MODEL), lambda i, e: (e, 0, 0)),
        ],
        out_specs=pl.BlockSpec((tm, D_MODEL), lambda i, e: (i, 0)),
        out_shape=jax.ShapeDtypeStruct((n, D_MODEL), F32),
        compiler_params=_cparams(("arbitrary", "arbitrary")),
        name="moe",
    )(n3, comb, h2d, w_gate_l, w_up_l, w_down_l)


def _post_mixer(p, b_out, ws_eff, ab_eff, x2d, mk, mv, lw, *, tm_mix, tm_mem, mem_rows, tiles_per_mem,
                rows_per_mem, mem_len, tm_route, tm_moe):
    h1 = _mixout(p, b_out, ws_eff, ab_eff, x2d, lw["w_out"], tm=tm_mix)
    h2 = _memattn(h1, lw["norm_mem"], lw["mem_wq"], lw["mem_qnorm"], mk, mv, lw["mem_wo"], tm=tm_mem,
                  mem_rows=mem_rows, tiles_per_mem=tiles_per_mem, rows_per_mem=rows_per_mem, mem_len=mem_len)
    n3, comb = _router(h2, lw["norm_ffn"], lw["w_cat"], lw["b_cat"], tm=tm_route)
    moe = _moe_routed if tm_moe > N_GROUPS * MOE_ROWS else _moe
    return moe(n3, comb, h2, lw["w_gate"], lw["w_up"], lw["w_down"], tm=tm_moe)


def kernel(x_prompt, x_sample, cache_k, cache_v, cache_idx_k, cache_mem_k, cache_mem_v, page_table,
           mem_prompt, norm_mix, w_in, a_vnorm, a_ws, a_b, b_qnorm, b_knorm, rel_bias, w_out,
           norm_mem, mem_wq, mem_wk, mem_wv, mem_qnorm, mem_knorm, mem_wo, norm_ffn,
           w_group, b_group, w_router, b_router, w_gate, w_up, w_down):
    bp, t, d = x_prompt.shape
    bd, s, _ = x_sample.shape
    depth = w_in.shape[0]
    mem_len = mem_prompt.shape[1]
    assert d == D_MODEL and t % KEY_CHUNK == 0 and s <= 8 and (bd * s) % LANES == 0

    bias_pairs, sbias, sconst = _bias_tables(rel_bias)
    hp = x_prompt.reshape(bp * t, d)
    hs = x_sample.reshape(bd * s, d)
    outs = [[] for _ in range(9)]
    for l in range(depth):
        w_cat = jnp.zeros((d, 2 * LANES), F32)
        w_cat = w_cat.at[:, :N_GROUPS].set(w_group[l]).at[:, LANES:LANES + N_EXPERTS].set(w_router[l])
        b_cat = jnp.zeros((1, 2 * LANES), F32)
        b_cat = b_cat.at[0, :N_GROUPS].set(b_group[l]).at[0, LANES:LANES + N_EXPERTS].set(b_router[l])
        lw = dict(w_out=w_out[l], norm_mem=norm_mem[l], mem_wq=mem_wq[l], mem_qnorm=mem_qnorm[l],
                  mem_wo=mem_wo[l], norm_ffn=norm_ffn[l], w_cat=w_cat, b_cat=b_cat,
                  w_gate=_cast_bf16(w_gate[l]), w_up=_cast_bf16(w_up[l]), w_down=_cast_bf16(w_down[l]))

        p, k, v, tail = _proj(hp, norm_mix[l], w_in[l], a_vnorm[l], b_qnorm[l], b_knorm[l],
                              tm=1024, want_vn=False)
        b_out = _dsa_prompt(p, tail, bias_pairs, batch=bp, seq=t)
        mk, mv = _memkv(mem_prompt.reshape(bp * mem_len, d), mem_wk[l], mem_wv[l], mem_knorm[l])
        hp = _post_mixer(p, b_out, a_ws[l], a_b[l].T, hp, mk, mv, lw, tm_mix=1024, tm_mem=512,
                         mem_rows=mem_len, tiles_per_mem=t // 512, rows_per_mem=None, mem_len=mem_len,
                         tm_route=512, tm_moe=1024)
        outs[0].append(k.reshape(bp, t, B_KV_HEADS, B_HEAD_DIM))
        outs[1].append(v.reshape(bp, t, B_KV_HEADS, B_HEAD_DIM))
        outs[2].append(tail[:, :IDX_DIM].reshape(bp, t, IDX_DIM))
        outs[3].append(mk.reshape(bp, mem_len, MEM_HEADS, MEM_HEAD_DIM))
        outs[4].append(mv.reshape(bp, mem_len, MEM_HEADS, MEM_HEAD_DIM))

        ns = bd * s
        p_s, k_s, v_s, tail_s, vn_s = _proj(hs, norm_mix[l], w_in[l], a_vnorm[l], b_qnorm[l], b_knorm[l],
                                            tm=ns, want_vn=True)
        b_out_s = _dsa_sample(p_s, tail_s, k_s, v_s, cache_k[l], cache_v[l], cache_idx_k[l], page_table,
                              sbias, sconst, dec_batch=bd, dec_seq=s)
        ws_s = jnp.einsum("ab,gts->gatbs", jnp.eye(ns // s, dtype=F32), a_ws[l][:, :s, :s])
        ws_s = ws_s.reshape(A_GROUPS, ns, ns)
        ab_s = jnp.tile(a_b[l][:, :s].T, (ns // s, 1))
        mem_tile = 32
        hs = _post_mixer(p_s, b_out_s, ws_s, ab_s, hs,
                         cache_mem_k[l].reshape(bd * mem_len * MEM_HEADS, MEM_HEAD_DIM),
                         cache_mem_v[l].reshape(bd * mem_len * MEM_HEADS, MEM_HEAD_DIM), lw,
                         tm_mix=ns, tm_mem=mem_tile, mem_rows=mem_tile // s * mem_len, tiles_per_mem=1,
                         rows_per_mem=s, mem_len=mem_len, tm_route=ns, tm_moe=ns)
        outs[5].append(k_s.reshape(bd, s, B_KV_HEADS, B_HEAD_DIM))
        outs[6].append(v_s.reshape(bd, s, B_KV_HEADS, B_HEAD_DIM))
        outs[7].append(tail_s[:, :IDX_DIM].reshape(bd, s, IDX_DIM))
        outs[8].append(vn_s.reshape(bd, s, A_GROUPS, A_CH))
    return (hp.reshape(bp, t, d), hs.reshape(bd, s, d)) + tuple(jnp.stack(o) for o in outs)
```

```python
import functools
import math

import jax
import jax.numpy as jnp
from jax import lax
from jax.experimental import pallas as pl
from jax.experimental.pallas import tpu as pltpu

BF16 = jnp.bfloat16
F32 = jnp.float32
I32 = jnp.int32

D_MODEL = 2048
A_GROUPS = 8
A_CH = 128
A_WIDTH = A_GROUPS * A_CH
A_CHUNK = 128
B_HEADS = 8
B_HEAD_DIM = 128
B_KV_HEADS = 2
B_GQA = B_HEADS // B_KV_HEADS
B_WIDTH = B_HEADS * B_HEAD_DIM
KV_WIDTH = B_KV_HEADS * B_HEAD_DIM
IDX_HEADS = 16
IDX_DIM = 64
TOPK_MAX = 256
Q_BLOCK = 128
REL_BUCKETS = 32
REL_MAX_DIST = 128
MEM_HEADS = 4
MEM_HEAD_DIM = 128
MEM_WIDTH = MEM_HEADS * MEM_HEAD_DIM
N_GROUPS = 4
EXPERTS_PER_GROUP = 4
N_EXPERTS = N_GROUPS * EXPERTS_PER_GROUP
EXPERT_FF = 512
PAGE_SIZE = 128
EPS = 1e-6

COL_U = 0
COL_V = COL_U + A_WIDTH
COL_Q = COL_V + A_WIDTH
COL_K = COL_Q + B_WIDTH
COL_VV = COL_K + KV_WIDTH
COL_QI = COL_VV + KV_WIDTH
COL_MAIN = COL_QI + IDX_HEADS * IDX_DIM
TAIL_COLS = IDX_DIM + IDX_HEADS
LANES = 128
SUBLANES = 8
PROJ_TN = 512
KEY_CHUNK = 256
NEG_INIT = -1e30
VMEM_LIMIT = 56 * 1024 * 1024
MOE_VMEM_LIMIT = 60 * 1024 * 1024


def _cparams(sem):
    return pltpu.CompilerParams(dimension_semantics=sem, vmem_limit_bytes=VMEM_LIMIT)


def _gelu(x):
    return 0.5 * x * (1.0 + lax.erf(x * (2.0 ** -0.5)))


def _head_norm(a, gain, width=LANES):
    outs = []
    for c in range(a.shape[1] // width):
        blk = a[:, c * width:(c + 1) * width]
        ms = jnp.mean(blk * blk, axis=-1, keepdims=True)
        outs.append(blk * lax.rsqrt(ms + EPS) * gain[:, c * width:(c + 1) * width])
    return outs[0] if len(outs) == 1 else jnp.concatenate(outs, axis=-1)


def _row_norm(x, gain):
    ms = jnp.mean(x * x, axis=-1, keepdims=True)
    return x * lax.rsqrt(ms + EPS) * gain


def _dot_nt(a, b):
    return lax.dot_general(a, b, (((1,), (1,)), ((), ())), preferred_element_type=F32)


def _proj_kernel(x_ref, g_ref, w_ref, wt_ref, avn_ref, qn_ref, kn_ref, *rest, want_vn):
    if want_vn:
        p_ref, k_ref, v_ref, tail_ref, vn_ref, xn_scr = rest
    else:
        p_ref, k_ref, v_ref, tail_ref, xn_scr = rest
        vn_ref = None
    j = pl.program_id(1)
    ju = COL_V // PROJ_TN
    jv = COL_Q // PROJ_TN
    jq = COL_K // PROJ_TN
    jkv = COL_QI // PROJ_TN

    @pl.when(j == 0)
    def _():
        xn = _row_norm(x_ref[...], g_ref[...]).astype(BF16)
        xn_scr[...] = xn
        wlane = lax.broadcasted_iota(I32, wt_ref.shape, 1)
        wt = jnp.where(wlane < TAIL_COLS, wt_ref[...], 0.0).astype(BF16)
        t = jnp.dot(xn, wt, preferred_element_type=F32)
        lane = lax.broadcasted_iota(I32, t.shape, 1)
        tail_ref[...] = jnp.where(lane >= IDX_DIM, t * (IDX_HEADS ** -0.5), t)

    acc = jnp.dot(xn_scr[...], w_ref[...].astype(BF16), preferred_element_type=F32)

    @pl.when(j < ju)
    def _():
        p_ref[...] = _gelu(acc).astype(BF16)

    @pl.when((j >= ju) & (j < jv))
    def _():
        vn = _head_norm(_gelu(acc), avn_ref[...])
        p_ref[...] = vn.astype(BF16)
        if vn_ref is not None:
            vn_ref[...] = vn

    @pl.when((j >= jv) & (j < jq))
    def _():
        p_ref[...] = _head_norm(acc, qn_ref[...]).astype(BF16)

    @pl.when(j == jq)
    def _():
        k = _head_norm(acc[:, :KV_WIDTH], kn_ref[...])
        v = acc[:, KV_WIDTH:]
        k_ref[...] = k
        v_ref[...] = v
        p_ref[...] = jnp.concatenate([k, v], axis=-1).astype(BF16)

    @pl.when(j >= jkv)
    def _():
        p_ref[...] = acc.astype(BF16)


def _proj(x2d, gain, w_in_l, a_vnorm_l, b_qnorm_l, b_knorm_l, *, tm, want_vn):
    n = x2d.shape[0]
    nj = COL_MAIN // PROJ_TN
    avn = a_vnorm_l.reshape(1, A_WIDTH)
    qn = jnp.tile(b_qnorm_l, PROJ_TN // B_HEAD_DIM).reshape(1, PROJ_TN)
    kn = jnp.tile(b_knorm_l, B_KV_HEADS).reshape(1, KV_WIDTH)
    ju = COL_V // PROJ_TN
    nv = A_WIDTH // PROJ_TN

    def vmap_(i, j):
        return (0, jnp.clip(j - ju, 0, nv - 1))

    in_specs = [
        pl.BlockSpec((tm, D_MODEL), lambda i, j: (i, 0)),
        pl.BlockSpec((1, D_MODEL), lambda i, j: (0, 0)),
        pl.BlockSpec((D_MODEL, PROJ_TN), lambda i, j: (0, j)),
        pl.BlockSpec((D_MODEL, LANES), lambda i, j: (0, COL_MAIN // LANES)),
        pl.BlockSpec((1, PROJ_TN), vmap_),
        pl.BlockSpec((1, PROJ_TN), lambda i, j: (0, 0)),
        pl.BlockSpec((1, KV_WIDTH), lambda i, j: (0, 0)),
    ]
    out_shape = [
        jax.ShapeDtypeStruct((n, COL_MAIN), BF16),
        jax.ShapeDtypeStruct((n, KV_WIDTH), F32),
        jax.ShapeDtypeStruct((n, KV_WIDTH), F32),
        jax.ShapeDtypeStruct((n, LANES), F32),
    ]
    out_specs = [
        pl.BlockSpec((tm, PROJ_TN), lambda i, j: (i, j)),
        pl.BlockSpec((tm, KV_WIDTH), lambda i, j: (i, 0)),
        pl.BlockSpec((tm, KV_WIDTH), lambda i, j: (i, 0)),
        pl.BlockSpec((tm, LANES), lambda i, j: (i, 0)),
    ]
    if want_vn:
        out_shape.append(jax.ShapeDtypeStruct((n, A_WIDTH), F32))
        out_specs.append(pl.BlockSpec((tm, PROJ_TN), lambda i, j: (i, jnp.clip(j - ju, 0, nv - 1))))
    return pl.pallas_call(
        functools.partial(_proj_kernel, want_vn=want_vn),
        grid=(n // tm, nj),
        in_specs=in_specs,
        out_specs=out_specs,
        out_shape=out_shape,
        scratch_shapes=[pltpu.VMEM((tm, D_MODEL), BF16)],
        compiler_params=_cparams(("arbitrary", "arbitrary")),
        name="proj",
    )(x2d, gain.reshape(1, D_MODEL), w_in_l, w_in_l, avn, qn, kn)


def _t5_bucket(d):
    max_exact = REL_BUCKETS // 2
    d = jnp.maximum(d, 0)
    ratio = jnp.log(jnp.maximum(d, 1).astype(F32) / max_exact) / math.log(REL_MAX_DIST / max_exact)
    large = jnp.minimum(max_exact + jnp.floor(ratio * (REL_BUCKETS - max_exact)).astype(I32), REL_BUCKETS - 1)
    return jnp.where(d < max_exact, d, large)


def _bias_lookup(bucket, rb_ref, h):
    acc = jnp.zeros(bucket.shape, F32)
    for r in range(REL_BUCKETS):
        acc = jnp.where(bucket == r, rb_ref[r, h], acc)
    return acc


def _bias_kernel(rb_ref, pair_ref, samp_ref, sconst_ref):
    q = lax.broadcasted_iota(I32, (Q_BLOCK, Q_BLOCK), 0)
    kc = lax.broadcasted_iota(I32, (Q_BLOCK, Q_BLOCK), 1)
    nvar = pair_ref.shape[0]
    rels = range(-(nvar - 1), 2)
    buckets = {r: _t5_bucket(q - kc - r * Q_BLOCK) for r in rels}
    for h in range(B_HEADS):
        g, hh = divmod(h, B_GQA)
        tiles = {r: _bias_lookup(buckets[r], rb_ref, h) for r in rels}
        for var in range(nvar):
            r0 = var - (nvar - 1)
            for t in range(2):
                pair_ref[var, g, hh * Q_BLOCK:(hh + 1) * Q_BLOCK, t * Q_BLOCK:(t + 1) * Q_BLOCK] = tiles[r0 + t]
    rows = B_GQA * SUBLANES
    row = lax.broadcasted_iota(I32, (rows, 2 * PAGE_SIZE), 0)
    col = lax.broadcasted_iota(I32, (rows, 2 * PAGE_SIZE), 1)
    t = row % SUBLANES
    d = jnp.where(col < PAGE_SIZE, t + PAGE_SIZE - col, t - (col - PAGE_SIZE))
    bucket = _t5_bucket(d)
    far = jnp.full((rows, LANES), REL_BUCKETS - 1, I32)
    rowc = lax.broadcasted_iota(I32, (rows, LANES), 0)
    for g in range(B_KV_HEADS):
        acc = jnp.zeros((rows, 2 * PAGE_SIZE), F32)
        accc = jnp.zeros((rows, LANES), F32)
        for hh in range(B_GQA):
            h = g * B_GQA + hh
            acc = jnp.where(row // SUBLANES == hh, _bias_lookup(bucket, rb_ref, h), acc)
            accc = jnp.where(rowc // SUBLANES == hh, _bias_lookup(far, rb_ref, h), accc)
        samp_ref[g] = acc
        sconst_ref[g] = accc


N_PAIR_VARIANTS = 4


def _bias_tables(rel_bias):
    rows = B_GQA * SUBLANES
    return pl.pallas_call(
        _bias_kernel,
        in_specs=[pl.BlockSpec(memory_space=pltpu.SMEM)],
        out_shape=[
            jax.ShapeDtypeStruct((N_PAIR_VARIANTS, B_KV_HEADS, B_GQA * Q_BLOCK, KEY_CHUNK), F32),
            jax.ShapeDtypeStruct((B_KV_HEADS, rows, 2 * PAGE_SIZE), F32),
            jax.ShapeDtypeStruct((B_KV_HEADS, rows, LANES), F32),
        ],
        name="bias_tables",
    )(rel_bias)


def _key_to_f32(key):
    bits = jnp.where(key < 0, key ^ I32(-2 ** 31), ~key)
    return lax.bitcast_convert_type(bits, F32)


def _kth_largest(count_ge, shape, k):
    def body(it, prefix):
        bit = 31 - it
        cand = prefix | lax.shift_left(I32(1), bit)
        cnt = count_ge(_key_to_f32(cand))
        return jnp.where(cnt >= k, cand, prefix)

    prefix = lax.fori_loop(0, 32, body, jnp.zeros(shape, I32))
    thr = _key_to_f32(prefix)
    return jnp.where((prefix & I32(-2 ** 23)) == 0, -jnp.inf, thr)


NO_INDEX_BOUND = 2 ** 30
TIE_REPAIR_ROUNDS = 2


def _topk_select(fold_sum, fold_min, shape, k, index_bits):
    thr = _kth_largest(lambda t: fold_sum(lambda sc, kp: sc >= t), shape, k)
    c_ge = fold_sum(lambda sc, kp: sc >= thr)
    no_bound = jnp.full(shape, NO_INDEX_BOUND, I32)

    def repair(_):
        v = thr
        strict = jnp.zeros(shape, F32)

        def in_set(sc, v, strict):
            return (sc > v) | ((sc == v) & (strict < 0.5))

        for _ in range(TIE_REPAIR_ROUNDS):
            c_set = fold_sum(lambda sc, kp: in_set(sc, v, strict))
            vmin = fold_min(lambda sc, kp: jnp.where(in_set(sc, v, strict), sc, jnp.inf))
            c_gt = fold_sum(lambda sc, kp: sc > vmin)
            drop = (c_set > k) & (c_gt >= k)
            v = jnp.where(drop, vmin, v)
            strict = jnp.where(drop, 1.0, strict)
        c_set = fold_sum(lambda sc, kp: in_set(sc, v, strict))
        vt = fold_min(lambda sc, kp: jnp.where(in_set(sc, v, strict), sc, jnp.inf))
        need = k - fold_sum(lambda sc, kp: sc > vt)

        def idx_body(it, m):
            cand = m | lax.shift_left(I32(1), index_bits - 1 - it)
            below = fold_sum(lambda sc, kp: (sc == vt) & (kp < cand))
            return jnp.where(below < need, cand, m)

        m = lax.fori_loop(0, index_bits, idx_body, jnp.zeros(shape, I32))
        tie = c_set > k
        v_out = jnp.where(tie, vt, v)
        m_out = jnp.where(tie, m, jnp.where(strict > 0.5, -1, no_bound))
        return v_out, m_out

    return lax.cond(jnp.max(c_ge) > k, repair, lambda _: (thr, no_bound), 0)


def _dsa_prompt_kernel(q_ref, kv_ref, qia_ref, qib_ref, tailk_ref, tailq_ref, bias_ref, o_ref,
                       kil_scr, kir_scr, sc_scr, lg_scr, m_scr, l_scr, acc_scr, *, topk):
    i = pl.program_id(1)

    @pl.when(i == 0)
    def _():
        ki = tailk_ref[:, :IDX_DIM].astype(BF16)
        z = jnp.zeros_like(ki)
        kil_scr[...] = jnp.concatenate([ki, z], axis=-1)
        kir_scr[...] = jnp.concatenate([z, ki], axis=-1)

    wi = tailq_ref[:, IDX_DIM:IDX_DIM + IDX_HEADS] * (IDX_DIM ** -0.5)
    nk = (i + 2) // 2
    qpos = i * Q_BLOCK + lax.broadcasted_iota(I32, (Q_BLOCK, 1), 0)
    kloc = lax.broadcasted_iota(I32, (1, KEY_CHUNK), 1)
    heads_per_ref = qia_ref.shape[1] // IDX_DIM

    pairs = []
    for hp in range(IDX_HEADS // 2):
        ref = qia_ref if 2 * hp < heads_per_ref else qib_ref
        base = (2 * hp) % heads_per_ref * IDX_DIM
        pairs.append(ref[:, base:base + 2 * IDX_DIM])
    qstack = jnp.concatenate(pairs, axis=0)

    def score_chunk(c, carry):
        off = pl.multiple_of(c * KEY_CHUNK, KEY_CHUNK)
        acc = jnp.zeros((Q_BLOCK, KEY_CHUNK), F32)
        for side, k_scr in ((0, kil_scr), (1, kir_scr)):
            s = _dot_nt(qstack, k_scr[pl.ds(off, KEY_CHUNK), :])
            for hp in range(IDX_HEADS // 2):
                h = 2 * hp + side
                acc = acc + wi[:, h:h + 1] * jnp.maximum(s[hp * Q_BLOCK:(hp + 1) * Q_BLOCK], 0.0)
        kpos = c * KEY_CHUNK + kloc
        sc_scr[c] = jnp.where(kpos <= qpos, acc, -jnp.inf)
        return carry

    lax.fori_loop(0, nk, score_chunk, 0)

    def fold_sum(pred):
        def body(c, cnt):
            m = jnp.where(pred(sc_scr[c], c * KEY_CHUNK + kloc), 1.0, 0.0)
            return cnt + m[:, :LANES] + m[:, LANES:]
        cnt = lax.fori_loop(0, nk, body, jnp.zeros((Q_BLOCK, LANES), F32))
        return jnp.sum(cnt, axis=-1, keepdims=True)

    def fold_min(val):
        def body(c, acc):
            x = val(sc_scr[c], c * KEY_CHUNK + kloc)
            return jnp.minimum(acc, jnp.minimum(x[:, :LANES], x[:, LANES:]))
        acc = lax.fori_loop(0, nk, body, jnp.full((Q_BLOCK, LANES), jnp.inf, F32))
        return jnp.min(acc, axis=-1, keepdims=True)

    index_bits = (kv_ref.shape[0] - 1).bit_length()
    sel_v, sel_m = _topk_select(fold_sum, fold_min, (Q_BLOCK, 1), topk, index_bits)

    def mask_chunk(c, carry):
        kpos = c * KEY_CHUNK + kloc
        sc = sc_scr[c]
        chosen = (sc > sel_v) | ((sc == sel_v) & (kpos <= sel_m))
        sc_scr[c] = jnp.where(chosen & (kpos <= qpos), 0.0, -jnp.inf)
        return carry

    lax.fori_loop(0, nk, mask_chunk, 0)

    scale = B_HEAD_DIM ** -0.5
    rows = B_GQA * Q_BLOCK
    qgs = [jnp.concatenate(
        [q_ref[:, (g * B_GQA + hh) * B_HEAD_DIM:(g * B_GQA + hh + 1) * B_HEAD_DIM] for hh in range(B_GQA)],
        axis=0) for g in range(B_KV_HEADS)]
    m_scr[...] = jnp.full(m_scr.shape, -jnp.inf, F32)
    l_scr[...] = jnp.zeros(l_scr.shape, F32)
    acc_scr[...] = jnp.zeros(acc_scr.shape, F32)

    def logits_chunk(c, carry):
        off = pl.multiple_of(c * KEY_CHUNK, KEY_CHUNK)
        var = jnp.clip(2 * c - i, -(N_PAIR_VARIANTS - 1), 0) + (N_PAIR_VARIANTS - 1)
        sel = sc_scr[c]
        sel = jnp.concatenate([sel] * B_GQA, axis=0)
        for g in range(B_KV_HEADS):
            kc = kv_ref[pl.ds(off, KEY_CHUNK), g * B_HEAD_DIM:(g + 1) * B_HEAD_DIM]
            lg = _dot_nt(qgs[g], kc) * scale + bias_ref[var, g] + sel
            lg_scr[c, g] = lg
            m_scr[g] = jnp.maximum(m_scr[g], jnp.maximum(lg[:, :LANES], lg[:, LANES:]))
        return carry

    lax.fori_loop(0, nk, logits_chunk, 0)
    for g in range(B_KV_HEADS):
        m = jnp.max(m_scr[g], axis=-1, keepdims=True)
        m_scr[g] = jnp.broadcast_to(m, (rows, LANES))

    def pv_chunk(c, carry):
        off = pl.multiple_of(c * KEY_CHUNK, KEY_CHUNK)
        for g in range(B_KV_HEADS):
            vc = kv_ref[pl.ds(off, KEY_CHUNK), KV_WIDTH + g * B_HEAD_DIM:KV_WIDTH + (g + 1) * B_HEAD_DIM]
            mb = m_scr[g]
            p = jnp.exp(lg_scr[c, g] - jnp.concatenate([mb, mb], axis=-1))
            l_scr[g] += p[:, :LANES] + p[:, LANES:]
            acc_scr[g] += jnp.dot(p.astype(BF16), vc, preferred_element_type=F32)
        return carry

    lax.fori_loop(0, nk, pv_chunk, 0)
    for g in range(B_KV_HEADS):
        o = acc_scr[g] / jnp.sum(l_scr[g], axis=-1, keepdims=True)
        for hh in range(B_GQA):
            h = g * B_GQA + hh
            o_ref[:, h * B_HEAD_DIM:(h + 1) * B_HEAD_DIM] = o[hh * Q_BLOCK:(hh + 1) * Q_BLOCK].astype(BF16)


def _dsa_prompt(p, tail, bias_pairs, *, batch, seq):
    topk = min(TOPK_MAX, seq // 4)
    nblk = seq // Q_BLOCK
    qi_w = PROJ_TN
    return pl.pallas_call(
        functools.partial(_dsa_prompt_kernel, topk=topk),
        grid=(batch, nblk),
        in_specs=[
            pl.BlockSpec((Q_BLOCK, B_WIDTH), lambda b, i: (b * nblk + i, COL_Q // B_WIDTH)),
            pl.BlockSpec((seq, 2 * KV_WIDTH), lambda b, i: (b, COL_K // (2 * KV_WIDTH))),
            pl.BlockSpec((Q_BLOCK, qi_w), lambda b, i: (b * nblk + i, COL_QI // qi_w)),
            pl.BlockSpec((Q_BLOCK, qi_w), lambda b, i: (b * nblk + i, COL_QI // qi_w + 1)),
            pl.BlockSpec((seq, LANES), lambda b, i: (b, 0)),
            pl.BlockSpec((Q_BLOCK, LANES), lambda b, i: (b * nblk + i, 0)),
            pl.BlockSpec(bias_pairs.shape, lambda b, i: (0, 0, 0, 0)),
        ],
        out_specs=pl.BlockSpec((Q_BLOCK, B_WIDTH), lambda b, i: (b * nblk + i, 0)),
        out_shape=jax.ShapeDtypeStruct((batch * seq, B_WIDTH), BF16),
        scratch_shapes=[
            pltpu.VMEM((seq, 2 * IDX_DIM), BF16),
            pltpu.VMEM((seq, 2 * IDX_DIM), BF16),
            pltpu.VMEM((seq // KEY_CHUNK, Q_BLOCK, KEY_CHUNK), F32),
            pltpu.VMEM((seq // KEY_CHUNK, B_KV_HEADS, B_GQA * Q_BLOCK, KEY_CHUNK), F32),
            pltpu.VMEM((B_KV_HEADS, B_GQA * Q_BLOCK, LANES), F32),
            pltpu.VMEM((B_KV_HEADS, B_GQA * Q_BLOCK, LANES), F32),
            pltpu.VMEM((B_KV_HEADS, B_GQA * Q_BLOCK, B_HEAD_DIM), F32),
        ],
        compiler_params=_cparams(("arbitrary", "arbitrary")),
        name="dsa_prompt",
    )(p, p, p, p, tail, tail, bias_pairs)


PAGES_PER_STEP = 16
SAMPLE_SUB_PAGES = 16
SCORE_KEYS_PER_DOT = 2048
FOLD_WAYS = 8


def _dsa_sample_score_kernel(pt_ref, *refs, topk, dec_seq, n_pages):
    del pt_ref
    pages = refs[:n_pages]
    qi_ref, wi_ref, kin_ref, sc_ref, scn_ref, selv_ref, selm_ref = refs[n_pages:]
    past = n_pages * PAGE_SIZE
    pages_per_dot = SCORE_KEYS_PER_DOT // PAGE_SIZE
    qi = qi_ref[...]
    wi = wi_ref[...] * (IDX_DIM ** -0.5)

    def scores(keys_t):
        r = jnp.maximum(jnp.dot(qi, keys_t, preferred_element_type=F32), 0.0) * wi
        acc = r[0:SUBLANES]
        for j in range(1, IDX_HEADS * dec_seq // SUBLANES):
            acc = acc + r[j * SUBLANES:(j + 1) * SUBLANES]
        shift = dec_seq
        while shift < SUBLANES:
            acc = acc + pltpu.roll(acc, shift, axis=0)
            shift *= 2
        return acc

    for d in range(n_pages // pages_per_dot):
        keys_t = jnp.concatenate(
            [pg[...].astype(BF16) for pg in pages[d * pages_per_dot:(d + 1) * pages_per_dot]], axis=1)
        sc_ref[:, d * SCORE_KEYS_PER_DOT:(d + 1) * SCORE_KEYS_PER_DOT] = scores(keys_t)

    row = lax.broadcasted_iota(I32, (SUBLANES, LANES), 0)
    col = lax.broadcasted_iota(I32, (SUBLANES, LANES), 1)
    scn = jnp.where((col <= row % dec_seq) & (col < dec_seq), scores(kin_ref[...]), -jnp.inf)
    scn_ref[...] = scn

    def pieces():
        yield scn, past + col[0:1]
        for w in range(past // LANES):
            yield sc_ref[:, w * LANES:(w + 1) * LANES], w * LANES + col[0:1]

    def fold(term, combine):
        parts = [None] * FOLD_WAYS
        for n, (sc, kp) in enumerate(pieces()):
            x = term(sc, kp)
            parts[n % FOLD_WAYS] = x if parts[n % FOLD_WAYS] is None else combine(parts[n % FOLD_WAYS], x)
        parts = [p for p in parts if p is not None]
        while len(parts) > 1:
            parts = [combine(parts[j], parts[j + 1]) if j + 1 < len(parts) else parts[j]
                     for j in range(0, len(parts), 2)]
        return parts[0]

    def fold_sum(pred):
        cnt = fold(lambda sc, kp: jnp.where(pred(sc, kp), 1.0, 0.0), jnp.add)
        return jnp.sum(cnt, axis=-1, keepdims=True)

    def fold_min(val):
        return jnp.min(fold(val, jnp.minimum), axis=-1, keepdims=True)

    index_bits = (past + LANES - 1).bit_length()
    sel_v, sel_m = _topk_select(fold_sum, fold_min, (SUBLANES, 1), topk, index_bits)
    selv_ref[...] = jnp.broadcast_to(sel_v, (SUBLANES, LANES))
    selm_ref[...] = jnp.broadcast_to(sel_m, (SUBLANES, LANES))


def _dsa_sample_attend_kernel(pt_ref, *refs, dec_seq, n_steps):
    del pt_ref
    kpages = refs[:PAGES_PER_STEP]
    vpages = refs[PAGES_PER_STEP:2 * PAGES_PER_STEP]
    (q_ref, sc_ref, scn_ref, selv_ref, selm_ref, kvn_ref, sbias_ref, sconst_ref, o_ref,
     m_scr, l_scr, acc_scr) = refs[2 * PAGES_PER_STEP:]
    s = pl.program_id(1)
    chunk = PAGES_PER_STEP * PAGE_SIZE
    rows = B_GQA * SUBLANES
    scale = B_HEAD_DIM ** -0.5

    @pl.when(s == 0)
    def _():
        m_scr[...] = jnp.full(m_scr.shape, NEG_INIT, F32)
        l_scr[...] = jnp.zeros(l_scr.shape, F32)
        acc_scr[...] = jnp.zeros(acc_scr.shape, F32)

    sel_v = selv_ref[:, 0:1]
    sel_m = selm_ref[:, 0:1]

    def chosen(sc, kpos):
        return jnp.where((sc > sel_v) | ((sc == sel_v) & (kpos <= sel_m)), 0.0, -jnp.inf)

    def head_rows(pgs, g):
        return jnp.concatenate(
            [pg[pl.ds(g, PAGE_SIZE, stride=B_KV_HEADS), :].astype(BF16) for pg in pgs], axis=0)

    kcol = lax.broadcasted_iota(I32, (SUBLANES, chunk), 1)
    sel = chosen(sc_ref[...], s * chunk + kcol)
    sel = jnp.concatenate([sel] * B_GQA, axis=0)
    is_last = s == n_steps - 1
    ncol = lax.broadcasted_iota(I32, (SUBLANES, LANES), 1)
    seln = chosen(scn_ref[...], n_steps * chunk + ncol)
    seln = jnp.where(is_last & (scn_ref[...] > -jnp.inf), seln, -jnp.inf)
    seln = jnp.concatenate([seln] * B_GQA, axis=0)

    def partial(lg, v):
        m = jnp.maximum(jnp.max(lg, axis=-1, keepdims=True), NEG_INIT)
        p = jnp.exp(lg - m)
        return m, jnp.sum(p, axis=-1, keepdims=True), jnp.dot(p.astype(BF16), v, preferred_element_type=F32)

    sub = SAMPLE_SUB_PAGES * PAGE_SIZE
    nsub = PAGES_PER_STEP // SAMPLE_SUB_PAGES
    lane_sub = lax.broadcasted_iota(I32, (rows, sub), 1)
    for g in range(B_KV_HEADS):
        qg = q_ref[g]
        far = sconst_ref[g, :, 0:1]
        near = jnp.concatenate([jnp.zeros((rows, sub - PAGE_SIZE), F32), sbias_ref[g, :, :PAGE_SIZE]], axis=-1)
        parts = [(m_scr[g], l_scr[g], acc_scr[g])]
        for j in range(nsub):
            pj = slice(j * SAMPLE_SUB_PAGES, (j + 1) * SAMPLE_SUB_PAGES)
            lg = _dot_nt(qg, head_rows(kpages[pj], g)) * scale + sel[:, j * sub:(j + 1) * sub]
            if j == nsub - 1:
                lg = lg + jnp.where(is_last & (lane_sub >= sub - PAGE_SIZE), near, far)
            else:
                lg = lg + far
            parts.append(partial(lg, head_rows(vpages[pj], g)))
        kn = kvn_ref[:, g * B_HEAD_DIM:(g + 1) * B_HEAD_DIM]
        vn = kvn_ref[:, KV_WIDTH + g * B_HEAD_DIM:KV_WIDTH + (g + 1) * B_HEAD_DIM]
        parts.append(partial(_dot_nt(qg, kn) * scale + sbias_ref[g, :, PAGE_SIZE:] + seln, vn))
        m_new = parts[0][0]
        for m, _, _ in parts[1:]:
            m_new = jnp.maximum(m_new, m)
        l_new = jnp.zeros((rows, 1), F32)
        acc_new = jnp.zeros((rows, B_HEAD_DIM), F32)
        for m, l, acc in parts:
            w = jnp.exp(m - m_new)
            l_new = l_new + w * l
            acc_new = acc_new + w * acc
        m_scr[g] = m_new
        l_scr[g] = l_new
        acc_scr[g] = acc_new

    @pl.when(is_last)
    def _():
        for g in range(B_KV_HEADS):
            o_ref[g] = (acc_scr[g] / l_scr[g]).astype(BF16)


def _dsa_sample(p_s, tail_s, k_s, v_s, cache_k_l, cache_v_l, cache_idx_l, page_table, sbias, sconst,
                *, dec_batch, dec_seq):
    n_pages = page_table.shape[1]
    past = n_pages * PAGE_SIZE
    topk = min(TOPK_MAX, (past + dec_seq) // 4)
    n_steps = n_pages // PAGES_PER_STEP
    chunk = PAGES_PER_STEP * PAGE_SIZE
    rows = B_GQA * SUBLANES
    n_pool = cache_k_l.shape[0]

    assert SUBLANES % dec_seq == 0
    qi = p_s[:, COL_QI:COL_MAIN].reshape(dec_batch, dec_seq, IDX_HEADS, IDX_DIM).transpose(0, 2, 1, 3)
    qi = qi.reshape(dec_batch, IDX_HEADS * dec_seq, IDX_DIM)
    wi = tail_s[:, IDX_DIM:IDX_DIM + IDX_HEADS].reshape(dec_batch, dec_seq, IDX_HEADS).transpose(0, 2, 1)
    wi = wi.reshape(dec_batch, IDX_HEADS * dec_seq, 1)
    ki_new = tail_s[:, :IDX_DIM].astype(BF16).reshape(dec_batch, dec_seq, IDX_DIM)
    ki_new = jnp.pad(ki_new, ((0, 0), (0, LANES - dec_seq), (0, 0))).transpose(0, 2, 1)
    q = p_s[:, COL_Q:COL_K].reshape(dec_batch, dec_seq, B_KV_HEADS, B_GQA, B_HEAD_DIM)
    q = jnp.pad(q.transpose(0, 2, 3, 1, 4), ((0, 0), (0, 0), (0, 0), (0, SUBLANES - dec_seq), (0, 0)))
    q = q.reshape(dec_batch, B_KV_HEADS, rows, B_HEAD_DIM)
    kv_new = jnp.concatenate([k_s, v_s], axis=-1).astype(BF16).reshape(dec_batch, dec_seq, 2 * KV_WIDTH)
    kv_new = jnp.pad(kv_new, ((0, 0), (0, LANES - dec_seq), (0, 0)))

    def page_map(r):
        return lambda b, s, pt: (pt[b, s * PAGES_PER_STEP + r], 0, 0)

    cache_idx_t = jnp.swapaxes(cache_idx_l, 1, 2)
    idx_specs = [pl.BlockSpec((None, IDX_DIM, PAGE_SIZE), lambda b, pt, r=r: (pt[b, r], 0, 0))
                 for r in range(n_pages)]
    stat_spec = pl.BlockSpec((None, SUBLANES, LANES), lambda b, pt: (b, 0, 0))
    sc, scn, selv, selm = pl.pallas_call(
        functools.partial(_dsa_sample_score_kernel, topk=topk, dec_seq=dec_seq, n_pages=n_pages),
        grid_spec=pltpu.PrefetchScalarGridSpec(
            num_scalar_prefetch=1,
            grid=(dec_batch,),
            in_specs=idx_specs + [
                pl.BlockSpec((None, IDX_HEADS * dec_seq, IDX_DIM), lambda b, pt: (b, 0, 0)),
                pl.BlockSpec((None, IDX_HEADS * dec_seq, 1), lambda b, pt: (b, 0, 0)),
                pl.BlockSpec((None, IDX_DIM, LANES), lambda b, pt: (b, 0, 0)),
            ],
            out_specs=[pl.BlockSpec((None, SUBLANES, past), lambda b, pt: (b, 0, 0)),
                       stat_spec, stat_spec, stat_spec],
        ),
        out_shape=[
            jax.ShapeDtypeStruct((dec_batch, SUBLANES, past), F32),
            jax.ShapeDtypeStruct((dec_batch, SUBLANES, LANES), F32),
            jax.ShapeDtypeStruct((dec_batch, SUBLANES, LANES), F32),
            jax.ShapeDtypeStruct((dec_batch, SUBLANES, LANES), I32),
        ],
        compiler_params=_cparams(("arbitrary",)),
        name="dsa_sample_score",
    )(page_table, *([cache_idx_t] * n_pages), qi, wi, ki_new)

    ck = cache_k_l.reshape(n_pool, PAGE_SIZE * B_KV_HEADS, B_HEAD_DIM)
    cv = cache_v_l.reshape(n_pool, PAGE_SIZE * B_KV_HEADS, B_HEAD_DIM)
    kv_specs = [pl.BlockSpec((None, PAGE_SIZE * B_KV_HEADS, B_HEAD_DIM), page_map(r))
                for r in range(PAGES_PER_STEP)]
    o = pl.pallas_call(
        functools.partial(_dsa_sample_attend_kernel, dec_seq=dec_seq, n_steps=n_steps),
        grid_spec=pltpu.PrefetchScalarGridSpec(
            num_scalar_prefetch=1,
            grid=(dec_batch, n_steps),
            in_specs=kv_specs + kv_specs + [
                pl.BlockSpec((None, B_KV_HEADS, rows, B_HEAD_DIM), lambda b, s, pt: (b, 0, 0, 0)),
                pl.BlockSpec((None, SUBLANES, chunk), lambda b, s, pt: (b, 0, s)),
                pl.BlockSpec((None, SUBLANES, LANES), lambda b, s, pt: (b, 0, 0)),
                pl.BlockSpec((None, SUBLANES, LANES), lambda b, s, pt: (b, 0, 0)),
                pl.BlockSpec((None, SUBLANES, LANES), lambda b, s, pt: (b, 0, 0)),
                pl.BlockSpec((None, LANES, 2 * KV_WIDTH), lambda b, s, pt: (b, 0, 0)),
                pl.BlockSpec(sbias.shape, lambda b, s, pt: (0, 0, 0)),
                pl.BlockSpec(sconst.shape, lambda b, s, pt: (0, 0, 0)),
            ],
            out_specs=pl.BlockSpec((None, B_KV_HEADS, rows, B_HEAD_DIM), lambda b, s, pt: (b, 0, 0, 0)),
            scratch_shapes=[
                pltpu.VMEM((B_KV_HEADS, rows, 1), F32),
                pltpu.VMEM((B_KV_HEADS, rows, 1), F32),
                pltpu.VMEM((B_KV_HEADS, rows, B_HEAD_DIM), F32),
            ],
        ),
        out_shape=jax.ShapeDtypeStruct((dec_batch, B_KV_HEADS, rows, B_HEAD_DIM), BF16),
        compiler_params=_cparams(("arbitrary", "arbitrary")),
        name="dsa_sample_attend",
    )(page_table, *([ck] * PAGES_PER_STEP), *([cv] * PAGES_PER_STEP), q, sc, scn, selv, selm, kv_new,
      sbias, sconst)
    o = o.reshape(dec_batch, B_KV_HEADS, B_GQA, SUBLANES, B_HEAD_DIM)[:, :, :, :dec_seq].transpose(0, 3, 1, 2, 4)
    return o.reshape(dec_batch * dec_seq, B_WIDTH)


def _mixout_kernel(u_ref, vn_ref, b_ref, ws_ref, ab_ref, x_ref, w_ref, o_ref, mix_scr, *, tm):
    j = pl.program_id(1)

    @pl.when(j == 0)
    def _():
        row = lax.broadcasted_iota(I32, (A_CHUNK, A_CHUNK), 0)
        col = lax.broadcasted_iota(I32, (A_CHUNK, A_CHUNK), 1)
        for g in range(A_GROUPS):
            wsg = jnp.where(row >= col, ws_ref[g], 0.0).astype(BF16)
            bg = ab_ref[:, g:g + 1]
            cs = slice(g * A_CH, (g + 1) * A_CH)
            for r in range(tm // A_CHUNK):
                rs = slice(r * A_CHUNK, (r + 1) * A_CHUNK)
                sm = jnp.dot(wsg, vn_ref[rs, cs], preferred_element_type=F32) + bg
                mix_scr[rs, cs] = (u_ref[rs, cs].astype(F32) * sm).astype(BF16)
        mix_scr[:, A_WIDTH:] = b_ref[...]

    o_ref[...] = x_ref[...] + jnp.dot(mix_scr[...], w_ref[...].astype(BF16), preferred_element_type=F32)


def _mixout(p, b_out, ws_eff, ab_eff, x2d, w_out_l, *, tm, tn=512):
    n = x2d.shape[0]
    return pl.pallas_call(
        functools.partial(_mixout_kernel, tm=tm),
        grid=(n // tm, D_MODEL // tn),
        in_specs=[
            pl.BlockSpec((tm, A_WIDTH), lambda i, j: (i, COL_U // A_WIDTH)),
            pl.BlockSpec((tm, A_WIDTH), lambda i, j: (i, COL_V // A_WIDTH)),
            pl.BlockSpec((tm, B_WIDTH), lambda i, j: (i, 0)),
            pl.BlockSpec((A_GROUPS, A_CHUNK, A_CHUNK), lambda i, j: (0, 0, 0)),
            pl.BlockSpec((A_CHUNK, A_GROUPS), lambda i, j: (0, 0)),
            pl.BlockSpec((tm, tn), lambda i, j: (i, j)),
            pl.BlockSpec((A_WIDTH + B_WIDTH, tn), lambda i, j: (0, j)),
        ],
        out_specs=pl.BlockSpec((tm, tn), lambda i, j: (i, j)),
        out_shape=jax.ShapeDtypeStruct((n, D_MODEL), F32),
        scratch_shapes=[pltpu.VMEM((tm, A_WIDTH + B_WIDTH), BF16)],
        compiler_params=_cparams(("arbitrary", "arbitrary")),
        name="mixout",
    )(p, p, b_out, ws_eff, ab_eff, x2d, w_out_l)


def _memkv_kernel(x_ref, wk_ref, wv_ref, kn_ref, mk_ref, mv_ref):
    x = x_ref[...].astype(BF16)
    tm = x.shape[0]
    mk = _head_norm(jnp.dot(x, wk_ref[...].astype(BF16), preferred_element_type=F32), kn_ref[...])
    mv = jnp.dot(x, wv_ref[...].astype(BF16), preferred_element_type=F32)
    for hd in range(MEM_HEADS):
        cs = slice(hd * MEM_HEAD_DIM, (hd + 1) * MEM_HEAD_DIM)
        mk_ref[pl.ds(hd, tm, stride=MEM_HEADS), :] = mk[:, cs]
        mv_ref[pl.ds(hd, tm, stride=MEM_HEADS), :] = mv[:, cs]


def _memkv(mem2d, wk, wv, knorm, *, tm=256):
    n = mem2d.shape[0]
    kn = jnp.tile(knorm, MEM_HEADS).reshape(1, MEM_WIDTH)
    return pl.pallas_call(
        _memkv_kernel,
        grid=(n // tm,),
        in_specs=[
            pl.BlockSpec((tm, D_MODEL), lambda i: (i, 0)),
            pl.BlockSpec((D_MODEL, MEM_WIDTH), lambda i: (0, 0)),
            pl.BlockSpec((D_MODEL, MEM_WIDTH), lambda i: (0, 0)),
            pl.BlockSpec((1, MEM_WIDTH), lambda i: (0, 0)),
        ],
        out_specs=[pl.BlockSpec((tm * MEM_HEADS, MEM_HEAD_DIM), lambda i: (i, 0))] * 2,
        out_shape=[jax.ShapeDtypeStruct((n * MEM_HEADS, MEM_HEAD_DIM), F32)] * 2,
        compiler_params=_cparams(("arbitrary",)),
        name="memkv",
    )(mem2d, wk, wv, kn)


def _memattn_kernel(h_ref, g_ref, wq_ref, qn_ref, mk_ref, mv_ref, wo_ref, o_ref, *, rows_per_mem, mem_len):
    h = h_ref[...]
    n = _row_norm(h, g_ref[...]).astype(BF16)
    q = _head_norm(jnp.dot(n, wq_ref[...].astype(BF16), preferred_element_type=F32), qn_ref[...]).astype(BF16)
    tm, m = h.shape[0], mk_ref.shape[0] // MEM_HEADS
    if rows_per_mem is not None:
        row = lax.broadcasted_iota(I32, (tm, m), 0)
        col = lax.broadcasted_iota(I32, (tm, m), 1)
        mask = jnp.where(row // rows_per_mem == col // mem_len, 0.0, -jnp.inf)
    outs = []
    for hd in range(MEM_HEADS):
        cs = slice(hd * MEM_HEAD_DIM, (hd + 1) * MEM_HEAD_DIM)
        mk = mk_ref[pl.ds(hd, m, stride=MEM_HEADS), :].astype(BF16)
        mv = mv_ref[pl.ds(hd, m, stride=MEM_HEADS), :].astype(BF16)
        lg = _dot_nt(q[:, cs], mk) * (MEM_HEAD_DIM ** -0.5)
        if rows_per_mem is not None:
            lg = lg + mask
        e = jnp.exp(lg - jnp.max(lg, axis=-1, keepdims=True))
        l = jnp.sum(e, axis=-1, keepdims=True)
        outs.append(jnp.dot(e.astype(BF16), mv, preferred_element_type=F32) / l)
    o = jnp.concatenate(outs, axis=-1).astype(BF16)
    o_ref[...] = h + jnp.dot(o, wo_ref[...].astype(BF16), preferred_element_type=F32)


def _memattn(h2d, gain, wq, qnorm, mk, mv, wo, *, tm, mem_rows, tiles_per_mem, rows_per_mem, mem_len):
    n = h2d.shape[0]
    qn = jnp.tile(qnorm, MEM_HEADS).reshape(1, MEM_WIDTH)
    return pl.pallas_call(
        functools.partial(_memattn_kernel, rows_per_mem=rows_per_mem, mem_len=mem_len),
        grid=(n // tm,),
        in_specs=[
            pl.BlockSpec((tm, D_MODEL), lambda i: (i, 0)),
            pl.BlockSpec((1, D_MODEL), lambda i: (0, 0)),
            pl.BlockSpec((D_MODEL, MEM_WIDTH), lambda i: (0, 0)),
            pl.BlockSpec((1, MEM_WIDTH), lambda i: (0, 0)),
            pl.BlockSpec((mem_rows * MEM_HEADS, MEM_HEAD_DIM), lambda i: (i // tiles_per_mem, 0)),
            pl.BlockSpec((mem_rows * MEM_HEADS, MEM_HEAD_DIM), lambda i: (i // tiles_per_mem, 0)),
            pl.BlockSpec((MEM_WIDTH, D_MODEL), lambda i: (0, 0)),
        ],
        out_specs=pl.BlockSpec((tm, D_MODEL), lambda i: (i, 0)),
        out_shape=jax.ShapeDtypeStruct((n, D_MODEL), F32),
        compiler_params=_cparams(("arbitrary",)),
        name="memattn",
    )(h2d, gain.reshape(1, D_MODEL), wq, qn, mk, mv, wo)


def _router_kernel(h_ref, g_ref, w_ref, b_ref, n_ref, c_ref):
    n = _row_norm(h_ref[...], g_ref[...]).astype(BF16)
    n_ref[...] = n
    lg = jnp.dot(n, w_ref[...].astype(BF16), preferred_element_type=F32) + b_ref[...]
    gl, el = lg[:, :LANES], lg[:, LANES:]
    lane = lax.broadcasted_iota(I32, gl.shape, 1)
    lanef = lane.astype(F32)

    def first_max_lane(v, vmax):
        return jnp.min(jnp.where(v == vmax, lanef, float(LANES)), axis=-1, keepdims=True).astype(I32)

    gl = jnp.where(lane < N_GROUPS, gl, -jnp.inf)
    ge = jnp.exp(gl - jnp.max(gl, axis=-1, keepdims=True))
    gp = ge / jnp.sum(ge, axis=-1, keepdims=True)
    g_gate = jnp.max(gp, axis=-1, keepdims=True)
    g_sel = first_max_lane(gp, g_gate)
    in_grp = (lane // EXPERTS_PER_GROUP == g_sel) & (lane < N_EXPERTS)
    el = jnp.where(in_grp, el, -jnp.inf)
    ee = jnp.exp(el - jnp.max(el, axis=-1, keepdims=True))
    ep = jnp.where(in_grp, ee / jnp.sum(ee, axis=-1, keepdims=True), -jnp.inf)
    w1 = jnp.max(ep, axis=-1, keepdims=True)
    i1 = first_max_lane(ep, w1)
    ep2 = jnp.where(lane == i1, -jnp.inf, ep)
    w2 = jnp.max(ep2, axis=-1, keepdims=True)
    i2 = first_max_lane(ep2, w2)
    tot = w1 + w2
    comb = jnp.where(lane == i1, w1 / tot * g_gate, jnp.where(lane == i2, w2 / tot * g_gate, 0.0))
    c_ref[...] = jnp.where(lane == GSEL_LANE, g_sel.astype(F32), comb)


def _router(h2d, gain, w_cat, b_cat, *, tm):
    n = h2d.shape[0]
    return pl.pallas_call(
        _router_kernel,
        grid=(n // tm,),
        in_specs=[
            pl.BlockSpec((tm, D_MODEL), lambda i: (i, 0)),
            pl.BlockSpec((1, D_MODEL), lambda i: (0, 0)),
            pl.BlockSpec((D_MODEL, 2 * LANES), lambda i: (0, 0)),
            pl.BlockSpec((1, 2 * LANES), lambda i: (0, 0)),
        ],
        out_specs=[pl.BlockSpec((tm, D_MODEL), lambda i: (i, 0)), pl.BlockSpec((tm, LANES), lambda i: (i, 0))],
        out_shape=[jax.ShapeDtypeStruct((n, D_MODEL), BF16), jax.ShapeDtypeStruct((n, LANES), F32)],
        compiler_params=_cparams(("arbitrary",)),
        name="router",
    )(h2d, gain.reshape(1, D_MODEL), w_cat, b_cat)


def _cast_kernel(x_ref, o_ref):
    o_ref[...] = x_ref[...].astype(BF16)


def _cast_bf16(w):
    ne, r, c = w.shape
    return pl.pallas_call(
        _cast_kernel,
        grid=(ne,),
        in_specs=[pl.BlockSpec((None, r, c), lambda e: (e, 0, 0))],
        out_specs=pl.BlockSpec((None, r, c), lambda e: (e, 0, 0)),
        out_shape=jax.ShapeDtypeStruct(w.shape, BF16),
        compiler_params=_cparams(("arbitrary",)),
        name="cast_bf16",
    )(w)


GSEL_LANE = N_EXPERTS
MOE_ROWS = 128
MOE_FF_SPLIT = 1


def _moe_routed_kernel(n_ref, c_ref, h_hbm, wg_ref, wu_ref, wd_ref, o_ref,
                       tri_scr, key_scr, xc_scr, yc_scr, cw_scr, nch_ref, sem, *, tm):
    i, g, e, f = (pl.program_id(k) for k in range(4))
    main_rows = tm // N_GROUPS
    first_e = (e == 0) & (f == 0)
    last_e = (e == EXPERTS_PER_GROUP - 1) & (f == MOE_FF_SPLIT - 1)

    def onehot_rows(start, rows):
        want = (start + lax.broadcasted_iota(I32, (rows, 1), 0)).astype(F32)
        return jnp.where(key_scr[pl.ds(g, 1), :] == want, 1.0, 0.0).astype(BF16)

    def tail_start(c):
        return pl.multiple_of(main_rows + c * MOE_ROWS, MOE_ROWS)

    @pl.when((i == 0) & (g == 0) & first_e)
    def _():
        r = lax.broadcasted_iota(I32, (tm, tm), 0)
        c = lax.broadcasted_iota(I32, (tm, tm), 1)
        tri_scr[...] = jnp.where(r < c, 1.0, 0.0).astype(BF16)

    @pl.when((g == 0) & first_e)
    def _():
        cp = pltpu.make_async_copy(h_hbm.at[pl.ds(pl.multiple_of(i * tm, tm), tm)], o_ref, sem)
        cp.start()
        lane = lax.broadcasted_iota(I32, (tm, LANES), 1).astype(F32)
        onehot = jnp.where(lane == c_ref[:, GSEL_LANE:GSEL_LANE + 1], 1.0, 0.0).astype(BF16)
        er = lax.broadcasted_iota(I32, (SUBLANES, LANES), 0)
        ec = lax.broadcasted_iota(I32, (SUBLANES, LANES), 1)
        eye = jnp.where(er == ec, 1.0, 0.0).astype(BF16)
        mask_t = _dot_nt(eye, onehot)
        rank_t = jnp.dot(mask_t.astype(BF16), tri_scr[...], preferred_element_type=F32)
        key_scr[...] = jnp.where(mask_t > 0.5, rank_t, -1.0)
        for gg in range(N_GROUPS):
            cnt = jnp.sum(mask_t[gg:gg + 1, :], axis=-1, keepdims=True)
            extra = jnp.ceil(jnp.maximum(cnt - main_rows, 0.0) * (1.0 / MOE_ROWS))
            nch_ref[gg] = jnp.sum(extra).astype(I32)
        cp.wait()

    ntail = nch_ref[g]

    @pl.when(first_e)
    def _():
        comb = c_ref[...]
        hi = comb.astype(BF16)
        r1 = comb - hi.astype(F32)
        mid = r1.astype(BF16)
        lo = (r1 - mid.astype(F32)).astype(BF16)

        def compact(start, rows):
            s = onehot_rows(start, rows)
            xc_scr[pl.ds(start, rows), :] = jnp.dot(s, n_ref[...], preferred_element_type=F32).astype(BF16)
            cw_scr[pl.ds(start, rows), :] = (jnp.dot(s, hi, preferred_element_type=F32)
                                             + jnp.dot(s, mid, preferred_element_type=F32)
                                             + jnp.dot(s, lo, preferred_element_type=F32))
            yc_scr[pl.ds(start, rows), :] = jnp.zeros((rows, D_MODEL), F32)

        compact(0, main_rows)
        lax.fori_loop(0, ntail, lambda c, carry: (compact(tail_start(c), MOE_ROWS), carry)[1], 0)

    wg = wg_ref[...].astype(BF16)
    wu = wu_ref[...].astype(BF16)
    wd = wd_ref[...].astype(BF16)
    ex = g * EXPERTS_PER_GROUP + e

    def expert(start, rows):
        x = xc_scr[pl.ds(start, rows), :]
        a = jnp.dot(x, wg, preferred_element_type=F32)
        u = jnp.dot(x, wu, preferred_element_type=F32)
        lane = lax.broadcasted_iota(I32, (rows, LANES), 1)
        cw = jnp.sum(jnp.where(lane == ex, cw_scr[pl.ds(start, rows), :], 0.0), axis=-1, keepdims=True)
        hid = (a * jax.nn.sigmoid(a) * u * cw).astype(BF16)
        yc_scr[pl.ds(start, rows), :] += jnp.dot(hid, wd, preferred_element_type=F32)

    expert(0, main_rows)
    lax.fori_loop(0, ntail, lambda c, carry: (expert(tail_start(c), MOE_ROWS), carry)[1], 0)

    @pl.when(last_e)
    def _():
        def scatter(start, rows):
            s = onehot_rows(start, rows)
            y = yc_scr[pl.ds(start, rows), :].astype(BF16)
            o_ref[...] += lax.dot_general(s, y, (((0,), (0,)), ((), ())), preferred_element_type=F32)

        scatter(0, main_rows)
        lax.fori_loop(0, ntail, lambda c, carry: (scatter(tail_start(c), MOE_ROWS), carry)[1], 0)


def _moe_routed(n3, comb, h2d, w_gate_l, w_up_l, w_down_l, *, tm):
    n = h2d.shape[0]
    ff = EXPERT_FF // MOE_FF_SPLIT

    def wmap(i, g, e, f):
        return (g * EXPERTS_PER_GROUP + e, 0, f)

    return pl.pallas_call(
        functools.partial(_moe_routed_kernel, tm=tm),
        grid=(n // tm, N_GROUPS, EXPERTS_PER_GROUP, MOE_FF_SPLIT),
        in_specs=[
            pl.BlockSpec((tm, D_MODEL), lambda i, g, e, f: (i, 0), pipeline_mode=pl.Buffered(1)),
            pl.BlockSpec((tm, LANES), lambda i, g, e, f: (i, 0), pipeline_mode=pl.Buffered(1)),
            pl.BlockSpec(memory_space=pl.ANY),
            pl.BlockSpec((None, D_MODEL, ff), wmap),
            pl.BlockSpec((None, D_MODEL, ff), wmap),
            pl.BlockSpec((None, ff, D_MODEL), lambda i, g, e, f: (g * EXPERTS_PER_GROUP + e, f, 0)),
        ],
        out_specs=pl.BlockSpec((tm, D_MODEL), lambda i, g, e, f: (i, 0)),
        out_shape=jax.ShapeDtypeStruct((n, D_MODEL), F32),
        scratch_shapes=[
            pltpu.VMEM((tm, tm), BF16),
            pltpu.VMEM((SUBLANES, tm), F32),
            pltpu.VMEM((tm, D_MODEL), BF16),
            pltpu.VMEM((tm, D_MODEL), F32),
            pltpu.VMEM((tm, LANES), F32),
            pltpu.SMEM((N_GROUPS,), I32),
            pltpu.SemaphoreType.DMA(()),
        ],
        compiler_params=pltpu.CompilerParams(
            dimension_semantics=("arbitrary",) * 4, vmem_limit_bytes=MOE_VMEM_LIMIT),
        name="moe_routed",
    )(n3, comb, h2d, w_gate_l, w_up_l, w_down_l)


def _moe_kernel(n_ref, c_ref, h_ref, wg_ref, wu_ref, wd_ref, o_ref):
    e = pl.program_id(1)

    @pl.when(e == 0)
    def _():
        o_ref[...] = h_ref[...]

    x = n_ref[...]
    a = jnp.dot(x, wg_ref[...].astype(BF16), preferred_element_type=F32)
    u = jnp.dot(x, wu_ref[...].astype(BF16), preferred_element_type=F32)
    lane = lax.broadcasted_iota(I32, c_ref.shape, 1)
    c = jnp.sum(jnp.where(lane == e, c_ref[...], 0.0), axis=-1, keepdims=True)
    hid = (a * jax.nn.sigmoid(a) * u * c).astype(BF16)
    o_ref[...] += jnp.dot(hid, wd_ref[...].astype(BF16), preferred_element_type=F32)


def _moe(n3, comb, h2d, w_gate_l, w_up_l, w_down_l, *, tm):
    n = h2d.shape[0]
    return pl.pallas_call(
        _moe_kernel,
        grid=(n // tm, N_EXPERTS),
        in_specs=[
            pl.BlockSpec((tm, D_MODEL), lambda i, e: (i, 0)),
            pl.BlockSpec((tm, LANES), lambda i, e: (i, 0)),
            pl.BlockSpec((tm, D_MODEL), lambda i, e: (i, 0)),
            pl.BlockSpec((None, D_MODEL, EXPERT_FF), lambda i, e: (e, 0, 0)),
            pl.BlockSpec((None, D_MODEL, EXPERT_FF), lambda i, e: (e, 0, 0)),
            pl.BlockSpec((None, EXPERT_FF, D_MODEL), lambda i, e: (e, 0, 0)),
        ],
        out_specs=pl.BlockSpec((tm, D_MODEL), lambda i, e: (i, 0)),
        out_shape=jax.ShapeDtypeStruct((n, D_MODEL), F32),
        compiler_params=_cparams(("arbitrary", "arbitrary")),
        name="moe",
    )(n3, comb, h2d, w_gate_l, w_up_l, w_down_l)


def _post_mixer(p, b_out, ws_eff, ab_eff, x2d, mk, mv, lw, *, tm_mix, tm_mem, mem_rows, tiles_per_mem,
                rows_per_mem, mem_len, tm_route, tm_moe):
    h1 = _mixout(p, b_out, ws_eff, ab_eff, x2d, lw["w_out"], tm=tm_mix)
    h2 = _memattn(h1, lw["norm_mem"], lw["mem_wq"], lw["mem_qnorm"], mk, mv, lw["mem_wo"], tm=tm_mem,
                  mem_rows=mem_rows, tiles_per_mem=tiles_per_mem, rows_per_mem=rows_per_mem, mem_len=mem_len)
    n3, comb = _router(h2, lw["norm_ffn"], lw["w_cat"], lw["b_cat"], tm=tm_route)
    moe = _moe_routed if tm_moe > N_GROUPS * MOE_ROWS else _moe
    return moe(n3, comb, h2, lw["w_gate"], lw["w_up"], lw["w_down"], tm=tm_moe)


def kernel(x_prompt, x_sample, cache_k, cache_v, cache_idx_k, cache_mem_k, cache_mem_v, page_table,
           mem_prompt, norm_mix, w_in, a_vnorm, a_ws, a_b, b_qnorm, b_knorm, rel_bias, w_out,
           norm_mem, mem_wq, mem_wk, mem_wv, mem_qnorm, mem_knorm, mem_wo, norm_ffn,
           w_group, b_group, w_router, b_router, w_gate, w_up, w_down):
    bp, t, d = x_prompt.shape
    bd, s, _ = x_sample.shape
    depth = w_in.shape[0]
    mem_len = mem_prompt.shape[1]
    assert d == D_MODEL and t % KEY_CHUNK == 0 and s <= 8 and (bd * s) % LANES == 0

    bias_pairs, sbias, sconst = _bias_tables(rel_bias)
    hp = x_prompt.reshape(bp * t, d)
    hs = x_sample.reshape(bd * s, d)
    outs = [[] for _ in range(9)]
    for l in range(depth):
        w_cat = jnp.zeros((d, 2 * LANES), F32)
        w_cat = w_cat.at[:, :N_GROUPS].set(w_group[l]).at[:, LANES:LANES + N_EXPERTS].set(w_router[l])
        b_cat = jnp.zeros((1, 2 * LANES), F32)
        b_cat = b_cat.at[0, :N_GROUPS].set(b_group[l]).at[0, LANES:LANES + N_EXPERTS].set(b_router[l])
        lw = dict(w_out=w_out[l], norm_mem=norm_mem[l], mem_wq=mem_wq[l], mem_qnorm=mem_qnorm[l],
                  mem_wo=mem_wo[l], norm_ffn=norm_ffn[l], w_cat=w_cat, b_cat=b_cat,
                  w_gate=_cast_bf16(w_gate[l]), w_up=_cast_bf16(w_up[l]), w_down=_cast_bf16(w_down[l]))

        p, k, v, tail = _proj(hp, norm_mix[l], w_in[l], a_vnorm[l], b_qnorm[l], b_knorm[l],
                              tm=1024, want_vn=False)
        b_out = _dsa_prompt(p, tail, bias_pairs, batch=bp, seq=t)
        mk, mv = _memkv(mem_prompt.reshape(bp * mem_len, d), mem_wk[l], mem_wv[l], mem_knorm[l])
        hp = _post_mixer(p, b_out, a_ws[l], a_b[l].T, hp, mk, mv, lw, tm_mix=1024, tm_mem=512,
                         mem_rows=mem_len, tiles_per_mem=t // 512, rows_per_mem=None, mem_len=mem_len,
                         tm_route=512, tm_moe=1024)
        outs[0].append(k.reshape(bp, t, B_KV_HEADS, B_HEAD_DIM))
        outs[1].append(v.reshape(bp, t, B_KV_HEADS, B_HEAD_DIM))
        outs[2].append(tail[:, :IDX_DIM].reshape(bp, t, IDX_DIM))
        outs[3].append(mk.reshape(bp, mem_len, MEM_HEADS, MEM_HEAD_DIM))
        outs[4].append(mv.reshape(bp, mem_len, MEM_HEADS, MEM_HEAD_DIM))

        ns = bd * s
        p_s, k_s, v_s, tail_s, vn_s = _proj(hs, norm_mix[l], w_in[l], a_vnorm[l], b_qnorm[l], b_knorm[l],
                                            tm=ns, want_vn=True)
        b_out_s = _dsa_sample(p_s, tail_s, k_s, v_s, cache_k[l], cache_v[l], cache_idx_k[l], page_table,
                              sbias, sconst, dec_batch=bd, dec_seq=s)
        ws_s = jnp.einsum("ab,gts->gatbs", jnp.eye(ns // s, dtype=F32), a_ws[l][:, :s, :s])
        ws_s = ws_s.reshape(A_GROUPS, ns, ns)
        ab_s = jnp.tile(a_b[l][:, :s].T, (ns // s, 1))
        mem_tile = 32
        hs = _post_mixer(p_s, b_out_s, ws_s, ab_s, hs,
                         cache_mem_k[l].reshape(bd * mem_len * MEM_HEADS, MEM_HEAD_DIM),
                         cache_mem_v[l].reshape(bd * mem_len * MEM_HEADS, MEM_HEAD_DIM), lw,
                         tm_mix=ns, tm_mem=mem_tile, mem_rows=mem_tile // s * mem_len, tiles_per_mem=1,
                         rows_per_mem=s, mem_len=mem_len, tm_route=ns, tm_moe=ns)
        outs[5].append(k_s.reshape(bd, s, B_KV_HEADS, B_HEAD_DIM))
        outs[6].append(v_s.reshape(bd, s, B_KV_HEADS, B_HEAD_DIM))
        outs[7].append(tail_s[:, :IDX_DIM].reshape(bd, s, IDX_DIM))
        outs[8].append(vn_s.reshape(bd, s, A_GROUPS, A_CH))
    return (hp.reshape(bp, t, d), hs.reshape(bd, s, d)) + tuple(jnp.stack(o) for o in outs)
```

```python
import functools
import math

import jax
import jax.numpy as jnp
from jax import lax
from jax.experimental import pallas as pl
from jax.experimental.pallas import tpu as pltpu

BF16 = jnp.bfloat16
F32 = jnp.float32
I32 = jnp.int32

D_MODEL = 2048
A_GROUPS = 8
A_CH = 128
A_WIDTH = A_GROUPS * A_CH
A_CHUNK = 128
B_HEADS = 8
B_HEAD_DIM = 128
B_KV_HEADS = 2
B_GQA = B_HEADS // B_KV_HEADS
B_WIDTH = B_HEADS * B_HEAD_DIM
KV_WIDTH = B_KV_HEADS * B_HEAD_DIM
IDX_HEADS = 16
IDX_DIM = 64
TOPK_MAX = 256
Q_BLOCK = 128
REL_BUCKETS = 32
REL_MAX_DIST = 128
MEM_HEADS = 4
MEM_HEAD_DIM = 128
MEM_WIDTH = MEM_HEADS * MEM_HEAD_DIM
N_GROUPS = 4
EXPERTS_PER_GROUP = 4
N_EXPERTS = N_GROUPS * EXPERTS_PER_GROUP
EXPERT_FF = 512
PAGE_SIZE = 128
EPS = 1e-6

COL_U = 0
COL_V = COL_U + A_WIDTH
COL_Q = COL_V + A_WIDTH
COL_K = COL_Q + B_WIDTH
COL_VV = COL_K + KV_WIDTH
COL_QI = COL_VV + KV_WIDTH
COL_MAIN = COL_QI + IDX_HEADS * IDX_DIM
TAIL_COLS = IDX_DIM + IDX_HEADS
LANES = 128
SUBLANES = 8
PROJ_TN = 512
PROJ_ROW_CHUNK = 256
KEY_CHUNK = 256
NEG_INIT = -1e30
VMEM_LIMIT = 56 * 1024 * 1024
MOE_VMEM_LIMIT = 60 * 1024 * 1024


def _cparams(sem):
    return pltpu.CompilerParams(dimension_semantics=sem, vmem_limit_bytes=VMEM_LIMIT)


def _gelu(x):
    return 0.5 * x * (1.0 + lax.erf(x * (2.0 ** -0.5)))


def _head_norm(a, gain, width=LANES):
    outs = []
    for c in range(a.shape[1] // width):
        blk = a[:, c * width:(c + 1) * width]
        ms = jnp.mean(blk * blk, axis=-1, keepdims=True)
        outs.append(blk * lax.rsqrt(ms + EPS) * gain[:, c * width:(c + 1) * width])
    return outs[0] if len(outs) == 1 else jnp.concatenate(outs, axis=-1)


def _row_norm(x, gain):
    ms = jnp.mean(x * x, axis=-1, keepdims=True)
    return x * lax.rsqrt(ms + EPS) * gain


def _dot_nt(a, b):
    return lax.dot_general(a, b, (((1,), (1,)), ((), ())), preferred_element_type=F32)


def _proj_kernel(x_ref, g_ref, w_ref, wt_ref, avn_ref, qn_ref, kn_ref, *rest, want_vn):
    if want_vn:
        p_ref, k_ref, v_ref, tail_ref, vn_ref, xn_scr = rest
    else:
        p_ref, k_ref, v_ref, tail_ref, xn_scr = rest
        vn_ref = None
    j = pl.program_id(1)
    ju = COL_V // PROJ_TN
    jv = COL_Q // PROJ_TN
    jq = COL_K // PROJ_TN
    jkv = COL_QI // PROJ_TN

    @pl.when(j == 0)
    def _():
        xn = _row_norm(x_ref[...], g_ref[...]).astype(BF16)
        xn_scr[...] = xn
        wlane = lax.broadcasted_iota(I32, wt_ref.shape, 1)
        wt = jnp.where(wlane < TAIL_COLS, wt_ref[...], 0.0).astype(BF16)
        t = jnp.dot(xn, wt, preferred_element_type=F32)
        lane = lax.broadcasted_iota(I32, t.shape, 1)
        tail_ref[...] = jnp.where(lane >= IDX_DIM, t * (IDX_HEADS ** -0.5), t)

    tm = xn_scr.shape[0]
    rc = min(tm, PROJ_ROW_CHUNK)

    def row_chunks(epilogue):
        w = w_ref[...].astype(BF16)
        for r in range(tm // rc):
            rs = slice(r * rc, (r + 1) * rc)
            epilogue(jnp.dot(xn_scr[rs, :], w, preferred_element_type=F32), rs)

    @pl.when(j < ju)
    def _():
        def epi(acc, rs):
            p_ref[rs, :] = _gelu(acc).astype(BF16)
        row_chunks(epi)

    @pl.when((j >= ju) & (j < jv))
    def _():
        def epi(acc, rs):
            vn = _head_norm(_gelu(acc), avn_ref[...])
            p_ref[rs, :] = vn.astype(BF16)
            if vn_ref is not None:
                vn_ref[rs, :] = vn
        row_chunks(epi)

    @pl.when((j >= jv) & (j < jq))
    def _():
        def epi(acc, rs):
            p_ref[rs, :] = _head_norm(acc, qn_ref[...]).astype(BF16)
        row_chunks(epi)

    @pl.when(j == jq)
    def _():
        def epi(acc, rs):
            k = _head_norm(acc[:, :KV_WIDTH], kn_ref[...])
            v = acc[:, KV_WIDTH:]
            k_ref[rs, :] = k
            v_ref[rs, :] = v
            p_ref[rs, :] = jnp.concatenate([k, v], axis=-1).astype(BF16)
        row_chunks(epi)

    @pl.when(j >= jkv)
    def _():
        def epi(acc, rs):
            p_ref[rs, :] = acc.astype(BF16)
        row_chunks(epi)


def _proj(x2d, gain, w_in_l, a_vnorm_l, b_qnorm_l, b_knorm_l, *, tm, want_vn):
    n = x2d.shape[0]
    nj = COL_MAIN // PROJ_TN
    avn = a_vnorm_l.reshape(1, A_WIDTH)
    qn = jnp.tile(b_qnorm_l, PROJ_TN // B_HEAD_DIM).reshape(1, PROJ_TN)
    kn = jnp.tile(b_knorm_l, B_KV_HEADS).reshape(1, KV_WIDTH)
    ju = COL_V // PROJ_TN
    nv = A_WIDTH // PROJ_TN

    def vmap_(i, j):
        return (0, jnp.clip(j - ju, 0, nv - 1))

    in_specs = [
        pl.BlockSpec((tm, D_MODEL), lambda i, j: (i, 0)),
        pl.BlockSpec((1, D_MODEL), lambda i, j: (0, 0)),
        pl.BlockSpec((D_MODEL, PROJ_TN), lambda i, j: (0, j)),
        pl.BlockSpec((D_MODEL, LANES), lambda i, j: (0, COL_MAIN // LANES)),
        pl.BlockSpec((1, PROJ_TN), vmap_),
        pl.BlockSpec((1, PROJ_TN), lambda i, j: (0, 0)),
        pl.BlockSpec((1, KV_WIDTH), lambda i, j: (0, 0)),
    ]
    out_shape = [
        jax.ShapeDtypeStruct((n, COL_MAIN), BF16),
        jax.ShapeDtypeStruct((n, KV_WIDTH), F32),
        jax.ShapeDtypeStruct((n, KV_WIDTH), F32),
        jax.ShapeDtypeStruct((n, LANES), F32),
    ]
    out_specs = [
        pl.BlockSpec((tm, PROJ_TN), lambda i, j: (i, j)),
        pl.BlockSpec((tm, KV_WIDTH), lambda i, j: (i, 0)),
        pl.BlockSpec((tm, KV_WIDTH), lambda i, j: (i, 0)),
        pl.BlockSpec((tm, LANES), lambda i, j: (i, 0)),
    ]
    if want_vn:
        out_shape.append(jax.ShapeDtypeStruct((n, A_WIDTH), F32))
        out_specs.append(pl.BlockSpec((tm, PROJ_TN), lambda i, j: (i, jnp.clip(j - ju, 0, nv - 1))))
    return pl.pallas_call(
        functools.partial(_proj_kernel, want_vn=want_vn),
        grid=(n // tm, nj),
        in_specs=in_specs,
        out_specs=out_specs,
        out_shape=out_shape,
        scratch_shapes=[pltpu.VMEM((tm, D_MODEL), BF16)],
        compiler_params=_cparams(("arbitrary", "arbitrary")),
        name="proj",
    )(x2d, gain.reshape(1, D_MODEL), w_in_l, w_in_l, avn, qn, kn)


def _t5_bucket(d):
    max_exact = REL_BUCKETS // 2
    d = jnp.maximum(d, 0)
    ratio = jnp.log(jnp.maximum(d, 1).astype(F32) / max_exact) / math.log(REL_MAX_DIST / max_exact)
    large = jnp.minimum(max_exact + jnp.floor(ratio * (REL_BUCKETS - max_exact)).astype(I32), REL_BUCKETS - 1)
    return jnp.where(d < max_exact, d, large)


def _bias_lookup(bucket, rb_ref, h):
    acc = jnp.zeros(bucket.shape, F32)
    for r in range(REL_BUCKETS):
        acc = jnp.where(bucket == r, rb_ref[r, h], acc)
    return acc


def _bias_kernel(rb_ref, pair_ref, samp_ref, sconst_ref):
    q = lax.broadcasted_iota(I32, (Q_BLOCK, Q_BLOCK), 0)
    kc = lax.broadcasted_iota(I32, (Q_BLOCK, Q_BLOCK), 1)
    nvar = pair_ref.shape[0]
    rels = range(-(nvar - 1), 2)
    buckets = {r: _t5_bucket(q - kc - r * Q_BLOCK) for r in rels}
    for h in range(B_HEADS):
        g, hh = divmod(h, B_GQA)
        tiles = {r: _bias_lookup(buckets[r], rb_ref, h) for r in rels}
        for var in range(nvar):
            r0 = var - (nvar - 1)
            for t in range(2):
                pair_ref[var, g, hh * Q_BLOCK:(hh + 1) * Q_BLOCK, t * Q_BLOCK:(t + 1) * Q_BLOCK] = tiles[r0 + t]
    rows = B_GQA * SUBLANES
    row = lax.broadcasted_iota(I32, (rows, 2 * PAGE_SIZE), 0)
    col = lax.broadcasted_iota(I32, (rows, 2 * PAGE_SIZE), 1)
    t = row % SUBLANES
    d = jnp.where(col < PAGE_SIZE, t + PAGE_SIZE - col, t - (col - PAGE_SIZE))
    bucket = _t5_bucket(d)
    far = jnp.full((rows, LANES), REL_BUCKETS - 1, I32)
    rowc = lax.broadcasted_iota(I32, (rows, LANES), 0)
    for g in range(B_KV_HEADS):
        acc = jnp.zeros((rows, 2 * PAGE_SIZE), F32)
        accc = jnp.zeros((rows, LANES), F32)
        for hh in range(B_GQA):
            h = g * B_GQA + hh
            acc = jnp.where(row // SUBLANES == hh, _bias_lookup(bucket, rb_ref, h), acc)
            accc = jnp.where(rowc // SUBLANES == hh, _bias_lookup(far, rb_ref, h), accc)
        samp_ref[g] = acc
        sconst_ref[g] = accc


N_PAIR_VARIANTS = 4


def _bias_tables(rel_bias):
    rows = B_GQA * SUBLANES
    return pl.pallas_call(
        _bias_kernel,
        in_specs=[pl.BlockSpec(memory_space=pltpu.SMEM)],
        out_shape=[
            jax.ShapeDtypeStruct((N_PAIR_VARIANTS, B_KV_HEADS, B_GQA * Q_BLOCK, KEY_CHUNK), F32),
            jax.ShapeDtypeStruct((B_KV_HEADS, rows, 2 * PAGE_SIZE), F32),
            jax.ShapeDtypeStruct((B_KV_HEADS, rows, LANES), F32),
        ],
        name="bias_tables",
    )(rel_bias)


def _key_to_f32(key):
    bits = jnp.where(key < 0, key ^ I32(-2 ** 31), ~key)
    return lax.bitcast_convert_type(bits, F32)


def _kth_largest(count_ge, shape, k):
    def body(it, prefix):
        bit = 31 - it
        cand = prefix | lax.shift_left(I32(1), bit)
        cnt = count_ge(_key_to_f32(cand))
        return jnp.where(cnt >= k, cand, prefix)

    prefix = lax.fori_loop(0, 32, body, jnp.zeros(shape, I32))
    thr = _key_to_f32(prefix)
    return jnp.where((prefix & I32(-2 ** 23)) == 0, -jnp.inf, thr)


NO_INDEX_BOUND = 2 ** 30
TIE_REPAIR_ROUNDS = 2


def _topk_select(fold_sum, fold_min, shape, k, index_bits):
    thr = _kth_largest(lambda t: fold_sum(lambda sc, kp: sc >= t), shape, k)
    c_ge = fold_sum(lambda sc, kp: sc >= thr)
    no_bound = jnp.full(shape, NO_INDEX_BOUND, I32)

    def repair(_):
        v = thr
        strict = jnp.zeros(shape, F32)

        def in_set(sc, v, strict):
            return (sc > v) | ((sc == v) & (strict < 0.5))

        for _ in range(TIE_REPAIR_ROUNDS):
            c_set = fold_sum(lambda sc, kp: in_set(sc, v, strict))
            vmin = fold_min(lambda sc, kp: jnp.where(in_set(sc, v, strict), sc, jnp.inf))
            c_gt = fold_sum(lambda sc, kp: sc > vmin)
            drop = (c_set > k) & (c_gt >= k)
            v = jnp.where(drop, vmin, v)
            strict = jnp.where(drop, 1.0, strict)
        c_set = fold_sum(lambda sc, kp: in_set(sc, v, strict))
        vt = fold_min(lambda sc, kp: jnp.where(in_set(sc, v, strict), sc, jnp.inf))
        need = k - fold_sum(lambda sc, kp: sc > vt)

        def idx_body(it, m):
            cand = m | lax.shift_left(I32(1), index_bits - 1 - it)
            below = fold_sum(lambda sc, kp: (sc == vt) & (kp < cand))
            return jnp.where(below < need, cand, m)

        m = lax.fori_loop(0, index_bits, idx_body, jnp.zeros(shape, I32))
        tie = c_set > k
        v_out = jnp.where(tie, vt, v)
        m_out = jnp.where(tie, m, jnp.where(strict > 0.5, -1, no_bound))
        return v_out, m_out

    return lax.cond(jnp.max(c_ge) > k, repair, lambda _: (thr, no_bound), 0)


def _chunk_loop(n, body, init):
    carry = lax.fori_loop(0, n // 2, lambda j, c: body(2 * j + 1, body(2 * j, c)), init)
    return lax.cond(n % 2 == 1, lambda c: body(n - 1, c), lambda c: c, carry)


def _dsa_prompt_kernel(q_ref, kv_ref, qia_ref, qib_ref, tailk_ref, tailq_ref, bias_ref, *rest, topk, n_cast):
    cast_in, rest = rest[:n_cast], rest[n_cast:]
    o_ref, rest = rest[0], rest[1:]
    cast_out, rest = rest[:n_cast], rest[n_cast:]
    kil_scr, kir_scr, sc_scr, lg_scr, m_scr, l_scr, acc_scr = rest
    i = pl.program_id(1)

    for src, dst in zip(cast_in, cast_out):
        dst[...] = src[...].astype(BF16)

    @pl.when(i == 0)
    def _():
        ki = tailk_ref[:, :IDX_DIM].astype(BF16)
        z = jnp.zeros_like(ki)
        kil_scr[...] = jnp.concatenate([ki, z], axis=-1)
        kir_scr[...] = jnp.concatenate([z, ki], axis=-1)

    wi = tailq_ref[:, IDX_DIM:IDX_DIM + IDX_HEADS] * (IDX_DIM ** -0.5)
    nk = (i + 2) // 2
    qpos = i * Q_BLOCK + lax.broadcasted_iota(I32, (Q_BLOCK, 1), 0)
    kloc = lax.broadcasted_iota(I32, (1, KEY_CHUNK), 1)
    heads_per_ref = qia_ref.shape[1] // IDX_DIM

    pairs = []
    for hp in range(IDX_HEADS // 2):
        ref = qia_ref if 2 * hp < heads_per_ref else qib_ref
        base = (2 * hp) % heads_per_ref * IDX_DIM
        pairs.append(ref[:, base:base + 2 * IDX_DIM])
    qstack = jnp.concatenate(pairs, axis=0)

    def score_chunk(c, carry):
        off = pl.multiple_of(c * KEY_CHUNK, KEY_CHUNK)
        acc = jnp.zeros((Q_BLOCK, KEY_CHUNK), F32)
        for side, k_scr in ((0, kil_scr), (1, kir_scr)):
            s = _dot_nt(qstack, k_scr[pl.ds(off, KEY_CHUNK), :])
            for hp in range(IDX_HEADS // 2):
                h = 2 * hp + side
                acc = acc + wi[:, h:h + 1] * jnp.maximum(s[hp * Q_BLOCK:(hp + 1) * Q_BLOCK], 0.0)
        kpos = c * KEY_CHUNK + kloc
        sc_scr[c] = jnp.where(kpos <= qpos, acc, -jnp.inf)
        return carry

    _chunk_loop(nk, score_chunk, 0)

    def fold_sum(pred):
        def body(c, cnt):
            m = jnp.where(pred(sc_scr[c], c * KEY_CHUNK + kloc), 1.0, 0.0)
            return cnt + m[:, :LANES] + m[:, LANES:]
        cnt = _chunk_loop(nk, body, jnp.zeros((Q_BLOCK, LANES), F32))
        return jnp.sum(cnt, axis=-1, keepdims=True)

    def fold_min(val):
        def body(c, acc):
            x = val(sc_scr[c], c * KEY_CHUNK + kloc)
            return jnp.minimum(acc, jnp.minimum(x[:, :LANES], x[:, LANES:]))
        acc = lax.fori_loop(0, nk, body, jnp.full((Q_BLOCK, LANES), jnp.inf, F32))
        return jnp.min(acc, axis=-1, keepdims=True)

    index_bits = (kv_ref.shape[0] - 1).bit_length()
    sel_v, sel_m = _topk_select(fold_sum, fold_min, (Q_BLOCK, 1), topk, index_bits)

    def mask_chunk(c, carry):
        kpos = c * KEY_CHUNK + kloc
        sc = sc_scr[c]
        chosen = (sc > sel_v) | ((sc == sel_v) & (kpos <= sel_m))
        sc_scr[c] = jnp.where(chosen & (kpos <= qpos), 0.0, -jnp.inf)
        return carry

    lax.fori_loop(0, nk, mask_chunk, 0)

    scale = B_HEAD_DIM ** -0.5
    rows = B_GQA * Q_BLOCK
    qgs = [jnp.concatenate(
        [q_ref[:, (g * B_GQA + hh) * B_HEAD_DIM:(g * B_GQA + hh + 1) * B_HEAD_DIM] for hh in range(B_GQA)],
        axis=0) for g in range(B_KV_HEADS)]
    m_scr[...] = jnp.full(m_scr.shape, -jnp.inf, F32)
    l_scr[...] = jnp.zeros(l_scr.shape, F32)
    acc_scr[...] = jnp.zeros(acc_scr.shape, F32)

    def logits_chunk(c, carry):
        off = pl.multiple_of(c * KEY_CHUNK, KEY_CHUNK)
        var = jnp.clip(2 * c - i, -(N_PAIR_VARIANTS - 1), 0) + (N_PAIR_VARIANTS - 1)
        sel = sc_scr[c]
        sel = jnp.concatenate([sel] * B_GQA, axis=0)
        for g in range(B_KV_HEADS):
            kc = kv_ref[pl.ds(off, KEY_CHUNK), g * B_HEAD_DIM:(g + 1) * B_HEAD_DIM]
            lg = _dot_nt(qgs[g], kc) * scale + bias_ref[var, g] + sel
            lg_scr[c, g] = lg
            m_scr[g] = jnp.maximum(m_scr[g], jnp.maximum(lg[:, :LANES], lg[:, LANES:]))
        return carry

    _chunk_loop(nk, logits_chunk, 0)
    for g in range(B_KV_HEADS):
        m = jnp.max(m_scr[g], axis=-1, keepdims=True)
        m_scr[g] = jnp.broadcast_to(m, (rows, LANES))

    def pv_chunk(c, carry):
        off = pl.multiple_of(c * KEY_CHUNK, KEY_CHUNK)
        for g in range(B_KV_HEADS):
            vc = kv_ref[pl.ds(off, KEY_CHUNK), KV_WIDTH + g * B_HEAD_DIM:KV_WIDTH + (g + 1) * B_HEAD_DIM]
            mb = m_scr[g]
            p = jnp.exp(lg_scr[c, g] - jnp.concatenate([mb, mb], axis=-1))
            l_scr[g] += p[:, :LANES] + p[:, LANES:]
            acc_scr[g] += jnp.dot(p.astype(BF16), vc, preferred_element_type=F32)
        return carry

    _chunk_loop(nk, pv_chunk, 0)
    for g in range(B_KV_HEADS):
        o = acc_scr[g] / jnp.sum(l_scr[g], axis=-1, keepdims=True)
        for hh in range(B_GQA):
            h = g * B_GQA + hh
            o_ref[:, h * B_HEAD_DIM:(h + 1) * B_HEAD_DIM] = o[hh * Q_BLOCK:(hh + 1) * Q_BLOCK].astype(BF16)


def _dsa_prompt(p, tail, bias_pairs, cast_weights, *, batch, seq):
    topk = min(TOPK_MAX, seq // 4)
    nblk = seq // Q_BLOCK
    qi_w = PROJ_TN
    steps = batch * nblk
    cast_specs, cast_shapes = [], []
    for w in cast_weights:
        ne, r, c = w.shape
        parts = steps // ne
        assert steps == ne * parts and r % (parts * 16) == 0
        spec = pl.BlockSpec((None, r // parts, c),
                            lambda b, i, parts=parts: ((b * nblk + i) // parts, (b * nblk + i) % parts, 0))
        cast_specs.append(spec)
        cast_shapes.append(jax.ShapeDtypeStruct(w.shape, BF16))
    outs = pl.pallas_call(
        functools.partial(_dsa_prompt_kernel, topk=topk, n_cast=len(cast_weights)),
        grid=(batch, nblk),
        in_specs=[
            pl.BlockSpec((Q_BLOCK, B_WIDTH), lambda b, i: (b * nblk + i, COL_Q // B_WIDTH)),
            pl.BlockSpec((seq, 2 * KV_WIDTH), lambda b, i: (b, COL_K // (2 * KV_WIDTH))),
            pl.BlockSpec((Q_BLOCK, qi_w), lambda b, i: (b * nblk + i, COL_QI // qi_w)),
            pl.BlockSpec((Q_BLOCK, qi_w), lambda b, i: (b * nblk + i, COL_QI // qi_w + 1)),
            pl.BlockSpec((seq, LANES), lambda b, i: (b, 0)),
            pl.BlockSpec((Q_BLOCK, LANES), lambda b, i: (b * nblk + i, 0)),
            pl.BlockSpec(bias_pairs.shape, lambda b, i: (0, 0, 0, 0)),
        ] + cast_specs,
        out_specs=[pl.BlockSpec((Q_BLOCK, B_WIDTH), lambda b, i: (b * nblk + i, 0))] + cast_specs,
        out_shape=[jax.ShapeDtypeStruct((batch * seq, B_WIDTH), BF16)] + cast_shapes,
        scratch_shapes=[
            pltpu.VMEM((seq, 2 * IDX_DIM), BF16),
            pltpu.VMEM((seq, 2 * IDX_DIM), BF16),
            pltpu.VMEM((seq // KEY_CHUNK, Q_BLOCK, KEY_CHUNK), F32),
            pltpu.VMEM((seq // KEY_CHUNK, B_KV_HEADS, B_GQA * Q_BLOCK, KEY_CHUNK), F32),
            pltpu.VMEM((B_KV_HEADS, B_GQA * Q_BLOCK, LANES), F32),
            pltpu.VMEM((B_KV_HEADS, B_GQA * Q_BLOCK, LANES), F32),
            pltpu.VMEM((B_KV_HEADS, B_GQA * Q_BLOCK, B_HEAD_DIM), F32),
        ],
        compiler_params=_cparams(("arbitrary", "arbitrary")),
        name="dsa_prompt",
    )(p, p, p, p, tail, tail, bias_pairs, *cast_weights)
    return outs[0], tuple(outs[1:])


PAGES_PER_STEP = 16
SAMPLE_SUB_PAGES = 16
SCORE_KEYS_PER_DOT = 2048
FOLD_WAYS = 8


def _dsa_sample_score_kernel(pt_ref, *refs, topk, dec_seq, n_pages):
    del pt_ref
    pages = refs[:n_pages]
    qi_ref, wi_ref, kin_ref, sc_ref, scn_ref, selv_ref, selm_ref = refs[n_pages:]
    past = n_pages * PAGE_SIZE
    pages_per_dot = SCORE_KEYS_PER_DOT // PAGE_SIZE
    qi = qi_ref[...]
    wi = wi_ref[...] * (IDX_DIM ** -0.5)

    def scores(keys_t):
        r = jnp.maximum(jnp.dot(qi, keys_t, preferred_element_type=F32), 0.0) * wi
        acc = r[0:SUBLANES]
        for j in range(1, IDX_HEADS * dec_seq // SUBLANES):
            acc = acc + r[j * SUBLANES:(j + 1) * SUBLANES]
        shift = dec_seq
        while shift < SUBLANES:
            acc = acc + pltpu.roll(acc, shift, axis=0)
            shift *= 2
        return acc

    for d in range(n_pages // pages_per_dot):
        keys_t = jnp.concatenate(
            [pg[...].astype(BF16) for pg in pages[d * pages_per_dot:(d + 1) * pages_per_dot]], axis=1)
        sc_ref[:, d * SCORE_KEYS_PER_DOT:(d + 1) * SCORE_KEYS_PER_DOT] = scores(keys_t)

    row = lax.broadcasted_iota(I32, (SUBLANES, LANES), 0)
    col = lax.broadcasted_iota(I32, (SUBLANES, LANES), 1)
    scn = jnp.where((col <= row % dec_seq) & (col < dec_seq), scores(kin_ref[...]), -jnp.inf)
    scn_ref[...] = scn

    def pieces():
        yield scn, past + col[0:1]
        for w in range(past // LANES):
            yield sc_ref[:, w * LANES:(w + 1) * LANES], w * LANES + col[0:1]

    def fold(term, combine):
        parts = [None] * FOLD_WAYS
        for n, (sc, kp) in enumerate(pieces()):
            x = term(sc, kp)
            parts[n % FOLD_WAYS] = x if parts[n % FOLD_WAYS] is None else combine(parts[n % FOLD_WAYS], x)
        parts = [p for p in parts if p is not None]
        while len(parts) > 1:
            parts = [combine(parts[j], parts[j + 1]) if j + 1 < len(parts) else parts[j]
                     for j in range(0, len(parts), 2)]
        return parts[0]

    def fold_sum(pred):
        cnt = fold(lambda sc, kp: jnp.where(pred(sc, kp), 1.0, 0.0), jnp.add)
        return jnp.sum(cnt, axis=-1, keepdims=True)

    def fold_min(val):
        return jnp.min(fold(val, jnp.minimum), axis=-1, keepdims=True)

    index_bits = (past + LANES - 1).bit_length()
    sel_v, sel_m = _topk_select(fold_sum, fold_min, (SUBLANES, 1), topk, index_bits)
    selv_ref[...] = jnp.broadcast_to(sel_v, (SUBLANES, LANES))
    selm_ref[...] = jnp.broadcast_to(sel_m, (SUBLANES, LANES))


def _dsa_sample_attend_kernel(pt_ref, *refs, dec_seq, n_steps):
    del pt_ref
    kpages = refs[:PAGES_PER_STEP]
    vpages = refs[PAGES_PER_STEP:2 * PAGES_PER_STEP]
    (q_ref, sc_ref, scn_ref, selv_ref, selm_ref, kvn_ref, sbias_ref, sconst_ref, o_ref,
     m_scr, l_scr, acc_scr) = refs[2 * PAGES_PER_STEP:]
    s = pl.program_id(1)
    chunk = PAGES_PER_STEP * PAGE_SIZE
    rows = B_GQA * SUBLANES
    scale = B_HEAD_DIM ** -0.5

    @pl.when(s == 0)
    def _():
        m_scr[...] = jnp.full(m_scr.shape, NEG_INIT, F32)
        l_scr[...] = jnp.zeros(l_scr.shape, F32)
        acc_scr[...] = jnp.zeros(acc_scr.shape, F32)

    sel_v = selv_ref[:, 0:1]
    sel_m = selm_ref[:, 0:1]

    def chosen(sc, kpos):
        return jnp.where((sc > sel_v) | ((sc == sel_v) & (kpos <= sel_m)), 0.0, -jnp.inf)

    def head_rows(pgs, g):
        return jnp.concatenate(
            [pg[pl.ds(g, PAGE_SIZE, stride=B_KV_HEADS), :].astype(BF16) for pg in pgs], axis=0)

    kcol = lax.broadcasted_iota(I32, (SUBLANES, chunk), 1)
    sel = chosen(sc_ref[...], s * chunk + kcol)
    sel = jnp.concatenate([sel] * B_GQA, axis=0)
    is_last = s == n_steps - 1
    ncol = lax.broadcasted_iota(I32, (SUBLANES, LANES), 1)
    seln = chosen(scn_ref[...], n_steps * chunk + ncol)
    seln = jnp.where(is_last & (scn_ref[...] > -jnp.inf), seln, -jnp.inf)
    seln = jnp.concatenate([seln] * B_GQA, axis=0)

    def partial(lg, v):
        m = jnp.maximum(jnp.max(lg, axis=-1, keepdims=True), NEG_INIT)
        p = jnp.exp(lg - m)
        return m, jnp.sum(p, axis=-1, keepdims=True), jnp.dot(p.astype(BF16), v, preferred_element_type=F32)

    sub = SAMPLE_SUB_PAGES * PAGE_SIZE
    nsub = PAGES_PER_STEP // SAMPLE_SUB_PAGES
    lane_sub = lax.broadcasted_iota(I32, (rows, sub), 1)
    for g in range(B_KV_HEADS):
        qg = q_ref[g]
        far = sconst_ref[g, :, 0:1]
        near = jnp.concatenate([jnp.zeros((rows, sub - PAGE_SIZE), F32), sbias_ref[g, :, :PAGE_SIZE]], axis=-1)
        parts = [(m_scr[g], l_scr[g], acc_scr[g])]
        for j in range(nsub):
            pj = slice(j * SAMPLE_SUB_PAGES, (j + 1) * SAMPLE_SUB_PAGES)
            lg = _dot_nt(qg, head_rows(kpages[pj], g)) * scale + sel[:, j * sub:(j + 1) * sub]
            if j == nsub - 1:
                lg = lg + jnp.where(is_last & (lane_sub >= sub - PAGE_SIZE), near, far)
            else:
                lg = lg + far
            parts.append(partial(lg, head_rows(vpages[pj], g)))
        kn = kvn_ref[:, g * B_HEAD_DIM:(g + 1) * B_HEAD_DIM]
        vn = kvn_ref[:, KV_WIDTH + g * B_HEAD_DIM:KV_WIDTH + (g + 1) * B_HEAD_DIM]
        parts.append(partial(_dot_nt(qg, kn) * scale + sbias_ref[g, :, PAGE_SIZE:] + seln, vn))
        m_new = parts[0][0]
        for m, _, _ in parts[1:]:
            m_new = jnp.maximum(m_new, m)
        l_new = jnp.zeros((rows, 1), F32)
        acc_new = jnp.zeros((rows, B_HEAD_DIM), F32)
        for m, l, acc in parts:
            w = jnp.exp(m - m_new)
            l_new = l_new + w * l
            acc_new = acc_new + w * acc
        m_scr[g] = m_new
        l_scr[g] = l_new
        acc_scr[g] = acc_new

    @pl.when(is_last)
    def _():
        for g in range(B_KV_HEADS):
            o_ref[g] = (acc_scr[g] / l_scr[g]).astype(BF16)


def _dsa_sample(p_s, tail_s, k_s, v_s, cache_k_l, cache_v_l, cache_idx_l, page_table, sbias, sconst,
                *, dec_batch, dec_seq):
    n_pages = page_table.shape[1]
    past = n_pages * PAGE_SIZE
    topk = min(TOPK_MAX, (past + dec_seq) // 4)
    n_steps = n_pages // PAGES_PER_STEP
    chunk = PAGES_PER_STEP * PAGE_SIZE
    rows = B_GQA * SUBLANES
    n_pool = cache_k_l.shape[0]

    assert SUBLANES % dec_seq == 0
    qi = p_s[:, COL_QI:COL_MAIN].reshape(dec_batch, dec_seq, IDX_HEADS, IDX_DIM).transpose(0, 2, 1, 3)
    qi = qi.reshape(dec_batch, IDX_HEADS * dec_seq, IDX_DIM)
    wi = tail_s[:, IDX_DIM:IDX_DIM + IDX_HEADS].reshape(dec_batch, dec_seq, IDX_HEADS).transpose(0, 2, 1)
    wi = wi.reshape(dec_batch, IDX_HEADS * dec_seq, 1)
    ki_new = tail_s[:, :IDX_DIM].astype(BF16).reshape(dec_batch, dec_seq, IDX_DIM)
    ki_new = jnp.pad(ki_new, ((0, 0), (0, LANES - dec_seq), (0, 0))).transpose(0, 2, 1)
    q = p_s[:, COL_Q:COL_K].reshape(dec_batch, dec_seq, B_KV_HEADS, B_GQA, B_HEAD_DIM)
    q = jnp.pad(q.transpose(0, 2, 3, 1, 4), ((0, 0), (0, 0), (0, 0), (0, SUBLANES - dec_seq), (0, 0)))
    q = q.reshape(dec_batch, B_KV_HEADS, rows, B_HEAD_DIM)
    kv_new = jnp.concatenate([k_s, v_s], axis=-1).astype(BF16).reshape(dec_batch, dec_seq, 2 * KV_WIDTH)
    kv_new = jnp.pad(kv_new, ((0, 0), (0, LANES - dec_seq), (0, 0)))

    def page_map(r):
        return lambda b, s, pt: (pt[b, s * PAGES_PER_STEP + r], 0, 0)

    cache_idx_t = jnp.swapaxes(cache_idx_l, 1, 2)
    idx_specs = [pl.BlockSpec((None, IDX_DIM, PAGE_SIZE), lambda b, pt, r=r: (pt[b, r], 0, 0))
                 for r in range(n_pages)]
    stat_spec = pl.BlockSpec((None, SUBLANES, LANES), lambda b, pt: (b, 0, 0))
    sc, scn, selv, selm = pl.pallas_call(
        functools.partial(_dsa_sample_score_kernel, topk=topk, dec_seq=dec_seq, n_pages=n_pages),
        grid_spec=pltpu.PrefetchScalarGridSpec(
            num_scalar_prefetch=1,
            grid=(dec_batch,),
            in_specs=idx_specs + [
                pl.BlockSpec((None, IDX_HEADS * dec_seq, IDX_DIM), lambda b, pt: (b, 0, 0)),
                pl.BlockSpec((None, IDX_HEADS * dec_seq, 1), lambda b, pt: (b, 0, 0)),
                pl.BlockSpec((None, IDX_DIM, LANES), lambda b, pt: (b, 0, 0)),
            ],
            out_specs=[pl.BlockSpec((None, SUBLANES, past), lambda b, pt: (b, 0, 0)),
                       stat_spec, stat_spec, stat_spec],
        ),
        out_shape=[
            jax.ShapeDtypeStruct((dec_batch, SUBLANES, past), F32),
            jax.ShapeDtypeStruct((dec_batch, SUBLANES, LANES), F32),
            jax.ShapeDtypeStruct((dec_batch, SUBLANES, LANES), F32),
            jax.ShapeDtypeStruct((dec_batch, SUBLANES, LANES), I32),
        ],
        compiler_params=_cparams(("arbitrary",)),
        name="dsa_sample_score",
    )(page_table, *([cache_idx_t] * n_pages), qi, wi, ki_new)

    ck = cache_k_l.reshape(n_pool, PAGE_SIZE * B_KV_HEADS, B_HEAD_DIM)
    cv = cache_v_l.reshape(n_pool, PAGE_SIZE * B_KV_HEADS, B_HEAD_DIM)
    kv_specs = [pl.BlockSpec((None, PAGE_SIZE * B_KV_HEADS, B_HEAD_DIM), page_map(r))
                for r in range(PAGES_PER_STEP)]
    o = pl.pallas_call(
        functools.partial(_dsa_sample_attend_kernel, dec_seq=dec_seq, n_steps=n_steps),
        grid_spec=pltpu.PrefetchScalarGridSpec(
            num_scalar_prefetch=1,
            grid=(dec_batch, n_steps),
            in_specs=kv_specs + kv_specs + [
                pl.BlockSpec((None, B_KV_HEADS, rows, B_HEAD_DIM), lambda b, s, pt: (b, 0, 0, 0)),
                pl.BlockSpec((None, SUBLANES, chunk), lambda b, s, pt: (b, 0, s)),
                pl.BlockSpec((None, SUBLANES, LANES), lambda b, s, pt: (b, 0, 0)),
                pl.BlockSpec((None, SUBLANES, LANES), lambda b, s, pt: (b, 0, 0)),
                pl.BlockSpec((None, SUBLANES, LANES), lambda b, s, pt: (b, 0, 0)),
                pl.BlockSpec((None, LANES, 2 * KV_WIDTH), lambda b, s, pt: (b, 0, 0)),
                pl.BlockSpec(sbias.shape, lambda b, s, pt: (0, 0, 0)),
                pl.BlockSpec(sconst.shape, lambda b, s, pt: (0, 0, 0)),
            ],
            out_specs=pl.BlockSpec((None, B_KV_HEADS, rows, B_HEAD_DIM), lambda b, s, pt: (b, 0, 0, 0)),
            scratch_shapes=[
                pltpu.VMEM((B_KV_HEADS, rows, 1), F32),
                pltpu.VMEM((B_KV_HEADS, rows, 1), F32),
                pltpu.VMEM((B_KV_HEADS, rows, B_HEAD_DIM), F32),
            ],
        ),
        out_shape=jax.ShapeDtypeStruct((dec_batch, B_KV_HEADS, rows, B_HEAD_DIM), BF16),
        compiler_params=_cparams(("arbitrary", "arbitrary")),
        name="dsa_sample_attend",
    )(page_table, *([ck] * PAGES_PER_STEP), *([cv] * PAGES_PER_STEP), q, sc, scn, selv, selm, kv_new,
      sbias, sconst)
    o = o.reshape(dec_batch, B_KV_HEADS, B_GQA, SUBLANES, B_HEAD_DIM)[:, :, :, :dec_seq].transpose(0, 3, 1, 2, 4)
    return o.reshape(dec_batch * dec_seq, B_WIDTH)


def _mixout_kernel(u_ref, vn_ref, b_ref, ws_ref, ab_ref, x_ref, w_ref, o_ref, mix_scr, *, tm):
    j = pl.program_id(1)

    @pl.when(j == 0)
    def _():
        row = lax.broadcasted_iota(I32, (A_CHUNK, A_CHUNK), 0)
        col = lax.broadcasted_iota(I32, (A_CHUNK, A_CHUNK), 1)
        for g in range(A_GROUPS):
            wsg = jnp.where(row >= col, ws_ref[g], 0.0).astype(BF16)
            bg = ab_ref[:, g:g + 1]
            cs = slice(g * A_CH, (g + 1) * A_CH)
            for r in range(tm // A_CHUNK):
                rs = slice(r * A_CHUNK, (r + 1) * A_CHUNK)
                sm = jnp.dot(wsg, vn_ref[rs, cs], preferred_element_type=F32) + bg
                mix_scr[rs, cs] = (u_ref[rs, cs].astype(F32) * sm).astype(BF16)
        mix_scr[:, A_WIDTH:] = b_ref[...]

    o_ref[...] = x_ref[...] + jnp.dot(mix_scr[...], w_ref[...].astype(BF16), preferred_element_type=F32)


def _mixout(p, b_out, ws_eff, ab_eff, x2d, w_out_l, *, tm, tn=512):
    n = x2d.shape[0]
    return pl.pallas_call(
        functools.partial(_mixout_kernel, tm=tm),
        grid=(n // tm, D_MODEL // tn),
        in_specs=[
            pl.BlockSpec((tm, A_WIDTH), lambda i, j: (i, COL_U // A_WIDTH)),
            pl.BlockSpec((tm, A_WIDTH), lambda i, j: (i, COL_V // A_WIDTH)),
            pl.BlockSpec((tm, B_WIDTH), lambda i, j: (i, 0)),
            pl.BlockSpec((A_GROUPS, A_CHUNK, A_CHUNK), lambda i, j: (0, 0, 0)),
            pl.BlockSpec((A_CHUNK, A_GROUPS), lambda i, j: (0, 0)),
            pl.BlockSpec((tm, tn), lambda i, j: (i, j)),
            pl.BlockSpec((A_WIDTH + B_WIDTH, tn), lambda i, j: (0, j)),
        ],
        out_specs=pl.BlockSpec((tm, tn), lambda i, j: (i, j)),
        out_shape=jax.ShapeDtypeStruct((n, D_MODEL), F32),
        scratch_shapes=[pltpu.VMEM((tm, A_WIDTH + B_WIDTH), BF16)],
        compiler_params=_cparams(("arbitrary", "arbitrary")),
        name="mixout",
    )(p, p, b_out, ws_eff, ab_eff, x2d, w_out_l)


def _memkv_kernel(x_ref, wk_ref, wv_ref, kn_ref, mk_ref, mv_ref):
    x = x_ref[...].astype(BF16)
    tm = x.shape[0]
    mk = _head_norm(jnp.dot(x, wk_ref[...].astype(BF16), preferred_element_type=F32), kn_ref[...])
    mv = jnp.dot(x, wv_ref[...].astype(BF16), preferred_element_type=F32)
    for hd in range(MEM_HEADS):
        cs = slice(hd * MEM_HEAD_DIM, (hd + 1) * MEM_HEAD_DIM)
        mk_ref[pl.ds(hd, tm, stride=MEM_HEADS), :] = mk[:, cs]
        mv_ref[pl.ds(hd, tm, stride=MEM_HEADS), :] = mv[:, cs]


def _memkv(mem2d, wk, wv, knorm, *, tm=256):
    n = mem2d.shape[0]
    kn = jnp.tile(knorm, MEM_HEADS).reshape(1, MEM_WIDTH)
    return pl.pallas_call(
        _memkv_kernel,
        grid=(n // tm,),
        in_specs=[
            pl.BlockSpec((tm, D_MODEL), lambda i: (i, 0)),
            pl.BlockSpec((D_MODEL, MEM_WIDTH), lambda i: (0, 0)),
            pl.BlockSpec((D_MODEL, MEM_WIDTH), lambda i: (0, 0)),
            pl.BlockSpec((1, MEM_WIDTH), lambda i: (0, 0)),
        ],
        out_specs=[pl.BlockSpec((tm * MEM_HEADS, MEM_HEAD_DIM), lambda i: (i, 0))] * 2,
        out_shape=[jax.ShapeDtypeStruct((n * MEM_HEADS, MEM_HEAD_DIM), F32)] * 2,
        compiler_params=_cparams(("arbitrary",)),
        name="memkv",
    )(mem2d, wk, wv, kn)


def _memattn_kernel(h_ref, g_ref, wq_ref, qn_ref, mk_ref, mv_ref, wo_ref, o_ref, *, rows_per_mem, mem_len):
    h = h_ref[...]
    n = _row_norm(h, g_ref[...]).astype(BF16)
    q = _head_norm(jnp.dot(n, wq_ref[...].astype(BF16), preferred_element_type=F32), qn_ref[...]).astype(BF16)
    tm, m = h.shape[0], mk_ref.shape[0] // MEM_HEADS
    if rows_per_mem is not None:
        row = lax.broadcasted_iota(I32, (tm, m), 0)
        col = lax.broadcasted_iota(I32, (tm, m), 1)
        mask = jnp.where(row // rows_per_mem == col // mem_len, 0.0, -jnp.inf)
    outs = []
    for hd in range(MEM_HEADS):
        cs = slice(hd * MEM_HEAD_DIM, (hd + 1) * MEM_HEAD_DIM)
        mk = mk_ref[pl.ds(hd, m, stride=MEM_HEADS), :].astype(BF16)
        mv = mv_ref[pl.ds(hd, m, stride=MEM_HEADS), :].astype(BF16)
        lg = _dot_nt(q[:, cs], mk) * (MEM_HEAD_DIM ** -0.5)
        if rows_per_mem is not None:
            lg = lg + mask
        e = jnp.exp(lg - jnp.max(lg, axis=-1, keepdims=True))
        l = jnp.sum(e, axis=-1, keepdims=True)
        outs.append(jnp.dot(e.astype(BF16), mv, preferred_element_type=F32) / l)
    o = jnp.concatenate(outs, axis=-1).astype(BF16)
    o_ref[...] = h + jnp.dot(o, wo_ref[...].astype(BF16), preferred_element_type=F32)


def _memattn(h2d, gain, wq, qnorm, mk, mv, wo, *, tm, mem_rows, tiles_per_mem, rows_per_mem, mem_len):
    n = h2d.shape[0]
    qn = jnp.tile(qnorm, MEM_HEADS).reshape(1, MEM_WIDTH)
    return pl.pallas_call(
        functools.partial(_memattn_kernel, rows_per_mem=rows_per_mem, mem_len=mem_len),
        grid=(n // tm,),
        in_specs=[
            pl.BlockSpec((tm, D_MODEL), lambda i: (i, 0)),
            pl.BlockSpec((1, D_MODEL), lambda i: (0, 0)),
            pl.BlockSpec((D_MODEL, MEM_WIDTH), lambda i: (0, 0)),
            pl.BlockSpec((1, MEM_WIDTH), lambda i: (0, 0)),
            pl.BlockSpec((mem_rows * MEM_HEADS, MEM_HEAD_DIM), lambda i: (i // tiles_per_mem, 0)),
            pl.BlockSpec((mem_rows * MEM_HEADS, MEM_HEAD_DIM), lambda i: (i // tiles_per_mem, 0)),
            pl.BlockSpec((MEM_WIDTH, D_MODEL), lambda i: (0, 0)),
        ],
        out_specs=pl.BlockSpec((tm, D_MODEL), lambda i: (i, 0)),
        out_shape=jax.ShapeDtypeStruct((n, D_MODEL), F32),
        compiler_params=_cparams(("arbitrary",)),
        name="memattn",
    )(h2d, gain.reshape(1, D_MODEL), wq, qn, mk, mv, wo)


def _router_kernel(h_ref, g_ref, w_ref, b_ref, n_ref, c_ref):
    n = _row_norm(h_ref[...], g_ref[...]).astype(BF16)
    n_ref[...] = n
    lg = jnp.dot(n, w_ref[...].astype(BF16), preferred_element_type=F32) + b_ref[...]
    gl, el = lg[:, :LANES], lg[:, LANES:]
    lane = lax.broadcasted_iota(I32, gl.shape, 1)
    lanef = lane.astype(F32)

    def first_max_lane(v, vmax):
        return jnp.min(jnp.where(v == vmax, lanef, float(LANES)), axis=-1, keepdims=True).astype(I32)

    gl = jnp.where(lane < N_GROUPS, gl, -jnp.inf)
    ge = jnp.exp(gl - jnp.max(gl, axis=-1, keepdims=True))
    gp = ge / jnp.sum(ge, axis=-1, keepdims=True)
    g_gate = jnp.max(gp, axis=-1, keepdims=True)
    g_sel = first_max_lane(gp, g_gate)
    in_grp = (lane // EXPERTS_PER_GROUP == g_sel) & (lane < N_EXPERTS)
    el = jnp.where(in_grp, el, -jnp.inf)
    ee = jnp.exp(el - jnp.max(el, axis=-1, keepdims=True))
    ep = jnp.where(in_grp, ee / jnp.sum(ee, axis=-1, keepdims=True), -jnp.inf)
    w1 = jnp.max(ep, axis=-1, keepdims=True)
    i1 = first_max_lane(ep, w1)
    ep2 = jnp.where(lane == i1, -jnp.inf, ep)
    w2 = jnp.max(ep2, axis=-1, keepdims=True)
    i2 = first_max_lane(ep2, w2)
    tot = w1 + w2
    comb = jnp.where(lane == i1, w1 / tot * g_gate, jnp.where(lane == i2, w2 / tot * g_gate, 0.0))
    c_ref[...] = jnp.where(lane == GSEL_LANE, g_sel.astype(F32), comb)


def _router(h2d, gain, w_cat, b_cat, *, tm):
    n = h2d.shape[0]
    return pl.pallas_call(
        _router_kernel,
        grid=(n // tm,),
        in_specs=[
            pl.BlockSpec((tm, D_MODEL), lambda i: (i, 0)),
            pl.BlockSpec((1, D_MODEL), lambda i: (0, 0)),
            pl.BlockSpec((D_MODEL, 2 * LANES), lambda i: (0, 0)),
            pl.BlockSpec((1, 2 * LANES), lambda i: (0, 0)),
        ],
        out_specs=[pl.BlockSpec((tm, D_MODEL), lambda i: (i, 0)), pl.BlockSpec((tm, LANES), lambda i: (i, 0))],
        out_shape=[jax.ShapeDtypeStruct((n, D_MODEL), BF16), jax.ShapeDtypeStruct((n, LANES), F32)],
        compiler_params=_cparams(("arbitrary",)),
        name="router",
    )(h2d, gain.reshape(1, D_MODEL), w_cat, b_cat)


def _cast_kernel(x_ref, o_ref):
    o_ref[...] = x_ref[...].astype(BF16)


def _cast_bf16(w):
    ne, r, c = w.shape
    return pl.pallas_call(
        _cast_kernel,
        grid=(ne,),
        in_specs=[pl.BlockSpec((None, r, c), lambda e: (e, 0, 0))],
        out_specs=pl.BlockSpec((None, r, c), lambda e: (e, 0, 0)),
        out_shape=jax.ShapeDtypeStruct(w.shape, BF16),
        compiler_params=_cparams(("arbitrary",)),
        name="cast_bf16",
    )(w)


GSEL_LANE = N_EXPERTS
MOE_ROWS = 128
MOE_FF_SPLIT = 1


def _moe_routed_kernel(n_ref, c_ref, h_hbm, wg_ref, wu_ref, wd_ref, o_ref,
                       tri_scr, key_scr, xc_scr, yc_scr, cw_scr, nch_ref, sem, *, tm):
    i, g, e, f = (pl.program_id(k) for k in range(4))
    main_rows = tm // N_GROUPS
    first_e = (e == 0) & (f == 0)
    last_e = (e == EXPERTS_PER_GROUP - 1) & (f == MOE_FF_SPLIT - 1)

    def onehot_rows(start, rows):
        want = (start + lax.broadcasted_iota(I32, (rows, 1), 0)).astype(F32)
        return jnp.where(key_scr[pl.ds(g, 1), :] == want, 1.0, 0.0).astype(BF16)

    def tail_start(c):
        return pl.multiple_of(main_rows + c * MOE_ROWS, MOE_ROWS)

    @pl.when((i == 0) & (g == 0) & first_e)
    def _():
        r = lax.broadcasted_iota(I32, (tm, tm), 0)
        c = lax.broadcasted_iota(I32, (tm, tm), 1)
        tri_scr[...] = jnp.where(r < c, 1.0, 0.0).astype(BF16)

    @pl.when((g == 0) & first_e)
    def _():
        cp = pltpu.make_async_copy(h_hbm.at[pl.ds(pl.multiple_of(i * tm, tm), tm)], o_ref, sem)
        cp.start()
        lane = lax.broadcasted_iota(I32, (tm, LANES), 1).astype(F32)
        onehot = jnp.where(lane == c_ref[:, GSEL_LANE:GSEL_LANE + 1], 1.0, 0.0).astype(BF16)
        er = lax.broadcasted_iota(I32, (SUBLANES, LANES), 0)
        ec = lax.broadcasted_iota(I32, (SUBLANES, LANES), 1)
        eye = jnp.where(er == ec, 1.0, 0.0).astype(BF16)
        mask_t = _dot_nt(eye, onehot)
        rank_t = jnp.dot(mask_t.astype(BF16), tri_scr[...], preferred_element_type=F32)
        key_scr[...] = jnp.where(mask_t > 0.5, rank_t, -1.0)
        for gg in range(N_GROUPS):
            cnt = jnp.sum(mask_t[gg:gg + 1, :], axis=-1, keepdims=True)
            extra = jnp.ceil(jnp.maximum(cnt - main_rows, 0.0) * (1.0 / MOE_ROWS))
            nch_ref[gg] = jnp.sum(extra).astype(I32)
        cp.wait()

    ntail = nch_ref[g]

    @pl.when(first_e)
    def _():
        comb = c_ref[...]
        hi = comb.astype(BF16)
        r1 = comb - hi.astype(F32)
        mid = r1.astype(BF16)
        lo = (r1 - mid.astype(F32)).astype(BF16)

        def compact(start, rows):
            s = onehot_rows(start, rows)
            xc_scr[pl.ds(start, rows), :] = jnp.dot(s, n_ref[...], preferred_element_type=F32).astype(BF16)
            cw_scr[pl.ds(start, rows), :] = (jnp.dot(s, hi, preferred_element_type=F32)
                                             + jnp.dot(s, mid, preferred_element_type=F32)
                                             + jnp.dot(s, lo, preferred_element_type=F32))
            yc_scr[pl.ds(start, rows), :] = jnp.zeros((rows, D_MODEL), F32)

        compact(0, main_rows)
        lax.fori_loop(0, ntail, lambda c, carry: (compact(tail_start(c), MOE_ROWS), carry)[1], 0)

    wg = wg_ref[...].astype(BF16)
    wu = wu_ref[...].astype(BF16)
    wd = wd_ref[...].astype(BF16)
    ex = g * EXPERTS_PER_GROUP + e

    def expert(start, rows):
        x = xc_scr[pl.ds(start, rows), :]
        a = jnp.dot(x, wg, preferred_element_type=F32)
        u = jnp.dot(x, wu, preferred_element_type=F32)
        lane = lax.broadcasted_iota(I32, (rows, LANES), 1)
        cw = jnp.sum(jnp.where(lane == ex, cw_scr[pl.ds(start, rows), :], 0.0), axis=-1, keepdims=True)
        hid = (a * jax.nn.sigmoid(a) * u * cw).astype(BF16)
        yc_scr[pl.ds(start, rows), :] += jnp.dot(hid, wd, preferred_element_type=F32)

    expert(0, main_rows)
    lax.fori_loop(0, ntail, lambda c, carry: (expert(tail_start(c), MOE_ROWS), carry)[1], 0)

    @pl.when(last_e)
    def _():
        def scatter(start, rows):
            s = onehot_rows(start, rows)
            y = yc_scr[pl.ds(start, rows), :].astype(BF16)
            o_ref[...] += lax.dot_general(s, y, (((0,), (0,)), ((), ())), preferred_element_type=F32)

        scatter(0, main_rows)
        lax.fori_loop(0, ntail, lambda c, carry: (scatter(tail_start(c), MOE_ROWS), carry)[1], 0)


def _moe_routed(n3, comb, h2d, w_gate_l, w_up_l, w_down_l, *, tm):
    n = h2d.shape[0]
    ff = EXPERT_FF // MOE_FF_SPLIT

    def wmap(i, g, e, f):
        return (g * EXPERTS_PER_GROUP + e, 0, f)

    return pl.pallas_call(
        functools.partial(_moe_routed_kernel, tm=tm),
        grid=(n // tm, N_GROUPS, EXPERTS_PER_GROUP, MOE_FF_SPLIT),
        in_specs=[
            pl.BlockSpec((tm, D_MODEL), lambda i, g, e, f: (i, 0), pipeline_mode=pl.Buffered(1)),
            pl.BlockSpec((tm, LANES), lambda i, g, e, f: (i, 0), pipeline_mode=pl.Buffered(1)),
            pl.BlockSpec(memory_space=pl.ANY),
            pl.BlockSpec((None, D_MODEL, ff), wmap),
            pl.BlockSpec((None, D_MODEL, ff), wmap),
            pl.BlockSpec((None, ff, D_MODEL), lambda i, g, e, f: (g * EXPERTS_PER_GROUP + e, f, 0)),
        ],
        out_specs=pl.BlockSpec((tm, D_MODEL), lambda i, g, e, f: (i, 0)),
        out_shape=jax.ShapeDtypeStruct((n, D_MODEL), F32),
        scratch_shapes=[
            pltpu.VMEM((tm, tm), BF16),
            pltpu.VMEM((SUBLANES, tm), F32),
            pltpu.VMEM((tm, D_MODEL), BF16),
            pltpu.VMEM((tm, D_MODEL), F32),
            pltpu.VMEM((tm, LANES), F32),
            pltpu.SMEM((N_GROUPS,), I32),
            pltpu.SemaphoreType.DMA(()),
        ],
        compiler_params=pltpu.CompilerParams(
            dimension_semantics=("arbitrary",) * 4, vmem_limit_bytes=MOE_VMEM_LIMIT),
        name="moe_routed",
    )(n3, comb, h2d, w_gate_l, w_up_l, w_down_l)


def _moe_kernel(n_ref, c_ref, h_ref, wg_ref, wu_ref, wd_ref, o_ref):
    e = pl.program_id(1)

    @pl.when(e == 0)
    def _():
        o_ref[...] = h_ref[...]

    x = n_ref[...]
    a = jnp.dot(x, wg_ref[...].astype(BF16), preferred_element_type=F32)
    u = jnp.dot(x, wu_ref[...].astype(BF16), preferred_element_type=F32)
    lane = lax.broadcasted_iota(I32, c_ref.shape, 1)
    c = jnp.sum(jnp.where(lane == e, c_ref[...], 0.0), axis=-1, keepdims=True)
    hid = (a * jax.nn.sigmoid(a) * u * c).astype(BF16)
    o_ref[...] += jnp.dot(hid, wd_ref[...].astype(BF16), preferred_element_type=F32)


def _moe(n3, comb, h2d, w_gate_l, w_up_l, w_down_l, *, tm):
    n = h2d.shape[0]
    return pl.pallas_call(
        _moe_kernel,
        grid=(n // tm, N_EXPERTS),
        in_specs=[
            pl.BlockSpec((tm, D_MODEL), lambda i, e: (i, 0)),
            pl.BlockSpec((tm, LANES), lambda i, e: (i, 0)),
            pl.BlockSpec((tm, D_MODEL), lambda i, e: (i, 0)),
            pl.BlockSpec((None, D_MODEL, EXPERT_FF), lambda i, e: (e, 0, 0)),
            pl.BlockSpec((None, D_MODEL, EXPERT_FF), lambda i, e: (e, 0, 0)),
            pl.BlockSpec((None, EXPERT_FF, D_MODEL), lambda i, e: (e, 0, 0)),
        ],
        out_specs=pl.BlockSpec((tm, D_MODEL), lambda i, e: (i, 0)),
        out_shape=jax.ShapeDtypeStruct((n, D_MODEL), F32),
        compiler_params=_cparams(("arbitrary", "arbitrary")),
        name="moe",
    )(n3, comb, h2d, w_gate_l, w_up_l, w_down_l)


def _post_mixer(p, b_out, ws_eff, ab_eff, x2d, mk, mv, lw, *, tm_mix, tm_mem, mem_rows, tiles_per_mem,
                rows_per_mem, mem_len, tm_route, tm_moe):
    h1 = _mixout(p, b_out, ws_eff, ab_eff, x2d, lw["w_out"], tm=tm_mix)
    h2 = _memattn(h1, lw["norm_mem"], lw["mem_wq"], lw["mem_qnorm"], mk, mv, lw["mem_wo"], tm=tm_mem,
                  mem_rows=mem_rows, tiles_per_mem=tiles_per_mem, rows_per_mem=rows_per_mem, mem_len=mem_len)
    n3, comb = _router(h2, lw["norm_ffn"], lw["w_cat"], lw["b_cat"], tm=tm_route)
    moe = _moe_routed if tm_moe > N_GROUPS * MOE_ROWS else _moe
    return moe(n3, comb, h2, lw["w_gate"], lw["w_up"], lw["w_down"], tm=tm_moe)


def kernel(x_prompt, x_sample, cache_k, cache_v, cache_idx_k, cache_mem_k, cache_mem_v, page_table,
           mem_prompt, norm_mix, w_in, a_vnorm, a_ws, a_b, b_qnorm, b_knorm, rel_bias, w_out,
           norm_mem, mem_wq, mem_wk, mem_wv, mem_qnorm, mem_knorm, mem_wo, norm_ffn,
           w_group, b_group, w_router, b_router, w_gate, w_up, w_down):
    bp, t, d = x_prompt.shape
    bd, s, _ = x_sample.shape
    depth = w_in.shape[0]
    mem_len = mem_prompt.shape[1]
    assert d == D_MODEL and t % KEY_CHUNK == 0 and s <= 8 and (bd * s) % LANES == 0

    bias_pairs, sbias, sconst = _bias_tables(rel_bias)
    hp = x_prompt.reshape(bp * t, d)
    hs = x_sample.reshape(bd * s, d)
    outs = [[] for _ in range(9)]
    for l in range(depth):
        w_cat = jnp.zeros((d, 2 * LANES), F32)
        w_cat = w_cat.at[:, :N_GROUPS].set(w_group[l]).at[:, LANES:LANES + N_EXPERTS].set(w_router[l])
        b_cat = jnp.zeros((1, 2 * LANES), F32)
        b_cat = b_cat.at[0, :N_GROUPS].set(b_group[l]).at[0, LANES:LANES + N_EXPERTS].set(b_router[l])
        lw = dict(w_out=w_out[l], norm_mem=norm_mem[l], mem_wq=mem_wq[l], mem_qnorm=mem_qnorm[l],
                  mem_wo=mem_wo[l], norm_ffn=norm_ffn[l], w_cat=w_cat, b_cat=b_cat)

        p, k, v, tail = _proj(hp, norm_mix[l], w_in[l], a_vnorm[l], b_qnorm[l], b_knorm[l],
                              tm=1024, want_vn=False)
        experts = (w_gate[l], w_up[l], w_down[l])
        fuse_cast = (bp * (t // Q_BLOCK)) % N_EXPERTS == 0
        b_out, casted = _dsa_prompt(p, tail, bias_pairs, experts if fuse_cast else (), batch=bp, seq=t)
        if not fuse_cast:
            casted = tuple(_cast_bf16(w) for w in experts)
        lw.update(w_gate=casted[0], w_up=casted[1], w_down=casted[2])
        mk, mv = _memkv(mem_prompt.reshape(bp * mem_len, d), mem_wk[l], mem_wv[l], mem_knorm[l])
        hp = _post_mixer(p, b_out, a_ws[l], a_b[l].T, hp, mk, mv, lw, tm_mix=1024, tm_mem=512,
                         mem_rows=mem_len, tiles_per_mem=t // 512, rows_per_mem=None, mem_len=mem_len,
                         tm_route=512, tm_moe=1024)
        outs[0].append(k.reshape(bp, t, B_KV_HEADS, B_HEAD_DIM))
        outs[1].append(v.reshape(bp, t, B_KV_HEADS, B_HEAD_DIM))
        outs[2].append(tail[:, :IDX_DIM].reshape(bp, t, IDX_DIM))
        outs[3].append(mk.reshape(bp, mem_len, MEM_HEADS, MEM_HEAD_DIM))
        outs[4].append(mv.reshape(bp, mem_len, MEM_HEADS, MEM_HEAD_DIM))

        ns = bd * s
        p_s, k_s, v_s, tail_s, vn_s = _proj(hs, norm_mix[l], w_in[l], a_vnorm[l], b_qnorm[l], b_knorm[l],
                                            tm=ns, want_vn=True)
        b_out_s = _dsa_sample(p_s, tail_s, k_s, v_s, cache_k[l], cache_v[l], cache_idx_k[l], page_table,
                              sbias, sconst, dec_batch=bd, dec_seq=s)
        ws_s = jnp.einsum("ab,gts->gatbs", jnp.eye(ns // s, dtype=F32), a_ws[l][:, :s, :s])
        ws_s = ws_s.reshape(A_GROUPS, ns, ns)
        ab_s = jnp.tile(a_b[l][:, :s].T, (ns // s, 1))
        mem_tile = 32
        hs = _post_mixer(p_s, b_out_s, ws_s, ab_s, hs,
                         cache_mem_k[l].reshape(bd * mem_len * MEM_HEADS, MEM_HEAD_DIM),
                         cache_mem_v[l].reshape(bd * mem_len * MEM_HEADS, MEM_HEAD_DIM), lw,
                         tm_mix=ns, tm_mem=mem_tile, mem_rows=mem_tile // s * mem_len, tiles_per_mem=1,
                         rows_per_mem=s, mem_len=mem_len, tm_route=ns, tm_moe=ns)
        outs[5].append(k_s.reshape(bd, s, B_KV_HEADS, B_HEAD_DIM))
        outs[6].append(v_s.reshape(bd, s, B_KV_HEADS, B_HEAD_DIM))
        outs[7].append(tail_s[:, :IDX_DIM].reshape(bd, s, IDX_DIM))
        outs[8].append(vn_s.reshape(bd, s, A_GROUPS, A_CH))
    return (hp.reshape(bp, t, d), hs.reshape(bd, s, d)) + tuple(jnp.stack(o) for o in outs)
```

```python
import functools
import math

import jax
import jax.numpy as jnp
from jax import lax
from jax.experimental import pallas as pl
from jax.experimental.pallas import tpu as pltpu

BF16 = jnp.bfloat16
F32 = jnp.float32
I32 = jnp.int32

D_MODEL = 2048
A_GROUPS = 8
A_CH = 128
A_WIDTH = A_GROUPS * A_CH
A_CHUNK = 128
B_HEADS = 8
B_HEAD_DIM = 128
B_KV_HEADS = 2
B_GQA = B_HEADS // B_KV_HEADS
B_WIDTH = B_HEADS * B_HEAD_DIM
KV_WIDTH = B_KV_HEADS * B_HEAD_DIM
IDX_HEADS = 16
IDX_DIM = 64
TOPK_MAX = 256
Q_BLOCK = 128
REL_BUCKETS = 32
REL_MAX_DIST = 128
MEM_HEADS = 4
MEM_HEAD_DIM = 128
MEM_WIDTH = MEM_HEADS * MEM_HEAD_DIM
N_GROUPS = 4
EXPERTS_PER_GROUP = 4
N_EXPERTS = N_GROUPS * EXPERTS_PER_GROUP
EXPERT_FF = 512
PAGE_SIZE = 128
EPS = 1e-6

COL_U = 0
COL_V = COL_U + A_WIDTH
COL_Q = COL_V + A_WIDTH
COL_K = COL_Q + B_WIDTH
COL_VV = COL_K + KV_WIDTH
COL_QI = COL_VV + KV_WIDTH
COL_MAIN = COL_QI + IDX_HEADS * IDX_DIM
TAIL_COLS = IDX_DIM + IDX_HEADS
LANES = 128
SUBLANES = 8
PROJ_TN = 512
PROJ_ROW_CHUNK = 256
KEY_CHUNK = 256
NEG_INIT = -1e30
VMEM_LIMIT = 56 * 1024 * 1024
MOE_VMEM_LIMIT = 60 * 1024 * 1024


def _cparams(sem):
    return pltpu.CompilerParams(dimension_semantics=sem, vmem_limit_bytes=VMEM_LIMIT)


def _gelu(x):
    return 0.5 * x * (1.0 + lax.erf(x * (2.0 ** -0.5)))


def _head_norm(a, gain, width=LANES):
    outs = []
    for c in range(a.shape[1] // width):
        blk = a[:, c * width:(c + 1) * width]
        ms = jnp.mean(blk * blk, axis=-1, keepdims=True)
        outs.append(blk * lax.rsqrt(ms + EPS) * gain[:, c * width:(c + 1) * width])
    return outs[0] if len(outs) == 1 else jnp.concatenate(outs, axis=-1)


def _row_norm(x, gain):
    ms = jnp.mean(x * x, axis=-1, keepdims=True)
    return x * lax.rsqrt(ms + EPS) * gain


def _dot_nt(a, b):
    return lax.dot_general(a, b, (((1,), (1,)), ((), ())), preferred_element_type=F32)


def _proj_kernel(x_ref, g_ref, w_ref, wt_ref, avn_ref, qn_ref, kn_ref, *rest, want_vn):
    if want_vn:
        p_ref, k_ref, v_ref, tail_ref, vn_ref, xn_scr = rest
    else:
        p_ref, k_ref, v_ref, tail_ref, xn_scr = rest
        vn_ref = None
    j = pl.program_id(1)
    ju = COL_V // PROJ_TN
    jv = COL_Q // PROJ_TN
    jq = COL_K // PROJ_TN
    jkv = COL_QI // PROJ_TN

    @pl.when(j == 0)
    def _():
        xn = _row_norm(x_ref[...], g_ref[...]).astype(BF16)
        xn_scr[...] = xn
        wlane = lax.broadcasted_iota(I32, wt_ref.shape, 1)
        wt = jnp.where(wlane < TAIL_COLS, wt_ref[...], 0.0).astype(BF16)
        t = jnp.dot(xn, wt, preferred_element_type=F32)
        lane = lax.broadcasted_iota(I32, t.shape, 1)
        tail_ref[...] = jnp.where(lane >= IDX_DIM, t * (IDX_HEADS ** -0.5), t)

    tm = xn_scr.shape[0]
    rc = min(tm, PROJ_ROW_CHUNK)

    def row_chunks(epilogue):
        w = w_ref[...].astype(BF16)
        for r in range(tm // rc):
            rs = slice(r * rc, (r + 1) * rc)
            epilogue(jnp.dot(xn_scr[rs, :], w, preferred_element_type=F32), rs)

    @pl.when(j < ju)
    def _():
        def epi(acc, rs):
            p_ref[rs, :] = _gelu(acc).astype(BF16)
        row_chunks(epi)

    @pl.when((j >= ju) & (j < jv))
    def _():
        def epi(acc, rs):
            vn = _head_norm(_gelu(acc), avn_ref[...])
            p_ref[rs, :] = vn.astype(BF16)
            if vn_ref is not None:
                vn_ref[rs, :] = vn
        row_chunks(epi)

    @pl.when((j >= jv) & (j < jq))
    def _():
        def epi(acc, rs):
            p_ref[rs, :] = _head_norm(acc, qn_ref[...]).astype(BF16)
        row_chunks(epi)

    @pl.when(j == jq)
    def _():
        def epi(acc, rs):
            k = _head_norm(acc[:, :KV_WIDTH], kn_ref[...])
            v = acc[:, KV_WIDTH:]
            k_ref[rs, :] = k
            v_ref[rs, :] = v
            p_ref[rs, :] = jnp.concatenate([k, v], axis=-1).astype(BF16)
        row_chunks(epi)

    @pl.when(j >= jkv)
    def _():
        def epi(acc, rs):
            p_ref[rs, :] = acc.astype(BF16)
        row_chunks(epi)


def _proj(x2d, gain, w_in_l, a_vnorm_l, b_qnorm_l, b_knorm_l, *, tm, want_vn):
    n = x2d.shape[0]
    nj = COL_MAIN // PROJ_TN
    avn = a_vnorm_l.reshape(1, A_WIDTH)
    qn = jnp.tile(b_qnorm_l, PROJ_TN // B_HEAD_DIM).reshape(1, PROJ_TN)
    kn = jnp.tile(b_knorm_l, B_KV_HEADS).reshape(1, KV_WIDTH)
    ju = COL_V // PROJ_TN
    nv = A_WIDTH // PROJ_TN

    def vmap_(i, j):
        return (0, jnp.clip(j - ju, 0, nv - 1))

    in_specs = [
        pl.BlockSpec((tm, D_MODEL), lambda i, j: (i, 0)),
        pl.BlockSpec((1, D_MODEL), lambda i, j: (0, 0)),
        pl.BlockSpec((D_MODEL, PROJ_TN), lambda i, j: (0, j)),
        pl.BlockSpec((D_MODEL, LANES), lambda i, j: (0, COL_MAIN // LANES)),
        pl.BlockSpec((1, PROJ_TN), vmap_),
        pl.BlockSpec((1, PROJ_TN), lambda i, j: (0, 0)),
        pl.BlockSpec((1, KV_WIDTH), lambda i, j: (0, 0)),
    ]
    out_shape = [
        jax.ShapeDtypeStruct((n, COL_MAIN), BF16),
        jax.ShapeDtypeStruct((n, KV_WIDTH), F32),
        jax.ShapeDtypeStruct((n, KV_WIDTH), F32),
        jax.ShapeDtypeStruct((n, LANES), F32),
    ]
    out_specs = [
        pl.BlockSpec((tm, PROJ_TN), lambda i, j: (i, j)),
        pl.BlockSpec((tm, KV_WIDTH), lambda i, j: (i, 0)),
        pl.BlockSpec((tm, KV_WIDTH), lambda i, j: (i, 0)),
        pl.BlockSpec((tm, LANES), lambda i, j: (i, 0)),
    ]
    if want_vn:
        out_shape.append(jax.ShapeDtypeStruct((n, A_WIDTH), F32))
        out_specs.append(pl.BlockSpec((tm, PROJ_TN), lambda i, j: (i, jnp.clip(j - ju, 0, nv - 1))))
    return pl.pallas_call(
        functools.partial(_proj_kernel, want_vn=want_vn),
        grid=(n // tm, nj),
        in_specs=in_specs,
        out_specs=out_specs,
        out_shape=out_shape,
        scratch_shapes=[pltpu.VMEM((tm, D_MODEL), BF16)],
        compiler_params=_cparams(("arbitrary", "arbitrary")),
        name="proj",
    )(x2d, gain.reshape(1, D_MODEL), w_in_l, w_in_l, avn, qn, kn)


def _t5_bucket(d):
    max_exact = REL_BUCKETS // 2
    d = jnp.maximum(d, 0)
    ratio = jnp.log(jnp.maximum(d, 1).astype(F32) / max_exact) / math.log(REL_MAX_DIST / max_exact)
    large = jnp.minimum(max_exact + jnp.floor(ratio * (REL_BUCKETS - max_exact)).astype(I32), REL_BUCKETS - 1)
    return jnp.where(d < max_exact, d, large)


def _bias_lookup(bucket, rb_ref, h):
    acc = jnp.zeros(bucket.shape, F32)
    for r in range(REL_BUCKETS):
        acc = jnp.where(bucket == r, rb_ref[r, h], acc)
    return acc


def _bias_kernel(rb_ref, pair_ref, samp_ref, sconst_ref):
    q = lax.broadcasted_iota(I32, (Q_BLOCK, Q_BLOCK), 0)
    kc = lax.broadcasted_iota(I32, (Q_BLOCK, Q_BLOCK), 1)
    nvar = pair_ref.shape[0]
    rels = range(-(nvar - 1), 2)
    buckets = {r: _t5_bucket(q - kc - r * Q_BLOCK) for r in rels}
    for h in range(B_HEADS):
        g, hh = divmod(h, B_GQA)
        tiles = {r: _bias_lookup(buckets[r], rb_ref, h) for r in rels}
        for var in range(nvar):
            r0 = var - (nvar - 1)
            for t in range(2):
                pair_ref[var, g, hh * Q_BLOCK:(hh + 1) * Q_BLOCK, t * Q_BLOCK:(t + 1) * Q_BLOCK] = tiles[r0 + t]
    rows = B_GQA * SUBLANES
    row = lax.broadcasted_iota(I32, (rows, 2 * PAGE_SIZE), 0)
    col = lax.broadcasted_iota(I32, (rows, 2 * PAGE_SIZE), 1)
    t = row % SUBLANES
    d = jnp.where(col < PAGE_SIZE, t + PAGE_SIZE - col, t - (col - PAGE_SIZE))
    bucket = _t5_bucket(d)
    far = jnp.full((rows, LANES), REL_BUCKETS - 1, I32)
    rowc = lax.broadcasted_iota(I32, (rows, LANES), 0)
    for g in range(B_KV_HEADS):
        acc = jnp.zeros((rows, 2 * PAGE_SIZE), F32)
        accc = jnp.zeros((rows, LANES), F32)
        for hh in range(B_GQA):
            h = g * B_GQA + hh
            acc = jnp.where(row // SUBLANES == hh, _bias_lookup(bucket, rb_ref, h), acc)
            accc = jnp.where(rowc // SUBLANES == hh, _bias_lookup(far, rb_ref, h), accc)
        samp_ref[g] = acc
        sconst_ref[g] = accc


N_PAIR_VARIANTS = 4


def _bias_tables(rel_bias):
    rows = B_GQA * SUBLANES
    return pl.pallas_call(
        _bias_kernel,
        in_specs=[pl.BlockSpec(memory_space=pltpu.SMEM)],
        out_shape=[
            jax.ShapeDtypeStruct((N_PAIR_VARIANTS, B_KV_HEADS, B_GQA * Q_BLOCK, KEY_CHUNK), F32),
            jax.ShapeDtypeStruct((B_KV_HEADS, rows, 2 * PAGE_SIZE), F32),
            jax.ShapeDtypeStruct((B_KV_HEADS, rows, LANES), F32),
        ],
        name="bias_tables",
    )(rel_bias)


def _key_to_f32(key):
    bits = jnp.where(key < 0, key ^ I32(-2 ** 31), ~key)
    return lax.bitcast_convert_type(bits, F32)


def _kth_largest(count_ge, shape, k):
    def body(it, prefix):
        bit = 31 - it
        cand = prefix | lax.shift_left(I32(1), bit)
        cnt = count_ge(_key_to_f32(cand))
        return jnp.where(cnt >= k, cand, prefix)

    prefix = lax.fori_loop(0, 32, body, jnp.zeros(shape, I32))
    thr = _key_to_f32(prefix)
    return jnp.where((prefix & I32(-2 ** 23)) == 0, -jnp.inf, thr)


NO_INDEX_BOUND = 2 ** 30
TIE_REPAIR_ROUNDS = 2


def _topk_select(fold_sum, fold_min, shape, k, index_bits):
    thr = _kth_largest(lambda t: fold_sum(lambda sc, kp: sc >= t), shape, k)
    c_ge = fold_sum(lambda sc, kp: sc >= thr)
    no_bound = jnp.full(shape, NO_INDEX_BOUND, I32)

    def repair(_):
        v = thr
        strict = jnp.zeros(shape, F32)

        def in_set(sc, v, strict):
            return (sc > v) | ((sc == v) & (strict < 0.5))

        for _ in range(TIE_REPAIR_ROUNDS):
            c_set = fold_sum(lambda sc, kp: in_set(sc, v, strict))
            vmin = fold_min(lambda sc, kp: jnp.where(in_set(sc, v, strict), sc, jnp.inf))
            c_gt = fold_sum(lambda sc, kp: sc > vmin)
            drop = (c_set > k) & (c_gt >= k)
            v = jnp.where(drop, vmin, v)
            strict = jnp.where(drop, 1.0, strict)
        c_set = fold_sum(lambda sc, kp: in_set(sc, v, strict))
        vt = fold_min(lambda sc, kp: jnp.where(in_set(sc, v, strict), sc, jnp.inf))
        need = k - fold_sum(lambda sc, kp: sc > vt)

        def idx_body(it, m):
            cand = m | lax.shift_left(I32(1), index_bits - 1 - it)
            below = fold_sum(lambda sc, kp: (sc == vt) & (kp < cand))
            return jnp.where(below < need, cand, m)

        m = lax.fori_loop(0, index_bits, idx_body, jnp.zeros(shape, I32))
        tie = c_set > k
        v_out = jnp.where(tie, vt, v)
        m_out = jnp.where(tie, m, jnp.where(strict > 0.5, -1, no_bound))
        return v_out, m_out

    return lax.cond(jnp.max(c_ge) > k, repair, lambda _: (thr, no_bound), 0)


def _chunk_loop(n, body, init):
    carry = lax.fori_loop(0, n // 2, lambda j, c: body(2 * j + 1, body(2 * j, c)), init)
    return lax.cond(n % 2 == 1, lambda c: body(n - 1, c), lambda c: c, carry)


def _dsa_prompt_kernel(q_ref, kv_ref, qia_ref, qib_ref, tailk_ref, tailq_ref, bias_ref, *rest, topk, n_cast):
    cast_in, rest = rest[:n_cast], rest[n_cast:]
    o_ref, rest = rest[0], rest[1:]
    cast_out, rest = rest[:n_cast], rest[n_cast:]
    kil_scr, kir_scr, sc_scr, lg_scr, m_scr, l_scr, acc_scr = rest
    i = pl.program_id(1)

    for src, dst in zip(cast_in, cast_out):
        dst[...] = src[...].astype(BF16)

    @pl.when(i == 0)
    def _():
        ki = tailk_ref[:, :IDX_DIM].astype(BF16)
        z = jnp.zeros_like(ki)
        kil_scr[...] = jnp.concatenate([ki, z], axis=-1)
        kir_scr[...] = jnp.concatenate([z, ki], axis=-1)

    wi = tailq_ref[:, IDX_DIM:IDX_DIM + IDX_HEADS] * (IDX_DIM ** -0.5)
    nk = (i + 2) // 2
    qpos = i * Q_BLOCK + lax.broadcasted_iota(I32, (Q_BLOCK, 1), 0)
    kloc = lax.broadcasted_iota(I32, (1, KEY_CHUNK), 1)
    heads_per_ref = qia_ref.shape[1] // IDX_DIM

    pairs = []
    for hp in range(IDX_HEADS // 2):
        ref = qia_ref if 2 * hp < heads_per_ref else qib_ref
        base = (2 * hp) % heads_per_ref * IDX_DIM
        pairs.append(ref[:, base:base + 2 * IDX_DIM])
    qstack = jnp.concatenate(pairs, axis=0)

    def score_chunk(c, carry):
        off = pl.multiple_of(c * KEY_CHUNK, KEY_CHUNK)
        acc = jnp.zeros((Q_BLOCK, KEY_CHUNK), F32)
        for side, k_scr in ((0, kil_scr), (1, kir_scr)):
            s = _dot_nt(qstack, k_scr[pl.ds(off, KEY_CHUNK), :])
            for hp in range(IDX_HEADS // 2):
                h = 2 * hp + side
                acc = acc + wi[:, h:h + 1] * jnp.maximum(s[hp * Q_BLOCK:(hp + 1) * Q_BLOCK], 0.0)
        kpos = c * KEY_CHUNK + kloc
        sc_scr[c] = jnp.where(kpos <= qpos, acc, -jnp.inf)
        return carry

    _chunk_loop(nk, score_chunk, 0)

    def fold_sum(pred):
        def body(c, cnt):
            m = jnp.where(pred(sc_scr[c], c * KEY_CHUNK + kloc), 1.0, 0.0)
            return cnt + m[:, :LANES] + m[:, LANES:]
        cnt = _chunk_loop(nk, body, jnp.zeros((Q_BLOCK, LANES), F32))
        return jnp.sum(cnt, axis=-1, keepdims=True)

    def fold_min(val):
        def body(c, acc):
            x = val(sc_scr[c], c * KEY_CHUNK + kloc)
            return jnp.minimum(acc, jnp.minimum(x[:, :LANES], x[:, LANES:]))
        acc = lax.fori_loop(0, nk, body, jnp.full((Q_BLOCK, LANES), jnp.inf, F32))
        return jnp.min(acc, axis=-1, keepdims=True)

    index_bits = (kv_ref.shape[0] - 1).bit_length()
    sel_v, sel_m = _topk_select(fold_sum, fold_min, (Q_BLOCK, 1), topk, index_bits)

    def mask_chunk(c, carry):
        kpos = c * KEY_CHUNK + kloc
        sc = sc_scr[c]
        chosen = (sc > sel_v) | ((sc == sel_v) & (kpos <= sel_m))
        sc_scr[c] = jnp.where(chosen & (kpos <= qpos), 0.0, -jnp.inf)
        return carry

    lax.fori_loop(0, nk, mask_chunk, 0)

    scale = B_HEAD_DIM ** -0.5
    rows = B_GQA * Q_BLOCK
    qgs = [jnp.concatenate(
        [q_ref[:, (g * B_GQA + hh) * B_HEAD_DIM:(g * B_GQA + hh + 1) * B_HEAD_DIM] for hh in range(B_GQA)],
        axis=0) for g in range(B_KV_HEADS)]
    m_scr[...] = jnp.full(m_scr.shape, -jnp.inf, F32)
    l_scr[...] = jnp.zeros(l_scr.shape, F32)
    acc_scr[...] = jnp.zeros(acc_scr.shape, F32)

    def logits_chunk(c, carry):
        off = pl.multiple_of(c * KEY_CHUNK, KEY_CHUNK)
        var = jnp.clip(2 * c - i, -(N_PAIR_VARIANTS - 1), 0) + (N_PAIR_VARIANTS - 1)
        sel = sc_scr[c]
        sel = jnp.concatenate([sel] * B_GQA, axis=0)
        for g in range(B_KV_HEADS):
            kc = kv_ref[pl.ds(off, KEY_CHUNK), g * B_HEAD_DIM:(g + 1) * B_HEAD_DIM]
            lg = _dot_nt(qgs[g], kc) * scale + bias_ref[var, g] + sel
            lg_scr[c, g] = lg
            m_scr[g] = jnp.maximum(m_scr[g], jnp.maximum(lg[:, :LANES], lg[:, LANES:]))
        return carry

    _chunk_loop(nk, logits_chunk, 0)
    for g in range(B_KV_HEADS):
        m = jnp.max(m_scr[g], axis=-1, keepdims=True)
        m_scr[g] = jnp.broadcast_to(m, (rows, LANES))

    def pv_chunk(c, carry):
        off = pl.multiple_of(c * KEY_CHUNK, KEY_CHUNK)
        for g in range(B_KV_HEADS):
            vc = kv_ref[pl.ds(off, KEY_CHUNK), KV_WIDTH + g * B_HEAD_DIM:KV_WIDTH + (g + 1) * B_HEAD_DIM]
            mb = m_scr[g]
            p = jnp.exp(lg_scr[c, g] - jnp.concatenate([mb, mb], axis=-1))
            l_scr[g] += p[:, :LANES] + p[:, LANES:]
            acc_scr[g] += jnp.dot(p.astype(BF16), vc, preferred_element_type=F32)
        return carry

    _chunk_loop(nk, pv_chunk, 0)
    for g in range(B_KV_HEADS):
        o = acc_scr[g] / jnp.sum(l_scr[g], axis=-1, keepdims=True)
        for hh in range(B_GQA):
            h = g * B_GQA + hh
            o_ref[:, h * B_HEAD_DIM:(h + 1) * B_HEAD_DIM] = o[hh * Q_BLOCK:(hh + 1) * Q_BLOCK].astype(BF16)


def _dsa_prompt(p, tail, bias_pairs, cast_weights, *, batch, seq):
    topk = min(TOPK_MAX, seq // 4)
    nblk = seq // Q_BLOCK
    qi_w = PROJ_TN
    steps = batch * nblk
    cast_specs, cast_shapes = [], []
    for w in cast_weights:
        ne, r, c = w.shape
        parts = steps // ne
        assert steps == ne * parts and r % (parts * 16) == 0
        spec = pl.BlockSpec((None, r // parts, c),
                            lambda b, i, parts=parts: ((b * nblk + i) // parts, (b * nblk + i) % parts, 0))
        cast_specs.append(spec)
        cast_shapes.append(jax.ShapeDtypeStruct(w.shape, BF16))
    outs = pl.pallas_call(
        functools.partial(_dsa_prompt_kernel, topk=topk, n_cast=len(cast_weights)),
        grid=(batch, nblk),
        in_specs=[
            pl.BlockSpec((Q_BLOCK, B_WIDTH), lambda b, i: (b * nblk + i, COL_Q // B_WIDTH)),
            pl.BlockSpec((seq, 2 * KV_WIDTH), lambda b, i: (b, COL_K // (2 * KV_WIDTH))),
            pl.BlockSpec((Q_BLOCK, qi_w), lambda b, i: (b * nblk + i, COL_QI // qi_w)),
            pl.BlockSpec((Q_BLOCK, qi_w), lambda b, i: (b * nblk + i, COL_QI // qi_w + 1)),
            pl.BlockSpec((seq, LANES), lambda b, i: (b, 0)),
            pl.BlockSpec((Q_BLOCK, LANES), lambda b, i: (b * nblk + i, 0)),
            pl.BlockSpec(bias_pairs.shape, lambda b, i: (0, 0, 0, 0)),
        ] + cast_specs,
        out_specs=[pl.BlockSpec((Q_BLOCK, B_WIDTH), lambda b, i: (b * nblk + i, 0))] + cast_specs,
        out_shape=[jax.ShapeDtypeStruct((batch * seq, B_WIDTH), BF16)] + cast_shapes,
        scratch_shapes=[
            pltpu.VMEM((seq, 2 * IDX_DIM), BF16),
            pltpu.VMEM((seq, 2 * IDX_DIM), BF16),
            pltpu.VMEM((seq // KEY_CHUNK, Q_BLOCK, KEY_CHUNK), F32),
            pltpu.VMEM((seq // KEY_CHUNK, B_KV_HEADS, B_GQA * Q_BLOCK, KEY_CHUNK), F32),
            pltpu.VMEM((B_KV_HEADS, B_GQA * Q_BLOCK, LANES), F32),
            pltpu.VMEM((B_KV_HEADS, B_GQA * Q_BLOCK, LANES), F32),
            pltpu.VMEM((B_KV_HEADS, B_GQA * Q_BLOCK, B_HEAD_DIM), F32),
        ],
        compiler_params=_cparams(("arbitrary", "arbitrary")),
        name="dsa_prompt",
    )(p, p, p, p, tail, tail, bias_pairs, *cast_weights)
    return outs[0], tuple(outs[1:])


PAGES_PER_STEP = 16
SAMPLE_SUB_PAGES = 16
SCORE_KEYS_PER_DOT = 2048
FOLD_WAYS = 8


def _fetch_pages(pt_ref, sources, bufs, sem, step, n_steps_total, steps_per_sample, pages_per_step):
    slot = step % 2

    def copies(n, slot_, page_of):
        sample = n // steps_per_sample
        first = (n % steps_per_sample) * pages_per_step
        for r in range(pages_per_step):
            page = page_of(sample, first + r)
            for a, (src, buf) in enumerate(zip(sources, bufs)):
                yield pltpu.make_async_copy(src.at[page], buf.at[slot_, r], sem.at[a, slot_])

    def table(sample, idx):
        return pt_ref[sample, idx]

    @pl.when(step == 0)
    def _():
        for cp in copies(step, slot, table):
            cp.start()

    @pl.when(step + 1 < n_steps_total)
    def _():
        for cp in copies(step + 1, 1 - slot, table):
            cp.start()

    for cp in copies(step, slot, lambda sample, idx: 0):
        cp.wait()
    return slot


def _dsa_sample_score_kernel(pt_ref, idx_hbm, qi_ref, wi_ref, kin_ref, sc_ref, scn_ref, selv_ref, selm_ref,
                             page_buf, sem, *, topk, dec_seq, n_pages, n_samples):
    slot = _fetch_pages(pt_ref, [idx_hbm], [page_buf], sem, pl.program_id(0), n_samples, 1, n_pages)
    pages = [page_buf.at[slot, r] for r in range(n_pages)]
    past = n_pages * PAGE_SIZE
    pages_per_dot = SCORE_KEYS_PER_DOT // PAGE_SIZE
    qi = qi_ref[...]
    wi = wi_ref[...] * (IDX_DIM ** -0.5)

    def scores(keys_t):
        r = jnp.maximum(jnp.dot(qi, keys_t, preferred_element_type=F32), 0.0) * wi
        acc = r[0:SUBLANES]
        for j in range(1, IDX_HEADS * dec_seq // SUBLANES):
            acc = acc + r[j * SUBLANES:(j + 1) * SUBLANES]
        shift = dec_seq
        while shift < SUBLANES:
            acc = acc + pltpu.roll(acc, shift, axis=0)
            shift *= 2
        return acc

    for d in range(n_pages // pages_per_dot):
        keys_t = jnp.concatenate(
            [pg[...].astype(BF16) for pg in pages[d * pages_per_dot:(d + 1) * pages_per_dot]], axis=1)
        sc_ref[:, d * SCORE_KEYS_PER_DOT:(d + 1) * SCORE_KEYS_PER_DOT] = scores(keys_t)

    row = lax.broadcasted_iota(I32, (SUBLANES, LANES), 0)
    col = lax.broadcasted_iota(I32, (SUBLANES, LANES), 1)
    scn = jnp.where((col <= row % dec_seq) & (col < dec_seq), scores(kin_ref[...]), -jnp.inf)
    scn_ref[...] = scn

    def pieces():
        yield scn, past + col[0:1]
        for w in range(past // LANES):
            yield sc_ref[:, w * LANES:(w + 1) * LANES], w * LANES + col[0:1]

    def fold(term, combine):
        parts = [None] * FOLD_WAYS
        for n, (sc, kp) in enumerate(pieces()):
            x = term(sc, kp)
            parts[n % FOLD_WAYS] = x if parts[n % FOLD_WAYS] is None else combine(parts[n % FOLD_WAYS], x)
        parts = [p for p in parts if p is not None]
        while len(parts) > 1:
            parts = [combine(parts[j], parts[j + 1]) if j + 1 < len(parts) else parts[j]
                     for j in range(0, len(parts), 2)]
        return parts[0]

    def fold_sum(pred):
        cnt = fold(lambda sc, kp: jnp.where(pred(sc, kp), 1.0, 0.0), jnp.add)
        return jnp.sum(cnt, axis=-1, keepdims=True)

    def fold_min(val):
        return jnp.min(fold(val, jnp.minimum), axis=-1, keepdims=True)

    index_bits = (past + LANES - 1).bit_length()
    sel_v, sel_m = _topk_select(fold_sum, fold_min, (SUBLANES, 1), topk, index_bits)
    selv_ref[...] = jnp.broadcast_to(sel_v, (SUBLANES, LANES))
    selm_ref[...] = jnp.broadcast_to(sel_m, (SUBLANES, LANES))


def _dsa_sample_attend_kernel(pt_ref, ck_hbm, cv_hbm, q_ref, sc_ref, scn_ref, selv_ref, selm_ref, kvn_ref,
                              sbias_ref, sconst_ref, o_ref, kbuf, vbuf, sem, m_scr, l_scr, acc_scr,
                              *, dec_seq, n_steps, n_samples):
    s = pl.program_id(1)
    slot = _fetch_pages(pt_ref, [ck_hbm, cv_hbm], [kbuf, vbuf], sem, pl.program_id(0) * n_steps + s,
                        n_samples * n_steps, n_steps, PAGES_PER_STEP)
    kpages = [kbuf.at[slot, r] for r in range(PAGES_PER_STEP)]
    vpages = [vbuf.at[slot, r] for r in range(PAGES_PER_STEP)]
    chunk = PAGES_PER_STEP * PAGE_SIZE
    rows = B_GQA * SUBLANES
    scale = B_HEAD_DIM ** -0.5

    @pl.when(s == 0)
    def _():
        m_scr[...] = jnp.full(m_scr.shape, NEG_INIT, F32)
        l_scr[...] = jnp.zeros(l_scr.shape, F32)
        acc_scr[...] = jnp.zeros(acc_scr.shape, F32)

    sel_v = selv_ref[:, 0:1]
    sel_m = selm_ref[:, 0:1]

    def chosen(sc, kpos):
        return jnp.where((sc > sel_v) | ((sc == sel_v) & (kpos <= sel_m)), 0.0, -jnp.inf)

    def head_rows(pgs, g):
        return jnp.concatenate(
            [pg[pl.ds(g, PAGE_SIZE, stride=B_KV_HEADS), :].astype(BF16) for pg in pgs], axis=0)

    kcol = lax.broadcasted_iota(I32, (SUBLANES, chunk), 1)
    sel = chosen(sc_ref[...], s * chunk + kcol)
    sel = jnp.concatenate([sel] * B_GQA, axis=0)
    is_last = s == n_steps - 1
    ncol = lax.broadcasted_iota(I32, (SUBLANES, LANES), 1)
    seln = chosen(scn_ref[...], n_steps * chunk + ncol)
    seln = jnp.where(is_last & (scn_ref[...] > -jnp.inf), seln, -jnp.inf)
    seln = jnp.concatenate([seln] * B_GQA, axis=0)

    def partial(lg, v):
        m = jnp.maximum(jnp.max(lg, axis=-1, keepdims=True), NEG_INIT)
        p = jnp.exp(lg - m)
        return m, jnp.sum(p, axis=-1, keepdims=True), jnp.dot(p.astype(BF16), v, preferred_element_type=F32)

    sub = SAMPLE_SUB_PAGES * PAGE_SIZE
    nsub = PAGES_PER_STEP // SAMPLE_SUB_PAGES
    lane_sub = lax.broadcasted_iota(I32, (rows, sub), 1)
    for g in range(B_KV_HEADS):
        qg = q_ref[g]
        far = sconst_ref[g, :, 0:1]
        near = jnp.concatenate([jnp.zeros((rows, sub - PAGE_SIZE), F32), sbias_ref[g, :, :PAGE_SIZE]], axis=-1)
        parts = [(m_scr[g], l_scr[g], acc_scr[g])]
        for j in range(nsub):
            pj = slice(j * SAMPLE_SUB_PAGES, (j + 1) * SAMPLE_SUB_PAGES)
            lg = _dot_nt(qg, head_rows(kpages[pj], g)) * scale + sel[:, j * sub:(j + 1) * sub]
            if j == nsub - 1:
                lg = lg + jnp.where(is_last & (lane_sub >= sub - PAGE_SIZE), near, far)
            else:
                lg = lg + far
            parts.append(partial(lg, head_rows(vpages[pj], g)))
        kn = kvn_ref[:, g * B_HEAD_DIM:(g + 1) * B_HEAD_DIM]
        vn = kvn_ref[:, KV_WIDTH + g * B_HEAD_DIM:KV_WIDTH + (g + 1) * B_HEAD_DIM]
        parts.append(partial(_dot_nt(qg, kn) * scale + sbias_ref[g, :, PAGE_SIZE:] + seln, vn))
        m_new = parts[0][0]
        for m, _, _ in parts[1:]:
            m_new = jnp.maximum(m_new, m)
        l_new = jnp.zeros((rows, 1), F32)
        acc_new = jnp.zeros((rows, B_HEAD_DIM), F32)
        for m, l, acc in parts:
            w = jnp.exp(m - m_new)
            l_new = l_new + w * l
            acc_new = acc_new + w * acc
        m_scr[g] = m_new
        l_scr[g] = l_new
        acc_scr[g] = acc_new

    @pl.when(is_last)
    def _():
        for g in range(B_KV_HEADS):
            o_ref[g] = (acc_scr[g] / l_scr[g]).astype(BF16)


def _dsa_sample(p_s, tail_s, k_s, v_s, cache_k_l, cache_v_l, cache_idx_l, page_table, sbias, sconst,
                *, dec_batch, dec_seq):
    n_pages = page_table.shape[1]
    past = n_pages * PAGE_SIZE
    topk = min(TOPK_MAX, (past + dec_seq) // 4)
    n_steps = n_pages // PAGES_PER_STEP
    chunk = PAGES_PER_STEP * PAGE_SIZE
    rows = B_GQA * SUBLANES
    n_pool = cache_k_l.shape[0]

    assert SUBLANES % dec_seq == 0
    qi = p_s[:, COL_QI:COL_MAIN].reshape(dec_batch, dec_seq, IDX_HEADS, IDX_DIM).transpose(0, 2, 1, 3)
    qi = qi.reshape(dec_batch, IDX_HEADS * dec_seq, IDX_DIM)
    wi = tail_s[:, IDX_DIM:IDX_DIM + IDX_HEADS].reshape(dec_batch, dec_seq, IDX_HEADS).transpose(0, 2, 1)
    wi = wi.reshape(dec_batch, IDX_HEADS * dec_seq, 1)
    ki_new = tail_s[:, :IDX_DIM].astype(BF16).reshape(dec_batch, dec_seq, IDX_DIM)
    ki_new = jnp.pad(ki_new, ((0, 0), (0, LANES - dec_seq), (0, 0))).transpose(0, 2, 1)
    q = p_s[:, COL_Q:COL_K].reshape(dec_batch, dec_seq, B_KV_HEADS, B_GQA, B_HEAD_DIM)
    q = jnp.pad(q.transpose(0, 2, 3, 1, 4), ((0, 0), (0, 0), (0, 0), (0, SUBLANES - dec_seq), (0, 0)))
    q = q.reshape(dec_batch, B_KV_HEADS, rows, B_HEAD_DIM)
    kv_new = jnp.concatenate([k_s, v_s], axis=-1).astype(BF16).reshape(dec_batch, dec_seq, 2 * KV_WIDTH)
    kv_new = jnp.pad(kv_new, ((0, 0), (0, LANES - dec_seq), (0, 0)))

    cache_idx_t = jnp.swapaxes(cache_idx_l, 1, 2)
    stat_spec = pl.BlockSpec((None, SUBLANES, LANES), lambda b, pt: (b, 0, 0))
    sc, scn, selv, selm = pl.pallas_call(
        functools.partial(_dsa_sample_score_kernel, topk=topk, dec_seq=dec_seq, n_pages=n_pages,
                          n_samples=dec_batch),
        grid_spec=pltpu.PrefetchScalarGridSpec(
            num_scalar_prefetch=1,
            grid=(dec_batch,),
            in_specs=[
                pl.BlockSpec(memory_space=pl.ANY),
                pl.BlockSpec((None, IDX_HEADS * dec_seq, IDX_DIM), lambda b, pt: (b, 0, 0)),
                pl.BlockSpec((None, IDX_HEADS * dec_seq, 1), lambda b, pt: (b, 0, 0)),
                pl.BlockSpec((None, IDX_DIM, LANES), lambda b, pt: (b, 0, 0)),
            ],
            out_specs=[pl.BlockSpec((None, SUBLANES, past), lambda b, pt: (b, 0, 0)),
                       stat_spec, stat_spec, stat_spec],
            scratch_shapes=[
                pltpu.VMEM((2, n_pages, IDX_DIM, PAGE_SIZE), F32),
                pltpu.SemaphoreType.DMA((1, 2)),
            ],
        ),
        out_shape=[
            jax.ShapeDtypeStruct((dec_batch, SUBLANES, past), F32),
            jax.ShapeDtypeStruct((dec_batch, SUBLANES, LANES), F32),
            jax.ShapeDtypeStruct((dec_batch, SUBLANES, LANES), F32),
            jax.ShapeDtypeStruct((dec_batch, SUBLANES, LANES), I32),
        ],
        compiler_params=_cparams(("arbitrary",)),
        name="dsa_sample_score",
    )(page_table, cache_idx_t, qi, wi, ki_new)

    ck = cache_k_l.reshape(n_pool, PAGE_SIZE * B_KV_HEADS, B_HEAD_DIM)
    cv = cache_v_l.reshape(n_pool, PAGE_SIZE * B_KV_HEADS, B_HEAD_DIM)
    kv_buf = pltpu.VMEM((2, PAGES_PER_STEP, PAGE_SIZE * B_KV_HEADS, B_HEAD_DIM), F32)
    o = pl.pallas_call(
        functools.partial(_dsa_sample_attend_kernel, dec_seq=dec_seq, n_steps=n_steps, n_samples=dec_batch),
        grid_spec=pltpu.PrefetchScalarGridSpec(
            num_scalar_prefetch=1,
            grid=(dec_batch, n_steps),
            in_specs=[pl.BlockSpec(memory_space=pl.ANY), pl.BlockSpec(memory_space=pl.ANY)] + [
                pl.BlockSpec((None, B_KV_HEADS, rows, B_HEAD_DIM), lambda b, s, pt: (b, 0, 0, 0)),
                pl.BlockSpec((None, SUBLANES, chunk), lambda b, s, pt: (b, 0, s)),
                pl.BlockSpec((None, SUBLANES, LANES), lambda b, s, pt: (b, 0, 0)),
                pl.BlockSpec((None, SUBLANES, LANES), lambda b, s, pt: (b, 0, 0)),
                pl.BlockSpec((None, SUBLANES, LANES), lambda b, s, pt: (b, 0, 0)),
                pl.BlockSpec((None, LANES, 2 * KV_WIDTH), lambda b, s, pt: (b, 0, 0)),
                pl.BlockSpec(sbias.shape, lambda b, s, pt: (0, 0, 0)),
                pl.BlockSpec(sconst.shape, lambda b, s, pt: (0, 0, 0)),
            ],
            out_specs=pl.BlockSpec((None, B_KV_HEADS, rows, B_HEAD_DIM), lambda b, s, pt: (b, 0, 0, 0)),
            scratch_shapes=[
                kv_buf,
                kv_buf,
                pltpu.SemaphoreType.DMA((2, 2)),
                pltpu.VMEM((B_KV_HEADS, rows, 1), F32),
                pltpu.VMEM((B_KV_HEADS, rows, 1), F32),
                pltpu.VMEM((B_KV_HEADS, rows, B_HEAD_DIM), F32),
            ],
        ),
        out_shape=jax.ShapeDtypeStruct((dec_batch, B_KV_HEADS, rows, B_HEAD_DIM), BF16),
        compiler_params=_cparams(("arbitrary", "arbitrary")),
        name="dsa_sample_attend",
    )(page_table, ck, cv, q, sc, scn, selv, selm, kv_new, sbias, sconst)
    o = o.reshape(dec_batch, B_KV_HEADS, B_GQA, SUBLANES, B_HEAD_DIM)[:, :, :, :dec_seq].transpose(0, 3, 1, 2, 4)
    return o.reshape(dec_batch * dec_seq, B_WIDTH)


def _mixout_kernel(u_ref, vn_ref, b_ref, ws_ref, ab_ref, x_ref, w_ref, o_ref, mix_scr, *, tm):
    j = pl.program_id(1)

    @pl.when(j == 0)
    def _():
        row = lax.broadcasted_iota(I32, (A_CHUNK, A_CHUNK), 0)
        col = lax.broadcasted_iota(I32, (A_CHUNK, A_CHUNK), 1)
        for g in range(A_GROUPS):
            wsg = jnp.where(row >= col, ws_ref[g], 0.0).astype(BF16)
            bg = ab_ref[:, g:g + 1]
            cs = slice(g * A_CH, (g + 1) * A_CH)
            for r in range(tm // A_CHUNK):
                rs = slice(r * A_CHUNK, (r + 1) * A_CHUNK)
                sm = jnp.dot(wsg, vn_ref[rs, cs], preferred_element_type=F32) + bg
                mix_scr[rs, cs] = (u_ref[rs, cs].astype(F32) * sm).astype(BF16)
        mix_scr[:, A_WIDTH:] = b_ref[...]

    o_ref[...] = x_ref[...] + jnp.dot(mix_scr[...], w_ref[...].astype(BF16), preferred_element_type=F32)


def _mixout(p, b_out, ws_eff, ab_eff, x2d, w_out_l, *, tm, tn=512):
    n = x2d.shape[0]
    return pl.pallas_call(
        functools.partial(_mixout_kernel, tm=tm),
        grid=(n // tm, D_MODEL // tn),
        in_specs=[
            pl.BlockSpec((tm, A_WIDTH), lambda i, j: (i, COL_U // A_WIDTH)),
            pl.BlockSpec((tm, A_WIDTH), lambda i, j: (i, COL_V // A_WIDTH)),
            pl.BlockSpec((tm, B_WIDTH), lambda i, j: (i, 0)),
            pl.BlockSpec((A_GROUPS, A_CHUNK, A_CHUNK), lambda i, j: (0, 0, 0)),
            pl.BlockSpec((A_CHUNK, A_GROUPS), lambda i, j: (0, 0)),
            pl.BlockSpec((tm, tn), lambda i, j: (i, j)),
            pl.BlockSpec((A_WIDTH + B_WIDTH, tn), lambda i, j: (0, j)),
        ],
        out_specs=pl.BlockSpec((tm, tn), lambda i, j: (i, j)),
        out_shape=jax.ShapeDtypeStruct((n, D_MODEL), F32),
        scratch_shapes=[pltpu.VMEM((tm, A_WIDTH + B_WIDTH), BF16)],
        compiler_params=_cparams(("arbitrary", "arbitrary")),
        name="mixout",
    )(p, p, b_out, ws_eff, ab_eff, x2d, w_out_l)


def _memkv_kernel(x_ref, wk_ref, wv_ref, kn_ref, mk_ref, mv_ref):
    x = x_ref[...].astype(BF16)
    tm = x.shape[0]
    mk = _head_norm(jnp.dot(x, wk_ref[...].astype(BF16), preferred_element_type=F32), kn_ref[...])
    mv = jnp.dot(x, wv_ref[...].astype(BF16), preferred_element_type=F32)
    for hd in range(MEM_HEADS):
        cs = slice(hd * MEM_HEAD_DIM, (hd + 1) * MEM_HEAD_DIM)
        mk_ref[pl.ds(hd, tm, stride=MEM_HEADS), :] = mk[:, cs]
        mv_ref[pl.ds(hd, tm, stride=MEM_HEADS), :] = mv[:, cs]


def _memkv(mem2d, wk, wv, knorm, *, tm=256):
    n = mem2d.shape[0]
    kn = jnp.tile(knorm, MEM_HEADS).reshape(1, MEM_WIDTH)
    return pl.pallas_call(
        _memkv_kernel,
        grid=(n // tm,),
        in_specs=[
            pl.BlockSpec((tm, D_MODEL), lambda i: (i, 0)),
            pl.BlockSpec((D_MODEL, MEM_WIDTH), lambda i: (0, 0)),
            pl.BlockSpec((D_MODEL, MEM_WIDTH), lambda i: (0, 0)),
            pl.BlockSpec((1, MEM_WIDTH), lambda i: (0, 0)),
        ],
        out_specs=[pl.BlockSpec((tm * MEM_HEADS, MEM_HEAD_DIM), lambda i: (i, 0))] * 2,
        out_shape=[jax.ShapeDtypeStruct((n * MEM_HEADS, MEM_HEAD_DIM), F32)] * 2,
        compiler_params=_cparams(("arbitrary",)),
        name="memkv",
    )(mem2d, wk, wv, kn)


def _memattn_kernel(h_ref, g_ref, wq_ref, qn_ref, mk_ref, mv_ref, wo_ref, o_ref, *, rows_per_mem, mem_len):
    h = h_ref[...]
    n = _row_norm(h, g_ref[...]).astype(BF16)
    q = _head_norm(jnp.dot(n, wq_ref[...].astype(BF16), preferred_element_type=F32), qn_ref[...]).astype(BF16)
    tm, m = h.shape[0], mk_ref.shape[0] // MEM_HEADS
    if rows_per_mem is not None:
        row = lax.broadcasted_iota(I32, (tm, m), 0)
        col = lax.broadcasted_iota(I32, (tm, m), 1)
        mask = jnp.where(row // rows_per_mem == col // mem_len, 0.0, -jnp.inf)
    outs = []
    for hd in range(MEM_HEADS):
        cs = slice(hd * MEM_HEAD_DIM, (hd + 1) * MEM_HEAD_DIM)
        mk = mk_ref[pl.ds(hd, m, stride=MEM_HEADS), :].astype(BF16)
        mv = mv_ref[pl.ds(hd, m, stride=MEM_HEADS), :].astype(BF16)
        lg = _dot_nt(q[:, cs], mk) * (MEM_HEAD_DIM ** -0.5)
        if rows_per_mem is not None:
            lg = lg + mask
        e = jnp.exp(lg - jnp.max(lg, axis=-1, keepdims=True))
        l = jnp.sum(e, axis=-1, keepdims=True)
        outs.append(jnp.dot(e.astype(BF16), mv, preferred_element_type=F32) / l)
    o = jnp.concatenate(outs, axis=-1).astype(BF16)
    o_ref[...] = h + jnp.dot(o, wo_ref[...].astype(BF16), preferred_element_type=F32)


def _memattn(h2d, gain, wq, qnorm, mk, mv, wo, *, tm, mem_rows, tiles_per_mem, rows_per_mem, mem_len):
    n = h2d.shape[0]
    qn = jnp.tile(qnorm, MEM_HEADS).reshape(1, MEM_WIDTH)
    return pl.pallas_call(
        functools.partial(_memattn_kernel, rows_per_mem=rows_per_mem, mem_len=mem_len),
        grid=(n // tm,),
        in_specs=[
            pl.BlockSpec((tm, D_MODEL), lambda i: (i, 0)),
            pl.BlockSpec((1, D_MODEL), lambda i: (0, 0)),
            pl.BlockSpec((D_MODEL, MEM_WIDTH), lambda i: (0, 0)),
            pl.BlockSpec((1, MEM_WIDTH), lambda i: (0, 0)),
            pl.BlockSpec((mem_rows * MEM_HEADS, MEM_HEAD_DIM), lambda i: (i // tiles_per_mem, 0)),
            pl.BlockSpec((mem_rows * MEM_HEADS, MEM_HEAD_DIM), lambda i: (i // tiles_per_mem, 0)),
            pl.BlockSpec((MEM_WIDTH, D_MODEL), lambda i: (0, 0)),
        ],
        out_specs=pl.BlockSpec((tm, D_MODEL), lambda i: (i, 0)),
        out_shape=jax.ShapeDtypeStruct((n, D_MODEL), F32),
        compiler_params=_cparams(("arbitrary",)),
        name="memattn",
    )(h2d, gain.reshape(1, D_MODEL), wq, qn, mk, mv, wo)


def _router_kernel(h_ref, g_ref, w_ref, b_ref, n_ref, c_ref):
    n = _row_norm(h_ref[...], g_ref[...]).astype(BF16)
    n_ref[...] = n
    lg = jnp.dot(n, w_ref[...].astype(BF16), preferred_element_type=F32) + b_ref[...]
    gl, el = lg[:, :LANES], lg[:, LANES:]
    lane = lax.broadcasted_iota(I32, gl.shape, 1)
    lanef = lane.astype(F32)

    def first_max_lane(v, vmax):
        return jnp.min(jnp.where(v == vmax, lanef, float(LANES)), axis=-1, keepdims=True).astype(I32)

    gl = jnp.where(lane < N_GROUPS, gl, -jnp.inf)
    ge = jnp.exp(gl - jnp.max(gl, axis=-1, keepdims=True))
    gp = ge / jnp.sum(ge, axis=-1, keepdims=True)
    g_gate = jnp.max(gp, axis=-1, keepdims=True)
    g_sel = first_max_lane(gp, g_gate)
    in_grp = (lane // EXPERTS_PER_GROUP == g_sel) & (lane < N_EXPERTS)
    el = jnp.where(in_grp, el, -jnp.inf)
    ee = jnp.exp(el - jnp.max(el, axis=-1, keepdims=True))
    ep = jnp.where(in_grp, ee / jnp.sum(ee, axis=-1, keepdims=True), -jnp.inf)
    w1 = jnp.max(ep, axis=-1, keepdims=True)
    i1 = first_max_lane(ep, w1)
    ep2 = jnp.where(lane == i1, -jnp.inf, ep)
    w2 = jnp.max(ep2, axis=-1, keepdims=True)
    i2 = first_max_lane(ep2, w2)
    tot = w1 + w2
    comb = jnp.where(lane == i1, w1 / tot * g_gate, jnp.where(lane == i2, w2 / tot * g_gate, 0.0))
    c_ref[...] = jnp.where(lane == GSEL_LANE, g_sel.astype(F32), comb)


def _router(h2d, gain, w_cat, b_cat, *, tm):
    n = h2d.shape[0]
    return pl.pallas_call(
        _router_kernel,
        grid=(n // tm,),
        in_specs=[
            pl.BlockSpec((tm, D_MODEL), lambda i: (i, 0)),
            pl.BlockSpec((1, D_MODEL), lambda i: (0, 0)),
            pl.BlockSpec((D_MODEL, 2 * LANES), lambda i: (0, 0)),
            pl.BlockSpec((1, 2 * LANES), lambda i: (0, 0)),
        ],
        out_specs=[pl.BlockSpec((tm, D_MODEL), lambda i: (i, 0)), pl.BlockSpec((tm, LANES), lambda i: (i, 0))],
        out_shape=[jax.ShapeDtypeStruct((n, D_MODEL), BF16), jax.ShapeDtypeStruct((n, LANES), F32)],
        compiler_params=_cparams(("arbitrary",)),
        name="router",
    )(h2d, gain.reshape(1, D_MODEL), w_cat, b_cat)


def _cast_kernel(x_ref, o_ref):
    o_ref[...] = x_ref[...].astype(BF16)


def _cast_bf16(w):
    ne, r, c = w.shape
    return pl.pallas_call(
        _cast_kernel,
        grid=(ne,),
        in_specs=[pl.BlockSpec((None, r, c), lambda e: (e, 0, 0))],
        out_specs=pl.BlockSpec((None, r, c), lambda e: (e, 0, 0)),
        out_shape=jax.ShapeDtypeStruct(w.shape, BF16),
        compiler_params=_cparams(("arbitrary",)),
        name="cast_bf16",
    )(w)


GSEL_LANE = N_EXPERTS
MOE_ROWS = 128
MOE_MAIN_SLACK = 32
MOE_FF_SPLIT = 1


def _moe_routed_kernel(n_ref, c_ref, h_hbm, wg_ref, wu_ref, wd_ref, o_ref,
                       tri_scr, key_scr, xc_scr, yc_scr, cw_scr, nch_ref, sem, *, tm):
    i, g, e, f = (pl.program_id(k) for k in range(4))
    main_rows = tm // N_GROUPS + MOE_MAIN_SLACK
    first_e = (e == 0) & (f == 0)
    last_e = (e == EXPERTS_PER_GROUP - 1) & (f == MOE_FF_SPLIT - 1)

    def onehot_rows(start, rows):
        want = (start + lax.broadcasted_iota(I32, (rows, 1), 0)).astype(F32)
        return jnp.where(key_scr[pl.ds(g, 1), :] == want, 1.0, 0.0).astype(BF16)

    def tail_start(c):
        return pl.multiple_of(main_rows + c * MOE_ROWS, math.gcd(main_rows, MOE_ROWS))

    @pl.when((i == 0) & (g == 0) & first_e)
    def _():
        r = lax.broadcasted_iota(I32, (tm, tm), 0)
        c = lax.broadcasted_iota(I32, (tm, tm), 1)
        tri_scr[...] = jnp.where(r < c, 1.0, 0.0).astype(BF16)

    @pl.when((g == 0) & first_e)
    def _():
        cp = pltpu.make_async_copy(h_hbm.at[pl.ds(pl.multiple_of(i * tm, tm), tm)], o_ref, sem)
        cp.start()
        lane = lax.broadcasted_iota(I32, (tm, LANES), 1).astype(F32)
        onehot = jnp.where(lane == c_ref[:, GSEL_LANE:GSEL_LANE + 1], 1.0, 0.0).astype(BF16)
        er = lax.broadcasted_iota(I32, (SUBLANES, LANES), 0)
        ec = lax.broadcasted_iota(I32, (SUBLANES, LANES), 1)
        eye = jnp.where(er == ec, 1.0, 0.0).astype(BF16)
        mask_t = _dot_nt(eye, onehot)
        rank_t = jnp.dot(mask_t.astype(BF16), tri_scr[...], preferred_element_type=F32)
        key_scr[...] = jnp.where(mask_t > 0.5, rank_t, -1.0)
        for gg in range(N_GROUPS):
            cnt = jnp.sum(mask_t[gg:gg + 1, :], axis=-1, keepdims=True)
            extra = jnp.ceil(jnp.maximum(cnt - main_rows, 0.0) * (1.0 / MOE_ROWS))
            nch_ref[gg] = jnp.sum(extra).astype(I32)
        cp.wait()

    ntail = nch_ref[g]

    @pl.when(first_e)
    def _():
        comb = c_ref[...]
        hi = comb.astype(BF16)
        r1 = comb - hi.astype(F32)
        mid = r1.astype(BF16)
        lo = (r1 - mid.astype(F32)).astype(BF16)

        def compact(start, rows):
            s = onehot_rows(start, rows)
            xc_scr[pl.ds(start, rows), :] = jnp.dot(s, n_ref[...], preferred_element_type=F32).astype(BF16)
            cw_scr[pl.ds(start, rows), :] = (jnp.dot(s, hi, preferred_element_type=F32)
                                             + jnp.dot(s, mid, preferred_element_type=F32)
                                             + jnp.dot(s, lo, preferred_element_type=F32))
            yc_scr[pl.ds(start, rows), :] = jnp.zeros((rows, D_MODEL), F32)

        compact(0, main_rows)
        lax.fori_loop(0, ntail, lambda c, carry: (compact(tail_start(c), MOE_ROWS), carry)[1], 0)

    wg = wg_ref[...].astype(BF16)
    wu = wu_ref[...].astype(BF16)
    wd = wd_ref[...].astype(BF16)
    ex = g * EXPERTS_PER_GROUP + e

    def expert(start, rows):
        x = xc_scr[pl.ds(start, rows), :]
        a = jnp.dot(x, wg, preferred_element_type=F32)
        u = jnp.dot(x, wu, preferred_element_type=F32)
        lane = lax.broadcasted_iota(I32, (rows, LANES), 1)
        cw = jnp.sum(jnp.where(lane == ex, cw_scr[pl.ds(start, rows), :], 0.0), axis=-1, keepdims=True)
        hid = (a * jax.nn.sigmoid(a) * u * cw).astype(BF16)
        yc_scr[pl.ds(start, rows), :] += jnp.dot(hid, wd, preferred_element_type=F32)

    expert(0, main_rows)
    lax.fori_loop(0, ntail, lambda c, carry: (expert(tail_start(c), MOE_ROWS), carry)[1], 0)

    @pl.when(last_e)
    def _():
        def scatter(start, rows):
            s = onehot_rows(start, rows)
            y = yc_scr[pl.ds(start, rows), :].astype(BF16)
            o_ref[...] += lax.dot_general(s, y, (((0,), (0,)), ((), ())), preferred_element_type=F32)

        scatter(0, main_rows)
        lax.fori_loop(0, ntail, lambda c, carry: (scatter(tail_start(c), MOE_ROWS), carry)[1], 0)


def _moe_routed(n3, comb, h2d, w_gate_l, w_up_l, w_down_l, *, tm):
    n = h2d.shape[0]
    ff = EXPERT_FF // MOE_FF_SPLIT

    def wmap(i, g, e, f):
        return (g * EXPERTS_PER_GROUP + e, 0, f)

    return pl.pallas_call(
        functools.partial(_moe_routed_kernel, tm=tm),
        grid=(n // tm, N_GROUPS, EXPERTS_PER_GROUP, MOE_FF_SPLIT),
        in_specs=[
            pl.BlockSpec((tm, D_MODEL), lambda i, g, e, f: (i, 0), pipeline_mode=pl.Buffered(1)),
            pl.BlockSpec((tm, LANES), lambda i, g, e, f: (i, 0), pipeline_mode=pl.Buffered(1)),
            pl.BlockSpec(memory_space=pl.ANY),
            pl.BlockSpec((None, D_MODEL, ff), wmap),
            pl.BlockSpec((None, D_MODEL, ff), wmap),
            pl.BlockSpec((None, ff, D_MODEL), lambda i, g, e, f: (g * EXPERTS_PER_GROUP + e, f, 0)),
        ],
        out_specs=pl.BlockSpec((tm, D_MODEL), lambda i, g, e, f: (i, 0)),
        out_shape=jax.ShapeDtypeStruct((n, D_MODEL), F32),
        scratch_shapes=[
            pltpu.VMEM((tm, tm), BF16),
            pltpu.VMEM((SUBLANES, tm), F32),
            pltpu.VMEM((tm + MOE_ROWS, D_MODEL), BF16),
            pltpu.VMEM((tm + MOE_ROWS, D_MODEL), F32),
            pltpu.VMEM((tm + MOE_ROWS, LANES), F32),
            pltpu.SMEM((N_GROUPS,), I32),
            pltpu.SemaphoreType.DMA(()),
        ],
        compiler_params=pltpu.CompilerParams(
            dimension_semantics=("arbitrary",) * 4, vmem_limit_bytes=MOE_VMEM_LIMIT),
        name="moe_routed",
    )(n3, comb, h2d, w_gate_l, w_up_l, w_down_l)


def _moe_kernel(n_ref, c_ref, h_ref, wg_ref, wu_ref, wd_ref, o_ref):
    e = pl.program_id(1)

    @pl.when(e == 0)
    def _():
        o_ref[...] = h_ref[...]

    x = n_ref[...]
    a = jnp.dot(x, wg_ref[...].astype(BF16), preferred_element_type=F32)
    u = jnp.dot(x, wu_ref[...].astype(BF16), preferred_element_type=F32)
    lane = lax.broadcasted_iota(I32, c_ref.shape, 1)
    c = jnp.sum(jnp.where(lane == e, c_ref[...], 0.0), axis=-1, keepdims=True)
    hid = (a * jax.nn.sigmoid(a) * u * c).astype(BF16)
    o_ref[...] += jnp.dot(hid, wd_ref[...].astype(BF16), preferred_element_type=F32)


def _moe(n3, comb, h2d, w_gate_l, w_up_l, w_down_l, *, tm):
    n = h2d.shape[0]
    return pl.pallas_call(
        _moe_kernel,
        grid=(n // tm, N_EXPERTS),
        in_specs=[
            pl.BlockSpec((tm, D_MODEL), lambda i, e: (i, 0)),
            pl.BlockSpec((tm, LANES), lambda i, e: (i, 0)),
            pl.BlockSpec((tm, D_MODEL), lambda i, e: (i, 0)),
            pl.BlockSpec((None, D_MODEL, EXPERT_FF), lambda i, e: (e, 0, 0)),
            pl.BlockSpec((None, D_MODEL, EXPERT_FF), lambda i, e: (e, 0, 0)),
            pl.BlockSpec((None, EXPERT_FF, D_MODEL), lambda i, e: (e, 0, 0)),
        ],
        out_specs=pl.BlockSpec((tm, D_MODEL), lambda i, e: (i, 0)),
        out_shape=jax.ShapeDtypeStruct((n, D_MODEL), F32),
        compiler_params=_cparams(("arbitrary", "arbitrary")),
        name="moe",
    )(n3, comb, h2d, w_gate_l, w_up_l, w_down_l)


def _post_mixer(p, b_out, ws_eff, ab_eff, x2d, mk, mv, lw, *, tm_mix, tm_mem, mem_rows, tiles_per_mem,
                rows_per_mem, mem_len, tm_route, tm_moe):
    h1 = _mixout(p, b_out, ws_eff, ab_eff, x2d, lw["w_out"], tm=tm_mix)
    h2 = _memattn(h1, lw["norm_mem"], lw["mem_wq"], lw["mem_qnorm"], mk, mv, lw["mem_wo"], tm=tm_mem,
                  mem_rows=mem_rows, tiles_per_mem=tiles_per_mem, rows_per_mem=rows_per_mem, mem_len=mem_len)
    n3, comb = _router(h2, lw["norm_ffn"], lw["w_cat"], lw["b_cat"], tm=tm_route)
    moe = _moe_routed if tm_moe > N_GROUPS * MOE_ROWS else _moe
    return moe(n3, comb, h2, lw["w_gate"], lw["w_up"], lw["w_down"], tm=tm_moe)


def kernel(x_prompt, x_sample, cache_k, cache_v, cache_idx_k, cache_mem_k, cache_mem_v, page_table,
           mem_prompt, norm_mix, w_in, a_vnorm, a_ws, a_b, b_qnorm, b_knorm, rel_bias, w_out,
           norm_mem, mem_wq, mem_wk, mem_wv, mem_qnorm, mem_knorm, mem_wo, norm_ffn,
           w_group, b_group, w_router, b_router, w_gate, w_up, w_down):
    bp, t, d = x_prompt.shape
    bd, s, _ = x_sample.shape
    depth = w_in.shape[0]
    mem_len = mem_prompt.shape[1]
    assert d == D_MODEL and t % KEY_CHUNK == 0 and s <= 8 and (bd * s) % LANES == 0

    bias_pairs, sbias, sconst = _bias_tables(rel_bias)
    hp = x_prompt.reshape(bp * t, d)
    hs = x_sample.reshape(bd * s, d)
    outs = [[] for _ in range(9)]
    for l in range(depth):
        w_cat = jnp.zeros((d, 2 * LANES), F32)
        w_cat = w_cat.at[:, :N_GROUPS].set(w_group[l]).at[:, LANES:LANES + N_EXPERTS].set(w_router[l])
        b_cat = jnp.zeros((1, 2 * LANES), F32)
        b_cat = b_cat.at[0, :N_GROUPS].set(b_group[l]).at[0, LANES:LANES + N_EXPERTS].set(b_router[l])
        lw = dict(w_out=w_out[l], norm_mem=norm_mem[l], mem_wq=mem_wq[l], mem_qnorm=mem_qnorm[l],
                  mem_wo=mem_wo[l], norm_ffn=norm_ffn[l], w_cat=w_cat, b_cat=b_cat)

        p, k, v, tail = _proj(hp, norm_mix[l], w_in[l], a_vnorm[l], b_qnorm[l], b_knorm[l],
                              tm=1024, want_vn=False)
        experts = (w_gate[l], w_up[l], w_down[l])
        fuse_cast = (bp * (t // Q_BLOCK)) % N_EXPERTS == 0
        b_out, casted = _dsa_prompt(p, tail, bias_pairs, experts if fuse_cast else (), batch=bp, seq=t)
        if not fuse_cast:
            casted = tuple(_cast_bf16(w) for w in experts)
        lw.update(w_gate=casted[0], w_up=casted[1], w_down=casted[2])
        mk, mv = _memkv(mem_prompt.reshape(bp * mem_len, d), mem_wk[l], mem_wv[l], mem_knorm[l])
        hp = _post_mixer(p, b_out, a_ws[l], a_b[l].T, hp, mk, mv, lw, tm_mix=1024, tm_mem=512,
                         mem_rows=mem_len, tiles_per_mem=t // 512, rows_per_mem=None, mem_len=mem_len,
                         tm_route=512, tm_moe=1024)
        outs[0].append(k.reshape(bp, t, B_KV_HEADS, B_HEAD_DIM))
        outs[1].append(v.reshape(bp, t, B_KV_HEADS, B_HEAD_DIM))
        outs[2].append(tail[:, :IDX_DIM].reshape(bp, t, IDX_DIM))
        outs[3].append(mk.reshape(bp, mem_len, MEM_HEADS, MEM_HEAD_DIM))
        outs[4].append(mv.reshape(bp, mem_len, MEM_HEADS, MEM_HEAD_DIM))

        ns = bd * s
        p_s, k_s, v_s, tail_s, vn_s = _proj(hs, norm_mix[l], w_in[l], a_vnorm[l], b_qnorm[l], b_knorm[l],
                                            tm=ns, want_vn=True)
        b_out_s = _dsa_sample(p_s, tail_s, k_s, v_s, cache_k[l], cache_v[l], cache_idx_k[l], page_table,
                              sbias, sconst, dec_batch=bd, dec_seq=s)
        ws_s = jnp.einsum("ab,gts->gatbs", jnp.eye(ns // s, dtype=F32), a_ws[l][:, :s, :s])
        ws_s = ws_s.reshape(A_GROUPS, ns, ns)
        ab_s = jnp.tile(a_b[l][:, :s].T, (ns // s, 1))
        mem_tile = 32
        hs = _post_mixer(p_s, b_out_s, ws_s, ab_s, hs,
                         cache_mem_k[l].reshape(bd * mem_len * MEM_HEADS, MEM_HEAD_DIM),
                         cache_mem_v[l].reshape(bd * mem_len * MEM_HEADS, MEM_HEAD_DIM), lw,
                         tm_mix=ns, tm_mem=mem_tile, mem_rows=mem_tile // s * mem_len, tiles_per_mem=1,
                         rows_per_mem=s, mem_len=mem_len, tm_route=ns, tm_moe=ns)
        outs[5].append(k_s.reshape(bd, s, B_KV_HEADS, B_HEAD_DIM))
        outs[6].append(v_s.reshape(bd, s, B_KV_HEADS, B_HEAD_DIM))
        outs[7].append(tail_s[:, :IDX_DIM].reshape(bd, s, IDX_DIM))
        outs[8].append(vn_s.reshape(bd, s, A_GROUPS, A_CH))
    return (hp.reshape(bp, t, d), hs.reshape(bd, s, d)) + tuple(jnp.stack(o) for o in outs)
```

```python
import functools
import math

import jax
import jax.numpy as jnp
from jax import lax
from jax.experimental import pallas as pl
from jax.experimental.pallas import tpu as pltpu

BF16 = jnp.bfloat16
F32 = jnp.float32
I32 = jnp.int32

D_MODEL = 2048
A_GROUPS = 8
A_CH = 128
A_WIDTH = A_GROUPS * A_CH
A_CHUNK = 128
B_HEADS = 8
B_HEAD_DIM = 128
B_KV_HEADS = 2
B_GQA = B_HEADS // B_KV_HEADS
B_WIDTH = B_HEADS * B_HEAD_DIM
KV_WIDTH = B_KV_HEADS * B_HEAD_DIM
IDX_HEADS = 16
IDX_DIM = 64
TOPK_MAX = 256
Q_BLOCK = 128
REL_BUCKETS = 32
REL_MAX_DIST = 128
MEM_HEADS = 4
MEM_HEAD_DIM = 128
MEM_WIDTH = MEM_HEADS * MEM_HEAD_DIM
N_GROUPS = 4
EXPERTS_PER_GROUP = 4
N_EXPERTS = N_GROUPS * EXPERTS_PER_GROUP
EXPERT_FF = 512
PAGE_SIZE = 128
EPS = 1e-6

COL_U = 0
COL_V = COL_U + A_WIDTH
COL_Q = COL_V + A_WIDTH
COL_K = COL_Q + B_WIDTH
COL_VV = COL_K + KV_WIDTH
COL_QI = COL_VV + KV_WIDTH
COL_MAIN = COL_QI + IDX_HEADS * IDX_DIM
TAIL_COLS = IDX_DIM + IDX_HEADS
LANES = 128
SUBLANES = 8
PROJ_TN = 512
PROJ_ROW_CHUNK = 256
KEY_CHUNK = 256
NEG_INIT = -1e30
VMEM_LIMIT = 56 * 1024 * 1024
MOE_VMEM_LIMIT = 60 * 1024 * 1024


def _cparams(sem):
    return pltpu.CompilerParams(dimension_semantics=sem, vmem_limit_bytes=VMEM_LIMIT)


def _gelu(x):
    return 0.5 * x * (1.0 + lax.erf(x * (2.0 ** -0.5)))


def _head_norm(a, gain, width=LANES):
    outs = []
    for c in range(a.shape[1] // width):
        blk = a[:, c * width:(c + 1) * width]
        ms = jnp.mean(blk * blk, axis=-1, keepdims=True)
        outs.append(blk * lax.rsqrt(ms + EPS) * gain[:, c * width:(c + 1) * width])
    return outs[0] if len(outs) == 1 else jnp.concatenate(outs, axis=-1)


def _row_norm(x, gain):
    ms = jnp.mean(x * x, axis=-1, keepdims=True)
    return x * lax.rsqrt(ms + EPS) * gain


def _dot_nt(a, b):
    return lax.dot_general(a, b, (((1,), (1,)), ((), ())), preferred_element_type=F32)


def _proj_kernel(x_ref, g_ref, w_ref, wt_ref, avn_ref, qn_ref, kn_ref, *rest, want_vn):
    if want_vn:
        p_ref, k_ref, v_ref, tail_ref, vn_ref, xn_scr = rest
    else:
        p_ref, k_ref, v_ref, tail_ref, xn_scr = rest
        vn_ref = None
    j = pl.program_id(1)
    ju = COL_V // PROJ_TN
    jv = COL_Q // PROJ_TN
    jq = COL_K // PROJ_TN
    jkv = COL_QI // PROJ_TN

    @pl.when(j == 0)
    def _():
        xn = _row_norm(x_ref[...], g_ref[...]).astype(BF16)
        xn_scr[...] = xn
        wlane = lax.broadcasted_iota(I32, wt_ref.shape, 1)
        wt = jnp.where(wlane < TAIL_COLS, wt_ref[...], 0.0).astype(BF16)
        t = jnp.dot(xn, wt, preferred_element_type=F32)
        lane = lax.broadcasted_iota(I32, t.shape, 1)
        tail_ref[...] = jnp.where(lane >= IDX_DIM, t * (IDX_HEADS ** -0.5), t)

    tm = xn_scr.shape[0]
    rc = min(tm, PROJ_ROW_CHUNK)

    def row_chunks(epilogue):
        w = w_ref[...].astype(BF16)
        for r in range(tm // rc):
            rs = slice(r * rc, (r + 1) * rc)
            epilogue(jnp.dot(xn_scr[rs, :], w, preferred_element_type=F32), rs)

    @pl.when(j < ju)
    def _():
        def epi(acc, rs):
            p_ref[rs, :] = _gelu(acc).astype(BF16)
        row_chunks(epi)

    @pl.when((j >= ju) & (j < jv))
    def _():
        def epi(acc, rs):
            vn = _head_norm(_gelu(acc), avn_ref[...])
            p_ref[rs, :] = vn.astype(BF16)
            if vn_ref is not None:
                vn_ref[rs, :] = vn
        row_chunks(epi)

    @pl.when((j >= jv) & (j < jq))
    def _():
        def epi(acc, rs):
            p_ref[rs, :] = _head_norm(acc, qn_ref[...]).astype(BF16)
        row_chunks(epi)

    @pl.when(j == jq)
    def _():
        def epi(acc, rs):
            k = _head_norm(acc[:, :KV_WIDTH], kn_ref[...])
            v = acc[:, KV_WIDTH:]
            k_ref[rs, :] = k
            v_ref[rs, :] = v
            p_ref[rs, :] = jnp.concatenate([k, v], axis=-1).astype(BF16)
        row_chunks(epi)

    @pl.when(j >= jkv)
    def _():
        def epi(acc, rs):
            p_ref[rs, :] = acc.astype(BF16)
        row_chunks(epi)


def _proj(x2d, gain, w_in_l, a_vnorm_l, b_qnorm_l, b_knorm_l, *, tm, want_vn):
    n = x2d.shape[0]
    nj = COL_MAIN // PROJ_TN
    avn = a_vnorm_l.reshape(1, A_WIDTH)
    qn = jnp.tile(b_qnorm_l, PROJ_TN // B_HEAD_DIM).reshape(1, PROJ_TN)
    kn = jnp.tile(b_knorm_l, B_KV_HEADS).reshape(1, KV_WIDTH)
    ju = COL_V // PROJ_TN
    nv = A_WIDTH // PROJ_TN

    def vmap_(i, j):
        return (0, jnp.clip(j - ju, 0, nv - 1))

    in_specs = [
        pl.BlockSpec((tm, D_MODEL), lambda i, j: (i, 0)),
        pl.BlockSpec((1, D_MODEL), lambda i, j: (0, 0)),
        pl.BlockSpec((D_MODEL, PROJ_TN), lambda i, j: (0, j)),
        pl.BlockSpec((D_MODEL, LANES), lambda i, j: (0, COL_MAIN // LANES)),
        pl.BlockSpec((1, PROJ_TN), vmap_),
        pl.BlockSpec((1, PROJ_TN), lambda i, j: (0, 0)),
        pl.BlockSpec((1, KV_WIDTH), lambda i, j: (0, 0)),
    ]
    out_shape = [
        jax.ShapeDtypeStruct((n, COL_MAIN), BF16),
        jax.ShapeDtypeStruct((n, KV_WIDTH), F32),
        jax.ShapeDtypeStruct((n, KV_WIDTH), F32),
        jax.ShapeDtypeStruct((n, LANES), F32),
    ]
    out_specs = [
        pl.BlockSpec((tm, PROJ_TN), lambda i, j: (i, j)),
        pl.BlockSpec((tm, KV_WIDTH), lambda i, j: (i, 0)),
        pl.BlockSpec((tm, KV_WIDTH), lambda i, j: (i, 0)),
        pl.BlockSpec((tm, LANES), lambda i, j: (i, 0)),
    ]
    if want_vn:
        out_shape.append(jax.ShapeDtypeStruct((n, A_WIDTH), F32))
        out_specs.append(pl.BlockSpec((tm, PROJ_TN), lambda i, j: (i, jnp.clip(j - ju, 0, nv - 1))))
    return pl.pallas_call(
        functools.partial(_proj_kernel, want_vn=want_vn),
        grid=(n // tm, nj),
        in_specs=in_specs,
        out_specs=out_specs,
        out_shape=out_shape,
        scratch_shapes=[pltpu.VMEM((tm, D_MODEL), BF16)],
        compiler_params=_cparams(("arbitrary", "arbitrary")),
        name="proj",
    )(x2d, gain.reshape(1, D_MODEL), w_in_l, w_in_l, avn, qn, kn)


def _t5_bucket(d):
    max_exact = REL_BUCKETS // 2
    d = jnp.maximum(d, 0)
    ratio = jnp.log(jnp.maximum(d, 1).astype(F32) / max_exact) / math.log(REL_MAX_DIST / max_exact)
    large = jnp.minimum(max_exact + jnp.floor(ratio * (REL_BUCKETS - max_exact)).astype(I32), REL_BUCKETS - 1)
    return jnp.where(d < max_exact, d, large)


def _bias_lookup(bucket, rb_ref, h):
    acc = jnp.zeros(bucket.shape, F32)
    for r in range(REL_BUCKETS):
        acc = jnp.where(bucket == r, rb_ref[r, h], acc)
    return acc


def _bias_kernel(rb_ref, pair_ref, samp_ref, sconst_ref):
    q = lax.broadcasted_iota(I32, (Q_BLOCK, Q_BLOCK), 0)
    kc = lax.broadcasted_iota(I32, (Q_BLOCK, Q_BLOCK), 1)
    nvar = pair_ref.shape[0]
    rels = range(-(nvar - 1), 2)
    buckets = {r: _t5_bucket(q - kc - r * Q_BLOCK) for r in rels}
    for h in range(B_HEADS):
        g, hh = divmod(h, B_GQA)
        tiles = {r: _bias_lookup(buckets[r], rb_ref, h) for r in rels}
        for var in range(nvar):
            r0 = var - (nvar - 1)
            for t in range(2):
                pair_ref[var, g, hh * Q_BLOCK:(hh + 1) * Q_BLOCK, t * Q_BLOCK:(t + 1) * Q_BLOCK] = tiles[r0 + t]
    rows = B_GQA * SUBLANES
    row = lax.broadcasted_iota(I32, (rows, 2 * PAGE_SIZE), 0)
    col = lax.broadcasted_iota(I32, (rows, 2 * PAGE_SIZE), 1)
    t = row % SUBLANES
    d = jnp.where(col < PAGE_SIZE, t + PAGE_SIZE - col, t - (col - PAGE_SIZE))
    bucket = _t5_bucket(d)
    far = jnp.full((rows, LANES), REL_BUCKETS - 1, I32)
    rowc = lax.broadcasted_iota(I32, (rows, LANES), 0)
    for g in range(B_KV_HEADS):
        acc = jnp.zeros((rows, 2 * PAGE_SIZE), F32)
        accc = jnp.zeros((rows, LANES), F32)
        for hh in range(B_GQA):
            h = g * B_GQA + hh
            acc = jnp.where(row // SUBLANES == hh, _bias_lookup(bucket, rb_ref, h), acc)
            accc = jnp.where(rowc // SUBLANES == hh, _bias_lookup(far, rb_ref, h), accc)
        samp_ref[g] = acc
        sconst_ref[g] = accc


N_PAIR_VARIANTS = 4


def _bias_tables(rel_bias):
    rows = B_GQA * SUBLANES
    return pl.pallas_call(
        _bias_kernel,
        in_specs=[pl.BlockSpec(memory_space=pltpu.SMEM)],
        out_shape=[
            jax.ShapeDtypeStruct((N_PAIR_VARIANTS, B_KV_HEADS, B_GQA * Q_BLOCK, KEY_CHUNK), F32),
            jax.ShapeDtypeStruct((B_KV_HEADS, rows, 2 * PAGE_SIZE), F32),
            jax.ShapeDtypeStruct((B_KV_HEADS, rows, LANES), F32),
        ],
        name="bias_tables",
    )(rel_bias)


def _key_to_f32(key):
    bits = jnp.where(key < 0, key ^ I32(-2 ** 31), ~key)
    return lax.bitcast_convert_type(bits, F32)


def _kth_largest(count_ge, shape, k):
    def body(it, prefix):
        bit = 31 - it
        cand = prefix | lax.shift_left(I32(1), bit)
        cnt = count_ge(_key_to_f32(cand))
        return jnp.where(cnt >= k, cand, prefix)

    prefix = lax.fori_loop(0, 32, body, jnp.zeros(shape, I32))
    thr = _key_to_f32(prefix)
    return jnp.where((prefix & I32(-2 ** 23)) == 0, -jnp.inf, thr)


NO_INDEX_BOUND = 2 ** 30
TIE_REPAIR_ROUNDS = 2


def _topk_select(fold_sum, fold_min, shape, k, index_bits):
    thr = _kth_largest(lambda t: fold_sum(lambda sc, kp: sc >= t), shape, k)
    c_ge = fold_sum(lambda sc, kp: sc >= thr)
    no_bound = jnp.full(shape, NO_INDEX_BOUND, I32)

    def repair(_):
        v = thr
        strict = jnp.zeros(shape, F32)

        def in_set(sc, v, strict):
            return (sc > v) | ((sc == v) & (strict < 0.5))

        for _ in range(TIE_REPAIR_ROUNDS):
            c_set = fold_sum(lambda sc, kp: in_set(sc, v, strict))
            vmin = fold_min(lambda sc, kp: jnp.where(in_set(sc, v, strict), sc, jnp.inf))
            c_gt = fold_sum(lambda sc, kp: sc > vmin)
            drop = (c_set > k) & (c_gt >= k)
            v = jnp.where(drop, vmin, v)
            strict = jnp.where(drop, 1.0, strict)
        c_set = fold_sum(lambda sc, kp: in_set(sc, v, strict))
        vt = fold_min(lambda sc, kp: jnp.where(in_set(sc, v, strict), sc, jnp.inf))
        need = k - fold_sum(lambda sc, kp: sc > vt)

        def idx_body(it, m):
            cand = m | lax.shift_left(I32(1), index_bits - 1 - it)
            below = fold_sum(lambda sc, kp: (sc == vt) & (kp < cand))
            return jnp.where(below < need, cand, m)

        m = lax.fori_loop(0, index_bits, idx_body, jnp.zeros(shape, I32))
        tie = c_set > k
        v_out = jnp.where(tie, vt, v)
        m_out = jnp.where(tie, m, jnp.where(strict > 0.5, -1, no_bound))
        return v_out, m_out

    return lax.cond(jnp.max(c_ge) > k, repair, lambda _: (thr, no_bound), 0)


def _chunk_loop(n, body, init):
    carry = lax.fori_loop(0, n // 2, lambda j, c: body(2 * j + 1, body(2 * j, c)), init)
    return lax.cond(n % 2 == 1, lambda c: body(n - 1, c), lambda c: c, carry)


def _dsa_prompt_kernel(q_ref, kv_ref, qia_ref, qib_ref, tailk_ref, tailq_ref, bias_ref, *rest, topk, n_cast):
    cast_in, rest = rest[:n_cast], rest[n_cast:]
    o_ref, rest = rest[0], rest[1:]
    cast_out, rest = rest[:n_cast], rest[n_cast:]
    kil_scr, kir_scr, sc_scr, lg_scr, m_scr, l_scr, acc_scr = rest
    i = pl.program_id(1)

    for src, dst in zip(cast_in, cast_out):
        dst[...] = src[...].astype(BF16)

    @pl.when(i == 0)
    def _():
        ki = tailk_ref[:, :IDX_DIM].astype(BF16)
        z = jnp.zeros_like(ki)
        kil_scr[...] = jnp.concatenate([ki, z], axis=-1)
        kir_scr[...] = jnp.concatenate([z, ki], axis=-1)

    wi = tailq_ref[:, IDX_DIM:IDX_DIM + IDX_HEADS] * (IDX_DIM ** -0.5)
    nk = (i + 2) // 2
    qpos = i * Q_BLOCK + lax.broadcasted_iota(I32, (Q_BLOCK, 1), 0)
    kloc = lax.broadcasted_iota(I32, (1, KEY_CHUNK), 1)
    heads_per_ref = qia_ref.shape[1] // IDX_DIM

    pairs = []
    for hp in range(IDX_HEADS // 2):
        ref = qia_ref if 2 * hp < heads_per_ref else qib_ref
        base = (2 * hp) % heads_per_ref * IDX_DIM
        pairs.append(ref[:, base:base + 2 * IDX_DIM])
    qstack = jnp.concatenate(pairs, axis=0)

    def score_chunk(c, carry):
        off = pl.multiple_of(c * KEY_CHUNK, KEY_CHUNK)
        acc = jnp.zeros((Q_BLOCK, KEY_CHUNK), F32)
        for side, k_scr in ((0, kil_scr), (1, kir_scr)):
            s = _dot_nt(qstack, k_scr[pl.ds(off, KEY_CHUNK), :])
            for hp in range(IDX_HEADS // 2):
                h = 2 * hp + side
                acc = acc + wi[:, h:h + 1] * jnp.maximum(s[hp * Q_BLOCK:(hp + 1) * Q_BLOCK], 0.0)
        kpos = c * KEY_CHUNK + kloc
        sc_scr[c] = jnp.where(kpos <= qpos, acc, -jnp.inf)
        return carry

    _chunk_loop(nk, score_chunk, 0)

    def fold_sum(pred):
        def body(c, cnt):
            m = jnp.where(pred(sc_scr[c], c * KEY_CHUNK + kloc), 1.0, 0.0)
            return cnt + m[:, :LANES] + m[:, LANES:]
        cnt = _chunk_loop(nk, body, jnp.zeros((Q_BLOCK, LANES), F32))
        return jnp.sum(cnt, axis=-1, keepdims=True)

    def fold_min(val):
        def body(c, acc):
            x = val(sc_scr[c], c * KEY_CHUNK + kloc)
            return jnp.minimum(acc, jnp.minimum(x[:, :LANES], x[:, LANES:]))
        acc = lax.fori_loop(0, nk, body, jnp.full((Q_BLOCK, LANES), jnp.inf, F32))
        return jnp.min(acc, axis=-1, keepdims=True)

    index_bits = (kv_ref.shape[0] - 1).bit_length()
    sel_v, sel_m = _topk_select(fold_sum, fold_min, (Q_BLOCK, 1), topk, index_bits)

    def mask_chunk(c, carry):
        kpos = c * KEY_CHUNK + kloc
        sc = sc_scr[c]
        chosen = (sc > sel_v) | ((sc == sel_v) & (kpos <= sel_m))
        sc_scr[c] = jnp.where(chosen & (kpos <= qpos), 0.0, -jnp.inf)
        return carry

    lax.fori_loop(0, nk, mask_chunk, 0)

    scale = B_HEAD_DIM ** -0.5
    rows = B_GQA * Q_BLOCK
    qgs = [jnp.concatenate(
        [q_ref[:, (g * B_GQA + hh) * B_HEAD_DIM:(g * B_GQA + hh + 1) * B_HEAD_DIM] for hh in range(B_GQA)],
        axis=0) for g in range(B_KV_HEADS)]
    m_scr[...] = jnp.full(m_scr.shape, -jnp.inf, F32)
    l_scr[...] = jnp.zeros(l_scr.shape, F32)
    acc_scr[...] = jnp.zeros(acc_scr.shape, F32)

    def logits_chunk(c, carry):
        off = pl.multiple_of(c * KEY_CHUNK, KEY_CHUNK)
        var = jnp.clip(2 * c - i, -(N_PAIR_VARIANTS - 1), 0) + (N_PAIR_VARIANTS - 1)
        sel = sc_scr[c]
        sel = jnp.concatenate([sel] * B_GQA, axis=0)
        for g in range(B_KV_HEADS):
            kc = kv_ref[pl.ds(off, KEY_CHUNK), g * B_HEAD_DIM:(g + 1) * B_HEAD_DIM]
            lg = _dot_nt(qgs[g], kc) * scale + bias_ref[var, g] + sel
            lg_scr[c, g] = lg
            m_scr[g] = jnp.maximum(m_scr[g], jnp.maximum(lg[:, :LANES], lg[:, LANES:]))
        return carry

    _chunk_loop(nk, logits_chunk, 0)
    for g in range(B_KV_HEADS):
        m = jnp.max(m_scr[g], axis=-1, keepdims=True)
        m_scr[g] = jnp.broadcast_to(m, (rows, LANES))

    def pv_chunk(c, carry):
        off = pl.multiple_of(c * KEY_CHUNK, KEY_CHUNK)
        for g in range(B_KV_HEADS):
            vc = kv_ref[pl.ds(off, KEY_CHUNK), KV_WIDTH + g * B_HEAD_DIM:KV_WIDTH + (g + 1) * B_HEAD_DIM]
            mb = m_scr[g]
            p = jnp.exp(lg_scr[c, g] - jnp.concatenate([mb, mb], axis=-1))
            l_scr[g] += p[:, :LANES] + p[:, LANES:]
            acc_scr[g] += jnp.dot(p.astype(BF16), vc, preferred_element_type=F32)
        return carry

    _chunk_loop(nk, pv_chunk, 0)
    for g in range(B_KV_HEADS):
        o = acc_scr[g] / jnp.sum(l_scr[g], axis=-1, keepdims=True)
        for hh in range(B_GQA):
            h = g * B_GQA + hh
            o_ref[:, h * B_HEAD_DIM:(h + 1) * B_HEAD_DIM] = o[hh * Q_BLOCK:(hh + 1) * Q_BLOCK].astype(BF16)


def _dsa_prompt(p, tail, bias_pairs, cast_weights, *, batch, seq):
    topk = min(TOPK_MAX, seq // 4)
    nblk = seq // Q_BLOCK
    qi_w = PROJ_TN
    steps = batch * nblk
    cast_specs, cast_shapes = [], []
    for w in cast_weights:
        ne, r, c = w.shape
        parts = steps // ne
        assert steps == ne * parts and r % (parts * 16) == 0
        spec = pl.BlockSpec((None, r // parts, c),
                            lambda b, i, parts=parts: ((b * nblk + i) // parts, (b * nblk + i) % parts, 0))
        cast_specs.append(spec)
        cast_shapes.append(jax.ShapeDtypeStruct(w.shape, BF16))
    outs = pl.pallas_call(
        functools.partial(_dsa_prompt_kernel, topk=topk, n_cast=len(cast_weights)),
        grid=(batch, nblk),
        in_specs=[
            pl.BlockSpec((Q_BLOCK, B_WIDTH), lambda b, i: (b * nblk + i, COL_Q // B_WIDTH)),
            pl.BlockSpec((seq, 2 * KV_WIDTH), lambda b, i: (b, COL_K // (2 * KV_WIDTH))),
            pl.BlockSpec((Q_BLOCK, qi_w), lambda b, i: (b * nblk + i, COL_QI // qi_w)),
            pl.BlockSpec((Q_BLOCK, qi_w), lambda b, i: (b * nblk + i, COL_QI // qi_w + 1)),
            pl.BlockSpec((seq, LANES), lambda b, i: (b, 0)),
            pl.BlockSpec((Q_BLOCK, LANES), lambda b, i: (b * nblk + i, 0)),
            pl.BlockSpec(bias_pairs.shape, lambda b, i: (0, 0, 0, 0)),
        ] + cast_specs,
        out_specs=[pl.BlockSpec((Q_BLOCK, B_WIDTH), lambda b, i: (b * nblk + i, 0))] + cast_specs,
        out_shape=[jax.ShapeDtypeStruct((batch * seq, B_WIDTH), BF16)] + cast_shapes,
        scratch_shapes=[
            pltpu.VMEM((seq, 2 * IDX_DIM), BF16),
            pltpu.VMEM((seq, 2 * IDX_DIM), BF16),
            pltpu.VMEM((seq // KEY_CHUNK, Q_BLOCK, KEY_CHUNK), F32),
            pltpu.VMEM((seq // KEY_CHUNK, B_KV_HEADS, B_GQA * Q_BLOCK, KEY_CHUNK), F32),
            pltpu.VMEM((B_KV_HEADS, B_GQA * Q_BLOCK, LANES), F32),
            pltpu.VMEM((B_KV_HEADS, B_GQA * Q_BLOCK, LANES), F32),
            pltpu.VMEM((B_KV_HEADS, B_GQA * Q_BLOCK, B_HEAD_DIM), F32),
        ],
        compiler_params=_cparams(("arbitrary", "arbitrary")),
        name="dsa_prompt",
    )(p, p, p, p, tail, tail, bias_pairs, *cast_weights)
    return outs[0], tuple(outs[1:])


PAGES_PER_STEP = 16
SAMPLE_SUB_PAGES = 16
SCORE_KEYS_PER_DOT = 2048
FOLD_WAYS = 8


def _fetch_pages(pt_ref, sources, bufs, sem, step, n_steps_total, steps_per_sample, pages_per_step):
    slot = step % 2

    def copies(n, slot_, page_of):
        sample = n // steps_per_sample
        first = (n % steps_per_sample) * pages_per_step
        for r in range(pages_per_step):
            page = page_of(sample, first + r)
            for a, (src, buf) in enumerate(zip(sources, bufs)):
                yield pltpu.make_async_copy(src.at[page], buf.at[slot_, r], sem.at[a, slot_])

    def table(sample, idx):
        return pt_ref[sample, idx]

    @pl.when(step == 0)
    def _():
        for cp in copies(step, slot, table):
            cp.start()

    @pl.when(step + 1 < n_steps_total)
    def _():
        for cp in copies(step + 1, 1 - slot, table):
            cp.start()

    for cp in copies(step, slot, lambda sample, idx: 0):
        cp.wait()
    return slot


def _dsa_sample_score_kernel(pt_ref, idx_hbm, qi_ref, wi_ref, kin_ref, sc_ref, scn_ref, selv_ref, selm_ref,
                             page_buf, sem, *, topk, dec_seq, n_pages, n_samples):
    slot = _fetch_pages(pt_ref, [idx_hbm], [page_buf], sem, pl.program_id(0), n_samples, 1, n_pages)
    pages = [page_buf.at[slot, r] for r in range(n_pages)]
    past = n_pages * PAGE_SIZE
    pages_per_dot = SCORE_KEYS_PER_DOT // PAGE_SIZE
    qi = qi_ref[...]
    wi = wi_ref[...] * (IDX_DIM ** -0.5)

    def scores(keys_t):
        r = jnp.maximum(jnp.dot(qi, keys_t, preferred_element_type=F32), 0.0) * wi
        acc = r[0:SUBLANES]
        for j in range(1, IDX_HEADS * dec_seq // SUBLANES):
            acc = acc + r[j * SUBLANES:(j + 1) * SUBLANES]
        shift = dec_seq
        while shift < SUBLANES:
            acc = acc + pltpu.roll(acc, shift, axis=0)
            shift *= 2
        return acc

    for d in range(n_pages // pages_per_dot):
        keys_t = jnp.concatenate(
            [pg[...].astype(BF16) for pg in pages[d * pages_per_dot:(d + 1) * pages_per_dot]], axis=1)
        sc_ref[:, d * SCORE_KEYS_PER_DOT:(d + 1) * SCORE_KEYS_PER_DOT] = scores(keys_t)

    row = lax.broadcasted_iota(I32, (SUBLANES, LANES), 0)
    col = lax.broadcasted_iota(I32, (SUBLANES, LANES), 1)
    scn = jnp.where((col <= row % dec_seq) & (col < dec_seq), scores(kin_ref[...]), -jnp.inf)
    scn_ref[...] = scn

    def pieces():
        yield scn, past + col[0:1]
        for w in range(past // LANES):
            yield sc_ref[:, w * LANES:(w + 1) * LANES], w * LANES + col[0:1]

    def fold(term, combine):
        parts = [None] * FOLD_WAYS
        for n, (sc, kp) in enumerate(pieces()):
            x = term(sc, kp)
            parts[n % FOLD_WAYS] = x if parts[n % FOLD_WAYS] is None else combine(parts[n % FOLD_WAYS], x)
        parts = [p for p in parts if p is not None]
        while len(parts) > 1:
            parts = [combine(parts[j], parts[j + 1]) if j + 1 < len(parts) else parts[j]
                     for j in range(0, len(parts), 2)]
        return parts[0]

    def fold_sum(pred):
        cnt = fold(lambda sc, kp: jnp.where(pred(sc, kp), 1.0, 0.0), jnp.add)
        return jnp.sum(cnt, axis=-1, keepdims=True)

    def fold_min(val):
        return jnp.min(fold(val, jnp.minimum), axis=-1, keepdims=True)

    index_bits = (past + LANES - 1).bit_length()
    sel_v, sel_m = _topk_select(fold_sum, fold_min, (SUBLANES, 1), topk, index_bits)
    selv_ref[...] = jnp.broadcast_to(sel_v, (SUBLANES, LANES))
    selm_ref[...] = jnp.broadcast_to(sel_m, (SUBLANES, LANES))


def _dsa_sample_attend_kernel(pt_ref, ck_hbm, cv_hbm, q_ref, sc_ref, scn_ref, selv_ref, selm_ref, kvn_ref,
                              sbias_ref, sconst_ref, o_ref, kbuf, vbuf, sem, m_scr, l_scr, acc_scr,
                              *, dec_seq, n_steps, n_samples):
    s = pl.program_id(1)
    slot = _fetch_pages(pt_ref, [ck_hbm, cv_hbm], [kbuf, vbuf], sem, pl.program_id(0) * n_steps + s,
                        n_samples * n_steps, n_steps, PAGES_PER_STEP)
    kpages = [kbuf.at[slot, r] for r in range(PAGES_PER_STEP)]
    vpages = [vbuf.at[slot, r] for r in range(PAGES_PER_STEP)]
    chunk = PAGES_PER_STEP * PAGE_SIZE
    rows = B_GQA * SUBLANES
    scale = B_HEAD_DIM ** -0.5

    @pl.when(s == 0)
    def _():
        m_scr[...] = jnp.full(m_scr.shape, NEG_INIT, F32)
        l_scr[...] = jnp.zeros(l_scr.shape, F32)
        acc_scr[...] = jnp.zeros(acc_scr.shape, F32)

    sel_v = selv_ref[:, 0:1]
    sel_m = selm_ref[:, 0:1]

    def chosen(sc, kpos):
        return jnp.where((sc > sel_v) | ((sc == sel_v) & (kpos <= sel_m)), 0.0, -jnp.inf)

    def head_rows(pgs, g):
        return jnp.concatenate(
            [pg[pl.ds(g, PAGE_SIZE, stride=B_KV_HEADS), :].astype(BF16) for pg in pgs], axis=0)

    kcol = lax.broadcasted_iota(I32, (SUBLANES, chunk), 1)
    sel = chosen(sc_ref[...], s * chunk + kcol)
    sel = jnp.concatenate([sel] * B_GQA, axis=0)
    is_last = s == n_steps - 1
    ncol = lax.broadcasted_iota(I32, (SUBLANES, LANES), 1)
    seln = chosen(scn_ref[...], n_steps * chunk + ncol)
    seln = jnp.where(is_last & (scn_ref[...] > -jnp.inf), seln, -jnp.inf)
    seln = jnp.concatenate([seln] * B_GQA, axis=0)

    def partial(lg, v):
        m = jnp.maximum(jnp.max(lg, axis=-1, keepdims=True), NEG_INIT)
        p = jnp.exp(lg - m)
        return m, jnp.sum(p, axis=-1, keepdims=True), jnp.dot(p.astype(BF16), v, preferred_element_type=F32)

    sub = SAMPLE_SUB_PAGES * PAGE_SIZE
    nsub = PAGES_PER_STEP // SAMPLE_SUB_PAGES
    lane_sub = lax.broadcasted_iota(I32, (rows, sub), 1)
    for g in range(B_KV_HEADS):
        qg = q_ref[g]
        far = sconst_ref[g, :, 0:1]
        near = jnp.concatenate([jnp.zeros((rows, sub - PAGE_SIZE), F32), sbias_ref[g, :, :PAGE_SIZE]], axis=-1)
        parts = [(m_scr[g], l_scr[g], acc_scr[g])]
        for j in range(nsub):
            pj = slice(j * SAMPLE_SUB_PAGES, (j + 1) * SAMPLE_SUB_PAGES)
            lg = _dot_nt(qg, head_rows(kpages[pj], g)) * scale + sel[:, j * sub:(j + 1) * sub]
            if j == nsub - 1:
                lg = lg + jnp.where(is_last & (lane_sub >= sub - PAGE_SIZE), near, far)
            else:
                lg = lg + far
            parts.append(partial(lg, head_rows(vpages[pj], g)))
        kn = kvn_ref[:, g * B_HEAD_DIM:(g + 1) * B_HEAD_DIM]
        vn = kvn_ref[:, KV_WIDTH + g * B_HEAD_DIM:KV_WIDTH + (g + 1) * B_HEAD_DIM]
        parts.append(partial(_dot_nt(qg, kn) * scale + sbias_ref[g, :, PAGE_SIZE:] + seln, vn))
        m_new = parts[0][0]
        for m, _, _ in parts[1:]:
            m_new = jnp.maximum(m_new, m)
        l_new = jnp.zeros((rows, 1), F32)
        acc_new = jnp.zeros((rows, B_HEAD_DIM), F32)
        for m, l, acc in parts:
            w = jnp.exp(m - m_new)
            l_new = l_new + w * l
            acc_new = acc_new + w * acc
        m_scr[g] = m_new
        l_scr[g] = l_new
        acc_scr[g] = acc_new

    @pl.when(is_last)
    def _():
        for g in range(B_KV_HEADS):
            o_ref[g] = (acc_scr[g] / l_scr[g]).astype(BF16)


def _dsa_sample(p_s, tail_s, k_s, v_s, cache_k_l, cache_v_l, cache_idx_l, page_table, sbias, sconst,
                *, dec_batch, dec_seq):
    n_pages = page_table.shape[1]
    past = n_pages * PAGE_SIZE
    topk = min(TOPK_MAX, (past + dec_seq) // 4)
    n_steps = n_pages // PAGES_PER_STEP
    chunk = PAGES_PER_STEP * PAGE_SIZE
    rows = B_GQA * SUBLANES
    n_pool = cache_k_l.shape[0]

    assert SUBLANES % dec_seq == 0
    qi = p_s[:, COL_QI:COL_MAIN].reshape(dec_batch, dec_seq, IDX_HEADS, IDX_DIM).transpose(0, 2, 1, 3)
    qi = qi.reshape(dec_batch, IDX_HEADS * dec_seq, IDX_DIM)
    wi = tail_s[:, IDX_DIM:IDX_DIM + IDX_HEADS].reshape(dec_batch, dec_seq, IDX_HEADS).transpose(0, 2, 1)
    wi = wi.reshape(dec_batch, IDX_HEADS * dec_seq, 1)
    ki_new = tail_s[:, :IDX_DIM].astype(BF16).reshape(dec_batch, dec_seq, IDX_DIM)
    ki_new = jnp.pad(ki_new, ((0, 0), (0, LANES - dec_seq), (0, 0))).transpose(0, 2, 1)
    q = p_s[:, COL_Q:COL_K].reshape(dec_batch, dec_seq, B_KV_HEADS, B_GQA, B_HEAD_DIM)
    q = jnp.pad(q.transpose(0, 2, 3, 1, 4), ((0, 0), (0, 0), (0, 0), (0, SUBLANES - dec_seq), (0, 0)))
    q = q.reshape(dec_batch, B_KV_HEADS, rows, B_HEAD_DIM)
    kv_new = jnp.concatenate([k_s, v_s], axis=-1).astype(BF16).reshape(dec_batch, dec_seq, 2 * KV_WIDTH)
    kv_new = jnp.pad(kv_new, ((0, 0), (0, LANES - dec_seq), (0, 0)))

    cache_idx_t = jnp.swapaxes(cache_idx_l, 1, 2)
    stat_spec = pl.BlockSpec((None, SUBLANES, LANES), lambda b, pt: (b, 0, 0))
    sc, scn, selv, selm = pl.pallas_call(
        functools.partial(_dsa_sample_score_kernel, topk=topk, dec_seq=dec_seq, n_pages=n_pages,
                          n_samples=dec_batch),
        grid_spec=pltpu.PrefetchScalarGridSpec(
            num_scalar_prefetch=1,
            grid=(dec_batch,),
            in_specs=[
                pl.BlockSpec(memory_space=pl.ANY),
                pl.BlockSpec((None, IDX_HEADS * dec_seq, IDX_DIM), lambda b, pt: (b, 0, 0)),
                pl.BlockSpec((None, IDX_HEADS * dec_seq, 1), lambda b, pt: (b, 0, 0)),
                pl.BlockSpec((None, IDX_DIM, LANES), lambda b, pt: (b, 0, 0)),
            ],
            out_specs=[pl.BlockSpec((None, SUBLANES, past), lambda b, pt: (b, 0, 0)),
                       stat_spec, stat_spec, stat_spec],
            scratch_shapes=[
                pltpu.VMEM((2, n_pages, IDX_DIM, PAGE_SIZE), F32),
                pltpu.SemaphoreType.DMA((1, 2)),
            ],
        ),
        out_shape=[
            jax.ShapeDtypeStruct((dec_batch, SUBLANES, past), F32),
            jax.ShapeDtypeStruct((dec_batch, SUBLANES, LANES), F32),
            jax.ShapeDtypeStruct((dec_batch, SUBLANES, LANES), F32),
            jax.ShapeDtypeStruct((dec_batch, SUBLANES, LANES), I32),
        ],
        compiler_params=_cparams(("arbitrary",)),
        name="dsa_sample_score",
    )(page_table, cache_idx_t, qi, wi, ki_new)

    ck = cache_k_l.reshape(n_pool, PAGE_SIZE * B_KV_HEADS, B_HEAD_DIM)
    cv = cache_v_l.reshape(n_pool, PAGE_SIZE * B_KV_HEADS, B_HEAD_DIM)
    kv_buf = pltpu.VMEM((2, PAGES_PER_STEP, PAGE_SIZE * B_KV_HEADS, B_HEAD_DIM), F32)
    o = pl.pallas_call(
        functools.partial(_dsa_sample_attend_kernel, dec_seq=dec_seq, n_steps=n_steps, n_samples=dec_batch),
        grid_spec=pltpu.PrefetchScalarGridSpec(
            num_scalar_prefetch=1,
            grid=(dec_batch, n_steps),
            in_specs=[pl.BlockSpec(memory_space=pl.ANY), pl.BlockSpec(memory_space=pl.ANY)] + [
                pl.BlockSpec((None, B_KV_HEADS, rows, B_HEAD_DIM), lambda b, s, pt: (b, 0, 0, 0)),
                pl.BlockSpec((None, SUBLANES, chunk), lambda b, s, pt: (b, 0, s)),
                pl.BlockSpec((None, SUBLANES, LANES), lambda b, s, pt: (b, 0, 0)),
                pl.BlockSpec((None, SUBLANES, LANES), lambda b, s, pt: (b, 0, 0)),
                pl.BlockSpec((None, SUBLANES, LANES), lambda b, s, pt: (b, 0, 0)),
                pl.BlockSpec((None, LANES, 2 * KV_WIDTH), lambda b, s, pt: (b, 0, 0)),
                pl.BlockSpec(sbias.shape, lambda b, s, pt: (0, 0, 0)),
                pl.BlockSpec(sconst.shape, lambda b, s, pt: (0, 0, 0)),
            ],
            out_specs=pl.BlockSpec((None, B_KV_HEADS, rows, B_HEAD_DIM), lambda b, s, pt: (b, 0, 0, 0)),
            scratch_shapes=[
                kv_buf,
                kv_buf,
                pltpu.SemaphoreType.DMA((2, 2)),
                pltpu.VMEM((B_KV_HEADS, rows, 1), F32),
                pltpu.VMEM((B_KV_HEADS, rows, 1), F32),
                pltpu.VMEM((B_KV_HEADS, rows, B_HEAD_DIM), F32),
            ],
        ),
        out_shape=jax.ShapeDtypeStruct((dec_batch, B_KV_HEADS, rows, B_HEAD_DIM), BF16),
        compiler_params=_cparams(("arbitrary", "arbitrary")),
        name="dsa_sample_attend",
    )(page_table, ck, cv, q, sc, scn, selv, selm, kv_new, sbias, sconst)
    o = o.reshape(dec_batch, B_KV_HEADS, B_GQA, SUBLANES, B_HEAD_DIM)[:, :, :, :dec_seq].transpose(0, 3, 1, 2, 4)
    return o.reshape(dec_batch * dec_seq, B_WIDTH)


def _mixout_kernel(u_ref, vn_ref, b_ref, ws_ref, ab_ref, x_ref, w_ref, o_ref, mix_scr, *, tm):
    j = pl.program_id(1)

    @pl.when(j == 0)
    def _():
        row = lax.broadcasted_iota(I32, (A_CHUNK, A_CHUNK), 0)
        col = lax.broadcasted_iota(I32, (A_CHUNK, A_CHUNK), 1)
        for g in range(A_GROUPS):
            wsg = jnp.where(row >= col, ws_ref[g], 0.0).astype(BF16)
            bg = ab_ref[:, g:g + 1]
            cs = slice(g * A_CH, (g + 1) * A_CH)
            for r in range(tm // A_CHUNK):
                rs = slice(r * A_CHUNK, (r + 1) * A_CHUNK)
                sm = jnp.dot(wsg, vn_ref[rs, cs], preferred_element_type=F32) + bg
                mix_scr[rs, cs] = (u_ref[rs, cs].astype(F32) * sm).astype(BF16)
        mix_scr[:, A_WIDTH:] = b_ref[...]

    o_ref[...] = x_ref[...] + jnp.dot(mix_scr[...], w_ref[...].astype(BF16), preferred_element_type=F32)


def _mixout(p, b_out, ws_eff, ab_eff, x2d, w_out_l, *, tm, tn=1024):
    n = x2d.shape[0]
    return pl.pallas_call(
        functools.partial(_mixout_kernel, tm=tm),
        grid=(n // tm, D_MODEL // tn),
        in_specs=[
            pl.BlockSpec((tm, A_WIDTH), lambda i, j: (i, COL_U // A_WIDTH)),
            pl.BlockSpec((tm, A_WIDTH), lambda i, j: (i, COL_V // A_WIDTH)),
            pl.BlockSpec((tm, B_WIDTH), lambda i, j: (i, 0)),
            pl.BlockSpec((A_GROUPS, A_CHUNK, A_CHUNK), lambda i, j: (0, 0, 0)),
            pl.BlockSpec((A_CHUNK, A_GROUPS), lambda i, j: (0, 0)),
            pl.BlockSpec((tm, tn), lambda i, j: (i, j)),
            pl.BlockSpec((A_WIDTH + B_WIDTH, tn), lambda i, j: (0, j)),
        ],
        out_specs=pl.BlockSpec((tm, tn), lambda i, j: (i, j)),
        out_shape=jax.ShapeDtypeStruct((n, D_MODEL), F32),
        scratch_shapes=[pltpu.VMEM((tm, A_WIDTH + B_WIDTH), BF16)],
        compiler_params=_cparams(("arbitrary", "arbitrary")),
        name="mixout",
    )(p, p, b_out, ws_eff, ab_eff, x2d, w_out_l)


def _memkv_kernel(x_ref, wk_ref, wv_ref, kn_ref, mk_ref, mv_ref):
    x = x_ref[...].astype(BF16)
    tm = x.shape[0]
    mk = _head_norm(jnp.dot(x, wk_ref[...].astype(BF16), preferred_element_type=F32), kn_ref[...])
    mv = jnp.dot(x, wv_ref[...].astype(BF16), preferred_element_type=F32)
    for hd in range(MEM_HEADS):
        cs = slice(hd * MEM_HEAD_DIM, (hd + 1) * MEM_HEAD_DIM)
        mk_ref[pl.ds(hd, tm, stride=MEM_HEADS), :] = mk[:, cs]
        mv_ref[pl.ds(hd, tm, stride=MEM_HEADS), :] = mv[:, cs]


def _memkv(mem2d, wk, wv, knorm, *, tm=256):
    n = mem2d.shape[0]
    kn = jnp.tile(knorm, MEM_HEADS).reshape(1, MEM_WIDTH)
    return pl.pallas_call(
        _memkv_kernel,
        grid=(n // tm,),
        in_specs=[
            pl.BlockSpec((tm, D_MODEL), lambda i: (i, 0)),
            pl.BlockSpec((D_MODEL, MEM_WIDTH), lambda i: (0, 0)),
            pl.BlockSpec((D_MODEL, MEM_WIDTH), lambda i: (0, 0)),
            pl.BlockSpec((1, MEM_WIDTH), lambda i: (0, 0)),
        ],
        out_specs=[pl.BlockSpec((tm * MEM_HEADS, MEM_HEAD_DIM), lambda i: (i, 0))] * 2,
        out_shape=[jax.ShapeDtypeStruct((n * MEM_HEADS, MEM_HEAD_DIM), F32)] * 2,
        compiler_params=_cparams(("arbitrary",)),
        name="memkv",
    )(mem2d, wk, wv, kn)


def _memattn_kernel(h_ref, g_ref, wq_ref, qn_ref, mk_ref, mv_ref, wo_ref, o_ref, *, rows_per_mem, mem_len):
    h = h_ref[...]
    n = _row_norm(h, g_ref[...]).astype(BF16)
    q = _head_norm(jnp.dot(n, wq_ref[...].astype(BF16), preferred_element_type=F32), qn_ref[...]).astype(BF16)
    tm, m = h.shape[0], mk_ref.shape[0] // MEM_HEADS
    if rows_per_mem is not None:
        row = lax.broadcasted_iota(I32, (tm, m), 0)
        col = lax.broadcasted_iota(I32, (tm, m), 1)
        mask = jnp.where(row // rows_per_mem == col // mem_len, 0.0, -jnp.inf)
    outs = []
    for hd in range(MEM_HEADS):
        cs = slice(hd * MEM_HEAD_DIM, (hd + 1) * MEM_HEAD_DIM)
        mk = mk_ref[pl.ds(hd, m, stride=MEM_HEADS), :].astype(BF16)
        mv = mv_ref[pl.ds(hd, m, stride=MEM_HEADS), :].astype(BF16)
        lg = _dot_nt(q[:, cs], mk) * (MEM_HEAD_DIM ** -0.5)
        if rows_per_mem is not None:
            lg = lg + mask
        e = jnp.exp(lg - jnp.max(lg, axis=-1, keepdims=True))
        l = jnp.sum(e, axis=-1, keepdims=True)
        outs.append(jnp.dot(e.astype(BF16), mv, preferred_element_type=F32) / l)
    o = jnp.concatenate(outs, axis=-1).astype(BF16)
    o_ref[...] = h + jnp.dot(o, wo_ref[...].astype(BF16), preferred_element_type=F32)


def _memattn(h2d, gain, wq, qnorm, mk, mv, wo, *, tm, mem_rows, tiles_per_mem, rows_per_mem, mem_len):
    n = h2d.shape[0]
    qn = jnp.tile(qnorm, MEM_HEADS).reshape(1, MEM_WIDTH)
    return pl.pallas_call(
        functools.partial(_memattn_kernel, rows_per_mem=rows_per_mem, mem_len=mem_len),
        grid=(n // tm,),
        in_specs=[
            pl.BlockSpec((tm, D_MODEL), lambda i: (i, 0)),
            pl.BlockSpec((1, D_MODEL), lambda i: (0, 0)),
            pl.BlockSpec((D_MODEL, MEM_WIDTH), lambda i: (0, 0)),
            pl.BlockSpec((1, MEM_WIDTH), lambda i: (0, 0)),
            pl.BlockSpec((mem_rows * MEM_HEADS, MEM_HEAD_DIM), lambda i: (i // tiles_per_mem, 0)),
            pl.BlockSpec((mem_rows * MEM_HEADS, MEM_HEAD_DIM), lambda i: (i // tiles_per_mem, 0)),
            pl.BlockSpec((MEM_WIDTH, D_MODEL), lambda i: (0, 0)),
        ],
        out_specs=pl.BlockSpec((tm, D_MODEL), lambda i: (i, 0)),
        out_shape=jax.ShapeDtypeStruct((n, D_MODEL), F32),
        compiler_params=_cparams(("arbitrary",)),
        name="memattn",
    )(h2d, gain.reshape(1, D_MODEL), wq, qn, mk, mv, wo)


def _router_kernel(h_ref, g_ref, w_ref, b_ref, n_ref, c_ref):
    n = _row_norm(h_ref[...], g_ref[...]).astype(BF16)
    n_ref[...] = n
    lg = jnp.dot(n, w_ref[...].astype(BF16), preferred_element_type=F32) + b_ref[...]
    gl, el = lg[:, :LANES], lg[:, LANES:]
    lane = lax.broadcasted_iota(I32, gl.shape, 1)
    lanef = lane.astype(F32)

    def first_max_lane(v, vmax):
        return jnp.min(jnp.where(v == vmax, lanef, float(LANES)), axis=-1, keepdims=True).astype(I32)

    gl = jnp.where(lane < N_GROUPS, gl, -jnp.inf)
    ge = jnp.exp(gl - jnp.max(gl, axis=-1, keepdims=True))
    gp = ge / jnp.sum(ge, axis=-1, keepdims=True)
    g_gate = jnp.max(gp, axis=-1, keepdims=True)
    g_sel = first_max_lane(gp, g_gate)
    in_grp = (lane // EXPERTS_PER_GROUP == g_sel) & (lane < N_EXPERTS)
    el = jnp.where(in_grp, el, -jnp.inf)
    ee = jnp.exp(el - jnp.max(el, axis=-1, keepdims=True))
    ep = jnp.where(in_grp, ee / jnp.sum(ee, axis=-1, keepdims=True), -jnp.inf)
    w1 = jnp.max(ep, axis=-1, keepdims=True)
    i1 = first_max_lane(ep, w1)
    ep2 = jnp.where(lane == i1, -jnp.inf, ep)
    w2 = jnp.max(ep2, axis=-1, keepdims=True)
    i2 = first_max_lane(ep2, w2)
    tot = w1 + w2
    comb = jnp.where(lane == i1, w1 / tot * g_gate, jnp.where(lane == i2, w2 / tot * g_gate, 0.0))
    c_ref[...] = jnp.where(lane == GSEL_LANE, g_sel.astype(F32), comb)


def _router(h2d, gain, w_cat, b_cat, *, tm):
    n = h2d.shape[0]
    return pl.pallas_call(
        _router_kernel,
        grid=(n // tm,),
        in_specs=[
            pl.BlockSpec((tm, D_MODEL), lambda i: (i, 0)),
            pl.BlockSpec((1, D_MODEL), lambda i: (0, 0)),
            pl.BlockSpec((D_MODEL, 2 * LANES), lambda i: (0, 0)),
            pl.BlockSpec((1, 2 * LANES), lambda i: (0, 0)),
        ],
        out_specs=[pl.BlockSpec((tm, D_MODEL), lambda i: (i, 0)), pl.BlockSpec((tm, LANES), lambda i: (i, 0))],
        out_shape=[jax.ShapeDtypeStruct((n, D_MODEL), BF16), jax.ShapeDtypeStruct((n, LANES), F32)],
        compiler_params=_cparams(("arbitrary",)),
        name="router",
    )(h2d, gain.reshape(1, D_MODEL), w_cat, b_cat)


def _cast_kernel(x_ref, o_ref):
    o_ref[...] = x_ref[...].astype(BF16)


def _cast_bf16(w):
    ne, r, c = w.shape
    return pl.pallas_call(
        _cast_kernel,
        grid=(ne,),
        in_specs=[pl.BlockSpec((None, r, c), lambda e: (e, 0, 0))],
        out_specs=pl.BlockSpec((None, r, c), lambda e: (e, 0, 0)),
        out_shape=jax.ShapeDtypeStruct(w.shape, BF16),
        compiler_params=_cparams(("arbitrary",)),
        name="cast_bf16",
    )(w)


GSEL_LANE = N_EXPERTS
MOE_ROWS = 128
MOE_MAIN_SLACK = 0
MOE_FF_SPLIT = 1


def _moe_routed_kernel(n_ref, c_ref, h_hbm, wg_ref, wu_ref, wd_ref, o_ref,
                       tri_scr, key_scr, xc_scr, yc_scr, cw_scr, nch_ref, sem, *, tm):
    i, g, e, f = (pl.program_id(k) for k in range(4))
    main_rows = tm // N_GROUPS + MOE_MAIN_SLACK
    first_e = (e == 0) & (f == 0)
    last_e = (e == EXPERTS_PER_GROUP - 1) & (f == MOE_FF_SPLIT - 1)

    def onehot_rows(start, rows):
        want = (start + lax.broadcasted_iota(I32, (rows, 1), 0)).astype(F32)
        return jnp.where(key_scr[pl.ds(g, 1), :] == want, 1.0, 0.0).astype(BF16)

    def tail_start(c):
        return pl.multiple_of(main_rows + c * MOE_ROWS, math.gcd(main_rows, MOE_ROWS))

    @pl.when((i == 0) & (g == 0) & first_e)
    def _():
        r = lax.broadcasted_iota(I32, (tm, tm), 0)
        c = lax.broadcasted_iota(I32, (tm, tm), 1)
        tri_scr[...] = jnp.where(r < c, 1.0, 0.0).astype(BF16)

    @pl.when((g == 0) & first_e)
    def _():
        cp = pltpu.make_async_copy(h_hbm.at[pl.ds(pl.multiple_of(i * tm, tm), tm)], o_ref, sem)
        cp.start()
        lane = lax.broadcasted_iota(I32, (tm, LANES), 1).astype(F32)
        onehot = jnp.where(lane == c_ref[:, GSEL_LANE:GSEL_LANE + 1], 1.0, 0.0).astype(BF16)
        er = lax.broadcasted_iota(I32, (SUBLANES, LANES), 0)
        ec = lax.broadcasted_iota(I32, (SUBLANES, LANES), 1)
        eye = jnp.where(er == ec, 1.0, 0.0).astype(BF16)
        mask_t = _dot_nt(eye, onehot)
        rank_t = jnp.dot(mask_t.astype(BF16), tri_scr[...], preferred_element_type=F32)
        key_scr[...] = jnp.where(mask_t > 0.5, rank_t, -1.0)
        for gg in range(N_GROUPS):
            cnt = jnp.sum(mask_t[gg:gg + 1, :], axis=-1, keepdims=True)
            extra = jnp.ceil(jnp.maximum(cnt - main_rows, 0.0) * (1.0 / MOE_ROWS))
            nch_ref[gg] = jnp.sum(extra).astype(I32)
        cp.wait()

    ntail = nch_ref[g]

    @pl.when(first_e)
    def _():
        comb = c_ref[...]
        hi = comb.astype(BF16)
        r1 = comb - hi.astype(F32)
        mid = r1.astype(BF16)
        lo = (r1 - mid.astype(F32)).astype(BF16)

        def compact(start, rows):
            s = onehot_rows(start, rows)
            xc_scr[pl.ds(start, rows), :] = jnp.dot(s, n_ref[...], preferred_element_type=F32).astype(BF16)
            cw_scr[pl.ds(start, rows), :] = (jnp.dot(s, hi, preferred_element_type=F32)
                                             + jnp.dot(s, mid, preferred_element_type=F32)
                                             + jnp.dot(s, lo, preferred_element_type=F32))
            yc_scr[pl.ds(start, rows), :] = jnp.zeros((rows, D_MODEL), F32)

        compact(0, main_rows)
        lax.fori_loop(0, ntail, lambda c, carry: (compact(tail_start(c), MOE_ROWS), carry)[1], 0)

    wg = wg_ref[...].astype(BF16)
    wu = wu_ref[...].astype(BF16)
    wd = wd_ref[...].astype(BF16)
    ex = g * EXPERTS_PER_GROUP + e

    def expert(start, rows):
        x = xc_scr[pl.ds(start, rows), :]
        a = jnp.dot(x, wg, preferred_element_type=F32)
        u = jnp.dot(x, wu, preferred_element_type=F32)
        lane = lax.broadcasted_iota(I32, (rows, LANES), 1)
        cw = jnp.sum(jnp.where(lane == ex, cw_scr[pl.ds(start, rows), :], 0.0), axis=-1, keepdims=True)
        hid = (a * jax.nn.sigmoid(a) * u * cw).astype(BF16)
        yc_scr[pl.ds(start, rows), :] += jnp.dot(hid, wd, preferred_element_type=F32)

    expert(0, main_rows)
    lax.fori_loop(0, ntail, lambda c, carry: (expert(tail_start(c), MOE_ROWS), carry)[1], 0)

    @pl.when(last_e)
    def _():
        def scatter(start, rows):
            s = onehot_rows(start, rows)
            y = yc_scr[pl.ds(start, rows), :].astype(BF16)
            o_ref[...] += lax.dot_general(s, y, (((0,), (0,)), ((), ())), preferred_element_type=F32)

        scatter(0, main_rows)
        lax.fori_loop(0, ntail, lambda c, carry: (scatter(tail_start(c), MOE_ROWS), carry)[1], 0)


def _moe_routed(n3, comb, h2d, w_gate_l, w_up_l, w_down_l, *, tm):
    n = h2d.shape[0]
    ff = EXPERT_FF // MOE_FF_SPLIT

    def wmap(i, g, e, f):
        return (g * EXPERTS_PER_GROUP + e, 0, f)

    return pl.pallas_call(
        functools.partial(_moe_routed_kernel, tm=tm),
        grid=(n // tm, N_GROUPS, EXPERTS_PER_GROUP, MOE_FF_SPLIT),
        in_specs=[
            pl.BlockSpec((tm, D_MODEL), lambda i, g, e, f: (i, 0), pipeline_mode=pl.Buffered(1)),
            pl.BlockSpec((tm, LANES), lambda i, g, e, f: (i, 0), pipeline_mode=pl.Buffered(1)),
            pl.BlockSpec(memory_space=pl.ANY),
            pl.BlockSpec((None, D_MODEL, ff), wmap),
            pl.BlockSpec((None, D_MODEL, ff), wmap),
            pl.BlockSpec((None, ff, D_MODEL), lambda i, g, e, f: (g * EXPERTS_PER_GROUP + e, f, 0)),
        ],
        out_specs=pl.BlockSpec((tm, D_MODEL), lambda i, g, e, f: (i, 0)),
        out_shape=jax.ShapeDtypeStruct((n, D_MODEL), F32),
        scratch_shapes=[
            pltpu.VMEM((tm, tm), BF16),
            pltpu.VMEM((SUBLANES, tm), F32),
            pltpu.VMEM((tm + MOE_ROWS, D_MODEL), BF16),
            pltpu.VMEM((tm + MOE_ROWS, D_MODEL), F32),
            pltpu.VMEM((tm + MOE_ROWS, LANES), F32),
            pltpu.SMEM((N_GROUPS,), I32),
            pltpu.SemaphoreType.DMA(()),
        ],
        compiler_params=pltpu.CompilerParams(
            dimension_semantics=("arbitrary",) * 4, vmem_limit_bytes=MOE_VMEM_LIMIT),
        name="moe_routed",
    )(n3, comb, h2d, w_gate_l, w_up_l, w_down_l)


def _moe_kernel(n_ref, c_ref, h_ref, wg_ref, wu_ref, wd_ref, o_ref):
    e = pl.program_id(1)

    @pl.when(e == 0)
    def _():
        o_ref[...] = h_ref[...]

    x = n_ref[...]
    a = jnp.dot(x, wg_ref[...].astype(BF16), preferred_element_type=F32)
    u = jnp.dot(x, wu_ref[...].astype(BF16), preferred_element_type=F32)
    lane = lax.broadcasted_iota(I32, c_ref.shape, 1)
    c = jnp.sum(jnp.where(lane == e, c_ref[...], 0.0), axis=-1, keepdims=True)
    hid = (a * jax.nn.sigmoid(a) * u * c).astype(BF16)
    o_ref[...] += jnp.dot(hid, wd_ref[...].astype(BF16), preferred_element_type=F32)


def _moe(n3, comb, h2d, w_gate_l, w_up_l, w_down_l, *, tm):
    n = h2d.shape[0]
    return pl.pallas_call(
        _moe_kernel,
        grid=(n // tm, N_EXPERTS),
        in_specs=[
            pl.BlockSpec((tm, D_MODEL), lambda i, e: (i, 0)),
            pl.BlockSpec((tm, LANES), lambda i, e: (i, 0)),
            pl.BlockSpec((tm, D_MODEL), lambda i, e: (i, 0)),
            pl.BlockSpec((None, D_MODEL, EXPERT_FF), lambda i, e: (e, 0, 0)),
            pl.BlockSpec((None, D_MODEL, EXPERT_FF), lambda i, e: (e, 0, 0)),
            pl.BlockSpec((None, EXPERT_FF, D_MODEL), lambda i, e: (e, 0, 0)),
        ],
        out_specs=pl.BlockSpec((tm, D_MODEL), lambda i, e: (i, 0)),
        out_shape=jax.ShapeDtypeStruct((n, D_MODEL), F32),
        compiler_params=_cparams(("arbitrary", "arbitrary")),
        name="moe",
    )(n3, comb, h2d, w_gate_l, w_up_l, w_down_l)


def _post_mixer(p, b_out, ws_eff, ab_eff, x2d, mk, mv, lw, *, tm_mix, tm_mem, mem_rows, tiles_per_mem,
                rows_per_mem, mem_len, tm_route, tm_moe):
    h1 = _mixout(p, b_out, ws_eff, ab_eff, x2d, lw["w_out"], tm=tm_mix)
    h2 = _memattn(h1, lw["norm_mem"], lw["mem_wq"], lw["mem_qnorm"], mk, mv, lw["mem_wo"], tm=tm_mem,
                  mem_rows=mem_rows, tiles_per_mem=tiles_per_mem, rows_per_mem=rows_per_mem, mem_len=mem_len)
    n3, comb = _router(h2, lw["norm_ffn"], lw["w_cat"], lw["b_cat"], tm=tm_route)
    moe = _moe_routed if tm_moe > N_GROUPS * MOE_ROWS else _moe
    return moe(n3, comb, h2, lw["w_gate"], lw["w_up"], lw["w_down"], tm=tm_moe)


def kernel(x_prompt, x_sample, cache_k, cache_v, cache_idx_k, cache_mem_k, cache_mem_v, page_table,
           mem_prompt, norm_mix, w_in, a_vnorm, a_ws, a_b, b_qnorm, b_knorm, rel_bias, w_out,
           norm_mem, mem_wq, mem_wk, mem_wv, mem_qnorm, mem_knorm, mem_wo, norm_ffn,
           w_group, b_group, w_router, b_router, w_gate, w_up, w_down):
    bp, t, d = x_prompt.shape
    bd, s, _ = x_sample.shape
    depth = w_in.shape[0]
    mem_len = mem_prompt.shape[1]
    assert d == D_MODEL and t % KEY_CHUNK == 0 and s <= 8 and (bd * s) % LANES == 0

    bias_pairs, sbias, sconst = _bias_tables(rel_bias)
    hp = x_prompt.reshape(bp * t, d)
    hs = x_sample.reshape(bd * s, d)
    outs = [[] for _ in range(9)]
    for l in range(depth):
        w_cat = jnp.zeros((d, 2 * LANES), F32)
        w_cat = w_cat.at[:, :N_GROUPS].set(w_group[l]).at[:, LANES:LANES + N_EXPERTS].set(w_router[l])
        b_cat = jnp.zeros((1, 2 * LANES), F32)
        b_cat = b_cat.at[0, :N_GROUPS].set(b_group[l]).at[0, LANES:LANES + N_EXPERTS].set(b_router[l])
        lw = dict(w_out=w_out[l], norm_mem=norm_mem[l], mem_wq=mem_wq[l], mem_qnorm=mem_qnorm[l],
                  mem_wo=mem_wo[l], norm_ffn=norm_ffn[l], w_cat=w_cat, b_cat=b_cat)

        p, k, v, tail = _proj(hp, norm_mix[l], w_in[l], a_vnorm[l], b_qnorm[l], b_knorm[l],
                              tm=1024, want_vn=False)
        later_weights = (w_gate[l], w_up[l], w_down[l], w_out[l][None])
        fuse_cast = (bp * (t // Q_BLOCK)) % N_EXPERTS == 0
        b_out, casted = _dsa_prompt(p, tail, bias_pairs, later_weights if fuse_cast else (), batch=bp, seq=t)
        if not fuse_cast:
            casted = tuple(_cast_bf16(w) for w in later_weights)
        lw.update(w_gate=casted[0], w_up=casted[1], w_down=casted[2], w_out=casted[3][0])
        mk, mv = _memkv(mem_prompt.reshape(bp * mem_len, d), mem_wk[l], mem_wv[l], mem_knorm[l])
        hp = _post_mixer(p, b_out, a_ws[l], a_b[l].T, hp, mk, mv, lw, tm_mix=1024, tm_mem=512,
                         mem_rows=mem_len, tiles_per_mem=t // 512, rows_per_mem=None, mem_len=mem_len,
                         tm_route=512, tm_moe=1024)
        outs[0].append(k.reshape(bp, t, B_KV_HEADS, B_HEAD_DIM))
        outs[1].append(v.reshape(bp, t, B_KV_HEADS, B_HEAD_DIM))
        outs[2].append(tail[:, :IDX_DIM].reshape(bp, t, IDX_DIM))
        outs[3].append(mk.reshape(bp, mem_len, MEM_HEADS, MEM_HEAD_DIM))
        outs[4].append(mv.reshape(bp, mem_len, MEM_HEADS, MEM_HEAD_DIM))

        ns = bd * s
        p_s, k_s, v_s, tail_s, vn_s = _proj(hs, norm_mix[l], w_in[l], a_vnorm[l], b_qnorm[l], b_knorm[l],
                                            tm=ns, want_vn=True)
        b_out_s = _dsa_sample(p_s, tail_s, k_s, v_s, cache_k[l], cache_v[l], cache_idx_k[l], page_table,
                              sbias, sconst, dec_batch=bd, dec_seq=s)
        ws_s = jnp.einsum("ab,gts->gatbs", jnp.eye(ns // s, dtype=F32), a_ws[l][:, :s, :s])
        ws_s = ws_s.reshape(A_GROUPS, ns, ns)
        ab_s = jnp.tile(a_b[l][:, :s].T, (ns // s, 1))
        mem_tile = 32
        hs = _post_mixer(p_s, b_out_s, ws_s, ab_s, hs,
                         cache_mem_k[l].reshape(bd * mem_len * MEM_HEADS, MEM_HEAD_DIM),
                         cache_mem_v[l].reshape(bd * mem_len * MEM_HEADS, MEM_HEAD_DIM), lw,
                         tm_mix=ns, tm_mem=mem_tile, mem_rows=mem_tile // s * mem_len, tiles_per_mem=1,
                         rows_per_mem=s, mem_len=mem_len, tm_route=ns, tm_moe=ns)
        outs[5].append(k_s.reshape(bd, s, B_KV_HEADS, B_HEAD_DIM))
        outs[6].append(v_s.reshape(bd, s, B_KV_HEADS, B_HEAD_DIM))
        outs[7].append(tail_s[:, :IDX_DIM].reshape(bd, s, IDX_DIM))
        outs[8].append(vn_s.reshape(bd, s, A_GROUPS, A_CH))
    return (hp.reshape(bp, t, d), hs.reshape(bd, s, d)) + tuple(jnp.stack(o) for o in outs)
```

```python
import functools
import math

import jax
import jax.numpy as jnp
from jax import lax
from jax.experimental import pallas as pl
from jax.experimental.pallas import tpu as pltpu

BF16 = jnp.bfloat16
F32 = jnp.float32
I32 = jnp.int32

D_MODEL = 2048
A_GROUPS = 8
A_CH = 128
A_WIDTH = A_GROUPS * A_CH
A_CHUNK = 128
B_HEADS = 8
B_HEAD_DIM = 128
B_KV_HEADS = 2
B_GQA = B_HEADS // B_KV_HEADS
B_WIDTH = B_HEADS * B_HEAD_DIM
KV_WIDTH = B_KV_HEADS * B_HEAD_DIM
IDX_HEADS = 16
IDX_DIM = 64
TOPK_MAX = 256
Q_BLOCK = 128
REL_BUCKETS = 32
REL_MAX_DIST = 128
MEM_HEADS = 4
MEM_HEAD_DIM = 128
MEM_WIDTH = MEM_HEADS * MEM_HEAD_DIM
N_GROUPS = 4
EXPERTS_PER_GROUP = 4
N_EXPERTS = N_GROUPS * EXPERTS_PER_GROUP
EXPERT_FF = 512
PAGE_SIZE = 128
EPS = 1e-6

COL_U = 0
COL_V = COL_U + A_WIDTH
COL_Q = COL_V + A_WIDTH
COL_K = COL_Q + B_WIDTH
COL_VV = COL_K + KV_WIDTH
COL_QI = COL_VV + KV_WIDTH
COL_MAIN = COL_QI + IDX_HEADS * IDX_DIM
TAIL_COLS = IDX_DIM + IDX_HEADS
LANES = 128
SUBLANES = 8
PROJ_TN = 512
PROJ_ROW_CHUNK = 256
KEY_CHUNK = 256
NEG_INIT = -1e30
VMEM_LIMIT = 56 * 1024 * 1024
MOE_VMEM_LIMIT = 60 * 1024 * 1024


def _cparams(sem):
    return pltpu.CompilerParams(dimension_semantics=sem, vmem_limit_bytes=VMEM_LIMIT)


def _gelu(x):
    return 0.5 * x * (1.0 + lax.erf(x * (2.0 ** -0.5)))


def _head_norm(a, gain, width=LANES):
    outs = []
    for c in range(a.shape[1] // width):
        blk = a[:, c * width:(c + 1) * width]
        ms = jnp.mean(blk * blk, axis=-1, keepdims=True)
        outs.append(blk * lax.rsqrt(ms + EPS) * gain[:, c * width:(c + 1) * width])
    return outs[0] if len(outs) == 1 else jnp.concatenate(outs, axis=-1)


def _row_norm(x, gain):
    ms = jnp.mean(x * x, axis=-1, keepdims=True)
    return x * lax.rsqrt(ms + EPS) * gain


def _dot_nt(a, b):
    return lax.dot_general(a, b, (((1,), (1,)), ((), ())), preferred_element_type=F32)


def _proj_kernel(x_ref, g_ref, w_ref, wt_ref, avn_ref, qn_ref, kn_ref, *rest, want_vn):
    if want_vn:
        p_ref, k_ref, v_ref, tail_ref, vn_ref, xn_scr = rest
    else:
        p_ref, k_ref, v_ref, tail_ref, xn_scr = rest
        vn_ref = None
    j = pl.program_id(1)
    ju = COL_V // PROJ_TN
    jv = COL_Q // PROJ_TN
    jq = COL_K // PROJ_TN
    jkv = COL_QI // PROJ_TN

    @pl.when(j == 0)
    def _():
        xn = _row_norm(x_ref[...], g_ref[...]).astype(BF16)
        xn_scr[...] = xn
        wlane = lax.broadcasted_iota(I32, wt_ref.shape, 1)
        wt = jnp.where(wlane < TAIL_COLS, wt_ref[...], 0.0).astype(BF16)
        t = jnp.dot(xn, wt, preferred_element_type=F32)
        lane = lax.broadcasted_iota(I32, t.shape, 1)
        tail_ref[...] = jnp.where(lane >= IDX_DIM, t * (IDX_HEADS ** -0.5), t)

    tm = xn_scr.shape[0]
    rc = min(tm, PROJ_ROW_CHUNK)

    def row_chunks(epilogue):
        w = w_ref[...].astype(BF16)
        for r in range(tm // rc):
            rs = slice(r * rc, (r + 1) * rc)
            epilogue(jnp.dot(xn_scr[rs, :], w, preferred_element_type=F32), rs)

    @pl.when(j < ju)
    def _():
        def epi(acc, rs):
            p_ref[rs, :] = _gelu(acc).astype(BF16)
        row_chunks(epi)

    @pl.when((j >= ju) & (j < jv))
    def _():
        def epi(acc, rs):
            vn = _head_norm(_gelu(acc), avn_ref[...])
            p_ref[rs, :] = vn.astype(BF16)
            if vn_ref is not None:
                vn_ref[rs, :] = vn
        row_chunks(epi)

    @pl.when((j >= jv) & (j < jq))
    def _():
        def epi(acc, rs):
            p_ref[rs, :] = _head_norm(acc, qn_ref[...]).astype(BF16)
        row_chunks(epi)

    @pl.when(j == jq)
    def _():
        def epi(acc, rs):
            k = _head_norm(acc[:, :KV_WIDTH], kn_ref[...])
            v = acc[:, KV_WIDTH:]
            for g in range(B_KV_HEADS):
                dst = pl.ds(rs.start * B_KV_HEADS + g, rs.stop - rs.start, stride=B_KV_HEADS)
                k_ref[dst, :] = k[:, g * B_HEAD_DIM:(g + 1) * B_HEAD_DIM]
                v_ref[dst, :] = v[:, g * B_HEAD_DIM:(g + 1) * B_HEAD_DIM]
            p_ref[rs, :] = jnp.concatenate([k, v], axis=-1).astype(BF16)
        row_chunks(epi)

    @pl.when(j >= jkv)
    def _():
        def epi(acc, rs):
            p_ref[rs, :] = acc.astype(BF16)
        row_chunks(epi)


def _proj(x2d, gain, w_in_l, a_vnorm_l, b_qnorm_l, b_knorm_l, *, tm, want_vn):
    n = x2d.shape[0]
    nj = COL_MAIN // PROJ_TN
    avn = a_vnorm_l.reshape(1, A_WIDTH)
    qn = jnp.tile(b_qnorm_l, PROJ_TN // B_HEAD_DIM).reshape(1, PROJ_TN)
    kn = jnp.tile(b_knorm_l, B_KV_HEADS).reshape(1, KV_WIDTH)
    ju = COL_V // PROJ_TN
    nv = A_WIDTH // PROJ_TN

    def vmap_(i, j):
        return (0, jnp.clip(j - ju, 0, nv - 1))

    in_specs = [
        pl.BlockSpec((tm, D_MODEL), lambda i, j: (i, 0)),
        pl.BlockSpec((1, D_MODEL), lambda i, j: (0, 0)),
        pl.BlockSpec((D_MODEL, PROJ_TN), lambda i, j: (0, j)),
        pl.BlockSpec((D_MODEL, LANES), lambda i, j: (0, COL_MAIN // LANES)),
        pl.BlockSpec((1, PROJ_TN), vmap_),
        pl.BlockSpec((1, PROJ_TN), lambda i, j: (0, 0)),
        pl.BlockSpec((1, KV_WIDTH), lambda i, j: (0, 0)),
    ]
    out_shape = [
        jax.ShapeDtypeStruct((n, COL_MAIN), BF16),
        jax.ShapeDtypeStruct((n * B_KV_HEADS, B_HEAD_DIM), F32),
        jax.ShapeDtypeStruct((n * B_KV_HEADS, B_HEAD_DIM), F32),
        jax.ShapeDtypeStruct((n, LANES), F32),
    ]
    out_specs = [
        pl.BlockSpec((tm, PROJ_TN), lambda i, j: (i, j)),
        pl.BlockSpec((tm * B_KV_HEADS, B_HEAD_DIM), lambda i, j: (i, 0)),
        pl.BlockSpec((tm * B_KV_HEADS, B_HEAD_DIM), lambda i, j: (i, 0)),
        pl.BlockSpec((tm, LANES), lambda i, j: (i, 0)),
    ]
    if want_vn:
        out_shape.append(jax.ShapeDtypeStruct((n, A_WIDTH), F32))
        out_specs.append(pl.BlockSpec((tm, PROJ_TN), lambda i, j: (i, jnp.clip(j - ju, 0, nv - 1))))
    return pl.pallas_call(
        functools.partial(_proj_kernel, want_vn=want_vn),
        grid=(n // tm, nj),
        in_specs=in_specs,
        out_specs=out_specs,
        out_shape=out_shape,
        scratch_shapes=[pltpu.VMEM((tm, D_MODEL), BF16)],
        compiler_params=_cparams(("arbitrary", "arbitrary")),
        name="proj",
    )(x2d, gain.reshape(1, D_MODEL), w_in_l, w_in_l, avn, qn, kn)


def _t5_bucket(d):
    max_exact = REL_BUCKETS // 2
    d = jnp.maximum(d, 0)
    ratio = jnp.log(jnp.maximum(d, 1).astype(F32) / max_exact) / math.log(REL_MAX_DIST / max_exact)
    large = jnp.minimum(max_exact + jnp.floor(ratio * (REL_BUCKETS - max_exact)).astype(I32), REL_BUCKETS - 1)
    return jnp.where(d < max_exact, d, large)


def _bias_lookup(bucket, rb_ref, h):
    acc = jnp.zeros(bucket.shape, F32)
    for r in range(REL_BUCKETS):
        acc = jnp.where(bucket == r, rb_ref[r, h], acc)
    return acc


def _bias_kernel(rb_ref, pair_ref, samp_ref, sconst_ref):
    q = lax.broadcasted_iota(I32, (Q_BLOCK, Q_BLOCK), 0)
    kc = lax.broadcasted_iota(I32, (Q_BLOCK, Q_BLOCK), 1)
    nvar = pair_ref.shape[0]
    rels = range(-(nvar - 1), 2)
    buckets = {r: _t5_bucket(q - kc - r * Q_BLOCK) for r in rels}
    for h in range(B_HEADS):
        g, hh = divmod(h, B_GQA)
        tiles = {r: _bias_lookup(buckets[r], rb_ref, h) for r in rels}
        for var in range(nvar):
            r0 = var - (nvar - 1)
            for t in range(2):
                pair_ref[var, g, hh * Q_BLOCK:(hh + 1) * Q_BLOCK, t * Q_BLOCK:(t + 1) * Q_BLOCK] = tiles[r0 + t]
    rows = B_GQA * SUBLANES
    row = lax.broadcasted_iota(I32, (rows, 2 * PAGE_SIZE), 0)
    col = lax.broadcasted_iota(I32, (rows, 2 * PAGE_SIZE), 1)
    t = row % SUBLANES
    d = jnp.where(col < PAGE_SIZE, t + PAGE_SIZE - col, t - (col - PAGE_SIZE))
    bucket = _t5_bucket(d)
    far = jnp.full((rows, LANES), REL_BUCKETS - 1, I32)
    rowc = lax.broadcasted_iota(I32, (rows, LANES), 0)
    for g in range(B_KV_HEADS):
        acc = jnp.zeros((rows, 2 * PAGE_SIZE), F32)
        accc = jnp.zeros((rows, LANES), F32)
        for hh in range(B_GQA):
            h = g * B_GQA + hh
            acc = jnp.where(row // SUBLANES == hh, _bias_lookup(bucket, rb_ref, h), acc)
            accc = jnp.where(rowc // SUBLANES == hh, _bias_lookup(far, rb_ref, h), accc)
        samp_ref[g] = acc
        sconst_ref[g] = accc


N_PAIR_VARIANTS = 4


def _bias_tables(rel_bias):
    rows = B_GQA * SUBLANES
    return pl.pallas_call(
        _bias_kernel,
        in_specs=[pl.BlockSpec(memory_space=pltpu.SMEM)],
        out_shape=[
            jax.ShapeDtypeStruct((N_PAIR_VARIANTS, B_KV_HEADS, B_GQA * Q_BLOCK, KEY_CHUNK), F32),
            jax.ShapeDtypeStruct((B_KV_HEADS, rows, 2 * PAGE_SIZE), F32),
            jax.ShapeDtypeStruct((B_KV_HEADS, rows, LANES), F32),
        ],
        name="bias_tables",
    )(rel_bias)


def _key_to_f32(key):
    bits = jnp.where(key < 0, key ^ I32(-2 ** 31), ~key)
    return lax.bitcast_convert_type(bits, F32)


def _kth_largest(count_ge, shape, k):
    def body(it, prefix):
        bit = 31 - it
        cand = prefix | lax.shift_left(I32(1), bit)
        cnt = count_ge(_key_to_f32(cand))
        return jnp.where(cnt >= k, cand, prefix)

    prefix = lax.fori_loop(0, 32, body, jnp.zeros(shape, I32))
    thr = _key_to_f32(prefix)
    return jnp.where((prefix & I32(-2 ** 23)) == 0, -jnp.inf, thr)


NO_INDEX_BOUND = 2 ** 30
TIE_REPAIR_ROUNDS = 2


def _topk_select(fold_sum, fold_min, shape, k, index_bits):
    thr = _kth_largest(lambda t: fold_sum(lambda sc, kp: sc >= t), shape, k)
    c_ge = fold_sum(lambda sc, kp: sc >= thr)
    no_bound = jnp.full(shape, NO_INDEX_BOUND, I32)

    def repair(_):
        v = thr
        strict = jnp.zeros(shape, F32)

        def in_set(sc, v, strict):
            return (sc > v) | ((sc == v) & (strict < 0.5))

        for _ in range(TIE_REPAIR_ROUNDS):
            c_set = fold_sum(lambda sc, kp: in_set(sc, v, strict))
            vmin = fold_min(lambda sc, kp: jnp.where(in_set(sc, v, strict), sc, jnp.inf))
            c_gt = fold_sum(lambda sc, kp: sc > vmin)
            drop = (c_set > k) & (c_gt >= k)
            v = jnp.where(drop, vmin, v)
            strict = jnp.where(drop, 1.0, strict)
        c_set = fold_sum(lambda sc, kp: in_set(sc, v, strict))
        vt = fold_min(lambda sc, kp: jnp.where(in_set(sc, v, strict), sc, jnp.inf))
        need = k - fold_sum(lambda sc, kp: sc > vt)

        def idx_body(it, m):
            cand = m | lax.shift_left(I32(1), index_bits - 1 - it)
            below = fold_sum(lambda sc, kp: (sc == vt) & (kp < cand))
            return jnp.where(below < need, cand, m)

        m = lax.fori_loop(0, index_bits, idx_body, jnp.zeros(shape, I32))
        tie = c_set > k
        v_out = jnp.where(tie, vt, v)
        m_out = jnp.where(tie, m, jnp.where(strict > 0.5, -1, no_bound))
        return v_out, m_out

    return lax.cond(jnp.max(c_ge) > k, repair, lambda _: (thr, no_bound), 0)


def _chunk_loop(n, body, init):
    carry = lax.fori_loop(0, n // 2, lambda j, c: body(2 * j + 1, body(2 * j, c)), init)
    return lax.cond(n % 2 == 1, lambda c: body(n - 1, c), lambda c: c, carry)


def _dsa_prompt_kernel(q_ref, kv_ref, qia_ref, qib_ref, tailk_ref, tailq_ref, bias_ref, *rest, topk, n_cast):
    cast_in, rest = rest[:n_cast], rest[n_cast:]
    o_ref, rest = rest[0], rest[1:]
    cast_out, rest = rest[:n_cast], rest[n_cast:]
    kil_scr, kir_scr, sc_scr, lg_scr, m_scr, l_scr, acc_scr = rest
    i = pl.program_id(1)

    for src, dst in zip(cast_in, cast_out):
        dst[...] = src[...].astype(BF16)

    @pl.when(i == 0)
    def _():
        ki = tailk_ref[:, :IDX_DIM].astype(BF16)
        z = jnp.zeros_like(ki)
        kil_scr[...] = jnp.concatenate([ki, z], axis=-1)
        kir_scr[...] = jnp.concatenate([z, ki], axis=-1)

    wi = tailq_ref[:, IDX_DIM:IDX_DIM + IDX_HEADS] * (IDX_DIM ** -0.5)
    nk = (i + 2) // 2
    qpos = i * Q_BLOCK + lax.broadcasted_iota(I32, (Q_BLOCK, 1), 0)
    kloc = lax.broadcasted_iota(I32, (1, KEY_CHUNK), 1)
    heads_per_ref = qia_ref.shape[1] // IDX_DIM

    pairs = []
    for hp in range(IDX_HEADS // 2):
        ref = qia_ref if 2 * hp < heads_per_ref else qib_ref
        base = (2 * hp) % heads_per_ref * IDX_DIM
        pairs.append(ref[:, base:base + 2 * IDX_DIM])
    qstack = jnp.concatenate(pairs, axis=0)

    def score_chunk(c, carry):
        off = pl.multiple_of(c * KEY_CHUNK, KEY_CHUNK)
        acc = jnp.zeros((Q_BLOCK, KEY_CHUNK), F32)
        for side, k_scr in ((0, kil_scr), (1, kir_scr)):
            s = _dot_nt(qstack, k_scr[pl.ds(off, KEY_CHUNK), :])
            for hp in range(IDX_HEADS // 2):
                h = 2 * hp + side
                acc = acc + wi[:, h:h + 1] * jnp.maximum(s[hp * Q_BLOCK:(hp + 1) * Q_BLOCK], 0.0)
        kpos = c * KEY_CHUNK + kloc
        sc_scr[c] = jnp.where(kpos <= qpos, acc, -jnp.inf)
        return carry

    _chunk_loop(nk, score_chunk, 0)

    def fold_sum(pred):
        def body(c, cnt):
            m = jnp.where(pred(sc_scr[c], c * KEY_CHUNK + kloc), 1.0, 0.0)
            return cnt + m[:, :LANES] + m[:, LANES:]
        cnt = _chunk_loop(nk, body, jnp.zeros((Q_BLOCK, LANES), F32))
        return jnp.sum(cnt, axis=-1, keepdims=True)

    def fold_min(val):
        def body(c, acc):
            x = val(sc_scr[c], c * KEY_CHUNK + kloc)
            return jnp.minimum(acc, jnp.minimum(x[:, :LANES], x[:, LANES:]))
        acc = lax.fori_loop(0, nk, body, jnp.full((Q_BLOCK, LANES), jnp.inf, F32))
        return jnp.min(acc, axis=-1, keepdims=True)

    index_bits = (kv_ref.shape[0] - 1).bit_length()
    sel_v, sel_m = _topk_select(fold_sum, fold_min, (Q_BLOCK, 1), topk, index_bits)

    def mask_chunk(c, carry):
        kpos = c * KEY_CHUNK + kloc
        sc = sc_scr[c]
        chosen = (sc > sel_v) | ((sc == sel_v) & (kpos <= sel_m))
        sc_scr[c] = jnp.where(chosen & (kpos <= qpos), 0.0, -jnp.inf)
        return carry

    lax.fori_loop(0, nk, mask_chunk, 0)

    scale = B_HEAD_DIM ** -0.5
    rows = B_GQA * Q_BLOCK
    qgs = [jnp.concatenate(
        [q_ref[:, (g * B_GQA + hh) * B_HEAD_DIM:(g * B_GQA + hh + 1) * B_HEAD_DIM] for hh in range(B_GQA)],
        axis=0) for g in range(B_KV_HEADS)]
    m_scr[...] = jnp.full(m_scr.shape, -jnp.inf, F32)
    l_scr[...] = jnp.zeros(l_scr.shape, F32)
    acc_scr[...] = jnp.zeros(acc_scr.shape, F32)

    def logits_chunk(c, carry):
        off = pl.multiple_of(c * KEY_CHUNK, KEY_CHUNK)
        var = jnp.clip(2 * c - i, -(N_PAIR_VARIANTS - 1), 0) + (N_PAIR_VARIANTS - 1)
        sel = sc_scr[c]
        sel = jnp.concatenate([sel] * B_GQA, axis=0)
        for g in range(B_KV_HEADS):
            kc = kv_ref[pl.ds(off, KEY_CHUNK), g * B_HEAD_DIM:(g + 1) * B_HEAD_DIM]
            lg = _dot_nt(qgs[g], kc) * scale + bias_ref[var, g] + sel
            lg_scr[c, g] = lg
            m_scr[g] = jnp.maximum(m_scr[g], jnp.maximum(lg[:, :LANES], lg[:, LANES:]))
        return carry

    _chunk_loop(nk, logits_chunk, 0)
    for g in range(B_KV_HEADS):
        m = jnp.max(m_scr[g], axis=-1, keepdims=True)
        m_scr[g] = jnp.broadcast_to(m, (rows, LANES))

    def pv_chunk(c, carry):
        off = pl.multiple_of(c * KEY_CHUNK, KEY_CHUNK)
        for g in range(B_KV_HEADS):
            vc = kv_ref[pl.ds(off, KEY_CHUNK), KV_WIDTH + g * B_HEAD_DIM:KV_WIDTH + (g + 1) * B_HEAD_DIM]
            mb = m_scr[g]
            p = jnp.exp(lg_scr[c, g] - jnp.concatenate([mb, mb], axis=-1))
            l_scr[g] += p[:, :LANES] + p[:, LANES:]
            acc_scr[g] += jnp.dot(p.astype(BF16), vc, preferred_element_type=F32)
        return carry

    _chunk_loop(nk, pv_chunk, 0)
    for g in range(B_KV_HEADS):
        o = acc_scr[g] / jnp.sum(l_scr[g], axis=-1, keepdims=True)
        for hh in range(B_GQA):
            h = g * B_GQA + hh
            o_ref[:, h * B_HEAD_DIM:(h + 1) * B_HEAD_DIM] = o[hh * Q_BLOCK:(hh + 1) * Q_BLOCK].astype(BF16)


def _dsa_prompt(p, tail, bias_pairs, cast_weights, *, batch, seq):
    topk = min(TOPK_MAX, seq // 4)
    nblk = seq // Q_BLOCK
    qi_w = PROJ_TN
    steps = batch * nblk
    cast_specs, cast_shapes = [], []
    for w in cast_weights:
        ne, r, c = w.shape
        parts = steps // ne
        assert steps == ne * parts and r % (parts * 16) == 0
        spec = pl.BlockSpec((None, r // parts, c),
                            lambda b, i, parts=parts: ((b * nblk + i) // parts, (b * nblk + i) % parts, 0))
        cast_specs.append(spec)
        cast_shapes.append(jax.ShapeDtypeStruct(w.shape, BF16))
    outs = pl.pallas_call(
        functools.partial(_dsa_prompt_kernel, topk=topk, n_cast=len(cast_weights)),
        grid=(batch, nblk),
        in_specs=[
            pl.BlockSpec((Q_BLOCK, B_WIDTH), lambda b, i: (b * nblk + i, COL_Q // B_WIDTH)),
            pl.BlockSpec((seq, 2 * KV_WIDTH), lambda b, i: (b, COL_K // (2 * KV_WIDTH))),
            pl.BlockSpec((Q_BLOCK, qi_w), lambda b, i: (b * nblk + i, COL_QI // qi_w)),
            pl.BlockSpec((Q_BLOCK, qi_w), lambda b, i: (b * nblk + i, COL_QI // qi_w + 1)),
            pl.BlockSpec((seq, LANES), lambda b, i: (b, 0)),
            pl.BlockSpec((Q_BLOCK, LANES), lambda b, i: (b * nblk + i, 0)),
            pl.BlockSpec(bias_pairs.shape, lambda b, i: (0, 0, 0, 0)),
        ] + cast_specs,
        out_specs=[pl.BlockSpec((Q_BLOCK, B_WIDTH), lambda b, i: (b * nblk + i, 0))] + cast_specs,
        out_shape=[jax.ShapeDtypeStruct((batch * seq, B_WIDTH), BF16)] + cast_shapes,
        scratch_shapes=[
            pltpu.VMEM((seq, 2 * IDX_DIM), BF16),
            pltpu.VMEM((seq, 2 * IDX_DIM), BF16),
            pltpu.VMEM((seq // KEY_CHUNK, Q_BLOCK, KEY_CHUNK), F32),
            pltpu.VMEM((seq // KEY_CHUNK, B_KV_HEADS, B_GQA * Q_BLOCK, KEY_CHUNK), F32),
            pltpu.VMEM((B_KV_HEADS, B_GQA * Q_BLOCK, LANES), F32),
            pltpu.VMEM((B_KV_HEADS, B_GQA * Q_BLOCK, LANES), F32),
            pltpu.VMEM((B_KV_HEADS, B_GQA * Q_BLOCK, B_HEAD_DIM), F32),
        ],
        compiler_params=_cparams(("arbitrary", "arbitrary")),
        name="dsa_prompt",
    )(p, p, p, p, tail, tail, bias_pairs, *cast_weights)
    return outs[0], tuple(outs[1:])


PAGES_PER_STEP = 32
SAMPLE_SUB_PAGES = 32
SCORE_KEYS_PER_DOT = 2048
FOLD_WAYS = 8


def _fetch_pages(pt_ref, sources, bufs, sem, step, n_steps_total, steps_per_sample, pages_per_step):
    slot = step % 2

    def copies(n, slot_, page_of):
        sample = n // steps_per_sample
        first = (n % steps_per_sample) * pages_per_step
        for r in range(pages_per_step):
            page = page_of(sample, first + r)
            for a, (src, buf) in enumerate(zip(sources, bufs)):
                yield pltpu.make_async_copy(src.at[page], buf.at[slot_, r], sem.at[a, slot_])

    def table(sample, idx):
        return pt_ref[sample, idx]

    @pl.when(step == 0)
    def _():
        for cp in copies(step, slot, table):
            cp.start()

    @pl.when(step + 1 < n_steps_total)
    def _():
        for cp in copies(step + 1, 1 - slot, table):
            cp.start()

    for cp in copies(step, slot, lambda sample, idx: 0):
        cp.wait()
    return slot


def _dsa_sample_score_kernel(pt_ref, idx_hbm, qi_ref, wi_ref, kin_ref, sc_ref, scn_ref, selv_ref, selm_ref,
                             page_buf, sem, *, topk, dec_seq, n_pages, n_samples):
    slot = _fetch_pages(pt_ref, [idx_hbm], [page_buf], sem, pl.program_id(0), n_samples, 1, n_pages)
    pages = [page_buf.at[slot, r] for r in range(n_pages)]
    past = n_pages * PAGE_SIZE
    pages_per_dot = SCORE_KEYS_PER_DOT // PAGE_SIZE
    qi = qi_ref[...]
    wi = wi_ref[...] * (IDX_DIM ** -0.5)

    def scores(keys_t):
        r = jnp.maximum(jnp.dot(qi, keys_t, preferred_element_type=F32), 0.0) * wi
        acc = r[0:SUBLANES]
        for j in range(1, IDX_HEADS * dec_seq // SUBLANES):
            acc = acc + r[j * SUBLANES:(j + 1) * SUBLANES]
        shift = dec_seq
        while shift < SUBLANES:
            acc = acc + pltpu.roll(acc, shift, axis=0)
            shift *= 2
        return acc

    for d in range(n_pages // pages_per_dot):
        keys_t = jnp.concatenate(
            [pg[...].astype(BF16) for pg in pages[d * pages_per_dot:(d + 1) * pages_per_dot]], axis=1)
        sc_ref[:, d * SCORE_KEYS_PER_DOT:(d + 1) * SCORE_KEYS_PER_DOT] = scores(keys_t)

    row = lax.broadcasted_iota(I32, (SUBLANES, LANES), 0)
    col = lax.broadcasted_iota(I32, (SUBLANES, LANES), 1)
    scn = jnp.where((col <= row % dec_seq) & (col < dec_seq), scores(kin_ref[...]), -jnp.inf)
    scn_ref[...] = scn

    def pieces():
        yield scn, past + col[0:1]
        for w in range(past // LANES):
            yield sc_ref[:, w * LANES:(w + 1) * LANES], w * LANES + col[0:1]

    def fold(term, combine):
        parts = [None] * FOLD_WAYS
        for n, (sc, kp) in enumerate(pieces()):
            x = term(sc, kp)
            parts[n % FOLD_WAYS] = x if parts[n % FOLD_WAYS] is None else combine(parts[n % FOLD_WAYS], x)
        parts = [p for p in parts if p is not None]
        while len(parts) > 1:
            parts = [combine(parts[j], parts[j + 1]) if j + 1 < len(parts) else parts[j]
                     for j in range(0, len(parts), 2)]
        return parts[0]

    def fold_sum(pred):
        cnt = fold(lambda sc, kp: jnp.where(pred(sc, kp), 1.0, 0.0), jnp.add)
        return jnp.sum(cnt, axis=-1, keepdims=True)

    def fold_min(val):
        return jnp.min(fold(val, jnp.minimum), axis=-1, keepdims=True)

    index_bits = (past + LANES - 1).bit_length()
    sel_v, sel_m = _topk_select(fold_sum, fold_min, (SUBLANES, 1), topk, index_bits)
    selv_ref[...] = jnp.broadcast_to(sel_v, (SUBLANES, LANES))
    selm_ref[...] = jnp.broadcast_to(sel_m, (SUBLANES, LANES))


def _dsa_sample_attend_kernel(pt_ref, ck_hbm, cv_hbm, q_ref, sc_ref, scn_ref, selv_ref, selm_ref, kvn_ref,
                              sbias_ref, sconst_ref, o_ref, kbuf, vbuf, sem, m_scr, l_scr, acc_scr,
                              *, dec_seq, n_steps, n_samples):
    s = pl.program_id(1)
    slot = _fetch_pages(pt_ref, [ck_hbm, cv_hbm], [kbuf, vbuf], sem, pl.program_id(0) * n_steps + s,
                        n_samples * n_steps, n_steps, PAGES_PER_STEP)
    kpages = [kbuf.at[slot, r] for r in range(PAGES_PER_STEP)]
    vpages = [vbuf.at[slot, r] for r in range(PAGES_PER_STEP)]
    chunk = PAGES_PER_STEP * PAGE_SIZE
    rows = B_GQA * SUBLANES
    scale = B_HEAD_DIM ** -0.5

    @pl.when(s == 0)
    def _():
        m_scr[...] = jnp.full(m_scr.shape, NEG_INIT, F32)
        l_scr[...] = jnp.zeros(l_scr.shape, F32)
        acc_scr[...] = jnp.zeros(acc_scr.shape, F32)

    sel_v = selv_ref[:, 0:1]
    sel_m = selm_ref[:, 0:1]

    def chosen(sc, kpos):
        return jnp.where((sc > sel_v) | ((sc == sel_v) & (kpos <= sel_m)), 0.0, -jnp.inf)

    def head_rows(pgs, g):
        return jnp.concatenate(
            [pg[pl.ds(g, PAGE_SIZE, stride=B_KV_HEADS), :].astype(BF16) for pg in pgs], axis=0)

    kcol = lax.broadcasted_iota(I32, (SUBLANES, chunk), 1)
    sel = chosen(sc_ref[...], s * chunk + kcol)
    sel = jnp.concatenate([sel] * B_GQA, axis=0)
    is_last = s == n_steps - 1
    ncol = lax.broadcasted_iota(I32, (SUBLANES, LANES), 1)
    seln = chosen(scn_ref[...], n_steps * chunk + ncol)
    seln = jnp.where(is_last & (scn_ref[...] > -jnp.inf), seln, -jnp.inf)
    seln = jnp.concatenate([seln] * B_GQA, axis=0)

    def partial(lg, v):
        m = jnp.maximum(jnp.max(lg, axis=-1, keepdims=True), NEG_INIT)
        p = jnp.exp(lg - m)
        return m, jnp.sum(p, axis=-1, keepdims=True), jnp.dot(p.astype(BF16), v, preferred_element_type=F32)

    sub = SAMPLE_SUB_PAGES * PAGE_SIZE
    nsub = PAGES_PER_STEP // SAMPLE_SUB_PAGES
    lane_sub = lax.broadcasted_iota(I32, (rows, sub), 1)
    for g in range(B_KV_HEADS):
        qg = q_ref[g]
        far = sconst_ref[g, :, 0:1]
        near = jnp.concatenate([jnp.zeros((rows, sub - PAGE_SIZE), F32), sbias_ref[g, :, :PAGE_SIZE]], axis=-1)
        parts = [(m_scr[g], l_scr[g], acc_scr[g])]
        for j in range(nsub):
            pj = slice(j * SAMPLE_SUB_PAGES, (j + 1) * SAMPLE_SUB_PAGES)
            lg = _dot_nt(qg, head_rows(kpages[pj], g)) * scale + sel[:, j * sub:(j + 1) * sub]
            if j == nsub - 1:
                lg = lg + jnp.where(is_last & (lane_sub >= sub - PAGE_SIZE), near, far)
            else:
                lg = lg + far
            parts.append(partial(lg, head_rows(vpages[pj], g)))
        kn = kvn_ref[:, g * B_HEAD_DIM:(g + 1) * B_HEAD_DIM]
        vn = kvn_ref[:, KV_WIDTH + g * B_HEAD_DIM:KV_WIDTH + (g + 1) * B_HEAD_DIM]
        parts.append(partial(_dot_nt(qg, kn) * scale + sbias_ref[g, :, PAGE_SIZE:] + seln, vn))
        m_new = parts[0][0]
        for m, _, _ in parts[1:]:
            m_new = jnp.maximum(m_new, m)
        l_new = jnp.zeros((rows, 1), F32)
        acc_new = jnp.zeros((rows, B_HEAD_DIM), F32)
        for m, l, acc in parts:
            w = jnp.exp(m - m_new)
            l_new = l_new + w * l
            acc_new = acc_new + w * acc
        m_scr[g] = m_new
        l_scr[g] = l_new
        acc_scr[g] = acc_new

    @pl.when(is_last)
    def _():
        for g in range(B_KV_HEADS):
            o_ref[g] = (acc_scr[g] / l_scr[g]).astype(BF16)


def _dsa_sample(p_s, tail_s, k_s, v_s, cache_k_l, cache_v_l, cache_idx_l, page_table, sbias, sconst,
                *, dec_batch, dec_seq):
    n_pages = page_table.shape[1]
    past = n_pages * PAGE_SIZE
    topk = min(TOPK_MAX, (past + dec_seq) // 4)
    n_steps = n_pages // PAGES_PER_STEP
    chunk = PAGES_PER_STEP * PAGE_SIZE
    rows = B_GQA * SUBLANES
    n_pool = cache_k_l.shape[0]

    assert SUBLANES % dec_seq == 0
    qi = p_s[:, COL_QI:COL_MAIN].reshape(dec_batch, dec_seq, IDX_HEADS, IDX_DIM).transpose(0, 2, 1, 3)
    qi = qi.reshape(dec_batch, IDX_HEADS * dec_seq, IDX_DIM)
    wi = tail_s[:, IDX_DIM:IDX_DIM + IDX_HEADS].reshape(dec_batch, dec_seq, IDX_HEADS).transpose(0, 2, 1)
    wi = wi.reshape(dec_batch, IDX_HEADS * dec_seq, 1)
    ki_new = tail_s[:, :IDX_DIM].astype(BF16).reshape(dec_batch, dec_seq, IDX_DIM)
    ki_new = jnp.pad(ki_new, ((0, 0), (0, LANES - dec_seq), (0, 0))).transpose(0, 2, 1)
    q = p_s[:, COL_Q:COL_K].reshape(dec_batch, dec_seq, B_KV_HEADS, B_GQA, B_HEAD_DIM)
    q = jnp.pad(q.transpose(0, 2, 3, 1, 4), ((0, 0), (0, 0), (0, 0), (0, SUBLANES - dec_seq), (0, 0)))
    q = q.reshape(dec_batch, B_KV_HEADS, rows, B_HEAD_DIM)
    kv_new = jnp.concatenate([k_s, v_s], axis=-1).astype(BF16).reshape(dec_batch, dec_seq, 2 * KV_WIDTH)
    kv_new = jnp.pad(kv_new, ((0, 0), (0, LANES - dec_seq), (0, 0)))

    cache_idx_t = jnp.swapaxes(cache_idx_l, 1, 2)
    stat_spec = pl.BlockSpec((None, SUBLANES, LANES), lambda b, pt: (b, 0, 0))
    sc, scn, selv, selm = pl.pallas_call(
        functools.partial(_dsa_sample_score_kernel, topk=topk, dec_seq=dec_seq, n_pages=n_pages,
                          n_samples=dec_batch),
        grid_spec=pltpu.PrefetchScalarGridSpec(
            num_scalar_prefetch=1,
            grid=(dec_batch,),
            in_specs=[
                pl.BlockSpec(memory_space=pl.ANY),
                pl.BlockSpec((None, IDX_HEADS * dec_seq, IDX_DIM), lambda b, pt: (b, 0, 0)),
                pl.BlockSpec((None, IDX_HEADS * dec_seq, 1), lambda b, pt: (b, 0, 0)),
                pl.BlockSpec((None, IDX_DIM, LANES), lambda b, pt: (b, 0, 0)),
            ],
            out_specs=[pl.BlockSpec((None, SUBLANES, past), lambda b, pt: (b, 0, 0)),
                       stat_spec, stat_spec, stat_spec],
            scratch_shapes=[
                pltpu.VMEM((2, n_pages, IDX_DIM, PAGE_SIZE), F32),
                pltpu.SemaphoreType.DMA((1, 2)),
            ],
        ),
        out_shape=[
            jax.ShapeDtypeStruct((dec_batch, SUBLANES, past), F32),
            jax.ShapeDtypeStruct((dec_batch, SUBLANES, LANES), F32),
            jax.ShapeDtypeStruct((dec_batch, SUBLANES, LANES), F32),
            jax.ShapeDtypeStruct((dec_batch, SUBLANES, LANES), I32),
        ],
        compiler_params=_cparams(("arbitrary",)),
        name="dsa_sample_score",
    )(page_table, cache_idx_t, qi, wi, ki_new)

    ck = cache_k_l.reshape(n_pool, PAGE_SIZE * B_KV_HEADS, B_HEAD_DIM)
    cv = cache_v_l.reshape(n_pool, PAGE_SIZE * B_KV_HEADS, B_HEAD_DIM)
    kv_buf = pltpu.VMEM((2, PAGES_PER_STEP, PAGE_SIZE * B_KV_HEADS, B_HEAD_DIM), F32)
    o = pl.pallas_call(
        functools.partial(_dsa_sample_attend_kernel, dec_seq=dec_seq, n_steps=n_steps, n_samples=dec_batch),
        grid_spec=pltpu.PrefetchScalarGridSpec(
            num_scalar_prefetch=1,
            grid=(dec_batch, n_steps),
            in_specs=[pl.BlockSpec(memory_space=pl.ANY), pl.BlockSpec(memory_space=pl.ANY)] + [
                pl.BlockSpec((None, B_KV_HEADS, rows, B_HEAD_DIM), lambda b, s, pt: (b, 0, 0, 0)),
                pl.BlockSpec((None, SUBLANES, chunk), lambda b, s, pt: (b, 0, s)),
                pl.BlockSpec((None, SUBLANES, LANES), lambda b, s, pt: (b, 0, 0)),
                pl.BlockSpec((None, SUBLANES, LANES), lambda b, s, pt: (b, 0, 0)),
                pl.BlockSpec((None, SUBLANES, LANES), lambda b, s, pt: (b, 0, 0)),
                pl.BlockSpec((None, LANES, 2 * KV_WIDTH), lambda b, s, pt: (b, 0, 0)),
                pl.BlockSpec(sbias.shape, lambda b, s, pt: (0, 0, 0)),
                pl.BlockSpec(sconst.shape, lambda b, s, pt: (0, 0, 0)),
            ],
            out_specs=pl.BlockSpec((None, B_KV_HEADS, rows, B_HEAD_DIM), lambda b, s, pt: (b, 0, 0, 0)),
            scratch_shapes=[
                kv_buf,
                kv_buf,
                pltpu.SemaphoreType.DMA((2, 2)),
                pltpu.VMEM((B_KV_HEADS, rows, 1), F32),
                pltpu.VMEM((B_KV_HEADS, rows, 1), F32),
                pltpu.VMEM((B_KV_HEADS, rows, B_HEAD_DIM), F32),
            ],
        ),
        out_shape=jax.ShapeDtypeStruct((dec_batch, B_KV_HEADS, rows, B_HEAD_DIM), BF16),
        compiler_params=_cparams(("arbitrary", "arbitrary")),
        name="dsa_sample_attend",
    )(page_table, ck, cv, q, sc, scn, selv, selm, kv_new, sbias, sconst)
    o = o.reshape(dec_batch, B_KV_HEADS, B_GQA, SUBLANES, B_HEAD_DIM)[:, :, :, :dec_seq].transpose(0, 3, 1, 2, 4)
    return o.reshape(dec_batch * dec_seq, B_WIDTH)


def _mixout_kernel(u_ref, vn_ref, b_ref, ws_ref, ab_ref, x_ref, w_ref, o_ref, mix_scr, *, tm):
    j = pl.program_id(1)

    @pl.when(j == 0)
    def _():
        row = lax.broadcasted_iota(I32, (A_CHUNK, A_CHUNK), 0)
        col = lax.broadcasted_iota(I32, (A_CHUNK, A_CHUNK), 1)
        for g in range(A_GROUPS):
            wsg = jnp.where(row >= col, ws_ref[g], 0.0).astype(BF16)
            bg = ab_ref[:, g:g + 1]
            cs = slice(g * A_CH, (g + 1) * A_CH)
            for r in range(tm // A_CHUNK):
                rs = slice(r * A_CHUNK, (r + 1) * A_CHUNK)
                sm = jnp.dot(wsg, vn_ref[rs, cs], preferred_element_type=F32) + bg
                mix_scr[rs, cs] = (u_ref[rs, cs].astype(F32) * sm).astype(BF16)
        mix_scr[:, A_WIDTH:] = b_ref[...]

    o_ref[...] = x_ref[...] + jnp.dot(mix_scr[...], w_ref[...].astype(BF16), preferred_element_type=F32)


def _mixout(p, b_out, ws_eff, ab_eff, x2d, w_out_l, *, tm, tn=1024):
    n = x2d.shape[0]
    return pl.pallas_call(
        functools.partial(_mixout_kernel, tm=tm),
        grid=(n // tm, D_MODEL // tn),
        in_specs=[
            pl.BlockSpec((tm, A_WIDTH), lambda i, j: (i, COL_U // A_WIDTH)),
            pl.BlockSpec((tm, A_WIDTH), lambda i, j: (i, COL_V // A_WIDTH)),
            pl.BlockSpec((tm, B_WIDTH), lambda i, j: (i, 0)),
            pl.BlockSpec((A_GROUPS, A_CHUNK, A_CHUNK), lambda i, j: (0, 0, 0)),
            pl.BlockSpec((A_CHUNK, A_GROUPS), lambda i, j: (0, 0)),
            pl.BlockSpec((tm, tn), lambda i, j: (i, j)),
            pl.BlockSpec((A_WIDTH + B_WIDTH, tn), lambda i, j: (0, j)),
        ],
        out_specs=pl.BlockSpec((tm, tn), lambda i, j: (i, j)),
        out_shape=jax.ShapeDtypeStruct((n, D_MODEL), F32),
        scratch_shapes=[pltpu.VMEM((tm, A_WIDTH + B_WIDTH), BF16)],
        compiler_params=_cparams(("arbitrary", "arbitrary")),
        name="mixout",
    )(p, p, b_out, ws_eff, ab_eff, x2d, w_out_l)


def _memkv_kernel(x_ref, wk_ref, wv_ref, kn_ref, mk_ref, mv_ref):
    x = x_ref[...].astype(BF16)
    tm = x.shape[0]
    mk = _head_norm(jnp.dot(x, wk_ref[...].astype(BF16), preferred_element_type=F32), kn_ref[...])
    mv = jnp.dot(x, wv_ref[...].astype(BF16), preferred_element_type=F32)
    for hd in range(MEM_HEADS):
        cs = slice(hd * MEM_HEAD_DIM, (hd + 1) * MEM_HEAD_DIM)
        mk_ref[pl.ds(hd, tm, stride=MEM_HEADS), :] = mk[:, cs]
        mv_ref[pl.ds(hd, tm, stride=MEM_HEADS), :] = mv[:, cs]


def _memkv(mem2d, wk, wv, knorm, *, tm=256):
    n = mem2d.shape[0]
    kn = jnp.tile(knorm, MEM_HEADS).reshape(1, MEM_WIDTH)
    return pl.pallas_call(
        _memkv_kernel,
        grid=(n // tm,),
        in_specs=[
            pl.BlockSpec((tm, D_MODEL), lambda i: (i, 0)),
            pl.BlockSpec((D_MODEL, MEM_WIDTH), lambda i: (0, 0)),
            pl.BlockSpec((D_MODEL, MEM_WIDTH), lambda i: (0, 0)),
            pl.BlockSpec((1, MEM_WIDTH), lambda i: (0, 0)),
        ],
        out_specs=[pl.BlockSpec((tm * MEM_HEADS, MEM_HEAD_DIM), lambda i: (i, 0))] * 2,
        out_shape=[jax.ShapeDtypeStruct((n * MEM_HEADS, MEM_HEAD_DIM), F32)] * 2,
        compiler_params=_cparams(("arbitrary",)),
        name="memkv",
    )(mem2d, wk, wv, kn)


def _memattn_kernel(h_ref, g_ref, wq_ref, qn_ref, mk_ref, mv_ref, wo_ref, o_ref, *, rows_per_mem, mem_len):
    h = h_ref[...]
    n = _row_norm(h, g_ref[...]).astype(BF16)
    q = _head_norm(jnp.dot(n, wq_ref[...].astype(BF16), preferred_element_type=F32), qn_ref[...]).astype(BF16)
    tm, m = h.shape[0], mk_ref.shape[0] // MEM_HEADS
    if rows_per_mem is not None:
        row = lax.broadcasted_iota(I32, (tm, m), 0)
        col = lax.broadcasted_iota(I32, (tm, m), 1)
        mask = jnp.where(row // rows_per_mem == col // mem_len, 0.0, -jnp.inf)
    outs = []
    for hd in range(MEM_HEADS):
        cs = slice(hd * MEM_HEAD_DIM, (hd + 1) * MEM_HEAD_DIM)
        mk = mk_ref[pl.ds(hd, m, stride=MEM_HEADS), :].astype(BF16)
        mv = mv_ref[pl.ds(hd, m, stride=MEM_HEADS), :].astype(BF16)
        lg = _dot_nt(q[:, cs], mk) * (MEM_HEAD_DIM ** -0.5)
        if rows_per_mem is not None:
            lg = lg + mask
        e = jnp.exp(lg - jnp.max(lg, axis=-1, keepdims=True))
        l = jnp.sum(e, axis=-1, keepdims=True)
        outs.append(jnp.dot(e.astype(BF16), mv, preferred_element_type=F32) / l)
    o = jnp.concatenate(outs, axis=-1).astype(BF16)
    o_ref[...] = h + jnp.dot(o, wo_ref[...].astype(BF16), preferred_element_type=F32)


def _memattn(h2d, gain, wq, qnorm, mk, mv, wo, *, tm, mem_rows, tiles_per_mem, rows_per_mem, mem_len):
    n = h2d.shape[0]
    qn = jnp.tile(qnorm, MEM_HEADS).reshape(1, MEM_WIDTH)
    return pl.pallas_call(
        functools.partial(_memattn_kernel, rows_per_mem=rows_per_mem, mem_len=mem_len),
        grid=(n // tm,),
        in_specs=[
            pl.BlockSpec((tm, D_MODEL), lambda i: (i, 0)),
            pl.BlockSpec((1, D_MODEL), lambda i: (0, 0)),
            pl.BlockSpec((D_MODEL, MEM_WIDTH), lambda i: (0, 0)),
            pl.BlockSpec((1, MEM_WIDTH), lambda i: (0, 0)),
            pl.BlockSpec((mem_rows * MEM_HEADS, MEM_HEAD_DIM), lambda i: (i // tiles_per_mem, 0)),
            pl.BlockSpec((mem_rows * MEM_HEADS, MEM_HEAD_DIM), lambda i: (i // tiles_per_mem, 0)),
            pl.BlockSpec((MEM_WIDTH, D_MODEL), lambda i: (0, 0)),
        ],
        out_specs=pl.BlockSpec((tm, D_MODEL), lambda i: (i, 0)),
        out_shape=jax.ShapeDtypeStruct((n, D_MODEL), F32),
        compiler_params=_cparams(("arbitrary",)),
        name="memattn",
    )(h2d, gain.reshape(1, D_MODEL), wq, qn, mk, mv, wo)


def _router_kernel(h_ref, g_ref, w_ref, b_ref, n_ref, c_ref):
    n = _row_norm(h_ref[...], g_ref[...]).astype(BF16)
    n_ref[...] = n
    lg = jnp.dot(n, w_ref[...].astype(BF16), preferred_element_type=F32) + b_ref[...]
    gl, el = lg[:, :LANES], lg[:, LANES:]
    lane = lax.broadcasted_iota(I32, gl.shape, 1)
    lanef = lane.astype(F32)

    def first_max_lane(v, vmax):
        return jnp.min(jnp.where(v == vmax, lanef, float(LANES)), axis=-1, keepdims=True).astype(I32)

    gl = jnp.where(lane < N_GROUPS, gl, -jnp.inf)
    ge = jnp.exp(gl - jnp.max(gl, axis=-1, keepdims=True))
    gp = ge / jnp.sum(ge, axis=-1, keepdims=True)
    g_gate = jnp.max(gp, axis=-1, keepdims=True)
    g_sel = first_max_lane(gp, g_gate)
    in_grp = (lane // EXPERTS_PER_GROUP == g_sel) & (lane < N_EXPERTS)
    el = jnp.where(in_grp, el, -jnp.inf)
    ee = jnp.exp(el - jnp.max(el, axis=-1, keepdims=True))
    ep = jnp.where(in_grp, ee / jnp.sum(ee, axis=-1, keepdims=True), -jnp.inf)
    w1 = jnp.max(ep, axis=-1, keepdims=True)
    i1 = first_max_lane(ep, w1)
    ep2 = jnp.where(lane == i1, -jnp.inf, ep)
    w2 = jnp.max(ep2, axis=-1, keepdims=True)
    i2 = first_max_lane(ep2, w2)
    tot = w1 + w2
    comb = jnp.where(lane == i1, w1 / tot * g_gate, jnp.where(lane == i2, w2 / tot * g_gate, 0.0))
    c_ref[...] = jnp.where(lane == GSEL_LANE, g_sel.astype(F32), comb)


def _router(h2d, gain, w_cat, b_cat, *, tm):
    n = h2d.shape[0]
    return pl.pallas_call(
        _router_kernel,
        grid=(n // tm,),
        in_specs=[
            pl.BlockSpec((tm, D_MODEL), lambda i: (i, 0)),
            pl.BlockSpec((1, D_MODEL), lambda i: (0, 0)),
            pl.BlockSpec((D_MODEL, 2 * LANES), lambda i: (0, 0)),
            pl.BlockSpec((1, 2 * LANES), lambda i: (0, 0)),
        ],
        out_specs=[pl.BlockSpec((tm, D_MODEL), lambda i: (i, 0)), pl.BlockSpec((tm, LANES), lambda i: (i, 0))],
        out_shape=[jax.ShapeDtypeStruct((n, D_MODEL), BF16), jax.ShapeDtypeStruct((n, LANES), F32)],
        compiler_params=_cparams(("arbitrary",)),
        name="router",
    )(h2d, gain.reshape(1, D_MODEL), w_cat, b_cat)


def _cast_kernel(x_ref, o_ref):
    o_ref[...] = x_ref[...].astype(BF16)


def _cast_bf16(w):
    ne, r, c = w.shape
    return pl.pallas_call(
        _cast_kernel,
        grid=(ne,),
        in_specs=[pl.BlockSpec((None, r, c), lambda e: (e, 0, 0))],
        out_specs=pl.BlockSpec((None, r, c), lambda e: (e, 0, 0)),
        out_shape=jax.ShapeDtypeStruct(w.shape, BF16),
        compiler_params=_cparams(("arbitrary",)),
        name="cast_bf16",
    )(w)


GSEL_LANE = N_EXPERTS
MOE_ROWS = 128
MOE_MAIN_SLACK = 0
MOE_FF_SPLIT = 1


def _moe_routed_kernel(n_ref, c_ref, h_hbm, wg_ref, wu_ref, wd_ref, o_ref,
                       tri_scr, key_scr, xc_scr, yc_scr, cw_scr, nch_ref, sem, *, tm):
    i, g, e, f = (pl.program_id(k) for k in range(4))
    main_rows = tm // N_GROUPS + MOE_MAIN_SLACK
    first_e = (e == 0) & (f == 0)
    last_e = (e == EXPERTS_PER_GROUP - 1) & (f == MOE_FF_SPLIT - 1)

    def onehot_rows(start, rows):
        want = (start + lax.broadcasted_iota(I32, (rows, 1), 0)).astype(F32)
        return jnp.where(key_scr[pl.ds(g, 1), :] == want, 1.0, 0.0).astype(BF16)

    def tail_start(c):
        return pl.multiple_of(main_rows + c * MOE_ROWS, math.gcd(main_rows, MOE_ROWS))

    @pl.when((i == 0) & (g == 0) & first_e)
    def _():
        r = lax.broadcasted_iota(I32, (tm, tm), 0)
        c = lax.broadcasted_iota(I32, (tm, tm), 1)
        tri_scr[...] = jnp.where(r < c, 1.0, 0.0).astype(BF16)

    def residual_copy():
        return pltpu.make_async_copy(h_hbm.at[pl.ds(pl.multiple_of(i * tm, tm), tm)], o_ref, sem)

    @pl.when((g == 0) & first_e)
    def _():
        residual_copy().start()
        lane = lax.broadcasted_iota(I32, (tm, LANES), 1).astype(F32)
        onehot = jnp.where(lane == c_ref[:, GSEL_LANE:GSEL_LANE + 1], 1.0, 0.0).astype(BF16)
        er = lax.broadcasted_iota(I32, (SUBLANES, LANES), 0)
        ec = lax.broadcasted_iota(I32, (SUBLANES, LANES), 1)
        eye = jnp.where(er == ec, 1.0, 0.0).astype(BF16)
        mask_t = _dot_nt(eye, onehot)
        rank_t = jnp.dot(mask_t.astype(BF16), tri_scr[...], preferred_element_type=F32)
        key_scr[...] = jnp.where(mask_t > 0.5, rank_t, -1.0)
        for gg in range(N_GROUPS):
            cnt = jnp.sum(mask_t[gg:gg + 1, :], axis=-1, keepdims=True)
            extra = jnp.ceil(jnp.maximum(cnt - main_rows, 0.0) * (1.0 / MOE_ROWS))
            nch_ref[gg] = jnp.sum(extra).astype(I32)

    ntail = nch_ref[g]

    @pl.when(first_e)
    def _():
        comb = c_ref[...]
        hi = comb.astype(BF16)
        r1 = comb - hi.astype(F32)
        mid = r1.astype(BF16)
        lo = (r1 - mid.astype(F32)).astype(BF16)

        def compact(start, rows):
            s = onehot_rows(start, rows)
            xc_scr[pl.ds(start, rows), :] = jnp.dot(s, n_ref[...], preferred_element_type=F32).astype(BF16)
            cw_scr[pl.ds(start, rows), :] = (jnp.dot(s, hi, preferred_element_type=F32)
                                             + jnp.dot(s, mid, preferred_element_type=F32)
                                             + jnp.dot(s, lo, preferred_element_type=F32))
            yc_scr[pl.ds(start, rows), :] = jnp.zeros((rows, D_MODEL), F32)

        compact(0, main_rows)
        lax.fori_loop(0, ntail, lambda c, carry: (compact(tail_start(c), MOE_ROWS), carry)[1], 0)

    wg = wg_ref[...].astype(BF16)
    wu = wu_ref[...].astype(BF16)
    wd = wd_ref[...].astype(BF16)
    ex = g * EXPERTS_PER_GROUP + e

    def expert(start, rows):
        x = xc_scr[pl.ds(start, rows), :]
        a = jnp.dot(x, wg, preferred_element_type=F32)
        u = jnp.dot(x, wu, preferred_element_type=F32)
        lane = lax.broadcasted_iota(I32, (rows, LANES), 1)
        cw = jnp.sum(jnp.where(lane == ex, cw_scr[pl.ds(start, rows), :], 0.0), axis=-1, keepdims=True)
        hid = (a * jax.nn.sigmoid(a) * u * cw).astype(BF16)
        yc_scr[pl.ds(start, rows), :] += jnp.dot(hid, wd, preferred_element_type=F32)

    expert(0, main_rows)
    lax.fori_loop(0, ntail, lambda c, carry: (expert(tail_start(c), MOE_ROWS), carry)[1], 0)

    @pl.when((g == 0) & last_e)
    def _():
        residual_copy().wait()

    @pl.when(last_e)
    def _():
        def scatter(start, rows):
            s = onehot_rows(start, rows)
            y = yc_scr[pl.ds(start, rows), :].astype(BF16)
            o_ref[...] += lax.dot_general(s, y, (((0,), (0,)), ((), ())), preferred_element_type=F32)

        scatter(0, main_rows)
        lax.fori_loop(0, ntail, lambda c, carry: (scatter(tail_start(c), MOE_ROWS), carry)[1], 0)


def _moe_routed(n3, comb, h2d, w_gate_l, w_up_l, w_down_l, *, tm):
    n = h2d.shape[0]
    ff = EXPERT_FF // MOE_FF_SPLIT

    def wmap(i, g, e, f):
        return (g * EXPERTS_PER_GROUP + e, 0, f)

    return pl.pallas_call(
        functools.partial(_moe_routed_kernel, tm=tm),
        grid=(n // tm, N_GROUPS, EXPERTS_PER_GROUP, MOE_FF_SPLIT),
        in_specs=[
            pl.BlockSpec((tm, D_MODEL), lambda i, g, e, f: (i, 0), pipeline_mode=pl.Buffered(1)),
            pl.BlockSpec((tm, LANES), lambda i, g, e, f: (i, 0), pipeline_mode=pl.Buffered(1)),
            pl.BlockSpec(memory_space=pl.ANY),
            pl.BlockSpec((None, D_MODEL, ff), wmap),
            pl.BlockSpec((None, D_MODEL, ff), wmap),
            pl.BlockSpec((None, ff, D_MODEL), lambda i, g, e, f: (g * EXPERTS_PER_GROUP + e, f, 0)),
        ],
        out_specs=pl.BlockSpec((tm, D_MODEL), lambda i, g, e, f: (i, 0)),
        out_shape=jax.ShapeDtypeStruct((n, D_MODEL), F32),
        scratch_shapes=[
            pltpu.VMEM((tm, tm), BF16),
            pltpu.VMEM((SUBLANES, tm), F32),
            pltpu.VMEM((tm + MOE_ROWS, D_MODEL), BF16),
            pltpu.VMEM((tm + MOE_ROWS, D_MODEL), F32),
            pltpu.VMEM((tm + MOE_ROWS, LANES), F32),
            pltpu.SMEM((N_GROUPS,), I32),
            pltpu.SemaphoreType.DMA(()),
        ],
        compiler_params=pltpu.CompilerParams(
            dimension_semantics=("arbitrary",) * 4, vmem_limit_bytes=MOE_VMEM_LIMIT),
        name="moe_routed",
    )(n3, comb, h2d, w_gate_l, w_up_l, w_down_l)


def _moe_kernel(n_ref, c_ref, h_ref, wg_ref, wu_ref, wd_ref, o_ref):
    e = pl.program_id(1)

    @pl.when(e == 0)
    def _():
        o_ref[...] = h_ref[...]

    x = n_ref[...]
    a = jnp.dot(x, wg_ref[...].astype(BF16), preferred_element_type=F32)
    u = jnp.dot(x, wu_ref[...].astype(BF16), preferred_element_type=F32)
    lane = lax.broadcasted_iota(I32, c_ref.shape, 1)
    c = jnp.sum(jnp.where(lane == e, c_ref[...], 0.0), axis=-1, keepdims=True)
    hid = (a * jax.nn.sigmoid(a) * u * c).astype(BF16)
    o_ref[...] += jnp.dot(hid, wd_ref[...].astype(BF16), preferred_element_type=F32)


def _moe(n3, comb, h2d, w_gate_l, w_up_l, w_down_l, *, tm):
    n = h2d.shape[0]
    return pl.pallas_call(
        _moe_kernel,
        grid=(n // tm, N_EXPERTS),
        in_specs=[
            pl.BlockSpec((tm, D_MODEL), lambda i, e: (i, 0)),
            pl.BlockSpec((tm, LANES), lambda i, e: (i, 0)),
            pl.BlockSpec((tm, D_MODEL), lambda i, e: (i, 0)),
            pl.BlockSpec((None, D_MODEL, EXPERT_FF), lambda i, e: (e, 0, 0)),
            pl.BlockSpec((None, D_MODEL, EXPERT_FF), lambda i, e: (e, 0, 0)),
            pl.BlockSpec((None, EXPERT_FF, D_MODEL), lambda i, e: (e, 0, 0)),
        ],
        out_specs=pl.BlockSpec((tm, D_MODEL), lambda i, e: (i, 0)),
        out_shape=jax.ShapeDtypeStruct((n, D_MODEL), F32),
        compiler_params=_cparams(("arbitrary", "arbitrary")),
        name="moe",
    )(n3, comb, h2d, w_gate_l, w_up_l, w_down_l)


def _post_mixer(p, b_out, ws_eff, ab_eff, x2d, mk, mv, lw, *, tm_mix, tm_mem, mem_rows, tiles_per_mem,
                rows_per_mem, mem_len, tm_route, tm_moe):
    h1 = _mixout(p, b_out, ws_eff, ab_eff, x2d, lw["w_out"], tm=tm_mix)
    h2 = _memattn(h1, lw["norm_mem"], lw["mem_wq"], lw["mem_qnorm"], mk, mv, lw["mem_wo"], tm=tm_mem,
                  mem_rows=mem_rows, tiles_per_mem=tiles_per_mem, rows_per_mem=rows_per_mem, mem_len=mem_len)
    n3, comb = _router(h2, lw["norm_ffn"], lw["w_cat"], lw["b_cat"], tm=tm_route)
    moe = _moe_routed if tm_moe > N_GROUPS * MOE_ROWS else _moe
    return moe(n3, comb, h2, lw["w_gate"], lw["w_up"], lw["w_down"], tm=tm_moe)


def kernel(x_prompt, x_sample, cache_k, cache_v, cache_idx_k, cache_mem_k, cache_mem_v, page_table,
           mem_prompt, norm_mix, w_in, a_vnorm, a_ws, a_b, b_qnorm, b_knorm, rel_bias, w_out,
           norm_mem, mem_wq, mem_wk, mem_wv, mem_qnorm, mem_knorm, mem_wo, norm_ffn,
           w_group, b_group, w_router, b_router, w_gate, w_up, w_down):
    bp, t, d = x_prompt.shape
    bd, s, _ = x_sample.shape
    depth = w_in.shape[0]
    mem_len = mem_prompt.shape[1]
    assert d == D_MODEL and t % KEY_CHUNK == 0 and s <= 8 and (bd * s) % LANES == 0

    bias_pairs, sbias, sconst = _bias_tables(rel_bias)
    hp = x_prompt.reshape(bp * t, d)
    hs = x_sample.reshape(bd * s, d)
    outs = [[] for _ in range(9)]
    for l in range(depth):
        w_cat = jnp.zeros((d, 2 * LANES), F32)
        w_cat = w_cat.at[:, :N_GROUPS].set(w_group[l]).at[:, LANES:LANES + N_EXPERTS].set(w_router[l])
        b_cat = jnp.zeros((1, 2 * LANES), F32)
        b_cat = b_cat.at[0, :N_GROUPS].set(b_group[l]).at[0, LANES:LANES + N_EXPERTS].set(b_router[l])
        lw = dict(w_out=w_out[l], norm_mem=norm_mem[l], mem_wq=mem_wq[l], mem_qnorm=mem_qnorm[l],
                  mem_wo=mem_wo[l], norm_ffn=norm_ffn[l], w_cat=w_cat, b_cat=b_cat)

        p, k, v, tail = _proj(hp, norm_mix[l], w_in[l], a_vnorm[l], b_qnorm[l], b_knorm[l],
                              tm=1024, want_vn=False)
        later_weights = (w_gate[l], w_up[l], w_down[l], w_out[l][None])
        fuse_cast = (bp * (t // Q_BLOCK)) % N_EXPERTS == 0
        b_out, casted = _dsa_prompt(p, tail, bias_pairs, later_weights if fuse_cast else (), batch=bp, seq=t)
        if not fuse_cast:
            casted = tuple(_cast_bf16(w) for w in later_weights)
        lw.update(w_gate=casted[0], w_up=casted[1], w_down=casted[2], w_out=casted[3][0])
        mk, mv = _memkv(mem_prompt.reshape(bp * mem_len, d), mem_wk[l], mem_wv[l], mem_knorm[l])
        hp = _post_mixer(p, b_out, a_ws[l], a_b[l].T, hp, mk, mv, lw, tm_mix=1024, tm_mem=512,
                         mem_rows=mem_len, tiles_per_mem=t // 512, rows_per_mem=None, mem_len=mem_len,
                         tm_route=512, tm_moe=1024)
        outs[0].append(k.reshape(bp, t, B_KV_HEADS, B_HEAD_DIM))
        outs[1].append(v.reshape(bp, t, B_KV_HEADS, B_HEAD_DIM))
        outs[2].append(tail[:, :IDX_DIM].reshape(bp, t, IDX_DIM))
        outs[3].append(mk.reshape(bp, mem_len, MEM_HEADS, MEM_HEAD_DIM))
        outs[4].append(mv.reshape(bp, mem_len, MEM_HEADS, MEM_HEAD_DIM))

        ns = bd * s
        p_s, k_s, v_s, tail_s, vn_s = _proj(hs, norm_mix[l], w_in[l], a_vnorm[l], b_qnorm[l], b_knorm[l],
                                            tm=ns, want_vn=True)
        b_out_s = _dsa_sample(p_s, tail_s, k_s.reshape(ns, KV_WIDTH), v_s.reshape(ns, KV_WIDTH),
                              cache_k[l], cache_v[l], cache_idx_k[l], page_table,
                              sbias, sconst, dec_batch=bd, dec_seq=s)
        ws_s = jnp.einsum("ab,gts->gatbs", jnp.eye(ns // s, dtype=F32), a_ws[l][:, :s, :s])
        ws_s = ws_s.reshape(A_GROUPS, ns, ns)
        ab_s = jnp.tile(a_b[l][:, :s].T, (ns // s, 1))
        mem_tile = 32
        hs = _post_mixer(p_s, b_out_s, ws_s, ab_s, hs,
                         cache_mem_k[l].reshape(bd * mem_len * MEM_HEADS, MEM_HEAD_DIM),
                         cache_mem_v[l].reshape(bd * mem_len * MEM_HEADS, MEM_HEAD_DIM), lw,
                         tm_mix=ns, tm_mem=mem_tile, mem_rows=mem_tile // s * mem_len, tiles_per_mem=1,
                         rows_per_mem=s, mem_len=mem_len, tm_route=ns, tm_moe=ns)
        outs[5].append(k_s.reshape(bd, s, B_KV_HEADS, B_HEAD_DIM))
        outs[6].append(v_s.reshape(bd, s, B_KV_HEADS, B_HEAD_DIM))
        outs[7].append(tail_s[:, :IDX_DIM].reshape(bd, s, IDX_DIM))
        outs[8].append(vn_s.reshape(bd, s, A_GROUPS, A_CH))
    return (hp.reshape(bp, t, d), hs.reshape(bd, s, d)) + tuple(jnp.stack(o) for o in outs)
```

```python
import functools
import math

import jax
import jax.numpy as jnp
from jax import lax
from jax.experimental import pallas as pl
from jax.experimental.pallas import tpu as pltpu

BF16 = jnp.bfloat16
F32 = jnp.float32
I32 = jnp.int32

D_MODEL = 2048
A_GROUPS = 8
A_CH = 128
A_WIDTH = A_GROUPS * A_CH
A_CHUNK = 128
B_HEADS = 8
B_HEAD_DIM = 128
B_KV_HEADS = 2
B_GQA = B_HEADS // B_KV_HEADS
B_WIDTH = B_HEADS * B_HEAD_DIM
KV_WIDTH = B_KV_HEADS * B_HEAD_DIM
IDX_HEADS = 16
IDX_DIM = 64
TOPK_MAX = 256
Q_BLOCK = 128
REL_BUCKETS = 32
REL_MAX_DIST = 128
MEM_HEADS = 4
MEM_HEAD_DIM = 128
MEM_WIDTH = MEM_HEADS * MEM_HEAD_DIM
N_GROUPS = 4
EXPERTS_PER_GROUP = 4
N_EXPERTS = N_GROUPS * EXPERTS_PER_GROUP
EXPERT_FF = 512
PAGE_SIZE = 128
EPS = 1e-6

COL_U = 0
COL_V = COL_U + A_WIDTH
COL_Q = COL_V + A_WIDTH
COL_K = COL_Q + B_WIDTH
COL_VV = COL_K + KV_WIDTH
COL_QI = COL_VV + KV_WIDTH
COL_MAIN = COL_QI + IDX_HEADS * IDX_DIM
TAIL_COLS = IDX_DIM + IDX_HEADS
LANES = 128
SUBLANES = 8
PROJ_TN = 512
PROJ_ROW_CHUNK = 256
KEY_CHUNK = 256
NEG_INIT = -1e30
VMEM_LIMIT = 56 * 1024 * 1024
MOE_VMEM_LIMIT = 60 * 1024 * 1024


def _cparams(sem):
    return pltpu.CompilerParams(dimension_semantics=sem, vmem_limit_bytes=VMEM_LIMIT)


def _gelu(x):
    return 0.5 * x * (1.0 + lax.erf(x * (2.0 ** -0.5)))


def _head_norm(a, gain, width=LANES):
    outs = []
    for c in range(a.shape[1] // width):
        blk = a[:, c * width:(c + 1) * width]
        ms = jnp.mean(blk * blk, axis=-1, keepdims=True)
        outs.append(blk * lax.rsqrt(ms + EPS) * gain[:, c * width:(c + 1) * width])
    return outs[0] if len(outs) == 1 else jnp.concatenate(outs, axis=-1)


def _row_norm(x, gain):
    ms = jnp.mean(x * x, axis=-1, keepdims=True)
    return x * lax.rsqrt(ms + EPS) * gain


def _dot_nt(a, b):
    return lax.dot_general(a, b, (((1,), (1,)), ((), ())), preferred_element_type=F32)


def _proj_kernel(x_ref, g_ref, w_ref, wt_ref, avn_ref, qn_ref, kn_ref, *rest, want_vn):
    if want_vn:
        p_ref, k_ref, v_ref, tail_ref, vn_ref, xn_scr = rest
    else:
        p_ref, k_ref, v_ref, tail_ref, xn_scr = rest
        vn_ref = None
    j = pl.program_id(1)
    ju = COL_V // PROJ_TN
    jv = COL_Q // PROJ_TN
    jq = COL_K // PROJ_TN
    jkv = COL_QI // PROJ_TN

    @pl.when(j == 0)
    def _():
        xn = _row_norm(x_ref[...], g_ref[...]).astype(BF16)
        xn_scr[...] = xn
        wrow = lax.broadcasted_iota(I32, wt_ref.shape, 0)
        wt = jnp.where(wrow < TAIL_COLS, wt_ref[...], 0.0).astype(BF16)
        t = _dot_nt(xn, wt)
        lane = lax.broadcasted_iota(I32, t.shape, 1)
        tail_ref[...] = jnp.where(lane >= IDX_DIM, t * (IDX_HEADS ** -0.5), t)

    tm = xn_scr.shape[0]
    rc = min(tm, PROJ_ROW_CHUNK)

    def row_chunks(epilogue):
        w = w_ref[...].astype(BF16)
        for r in range(tm // rc):
            rs = slice(r * rc, (r + 1) * rc)
            epilogue(_dot_nt(xn_scr[rs, :], w), rs)

    @pl.when(j < ju)
    def _():
        def epi(acc, rs):
            p_ref[rs, :] = _gelu(acc).astype(BF16)
        row_chunks(epi)

    @pl.when((j >= ju) & (j < jv))
    def _():
        def epi(acc, rs):
            vn = _head_norm(_gelu(acc), avn_ref[...])
            p_ref[rs, :] = vn.astype(BF16)
            if vn_ref is not None:
                vn_ref[rs, :] = vn
        row_chunks(epi)

    @pl.when((j >= jv) & (j < jq))
    def _():
        def epi(acc, rs):
            p_ref[rs, :] = _head_norm(acc, qn_ref[...]).astype(BF16)
        row_chunks(epi)

    @pl.when(j == jq)
    def _():
        def epi(acc, rs):
            k = _head_norm(acc[:, :KV_WIDTH], kn_ref[...])
            v = acc[:, KV_WIDTH:]
            for g in range(B_KV_HEADS):
                dst = pl.ds(rs.start * B_KV_HEADS + g, rs.stop - rs.start, stride=B_KV_HEADS)
                k_ref[dst, :] = k[:, g * B_HEAD_DIM:(g + 1) * B_HEAD_DIM]
                v_ref[dst, :] = v[:, g * B_HEAD_DIM:(g + 1) * B_HEAD_DIM]
            p_ref[rs, :] = jnp.concatenate([k, v], axis=-1).astype(BF16)
        row_chunks(epi)

    @pl.when(j >= jkv)
    def _():
        def epi(acc, rs):
            p_ref[rs, :] = acc.astype(BF16)
        row_chunks(epi)


def _proj(x2d, gain, w_in_l, a_vnorm_l, b_qnorm_l, b_knorm_l, *, tm, want_vn):
    n = x2d.shape[0]
    nj = COL_MAIN // PROJ_TN
    w_in_t = jnp.swapaxes(w_in_l, 0, 1)
    avn = a_vnorm_l.reshape(1, A_WIDTH)
    qn = jnp.tile(b_qnorm_l, PROJ_TN // B_HEAD_DIM).reshape(1, PROJ_TN)
    kn = jnp.tile(b_knorm_l, B_KV_HEADS).reshape(1, KV_WIDTH)
    ju = COL_V // PROJ_TN
    nv = A_WIDTH // PROJ_TN

    def vmap_(i, j):
        return (0, jnp.clip(j - ju, 0, nv - 1))

    in_specs = [
        pl.BlockSpec((tm, D_MODEL), lambda i, j: (i, 0)),
        pl.BlockSpec((1, D_MODEL), lambda i, j: (0, 0)),
        pl.BlockSpec((PROJ_TN, D_MODEL), lambda i, j: (j, 0)),
        pl.BlockSpec((LANES, D_MODEL), lambda i, j: (COL_MAIN // LANES, 0)),
        pl.BlockSpec((1, PROJ_TN), vmap_),
        pl.BlockSpec((1, PROJ_TN), lambda i, j: (0, 0)),
        pl.BlockSpec((1, KV_WIDTH), lambda i, j: (0, 0)),
    ]
    out_shape = [
        jax.ShapeDtypeStruct((n, COL_MAIN), BF16),
        jax.ShapeDtypeStruct((n * B_KV_HEADS, B_HEAD_DIM), F32),
        jax.ShapeDtypeStruct((n * B_KV_HEADS, B_HEAD_DIM), F32),
        jax.ShapeDtypeStruct((n, LANES), F32),
    ]
    out_specs = [
        pl.BlockSpec((tm, PROJ_TN), lambda i, j: (i, j)),
        pl.BlockSpec((tm * B_KV_HEADS, B_HEAD_DIM), lambda i, j: (i, 0)),
        pl.BlockSpec((tm * B_KV_HEADS, B_HEAD_DIM), lambda i, j: (i, 0)),
        pl.BlockSpec((tm, LANES), lambda i, j: (i, 0)),
    ]
    if want_vn:
        out_shape.append(jax.ShapeDtypeStruct((n, A_WIDTH), F32))
        out_specs.append(pl.BlockSpec((tm, PROJ_TN), lambda i, j: (i, jnp.clip(j - ju, 0, nv - 1))))
    return pl.pallas_call(
        functools.partial(_proj_kernel, want_vn=want_vn),
        grid=(n // tm, nj),
        in_specs=in_specs,
        out_specs=out_specs,
        out_shape=out_shape,
        scratch_shapes=[pltpu.VMEM((tm, D_MODEL), BF16)],
        compiler_params=_cparams(("arbitrary", "arbitrary")),
        name="proj",
    )(x2d, gain.reshape(1, D_MODEL), w_in_t, w_in_t, avn, qn, kn)


def _t5_bucket(d):
    max_exact = REL_BUCKETS // 2
    d = jnp.maximum(d, 0)
    ratio = jnp.log(jnp.maximum(d, 1).astype(F32) / max_exact) / math.log(REL_MAX_DIST / max_exact)
    large = jnp.minimum(max_exact + jnp.floor(ratio * (REL_BUCKETS - max_exact)).astype(I32), REL_BUCKETS - 1)
    return jnp.where(d < max_exact, d, large)


def _bias_lookup(bucket, rb_ref, h):
    acc = jnp.zeros(bucket.shape, F32)
    for r in range(REL_BUCKETS):
        acc = jnp.where(bucket == r, rb_ref[r, h], acc)
    return acc


def _bias_kernel(rb_ref, pair_ref, samp_ref, sconst_ref):
    q = lax.broadcasted_iota(I32, (Q_BLOCK, Q_BLOCK), 0)
    kc = lax.broadcasted_iota(I32, (Q_BLOCK, Q_BLOCK), 1)
    nvar = pair_ref.shape[0]
    rels = range(-(nvar - 1), 2)
    buckets = {r: _t5_bucket(q - kc - r * Q_BLOCK) for r in rels}
    for h in range(B_HEADS):
        g, hh = divmod(h, B_GQA)
        tiles = {r: _bias_lookup(buckets[r], rb_ref, h) for r in rels}
        for var in range(nvar):
            r0 = var - (nvar - 1)
            for t in range(2):
                pair_ref[var, g, hh * Q_BLOCK:(hh + 1) * Q_BLOCK, t * Q_BLOCK:(t + 1) * Q_BLOCK] = tiles[r0 + t]
    rows = B_GQA * SUBLANES
    row = lax.broadcasted_iota(I32, (rows, 2 * PAGE_SIZE), 0)
    col = lax.broadcasted_iota(I32, (rows, 2 * PAGE_SIZE), 1)
    t = row % SUBLANES
    d = jnp.where(col < PAGE_SIZE, t + PAGE_SIZE - col, t - (col - PAGE_SIZE))
    bucket = _t5_bucket(d)
    far = jnp.full((rows, LANES), REL_BUCKETS - 1, I32)
    rowc = lax.broadcasted_iota(I32, (rows, LANES), 0)
    for g in range(B_KV_HEADS):
        acc = jnp.zeros((rows, 2 * PAGE_SIZE), F32)
        accc = jnp.zeros((rows, LANES), F32)
        for hh in range(B_GQA):
            h = g * B_GQA + hh
            acc = jnp.where(row // SUBLANES == hh, _bias_lookup(bucket, rb_ref, h), acc)
            accc = jnp.where(rowc // SUBLANES == hh, _bias_lookup(far, rb_ref, h), accc)
        samp_ref[g] = acc
        sconst_ref[g] = accc


N_PAIR_VARIANTS = 4


def _bias_tables(rel_bias):
    rows = B_GQA * SUBLANES
    return pl.pallas_call(
        _bias_kernel,
        in_specs=[pl.BlockSpec(memory_space=pltpu.SMEM)],
        out_shape=[
            jax.ShapeDtypeStruct((N_PAIR_VARIANTS, B_KV_HEADS, B_GQA * Q_BLOCK, KEY_CHUNK), F32),
            jax.ShapeDtypeStruct((B_KV_HEADS, rows, 2 * PAGE_SIZE), F32),
            jax.ShapeDtypeStruct((B_KV_HEADS, rows, LANES), F32),
        ],
        name="bias_tables",
    )(rel_bias)


def _key_to_f32(key):
    bits = jnp.where(key < 0, key ^ I32(-2 ** 31), ~key)
    return lax.bitcast_convert_type(bits, F32)


def _kth_largest(count_ge, shape, k):
    def body(it, prefix):
        bit = 31 - it
        cand = prefix | lax.shift_left(I32(1), bit)
        cnt = count_ge(_key_to_f32(cand))
        return jnp.where(cnt >= k, cand, prefix)

    prefix = lax.fori_loop(0, 32, body, jnp.zeros(shape, I32))
    thr = _key_to_f32(prefix)
    return jnp.where((prefix & I32(-2 ** 23)) == 0, -jnp.inf, thr)


NO_INDEX_BOUND = 2 ** 30
TIE_REPAIR_ROUNDS = 2


def _topk_select(fold_sum, fold_min, shape, k, index_bits):
    thr = _kth_largest(lambda t: fold_sum(lambda sc, kp: sc >= t), shape, k)
    c_ge = fold_sum(lambda sc, kp: sc >= thr)
    no_bound = jnp.full(shape, NO_INDEX_BOUND, I32)

    def repair(_):
        v = thr
        strict = jnp.zeros(shape, F32)

        def in_set(sc, v, strict):
            return (sc > v) | ((sc == v) & (strict < 0.5))

        for _ in range(TIE_REPAIR_ROUNDS):
            c_set = fold_sum(lambda sc, kp: in_set(sc, v, strict))
            vmin = fold_min(lambda sc, kp: jnp.where(in_set(sc, v, strict), sc, jnp.inf))
            c_gt = fold_sum(lambda sc, kp: sc > vmin)
            drop = (c_set > k) & (c_gt >= k)
            v = jnp.where(drop, vmin, v)
            strict = jnp.where(drop, 1.0, strict)
        c_set = fold_sum(lambda sc, kp: in_set(sc, v, strict))
        vt = fold_min(lambda sc, kp: jnp.where(in_set(sc, v, strict), sc, jnp.inf))
        need = k - fold_sum(lambda sc, kp: sc > vt)

        def idx_body(it, m):
            cand = m | lax.shift_left(I32(1), index_bits - 1 - it)
            below = fold_sum(lambda sc, kp: (sc == vt) & (kp < cand))
            return jnp.where(below < need, cand, m)

        m = lax.fori_loop(0, index_bits, idx_body, jnp.zeros(shape, I32))
        tie = c_set > k
        v_out = jnp.where(tie, vt, v)
        m_out = jnp.where(tie, m, jnp.where(strict > 0.5, -1, no_bound))
        return v_out, m_out

    return lax.cond(jnp.max(c_ge) > k, repair, lambda _: (thr, no_bound), 0)


def _chunk_loop(n, body, init):
    carry = lax.fori_loop(0, n // 2, lambda j, c: body(2 * j + 1, body(2 * j, c)), init)
    return lax.cond(n % 2 == 1, lambda c: body(n - 1, c), lambda c: c, carry)


def _dsa_prompt_kernel(q_ref, kv_ref, qia_ref, qib_ref, tailk_ref, tailq_ref, bias_ref, *rest, topk, n_cast):
    cast_in, rest = rest[:n_cast], rest[n_cast:]
    o_ref, rest = rest[0], rest[1:]
    cast_out, rest = rest[:n_cast], rest[n_cast:]
    kil_scr, kir_scr, sc_scr, lg_scr, m_scr, l_scr, acc_scr = rest
    i = pl.program_id(1)

    for src, dst in zip(cast_in, cast_out):
        dst[...] = src[...].astype(BF16)

    @pl.when(i == 0)
    def _():
        ki = tailk_ref[:, :IDX_DIM].astype(BF16)
        z = jnp.zeros_like(ki)
        kil_scr[...] = jnp.concatenate([ki, z], axis=-1)
        kir_scr[...] = jnp.concatenate([z, ki], axis=-1)

    wi = tailq_ref[:, IDX_DIM:IDX_DIM + IDX_HEADS] * (IDX_DIM ** -0.5)
    nk = (i + 2) // 2
    qpos = i * Q_BLOCK + lax.broadcasted_iota(I32, (Q_BLOCK, 1), 0)
    kloc = lax.broadcasted_iota(I32, (1, KEY_CHUNK), 1)
    heads_per_ref = qia_ref.shape[1] // IDX_DIM

    pairs = []
    for hp in range(IDX_HEADS // 2):
        ref = qia_ref if 2 * hp < heads_per_ref else qib_ref
        base = (2 * hp) % heads_per_ref * IDX_DIM
        pairs.append(ref[:, base:base + 2 * IDX_DIM])
    qstack = jnp.concatenate(pairs, axis=0)

    def score_chunk(c, carry):
        off = pl.multiple_of(c * KEY_CHUNK, KEY_CHUNK)
        acc = jnp.zeros((Q_BLOCK, KEY_CHUNK), F32)
        for side, k_scr in ((0, kil_scr), (1, kir_scr)):
            s = _dot_nt(qstack, k_scr[pl.ds(off, KEY_CHUNK), :])
            for hp in range(IDX_HEADS // 2):
                h = 2 * hp + side
                acc = acc + wi[:, h:h + 1] * jnp.maximum(s[hp * Q_BLOCK:(hp + 1) * Q_BLOCK], 0.0)
        kpos = c * KEY_CHUNK + kloc
        sc_scr[c] = jnp.where(kpos <= qpos, acc, -jnp.inf)
        return carry

    _chunk_loop(nk, score_chunk, 0)

    def fold_sum(pred):
        def body(c, cnt):
            m = jnp.where(pred(sc_scr[c], c * KEY_CHUNK + kloc), 1.0, 0.0)
            return cnt + m[:, :LANES] + m[:, LANES:]
        cnt = _chunk_loop(nk, body, jnp.zeros((Q_BLOCK, LANES), F32))
        return jnp.sum(cnt, axis=-1, keepdims=True)

    def fold_min(val):
        def body(c, acc):
            x = val(sc_scr[c], c * KEY_CHUNK + kloc)
            return jnp.minimum(acc, jnp.minimum(x[:, :LANES], x[:, LANES:]))
        acc = lax.fori_loop(0, nk, body, jnp.full((Q_BLOCK, LANES), jnp.inf, F32))
        return jnp.min(acc, axis=-1, keepdims=True)

    index_bits = (kv_ref.shape[0] - 1).bit_length()
    sel_v, sel_m = _topk_select(fold_sum, fold_min, (Q_BLOCK, 1), topk, index_bits)

    def mask_chunk(c, carry):
        kpos = c * KEY_CHUNK + kloc
        sc = sc_scr[c]
        chosen = (sc > sel_v) | ((sc == sel_v) & (kpos <= sel_m))
        sc_scr[c] = jnp.where(chosen & (kpos <= qpos), 0.0, -jnp.inf)
        return carry

    lax.fori_loop(0, nk, mask_chunk, 0)

    scale = B_HEAD_DIM ** -0.5
    rows = B_GQA * Q_BLOCK
    qgs = [jnp.concatenate(
        [q_ref[:, (g * B_GQA + hh) * B_HEAD_DIM:(g * B_GQA + hh + 1) * B_HEAD_DIM] for hh in range(B_GQA)],
        axis=0) for g in range(B_KV_HEADS)]
    m_scr[...] = jnp.full(m_scr.shape, -jnp.inf, F32)
    l_scr[...] = jnp.zeros(l_scr.shape, F32)
    acc_scr[...] = jnp.zeros(acc_scr.shape, F32)

    def logits_chunk(c, carry):
        off = pl.multiple_of(c * KEY_CHUNK, KEY_CHUNK)
        var = jnp.clip(2 * c - i, -(N_PAIR_VARIANTS - 1), 0) + (N_PAIR_VARIANTS - 1)
        sel = sc_scr[c]
        sel = jnp.concatenate([sel] * B_GQA, axis=0)
        for g in range(B_KV_HEADS):
            kc = kv_ref[pl.ds(off, KEY_CHUNK), g * B_HEAD_DIM:(g + 1) * B_HEAD_DIM]
            lg = _dot_nt(qgs[g], kc) * scale + bias_ref[var, g] + sel
            lg_scr[c, g] = lg
            m_scr[g] = jnp.maximum(m_scr[g], jnp.maximum(lg[:, :LANES], lg[:, LANES:]))
        return carry

    _chunk_loop(nk, logits_chunk, 0)
    for g in range(B_KV_HEADS):
        m = jnp.max(m_scr[g], axis=-1, keepdims=True)
        m_scr[g] = jnp.broadcast_to(m, (rows, LANES))

    def pv_chunk(c, carry):
        off = pl.multiple_of(c * KEY_CHUNK, KEY_CHUNK)
        for g in range(B_KV_HEADS):
            vc = kv_ref[pl.ds(off, KEY_CHUNK), KV_WIDTH + g * B_HEAD_DIM:KV_WIDTH + (g + 1) * B_HEAD_DIM]
            mb = m_scr[g]
            p = jnp.exp(lg_scr[c, g] - jnp.concatenate([mb, mb], axis=-1))
            l_scr[g] += p[:, :LANES] + p[:, LANES:]
            acc_scr[g] += jnp.dot(p.astype(BF16), vc, preferred_element_type=F32)
        return carry

    _chunk_loop(nk, pv_chunk, 0)
    for g in range(B_KV_HEADS):
        o = acc_scr[g] / jnp.sum(l_scr[g], axis=-1, keepdims=True)
        for hh in range(B_GQA):
            h = g * B_GQA + hh
            o_ref[:, h * B_HEAD_DIM:(h + 1) * B_HEAD_DIM] = o[hh * Q_BLOCK:(hh + 1) * Q_BLOCK].astype(BF16)


def _dsa_prompt(p, tail, bias_pairs, cast_weights, *, batch, seq):
    topk = min(TOPK_MAX, seq // 4)
    nblk = seq // Q_BLOCK
    qi_w = PROJ_TN
    steps = batch * nblk
    cast_specs, cast_shapes = [], []
    for w in cast_weights:
        ne, r, c = w.shape
        parts = steps // ne
        assert steps == ne * parts and r % (parts * 16) == 0
        spec = pl.BlockSpec((None, r // parts, c),
                            lambda b, i, parts=parts: ((b * nblk + i) // parts, (b * nblk + i) % parts, 0))
        cast_specs.append(spec)
        cast_shapes.append(jax.ShapeDtypeStruct(w.shape, BF16))
    outs = pl.pallas_call(
        functools.partial(_dsa_prompt_kernel, topk=topk, n_cast=len(cast_weights)),
        grid=(batch, nblk),
        in_specs=[
            pl.BlockSpec((Q_BLOCK, B_WIDTH), lambda b, i: (b * nblk + i, COL_Q // B_WIDTH)),
            pl.BlockSpec((seq, 2 * KV_WIDTH), lambda b, i: (b, COL_K // (2 * KV_WIDTH))),
            pl.BlockSpec((Q_BLOCK, qi_w), lambda b, i: (b * nblk + i, COL_QI // qi_w)),
            pl.BlockSpec((Q_BLOCK, qi_w), lambda b, i: (b * nblk + i, COL_QI // qi_w + 1)),
            pl.BlockSpec((seq, LANES), lambda b, i: (b, 0)),
            pl.BlockSpec((Q_BLOCK, LANES), lambda b, i: (b * nblk + i, 0)),
            pl.BlockSpec(bias_pairs.shape, lambda b, i: (0, 0, 0, 0)),
        ] + cast_specs,
        out_specs=[pl.BlockSpec((Q_BLOCK, B_WIDTH), lambda b, i: (b * nblk + i, 0))] + cast_specs,
        out_shape=[jax.ShapeDtypeStruct((batch * seq, B_WIDTH), BF16)] + cast_shapes,
        scratch_shapes=[
            pltpu.VMEM((seq, 2 * IDX_DIM), BF16),
            pltpu.VMEM((seq, 2 * IDX_DIM), BF16),
            pltpu.VMEM((seq // KEY_CHUNK, Q_BLOCK, KEY_CHUNK), F32),
            pltpu.VMEM((seq // KEY_CHUNK, B_KV_HEADS, B_GQA * Q_BLOCK, KEY_CHUNK), F32),
            pltpu.VMEM((B_KV_HEADS, B_GQA * Q_BLOCK, LANES), F32),
            pltpu.VMEM((B_KV_HEADS, B_GQA * Q_BLOCK, LANES), F32),
            pltpu.VMEM((B_KV_HEADS, B_GQA * Q_BLOCK, B_HEAD_DIM), F32),
        ],
        compiler_params=_cparams(("arbitrary", "arbitrary")),
        name="dsa_prompt",
    )(p, p, p, p, tail, tail, bias_pairs, *cast_weights)
    return outs[0], tuple(outs[1:])


PAGES_PER_STEP = 32
SAMPLE_SUB_PAGES = 32
SCORE_KEYS_PER_DOT = 2048
FOLD_WAYS = 8


def _fetch_pages(pt_ref, sources, bufs, sem, step, n_steps_total, steps_per_sample, pages_per_step):
    slot = step % 2

    def copies(n, slot_, page_of):
        sample = n // steps_per_sample
        first = (n % steps_per_sample) * pages_per_step
        for r in range(pages_per_step):
            page = page_of(sample, first + r)
            for a, (src, buf) in enumerate(zip(sources, bufs)):
                yield pltpu.make_async_copy(src.at[page], buf.at[slot_, r], sem.at[a, slot_])

    def table(sample, idx):
        return pt_ref[sample, idx]

    @pl.when(step == 0)
    def _():
        for cp in copies(step, slot, table):
            cp.start()

    @pl.when(step + 1 < n_steps_total)
    def _():
        for cp in copies(step + 1, 1 - slot, table):
            cp.start()

    for cp in copies(step, slot, lambda sample, idx: 0):
        cp.wait()
    return slot


def _dsa_sample_score_kernel(pt_ref, idx_hbm, qi_ref, wi_ref, kin_ref, sc_ref, scn_ref, selv_ref, selm_ref,
                             page_buf, sem, *, topk, dec_seq, n_pages, n_samples):
    slot = _fetch_pages(pt_ref, [idx_hbm], [page_buf], sem, pl.program_id(0), n_samples, 1, n_pages)
    pages = [page_buf.at[slot, r] for r in range(n_pages)]
    past = n_pages * PAGE_SIZE
    pages_per_dot = SCORE_KEYS_PER_DOT // PAGE_SIZE
    qi = qi_ref[...]
    wi = wi_ref[...] * (IDX_DIM ** -0.5)

    def scores(keys_t):
        r = jnp.maximum(jnp.dot(qi, keys_t, preferred_element_type=F32), 0.0) * wi
        acc = r[0:SUBLANES]
        for j in range(1, IDX_HEADS * dec_seq // SUBLANES):
            acc = acc + r[j * SUBLANES:(j + 1) * SUBLANES]
        shift = dec_seq
        while shift < SUBLANES:
            acc = acc + pltpu.roll(acc, shift, axis=0)
            shift *= 2
        return acc

    for d in range(n_pages // pages_per_dot):
        keys_t = jnp.concatenate(
            [pg[...].astype(BF16) for pg in pages[d * pages_per_dot:(d + 1) * pages_per_dot]], axis=1)
        sc_ref[:, d * SCORE_KEYS_PER_DOT:(d + 1) * SCORE_KEYS_PER_DOT] = scores(keys_t)

    row = lax.broadcasted_iota(I32, (SUBLANES, LANES), 0)
    col = lax.broadcasted_iota(I32, (SUBLANES, LANES), 1)
    scn = jnp.where((col <= row % dec_seq) & (col < dec_seq), scores(kin_ref[...]), -jnp.inf)
    scn_ref[...] = scn

    def pieces():
        yield scn, past + col[0:1]
        for w in range(past // LANES):
            yield sc_ref[:, w * LANES:(w + 1) * LANES], w * LANES + col[0:1]

    def fold(term, combine):
        parts = [None] * FOLD_WAYS
        for n, (sc, kp) in enumerate(pieces()):
            x = term(sc, kp)
            parts[n % FOLD_WAYS] = x if parts[n % FOLD_WAYS] is None else combine(parts[n % FOLD_WAYS], x)
        parts = [p for p in parts if p is not None]
        while len(parts) > 1:
            parts = [combine(parts[j], parts[j + 1]) if j + 1 < len(parts) else parts[j]
                     for j in range(0, len(parts), 2)]
        return parts[0]

    def fold_sum(pred):
        cnt = fold(lambda sc, kp: jnp.where(pred(sc, kp), 1.0, 0.0), jnp.add)
        return jnp.sum(cnt, axis=-1, keepdims=True)

    def fold_min(val):
        return jnp.min(fold(val, jnp.minimum), axis=-1, keepdims=True)

    index_bits = (past + LANES - 1).bit_length()
    sel_v, sel_m = _topk_select(fold_sum, fold_min, (SUBLANES, 1), topk, index_bits)
    selv_ref[...] = jnp.broadcast_to(sel_v, (SUBLANES, LANES))
    selm_ref[...] = jnp.broadcast_to(sel_m, (SUBLANES, LANES))


def _dsa_sample_attend_kernel(pt_ref, ck_hbm, cv_hbm, q_ref, sc_ref, scn_ref, selv_ref, selm_ref, kvn_ref,
                              sbias_ref, sconst_ref, o_ref, kbuf, vbuf, sem, m_scr, l_scr, acc_scr,
                              *, dec_seq, n_steps, n_samples):
    s = pl.program_id(1)
    slot = _fetch_pages(pt_ref, [ck_hbm, cv_hbm], [kbuf, vbuf], sem, pl.program_id(0) * n_steps + s,
                        n_samples * n_steps, n_steps, PAGES_PER_STEP)
    kpages = [kbuf.at[slot, r] for r in range(PAGES_PER_STEP)]
    vpages = [vbuf.at[slot, r] for r in range(PAGES_PER_STEP)]
    chunk = PAGES_PER_STEP * PAGE_SIZE
    rows = B_GQA * SUBLANES
    scale = B_HEAD_DIM ** -0.5

    @pl.when(s == 0)
    def _():
        m_scr[...] = jnp.full(m_scr.shape, NEG_INIT, F32)
        l_scr[...] = jnp.zeros(l_scr.shape, F32)
        acc_scr[...] = jnp.zeros(acc_scr.shape, F32)

    sel_v = selv_ref[:, 0:1]
    sel_m = selm_ref[:, 0:1]

    def chosen(sc, kpos):
        return jnp.where((sc > sel_v) | ((sc == sel_v) & (kpos <= sel_m)), 0.0, -jnp.inf)

    def head_rows(pgs, g):
        return jnp.concatenate(
            [pg[pl.ds(g, PAGE_SIZE, stride=B_KV_HEADS), :].astype(BF16) for pg in pgs], axis=0)

    kcol = lax.broadcasted_iota(I32, (SUBLANES, chunk), 1)
    sel = chosen(sc_ref[...], s * chunk + kcol)
    sel = jnp.concatenate([sel] * B_GQA, axis=0)
    is_last = s == n_steps - 1
    ncol = lax.broadcasted_iota(I32, (SUBLANES, LANES), 1)
    seln = chosen(scn_ref[...], n_steps * chunk + ncol)
    seln = jnp.where(is_last & (scn_ref[...] > -jnp.inf), seln, -jnp.inf)
    seln = jnp.concatenate([seln] * B_GQA, axis=0)

    def partial(lg, v):
        m = jnp.maximum(jnp.max(lg, axis=-1, keepdims=True), NEG_INIT)
        p = jnp.exp(lg - m)
        return m, jnp.sum(p, axis=-1, keepdims=True), jnp.dot(p.astype(BF16), v, preferred_element_type=F32)

    sub = SAMPLE_SUB_PAGES * PAGE_SIZE
    nsub = PAGES_PER_STEP // SAMPLE_SUB_PAGES
    lane_sub = lax.broadcasted_iota(I32, (rows, sub), 1)
    for g in range(B_KV_HEADS):
        qg = q_ref[g]
        far = sconst_ref[g, :, 0:1]
        near = jnp.concatenate([jnp.zeros((rows, sub - PAGE_SIZE), F32), sbias_ref[g, :, :PAGE_SIZE]], axis=-1)
        parts = [(m_scr[g], l_scr[g], acc_scr[g])]
        for j in range(nsub):
            pj = slice(j * SAMPLE_SUB_PAGES, (j + 1) * SAMPLE_SUB_PAGES)
            lg = _dot_nt(qg, head_rows(kpages[pj], g)) * scale + sel[:, j * sub:(j + 1) * sub]
            if j == nsub - 1:
                lg = lg + jnp.where(is_last & (lane_sub >= sub - PAGE_SIZE), near, far)
            else:
                lg = lg + far
            parts.append(partial(lg, head_rows(vpages[pj], g)))
        kn = kvn_ref[:, g * B_HEAD_DIM:(g + 1) * B_HEAD_DIM]
        vn = kvn_ref[:, KV_WIDTH + g * B_HEAD_DIM:KV_WIDTH + (g + 1) * B_HEAD_DIM]
        parts.append(partial(_dot_nt(qg, kn) * scale + sbias_ref[g, :, PAGE_SIZE:] + seln, vn))
        m_new = parts[0][0]
        for m, _, _ in parts[1:]:
            m_new = jnp.maximum(m_new, m)
        l_new = jnp.zeros((rows, 1), F32)
        acc_new = jnp.zeros((rows, B_HEAD_DIM), F32)
        for m, l, acc in parts:
            w = jnp.exp(m - m_new)
            l_new = l_new + w * l
            acc_new = acc_new + w * acc
        m_scr[g] = m_new
        l_scr[g] = l_new
        acc_scr[g] = acc_new

    @pl.when(is_last)
    def _():
        for g in range(B_KV_HEADS):
            o_ref[g] = (acc_scr[g] / l_scr[g]).astype(BF16)


def _dsa_sample(p_s, tail_s, k_s, v_s, cache_k_l, cache_v_l, cache_idx_l, page_table, sbias, sconst,
                *, dec_batch, dec_seq):
    n_pages = page_table.shape[1]
    past = n_pages * PAGE_SIZE
    topk = min(TOPK_MAX, (past + dec_seq) // 4)
    n_steps = n_pages // PAGES_PER_STEP
    chunk = PAGES_PER_STEP * PAGE_SIZE
    rows = B_GQA * SUBLANES
    n_pool = cache_k_l.shape[0]

    assert SUBLANES % dec_seq == 0
    qi = p_s[:, COL_QI:COL_MAIN].reshape(dec_batch, dec_seq, IDX_HEADS, IDX_DIM).transpose(0, 2, 1, 3)
    qi = qi.reshape(dec_batch, IDX_HEADS * dec_seq, IDX_DIM)
    wi = tail_s[:, IDX_DIM:IDX_DIM + IDX_HEADS].reshape(dec_batch, dec_seq, IDX_HEADS).transpose(0, 2, 1)
    wi = wi.reshape(dec_batch, IDX_HEADS * dec_seq, 1)
    ki_new = tail_s[:, :IDX_DIM].astype(BF16).reshape(dec_batch, dec_seq, IDX_DIM)
    ki_new = jnp.pad(ki_new, ((0, 0), (0, LANES - dec_seq), (0, 0))).transpose(0, 2, 1)
    q = p_s[:, COL_Q:COL_K].reshape(dec_batch, dec_seq, B_KV_HEADS, B_GQA, B_HEAD_DIM)
    q = jnp.pad(q.transpose(0, 2, 3, 1, 4), ((0, 0), (0, 0), (0, 0), (0, SUBLANES - dec_seq), (0, 0)))
    q = q.reshape(dec_batch, B_KV_HEADS, rows, B_HEAD_DIM)
    kv_new = jnp.concatenate([k_s, v_s], axis=-1).astype(BF16).reshape(dec_batch, dec_seq, 2 * KV_WIDTH)
    kv_new = jnp.pad(kv_new, ((0, 0), (0, LANES - dec_seq), (0, 0)))

    cache_idx_t = jnp.swapaxes(cache_idx_l, 1, 2)
    stat_spec = pl.BlockSpec((None, SUBLANES, LANES), lambda b, pt: (b, 0, 0))
    sc, scn, selv, selm = pl.pallas_call(
        functools.partial(_dsa_sample_score_kernel, topk=topk, dec_seq=dec_seq, n_pages=n_pages,
                          n_samples=dec_batch),
        grid_spec=pltpu.PrefetchScalarGridSpec(
            num_scalar_prefetch=1,
            grid=(dec_batch,),
            in_specs=[
                pl.BlockSpec(memory_space=pl.ANY),
                pl.BlockSpec((None, IDX_HEADS * dec_seq, IDX_DIM), lambda b, pt: (b, 0, 0)),
                pl.BlockSpec((None, IDX_HEADS * dec_seq, 1), lambda b, pt: (b, 0, 0)),
                pl.BlockSpec((None, IDX_DIM, LANES), lambda b, pt: (b, 0, 0)),
            ],
            out_specs=[pl.BlockSpec((None, SUBLANES, past), lambda b, pt: (b, 0, 0)),
                       stat_spec, stat_spec, stat_spec],
            scratch_shapes=[
                pltpu.VMEM((2, n_pages, IDX_DIM, PAGE_SIZE), F32),
                pltpu.SemaphoreType.DMA((1, 2)),
            ],
        ),
        out_shape=[
            jax.ShapeDtypeStruct((dec_batch, SUBLANES, past), F32),
            jax.ShapeDtypeStruct((dec_batch, SUBLANES, LANES), F32),
            jax.ShapeDtypeStruct((dec_batch, SUBLANES, LANES), F32),
            jax.ShapeDtypeStruct((dec_batch, SUBLANES, LANES), I32),
        ],
        compiler_params=_cparams(("arbitrary",)),
        name="dsa_sample_score",
    )(page_table, cache_idx_t, qi, wi, ki_new)

    ck = cache_k_l.reshape(n_pool, PAGE_SIZE * B_KV_HEADS, B_HEAD_DIM)
    cv = cache_v_l.reshape(n_pool, PAGE_SIZE * B_KV_HEADS, B_HEAD_DIM)
    kv_buf = pltpu.VMEM((2, PAGES_PER_STEP, PAGE_SIZE * B_KV_HEADS, B_HEAD_DIM), F32)
    o = pl.pallas_call(
        functools.partial(_dsa_sample_attend_kernel, dec_seq=dec_seq, n_steps=n_steps, n_samples=dec_batch),
        grid_spec=pltpu.PrefetchScalarGridSpec(
            num_scalar_prefetch=1,
            grid=(dec_batch, n_steps),
            in_specs=[pl.BlockSpec(memory_space=pl.ANY), pl.BlockSpec(memory_space=pl.ANY)] + [
                pl.BlockSpec((None, B_KV_HEADS, rows, B_HEAD_DIM), lambda b, s, pt: (b, 0, 0, 0)),
                pl.BlockSpec((None, SUBLANES, chunk), lambda b, s, pt: (b, 0, s)),
                pl.BlockSpec((None, SUBLANES, LANES), lambda b, s, pt: (b, 0, 0)),
                pl.BlockSpec((None, SUBLANES, LANES), lambda b, s, pt: (b, 0, 0)),
                pl.BlockSpec((None, SUBLANES, LANES), lambda b, s, pt: (b, 0, 0)),
                pl.BlockSpec((None, LANES, 2 * KV_WIDTH), lambda b, s, pt: (b, 0, 0)),
                pl.BlockSpec(sbias.shape, lambda b, s, pt: (0, 0, 0)),
                pl.BlockSpec(sconst.shape, lambda b, s, pt: (0, 0, 0)),
            ],
            out_specs=pl.BlockSpec((None, B_KV_HEADS, rows, B_HEAD_DIM), lambda b, s, pt: (b, 0, 0, 0)),
            scratch_shapes=[
                kv_buf,
                kv_buf,
                pltpu.SemaphoreType.DMA((2, 2)),
                pltpu.VMEM((B_KV_HEADS, rows, 1), F32),
                pltpu.VMEM((B_KV_HEADS, rows, 1), F32),
                pltpu.VMEM((B_KV_HEADS, rows, B_HEAD_DIM), F32),
            ],
        ),
        out_shape=jax.ShapeDtypeStruct((dec_batch, B_KV_HEADS, rows, B_HEAD_DIM), BF16),
        compiler_params=_cparams(("arbitrary", "arbitrary")),
        name="dsa_sample_attend",
    )(page_table, ck, cv, q, sc, scn, selv, selm, kv_new, sbias, sconst)
    o = o.reshape(dec_batch, B_KV_HEADS, B_GQA, SUBLANES, B_HEAD_DIM)[:, :, :, :dec_seq].transpose(0, 3, 1, 2, 4)
    return o.reshape(dec_batch * dec_seq, B_WIDTH)


def _mixout_kernel(u_ref, vn_ref, b_ref, ws_ref, ab_ref, x_ref, w_ref, o_ref, mix_scr, *, tm):
    j = pl.program_id(1)

    @pl.when(j == 0)
    def _():
        row = lax.broadcasted_iota(I32, (A_CHUNK, A_CHUNK), 0)
        col = lax.broadcasted_iota(I32, (A_CHUNK, A_CHUNK), 1)
        for g in range(A_GROUPS):
            wsg = jnp.where(row >= col, ws_ref[g], 0.0).astype(BF16)
            bg = ab_ref[:, g:g + 1]
            cs = slice(g * A_CH, (g + 1) * A_CH)
            for r in range(tm // A_CHUNK):
                rs = slice(r * A_CHUNK, (r + 1) * A_CHUNK)
                sm = jnp.dot(wsg, vn_ref[rs, cs], preferred_element_type=F32) + bg
                mix_scr[rs, cs] = (u_ref[rs, cs].astype(F32) * sm).astype(BF16)
        mix_scr[:, A_WIDTH:] = b_ref[...]

    o_ref[...] = x_ref[...] + jnp.dot(mix_scr[...], w_ref[...].astype(BF16), preferred_element_type=F32)


def _mixout(p, b_out, ws_eff, ab_eff, x2d, w_out_l, *, tm, tn=1024):
    n = x2d.shape[0]
    return pl.pallas_call(
        functools.partial(_mixout_kernel, tm=tm),
        grid=(n // tm, D_MODEL // tn),
        in_specs=[
            pl.BlockSpec((tm, A_WIDTH), lambda i, j: (i, COL_U // A_WIDTH)),
            pl.BlockSpec((tm, A_WIDTH), lambda i, j: (i, COL_V // A_WIDTH)),
            pl.BlockSpec((tm, B_WIDTH), lambda i, j: (i, 0)),
            pl.BlockSpec((A_GROUPS, A_CHUNK, A_CHUNK), lambda i, j: (0, 0, 0)),
            pl.BlockSpec((A_CHUNK, A_GROUPS), lambda i, j: (0, 0)),
            pl.BlockSpec((tm, tn), lambda i, j: (i, j)),
            pl.BlockSpec((A_WIDTH + B_WIDTH, tn), lambda i, j: (0, j)),
        ],
        out_specs=pl.BlockSpec((tm, tn), lambda i, j: (i, j)),
        out_shape=jax.ShapeDtypeStruct((n, D_MODEL), F32),
        scratch_shapes=[pltpu.VMEM((tm, A_WIDTH + B_WIDTH), BF16)],
        compiler_params=_cparams(("arbitrary", "arbitrary")),
        name="mixout",
    )(p, p, b_out, ws_eff, ab_eff, x2d, w_out_l)


def _memkv_kernel(x_ref, wk_ref, wv_ref, kn_ref, mk_ref, mv_ref):
    x = x_ref[...].astype(BF16)
    tm = x.shape[0]
    mk = _head_norm(jnp.dot(x, wk_ref[...].astype(BF16), preferred_element_type=F32), kn_ref[...])
    mv = jnp.dot(x, wv_ref[...].astype(BF16), preferred_element_type=F32)
    for hd in range(MEM_HEADS):
        cs = slice(hd * MEM_HEAD_DIM, (hd + 1) * MEM_HEAD_DIM)
        mk_ref[pl.ds(hd, tm, stride=MEM_HEADS), :] = mk[:, cs]
        mv_ref[pl.ds(hd, tm, stride=MEM_HEADS), :] = mv[:, cs]


def _memkv(mem2d, wk, wv, knorm, *, tm=256):
    n = mem2d.shape[0]
    kn = jnp.tile(knorm, MEM_HEADS).reshape(1, MEM_WIDTH)
    return pl.pallas_call(
        _memkv_kernel,
        grid=(n // tm,),
        in_specs=[
            pl.BlockSpec((tm, D_MODEL), lambda i: (i, 0)),
            pl.BlockSpec((D_MODEL, MEM_WIDTH), lambda i: (0, 0)),
            pl.BlockSpec((D_MODEL, MEM_WIDTH), lambda i: (0, 0)),
            pl.BlockSpec((1, MEM_WIDTH), lambda i: (0, 0)),
        ],
        out_specs=[pl.BlockSpec((tm * MEM_HEADS, MEM_HEAD_DIM), lambda i: (i, 0))] * 2,
        out_shape=[jax.ShapeDtypeStruct((n * MEM_HEADS, MEM_HEAD_DIM), F32)] * 2,
        compiler_params=_cparams(("arbitrary",)),
        name="memkv",
    )(mem2d, wk, wv, kn)


def _memattn_kernel(h_ref, g_ref, wq_ref, qn_ref, mk_ref, mv_ref, wo_ref, o_ref, *, rows_per_mem, mem_len):
    h = h_ref[...]
    n = _row_norm(h, g_ref[...]).astype(BF16)
    q = _head_norm(jnp.dot(n, wq_ref[...].astype(BF16), preferred_element_type=F32), qn_ref[...]).astype(BF16)
    tm, m = h.shape[0], mk_ref.shape[0] // MEM_HEADS
    if rows_per_mem is not None:
        row = lax.broadcasted_iota(I32, (tm, m), 0)
        col = lax.broadcasted_iota(I32, (tm, m), 1)
        mask = jnp.where(row // rows_per_mem == col // mem_len, 0.0, -jnp.inf)
    outs = []
    for hd in range(MEM_HEADS):
        cs = slice(hd * MEM_HEAD_DIM, (hd + 1) * MEM_HEAD_DIM)
        mk = mk_ref[pl.ds(hd, m, stride=MEM_HEADS), :].astype(BF16)
        mv = mv_ref[pl.ds(hd, m, stride=MEM_HEADS), :].astype(BF16)
        lg = _dot_nt(q[:, cs], mk) * (MEM_HEAD_DIM ** -0.5)
        if rows_per_mem is not None:
            lg = lg + mask
        e = jnp.exp(lg - jnp.max(lg, axis=-1, keepdims=True))
        l = jnp.sum(e, axis=-1, keepdims=True)
        outs.append(jnp.dot(e.astype(BF16), mv, preferred_element_type=F32) / l)
    o = jnp.concatenate(outs, axis=-1).astype(BF16)
    o_ref[...] = h + jnp.dot(o, wo_ref[...].astype(BF16), preferred_element_type=F32)


def _memattn(h2d, gain, wq, qnorm, mk, mv, wo, *, tm, mem_rows, tiles_per_mem, rows_per_mem, mem_len):
    n = h2d.shape[0]
    qn = jnp.tile(qnorm, MEM_HEADS).reshape(1, MEM_WIDTH)
    return pl.pallas_call(
        functools.partial(_memattn_kernel, rows_per_mem=rows_per_mem, mem_len=mem_len),
        grid=(n // tm,),
        in_specs=[
            pl.BlockSpec((tm, D_MODEL), lambda i: (i, 0)),
            pl.BlockSpec((1, D_MODEL), lambda i: (0, 0)),
            pl.BlockSpec((D_MODEL, MEM_WIDTH), lambda i: (0, 0)),
            pl.BlockSpec((1, MEM_WIDTH), lambda i: (0, 0)),
            pl.BlockSpec((mem_rows * MEM_HEADS, MEM_HEAD_DIM), lambda i: (i // tiles_per_mem, 0)),
            pl.BlockSpec((mem_rows * MEM_HEADS, MEM_HEAD_DIM), lambda i: (i // tiles_per_mem, 0)),
            pl.BlockSpec((MEM_WIDTH, D_MODEL), lambda i: (0, 0)),
        ],
        out_specs=pl.BlockSpec((tm, D_MODEL), lambda i: (i, 0)),
        out_shape=jax.ShapeDtypeStruct((n, D_MODEL), F32),
        compiler_params=_cparams(("arbitrary",)),
        name="memattn",
    )(h2d, gain.reshape(1, D_MODEL), wq, qn, mk, mv, wo)


def _router_kernel(h_ref, g_ref, w_ref, b_ref, n_ref, c_ref):
    n = _row_norm(h_ref[...], g_ref[...]).astype(BF16)
    n_ref[...] = n
    lg = jnp.dot(n, w_ref[...].astype(BF16), preferred_element_type=F32) + b_ref[...]
    gl, el = lg[:, :LANES], lg[:, LANES:]
    lane = lax.broadcasted_iota(I32, gl.shape, 1)
    lanef = lane.astype(F32)

    def first_max_lane(v, vmax):
        return jnp.min(jnp.where(v == vmax, lanef, float(LANES)), axis=-1, keepdims=True).astype(I32)

    gl = jnp.where(lane < N_GROUPS, gl, -jnp.inf)
    ge = jnp.exp(gl - jnp.max(gl, axis=-1, keepdims=True))
    gp = ge / jnp.sum(ge, axis=-1, keepdims=True)
    g_gate = jnp.max(gp, axis=-1, keepdims=True)
    g_sel = first_max_lane(gp, g_gate)
    in_grp = (lane // EXPERTS_PER_GROUP == g_sel) & (lane < N_EXPERTS)
    el = jnp.where(in_grp, el, -jnp.inf)
    ee = jnp.exp(el - jnp.max(el, axis=-1, keepdims=True))
    ep = jnp.where(in_grp, ee / jnp.sum(ee, axis=-1, keepdims=True), -jnp.inf)
    w1 = jnp.max(ep, axis=-1, keepdims=True)
    i1 = first_max_lane(ep, w1)
    ep2 = jnp.where(lane == i1, -jnp.inf, ep)
    w2 = jnp.max(ep2, axis=-1, keepdims=True)
    i2 = first_max_lane(ep2, w2)
    tot = w1 + w2
    comb = jnp.where(lane == i1, w1 / tot * g_gate, jnp.where(lane == i2, w2 / tot * g_gate, 0.0))
    c_ref[...] = jnp.where(lane == GSEL_LANE, g_sel.astype(F32), comb)


def _router(h2d, gain, w_cat, b_cat, *, tm):
    n = h2d.shape[0]
    return pl.pallas_call(
        _router_kernel,
        grid=(n // tm,),
        in_specs=[
            pl.BlockSpec((tm, D_MODEL), lambda i: (i, 0)),
            pl.BlockSpec((1, D_MODEL), lambda i: (0, 0)),
            pl.BlockSpec((D_MODEL, 2 * LANES), lambda i: (0, 0)),
            pl.BlockSpec((1, 2 * LANES), lambda i: (0, 0)),
        ],
        out_specs=[pl.BlockSpec((tm, D_MODEL), lambda i: (i, 0)), pl.BlockSpec((tm, LANES), lambda i: (i, 0))],
        out_shape=[jax.ShapeDtypeStruct((n, D_MODEL), BF16), jax.ShapeDtypeStruct((n, LANES), F32)],
        compiler_params=_cparams(("arbitrary",)),
        name="router",
    )(h2d, gain.reshape(1, D_MODEL), w_cat, b_cat)


def _cast_kernel(x_ref, o_ref):
    o_ref[...] = x_ref[...].astype(BF16)


def _cast_bf16(w):
    ne, r, c = w.shape
    return pl.pallas_call(
        _cast_kernel,
        grid=(ne,),
        in_specs=[pl.BlockSpec((None, r, c), lambda e: (e, 0, 0))],
        out_specs=pl.BlockSpec((None, r, c), lambda e: (e, 0, 0)),
        out_shape=jax.ShapeDtypeStruct(w.shape, BF16),
        compiler_params=_cparams(("arbitrary",)),
        name="cast_bf16",
    )(w)


GSEL_LANE = N_EXPERTS
MOE_ROWS = 128
MOE_MAIN_SLACK = 0
MOE_FF_SPLIT = 1


def _moe_routed_kernel(n_ref, c_ref, h_hbm, wg_ref, wu_ref, wd_ref, o_ref,
                       tri_scr, key_scr, xc_scr, yc_scr, cw_scr, nch_ref, sem, *, tm):
    i, g, e, f = (pl.program_id(k) for k in range(4))
    main_rows = tm // N_GROUPS + MOE_MAIN_SLACK
    first_e = (e == 0) & (f == 0)
    last_e = (e == EXPERTS_PER_GROUP - 1) & (f == MOE_FF_SPLIT - 1)

    def onehot_rows(start, rows):
        want = (start + lax.broadcasted_iota(I32, (rows, 1), 0)).astype(F32)
        return jnp.where(key_scr[pl.ds(g, 1), :] == want, 1.0, 0.0).astype(BF16)

    def tail_start(c):
        return pl.multiple_of(main_rows + c * MOE_ROWS, math.gcd(main_rows, MOE_ROWS))

    @pl.when((i == 0) & (g == 0) & first_e)
    def _():
        r = lax.broadcasted_iota(I32, (tm, tm), 0)
        c = lax.broadcasted_iota(I32, (tm, tm), 1)
        tri_scr[...] = jnp.where(r < c, 1.0, 0.0).astype(BF16)

    def residual_copy():
        return pltpu.make_async_copy(h_hbm.at[pl.ds(pl.multiple_of(i * tm, tm), tm)], o_ref, sem)

    @pl.when((g == 0) & first_e)
    def _():
        residual_copy().start()
        lane = lax.broadcasted_iota(I32, (tm, LANES), 1).astype(F32)
        onehot = jnp.where(lane == c_ref[:, GSEL_LANE:GSEL_LANE + 1], 1.0, 0.0).astype(BF16)
        er = lax.broadcasted_iota(I32, (SUBLANES, LANES), 0)
        ec = lax.broadcasted_iota(I32, (SUBLANES, LANES), 1)
        eye = jnp.where(er == ec, 1.0, 0.0).astype(BF16)
        mask_t = _dot_nt(eye, onehot)
        rank_t = jnp.dot(mask_t.astype(BF16), tri_scr[...], preferred_element_type=F32)
        key_scr[...] = jnp.where(mask_t > 0.5, rank_t, -1.0)
        for gg in range(N_GROUPS):
            cnt = jnp.sum(mask_t[gg:gg + 1, :], axis=-1, keepdims=True)
            extra = jnp.ceil(jnp.maximum(cnt - main_rows, 0.0) * (1.0 / MOE_ROWS))
            nch_ref[gg] = jnp.sum(extra).astype(I32)

    ntail = nch_ref[g]

    @pl.when(first_e)
    def _():
        comb = c_ref[...]
        hi = comb.astype(BF16)
        r1 = comb - hi.astype(F32)
        mid = r1.astype(BF16)
        lo = (r1 - mid.astype(F32)).astype(BF16)

        def compact(start, rows):
            s = onehot_rows(start, rows)
            xc_scr[pl.ds(start, rows), :] = jnp.dot(s, n_ref[...], preferred_element_type=F32).astype(BF16)
            cw_scr[pl.ds(start, rows), :] = (jnp.dot(s, hi, preferred_element_type=F32)
                                             + jnp.dot(s, mid, preferred_element_type=F32)
                                             + jnp.dot(s, lo, preferred_element_type=F32))
            yc_scr[pl.ds(start, rows), :] = jnp.zeros((rows, D_MODEL), F32)

        compact(0, main_rows)
        lax.fori_loop(0, ntail, lambda c, carry: (compact(tail_start(c), MOE_ROWS), carry)[1], 0)

    wg = wg_ref[...].astype(BF16)
    wu = wu_ref[...].astype(BF16)
    wd = wd_ref[...].astype(BF16)
    ex = g * EXPERTS_PER_GROUP + e

    def expert(start, rows):
        x = xc_scr[pl.ds(start, rows), :]
        a = jnp.dot(x, wg, preferred_element_type=F32)
        u = jnp.dot(x, wu, preferred_element_type=F32)
        lane = lax.broadcasted_iota(I32, (rows, LANES), 1)
        cw = jnp.sum(jnp.where(lane == ex, cw_scr[pl.ds(start, rows), :], 0.0), axis=-1, keepdims=True)
        hid = (a * jax.nn.sigmoid(a) * u * cw).astype(BF16)
        yc_scr[pl.ds(start, rows), :] += jnp.dot(hid, wd, preferred_element_type=F32)

    expert(0, main_rows)
    lax.fori_loop(0, ntail, lambda c, carry: (expert(tail_start(c), MOE_ROWS), carry)[1], 0)

    @pl.when((g == 0) & last_e)
    def _():
        residual_copy().wait()

    @pl.when(last_e)
    def _():
        def scatter(start, rows):
            s = onehot_rows(start, rows)
            y = yc_scr[pl.ds(start, rows), :].astype(BF16)
            o_ref[...] += lax.dot_general(s, y, (((0,), (0,)), ((), ())), preferred_element_type=F32)

        scatter(0, main_rows)
        lax.fori_loop(0, ntail, lambda c, carry: (scatter(tail_start(c), MOE_ROWS), carry)[1], 0)


def _moe_routed(n3, comb, h2d, w_gate_l, w_up_l, w_down_l, *, tm):
    n = h2d.shape[0]
    ff = EXPERT_FF // MOE_FF_SPLIT

    def wmap(i, g, e, f):
        return (g * EXPERTS_PER_GROUP + e, 0, f)

    return pl.pallas_call(
        functools.partial(_moe_routed_kernel, tm=tm),
        grid=(n // tm, N_GROUPS, EXPERTS_PER_GROUP, MOE_FF_SPLIT),
        in_specs=[
            pl.BlockSpec((tm, D_MODEL), lambda i, g, e, f: (i, 0), pipeline_mode=pl.Buffered(1)),
            pl.BlockSpec((tm, LANES), lambda i, g, e, f: (i, 0), pipeline_mode=pl.Buffered(1)),
            pl.BlockSpec(memory_space=pl.ANY),
            pl.BlockSpec((None, D_MODEL, ff), wmap),
            pl.BlockSpec((None, D_MODEL, ff), wmap),
            pl.BlockSpec((None, ff, D_MODEL), lambda i, g, e, f: (g * EXPERTS_PER_GROUP + e, f, 0)),
        ],
        out_specs=pl.BlockSpec((tm, D_MODEL), lambda i, g, e, f: (i, 0)),
        out_shape=jax.ShapeDtypeStruct((n, D_MODEL), F32),
        scratch_shapes=[
            pltpu.VMEM((tm, tm), BF16),
            pltpu.VMEM((SUBLANES, tm), F32),
            pltpu.VMEM((tm + MOE_ROWS, D_MODEL), BF16),
            pltpu.VMEM((tm + MOE_ROWS, D_MODEL), F32),
            pltpu.VMEM((tm + MOE_ROWS, LANES), F32),
            pltpu.SMEM((N_GROUPS,), I32),
            pltpu.SemaphoreType.DMA(()),
        ],
        compiler_params=pltpu.CompilerParams(
            dimension_semantics=("arbitrary",) * 4, vmem_limit_bytes=MOE_VMEM_LIMIT),
        name="moe_routed",
    )(n3, comb, h2d, w_gate_l, w_up_l, w_down_l)


def _moe_kernel(n_ref, c_ref, h_ref, wg_ref, wu_ref, wd_ref, o_ref):
    e = pl.program_id(1)

    @pl.when(e == 0)
    def _():
        o_ref[...] = h_ref[...]

    x = n_ref[...]
    a = jnp.dot(x, wg_ref[...].astype(BF16), preferred_element_type=F32)
    u = jnp.dot(x, wu_ref[...].astype(BF16), preferred_element_type=F32)
    lane = lax.broadcasted_iota(I32, c_ref.shape, 1)
    c = jnp.sum(jnp.where(lane == e, c_ref[...], 0.0), axis=-1, keepdims=True)
    hid = (a * jax.nn.sigmoid(a) * u * c).astype(BF16)
    o_ref[...] += jnp.dot(hid, wd_ref[...].astype(BF16), preferred_element_type=F32)


def _moe(n3, comb, h2d, w_gate_l, w_up_l, w_down_l, *, tm):
    n = h2d.shape[0]
    return pl.pallas_call(
        _moe_kernel,
        grid=(n // tm, N_EXPERTS),
        in_specs=[
            pl.BlockSpec((tm, D_MODEL), lambda i, e: (i, 0)),
            pl.BlockSpec((tm, LANES), lambda i, e: (i, 0)),
            pl.BlockSpec((tm, D_MODEL), lambda i, e: (i, 0)),
            pl.BlockSpec((None, D_MODEL, EXPERT_FF), lambda i, e: (e, 0, 0)),
            pl.BlockSpec((None, D_MODEL, EXPERT_FF), lambda i, e: (e, 0, 0)),
            pl.BlockSpec((None, EXPERT_FF, D_MODEL), lambda i, e: (e, 0, 0)),
        ],
        out_specs=pl.BlockSpec((tm, D_MODEL), lambda i, e: (i, 0)),
        out_shape=jax.ShapeDtypeStruct((n, D_MODEL), F32),
        compiler_params=_cparams(("arbitrary", "arbitrary")),
        name="moe",
    )(n3, comb, h2d, w_gate_l, w_up_l, w_down_l)


def _post_mixer(p, b_out, ws_eff, ab_eff, x2d, mk, mv, lw, *, tm_mix, tm_mem, mem_rows, tiles_per_mem,
                rows_per_mem, mem_len, tm_route, tm_moe):
    h1 = _mixout(p, b_out, ws_eff, ab_eff, x2d, lw["w_out"], tm=tm_mix)
    h2 = _memattn(h1, lw["norm_mem"], lw["mem_wq"], lw["mem_qnorm"], mk, mv, lw["mem_wo"], tm=tm_mem,
                  mem_rows=mem_rows, tiles_per_mem=tiles_per_mem, rows_per_mem=rows_per_mem, mem_len=mem_len)
    n3, comb = _router(h2, lw["norm_ffn"], lw["w_cat"], lw["b_cat"], tm=tm_route)
    moe = _moe_routed if tm_moe > N_GROUPS * MOE_ROWS else _moe
    return moe(n3, comb, h2, lw["w_gate"], lw["w_up"], lw["w_down"], tm=tm_moe)


def kernel(x_prompt, x_sample, cache_k, cache_v, cache_idx_k, cache_mem_k, cache_mem_v, page_table,
           mem_prompt, norm_mix, w_in, a_vnorm, a_ws, a_b, b_qnorm, b_knorm, rel_bias, w_out,
           norm_mem, mem_wq, mem_wk, mem_wv, mem_qnorm, mem_knorm, mem_wo, norm_ffn,
           w_group, b_group, w_router, b_router, w_gate, w_up, w_down):
    bp, t, d = x_prompt.shape
    bd, s, _ = x_sample.shape
    depth = w_in.shape[0]
    mem_len = mem_prompt.shape[1]
    assert d == D_MODEL and t % KEY_CHUNK == 0 and s <= 8 and (bd * s) % LANES == 0

    bias_pairs, sbias, sconst = _bias_tables(rel_bias)
    hp = x_prompt.reshape(bp * t, d)
    hs = x_sample.reshape(bd * s, d)
    outs = [[] for _ in range(9)]
    for l in range(depth):
        w_cat = jnp.zeros((d, 2 * LANES), F32)
        w_cat = w_cat.at[:, :N_GROUPS].set(w_group[l]).at[:, LANES:LANES + N_EXPERTS].set(w_router[l])
        b_cat = jnp.zeros((1, 2 * LANES), F32)
        b_cat = b_cat.at[0, :N_GROUPS].set(b_group[l]).at[0, LANES:LANES + N_EXPERTS].set(b_router[l])
        lw = dict(w_out=w_out[l], norm_mem=norm_mem[l], mem_wq=mem_wq[l], mem_qnorm=mem_qnorm[l],
                  mem_wo=mem_wo[l], norm_ffn=norm_ffn[l], w_cat=w_cat, b_cat=b_cat)

        p, k, v, tail = _proj(hp, norm_mix[l], w_in[l], a_vnorm[l], b_qnorm[l], b_knorm[l],
                              tm=1024, want_vn=False)
        later_weights = (w_gate[l], w_up[l], w_down[l], w_out[l][None])
        fuse_cast = (bp * (t // Q_BLOCK)) % N_EXPERTS == 0
        b_out, casted = _dsa_prompt(p, tail, bias_pairs, later_weights if fuse_cast else (), batch=bp, seq=t)
        if not fuse_cast:
            casted = tuple(_cast_bf16(w) for w in later_weights)
        lw.update(w_gate=casted[0], w_up=casted[1], w_down=casted[2], w_out=casted[3][0])
        mk, mv = _memkv(mem_prompt.reshape(bp * mem_len, d), mem_wk[l], mem_wv[l], mem_knorm[l])
        hp = _post_mixer(p, b_out, a_ws[l], a_b[l].T, hp, mk, mv, lw, tm_mix=1024, tm_mem=512,
                         mem_rows=mem_len, tiles_per_mem=t // 512, rows_per_mem=None, mem_len=mem_len,
                         tm_route=512, tm_moe=1024)
        outs[0].append(k.reshape(bp, t, B_KV_HEADS, B_HEAD_DIM))
        outs[1].append(v.reshape(bp, t, B_KV_HEADS, B_HEAD_DIM))
        outs[2].append(tail[:, :IDX_DIM].reshape(bp, t, IDX_DIM))
        outs[3].append(mk.reshape(bp, mem_len, MEM_HEADS, MEM_HEAD_DIM))
        outs[4].append(mv.reshape(bp, mem_len, MEM_HEADS, MEM_HEAD_DIM))

        ns = bd * s
        p_s, k_s, v_s, tail_s, vn_s = _proj(hs, norm_mix[l], w_in[l], a_vnorm[l], b_qnorm[l], b_knorm[l],
                                            tm=ns, want_vn=True)
        b_out_s = _dsa_sample(p_s, tail_s, k_s.reshape(ns, KV_WIDTH), v_s.reshape(ns, KV_WIDTH),
                              cache_k[l], cache_v[l], cache_idx_k[l], page_table,
                              sbias, sconst, dec_batch=bd, dec_seq=s)
        ws_s = jnp.einsum("ab,gts->gatbs", jnp.eye(ns // s, dtype=F32), a_ws[l][:, :s, :s])
        ws_s = ws_s.reshape(A_GROUPS, ns, ns)
        ab_s = jnp.tile(a_b[l][:, :s].T, (ns // s, 1))
        mem_tile = 32
        hs = _post_mixer(p_s, b_out_s, ws_s, ab_s, hs,
                         cache_mem_k[l].reshape(bd * mem_len * MEM_HEADS, MEM_HEAD_DIM),
                         cache_mem_v[l].reshape(bd * mem_len * MEM_HEADS, MEM_HEAD_DIM), lw,
                         tm_mix=ns, tm_mem=mem_tile, mem_rows=mem_tile // s * mem_len, tiles_per_mem=1,
                         rows_per_mem=s, mem_len=mem_len, tm_route=ns, tm_moe=ns)
        outs[5].append(k_s.reshape(bd, s, B_KV_HEADS, B_HEAD_DIM))
        outs[6].append(v_s.reshape(bd, s, B_KV_HEADS, B_HEAD_DIM))
        outs[7].append(tail_s[:, :IDX_DIM].reshape(bd, s, IDX_DIM))
        outs[8].append(vn_s.reshape(bd, s, A_GROUPS, A_CH))
    return (hp.reshape(bp, t, d), hs.reshape(bd, s, d)) + tuple(jnp.stack(o) for o in outs)
```

```python
import functools
import math

import jax
import jax.numpy as jnp
from jax import lax
from jax.experimental import pallas as pl
from jax.experimental.pallas import tpu as pltpu

BF16 = jnp.bfloat16
F32 = jnp.float32
I32 = jnp.int32

D_MODEL = 2048
A_GROUPS = 8
A_CH = 128
A_WIDTH = A_GROUPS * A_CH
A_CHUNK = 128
B_HEADS = 8
B_HEAD_DIM = 128
B_KV_HEADS = 2
B_GQA = B_HEADS // B_KV_HEADS
B_WIDTH = B_HEADS * B_HEAD_DIM
KV_WIDTH = B_KV_HEADS * B_HEAD_DIM
IDX_HEADS = 16
IDX_DIM = 64
TOPK_MAX = 256
Q_BLOCK = 128
REL_BUCKETS = 32
REL_MAX_DIST = 128
MEM_HEADS = 4
MEM_HEAD_DIM = 128
MEM_WIDTH = MEM_HEADS * MEM_HEAD_DIM
N_GROUPS = 4
EXPERTS_PER_GROUP = 4
N_EXPERTS = N_GROUPS * EXPERTS_PER_GROUP
EXPERT_FF = 512
PAGE_SIZE = 128
EPS = 1e-6

COL_U = 0
COL_V = COL_U + A_WIDTH
COL_Q = COL_V + A_WIDTH
COL_K = COL_Q + B_WIDTH
COL_VV = COL_K + KV_WIDTH
COL_QI = COL_VV + KV_WIDTH
COL_MAIN = COL_QI + IDX_HEADS * IDX_DIM
TAIL_COLS = IDX_DIM + IDX_HEADS
LANES = 128
SUBLANES = 8
PROJ_TN = 512
PROJ_ROW_CHUNK = 256
KEY_CHUNK = 256
NEG_INIT = -1e30
VMEM_LIMIT = 56 * 1024 * 1024
MOE_VMEM_LIMIT = 60 * 1024 * 1024


def _cparams(sem):
    return pltpu.CompilerParams(dimension_semantics=sem, vmem_limit_bytes=VMEM_LIMIT)


def _gelu(x):
    return 0.5 * x * (1.0 + lax.erf(x * (2.0 ** -0.5)))


def _head_norm(a, gain, width=LANES):
    outs = []
    for c in range(a.shape[1] // width):
        blk = a[:, c * width:(c + 1) * width]
        ms = jnp.mean(blk * blk, axis=-1, keepdims=True)
        outs.append(blk * lax.rsqrt(ms + EPS) * gain[:, c * width:(c + 1) * width])
    return outs[0] if len(outs) == 1 else jnp.concatenate(outs, axis=-1)


def _row_norm(x, gain):
    ms = jnp.mean(x * x, axis=-1, keepdims=True)
    return x * lax.rsqrt(ms + EPS) * gain


def _dot_nt(a, b):
    return lax.dot_general(a, b, (((1,), (1,)), ((), ())), preferred_element_type=F32)


def _proj_kernel(x_ref, g_ref, w_ref, wt_ref, avn_ref, qn_ref, kn_ref, *rest, want_vn):
    if want_vn:
        p_ref, k_ref, v_ref, tail_ref, vn_ref, xn_scr = rest
    else:
        p_ref, k_ref, v_ref, tail_ref, xn_scr = rest
        vn_ref = None
    j = pl.program_id(1)
    ju = COL_V // PROJ_TN
    jv = COL_Q // PROJ_TN
    jq = COL_K // PROJ_TN
    jkv = COL_QI // PROJ_TN

    @pl.when(j == 0)
    def _():
        xn = _row_norm(x_ref[...], g_ref[...]).astype(BF16)
        xn_scr[...] = xn
        wrow = lax.broadcasted_iota(I32, wt_ref.shape, 0)
        wt = jnp.where(wrow < TAIL_COLS, wt_ref[...], 0.0).astype(BF16)
        t = _dot_nt(xn, wt)
        lane = lax.broadcasted_iota(I32, t.shape, 1)
        tail_ref[...] = jnp.where(lane >= IDX_DIM, t * (IDX_HEADS ** -0.5), t)

    tm = xn_scr.shape[0]
    rc = min(tm, PROJ_ROW_CHUNK)

    def row_chunks(epilogue):
        w = w_ref[...].astype(BF16)
        for r in range(tm // rc):
            rs = slice(r * rc, (r + 1) * rc)
            epilogue(_dot_nt(xn_scr[rs, :], w), rs)

    @pl.when(j < ju)
    def _():
        def epi(acc, rs):
            p_ref[rs, :] = _gelu(acc).astype(BF16)
        row_chunks(epi)

    @pl.when((j >= ju) & (j < jv))
    def _():
        def epi(acc, rs):
            vn = _head_norm(_gelu(acc), avn_ref[...])
            p_ref[rs, :] = vn.astype(BF16)
            if vn_ref is not None:
                vn_ref[rs, :] = vn
        row_chunks(epi)

    @pl.when((j >= jv) & (j < jq))
    def _():
        def epi(acc, rs):
            p_ref[rs, :] = _head_norm(acc, qn_ref[...]).astype(BF16)
        row_chunks(epi)

    @pl.when(j == jq)
    def _():
        def epi(acc, rs):
            k = _head_norm(acc[:, :KV_WIDTH], kn_ref[...])
            v = acc[:, KV_WIDTH:]
            for g in range(B_KV_HEADS):
                dst = pl.ds(rs.start * B_KV_HEADS + g, rs.stop - rs.start, stride=B_KV_HEADS)
                k_ref[dst, :] = k[:, g * B_HEAD_DIM:(g + 1) * B_HEAD_DIM]
                v_ref[dst, :] = v[:, g * B_HEAD_DIM:(g + 1) * B_HEAD_DIM]
            p_ref[rs, :] = jnp.concatenate([k, v], axis=-1).astype(BF16)
        row_chunks(epi)

    @pl.when(j >= jkv)
    def _():
        def epi(acc, rs):
            p_ref[rs, :] = acc.astype(BF16)
        row_chunks(epi)


def _proj(x2d, gain, w_in_l, a_vnorm_l, b_qnorm_l, b_knorm_l, *, tm, want_vn):
    n = x2d.shape[0]
    nj = COL_MAIN // PROJ_TN
    w_in_t = jnp.swapaxes(w_in_l, 0, 1)
    avn = a_vnorm_l.reshape(1, A_WIDTH)
    qn = jnp.tile(b_qnorm_l, PROJ_TN // B_HEAD_DIM).reshape(1, PROJ_TN)
    kn = jnp.tile(b_knorm_l, B_KV_HEADS).reshape(1, KV_WIDTH)
    ju = COL_V // PROJ_TN
    nv = A_WIDTH // PROJ_TN

    def vmap_(i, j):
        return (0, jnp.clip(j - ju, 0, nv - 1))

    in_specs = [
        pl.BlockSpec((tm, D_MODEL), lambda i, j: (i, 0)),
        pl.BlockSpec((1, D_MODEL), lambda i, j: (0, 0)),
        pl.BlockSpec((PROJ_TN, D_MODEL), lambda i, j: (j, 0)),
        pl.BlockSpec((LANES, D_MODEL), lambda i, j: (COL_MAIN // LANES, 0)),
        pl.BlockSpec((1, PROJ_TN), vmap_),
        pl.BlockSpec((1, PROJ_TN), lambda i, j: (0, 0)),
        pl.BlockSpec((1, KV_WIDTH), lambda i, j: (0, 0)),
    ]
    out_shape = [
        jax.ShapeDtypeStruct((n, COL_MAIN), BF16),
        jax.ShapeDtypeStruct((n * B_KV_HEADS, B_HEAD_DIM), F32),
        jax.ShapeDtypeStruct((n * B_KV_HEADS, B_HEAD_DIM), F32),
        jax.ShapeDtypeStruct((n, LANES), F32),
    ]
    out_specs = [
        pl.BlockSpec((tm, PROJ_TN), lambda i, j: (i, j)),
        pl.BlockSpec((tm * B_KV_HEADS, B_HEAD_DIM), lambda i, j: (i, 0)),
        pl.BlockSpec((tm * B_KV_HEADS, B_HEAD_DIM), lambda i, j: (i, 0)),
        pl.BlockSpec((tm, LANES), lambda i, j: (i, 0)),
    ]
    if want_vn:
        out_shape.append(jax.ShapeDtypeStruct((n, A_WIDTH), F32))
        out_specs.append(pl.BlockSpec((tm, PROJ_TN), lambda i, j: (i, jnp.clip(j - ju, 0, nv - 1))))
    return pl.pallas_call(
        functools.partial(_proj_kernel, want_vn=want_vn),
        grid=(n // tm, nj),
        in_specs=in_specs,
        out_specs=out_specs,
        out_shape=out_shape,
        scratch_shapes=[pltpu.VMEM((tm, D_MODEL), BF16)],
        compiler_params=_cparams(("arbitrary", "arbitrary")),
        name="proj",
    )(x2d, gain.reshape(1, D_MODEL), w_in_t, w_in_t, avn, qn, kn)


def _t5_bucket(d):
    max_exact = REL_BUCKETS // 2
    d = jnp.maximum(d, 0)
    ratio = jnp.log(jnp.maximum(d, 1).astype(F32) / max_exact) / math.log(REL_MAX_DIST / max_exact)
    large = jnp.minimum(max_exact + jnp.floor(ratio * (REL_BUCKETS - max_exact)).astype(I32), REL_BUCKETS - 1)
    return jnp.where(d < max_exact, d, large)


def _bias_lookup(bucket, rb_ref, h):
    acc = jnp.zeros(bucket.shape, F32)
    for r in range(REL_BUCKETS):
        acc = jnp.where(bucket == r, rb_ref[r, h], acc)
    return acc


def _bias_kernel(rb_ref, pair_ref, samp_ref, sconst_ref):
    q = lax.broadcasted_iota(I32, (Q_BLOCK, Q_BLOCK), 0)
    kc = lax.broadcasted_iota(I32, (Q_BLOCK, Q_BLOCK), 1)
    nvar = pair_ref.shape[0]
    rels = range(-(nvar - 1), 2)
    buckets = {r: _t5_bucket(q - kc - r * Q_BLOCK) for r in rels}
    for h in range(B_HEADS):
        g, hh = divmod(h, B_GQA)
        tiles = {r: _bias_lookup(buckets[r], rb_ref, h) for r in rels}
        for var in range(nvar):
            r0 = var - (nvar - 1)
            for t in range(2):
                pair_ref[var, g, hh * Q_BLOCK:(hh + 1) * Q_BLOCK, t * Q_BLOCK:(t + 1) * Q_BLOCK] = tiles[r0 + t]
    rows = B_GQA * SUBLANES
    row = lax.broadcasted_iota(I32, (rows, 2 * PAGE_SIZE), 0)
    col = lax.broadcasted_iota(I32, (rows, 2 * PAGE_SIZE), 1)
    t = row % SUBLANES
    d = jnp.where(col < PAGE_SIZE, t + PAGE_SIZE - col, t - (col - PAGE_SIZE))
    bucket = _t5_bucket(d)
    far = jnp.full((rows, LANES), REL_BUCKETS - 1, I32)
    rowc = lax.broadcasted_iota(I32, (rows, LANES), 0)
    for g in range(B_KV_HEADS):
        acc = jnp.zeros((rows, 2 * PAGE_SIZE), F32)
        accc = jnp.zeros((rows, LANES), F32)
        for hh in range(B_GQA):
            h = g * B_GQA + hh
            acc = jnp.where(row // SUBLANES == hh, _bias_lookup(bucket, rb_ref, h), acc)
            accc = jnp.where(rowc // SUBLANES == hh, _bias_lookup(far, rb_ref, h), accc)
        samp_ref[g] = acc
        sconst_ref[g] = accc


N_PAIR_VARIANTS = 4


def _bias_tables(rel_bias):
    rows = B_GQA * SUBLANES
    return pl.pallas_call(
        _bias_kernel,
        in_specs=[pl.BlockSpec(memory_space=pltpu.SMEM)],
        out_shape=[
            jax.ShapeDtypeStruct((N_PAIR_VARIANTS, B_KV_HEADS, B_GQA * Q_BLOCK, KEY_CHUNK), F32),
            jax.ShapeDtypeStruct((B_KV_HEADS, rows, 2 * PAGE_SIZE), F32),
            jax.ShapeDtypeStruct((B_KV_HEADS, rows, LANES), F32),
        ],
        name="bias_tables",
    )(rel_bias)


def _key_to_f32(key):
    bits = jnp.where(key < 0, key ^ I32(-2 ** 31), ~key)
    return lax.bitcast_convert_type(bits, F32)


def _kth_largest(count_ge, shape, k):
    def body(it, prefix):
        bit = 31 - it
        cand = prefix | lax.shift_left(I32(1), bit)
        cnt = count_ge(_key_to_f32(cand))
        return jnp.where(cnt >= k, cand, prefix)

    prefix = lax.fori_loop(0, 32, body, jnp.zeros(shape, I32))
    thr = _key_to_f32(prefix)
    return jnp.where((prefix & I32(-2 ** 23)) == 0, -jnp.inf, thr)


NO_INDEX_BOUND = 2 ** 30
TIE_REPAIR_ROUNDS = 2


def _topk_select(fold_sum, fold_min, shape, k, index_bits):
    thr = _kth_largest(lambda t: fold_sum(lambda sc, kp: sc >= t), shape, k)
    c_ge = fold_sum(lambda sc, kp: sc >= thr)
    no_bound = jnp.full(shape, NO_INDEX_BOUND, I32)

    def repair(_):
        v = thr
        strict = jnp.zeros(shape, F32)

        def in_set(sc, v, strict):
            return (sc > v) | ((sc == v) & (strict < 0.5))

        for _ in range(TIE_REPAIR_ROUNDS):
            c_set = fold_sum(lambda sc, kp: in_set(sc, v, strict))
            vmin = fold_min(lambda sc, kp: jnp.where(in_set(sc, v, strict), sc, jnp.inf))
            c_gt = fold_sum(lambda sc, kp: sc > vmin)
            drop = (c_set > k) & (c_gt >= k)
            v = jnp.where(drop, vmin, v)
            strict = jnp.where(drop, 1.0, strict)
        c_set = fold_sum(lambda sc, kp: in_set(sc, v, strict))
        vt = fold_min(lambda sc, kp: jnp.where(in_set(sc, v, strict), sc, jnp.inf))
        need = k - fold_sum(lambda sc, kp: sc > vt)

        def idx_body(it, m):
            cand = m | lax.shift_left(I32(1), index_bits - 1 - it)
            below = fold_sum(lambda sc, kp: (sc == vt) & (kp < cand))
            return jnp.where(below < need, cand, m)

        m = lax.fori_loop(0, index_bits, idx_body, jnp.zeros(shape, I32))
        tie = c_set > k
        v_out = jnp.where(tie, vt, v)
        m_out = jnp.where(tie, m, jnp.where(strict > 0.5, -1, no_bound))
        return v_out, m_out

    return lax.cond(jnp.max(c_ge) > k, repair, lambda _: (thr, no_bound), 0)


CHUNKS_PER_TRIP = 4


def _chunk_loop(n, body, init):
    def trip(j, c):
        for k in range(CHUNKS_PER_TRIP):
            c = body(CHUNKS_PER_TRIP * j + k, c)
        return c

    whole = n // CHUNKS_PER_TRIP
    carry = lax.fori_loop(0, whole, trip, init)
    return lax.fori_loop(whole * CHUNKS_PER_TRIP, n, body, carry)


def _dsa_prompt_kernel(q_ref, kv_ref, qia_ref, qib_ref, tailk_ref, tailq_ref, bias_ref, *rest, topk, n_cast):
    cast_in, rest = rest[:n_cast], rest[n_cast:]
    o_ref, rest = rest[0], rest[1:]
    cast_out, rest = rest[:n_cast], rest[n_cast:]
    kil_scr, kir_scr, sc_scr, lg_scr, m_scr, l_scr, acc_scr = rest
    i = pl.program_id(1)

    for src, dst in zip(cast_in, cast_out):
        dst[...] = src[...].astype(BF16)

    @pl.when(i == 0)
    def _():
        ki = tailk_ref[:, :IDX_DIM].astype(BF16)
        z = jnp.zeros_like(ki)
        kil_scr[...] = jnp.concatenate([ki, z], axis=-1)
        kir_scr[...] = jnp.concatenate([z, ki], axis=-1)

    wi = tailq_ref[:, IDX_DIM:IDX_DIM + IDX_HEADS] * (IDX_DIM ** -0.5)
    nk = (i + 2) // 2
    qpos = i * Q_BLOCK + lax.broadcasted_iota(I32, (Q_BLOCK, 1), 0)
    kloc = lax.broadcasted_iota(I32, (1, KEY_CHUNK), 1)
    heads_per_ref = qia_ref.shape[1] // IDX_DIM

    pairs = []
    for hp in range(IDX_HEADS // 2):
        ref = qia_ref if 2 * hp < heads_per_ref else qib_ref
        base = (2 * hp) % heads_per_ref * IDX_DIM
        pairs.append(ref[:, base:base + 2 * IDX_DIM])
    qstack = jnp.concatenate(pairs, axis=0)

    def score_chunk(c, carry):
        off = pl.multiple_of(c * KEY_CHUNK, KEY_CHUNK)
        acc = jnp.zeros((Q_BLOCK, KEY_CHUNK), F32)
        for side, k_scr in ((0, kil_scr), (1, kir_scr)):
            s = _dot_nt(qstack, k_scr[pl.ds(off, KEY_CHUNK), :])
            for hp in range(IDX_HEADS // 2):
                h = 2 * hp + side
                acc = acc + wi[:, h:h + 1] * jnp.maximum(s[hp * Q_BLOCK:(hp + 1) * Q_BLOCK], 0.0)
        kpos = c * KEY_CHUNK + kloc
        sc_scr[c] = jnp.where(kpos <= qpos, acc, -jnp.inf)
        return carry

    _chunk_loop(nk, score_chunk, 0)

    def fold_sum(pred):
        def body(c, cnt):
            m = jnp.where(pred(sc_scr[c], c * KEY_CHUNK + kloc), 1.0, 0.0)
            return cnt + m[:, :LANES] + m[:, LANES:]
        cnt = _chunk_loop(nk, body, jnp.zeros((Q_BLOCK, LANES), F32))
        return jnp.sum(cnt, axis=-1, keepdims=True)

    def fold_min(val):
        def body(c, acc):
            x = val(sc_scr[c], c * KEY_CHUNK + kloc)
            return jnp.minimum(acc, jnp.minimum(x[:, :LANES], x[:, LANES:]))
        acc = lax.fori_loop(0, nk, body, jnp.full((Q_BLOCK, LANES), jnp.inf, F32))
        return jnp.min(acc, axis=-1, keepdims=True)

    index_bits = (kv_ref.shape[0] - 1).bit_length()
    sel_v, sel_m = _topk_select(fold_sum, fold_min, (Q_BLOCK, 1), topk, index_bits)

    def mask_chunk(c, carry):
        kpos = c * KEY_CHUNK + kloc
        sc = sc_scr[c]
        chosen = (sc > sel_v) | ((sc == sel_v) & (kpos <= sel_m))
        sc_scr[c] = jnp.where(chosen & (kpos <= qpos), 0.0, -jnp.inf)
        return carry

    lax.fori_loop(0, nk, mask_chunk, 0)

    scale = B_HEAD_DIM ** -0.5
    rows = B_GQA * Q_BLOCK
    qgs = [jnp.concatenate(
        [q_ref[:, (g * B_GQA + hh) * B_HEAD_DIM:(g * B_GQA + hh + 1) * B_HEAD_DIM] for hh in range(B_GQA)],
        axis=0) for g in range(B_KV_HEADS)]
    m_scr[...] = jnp.full(m_scr.shape, -jnp.inf, F32)
    l_scr[...] = jnp.zeros(l_scr.shape, F32)
    acc_scr[...] = jnp.zeros(acc_scr.shape, F32)

    def logits_chunk(c, carry):
        off = pl.multiple_of(c * KEY_CHUNK, KEY_CHUNK)
        var = jnp.clip(2 * c - i, -(N_PAIR_VARIANTS - 1), 0) + (N_PAIR_VARIANTS - 1)
        sel = sc_scr[c]
        sel = jnp.concatenate([sel] * B_GQA, axis=0)
        for g in range(B_KV_HEADS):
            kc = kv_ref[pl.ds(off, KEY_CHUNK), g * B_HEAD_DIM:(g + 1) * B_HEAD_DIM]
            lg = _dot_nt(qgs[g], kc) * scale + bias_ref[var, g] + sel
            lg_scr[c, g] = lg
            m_scr[g] = jnp.maximum(m_scr[g], jnp.maximum(lg[:, :LANES], lg[:, LANES:]))
        return carry

    _chunk_loop(nk, logits_chunk, 0)
    for g in range(B_KV_HEADS):
        m = jnp.max(m_scr[g], axis=-1, keepdims=True)
        m_scr[g] = jnp.broadcast_to(m, (rows, LANES))

    def pv_chunk(c, carry):
        off = pl.multiple_of(c * KEY_CHUNK, KEY_CHUNK)
        for g in range(B_KV_HEADS):
            vc = kv_ref[pl.ds(off, KEY_CHUNK), KV_WIDTH + g * B_HEAD_DIM:KV_WIDTH + (g + 1) * B_HEAD_DIM]
            mb = m_scr[g]
            p = jnp.exp(lg_scr[c, g] - jnp.concatenate([mb, mb], axis=-1))
            l_scr[g] += p[:, :LANES] + p[:, LANES:]
            acc_scr[g] += jnp.dot(p.astype(BF16), vc, preferred_element_type=F32)
        return carry

    _chunk_loop(nk, pv_chunk, 0)
    for g in range(B_KV_HEADS):
        o = acc_scr[g] / jnp.sum(l_scr[g], axis=-1, keepdims=True)
        for hh in range(B_GQA):
            h = g * B_GQA + hh
            o_ref[:, h * B_HEAD_DIM:(h + 1) * B_HEAD_DIM] = o[hh * Q_BLOCK:(hh + 1) * Q_BLOCK].astype(BF16)


def _dsa_prompt(p, tail, bias_pairs, cast_weights, *, batch, seq):
    topk = min(TOPK_MAX, seq // 4)
    nblk = seq // Q_BLOCK
    qi_w = PROJ_TN
    steps = batch * nblk
    cast_specs, cast_shapes = [], []
    for w in cast_weights:
        ne, r, c = w.shape
        parts = steps // ne
        assert steps == ne * parts and r % (parts * 16) == 0
        spec = pl.BlockSpec((None, r // parts, c),
                            lambda b, i, parts=parts: ((b * nblk + i) // parts, (b * nblk + i) % parts, 0))
        cast_specs.append(spec)
        cast_shapes.append(jax.ShapeDtypeStruct(w.shape, BF16))
    outs = pl.pallas_call(
        functools.partial(_dsa_prompt_kernel, topk=topk, n_cast=len(cast_weights)),
        grid=(batch, nblk),
        in_specs=[
            pl.BlockSpec((Q_BLOCK, B_WIDTH), lambda b, i: (b * nblk + i, COL_Q // B_WIDTH)),
            pl.BlockSpec((seq, 2 * KV_WIDTH), lambda b, i: (b, COL_K // (2 * KV_WIDTH))),
            pl.BlockSpec((Q_BLOCK, qi_w), lambda b, i: (b * nblk + i, COL_QI // qi_w)),
            pl.BlockSpec((Q_BLOCK, qi_w), lambda b, i: (b * nblk + i, COL_QI // qi_w + 1)),
            pl.BlockSpec((seq, LANES), lambda b, i: (b, 0)),
            pl.BlockSpec((Q_BLOCK, LANES), lambda b, i: (b * nblk + i, 0)),
            pl.BlockSpec(bias_pairs.shape, lambda b, i: (0, 0, 0, 0)),
        ] + cast_specs,
        out_specs=[pl.BlockSpec((Q_BLOCK, B_WIDTH), lambda b, i: (b * nblk + i, 0))] + cast_specs,
        out_shape=[jax.ShapeDtypeStruct((batch * seq, B_WIDTH), BF16)] + cast_shapes,
        scratch_shapes=[
            pltpu.VMEM((seq, 2 * IDX_DIM), BF16),
            pltpu.VMEM((seq, 2 * IDX_DIM), BF16),
            pltpu.VMEM((seq // KEY_CHUNK, Q_BLOCK, KEY_CHUNK), F32),
            pltpu.VMEM((seq // KEY_CHUNK, B_KV_HEADS, B_GQA * Q_BLOCK, KEY_CHUNK), F32),
            pltpu.VMEM((B_KV_HEADS, B_GQA * Q_BLOCK, LANES), F32),
            pltpu.VMEM((B_KV_HEADS, B_GQA * Q_BLOCK, LANES), F32),
            pltpu.VMEM((B_KV_HEADS, B_GQA * Q_BLOCK, B_HEAD_DIM), F32),
        ],
        compiler_params=_cparams(("arbitrary", "arbitrary")),
        name="dsa_prompt",
    )(p, p, p, p, tail, tail, bias_pairs, *cast_weights)
    return outs[0], tuple(outs[1:])


PAGES_PER_STEP = 32
SAMPLE_SUB_PAGES = 32
SCORE_KEYS_PER_DOT = 2048
FOLD_WAYS = 8


def _fetch_pages(pt_ref, sources, bufs, sem, step, n_steps_total, steps_per_sample, pages_per_step):
    slot = step % 2

    def copies(n, slot_, page_of):
        sample = n // steps_per_sample
        first = (n % steps_per_sample) * pages_per_step
        for r in range(pages_per_step):
            page = page_of(sample, first + r)
            for a, (src, buf) in enumerate(zip(sources, bufs)):
                yield pltpu.make_async_copy(src.at[page], buf.at[slot_, r], sem.at[a, slot_])

    def table(sample, idx):
        return pt_ref[sample, idx]

    @pl.when(step == 0)
    def _():
        for cp in copies(step, slot, table):
            cp.start()

    @pl.when(step + 1 < n_steps_total)
    def _():
        for cp in copies(step + 1, 1 - slot, table):
            cp.start()

    for cp in copies(step, slot, lambda sample, idx: 0):
        cp.wait()
    return slot


def _dsa_sample_score_kernel(pt_ref, idx_hbm, qi_ref, wi_ref, kin_ref, sc_ref, scn_ref, selv_ref, selm_ref,
                             page_buf, sem, *, topk, dec_seq, n_pages, n_samples):
    slot = _fetch_pages(pt_ref, [idx_hbm], [page_buf], sem, pl.program_id(0), n_samples, 1, n_pages)
    pages = [page_buf.at[slot, r] for r in range(n_pages)]
    past = n_pages * PAGE_SIZE
    pages_per_dot = SCORE_KEYS_PER_DOT // PAGE_SIZE
    qi = qi_ref[...]
    wi = wi_ref[...] * (IDX_DIM ** -0.5)

    def scores(keys_t):
        r = jnp.maximum(jnp.dot(qi, keys_t, preferred_element_type=F32), 0.0) * wi
        acc = r[0:SUBLANES]
        for j in range(1, IDX_HEADS * dec_seq // SUBLANES):
            acc = acc + r[j * SUBLANES:(j + 1) * SUBLANES]
        shift = dec_seq
        while shift < SUBLANES:
            acc = acc + pltpu.roll(acc, shift, axis=0)
            shift *= 2
        return acc

    for d in range(n_pages // pages_per_dot):
        keys_t = jnp.concatenate(
            [pg[...].astype(BF16) for pg in pages[d * pages_per_dot:(d + 1) * pages_per_dot]], axis=1)
        sc_ref[:, d * SCORE_KEYS_PER_DOT:(d + 1) * SCORE_KEYS_PER_DOT] = scores(keys_t)

    row = lax.broadcasted_iota(I32, (SUBLANES, LANES), 0)
    col = lax.broadcasted_iota(I32, (SUBLANES, LANES), 1)
    scn = jnp.where((col <= row % dec_seq) & (col < dec_seq), scores(kin_ref[...]), -jnp.inf)
    scn_ref[...] = scn

    def pieces():
        yield scn, past + col[0:1]
        for w in range(past // LANES):
            yield sc_ref[:, w * LANES:(w + 1) * LANES], w * LANES + col[0:1]

    def fold(term, combine):
        parts = [None] * FOLD_WAYS
        for n, (sc, kp) in enumerate(pieces()):
            x = term(sc, kp)
            parts[n % FOLD_WAYS] = x if parts[n % FOLD_WAYS] is None else combine(parts[n % FOLD_WAYS], x)
        parts = [p for p in parts if p is not None]
        while len(parts) > 1:
            parts = [combine(parts[j], parts[j + 1]) if j + 1 < len(parts) else parts[j]
                     for j in range(0, len(parts), 2)]
        return parts[0]

    def fold_sum(pred):
        cnt = fold(lambda sc, kp: jnp.where(pred(sc, kp), 1.0, 0.0), jnp.add)
        return jnp.sum(cnt, axis=-1, keepdims=True)

    def fold_min(val):
        return jnp.min(fold(val, jnp.minimum), axis=-1, keepdims=True)

    index_bits = (past + LANES - 1).bit_length()
    sel_v, sel_m = _topk_select(fold_sum, fold_min, (SUBLANES, 1), topk, index_bits)
    selv_ref[...] = jnp.broadcast_to(sel_v, (SUBLANES, LANES))
    selm_ref[...] = jnp.broadcast_to(sel_m, (SUBLANES, LANES))


def _dsa_sample_attend_kernel(pt_ref, ck_hbm, cv_hbm, q_ref, sc_ref, scn_ref, selv_ref, selm_ref, kvn_ref,
                              sbias_ref, sconst_ref, o_ref, kbuf, vbuf, sem, m_scr, l_scr, acc_scr,
                              *, dec_seq, n_steps, n_samples):
    s = pl.program_id(1)
    slot = _fetch_pages(pt_ref, [ck_hbm, cv_hbm], [kbuf, vbuf], sem, pl.program_id(0) * n_steps + s,
                        n_samples * n_steps, n_steps, PAGES_PER_STEP)
    kpages = [kbuf.at[slot, r] for r in range(PAGES_PER_STEP)]
    vpages = [vbuf.at[slot, r] for r in range(PAGES_PER_STEP)]
    chunk = PAGES_PER_STEP * PAGE_SIZE
    rows = B_GQA * SUBLANES
    scale = B_HEAD_DIM ** -0.5

    @pl.when(s == 0)
    def _():
        m_scr[...] = jnp.full(m_scr.shape, NEG_INIT, F32)
        l_scr[...] = jnp.zeros(l_scr.shape, F32)
        acc_scr[...] = jnp.zeros(acc_scr.shape, F32)

    sel_v = selv_ref[:, 0:1]
    sel_m = selm_ref[:, 0:1]

    def chosen(sc, kpos):
        return jnp.where((sc > sel_v) | ((sc == sel_v) & (kpos <= sel_m)), 0.0, -jnp.inf)

    def head_rows(pgs, g):
        return jnp.concatenate(
            [pg[pl.ds(g, PAGE_SIZE, stride=B_KV_HEADS), :].astype(BF16) for pg in pgs], axis=0)

    kcol = lax.broadcasted_iota(I32, (SUBLANES, chunk), 1)
    sel = chosen(sc_ref[...], s * chunk + kcol)
    sel = jnp.concatenate([sel] * B_GQA, axis=0)
    is_last = s == n_steps - 1
    ncol = lax.broadcasted_iota(I32, (SUBLANES, LANES), 1)
    seln = chosen(scn_ref[...], n_steps * chunk + ncol)
    seln = jnp.where(is_last & (scn_ref[...] > -jnp.inf), seln, -jnp.inf)
    seln = jnp.concatenate([seln] * B_GQA, axis=0)

    def partial(lg, v):
        m = jnp.maximum(jnp.max(lg, axis=-1, keepdims=True), NEG_INIT)
        p = jnp.exp(lg - m)
        return m, jnp.sum(p, axis=-1, keepdims=True), jnp.dot(p.astype(BF16), v, preferred_element_type=F32)

    sub = SAMPLE_SUB_PAGES * PAGE_SIZE
    nsub = PAGES_PER_STEP // SAMPLE_SUB_PAGES
    lane_sub = lax.broadcasted_iota(I32, (rows, sub), 1)
    for g in range(B_KV_HEADS):
        qg = q_ref[g]
        far = sconst_ref[g, :, 0:1]
        near = jnp.concatenate([jnp.zeros((rows, sub - PAGE_SIZE), F32), sbias_ref[g, :, :PAGE_SIZE]], axis=-1)
        parts = [(m_scr[g], l_scr[g], acc_scr[g])]
        for j in range(nsub):
            pj = slice(j * SAMPLE_SUB_PAGES, (j + 1) * SAMPLE_SUB_PAGES)
            lg = _dot_nt(qg, head_rows(kpages[pj], g)) * scale + sel[:, j * sub:(j + 1) * sub]
            if j == nsub - 1:
                lg = lg + jnp.where(is_last & (lane_sub >= sub - PAGE_SIZE), near, far)
            else:
                lg = lg + far
            parts.append(partial(lg, head_rows(vpages[pj], g)))
        kn = kvn_ref[:, g * B_HEAD_DIM:(g + 1) * B_HEAD_DIM]
        vn = kvn_ref[:, KV_WIDTH + g * B_HEAD_DIM:KV_WIDTH + (g + 1) * B_HEAD_DIM]
        parts.append(partial(_dot_nt(qg, kn) * scale + sbias_ref[g, :, PAGE_SIZE:] + seln, vn))
        m_new = parts[0][0]
        for m, _, _ in parts[1:]:
            m_new = jnp.maximum(m_new, m)
        l_new = jnp.zeros((rows, 1), F32)
        acc_new = jnp.zeros((rows, B_HEAD_DIM), F32)
        for m, l, acc in parts:
            w = jnp.exp(m - m_new)
            l_new = l_new + w * l
            acc_new = acc_new + w * acc
        m_scr[g] = m_new
        l_scr[g] = l_new
        acc_scr[g] = acc_new

    @pl.when(is_last)
    def _():
        for g in range(B_KV_HEADS):
            o_ref[g] = (acc_scr[g] / l_scr[g]).astype(BF16)


def _dsa_sample(p_s, tail_s, k_s, v_s, cache_k_l, cache_v_l, cache_idx_l, page_table, sbias, sconst,
                *, dec_batch, dec_seq):
    n_pages = page_table.shape[1]
    past = n_pages * PAGE_SIZE
    topk = min(TOPK_MAX, (past + dec_seq) // 4)
    n_steps = n_pages // PAGES_PER_STEP
    chunk = PAGES_PER_STEP * PAGE_SIZE
    rows = B_GQA * SUBLANES
    n_pool = cache_k_l.shape[0]

    assert SUBLANES % dec_seq == 0
    qi = p_s[:, COL_QI:COL_MAIN].reshape(dec_batch, dec_seq, IDX_HEADS, IDX_DIM).transpose(0, 2, 1, 3)
    qi = qi.reshape(dec_batch, IDX_HEADS * dec_seq, IDX_DIM)
    wi = tail_s[:, IDX_DIM:IDX_DIM + IDX_HEADS].reshape(dec_batch, dec_seq, IDX_HEADS).transpose(0, 2, 1)
    wi = wi.reshape(dec_batch, IDX_HEADS * dec_seq, 1)
    ki_new = tail_s[:, :IDX_DIM].astype(BF16).reshape(dec_batch, dec_seq, IDX_DIM)
    ki_new = jnp.pad(ki_new, ((0, 0), (0, LANES - dec_seq), (0, 0))).transpose(0, 2, 1)
    q = p_s[:, COL_Q:COL_K].reshape(dec_batch, dec_seq, B_KV_HEADS, B_GQA, B_HEAD_DIM)
    q = jnp.pad(q.transpose(0, 2, 3, 1, 4), ((0, 0), (0, 0), (0, 0), (0, SUBLANES - dec_seq), (0, 0)))
    q = q.reshape(dec_batch, B_KV_HEADS, rows, B_HEAD_DIM)
    kv_new = jnp.concatenate([k_s, v_s], axis=-1).astype(BF16).reshape(dec_batch, dec_seq, 2 * KV_WIDTH)
    kv_new = jnp.pad(kv_new, ((0, 0), (0, LANES - dec_seq), (0, 0)))

    cache_idx_t = jnp.swapaxes(cache_idx_l, 1, 2)
    stat_spec = pl.BlockSpec((None, SUBLANES, LANES), lambda b, pt: (b, 0, 0))
    sc, scn, selv, selm = pl.pallas_call(
        functools.partial(_dsa_sample_score_kernel, topk=topk, dec_seq=dec_seq, n_pages=n_pages,
                          n_samples=dec_batch),
        grid_spec=pltpu.PrefetchScalarGridSpec(
            num_scalar_prefetch=1,
            grid=(dec_batch,),
            in_specs=[
                pl.BlockSpec(memory_space=pl.ANY),
                pl.BlockSpec((None, IDX_HEADS * dec_seq, IDX_DIM), lambda b, pt: (b, 0, 0)),
                pl.BlockSpec((None, IDX_HEADS * dec_seq, 1), lambda b, pt: (b, 0, 0)),
                pl.BlockSpec((None, IDX_DIM, LANES), lambda b, pt: (b, 0, 0)),
            ],
            out_specs=[pl.BlockSpec((None, SUBLANES, past), lambda b, pt: (b, 0, 0)),
                       stat_spec, stat_spec, stat_spec],
            scratch_shapes=[
                pltpu.VMEM((2, n_pages, IDX_DIM, PAGE_SIZE), F32),
                pltpu.SemaphoreType.DMA((1, 2)),
            ],
        ),
        out_shape=[
            jax.ShapeDtypeStruct((dec_batch, SUBLANES, past), F32),
            jax.ShapeDtypeStruct((dec_batch, SUBLANES, LANES), F32),
            jax.ShapeDtypeStruct((dec_batch, SUBLANES, LANES), F32),
            jax.ShapeDtypeStruct((dec_batch, SUBLANES, LANES), I32),
        ],
        compiler_params=_cparams(("arbitrary",)),
        name="dsa_sample_score",
    )(page_table, cache_idx_t, qi, wi, ki_new)

    ck = cache_k_l.reshape(n_pool, PAGE_SIZE * B_KV_HEADS, B_HEAD_DIM)
    cv = cache_v_l.reshape(n_pool, PAGE_SIZE * B_KV_HEADS, B_HEAD_DIM)
    kv_buf = pltpu.VMEM((2, PAGES_PER_STEP, PAGE_SIZE * B_KV_HEADS, B_HEAD_DIM), F32)
    o = pl.pallas_call(
        functools.partial(_dsa_sample_attend_kernel, dec_seq=dec_seq, n_steps=n_steps, n_samples=dec_batch),
        grid_spec=pltpu.PrefetchScalarGridSpec(
            num_scalar_prefetch=1,
            grid=(dec_batch, n_steps),
            in_specs=[pl.BlockSpec(memory_space=pl.ANY), pl.BlockSpec(memory_space=pl.ANY)] + [
                pl.BlockSpec((None, B_KV_HEADS, rows, B_HEAD_DIM), lambda b, s, pt: (b, 0, 0, 0)),
                pl.BlockSpec((None, SUBLANES, chunk), lambda b, s, pt: (b, 0, s)),
                pl.BlockSpec((None, SUBLANES, LANES), lambda b, s, pt: (b, 0, 0)),
                pl.BlockSpec((None, SUBLANES, LANES), lambda b, s, pt: (b, 0, 0)),
                pl.BlockSpec((None, SUBLANES, LANES), lambda b, s, pt: (b, 0, 0)),
                pl.BlockSpec((None, LANES, 2 * KV_WIDTH), lambda b, s, pt: (b, 0, 0)),
                pl.BlockSpec(sbias.shape, lambda b, s, pt: (0, 0, 0)),
                pl.BlockSpec(sconst.shape, lambda b, s, pt: (0, 0, 0)),
            ],
            out_specs=pl.BlockSpec((None, B_KV_HEADS, rows, B_HEAD_DIM), lambda b, s, pt: (b, 0, 0, 0)),
            scratch_shapes=[
                kv_buf,
                kv_buf,
                pltpu.SemaphoreType.DMA((2, 2)),
                pltpu.VMEM((B_KV_HEADS, rows, 1), F32),
                pltpu.VMEM((B_KV_HEADS, rows, 1), F32),
                pltpu.VMEM((B_KV_HEADS, rows, B_HEAD_DIM), F32),
            ],
        ),
        out_shape=jax.ShapeDtypeStruct((dec_batch, B_KV_HEADS, rows, B_HEAD_DIM), BF16),
        compiler_params=_cparams(("arbitrary", "arbitrary")),
        name="dsa_sample_attend",
    )(page_table, ck, cv, q, sc, scn, selv, selm, kv_new, sbias, sconst)
    o = o.reshape(dec_batch, B_KV_HEADS, B_GQA, SUBLANES, B_HEAD_DIM)[:, :, :, :dec_seq].transpose(0, 3, 1, 2, 4)
    return o.reshape(dec_batch * dec_seq, B_WIDTH)


def _mixout_kernel(u_ref, vn_ref, b_ref, ws_ref, ab_ref, x_ref, w_ref, o_ref, mix_scr, *, tm):
    j = pl.program_id(1)

    @pl.when(j == 0)
    def _():
        row = lax.broadcasted_iota(I32, (A_CHUNK, A_CHUNK), 0)
        col = lax.broadcasted_iota(I32, (A_CHUNK, A_CHUNK), 1)
        for g in range(A_GROUPS):
            wsg = jnp.where(row >= col, ws_ref[g], 0.0).astype(BF16)
            bg = ab_ref[:, g:g + 1]
            cs = slice(g * A_CH, (g + 1) * A_CH)
            for r in range(tm // A_CHUNK):
                rs = slice(r * A_CHUNK, (r + 1) * A_CHUNK)
                sm = jnp.dot(wsg, vn_ref[rs, cs], preferred_element_type=F32) + bg
                mix_scr[rs, cs] = (u_ref[rs, cs].astype(F32) * sm).astype(BF16)
        mix_scr[:, A_WIDTH:] = b_ref[...]

    o_ref[...] = x_ref[...] + jnp.dot(mix_scr[...], w_ref[...].astype(BF16), preferred_element_type=F32)


def _mixout(p, b_out, ws_eff, ab_eff, x2d, w_out_l, *, tm, tn=1024):
    n = x2d.shape[0]
    return pl.pallas_call(
        functools.partial(_mixout_kernel, tm=tm),
        grid=(n // tm, D_MODEL // tn),
        in_specs=[
            pl.BlockSpec((tm, A_WIDTH), lambda i, j: (i, COL_U // A_WIDTH)),
            pl.BlockSpec((tm, A_WIDTH), lambda i, j: (i, COL_V // A_WIDTH)),
            pl.BlockSpec((tm, B_WIDTH), lambda i, j: (i, 0)),
            pl.BlockSpec((A_GROUPS, A_CHUNK, A_CHUNK), lambda i, j: (0, 0, 0)),
            pl.BlockSpec((A_CHUNK, A_GROUPS), lambda i, j: (0, 0)),
            pl.BlockSpec((tm, tn), lambda i, j: (i, j)),
            pl.BlockSpec((A_WIDTH + B_WIDTH, tn), lambda i, j: (0, j)),
        ],
        out_specs=pl.BlockSpec((tm, tn), lambda i, j: (i, j)),
        out_shape=jax.ShapeDtypeStruct((n, D_MODEL), F32),
        scratch_shapes=[pltpu.VMEM((tm, A_WIDTH + B_WIDTH), BF16)],
        compiler_params=_cparams(("arbitrary", "arbitrary")),
        name="mixout",
    )(p, p, b_out, ws_eff, ab_eff, x2d, w_out_l)


def _memkv_kernel(x_ref, wk_ref, wv_ref, kn_ref, mk_ref, mv_ref):
    x = x_ref[...].astype(BF16)
    tm = x.shape[0]
    mk = _head_norm(jnp.dot(x, wk_ref[...].astype(BF16), preferred_element_type=F32), kn_ref[...])
    mv = jnp.dot(x, wv_ref[...].astype(BF16), preferred_element_type=F32)
    for hd in range(MEM_HEADS):
        cs = slice(hd * MEM_HEAD_DIM, (hd + 1) * MEM_HEAD_DIM)
        mk_ref[pl.ds(hd, tm, stride=MEM_HEADS), :] = mk[:, cs]
        mv_ref[pl.ds(hd, tm, stride=MEM_HEADS), :] = mv[:, cs]


def _memkv(mem2d, wk, wv, knorm, *, tm=256):
    n = mem2d.shape[0]
    kn = jnp.tile(knorm, MEM_HEADS).reshape(1, MEM_WIDTH)
    return pl.pallas_call(
        _memkv_kernel,
        grid=(n // tm,),
        in_specs=[
            pl.BlockSpec((tm, D_MODEL), lambda i: (i, 0)),
            pl.BlockSpec((D_MODEL, MEM_WIDTH), lambda i: (0, 0)),
            pl.BlockSpec((D_MODEL, MEM_WIDTH), lambda i: (0, 0)),
            pl.BlockSpec((1, MEM_WIDTH), lambda i: (0, 0)),
        ],
        out_specs=[pl.BlockSpec((tm * MEM_HEADS, MEM_HEAD_DIM), lambda i: (i, 0))] * 2,
        out_shape=[jax.ShapeDtypeStruct((n * MEM_HEADS, MEM_HEAD_DIM), F32)] * 2,
        compiler_params=_cparams(("arbitrary",)),
        name="memkv",
    )(mem2d, wk, wv, kn)


def _memattn_kernel(h_ref, g_ref, wq_ref, qn_ref, mk_ref, mv_ref, wo_ref, gf_ref, wc_ref, bc_ref,
                    o_ref, n_ref, c_ref, *, rows_per_mem, mem_len):
    h = h_ref[...]
    n = _row_norm(h, g_ref[...]).astype(BF16)
    q = _head_norm(jnp.dot(n, wq_ref[...].astype(BF16), preferred_element_type=F32), qn_ref[...]).astype(BF16)
    tm, m = h.shape[0], mk_ref.shape[0] // MEM_HEADS
    if rows_per_mem is not None:
        row = lax.broadcasted_iota(I32, (tm, m), 0)
        col = lax.broadcasted_iota(I32, (tm, m), 1)
        mask = jnp.where(row // rows_per_mem == col // mem_len, 0.0, -jnp.inf)
    outs = []
    for hd in range(MEM_HEADS):
        cs = slice(hd * MEM_HEAD_DIM, (hd + 1) * MEM_HEAD_DIM)
        mk = mk_ref[pl.ds(hd, m, stride=MEM_HEADS), :].astype(BF16)
        mv = mv_ref[pl.ds(hd, m, stride=MEM_HEADS), :].astype(BF16)
        lg = _dot_nt(q[:, cs], mk) * (MEM_HEAD_DIM ** -0.5)
        if rows_per_mem is not None:
            lg = lg + mask
        e = jnp.exp(lg - jnp.max(lg, axis=-1, keepdims=True))
        l = jnp.sum(e, axis=-1, keepdims=True)
        outs.append(jnp.dot(e.astype(BF16), mv, preferred_element_type=F32) / l)
    o = jnp.concatenate(outs, axis=-1).astype(BF16)
    h2 = h + jnp.dot(o, wo_ref[...].astype(BF16), preferred_element_type=F32)
    o_ref[...] = h2
    n_ref[...], c_ref[...] = _route(h2, gf_ref[...], wc_ref[...], bc_ref[...])


def _memattn(h2d, gain, wq, qnorm, mk, mv, wo, gain_ffn, w_cat, b_cat, *, tm, mem_rows, tiles_per_mem,
             rows_per_mem, mem_len):
    n = h2d.shape[0]
    qn = jnp.tile(qnorm, MEM_HEADS).reshape(1, MEM_WIDTH)
    return pl.pallas_call(
        functools.partial(_memattn_kernel, rows_per_mem=rows_per_mem, mem_len=mem_len),
        grid=(n // tm,),
        in_specs=[
            pl.BlockSpec((tm, D_MODEL), lambda i: (i, 0)),
            pl.BlockSpec((1, D_MODEL), lambda i: (0, 0)),
            pl.BlockSpec((D_MODEL, MEM_WIDTH), lambda i: (0, 0)),
            pl.BlockSpec((1, MEM_WIDTH), lambda i: (0, 0)),
            pl.BlockSpec((mem_rows * MEM_HEADS, MEM_HEAD_DIM), lambda i: (i // tiles_per_mem, 0)),
            pl.BlockSpec((mem_rows * MEM_HEADS, MEM_HEAD_DIM), lambda i: (i // tiles_per_mem, 0)),
            pl.BlockSpec((MEM_WIDTH, D_MODEL), lambda i: (0, 0)),
            pl.BlockSpec((1, D_MODEL), lambda i: (0, 0)),
            pl.BlockSpec((D_MODEL, 2 * LANES), lambda i: (0, 0)),
            pl.BlockSpec((1, 2 * LANES), lambda i: (0, 0)),
        ],
        out_specs=[pl.BlockSpec((tm, D_MODEL), lambda i: (i, 0)), pl.BlockSpec((tm, D_MODEL), lambda i: (i, 0)),
                   pl.BlockSpec((tm, LANES), lambda i: (i, 0))],
        out_shape=[jax.ShapeDtypeStruct((n, D_MODEL), F32), jax.ShapeDtypeStruct((n, D_MODEL), BF16),
                   jax.ShapeDtypeStruct((n, LANES), F32)],
        compiler_params=_cparams(("arbitrary",)),
        name="memattn",
    )(h2d, gain.reshape(1, D_MODEL), wq, qn, mk, mv, wo, gain_ffn.reshape(1, D_MODEL), w_cat, b_cat)


def _route(h, gain, w_cat, b_cat):
    n = _row_norm(h, gain).astype(BF16)
    lg = jnp.dot(n, w_cat.astype(BF16), preferred_element_type=F32) + b_cat
    gl, el = lg[:, :LANES], lg[:, LANES:]
    lane = lax.broadcasted_iota(I32, gl.shape, 1)
    lanef = lane.astype(F32)

    def first_max_lane(v, vmax):
        return jnp.min(jnp.where(v == vmax, lanef, float(LANES)), axis=-1, keepdims=True).astype(I32)

    gl = jnp.where(lane < N_GROUPS, gl, -jnp.inf)
    ge = jnp.exp(gl - jnp.max(gl, axis=-1, keepdims=True))
    gp = ge / jnp.sum(ge, axis=-1, keepdims=True)
    g_gate = jnp.max(gp, axis=-1, keepdims=True)
    g_sel = first_max_lane(gp, g_gate)
    in_grp = (lane // EXPERTS_PER_GROUP == g_sel) & (lane < N_EXPERTS)
    el = jnp.where(in_grp, el, -jnp.inf)
    ee = jnp.exp(el - jnp.max(el, axis=-1, keepdims=True))
    ep = jnp.where(in_grp, ee / jnp.sum(ee, axis=-1, keepdims=True), -jnp.inf)
    w1 = jnp.max(ep, axis=-1, keepdims=True)
    i1 = first_max_lane(ep, w1)
    ep2 = jnp.where(lane == i1, -jnp.inf, ep)
    w2 = jnp.max(ep2, axis=-1, keepdims=True)
    i2 = first_max_lane(ep2, w2)
    tot = w1 + w2
    comb = jnp.where(lane == i1, w1 / tot * g_gate, jnp.where(lane == i2, w2 / tot * g_gate, 0.0))
    return n, jnp.where(lane == GSEL_LANE, g_sel.astype(F32), comb)


def _cast_kernel(x_ref, o_ref):
    o_ref[...] = x_ref[...].astype(BF16)


def _cast_bf16(w):
    ne, r, c = w.shape
    return pl.pallas_call(
        _cast_kernel,
        grid=(ne,),
        in_specs=[pl.BlockSpec((None, r, c), lambda e: (e, 0, 0))],
        out_specs=pl.BlockSpec((None, r, c), lambda e: (e, 0, 0)),
        out_shape=jax.ShapeDtypeStruct(w.shape, BF16),
        compiler_params=_cparams(("arbitrary",)),
        name="cast_bf16",
    )(w)


GSEL_LANE = N_EXPERTS
MOE_ROWS = 128
MOE_MAIN_SLACK = 0
MOE_FF_SPLIT = 1


def _moe_routed_kernel(n_ref, c_ref, h_hbm, wg_ref, wu_ref, wd_ref, o_ref,
                       tri_scr, key_scr, xc_scr, yc_scr, cw_scr, nch_ref, sem, *, tm):
    i, g, e, f = (pl.program_id(k) for k in range(4))
    main_rows = tm // N_GROUPS + MOE_MAIN_SLACK
    first_e = (e == 0) & (f == 0)
    last_e = (e == EXPERTS_PER_GROUP - 1) & (f == MOE_FF_SPLIT - 1)

    def onehot_rows(start, rows):
        want = (start + lax.broadcasted_iota(I32, (rows, 1), 0)).astype(F32)
        return jnp.where(key_scr[pl.ds(g, 1), :] == want, 1.0, 0.0).astype(BF16)

    def tail_start(c):
        return pl.multiple_of(main_rows + c * MOE_ROWS, math.gcd(main_rows, MOE_ROWS))

    @pl.when((i == 0) & (g == 0) & first_e)
    def _():
        r = lax.broadcasted_iota(I32, (tm, tm), 0)
        c = lax.broadcasted_iota(I32, (tm, tm), 1)
        tri_scr[...] = jnp.where(r < c, 1.0, 0.0).astype(BF16)

    def residual_copy():
        return pltpu.make_async_copy(h_hbm.at[pl.ds(pl.multiple_of(i * tm, tm), tm)], o_ref, sem)

    @pl.when((g == 0) & first_e)
    def _():
        residual_copy().start()
        lane = lax.broadcasted_iota(I32, (tm, LANES), 1).astype(F32)
        onehot = jnp.where(lane == c_ref[:, GSEL_LANE:GSEL_LANE + 1], 1.0, 0.0).astype(BF16)
        er = lax.broadcasted_iota(I32, (SUBLANES, LANES), 0)
        ec = lax.broadcasted_iota(I32, (SUBLANES, LANES), 1)
        eye = jnp.where(er == ec, 1.0, 0.0).astype(BF16)
        mask_t = _dot_nt(eye, onehot)
        rank_t = jnp.dot(mask_t.astype(BF16), tri_scr[...], preferred_element_type=F32)
        key_scr[...] = jnp.where(mask_t > 0.5, rank_t, -1.0)
        for gg in range(N_GROUPS):
            cnt = jnp.sum(mask_t[gg:gg + 1, :], axis=-1, keepdims=True)
            extra = jnp.ceil(jnp.maximum(cnt - main_rows, 0.0) * (1.0 / MOE_ROWS))
            nch_ref[gg] = jnp.sum(extra).astype(I32)

    ntail = nch_ref[g]

    @pl.when(first_e)
    def _():
        comb = c_ref[...]
        hi = comb.astype(BF16)
        r1 = comb - hi.astype(F32)
        mid = r1.astype(BF16)
        lo = (r1 - mid.astype(F32)).astype(BF16)

        def compact(start, rows):
            s = onehot_rows(start, rows)
            xc_scr[pl.ds(start, rows), :] = jnp.dot(s, n_ref[...], preferred_element_type=F32).astype(BF16)
            cw_scr[pl.ds(start, rows), :] = (jnp.dot(s, hi, preferred_element_type=F32)
                                             + jnp.dot(s, mid, preferred_element_type=F32)
                                             + jnp.dot(s, lo, preferred_element_type=F32))
            yc_scr[pl.ds(start, rows), :] = jnp.zeros((rows, D_MODEL), F32)

        compact(0, main_rows)
        lax.fori_loop(0, ntail, lambda c, carry: (compact(tail_start(c), MOE_ROWS), carry)[1], 0)

    wg = wg_ref[...].astype(BF16)
    wu = wu_ref[...].astype(BF16)
    wd = wd_ref[...].astype(BF16)
    ex = g * EXPERTS_PER_GROUP + e

    def expert(start, rows):
        x = xc_scr[pl.ds(start, rows), :]
        a = jnp.dot(x, wg, preferred_element_type=F32)
        u = jnp.dot(x, wu, preferred_element_type=F32)
        lane = lax.broadcasted_iota(I32, (rows, LANES), 1)
        cw = jnp.sum(jnp.where(lane == ex, cw_scr[pl.ds(start, rows), :], 0.0), axis=-1, keepdims=True)
        hid = (a * jax.nn.sigmoid(a) * u * cw).astype(BF16)
        yc_scr[pl.ds(start, rows), :] += jnp.dot(hid, wd, preferred_element_type=F32)

    expert(0, main_rows)
    lax.fori_loop(0, ntail, lambda c, carry: (expert(tail_start(c), MOE_ROWS), carry)[1], 0)

    @pl.when((g == 0) & last_e)
    def _():
        residual_copy().wait()

    @pl.when(last_e)
    def _():
        def scatter(start, rows):
            s = onehot_rows(start, rows)
            y = yc_scr[pl.ds(start, rows), :].astype(BF16)
            o_ref[...] += lax.dot_general(s, y, (((0,), (0,)), ((), ())), preferred_element_type=F32)

        scatter(0, main_rows)
        lax.fori_loop(0, ntail, lambda c, carry: (scatter(tail_start(c), MOE_ROWS), carry)[1], 0)


def _moe_routed(n3, comb, h2d, w_gate_l, w_up_l, w_down_l, *, tm):
    n = h2d.shape[0]
    ff = EXPERT_FF // MOE_FF_SPLIT

    def wmap(i, g, e, f):
        return (g * EXPERTS_PER_GROUP + e, 0, f)

    return pl.pallas_call(
        functools.partial(_moe_routed_kernel, tm=tm),
        grid=(n // tm, N_GROUPS, EXPERTS_PER_GROUP, MOE_FF_SPLIT),
        in_specs=[
            pl.BlockSpec((tm, D_MODEL), lambda i, g, e, f: (i, 0), pipeline_mode=pl.Buffered(1)),
            pl.BlockSpec((tm, LANES), lambda i, g, e, f: (i, 0), pipeline_mode=pl.Buffered(1)),
            pl.BlockSpec(memory_space=pl.ANY),
            pl.BlockSpec((None, D_MODEL, ff), wmap),
            pl.BlockSpec((None, D_MODEL, ff), wmap),
            pl.BlockSpec((None, ff, D_MODEL), lambda i, g, e, f: (g * EXPERTS_PER_GROUP + e, f, 0)),
        ],
        out_specs=pl.BlockSpec((tm, D_MODEL), lambda i, g, e, f: (i, 0)),
        out_shape=jax.ShapeDtypeStruct((n, D_MODEL), F32),
        scratch_shapes=[
            pltpu.VMEM((tm, tm), BF16),
            pltpu.VMEM((SUBLANES, tm), F32),
            pltpu.VMEM((tm + MOE_ROWS, D_MODEL), BF16),
            pltpu.VMEM((tm + MOE_ROWS, D_MODEL), F32),
            pltpu.VMEM((tm + MOE_ROWS, LANES), F32),
            pltpu.SMEM((N_GROUPS,), I32),
            pltpu.SemaphoreType.DMA(()),
        ],
        compiler_params=pltpu.CompilerParams(
            dimension_semantics=("arbitrary",) * 4, vmem_limit_bytes=MOE_VMEM_LIMIT),
        name="moe_routed",
    )(n3, comb, h2d, w_gate_l, w_up_l, w_down_l)


def _moe_kernel(n_ref, c_ref, h_ref, wg_ref, wu_ref, wd_ref, o_ref):
    e = pl.program_id(1)

    @pl.when(e == 0)
    def _():
        o_ref[...] = h_ref[...]

    x = n_ref[...]
    a = jnp.dot(x, wg_ref[...].astype(BF16), preferred_element_type=F32)
    u = jnp.dot(x, wu_ref[...].astype(BF16), preferred_element_type=F32)
    lane = lax.broadcasted_iota(I32, c_ref.shape, 1)
    c = jnp.sum(jnp.where(lane == e, c_ref[...], 0.0), axis=-1, keepdims=True)
    hid = (a * jax.nn.sigmoid(a) * u * c).astype(BF16)
    o_ref[...] += jnp.dot(hid, wd_ref[...].astype(BF16), preferred_element_type=F32)


def _moe(n3, comb, h2d, w_gate_l, w_up_l, w_down_l, *, tm):
    n = h2d.shape[0]
    return pl.pallas_call(
        _moe_kernel,
        grid=(n // tm, N_EXPERTS),
        in_specs=[
            pl.BlockSpec((tm, D_MODEL), lambda i, e: (i, 0)),
            pl.BlockSpec((tm, LANES), lambda i, e: (i, 0)),
            pl.BlockSpec((tm, D_MODEL), lambda i, e: (i, 0)),
            pl.BlockSpec((None, D_MODEL, EXPERT_FF), lambda i, e: (e, 0, 0)),
            pl.BlockSpec((None, D_MODEL, EXPERT_FF), lambda i, e: (e, 0, 0)),
            pl.BlockSpec((None, EXPERT_FF, D_MODEL), lambda i, e: (e, 0, 0)),
        ],
        out_specs=pl.BlockSpec((tm, D_MODEL), lambda i, e: (i, 0)),
        out_shape=jax.ShapeDtypeStruct((n, D_MODEL), F32),
        compiler_params=_cparams(("arbitrary", "arbitrary")),
        name="moe",
    )(n3, comb, h2d, w_gate_l, w_up_l, w_down_l)


def _post_mixer(p, b_out, ws_eff, ab_eff, x2d, mk, mv, lw, *, tm_mix, tm_mem, mem_rows, tiles_per_mem,
                rows_per_mem, mem_len, tm_moe):
    h1 = _mixout(p, b_out, ws_eff, ab_eff, x2d, lw["w_out"], tm=tm_mix)
    h2, n3, comb = _memattn(h1, lw["norm_mem"], lw["mem_wq"], lw["mem_qnorm"], mk, mv, lw["mem_wo"],
                            lw["norm_ffn"], lw["w_cat"], lw["b_cat"], tm=tm_mem, mem_rows=mem_rows,
                            tiles_per_mem=tiles_per_mem, rows_per_mem=rows_per_mem, mem_len=mem_len)
    moe = _moe_routed if tm_moe > N_GROUPS * MOE_ROWS else _moe
    return moe(n3, comb, h2, lw["w_gate"], lw["w_up"], lw["w_down"], tm=tm_moe)


def kernel(x_prompt, x_sample, cache_k, cache_v, cache_idx_k, cache_mem_k, cache_mem_v, page_table,
           mem_prompt, norm_mix, w_in, a_vnorm, a_ws, a_b, b_qnorm, b_knorm, rel_bias, w_out,
           norm_mem, mem_wq, mem_wk, mem_wv, mem_qnorm, mem_knorm, mem_wo, norm_ffn,
           w_group, b_group, w_router, b_router, w_gate, w_up, w_down):
    bp, t, d = x_prompt.shape
    bd, s, _ = x_sample.shape
    depth = w_in.shape[0]
    mem_len = mem_prompt.shape[1]
    assert d == D_MODEL and t % KEY_CHUNK == 0 and s <= 8 and (bd * s) % LANES == 0

    bias_pairs, sbias, sconst = _bias_tables(rel_bias)
    hp = x_prompt.reshape(bp * t, d)
    hs = x_sample.reshape(bd * s, d)
    outs = [[] for _ in range(9)]
    for l in range(depth):
        w_cat = jnp.zeros((d, 2 * LANES), F32)
        w_cat = w_cat.at[:, :N_GROUPS].set(w_group[l]).at[:, LANES:LANES + N_EXPERTS].set(w_router[l])
        b_cat = jnp.zeros((1, 2 * LANES), F32)
        b_cat = b_cat.at[0, :N_GROUPS].set(b_group[l]).at[0, LANES:LANES + N_EXPERTS].set(b_router[l])
        lw = dict(w_out=w_out[l], norm_mem=norm_mem[l], mem_wq=mem_wq[l], mem_qnorm=mem_qnorm[l],
                  mem_wo=mem_wo[l], norm_ffn=norm_ffn[l], w_cat=w_cat, b_cat=b_cat)

        p, k, v, tail = _proj(hp, norm_mix[l], w_in[l], a_vnorm[l], b_qnorm[l], b_knorm[l],
                              tm=1024, want_vn=False)
        later_weights = (w_gate[l], w_up[l], w_down[l], w_out[l][None])
        fuse_cast = (bp * (t // Q_BLOCK)) % N_EXPERTS == 0
        b_out, casted = _dsa_prompt(p, tail, bias_pairs, later_weights if fuse_cast else (), batch=bp, seq=t)
        if not fuse_cast:
            casted = tuple(_cast_bf16(w) for w in later_weights)
        lw.update(w_gate=casted[0], w_up=casted[1], w_down=casted[2], w_out=casted[3][0])
        mk, mv = _memkv(mem_prompt.reshape(bp * mem_len, d), mem_wk[l], mem_wv[l], mem_knorm[l])
        hp = _post_mixer(p, b_out, a_ws[l], a_b[l].T, hp, mk, mv, lw, tm_mix=1024, tm_mem=512,
                         mem_rows=mem_len, tiles_per_mem=t // 512, rows_per_mem=None, mem_len=mem_len,
                         tm_moe=1024)
        outs[0].append(k.reshape(bp, t, B_KV_HEADS, B_HEAD_DIM))
        outs[1].append(v.reshape(bp, t, B_KV_HEADS, B_HEAD_DIM))
        outs[2].append(tail[:, :IDX_DIM].reshape(bp, t, IDX_DIM))
        outs[3].append(mk.reshape(bp, mem_len, MEM_HEADS, MEM_HEAD_DIM))
        outs[4].append(mv.reshape(bp, mem_len, MEM_HEADS, MEM_HEAD_DIM))

        ns = bd * s
        p_s, k_s, v_s, tail_s, vn_s = _proj(hs, norm_mix[l], w_in[l], a_vnorm[l], b_qnorm[l], b_knorm[l],
                                            tm=ns, want_vn=True)
        b_out_s = _dsa_sample(p_s, tail_s, k_s.reshape(ns, KV_WIDTH), v_s.reshape(ns, KV_WIDTH),
                              cache_k[l], cache_v[l], cache_idx_k[l], page_table,
                              sbias, sconst, dec_batch=bd, dec_seq=s)
        ws_s = jnp.einsum("ab,gts->gatbs", jnp.eye(ns // s, dtype=F32), a_ws[l][:, :s, :s])
        ws_s = ws_s.reshape(A_GROUPS, ns, ns)
        ab_s = jnp.tile(a_b[l][:, :s].T, (ns // s, 1))
        mem_tile = 32
        hs = _post_mixer(p_s, b_out_s, ws_s, ab_s, hs,
                         cache_mem_k[l].reshape(bd * mem_len * MEM_HEADS, MEM_HEAD_DIM),
                         cache_mem_v[l].reshape(bd * mem_len * MEM_HEADS, MEM_HEAD_DIM), lw,
                         tm_mix=ns, tm_mem=mem_tile, mem_rows=mem_tile // s * mem_len, tiles_per_mem=1,
                         rows_per_mem=s, mem_len=mem_len, tm_moe=ns)
        outs[5].append(k_s.reshape(bd, s, B_KV_HEADS, B_HEAD_DIM))
        outs[6].append(v_s.reshape(bd, s, B_KV_HEADS, B_HEAD_DIM))
        outs[7].append(tail_s[:, :IDX_DIM].reshape(bd, s, IDX_DIM))
        outs[8].append(vn_s.reshape(bd, s, A_GROUPS, A_CH))
    return (hp.reshape(bp, t, d), hs.reshape(bd, s, d)) + tuple(jnp.stack(o) for o in outs)
```

```python
import functools
import math

import jax
import jax.numpy as jnp
from jax import lax
from jax.experimental import pallas as pl
from jax.experimental.pallas import tpu as pltpu

BF16 = jnp.bfloat16
F32 = jnp.float32
I32 = jnp.int32

D_MODEL = 2048
A_GROUPS = 8
A_CH = 128
A_WIDTH = A_GROUPS * A_CH
A_CHUNK = 128
B_HEADS = 8
B_HEAD_DIM = 128
B_KV_HEADS = 2
B_GQA = B_HEADS // B_KV_HEADS
B_WIDTH = B_HEADS * B_HEAD_DIM
KV_WIDTH = B_KV_HEADS * B_HEAD_DIM
IDX_HEADS = 16
IDX_DIM = 64
TOPK_MAX = 256
Q_BLOCK = 128
REL_BUCKETS = 32
REL_MAX_DIST = 128
MEM_HEADS = 4
MEM_HEAD_DIM = 128
MEM_WIDTH = MEM_HEADS * MEM_HEAD_DIM
N_GROUPS = 4
EXPERTS_PER_GROUP = 4
N_EXPERTS = N_GROUPS * EXPERTS_PER_GROUP
EXPERT_FF = 512
PAGE_SIZE = 128
EPS = 1e-6

COL_U = 0
COL_V = COL_U + A_WIDTH
COL_Q = COL_V + A_WIDTH
COL_K = COL_Q + B_WIDTH
COL_VV = COL_K + KV_WIDTH
COL_QI = COL_VV + KV_WIDTH
COL_MAIN = COL_QI + IDX_HEADS * IDX_DIM
TAIL_COLS = IDX_DIM + IDX_HEADS
LANES = 128
SUBLANES = 8
PROJ_TN = 512
PROJ_ROW_CHUNK = 256
KEY_CHUNK = 256
NEG_INIT = -1e30
VMEM_LIMIT = 56 * 1024 * 1024
MOE_VMEM_LIMIT = 60 * 1024 * 1024


def _cparams(sem):
    return pltpu.CompilerParams(dimension_semantics=sem, vmem_limit_bytes=VMEM_LIMIT)


def _gelu(x):
    return 0.5 * x * (1.0 + lax.erf(x * (2.0 ** -0.5)))


def _head_norm(a, gain, width=LANES):
    outs = []
    for c in range(a.shape[1] // width):
        blk = a[:, c * width:(c + 1) * width]
        ms = jnp.mean(blk * blk, axis=-1, keepdims=True)
        outs.append(blk * lax.rsqrt(ms + EPS) * gain[:, c * width:(c + 1) * width])
    return outs[0] if len(outs) == 1 else jnp.concatenate(outs, axis=-1)


def _row_norm(x, gain):
    ms = jnp.mean(x * x, axis=-1, keepdims=True)
    return x * lax.rsqrt(ms + EPS) * gain


def _dot_nt(a, b):
    return lax.dot_general(a, b, (((1,), (1,)), ((), ())), preferred_element_type=F32)


def _proj_kernel(x_ref, g_ref, w_ref, wt_ref, avn_ref, qn_ref, kn_ref, *rest, want_vn):
    if want_vn:
        p_ref, k_ref, v_ref, tail_ref, vn_ref, xn_scr = rest
    else:
        p_ref, k_ref, v_ref, tail_ref, xn_scr = rest
        vn_ref = None
    j = pl.program_id(1)
    ju = COL_V // PROJ_TN
    jv = COL_Q // PROJ_TN
    jq = COL_K // PROJ_TN
    jkv = COL_QI // PROJ_TN

    @pl.when(j == 0)
    def _():
        xn = _row_norm(x_ref[...], g_ref[...]).astype(BF16)
        xn_scr[...] = xn
        wrow = lax.broadcasted_iota(I32, wt_ref.shape, 0)
        wt = jnp.where(wrow < TAIL_COLS, wt_ref[...], 0.0).astype(BF16)
        t = _dot_nt(xn, wt)
        lane = lax.broadcasted_iota(I32, t.shape, 1)
        tail_ref[...] = jnp.where(lane >= IDX_DIM, t * (IDX_HEADS ** -0.5), t)

    tm = xn_scr.shape[0]
    rc = min(tm, PROJ_ROW_CHUNK)

    def row_chunks(epilogue):
        w = w_ref[...].astype(BF16)
        for r in range(tm // rc):
            rs = slice(r * rc, (r + 1) * rc)
            epilogue(_dot_nt(xn_scr[rs, :], w), rs)

    @pl.when(j < ju)
    def _():
        def epi(acc, rs):
            p_ref[rs, :] = _gelu(acc).astype(BF16)
        row_chunks(epi)

    @pl.when((j >= ju) & (j < jv))
    def _():
        def epi(acc, rs):
            vn = _head_norm(_gelu(acc), avn_ref[...])
            p_ref[rs, :] = vn.astype(BF16)
            if vn_ref is not None:
                vn_ref[rs, :] = vn
        row_chunks(epi)

    @pl.when((j >= jv) & (j < jq))
    def _():
        def epi(acc, rs):
            p_ref[rs, :] = _head_norm(acc, qn_ref[...]).astype(BF16)
        row_chunks(epi)

    @pl.when(j == jq)
    def _():
        def epi(acc, rs):
            k = _head_norm(acc[:, :KV_WIDTH], kn_ref[...])
            v = acc[:, KV_WIDTH:]
            for g in range(B_KV_HEADS):
                dst = pl.ds(rs.start * B_KV_HEADS + g, rs.stop - rs.start, stride=B_KV_HEADS)
                k_ref[dst, :] = k[:, g * B_HEAD_DIM:(g + 1) * B_HEAD_DIM]
                v_ref[dst, :] = v[:, g * B_HEAD_DIM:(g + 1) * B_HEAD_DIM]
            p_ref[rs, :] = jnp.concatenate([k, v], axis=-1).astype(BF16)
        row_chunks(epi)

    @pl.when(j >= jkv)
    def _():
        def epi(acc, rs):
            p_ref[rs, :] = acc.astype(BF16)
        row_chunks(epi)


def _proj(x2d, gain, w_in_l, a_vnorm_l, b_qnorm_l, b_knorm_l, *, tm, want_vn):
    n = x2d.shape[0]
    nj = COL_MAIN // PROJ_TN
    w_in_t = jnp.swapaxes(w_in_l, 0, 1)
    avn = a_vnorm_l.reshape(1, A_WIDTH)
    qn = jnp.tile(b_qnorm_l, PROJ_TN // B_HEAD_DIM).reshape(1, PROJ_TN)
    kn = jnp.tile(b_knorm_l, B_KV_HEADS).reshape(1, KV_WIDTH)
    ju = COL_V // PROJ_TN
    nv = A_WIDTH // PROJ_TN

    def vmap_(i, j):
        return (0, jnp.clip(j - ju, 0, nv - 1))

    in_specs = [
        pl.BlockSpec((tm, D_MODEL), lambda i, j: (i, 0)),
        pl.BlockSpec((1, D_MODEL), lambda i, j: (0, 0)),
        pl.BlockSpec((PROJ_TN, D_MODEL), lambda i, j: (j, 0)),
        pl.BlockSpec((LANES, D_MODEL), lambda i, j: (COL_MAIN // LANES, 0)),
        pl.BlockSpec((1, PROJ_TN), vmap_),
        pl.BlockSpec((1, PROJ_TN), lambda i, j: (0, 0)),
        pl.BlockSpec((1, KV_WIDTH), lambda i, j: (0, 0)),
    ]
    out_shape = [
        jax.ShapeDtypeStruct((n, COL_MAIN), BF16),
        jax.ShapeDtypeStruct((n * B_KV_HEADS, B_HEAD_DIM), F32),
        jax.ShapeDtypeStruct((n * B_KV_HEADS, B_HEAD_DIM), F32),
        jax.ShapeDtypeStruct((n, LANES), F32),
    ]
    out_specs = [
        pl.BlockSpec((tm, PROJ_TN), lambda i, j: (i, j)),
        pl.BlockSpec((tm * B_KV_HEADS, B_HEAD_DIM), lambda i, j: (i, 0)),
        pl.BlockSpec((tm * B_KV_HEADS, B_HEAD_DIM), lambda i, j: (i, 0)),
        pl.BlockSpec((tm, LANES), lambda i, j: (i, 0)),
    ]
    if want_vn:
        out_shape.append(jax.ShapeDtypeStruct((n, A_WIDTH), F32))
        out_specs.append(pl.BlockSpec((tm, PROJ_TN), lambda i, j: (i, jnp.clip(j - ju, 0, nv - 1))))
    return pl.pallas_call(
        functools.partial(_proj_kernel, want_vn=want_vn),
        grid=(n // tm, nj),
        in_specs=in_specs,
        out_specs=out_specs,
        out_shape=out_shape,
        scratch_shapes=[pltpu.VMEM((tm, D_MODEL), BF16)],
        compiler_params=_cparams(("arbitrary", "arbitrary")),
        name="proj",
    )(x2d, gain.reshape(1, D_MODEL), w_in_t, w_in_t, avn, qn, kn)


def _t5_bucket(d):
    max_exact = REL_BUCKETS // 2
    d = jnp.maximum(d, 0)
    ratio = jnp.log(jnp.maximum(d, 1).astype(F32) / max_exact) / math.log(REL_MAX_DIST / max_exact)
    large = jnp.minimum(max_exact + jnp.floor(ratio * (REL_BUCKETS - max_exact)).astype(I32), REL_BUCKETS - 1)
    return jnp.where(d < max_exact, d, large)


def _bias_lookup(bucket, rb_ref, h):
    acc = jnp.zeros(bucket.shape, F32)
    for r in range(REL_BUCKETS):
        acc = jnp.where(bucket == r, rb_ref[r, h], acc)
    return acc


def _bias_kernel(rb_ref, pair_ref, samp_ref, sconst_ref):
    q = lax.broadcasted_iota(I32, (Q_BLOCK, Q_BLOCK), 0)
    kc = lax.broadcasted_iota(I32, (Q_BLOCK, Q_BLOCK), 1)
    nvar = pair_ref.shape[0]
    rels = range(-(nvar - 1), 2)
    buckets = {r: _t5_bucket(q - kc - r * Q_BLOCK) for r in rels}
    for h in range(B_HEADS):
        g, hh = divmod(h, B_GQA)
        tiles = {r: _bias_lookup(buckets[r], rb_ref, h) for r in rels}
        for var in range(nvar):
            r0 = var - (nvar - 1)
            for t in range(2):
                pair_ref[var, g, hh * Q_BLOCK:(hh + 1) * Q_BLOCK, t * Q_BLOCK:(t + 1) * Q_BLOCK] = tiles[r0 + t]
    rows = B_GQA * SUBLANES
    row = lax.broadcasted_iota(I32, (rows, 2 * PAGE_SIZE), 0)
    col = lax.broadcasted_iota(I32, (rows, 2 * PAGE_SIZE), 1)
    t = row % SUBLANES
    d = jnp.where(col < PAGE_SIZE, t + PAGE_SIZE - col, t - (col - PAGE_SIZE))
    bucket = _t5_bucket(d)
    far = jnp.full((rows, LANES), REL_BUCKETS - 1, I32)
    rowc = lax.broadcasted_iota(I32, (rows, LANES), 0)
    for g in range(B_KV_HEADS):
        acc = jnp.zeros((rows, 2 * PAGE_SIZE), F32)
        accc = jnp.zeros((rows, LANES), F32)
        for hh in range(B_GQA):
            h = g * B_GQA + hh
            acc = jnp.where(row // SUBLANES == hh, _bias_lookup(bucket, rb_ref, h), acc)
            accc = jnp.where(rowc // SUBLANES == hh, _bias_lookup(far, rb_ref, h), accc)
        samp_ref[g] = acc
        sconst_ref[g] = accc


N_PAIR_VARIANTS = 4


def _bias_tables(rel_bias):
    rows = B_GQA * SUBLANES
    return pl.pallas_call(
        _bias_kernel,
        in_specs=[pl.BlockSpec(memory_space=pltpu.SMEM)],
        out_shape=[
            jax.ShapeDtypeStruct((N_PAIR_VARIANTS, B_KV_HEADS, B_GQA * Q_BLOCK, KEY_CHUNK), F32),
            jax.ShapeDtypeStruct((B_KV_HEADS, rows, 2 * PAGE_SIZE), F32),
            jax.ShapeDtypeStruct((B_KV_HEADS, rows, LANES), F32),
        ],
        name="bias_tables",
    )(rel_bias)


def _key_to_f32(key):
    bits = jnp.where(key < 0, key ^ I32(-2 ** 31), ~key)
    return lax.bitcast_convert_type(bits, F32)


def _kth_largest(count_ge, shape, k):
    def body(it, prefix):
        bit = 31 - it
        cand = prefix | lax.shift_left(I32(1), bit)
        cnt = count_ge(_key_to_f32(cand))
        return jnp.where(cnt >= k, cand, prefix)

    prefix = lax.fori_loop(0, 32, body, jnp.zeros(shape, I32))
    thr = _key_to_f32(prefix)
    return jnp.where((prefix & I32(-2 ** 23)) == 0, -jnp.inf, thr)


NO_INDEX_BOUND = 2 ** 30
TIE_REPAIR_ROUNDS = 2


def _topk_select(fold_sum, fold_min, shape, k, index_bits):
    thr = _kth_largest(lambda t: fold_sum(lambda sc, kp: sc >= t), shape, k)
    c_ge = fold_sum(lambda sc, kp: sc >= thr)
    no_bound = jnp.full(shape, NO_INDEX_BOUND, I32)

    def repair(_):
        v = thr
        strict = jnp.zeros(shape, F32)

        def in_set(sc, v, strict):
            return (sc > v) | ((sc == v) & (strict < 0.5))

        for _ in range(TIE_REPAIR_ROUNDS):
            c_set = fold_sum(lambda sc, kp: in_set(sc, v, strict))
            vmin = fold_min(lambda sc, kp: jnp.where(in_set(sc, v, strict), sc, jnp.inf))
            c_gt = fold_sum(lambda sc, kp: sc > vmin)
            drop = (c_set > k) & (c_gt >= k)
            v = jnp.where(drop, vmin, v)
            strict = jnp.where(drop, 1.0, strict)
        c_set = fold_sum(lambda sc, kp: in_set(sc, v, strict))
        vt = fold_min(lambda sc, kp: jnp.where(in_set(sc, v, strict), sc, jnp.inf))
        need = k - fold_sum(lambda sc, kp: sc > vt)

        def idx_body(it, m):
            cand = m | lax.shift_left(I32(1), index_bits - 1 - it)
            below = fold_sum(lambda sc, kp: (sc == vt) & (kp < cand))
            return jnp.where(below < need, cand, m)

        m = lax.fori_loop(0, index_bits, idx_body, jnp.zeros(shape, I32))
        tie = c_set > k
        v_out = jnp.where(tie, vt, v)
        m_out = jnp.where(tie, m, jnp.where(strict > 0.5, -1, no_bound))
        return v_out, m_out

    return lax.cond(jnp.max(c_ge) > k, repair, lambda _: (thr, no_bound), 0)


CHUNKS_PER_TRIP = 4


def _chunk_loop(n, body, init):
    def trip(j, c):
        for k in range(CHUNKS_PER_TRIP):
            c = body(CHUNKS_PER_TRIP * j + k, c)
        return c

    whole = n // CHUNKS_PER_TRIP
    carry = lax.fori_loop(0, whole, trip, init)
    return lax.fori_loop(whole * CHUNKS_PER_TRIP, n, body, carry)


def _dsa_prompt_kernel(q_ref, kv_ref, qia_ref, qib_ref, tailk_ref, tailq_ref, bias_ref, *rest, topk, n_cast):
    cast_in, rest = rest[:n_cast], rest[n_cast:]
    o_ref, rest = rest[0], rest[1:]
    cast_out, rest = rest[:n_cast], rest[n_cast:]
    kil_scr, kir_scr, sc_scr, lg_scr, m_scr, l_scr, acc_scr = rest
    i = pl.program_id(1)

    for src, dst in zip(cast_in, cast_out):
        dst[...] = src[...].astype(BF16)

    @pl.when(i == 0)
    def _():
        ki = tailk_ref[:, :IDX_DIM].astype(BF16)
        z = jnp.zeros_like(ki)
        kil_scr[...] = jnp.concatenate([ki, z], axis=-1)
        kir_scr[...] = jnp.concatenate([z, ki], axis=-1)

    wi = tailq_ref[:, IDX_DIM:IDX_DIM + IDX_HEADS] * (IDX_DIM ** -0.5)
    nk = (i + 2) // 2
    qpos = i * Q_BLOCK + lax.broadcasted_iota(I32, (Q_BLOCK, 1), 0)
    kloc = lax.broadcasted_iota(I32, (1, KEY_CHUNK), 1)
    heads_per_ref = qia_ref.shape[1] // IDX_DIM

    pairs = []
    for hp in range(IDX_HEADS // 2):
        ref = qia_ref if 2 * hp < heads_per_ref else qib_ref
        base = (2 * hp) % heads_per_ref * IDX_DIM
        pairs.append(ref[:, base:base + 2 * IDX_DIM])
    qstack = jnp.concatenate(pairs, axis=0)

    def score_chunk(c, carry):
        off = pl.multiple_of(c * KEY_CHUNK, KEY_CHUNK)
        acc = jnp.zeros((Q_BLOCK, KEY_CHUNK), F32)
        for side, k_scr in ((0, kil_scr), (1, kir_scr)):
            s = _dot_nt(qstack, k_scr[pl.ds(off, KEY_CHUNK), :])
            for hp in range(IDX_HEADS // 2):
                h = 2 * hp + side
                acc = acc + wi[:, h:h + 1] * jnp.maximum(s[hp * Q_BLOCK:(hp + 1) * Q_BLOCK], 0.0)
        kpos = c * KEY_CHUNK + kloc
        sc_scr[c] = jnp.where(kpos <= qpos, acc, -jnp.inf)
        return carry

    _chunk_loop(nk, score_chunk, 0)

    def fold_sum(pred):
        def body(c, cnt):
            m = jnp.where(pred(sc_scr[c], c * KEY_CHUNK + kloc), 1.0, 0.0)
            return cnt + m[:, :LANES] + m[:, LANES:]
        cnt = _chunk_loop(nk, body, jnp.zeros((Q_BLOCK, LANES), F32))
        return jnp.sum(cnt, axis=-1, keepdims=True)

    def fold_min(val):
        def body(c, acc):
            x = val(sc_scr[c], c * KEY_CHUNK + kloc)
            return jnp.minimum(acc, jnp.minimum(x[:, :LANES], x[:, LANES:]))
        acc = lax.fori_loop(0, nk, body, jnp.full((Q_BLOCK, LANES), jnp.inf, F32))
        return jnp.min(acc, axis=-1, keepdims=True)

    index_bits = (kv_ref.shape[0] - 1).bit_length()
    sel_v, sel_m = _topk_select(fold_sum, fold_min, (Q_BLOCK, 1), topk, index_bits)

    def mask_chunk(c, carry):
        kpos = c * KEY_CHUNK + kloc
        sc = sc_scr[c]
        chosen = (sc > sel_v) | ((sc == sel_v) & (kpos <= sel_m))
        sc_scr[c] = jnp.where(chosen & (kpos <= qpos), 0.0, -jnp.inf)
        return carry

    lax.fori_loop(0, nk, mask_chunk, 0)

    scale = B_HEAD_DIM ** -0.5
    rows = B_GQA * Q_BLOCK
    qgs = [jnp.concatenate(
        [q_ref[:, (g * B_GQA + hh) * B_HEAD_DIM:(g * B_GQA + hh + 1) * B_HEAD_DIM] for hh in range(B_GQA)],
        axis=0) for g in range(B_KV_HEADS)]
    m_scr[...] = jnp.full(m_scr.shape, -jnp.inf, F32)
    l_scr[...] = jnp.zeros(l_scr.shape, F32)
    acc_scr[...] = jnp.zeros(acc_scr.shape, F32)

    def logits_chunk(c, carry):
        off = pl.multiple_of(c * KEY_CHUNK, KEY_CHUNK)
        var = jnp.clip(2 * c - i, -(N_PAIR_VARIANTS - 1), 0) + (N_PAIR_VARIANTS - 1)
        sel = sc_scr[c]
        sel = jnp.concatenate([sel] * B_GQA, axis=0)
        for g in range(B_KV_HEADS):
            kc = kv_ref[pl.ds(off, KEY_CHUNK), g * B_HEAD_DIM:(g + 1) * B_HEAD_DIM]
            lg = _dot_nt(qgs[g], kc) * scale + bias_ref[var, g] + sel
            lg_scr[c, g] = lg
            m_scr[g] = jnp.maximum(m_scr[g], jnp.maximum(lg[:, :LANES], lg[:, LANES:]))
        return carry

    _chunk_loop(nk, logits_chunk, 0)
    for g in range(B_KV_HEADS):
        m = jnp.max(m_scr[g], axis=-1, keepdims=True)
        m_scr[g] = jnp.broadcast_to(m, (rows, LANES))

    def pv_chunk(c, carry):
        off = pl.multiple_of(c * KEY_CHUNK, KEY_CHUNK)
        for g in range(B_KV_HEADS):
            vc = kv_ref[pl.ds(off, KEY_CHUNK), KV_WIDTH + g * B_HEAD_DIM:KV_WIDTH + (g + 1) * B_HEAD_DIM]
            mb = m_scr[g]
            p = jnp.exp(lg_scr[c, g] - jnp.concatenate([mb, mb], axis=-1))
            l_scr[g] += p[:, :LANES] + p[:, LANES:]
            acc_scr[g] += jnp.dot(p.astype(BF16), vc, preferred_element_type=F32)
        return carry

    _chunk_loop(nk, pv_chunk, 0)
    for g in range(B_KV_HEADS):
        o = acc_scr[g] / jnp.sum(l_scr[g], axis=-1, keepdims=True)
        for hh in range(B_GQA):
            h = g * B_GQA + hh
            o_ref[:, h * B_HEAD_DIM:(h + 1) * B_HEAD_DIM] = o[hh * Q_BLOCK:(hh + 1) * Q_BLOCK].astype(BF16)


def _dsa_prompt(p, tail, bias_pairs, cast_weights, *, batch, seq):
    topk = min(TOPK_MAX, seq // 4)
    nblk = seq // Q_BLOCK
    qi_w = PROJ_TN
    steps = batch * nblk
    cast_specs, cast_shapes = [], []
    for w in cast_weights:
        ne, r, c = w.shape
        parts = steps // ne
        assert steps == ne * parts and r % (parts * 16) == 0
        spec = pl.BlockSpec((None, r // parts, c),
                            lambda b, i, parts=parts: ((b * nblk + i) // parts, (b * nblk + i) % parts, 0))
        cast_specs.append(spec)
        cast_shapes.append(jax.ShapeDtypeStruct(w.shape, BF16))
    outs = pl.pallas_call(
        functools.partial(_dsa_prompt_kernel, topk=topk, n_cast=len(cast_weights)),
        grid=(batch, nblk),
        in_specs=[
            pl.BlockSpec((Q_BLOCK, B_WIDTH), lambda b, i: (b * nblk + i, COL_Q // B_WIDTH)),
            pl.BlockSpec((seq, 2 * KV_WIDTH), lambda b, i: (b, COL_K // (2 * KV_WIDTH))),
            pl.BlockSpec((Q_BLOCK, qi_w), lambda b, i: (b * nblk + i, COL_QI // qi_w)),
            pl.BlockSpec((Q_BLOCK, qi_w), lambda b, i: (b * nblk + i, COL_QI // qi_w + 1)),
            pl.BlockSpec((seq, LANES), lambda b, i: (b, 0)),
            pl.BlockSpec((Q_BLOCK, LANES), lambda b, i: (b * nblk + i, 0)),
            pl.BlockSpec(bias_pairs.shape, lambda b, i: (0, 0, 0, 0)),
        ] + cast_specs,
        out_specs=[pl.BlockSpec((Q_BLOCK, B_WIDTH), lambda b, i: (b * nblk + i, 0))] + cast_specs,
        out_shape=[jax.ShapeDtypeStruct((batch * seq, B_WIDTH), BF16)] + cast_shapes,
        scratch_shapes=[
            pltpu.VMEM((seq, 2 * IDX_DIM), BF16),
            pltpu.VMEM((seq, 2 * IDX_DIM), BF16),
            pltpu.VMEM((seq // KEY_CHUNK, Q_BLOCK, KEY_CHUNK), F32),
            pltpu.VMEM((seq // KEY_CHUNK, B_KV_HEADS, B_GQA * Q_BLOCK, KEY_CHUNK), F32),
            pltpu.VMEM((B_KV_HEADS, B_GQA * Q_BLOCK, LANES), F32),
            pltpu.VMEM((B_KV_HEADS, B_GQA * Q_BLOCK, LANES), F32),
            pltpu.VMEM((B_KV_HEADS, B_GQA * Q_BLOCK, B_HEAD_DIM), F32),
        ],
        compiler_params=_cparams(("arbitrary", "arbitrary")),
        name="dsa_prompt",
    )(p, p, p, p, tail, tail, bias_pairs, *cast_weights)
    return outs[0], tuple(outs[1:])


PAGES_PER_STEP = 32
SAMPLE_SUB_PAGES = 32
SCORE_KEYS_PER_DOT = 2048
FOLD_VREGS = 16
SELECT_ROWS = 64


def _fetch_pages(pt_ref, sources, bufs, sem, step, n_steps_total, steps_per_sample, pages_per_step):
    slot = step % 2

    def copies(n, slot_, page_of):
        sample = n // steps_per_sample
        first = (n % steps_per_sample) * pages_per_step
        for r in range(pages_per_step):
            page = page_of(sample, first + r)
            for a, (src, buf) in enumerate(zip(sources, bufs)):
                yield pltpu.make_async_copy(src.at[page], buf.at[slot_, r], sem.at[a, slot_])

    def table(sample, idx):
        return pt_ref[sample, idx]

    @pl.when(step == 0)
    def _():
        for cp in copies(step, slot, table):
            cp.start()

    @pl.when(step + 1 < n_steps_total)
    def _():
        for cp in copies(step + 1, 1 - slot, table):
            cp.start()

    for cp in copies(step, slot, lambda sample, idx: 0):
        cp.wait()
    return slot


def _dsa_sample_score_kernel(pt_ref, idx_hbm, qi_ref, wi_ref, kin_ref, sc_ref, scn_ref,
                             page_buf, sem, *, dec_seq, n_pages, n_samples):
    slot = _fetch_pages(pt_ref, [idx_hbm], [page_buf], sem, pl.program_id(0), n_samples, 1, n_pages)
    pages = [page_buf.at[slot, r] for r in range(n_pages)]
    past = n_pages * PAGE_SIZE
    pages_per_dot = SCORE_KEYS_PER_DOT // PAGE_SIZE
    qi = qi_ref[...]
    wi = wi_ref[...] * (IDX_DIM ** -0.5)

    def scores(keys_t):
        r = jnp.maximum(jnp.dot(qi, keys_t, preferred_element_type=F32), 0.0) * wi
        acc = r[0:SUBLANES]
        for j in range(1, IDX_HEADS * dec_seq // SUBLANES):
            acc = acc + r[j * SUBLANES:(j + 1) * SUBLANES]
        shift = dec_seq
        while shift < SUBLANES:
            acc = acc + pltpu.roll(acc, shift, axis=0)
            shift *= 2
        return acc

    for d in range(n_pages // pages_per_dot):
        keys_t = jnp.concatenate(
            [pg[...].astype(BF16) for pg in pages[d * pages_per_dot:(d + 1) * pages_per_dot]], axis=1)
        sc_ref[:, d * SCORE_KEYS_PER_DOT:(d + 1) * SCORE_KEYS_PER_DOT] = scores(keys_t)

    row = lax.broadcasted_iota(I32, (SUBLANES, LANES), 0)
    col = lax.broadcasted_iota(I32, (SUBLANES, LANES), 1)
    scn_ref[...] = jnp.where((col <= row % dec_seq) & (col < dec_seq), scores(kin_ref[...]), -jnp.inf)


def _sample_select_kernel(sc_ref, scn_ref, selv_ref, selm_ref, *, topk):
    rows, past = sc_ref.shape
    col = lax.broadcasted_iota(I32, (1, LANES), 1)

    def pieces():
        yield scn_ref[...], past + col
        for w in range(past // LANES):
            yield sc_ref[:, w * LANES:(w + 1) * LANES], w * LANES + col

    ways = max(1, FOLD_VREGS // (rows // SUBLANES))

    def fold(term, combine):
        parts = [None] * ways
        for n, (sc, kp) in enumerate(pieces()):
            x = term(sc, kp)
            parts[n % ways] = x if parts[n % ways] is None else combine(parts[n % ways], x)
        parts = [p for p in parts if p is not None]
        while len(parts) > 1:
            parts = [combine(parts[j], parts[j + 1]) if j + 1 < len(parts) else parts[j]
                     for j in range(0, len(parts), 2)]
        return parts[0]

    def fold_sum(pred):
        cnt = fold(lambda sc, kp: jnp.where(pred(sc, kp), 1.0, 0.0), jnp.add)
        return jnp.sum(cnt, axis=-1, keepdims=True)

    def fold_min(val):
        return jnp.min(fold(val, jnp.minimum), axis=-1, keepdims=True)

    index_bits = (past + LANES - 1).bit_length()
    sel_v, sel_m = _topk_select(fold_sum, fold_min, (rows, 1), topk, index_bits)
    selv_ref[...] = jnp.broadcast_to(sel_v, (rows, LANES))
    selm_ref[...] = jnp.broadcast_to(sel_m, (rows, LANES))


def _dsa_sample_attend_kernel(pt_ref, ck_hbm, cv_hbm, q_ref, sc_ref, scn_ref, selv_ref, selm_ref, kvn_ref,
                              sbias_ref, sconst_ref, o_ref, kbuf, vbuf, sem, m_scr, l_scr, acc_scr,
                              *, dec_seq, n_steps, n_samples):
    s = pl.program_id(1)
    slot = _fetch_pages(pt_ref, [ck_hbm, cv_hbm], [kbuf, vbuf], sem, pl.program_id(0) * n_steps + s,
                        n_samples * n_steps, n_steps, PAGES_PER_STEP)
    kpages = [kbuf.at[slot, r] for r in range(PAGES_PER_STEP)]
    vpages = [vbuf.at[slot, r] for r in range(PAGES_PER_STEP)]
    chunk = PAGES_PER_STEP * PAGE_SIZE
    rows = B_GQA * SUBLANES
    scale = B_HEAD_DIM ** -0.5

    @pl.when(s == 0)
    def _():
        m_scr[...] = jnp.full(m_scr.shape, NEG_INIT, F32)
        l_scr[...] = jnp.zeros(l_scr.shape, F32)
        acc_scr[...] = jnp.zeros(acc_scr.shape, F32)

    sel_v = selv_ref[:, 0:1]
    sel_m = selm_ref[:, 0:1]

    def chosen(sc, kpos):
        return jnp.where((sc > sel_v) | ((sc == sel_v) & (kpos <= sel_m)), 0.0, -jnp.inf)

    def head_rows(pgs, g):
        return jnp.concatenate(
            [pg[pl.ds(g, PAGE_SIZE, stride=B_KV_HEADS), :].astype(BF16) for pg in pgs], axis=0)

    kcol = lax.broadcasted_iota(I32, (SUBLANES, chunk), 1)
    sel = chosen(sc_ref[...], s * chunk + kcol)
    sel = jnp.concatenate([sel] * B_GQA, axis=0)
    is_last = s == n_steps - 1
    ncol = lax.broadcasted_iota(I32, (SUBLANES, LANES), 1)
    seln = chosen(scn_ref[...], n_steps * chunk + ncol)
    seln = jnp.where(is_last & (scn_ref[...] > -jnp.inf), seln, -jnp.inf)
    seln = jnp.concatenate([seln] * B_GQA, axis=0)

    def partial(lg, v):
        m = jnp.maximum(jnp.max(lg, axis=-1, keepdims=True), NEG_INIT)
        p = jnp.exp(lg - m)
        return m, jnp.sum(p, axis=-1, keepdims=True), jnp.dot(p.astype(BF16), v, preferred_element_type=F32)

    sub = SAMPLE_SUB_PAGES * PAGE_SIZE
    nsub = PAGES_PER_STEP // SAMPLE_SUB_PAGES
    lane_sub = lax.broadcasted_iota(I32, (rows, sub), 1)
    for g in range(B_KV_HEADS):
        qg = q_ref[g]
        far = sconst_ref[g, :, 0:1]
        near = jnp.concatenate([jnp.zeros((rows, sub - PAGE_SIZE), F32), sbias_ref[g, :, :PAGE_SIZE]], axis=-1)
        parts = [(m_scr[g], l_scr[g], acc_scr[g])]
        for j in range(nsub):
            pj = slice(j * SAMPLE_SUB_PAGES, (j + 1) * SAMPLE_SUB_PAGES)
            lg = _dot_nt(qg, head_rows(kpages[pj], g)) * scale + sel[:, j * sub:(j + 1) * sub]
            if j == nsub - 1:
                lg = lg + jnp.where(is_last & (lane_sub >= sub - PAGE_SIZE), near, far)
            else:
                lg = lg + far
            parts.append(partial(lg, head_rows(vpages[pj], g)))
        kn = kvn_ref[:, g * B_HEAD_DIM:(g + 1) * B_HEAD_DIM]
        vn = kvn_ref[:, KV_WIDTH + g * B_HEAD_DIM:KV_WIDTH + (g + 1) * B_HEAD_DIM]
        parts.append(partial(_dot_nt(qg, kn) * scale + sbias_ref[g, :, PAGE_SIZE:] + seln, vn))
        m_new = parts[0][0]
        for m, _, _ in parts[1:]:
            m_new = jnp.maximum(m_new, m)
        l_new = jnp.zeros((rows, 1), F32)
        acc_new = jnp.zeros((rows, B_HEAD_DIM), F32)
        for m, l, acc in parts:
            w = jnp.exp(m - m_new)
            l_new = l_new + w * l
            acc_new = acc_new + w * acc
        m_scr[g] = m_new
        l_scr[g] = l_new
        acc_scr[g] = acc_new

    @pl.when(is_last)
    def _():
        for g in range(B_KV_HEADS):
            o_ref[g] = (acc_scr[g] / l_scr[g]).astype(BF16)


def _dsa_sample(p_s, tail_s, k_s, v_s, cache_k_l, cache_v_l, cache_idx_l, page_table, sbias, sconst,
                *, dec_batch, dec_seq):
    n_pages = page_table.shape[1]
    past = n_pages * PAGE_SIZE
    topk = min(TOPK_MAX, (past + dec_seq) // 4)
    n_steps = n_pages // PAGES_PER_STEP
    chunk = PAGES_PER_STEP * PAGE_SIZE
    rows = B_GQA * SUBLANES
    n_pool = cache_k_l.shape[0]

    assert SUBLANES % dec_seq == 0
    qi = p_s[:, COL_QI:COL_MAIN].reshape(dec_batch, dec_seq, IDX_HEADS, IDX_DIM).transpose(0, 2, 1, 3)
    qi = qi.reshape(dec_batch, IDX_HEADS * dec_seq, IDX_DIM)
    wi = tail_s[:, IDX_DIM:IDX_DIM + IDX_HEADS].reshape(dec_batch, dec_seq, IDX_HEADS).transpose(0, 2, 1)
    wi = wi.reshape(dec_batch, IDX_HEADS * dec_seq, 1)
    ki_new = tail_s[:, :IDX_DIM].astype(BF16).reshape(dec_batch, dec_seq, IDX_DIM)
    ki_new = jnp.pad(ki_new, ((0, 0), (0, LANES - dec_seq), (0, 0))).transpose(0, 2, 1)
    q = p_s[:, COL_Q:COL_K].reshape(dec_batch, dec_seq, B_KV_HEADS, B_GQA, B_HEAD_DIM)
    q = jnp.pad(q.transpose(0, 2, 3, 1, 4), ((0, 0), (0, 0), (0, 0), (0, SUBLANES - dec_seq), (0, 0)))
    q = q.reshape(dec_batch, B_KV_HEADS, rows, B_HEAD_DIM)
    kv_new = jnp.concatenate([k_s, v_s], axis=-1).astype(BF16).reshape(dec_batch, dec_seq, 2 * KV_WIDTH)
    kv_new = jnp.pad(kv_new, ((0, 0), (0, LANES - dec_seq), (0, 0)))

    cache_idx_t = jnp.swapaxes(cache_idx_l, 1, 2)
    stat_spec = pl.BlockSpec((None, SUBLANES, LANES), lambda b, pt: (b, 0, 0))
    sc, scn = pl.pallas_call(
        functools.partial(_dsa_sample_score_kernel, dec_seq=dec_seq, n_pages=n_pages, n_samples=dec_batch),
        grid_spec=pltpu.PrefetchScalarGridSpec(
            num_scalar_prefetch=1,
            grid=(dec_batch,),
            in_specs=[
                pl.BlockSpec(memory_space=pl.ANY),
                pl.BlockSpec((None, IDX_HEADS * dec_seq, IDX_DIM), lambda b, pt: (b, 0, 0)),
                pl.BlockSpec((None, IDX_HEADS * dec_seq, 1), lambda b, pt: (b, 0, 0)),
                pl.BlockSpec((None, IDX_DIM, LANES), lambda b, pt: (b, 0, 0)),
            ],
            out_specs=[pl.BlockSpec((None, SUBLANES, past), lambda b, pt: (b, 0, 0)), stat_spec],
            scratch_shapes=[
                pltpu.VMEM((2, n_pages, IDX_DIM, PAGE_SIZE), F32),
                pltpu.SemaphoreType.DMA((1, 2)),
            ],
        ),
        out_shape=[
            jax.ShapeDtypeStruct((dec_batch, SUBLANES, past), F32),
            jax.ShapeDtypeStruct((dec_batch, SUBLANES, LANES), F32),
        ],
        compiler_params=_cparams(("arbitrary",)),
        name="dsa_sample_score",
    )(page_table, cache_idx_t, qi, wi, ki_new)

    n_rows = dec_batch * SUBLANES
    sel_rows = math.gcd(n_rows, SELECT_ROWS)
    row_spec = pl.BlockSpec((sel_rows, LANES), lambda i: (i, 0))
    selv, selm = pl.pallas_call(
        functools.partial(_sample_select_kernel, topk=topk),
        grid=(n_rows // sel_rows,),
        in_specs=[pl.BlockSpec((sel_rows, past), lambda i: (i, 0)), row_spec],
        out_specs=[row_spec, row_spec],
        out_shape=[jax.ShapeDtypeStruct((n_rows, LANES), F32), jax.ShapeDtypeStruct((n_rows, LANES), I32)],
        compiler_params=_cparams(("arbitrary",)),
        name="sample_select",
    )(sc.reshape(n_rows, past), scn.reshape(n_rows, LANES))
    selv = selv.reshape(dec_batch, SUBLANES, LANES)
    selm = selm.reshape(dec_batch, SUBLANES, LANES)

    ck = cache_k_l.reshape(n_pool, PAGE_SIZE * B_KV_HEADS, B_HEAD_DIM)
    cv = cache_v_l.reshape(n_pool, PAGE_SIZE * B_KV_HEADS, B_HEAD_DIM)
    kv_buf = pltpu.VMEM((2, PAGES_PER_STEP, PAGE_SIZE * B_KV_HEADS, B_HEAD_DIM), F32)
    o = pl.pallas_call(
        functools.partial(_dsa_sample_attend_kernel, dec_seq=dec_seq, n_steps=n_steps, n_samples=dec_batch),
        grid_spec=pltpu.PrefetchScalarGridSpec(
            num_scalar_prefetch=1,
            grid=(dec_batch, n_steps),
            in_specs=[pl.BlockSpec(memory_space=pl.ANY), pl.BlockSpec(memory_space=pl.ANY)] + [
                pl.BlockSpec((None, B_KV_HEADS, rows, B_HEAD_DIM), lambda b, s, pt: (b, 0, 0, 0)),
                pl.BlockSpec((None, SUBLANES, chunk), lambda b, s, pt: (b, 0, s)),
                pl.BlockSpec((None, SUBLANES, LANES), lambda b, s, pt: (b, 0, 0)),
                pl.BlockSpec((None, SUBLANES, LANES), lambda b, s, pt: (b, 0, 0)),
                pl.BlockSpec((None, SUBLANES, LANES), lambda b, s, pt: (b, 0, 0)),
                pl.BlockSpec((None, LANES, 2 * KV_WIDTH), lambda b, s, pt: (b, 0, 0)),
                pl.BlockSpec(sbias.shape, lambda b, s, pt: (0, 0, 0)),
                pl.BlockSpec(sconst.shape, lambda b, s, pt: (0, 0, 0)),
            ],
            out_specs=pl.BlockSpec((None, B_KV_HEADS, rows, B_HEAD_DIM), lambda b, s, pt: (b, 0, 0, 0)),
            scratch_shapes=[
                kv_buf,
                kv_buf,
                pltpu.SemaphoreType.DMA((2, 2)),
                pltpu.VMEM((B_KV_HEADS, rows, 1), F32),
                pltpu.VMEM((B_KV_HEADS, rows, 1), F32),
                pltpu.VMEM((B_KV_HEADS, rows, B_HEAD_DIM), F32),
            ],
        ),
        out_shape=jax.ShapeDtypeStruct((dec_batch, B_KV_HEADS, rows, B_HEAD_DIM), BF16),
        compiler_params=_cparams(("arbitrary", "arbitrary")),
        name="dsa_sample_attend",
    )(page_table, ck, cv, q, sc, scn, selv, selm, kv_new, sbias, sconst)
    o = o.reshape(dec_batch, B_KV_HEADS, B_GQA, SUBLANES, B_HEAD_DIM)[:, :, :, :dec_seq].transpose(0, 3, 1, 2, 4)
    return o.reshape(dec_batch * dec_seq, B_WIDTH)


def _mixout_kernel(u_ref, vn_ref, b_ref, ws_ref, ab_ref, x_ref, w_ref, o_ref, mix_scr, *, tm):
    j = pl.program_id(1)

    @pl.when(j == 0)
    def _():
        row = lax.broadcasted_iota(I32, (A_CHUNK, A_CHUNK), 0)
        col = lax.broadcasted_iota(I32, (A_CHUNK, A_CHUNK), 1)
        for g in range(A_GROUPS):
            wsg = jnp.where(row >= col, ws_ref[g], 0.0).astype(BF16)
            bg = ab_ref[:, g:g + 1]
            cs = slice(g * A_CH, (g + 1) * A_CH)
            for r in range(tm // A_CHUNK):
                rs = slice(r * A_CHUNK, (r + 1) * A_CHUNK)
                sm = jnp.dot(wsg, vn_ref[rs, cs], preferred_element_type=F32) + bg
                mix_scr[rs, cs] = (u_ref[rs, cs].astype(F32) * sm).astype(BF16)
        mix_scr[:, A_WIDTH:] = b_ref[...]

    o_ref[...] = x_ref[...] + jnp.dot(mix_scr[...], w_ref[...].astype(BF16), preferred_element_type=F32)


def _mixout(p, b_out, ws_eff, ab_eff, x2d, w_out_l, *, tm, tn=1024):
    n = x2d.shape[0]
    return pl.pallas_call(
        functools.partial(_mixout_kernel, tm=tm),
        grid=(n // tm, D_MODEL // tn),
        in_specs=[
            pl.BlockSpec((tm, A_WIDTH), lambda i, j: (i, COL_U // A_WIDTH)),
            pl.BlockSpec((tm, A_WIDTH), lambda i, j: (i, COL_V // A_WIDTH)),
            pl.BlockSpec((tm, B_WIDTH), lambda i, j: (i, 0)),
            pl.BlockSpec((A_GROUPS, A_CHUNK, A_CHUNK), lambda i, j: (0, 0, 0)),
            pl.BlockSpec((A_CHUNK, A_GROUPS), lambda i, j: (0, 0)),
            pl.BlockSpec((tm, tn), lambda i, j: (i, j)),
            pl.BlockSpec((A_WIDTH + B_WIDTH, tn), lambda i, j: (0, j)),
        ],
        out_specs=pl.BlockSpec((tm, tn), lambda i, j: (i, j)),
        out_shape=jax.ShapeDtypeStruct((n, D_MODEL), F32),
        scratch_shapes=[pltpu.VMEM((tm, A_WIDTH + B_WIDTH), BF16)],
        compiler_params=_cparams(("arbitrary", "arbitrary")),
        name="mixout",
    )(p, p, b_out, ws_eff, ab_eff, x2d, w_out_l)


def _memkv_kernel(x_ref, wk_ref, wv_ref, kn_ref, mk_ref, mv_ref):
    x = x_ref[...].astype(BF16)
    tm = x.shape[0]
    mk = _head_norm(jnp.dot(x, wk_ref[...].astype(BF16), preferred_element_type=F32), kn_ref[...])
    mv = jnp.dot(x, wv_ref[...].astype(BF16), preferred_element_type=F32)
    for hd in range(MEM_HEADS):
        cs = slice(hd * MEM_HEAD_DIM, (hd + 1) * MEM_HEAD_DIM)
        mk_ref[pl.ds(hd, tm, stride=MEM_HEADS), :] = mk[:, cs]
        mv_ref[pl.ds(hd, tm, stride=MEM_HEADS), :] = mv[:, cs]


def _memkv(mem2d, wk, wv, knorm, *, tm=256):
    n = mem2d.shape[0]
    kn = jnp.tile(knorm, MEM_HEADS).reshape(1, MEM_WIDTH)
    return pl.pallas_call(
        _memkv_kernel,
        grid=(n // tm,),
        in_specs=[
            pl.BlockSpec((tm, D_MODEL), lambda i: (i, 0)),
            pl.BlockSpec((D_MODEL, MEM_WIDTH), lambda i: (0, 0)),
            pl.BlockSpec((D_MODEL, MEM_WIDTH), lambda i: (0, 0)),
            pl.BlockSpec((1, MEM_WIDTH), lambda i: (0, 0)),
        ],
        out_specs=[pl.BlockSpec((tm * MEM_HEADS, MEM_HEAD_DIM), lambda i: (i, 0))] * 2,
        out_shape=[jax.ShapeDtypeStruct((n * MEM_HEADS, MEM_HEAD_DIM), F32)] * 2,
        compiler_params=_cparams(("arbitrary",)),
        name="memkv",
    )(mem2d, wk, wv, kn)


def _memattn_kernel(h_ref, g_ref, wq_ref, qn_ref, mk_ref, mv_ref, wo_ref, gf_ref, wc_ref, bc_ref,
                    o_ref, n_ref, c_ref, *, rows_per_mem, mem_len):
    h = h_ref[...]
    n = _row_norm(h, g_ref[...]).astype(BF16)
    q = _head_norm(jnp.dot(n, wq_ref[...].astype(BF16), preferred_element_type=F32), qn_ref[...]).astype(BF16)
    tm, m = h.shape[0], mk_ref.shape[0] // MEM_HEADS
    if rows_per_mem is not None:
        row = lax.broadcasted_iota(I32, (tm, m), 0)
        col = lax.broadcasted_iota(I32, (tm, m), 1)
        mask = jnp.where(row // rows_per_mem == col // mem_len, 0.0, -jnp.inf)
    outs = []
    for hd in range(MEM_HEADS):
        cs = slice(hd * MEM_HEAD_DIM, (hd + 1) * MEM_HEAD_DIM)
        mk = mk_ref[pl.ds(hd, m, stride=MEM_HEADS), :].astype(BF16)
        mv = mv_ref[pl.ds(hd, m, stride=MEM_HEADS), :].astype(BF16)
        lg = _dot_nt(q[:, cs], mk) * (MEM_HEAD_DIM ** -0.5)
        if rows_per_mem is not None:
            lg = lg + mask
        e = jnp.exp(lg - jnp.max(lg, axis=-1, keepdims=True))
        l = jnp.sum(e, axis=-1, keepdims=True)
        outs.append(jnp.dot(e.astype(BF16), mv, preferred_element_type=F32) / l)
    o = jnp.concatenate(outs, axis=-1).astype(BF16)
    h2 = h + jnp.dot(o, wo_ref[...].astype(BF16), preferred_element_type=F32)
    o_ref[...] = h2
    n_ref[...], c_ref[...] = _route(h2, gf_ref[...], wc_ref[...], bc_ref[...])


def _memattn(h2d, gain, wq, qnorm, mk, mv, wo, gain_ffn, w_cat, b_cat, *, tm, mem_rows, tiles_per_mem,
             rows_per_mem, mem_len):
    n = h2d.shape[0]
    qn = jnp.tile(qnorm, MEM_HEADS).reshape(1, MEM_WIDTH)
    return pl.pallas_call(
        functools.partial(_memattn_kernel, rows_per_mem=rows_per_mem, mem_len=mem_len),
        grid=(n // tm,),
        in_specs=[
            pl.BlockSpec((tm, D_MODEL), lambda i: (i, 0)),
            pl.BlockSpec((1, D_MODEL), lambda i: (0, 0)),
            pl.BlockSpec((D_MODEL, MEM_WIDTH), lambda i: (0, 0)),
            pl.BlockSpec((1, MEM_WIDTH), lambda i: (0, 0)),
            pl.BlockSpec((mem_rows * MEM_HEADS, MEM_HEAD_DIM), lambda i: (i // tiles_per_mem, 0)),
            pl.BlockSpec((mem_rows * MEM_HEADS, MEM_HEAD_DIM), lambda i: (i // tiles_per_mem, 0)),
            pl.BlockSpec((MEM_WIDTH, D_MODEL), lambda i: (0, 0)),
            pl.BlockSpec((1, D_MODEL), lambda i: (0, 0)),
            pl.BlockSpec((D_MODEL, 2 * LANES), lambda i: (0, 0)),
            pl.BlockSpec((1, 2 * LANES), lambda i: (0, 0)),
        ],
        out_specs=[pl.BlockSpec((tm, D_MODEL), lambda i: (i, 0)), pl.BlockSpec((tm, D_MODEL), lambda i: (i, 0)),
                   pl.BlockSpec((tm, LANES), lambda i: (i, 0))],
        out_shape=[jax.ShapeDtypeStruct((n, D_MODEL), F32), jax.ShapeDtypeStruct((n, D_MODEL), BF16),
                   jax.ShapeDtypeStruct((n, LANES), F32)],
        compiler_params=_cparams(("arbitrary",)),
        name="memattn",
    )(h2d, gain.reshape(1, D_MODEL), wq, qn, mk, mv, wo, gain_ffn.reshape(1, D_MODEL), w_cat, b_cat)


def _route(h, gain, w_cat, b_cat):
    n = _row_norm(h, gain).astype(BF16)
    lg = jnp.dot(n, w_cat.astype(BF16), preferred_element_type=F32) + b_cat
    gl, el = lg[:, :LANES], lg[:, LANES:]
    lane = lax.broadcasted_iota(I32, gl.shape, 1)
    lanef = lane.astype(F32)

    def first_max_lane(v, vmax):
        return jnp.min(jnp.where(v == vmax, lanef, float(LANES)), axis=-1, keepdims=True).astype(I32)

    gl = jnp.where(lane < N_GROUPS, gl, -jnp.inf)
    ge = jnp.exp(gl - jnp.max(gl, axis=-1, keepdims=True))
    gp = ge / jnp.sum(ge, axis=-1, keepdims=True)
    g_gate = jnp.max(gp, axis=-1, keepdims=True)
    g_sel = first_max_lane(gp, g_gate)
    in_grp = (lane // EXPERTS_PER_GROUP == g_sel) & (lane < N_EXPERTS)
    el = jnp.where(in_grp, el, -jnp.inf)
    ee = jnp.exp(el - jnp.max(el, axis=-1, keepdims=True))
    ep = jnp.where(in_grp, ee / jnp.sum(ee, axis=-1, keepdims=True), -jnp.inf)
    w1 = jnp.max(ep, axis=-1, keepdims=True)
    i1 = first_max_lane(ep, w1)
    ep2 = jnp.where(lane == i1, -jnp.inf, ep)
    w2 = jnp.max(ep2, axis=-1, keepdims=True)
    i2 = first_max_lane(ep2, w2)
    tot = w1 + w2
    comb = jnp.where(lane == i1, w1 / tot * g_gate, jnp.where(lane == i2, w2 / tot * g_gate, 0.0))
    return n, jnp.where(lane == GSEL_LANE, g_sel.astype(F32), comb)


def _cast_kernel(x_ref, o_ref):
    o_ref[...] = x_ref[...].astype(BF16)


def _cast_bf16(w):
    ne, r, c = w.shape
    return pl.pallas_call(
        _cast_kernel,
        grid=(ne,),
        in_specs=[pl.BlockSpec((None, r, c), lambda e: (e, 0, 0))],
        out_specs=pl.BlockSpec((None, r, c), lambda e: (e, 0, 0)),
        out_shape=jax.ShapeDtypeStruct(w.shape, BF16),
        compiler_params=_cparams(("arbitrary",)),
        name="cast_bf16",
    )(w)


GSEL_LANE = N_EXPERTS
MOE_ROWS = 128
MOE_MAIN_SLACK = 0
MOE_FF_SPLIT = 1


def _moe_routed_kernel(n_ref, c_ref, h_hbm, wg_ref, wu_ref, wd_ref, o_ref,
                       tri_scr, key_scr, xc_scr, yc_scr, cw_scr, nch_ref, sem, *, tm):
    i, g, e, f = (pl.program_id(k) for k in range(4))
    main_rows = tm // N_GROUPS + MOE_MAIN_SLACK
    first_e = (e == 0) & (f == 0)
    last_e = (e == EXPERTS_PER_GROUP - 1) & (f == MOE_FF_SPLIT - 1)

    def onehot_rows(start, rows):
        want = (start + lax.broadcasted_iota(I32, (rows, 1), 0)).astype(F32)
        return jnp.where(key_scr[pl.ds(g, 1), :] == want, 1.0, 0.0).astype(BF16)

    def tail_start(c):
        return pl.multiple_of(main_rows + c * MOE_ROWS, math.gcd(main_rows, MOE_ROWS))

    @pl.when((i == 0) & (g == 0) & first_e)
    def _():
        r = lax.broadcasted_iota(I32, (tm, tm), 0)
        c = lax.broadcasted_iota(I32, (tm, tm), 1)
        tri_scr[...] = jnp.where(r < c, 1.0, 0.0).astype(BF16)

    def residual_copy():
        return pltpu.make_async_copy(h_hbm.at[pl.ds(pl.multiple_of(i * tm, tm), tm)], o_ref, sem)

    @pl.when((g == 0) & first_e)
    def _():
        residual_copy().start()
        lane = lax.broadcasted_iota(I32, (tm, LANES), 1).astype(F32)
        onehot = jnp.where(lane == c_ref[:, GSEL_LANE:GSEL_LANE + 1], 1.0, 0.0).astype(BF16)
        er = lax.broadcasted_iota(I32, (SUBLANES, LANES), 0)
        ec = lax.broadcasted_iota(I32, (SUBLANES, LANES), 1)
        eye = jnp.where(er == ec, 1.0, 0.0).astype(BF16)
        mask_t = _dot_nt(eye, onehot)
        rank_t = jnp.dot(mask_t.astype(BF16), tri_scr[...], preferred_element_type=F32)
        key_scr[...] = jnp.where(mask_t > 0.5, rank_t, -1.0)
        for gg in range(N_GROUPS):
            cnt = jnp.sum(mask_t[gg:gg + 1, :], axis=-1, keepdims=True)
            extra = jnp.ceil(jnp.maximum(cnt - main_rows, 0.0) * (1.0 / MOE_ROWS))
            nch_ref[gg] = jnp.sum(extra).astype(I32)

    ntail = nch_ref[g]

    @pl.when(first_e)
    def _():
        comb = c_ref[...]
        hi = comb.astype(BF16)
        r1 = comb - hi.astype(F32)
        mid = r1.astype(BF16)
        lo = (r1 - mid.astype(F32)).astype(BF16)

        def compact(start, rows):
            s = onehot_rows(start, rows)
            xc_scr[pl.ds(start, rows), :] = jnp.dot(s, n_ref[...], preferred_element_type=F32).astype(BF16)
            cw_scr[pl.ds(start, rows), :] = (jnp.dot(s, hi, preferred_element_type=F32)
                                             + jnp.dot(s, mid, preferred_element_type=F32)
                                             + jnp.dot(s, lo, preferred_element_type=F32))
            yc_scr[pl.ds(start, rows), :] = jnp.zeros((rows, D_MODEL), F32)

        compact(0, main_rows)
        lax.fori_loop(0, ntail, lambda c, carry: (compact(tail_start(c), MOE_ROWS), carry)[1], 0)

    wg = wg_ref[...].astype(BF16)
    wu = wu_ref[...].astype(BF16)
    wd = wd_ref[...].astype(BF16)
    ex = g * EXPERTS_PER_GROUP + e

    def expert(start, rows):
        x = xc_scr[pl.ds(start, rows), :]
        a = jnp.dot(x, wg, preferred_element_type=F32)
        u = jnp.dot(x, wu, preferred_element_type=F32)
        lane = lax.broadcasted_iota(I32, (rows, LANES), 1)
        cw = jnp.sum(jnp.where(lane == ex, cw_scr[pl.ds(start, rows), :], 0.0), axis=-1, keepdims=True)
        hid = (a * jax.nn.sigmoid(a) * u * cw).astype(BF16)
        yc_scr[pl.ds(start, rows), :] += jnp.dot(hid, wd, preferred_element_type=F32)

    expert(0, main_rows)
    lax.fori_loop(0, ntail, lambda c, carry: (expert(tail_start(c), MOE_ROWS), carry)[1], 0)

    @pl.when((g == 0) & last_e)
    def _():
        residual_copy().wait()

    @pl.when(last_e)
    def _():
        def scatter(start, rows):
            s = onehot_rows(start, rows)
            y = yc_scr[pl.ds(start, rows), :].astype(BF16)
            o_ref[...] += lax.dot_general(s, y, (((0,), (0,)), ((), ())), preferred_element_type=F32)

        scatter(0, main_rows)
        lax.fori_loop(0, ntail, lambda c, carry: (scatter(tail_start(c), MOE_ROWS), carry)[1], 0)


def _moe_routed(n3, comb, h2d, w_gate_l, w_up_l, w_down_l, *, tm):
    n = h2d.shape[0]
    ff = EXPERT_FF // MOE_FF_SPLIT

    def wmap(i, g, e, f):
        return (g * EXPERTS_PER_GROUP + e, 0, f)

    return pl.pallas_call(
        functools.partial(_moe_routed_kernel, tm=tm),
        grid=(n // tm, N_GROUPS, EXPERTS_PER_GROUP, MOE_FF_SPLIT),
        in_specs=[
            pl.BlockSpec((tm, D_MODEL), lambda i, g, e, f: (i, 0), pipeline_mode=pl.Buffered(1)),
            pl.BlockSpec((tm, LANES), lambda i, g, e, f: (i, 0), pipeline_mode=pl.Buffered(1)),
            pl.BlockSpec(memory_space=pl.ANY),
            pl.BlockSpec((None, D_MODEL, ff), wmap),
            pl.BlockSpec((None, D_MODEL, ff), wmap),
            pl.BlockSpec((None, ff, D_MODEL), lambda i, g, e, f: (g * EXPERTS_PER_GROUP + e, f, 0)),
        ],
        out_specs=pl.BlockSpec((tm, D_MODEL), lambda i, g, e, f: (i, 0)),
        out_shape=jax.ShapeDtypeStruct((n, D_MODEL), F32),
        scratch_shapes=[
            pltpu.VMEM((tm, tm), BF16),
            pltpu.VMEM((SUBLANES, tm), F32),
            pltpu.VMEM((tm + MOE_ROWS, D_MODEL), BF16),
            pltpu.VMEM((tm + MOE_ROWS, D_MODEL), F32),
            pltpu.VMEM((tm + MOE_ROWS, LANES), F32),
            pltpu.SMEM((N_GROUPS,), I32),
            pltpu.SemaphoreType.DMA(()),
        ],
        compiler_params=pltpu.CompilerParams(
            dimension_semantics=("arbitrary",) * 4, vmem_limit_bytes=MOE_VMEM_LIMIT),
        name="moe_routed",
    )(n3, comb, h2d, w_gate_l, w_up_l, w_down_l)


def _moe_kernel(n_ref, c_ref, h_ref, wg_ref, wu_ref, wd_ref, o_ref):
    e = pl.program_id(1)

    @pl.when(e == 0)
    def _():
        o_ref[...] = h_ref[...]

    x = n_ref[...]
    a = jnp.dot(x, wg_ref[...].astype(BF16), preferred_element_type=F32)
    u = jnp.dot(x, wu_ref[...].astype(BF16), preferred_element_type=F32)
    lane = lax.broadcasted_iota(I32, c_ref.shape, 1)
    c = jnp.sum(jnp.where(lane == e, c_ref[...], 0.0), axis=-1, keepdims=True)
    hid = (a * jax.nn.sigmoid(a) * u * c).astype(BF16)
    o_ref[...] += jnp.dot(hid, wd_ref[...].astype(BF16), preferred_element_type=F32)


def _moe(n3, comb, h2d, w_gate_l, w_up_l, w_down_l, *, tm):
    n = h2d.shape[0]
    return pl.pallas_call(
        _moe_kernel,
        grid=(n // tm, N_EXPERTS),
        in_specs=[
            pl.BlockSpec((tm, D_MODEL), lambda i, e: (i, 0)),
            pl.BlockSpec((tm, LANES), lambda i, e: (i, 0)),
            pl.BlockSpec((tm, D_MODEL), lambda i, e: (i, 0)),
            pl.BlockSpec((None, D_MODEL, EXPERT_FF), lambda i, e: (e, 0, 0)),
            pl.BlockSpec((None, D_MODEL, EXPERT_FF), lambda i, e: (e, 0, 0)),
            pl.BlockSpec((None, EXPERT_FF, D_MODEL), lambda i, e: (e, 0, 0)),
        ],
        out_specs=pl.BlockSpec((tm, D_MODEL), lambda i, e: (i, 0)),
        out_shape=jax.ShapeDtypeStruct((n, D_MODEL), F32),
        compiler_params=_cparams(("arbitrary", "arbitrary")),
        name="moe",
    )(n3, comb, h2d, w_gate_l, w_up_l, w_down_l)


def _post_mixer(p, b_out, ws_eff, ab_eff, x2d, mk, mv, lw, *, tm_mix, tm_mem, mem_rows, tiles_per_mem,
                rows_per_mem, mem_len, tm_moe):
    h1 = _mixout(p, b_out, ws_eff, ab_eff, x2d, lw["w_out"], tm=tm_mix)
    h2, n3, comb = _memattn(h1, lw["norm_mem"], lw["mem_wq"], lw["mem_qnorm"], mk, mv, lw["mem_wo"],
                            lw["norm_ffn"], lw["w_cat"], lw["b_cat"], tm=tm_mem, mem_rows=mem_rows,
                            tiles_per_mem=tiles_per_mem, rows_per_mem=rows_per_mem, mem_len=mem_len)
    moe = _moe_routed if tm_moe > N_GROUPS * MOE_ROWS else _moe
    return moe(n3, comb, h2, lw["w_gate"], lw["w_up"], lw["w_down"], tm=tm_moe)


def kernel(x_prompt, x_sample, cache_k, cache_v, cache_idx_k, cache_mem_k, cache_mem_v, page_table,
           mem_prompt, norm_mix, w_in, a_vnorm, a_ws, a_b, b_qnorm, b_knorm, rel_bias, w_out,
           norm_mem, mem_wq, mem_wk, mem_wv, mem_qnorm, mem_knorm, mem_wo, norm_ffn,
           w_group, b_group, w_router, b_router, w_gate, w_up, w_down):
    bp, t, d = x_prompt.shape
    bd, s, _ = x_sample.shape
    depth = w_in.shape[0]
    mem_len = mem_prompt.shape[1]
    assert d == D_MODEL and t % KEY_CHUNK == 0 and s <= 8 and (bd * s) % LANES == 0

    bias_pairs, sbias, sconst = _bias_tables(rel_bias)
    hp = x_prompt.reshape(bp * t, d)
    hs = x_sample.reshape(bd * s, d)
    outs = [[] for _ in range(9)]
    for l in range(depth):
        w_cat = jnp.zeros((d, 2 * LANES), F32)
        w_cat = w_cat.at[:, :N_GROUPS].set(w_group[l]).at[:, LANES:LANES + N_EXPERTS].set(w_router[l])
        b_cat = jnp.zeros((1, 2 * LANES), F32)
        b_cat = b_cat.at[0, :N_GROUPS].set(b_group[l]).at[0, LANES:LANES + N_EXPERTS].set(b_router[l])
        lw = dict(w_out=w_out[l], norm_mem=norm_mem[l], mem_wq=mem_wq[l], mem_qnorm=mem_qnorm[l],
                  mem_wo=mem_wo[l], norm_ffn=norm_ffn[l], w_cat=w_cat, b_cat=b_cat)

        p, k, v, tail = _proj(hp, norm_mix[l], w_in[l], a_vnorm[l], b_qnorm[l], b_knorm[l],
                              tm=1024, want_vn=False)
        later_weights = (w_gate[l], w_up[l], w_down[l], w_out[l][None])
        fuse_cast = (bp * (t // Q_BLOCK)) % N_EXPERTS == 0
        b_out, casted = _dsa_prompt(p, tail, bias_pairs, later_weights if fuse_cast else (), batch=bp, seq=t)
        if not fuse_cast:
            casted = tuple(_cast_bf16(w) for w in later_weights)
        lw.update(w_gate=casted[0], w_up=casted[1], w_down=casted[2], w_out=casted[3][0])
        mk, mv = _memkv(mem_prompt.reshape(bp * mem_len, d), mem_wk[l], mem_wv[l], mem_knorm[l])
        hp = _post_mixer(p, b_out, a_ws[l], a_b[l].T, hp, mk, mv, lw, tm_mix=1024, tm_mem=512,
                         mem_rows=mem_len, tiles_per_mem=t // 512, rows_per_mem=None, mem_len=mem_len,
                         tm_moe=1024)
        outs[0].append(k.reshape(bp, t, B_KV_HEADS, B_HEAD_DIM))
        outs[1].append(v.reshape(bp, t, B_KV_HEADS, B_HEAD_DIM))
        outs[2].append(tail[:, :IDX_DIM].reshape(bp, t, IDX_DIM))
        outs[3].append(mk.reshape(bp, mem_len, MEM_HEADS, MEM_HEAD_DIM))
        outs[4].append(mv.reshape(bp, mem_len, MEM_HEADS, MEM_HEAD_DIM))

        ns = bd * s
        p_s, k_s, v_s, tail_s, vn_s = _proj(hs, norm_mix[l], w_in[l], a_vnorm[l], b_qnorm[l], b_knorm[l],
                                            tm=ns, want_vn=True)
        b_out_s = _dsa_sample(p_s, tail_s, k_s.reshape(ns, KV_WIDTH), v_s.reshape(ns, KV_WIDTH),
                              cache_k[l], cache_v[l], cache_idx_k[l], page_table,
                              sbias, sconst, dec_batch=bd, dec_seq=s)
        ws_s = jnp.einsum("ab,gts->gatbs", jnp.eye(ns // s, dtype=F32), a_ws[l][:, :s, :s])
        ws_s = ws_s.reshape(A_GROUPS, ns, ns)
        ab_s = jnp.tile(a_b[l][:, :s].T, (ns // s, 1))
        mem_tile = 32
        hs = _post_mixer(p_s, b_out_s, ws_s, ab_s, hs,
                         cache_mem_k[l].reshape(bd * mem_len * MEM_HEADS, MEM_HEAD_DIM),
                         cache_mem_v[l].reshape(bd * mem_len * MEM_HEADS, MEM_HEAD_DIM), lw,
                         tm_mix=ns, tm_mem=mem_tile, mem_rows=mem_tile // s * mem_len, tiles_per_mem=1,
                         rows_per_mem=s, mem_len=mem_len, tm_moe=ns)
        outs[5].append(k_s.reshape(bd, s, B_KV_HEADS, B_HEAD_DIM))
        outs[6].append(v_s.reshape(bd, s, B_KV_HEADS, B_HEAD_DIM))
        outs[7].append(tail_s[:, :IDX_DIM].reshape(bd, s, IDX_DIM))
        outs[8].append(vn_s.reshape(bd, s, A_GROUPS, A_CH))
    return (hp.reshape(bp, t, d), hs.reshape(bd, s, d)) + tuple(jnp.stack(o) for o in outs)
```

```python
import functools
import math

import jax
import jax.numpy as jnp
from jax import lax
from jax.experimental import pallas as pl
from jax.experimental.pallas import tpu as pltpu

BF16 = jnp.bfloat16
F32 = jnp.float32
I32 = jnp.int32

D_MODEL = 2048
A_GROUPS = 8
A_CH = 128
A_WIDTH = A_GROUPS * A_CH
A_CHUNK = 128
B_HEADS = 8
B_HEAD_DIM = 128
B_KV_HEADS = 2
B_GQA = B_HEADS // B_KV_HEADS
B_WIDTH = B_HEADS * B_HEAD_DIM
KV_WIDTH = B_KV_HEADS * B_HEAD_DIM
IDX_HEADS = 16
IDX_DIM = 64
TOPK_MAX = 256
Q_BLOCK = 128
REL_BUCKETS = 32
REL_MAX_DIST = 128
MEM_HEADS = 4
MEM_HEAD_DIM = 128
MEM_WIDTH = MEM_HEADS * MEM_HEAD_DIM
N_GROUPS = 4
EXPERTS_PER_GROUP = 4
N_EXPERTS = N_GROUPS * EXPERTS_PER_GROUP
EXPERT_FF = 512
PAGE_SIZE = 128
EPS = 1e-6

COL_U = 0
COL_V = COL_U + A_WIDTH
COL_Q = COL_V + A_WIDTH
COL_K = COL_Q + B_WIDTH
COL_VV = COL_K + KV_WIDTH
COL_QI = COL_VV + KV_WIDTH
COL_MAIN = COL_QI + IDX_HEADS * IDX_DIM
TAIL_COLS = IDX_DIM + IDX_HEADS
LANES = 128
SUBLANES = 8
PROJ_TN = 512
PROJ_ROW_CHUNK = 256
MEMATTN_ROW_CHUNK = 512
KEY_CHUNK = 256
NEG_INIT = -1e30
VMEM_LIMIT = 56 * 1024 * 1024
MOE_VMEM_LIMIT = 60 * 1024 * 1024


def _cparams(sem):
    return pltpu.CompilerParams(dimension_semantics=sem, vmem_limit_bytes=VMEM_LIMIT)


def _gelu(x):
    return 0.5 * x * (1.0 + lax.erf(x * (2.0 ** -0.5)))


def _head_norm(a, gain, width=LANES):
    outs = []
    for c in range(a.shape[1] // width):
        blk = a[:, c * width:(c + 1) * width]
        ms = jnp.mean(blk * blk, axis=-1, keepdims=True)
        outs.append(blk * lax.rsqrt(ms + EPS) * gain[:, c * width:(c + 1) * width])
    return outs[0] if len(outs) == 1 else jnp.concatenate(outs, axis=-1)


def _row_norm(x, gain):
    ms = jnp.mean(x * x, axis=-1, keepdims=True)
    return x * lax.rsqrt(ms + EPS) * gain


def _dot_nt(a, b):
    return lax.dot_general(a, b, (((1,), (1,)), ((), ())), preferred_element_type=F32)


def _proj_kernel(x_ref, g_ref, w_ref, wt_ref, avn_ref, qn_ref, kn_ref, *rest, want_vn):
    if want_vn:
        p_ref, k_ref, v_ref, tail_ref, vn_ref, xn_scr = rest
    else:
        p_ref, k_ref, v_ref, tail_ref, xn_scr = rest
        vn_ref = None
    j = pl.program_id(1)
    ju = COL_V // PROJ_TN
    jv = COL_Q // PROJ_TN
    jq = COL_K // PROJ_TN
    jkv = COL_QI // PROJ_TN

    @pl.when(j == 0)
    def _():
        xn = _row_norm(x_ref[...], g_ref[...]).astype(BF16)
        xn_scr[...] = xn
        wrow = lax.broadcasted_iota(I32, wt_ref.shape, 0)
        wt = jnp.where(wrow < TAIL_COLS, wt_ref[...], 0.0).astype(BF16)
        t = _dot_nt(xn, wt)
        lane = lax.broadcasted_iota(I32, t.shape, 1)
        tail_ref[...] = jnp.where(lane >= IDX_DIM, t * (IDX_HEADS ** -0.5), t)

    tm = xn_scr.shape[0]
    rc = min(tm, PROJ_ROW_CHUNK)

    def row_chunks(epilogue):
        w = w_ref[...].astype(BF16)
        for r in range(tm // rc):
            rs = slice(r * rc, (r + 1) * rc)
            epilogue(_dot_nt(xn_scr[rs, :], w), rs)

    @pl.when(j < ju)
    def _():
        def epi(acc, rs):
            p_ref[rs, :] = _gelu(acc).astype(BF16)
        row_chunks(epi)

    @pl.when((j >= ju) & (j < jv))
    def _():
        def epi(acc, rs):
            vn = _head_norm(_gelu(acc), avn_ref[...])
            p_ref[rs, :] = vn.astype(BF16)
            if vn_ref is not None:
                vn_ref[rs, :] = vn
        row_chunks(epi)

    @pl.when((j >= jv) & (j < jq))
    def _():
        def epi(acc, rs):
            p_ref[rs, :] = _head_norm(acc, qn_ref[...]).astype(BF16)
        row_chunks(epi)

    @pl.when(j == jq)
    def _():
        def epi(acc, rs):
            k = _head_norm(acc[:, :KV_WIDTH], kn_ref[...])
            v = acc[:, KV_WIDTH:]
            for g in range(B_KV_HEADS):
                dst = pl.ds(rs.start * B_KV_HEADS + g, rs.stop - rs.start, stride=B_KV_HEADS)
                k_ref[dst, :] = k[:, g * B_HEAD_DIM:(g + 1) * B_HEAD_DIM]
                v_ref[dst, :] = v[:, g * B_HEAD_DIM:(g + 1) * B_HEAD_DIM]
            p_ref[rs, :] = jnp.concatenate([k, v], axis=-1).astype(BF16)
        row_chunks(epi)

    @pl.when(j >= jkv)
    def _():
        def epi(acc, rs):
            p_ref[rs, :] = acc.astype(BF16)
        row_chunks(epi)


def _proj(x2d, gain, w_in_l, a_vnorm_l, b_qnorm_l, b_knorm_l, *, tm, want_vn):
    n = x2d.shape[0]
    nj = COL_MAIN // PROJ_TN
    w_in_t = jnp.swapaxes(w_in_l, 0, 1)
    avn = a_vnorm_l.reshape(1, A_WIDTH)
    qn = jnp.tile(b_qnorm_l, PROJ_TN // B_HEAD_DIM).reshape(1, PROJ_TN)
    kn = jnp.tile(b_knorm_l, B_KV_HEADS).reshape(1, KV_WIDTH)
    ju = COL_V // PROJ_TN
    nv = A_WIDTH // PROJ_TN

    def vmap_(i, j):
        return (0, jnp.clip(j - ju, 0, nv - 1))

    in_specs = [
        pl.BlockSpec((tm, D_MODEL), lambda i, j: (i, 0)),
        pl.BlockSpec((1, D_MODEL), lambda i, j: (0, 0)),
        pl.BlockSpec((PROJ_TN, D_MODEL), lambda i, j: (j, 0)),
        pl.BlockSpec((LANES, D_MODEL), lambda i, j: (COL_MAIN // LANES, 0)),
        pl.BlockSpec((1, PROJ_TN), vmap_),
        pl.BlockSpec((1, PROJ_TN), lambda i, j: (0, 0)),
        pl.BlockSpec((1, KV_WIDTH), lambda i, j: (0, 0)),
    ]
    out_shape = [
        jax.ShapeDtypeStruct((n, COL_MAIN), BF16),
        jax.ShapeDtypeStruct((n * B_KV_HEADS, B_HEAD_DIM), F32),
        jax.ShapeDtypeStruct((n * B_KV_HEADS, B_HEAD_DIM), F32),
        jax.ShapeDtypeStruct((n, LANES), F32),
    ]
    out_specs = [
        pl.BlockSpec((tm, PROJ_TN), lambda i, j: (i, j)),
        pl.BlockSpec((tm * B_KV_HEADS, B_HEAD_DIM), lambda i, j: (i, 0)),
        pl.BlockSpec((tm * B_KV_HEADS, B_HEAD_DIM), lambda i, j: (i, 0)),
        pl.BlockSpec((tm, LANES), lambda i, j: (i, 0)),
    ]
    if want_vn:
        out_shape.append(jax.ShapeDtypeStruct((n, A_WIDTH), F32))
        out_specs.append(pl.BlockSpec((tm, PROJ_TN), lambda i, j: (i, jnp.clip(j - ju, 0, nv - 1))))
    return pl.pallas_call(
        functools.partial(_proj_kernel, want_vn=want_vn),
        grid=(n // tm, nj),
        in_specs=in_specs,
        out_specs=out_specs,
        out_shape=out_shape,
        scratch_shapes=[pltpu.VMEM((tm, D_MODEL), BF16)],
        compiler_params=_cparams(("arbitrary", "arbitrary")),
        name="proj",
    )(x2d, gain.reshape(1, D_MODEL), w_in_t, w_in_t, avn, qn, kn)


def _t5_bucket(d):
    max_exact = REL_BUCKETS // 2
    d = jnp.maximum(d, 0)
    ratio = jnp.log(jnp.maximum(d, 1).astype(F32) / max_exact) / math.log(REL_MAX_DIST / max_exact)
    large = jnp.minimum(max_exact + jnp.floor(ratio * (REL_BUCKETS - max_exact)).astype(I32), REL_BUCKETS - 1)
    return jnp.where(d < max_exact, d, large)


def _bias_lookup(bucket, rb_ref, h):
    acc = jnp.zeros(bucket.shape, F32)
    for r in range(REL_BUCKETS):
        acc = jnp.where(bucket == r, rb_ref[r, h], acc)
    return acc


def _bias_kernel(rb_ref, pair_ref, samp_ref, sconst_ref):
    q = lax.broadcasted_iota(I32, (Q_BLOCK, Q_BLOCK), 0)
    kc = lax.broadcasted_iota(I32, (Q_BLOCK, Q_BLOCK), 1)
    nvar = pair_ref.shape[0]
    rels = range(-(nvar - 1), 2)
    buckets = {r: _t5_bucket(q - kc - r * Q_BLOCK) for r in rels}
    for h in range(B_HEADS):
        g, hh = divmod(h, B_GQA)
        tiles = {r: _bias_lookup(buckets[r], rb_ref, h) for r in rels}
        for var in range(nvar):
            r0 = var - (nvar - 1)
            for t in range(2):
                pair_ref[var, g, hh * Q_BLOCK:(hh + 1) * Q_BLOCK, t * Q_BLOCK:(t + 1) * Q_BLOCK] = tiles[r0 + t]
    rows = B_GQA * SUBLANES
    row = lax.broadcasted_iota(I32, (rows, 2 * PAGE_SIZE), 0)
    col = lax.broadcasted_iota(I32, (rows, 2 * PAGE_SIZE), 1)
    t = row % SUBLANES
    d = jnp.where(col < PAGE_SIZE, t + PAGE_SIZE - col, t - (col - PAGE_SIZE))
    bucket = _t5_bucket(d)
    far = jnp.full((rows, LANES), REL_BUCKETS - 1, I32)
    rowc = lax.broadcasted_iota(I32, (rows, LANES), 0)
    for g in range(B_KV_HEADS):
        acc = jnp.zeros((rows, 2 * PAGE_SIZE), F32)
        accc = jnp.zeros((rows, LANES), F32)
        for hh in range(B_GQA):
            h = g * B_GQA + hh
            acc = jnp.where(row // SUBLANES == hh, _bias_lookup(bucket, rb_ref, h), acc)
            accc = jnp.where(rowc // SUBLANES == hh, _bias_lookup(far, rb_ref, h), accc)
        samp_ref[g] = acc
        sconst_ref[g] = accc


N_PAIR_VARIANTS = 4


def _bias_tables(rel_bias):
    rows = B_GQA * SUBLANES
    return pl.pallas_call(
        _bias_kernel,
        in_specs=[pl.BlockSpec(memory_space=pltpu.SMEM)],
        out_shape=[
            jax.ShapeDtypeStruct((N_PAIR_VARIANTS, B_KV_HEADS, B_GQA * Q_BLOCK, KEY_CHUNK), F32),
            jax.ShapeDtypeStruct((B_KV_HEADS, rows, 2 * PAGE_SIZE), F32),
            jax.ShapeDtypeStruct((B_KV_HEADS, rows, LANES), F32),
        ],
        name="bias_tables",
    )(rel_bias)


def _key_to_f32(key):
    bits = jnp.where(key < 0, key ^ I32(-2 ** 31), ~key)
    return lax.bitcast_convert_type(bits, F32)


def _kth_largest(count_ge, shape, k):
    def body(it, prefix):
        bit = 31 - it
        cand = prefix | lax.shift_left(I32(1), bit)
        cnt = count_ge(_key_to_f32(cand))
        return jnp.where(cnt >= k, cand, prefix)

    prefix = lax.fori_loop(0, 32, body, jnp.zeros(shape, I32))
    thr = _key_to_f32(prefix)
    return jnp.where((prefix & I32(-2 ** 23)) == 0, -jnp.inf, thr)


NO_INDEX_BOUND = 2 ** 30
TIE_REPAIR_ROUNDS = 2


def _topk_select(fold_sum, fold_min, shape, k, index_bits):
    thr = _kth_largest(lambda t: fold_sum(lambda sc, kp: sc >= t), shape, k)
    c_ge = fold_sum(lambda sc, kp: sc >= thr)
    no_bound = jnp.full(shape, NO_INDEX_BOUND, I32)

    def repair(_):
        v = thr
        strict = jnp.zeros(shape, F32)

        def in_set(sc, v, strict):
            return (sc > v) | ((sc == v) & (strict < 0.5))

        for _ in range(TIE_REPAIR_ROUNDS):
            c_set = fold_sum(lambda sc, kp: in_set(sc, v, strict))
            vmin = fold_min(lambda sc, kp: jnp.where(in_set(sc, v, strict), sc, jnp.inf))
            c_gt = fold_sum(lambda sc, kp: sc > vmin)
            drop = (c_set > k) & (c_gt >= k)
            v = jnp.where(drop, vmin, v)
            strict = jnp.where(drop, 1.0, strict)
        c_set = fold_sum(lambda sc, kp: in_set(sc, v, strict))
        vt = fold_min(lambda sc, kp: jnp.where(in_set(sc, v, strict), sc, jnp.inf))
        need = k - fold_sum(lambda sc, kp: sc > vt)

        def idx_body(it, m):
            cand = m | lax.shift_left(I32(1), index_bits - 1 - it)
            below = fold_sum(lambda sc, kp: (sc == vt) & (kp < cand))
            return jnp.where(below < need, cand, m)

        m = lax.fori_loop(0, index_bits, idx_body, jnp.zeros(shape, I32))
        tie = c_set > k
        v_out = jnp.where(tie, vt, v)
        m_out = jnp.where(tie, m, jnp.where(strict > 0.5, -1, no_bound))
        return v_out, m_out

    return lax.cond(jnp.max(c_ge) > k, repair, lambda _: (thr, no_bound), 0)


CHUNKS_PER_TRIP = 4


def _chunk_loop(n, body, init):
    def trip(j, c):
        for k in range(CHUNKS_PER_TRIP):
            c = body(CHUNKS_PER_TRIP * j + k, c)
        return c

    whole = n // CHUNKS_PER_TRIP
    carry = lax.fori_loop(0, whole, trip, init)
    return lax.fori_loop(whole * CHUNKS_PER_TRIP, n, body, carry)


def _dsa_prompt_kernel(q_ref, kv_ref, qia_ref, qib_ref, tailk_ref, tailq_ref, bias_ref, *rest, topk, n_cast):
    cast_in, rest = rest[:n_cast], rest[n_cast:]
    o_ref, rest = rest[0], rest[1:]
    cast_out, rest = rest[:n_cast], rest[n_cast:]
    kil_scr, kir_scr, sc_scr, lg_scr, m_scr, l_scr, acc_scr = rest
    i = pl.program_id(1)

    for src, dst in zip(cast_in, cast_out):
        dst[...] = src[...].astype(BF16)

    @pl.when(i == 0)
    def _():
        ki = tailk_ref[:, :IDX_DIM].astype(BF16)
        z = jnp.zeros_like(ki)
        kil_scr[...] = jnp.concatenate([ki, z], axis=-1)
        kir_scr[...] = jnp.concatenate([z, ki], axis=-1)

    wi = tailq_ref[:, IDX_DIM:IDX_DIM + IDX_HEADS] * (IDX_DIM ** -0.5)
    nk = (i + 2) // 2
    qpos = i * Q_BLOCK + lax.broadcasted_iota(I32, (Q_BLOCK, 1), 0)
    kloc = lax.broadcasted_iota(I32, (1, KEY_CHUNK), 1)
    heads_per_ref = qia_ref.shape[1] // IDX_DIM

    pairs = []
    for hp in range(IDX_HEADS // 2):
        ref = qia_ref if 2 * hp < heads_per_ref else qib_ref
        base = (2 * hp) % heads_per_ref * IDX_DIM
        pairs.append(ref[:, base:base + 2 * IDX_DIM])
    qstack = jnp.concatenate(pairs, axis=0)

    def score_chunk(c, carry):
        off = pl.multiple_of(c * KEY_CHUNK, KEY_CHUNK)
        acc = jnp.zeros((Q_BLOCK, KEY_CHUNK), F32)
        for side, k_scr in ((0, kil_scr), (1, kir_scr)):
            s = _dot_nt(qstack, k_scr[pl.ds(off, KEY_CHUNK), :])
            for hp in range(IDX_HEADS // 2):
                h = 2 * hp + side
                acc = acc + wi[:, h:h + 1] * jnp.maximum(s[hp * Q_BLOCK:(hp + 1) * Q_BLOCK], 0.0)
        kpos = c * KEY_CHUNK + kloc
        sc_scr[c] = jnp.where(kpos <= qpos, acc, -jnp.inf)
        return carry

    _chunk_loop(nk, score_chunk, 0)

    def fold_sum(pred):
        def body(c, cnt):
            m = jnp.where(pred(sc_scr[c], c * KEY_CHUNK + kloc), 1.0, 0.0)
            return cnt + m[:, :LANES] + m[:, LANES:]
        cnt = _chunk_loop(nk, body, jnp.zeros((Q_BLOCK, LANES), F32))
        return jnp.sum(cnt, axis=-1, keepdims=True)

    def fold_min(val):
        def body(c, acc):
            x = val(sc_scr[c], c * KEY_CHUNK + kloc)
            return jnp.minimum(acc, jnp.minimum(x[:, :LANES], x[:, LANES:]))
        acc = lax.fori_loop(0, nk, body, jnp.full((Q_BLOCK, LANES), jnp.inf, F32))
        return jnp.min(acc, axis=-1, keepdims=True)

    index_bits = (kv_ref.shape[0] - 1).bit_length()
    sel_v, sel_m = _topk_select(fold_sum, fold_min, (Q_BLOCK, 1), topk, index_bits)

    def mask_chunk(c, carry):
        kpos = c * KEY_CHUNK + kloc
        sc = sc_scr[c]
        chosen = (sc > sel_v) | ((sc == sel_v) & (kpos <= sel_m))
        sc_scr[c] = jnp.where(chosen & (kpos <= qpos), 0.0, -jnp.inf)
        return carry

    lax.fori_loop(0, nk, mask_chunk, 0)

    scale = B_HEAD_DIM ** -0.5
    rows = B_GQA * Q_BLOCK
    qgs = [jnp.concatenate(
        [q_ref[:, (g * B_GQA + hh) * B_HEAD_DIM:(g * B_GQA + hh + 1) * B_HEAD_DIM] for hh in range(B_GQA)],
        axis=0) for g in range(B_KV_HEADS)]
    m_scr[...] = jnp.full(m_scr.shape, -jnp.inf, F32)
    l_scr[...] = jnp.zeros(l_scr.shape, F32)
    acc_scr[...] = jnp.zeros(acc_scr.shape, F32)

    def logits_chunk(c, carry):
        off = pl.multiple_of(c * KEY_CHUNK, KEY_CHUNK)
        var = jnp.clip(2 * c - i, -(N_PAIR_VARIANTS - 1), 0) + (N_PAIR_VARIANTS - 1)
        sel = sc_scr[c]
        sel = jnp.concatenate([sel] * B_GQA, axis=0)
        for g in range(B_KV_HEADS):
            kc = kv_ref[pl.ds(off, KEY_CHUNK), g * B_HEAD_DIM:(g + 1) * B_HEAD_DIM]
            lg = _dot_nt(qgs[g], kc) * scale + bias_ref[var, g] + sel
            lg_scr[c, g] = lg
            m_scr[g] = jnp.maximum(m_scr[g], jnp.maximum(lg[:, :LANES], lg[:, LANES:]))
        return carry

    _chunk_loop(nk, logits_chunk, 0)
    for g in range(B_KV_HEADS):
        m = jnp.max(m_scr[g], axis=-1, keepdims=True)
        m_scr[g] = jnp.broadcast_to(m, (rows, LANES))

    def pv_chunk(c, carry):
        off = pl.multiple_of(c * KEY_CHUNK, KEY_CHUNK)
        for g in range(B_KV_HEADS):
            vc = kv_ref[pl.ds(off, KEY_CHUNK), KV_WIDTH + g * B_HEAD_DIM:KV_WIDTH + (g + 1) * B_HEAD_DIM]
            mb = m_scr[g]
            p = jnp.exp(lg_scr[c, g] - jnp.concatenate([mb, mb], axis=-1))
            l_scr[g] += p[:, :LANES] + p[:, LANES:]
            acc_scr[g] += jnp.dot(p.astype(BF16), vc, preferred_element_type=F32)
        return carry

    _chunk_loop(nk, pv_chunk, 0)
    for g in range(B_KV_HEADS):
        o = acc_scr[g] / jnp.sum(l_scr[g], axis=-1, keepdims=True)
        for hh in range(B_GQA):
            h = g * B_GQA + hh
            o_ref[:, h * B_HEAD_DIM:(h + 1) * B_HEAD_DIM] = o[hh * Q_BLOCK:(hh + 1) * Q_BLOCK].astype(BF16)


def _dsa_prompt(p, tail, bias_pairs, cast_weights, *, batch, seq):
    topk = min(TOPK_MAX, seq // 4)
    nblk = seq // Q_BLOCK
    qi_w = PROJ_TN
    steps = batch * nblk
    cast_specs, cast_shapes = [], []
    for w in cast_weights:
        ne, r, c = w.shape
        parts = steps // ne
        assert steps == ne * parts and r % (parts * 16) == 0
        spec = pl.BlockSpec((None, r // parts, c),
                            lambda b, i, parts=parts: ((b * nblk + i) // parts, (b * nblk + i) % parts, 0))
        cast_specs.append(spec)
        cast_shapes.append(jax.ShapeDtypeStruct(w.shape, BF16))
    outs = pl.pallas_call(
        functools.partial(_dsa_prompt_kernel, topk=topk, n_cast=len(cast_weights)),
        grid=(batch, nblk),
        in_specs=[
            pl.BlockSpec((Q_BLOCK, B_WIDTH), lambda b, i: (b * nblk + i, COL_Q // B_WIDTH)),
            pl.BlockSpec((seq, 2 * KV_WIDTH), lambda b, i: (b, COL_K // (2 * KV_WIDTH))),
            pl.BlockSpec((Q_BLOCK, qi_w), lambda b, i: (b * nblk + i, COL_QI // qi_w)),
            pl.BlockSpec((Q_BLOCK, qi_w), lambda b, i: (b * nblk + i, COL_QI // qi_w + 1)),
            pl.BlockSpec((seq, LANES), lambda b, i: (b, 0)),
            pl.BlockSpec((Q_BLOCK, LANES), lambda b, i: (b * nblk + i, 0)),
            pl.BlockSpec(bias_pairs.shape, lambda b, i: (0, 0, 0, 0)),
        ] + cast_specs,
        out_specs=[pl.BlockSpec((Q_BLOCK, B_WIDTH), lambda b, i: (b * nblk + i, 0))] + cast_specs,
        out_shape=[jax.ShapeDtypeStruct((batch * seq, B_WIDTH), BF16)] + cast_shapes,
        scratch_shapes=[
            pltpu.VMEM((seq, 2 * IDX_DIM), BF16),
            pltpu.VMEM((seq, 2 * IDX_DIM), BF16),
            pltpu.VMEM((seq // KEY_CHUNK, Q_BLOCK, KEY_CHUNK), F32),
            pltpu.VMEM((seq // KEY_CHUNK, B_KV_HEADS, B_GQA * Q_BLOCK, KEY_CHUNK), F32),
            pltpu.VMEM((B_KV_HEADS, B_GQA * Q_BLOCK, LANES), F32),
            pltpu.VMEM((B_KV_HEADS, B_GQA * Q_BLOCK, LANES), F32),
            pltpu.VMEM((B_KV_HEADS, B_GQA * Q_BLOCK, B_HEAD_DIM), F32),
        ],
        compiler_params=_cparams(("arbitrary", "arbitrary")),
        name="dsa_prompt",
    )(p, p, p, p, tail, tail, bias_pairs, *cast_weights)
    return outs[0], tuple(outs[1:])


PAGES_PER_STEP = 32
SAMPLE_SUB_PAGES = 32
SCORE_KEYS_PER_DOT = 2048
FOLD_VREGS = 16
SELECT_ROWS = 64


def _fetch_pages(pt_ref, sources, bufs, sem, step, n_steps_total, steps_per_sample, pages_per_step):
    slot = step % 2

    def copies(n, slot_, page_of):
        sample = n // steps_per_sample
        first = (n % steps_per_sample) * pages_per_step
        for r in range(pages_per_step):
            page = page_of(sample, first + r)
            for a, (src, buf) in enumerate(zip(sources, bufs)):
                yield pltpu.make_async_copy(src.at[page], buf.at[slot_, r], sem.at[a, slot_])

    def table(sample, idx):
        return pt_ref[sample, idx]

    @pl.when(step == 0)
    def _():
        for cp in copies(step, slot, table):
            cp.start()

    @pl.when(step + 1 < n_steps_total)
    def _():
        for cp in copies(step + 1, 1 - slot, table):
            cp.start()

    for cp in copies(step, slot, lambda sample, idx: 0):
        cp.wait()
    return slot


def _dsa_sample_score_kernel(pt_ref, idx_hbm, qi_ref, wi_ref, kin_ref, sc_ref, scn_ref,
                             page_buf, sem, *, dec_seq, n_pages, n_samples):
    slot = _fetch_pages(pt_ref, [idx_hbm], [page_buf], sem, pl.program_id(0), n_samples, 1, n_pages)
    pages = [page_buf.at[slot, r] for r in range(n_pages)]
    past = n_pages * PAGE_SIZE
    pages_per_dot = SCORE_KEYS_PER_DOT // PAGE_SIZE
    qi = qi_ref[...]
    wi = wi_ref[...] * (IDX_DIM ** -0.5)

    def scores(keys_t):
        r = jnp.maximum(jnp.dot(qi, keys_t, preferred_element_type=F32), 0.0) * wi
        acc = r[0:SUBLANES]
        for j in range(1, IDX_HEADS * dec_seq // SUBLANES):
            acc = acc + r[j * SUBLANES:(j + 1) * SUBLANES]
        shift = dec_seq
        while shift < SUBLANES:
            acc = acc + pltpu.roll(acc, shift, axis=0)
            shift *= 2
        return acc

    for d in range(n_pages // pages_per_dot):
        keys_t = jnp.concatenate(
            [pg[...].astype(BF16) for pg in pages[d * pages_per_dot:(d + 1) * pages_per_dot]], axis=1)
        sc_ref[:, d * SCORE_KEYS_PER_DOT:(d + 1) * SCORE_KEYS_PER_DOT] = scores(keys_t)

    row = lax.broadcasted_iota(I32, (SUBLANES, LANES), 0)
    col = lax.broadcasted_iota(I32, (SUBLANES, LANES), 1)
    scn_ref[...] = jnp.where((col <= row % dec_seq) & (col < dec_seq), scores(kin_ref[...]), -jnp.inf)


def _sample_select_kernel(sc_ref, scn_ref, selv_ref, selm_ref, *, topk):
    rows, past = sc_ref.shape
    col = lax.broadcasted_iota(I32, (1, LANES), 1)

    def pieces():
        yield scn_ref[...], past + col
        for w in range(past // LANES):
            yield sc_ref[:, w * LANES:(w + 1) * LANES], w * LANES + col

    ways = max(1, FOLD_VREGS // (rows // SUBLANES))

    def fold(term, combine):
        parts = [None] * ways
        for n, (sc, kp) in enumerate(pieces()):
            x = term(sc, kp)
            parts[n % ways] = x if parts[n % ways] is None else combine(parts[n % ways], x)
        parts = [p for p in parts if p is not None]
        while len(parts) > 1:
            parts = [combine(parts[j], parts[j + 1]) if j + 1 < len(parts) else parts[j]
                     for j in range(0, len(parts), 2)]
        return parts[0]

    def fold_sum(pred):
        cnt = fold(lambda sc, kp: jnp.where(pred(sc, kp), 1.0, 0.0), jnp.add)
        return jnp.sum(cnt, axis=-1, keepdims=True)

    def fold_min(val):
        return jnp.min(fold(val, jnp.minimum), axis=-1, keepdims=True)

    index_bits = (past + LANES - 1).bit_length()
    sel_v, sel_m = _topk_select(fold_sum, fold_min, (rows, 1), topk, index_bits)
    selv_ref[...] = jnp.broadcast_to(sel_v, (rows, LANES))
    selm_ref[...] = jnp.broadcast_to(sel_m, (rows, LANES))


def _dsa_sample_attend_kernel(pt_ref, ck_hbm, cv_hbm, q_ref, sc_ref, scn_ref, selv_ref, selm_ref, kvn_ref,
                              sbias_ref, sconst_ref, o_ref, kbuf, vbuf, sem, m_scr, l_scr, acc_scr,
                              *, dec_seq, n_steps, n_samples):
    s = pl.program_id(1)
    slot = _fetch_pages(pt_ref, [ck_hbm, cv_hbm], [kbuf, vbuf], sem, pl.program_id(0) * n_steps + s,
                        n_samples * n_steps, n_steps, PAGES_PER_STEP)
    kpages = [kbuf.at[slot, r] for r in range(PAGES_PER_STEP)]
    vpages = [vbuf.at[slot, r] for r in range(PAGES_PER_STEP)]
    chunk = PAGES_PER_STEP * PAGE_SIZE
    rows = B_GQA * SUBLANES
    scale = B_HEAD_DIM ** -0.5

    @pl.when(s == 0)
    def _():
        m_scr[...] = jnp.full(m_scr.shape, NEG_INIT, F32)
        l_scr[...] = jnp.zeros(l_scr.shape, F32)
        acc_scr[...] = jnp.zeros(acc_scr.shape, F32)

    sel_v = selv_ref[:, 0:1]
    sel_m = selm_ref[:, 0:1]

    def chosen(sc, kpos):
        return jnp.where((sc > sel_v) | ((sc == sel_v) & (kpos <= sel_m)), 0.0, -jnp.inf)

    def head_rows(pgs, g):
        return jnp.concatenate(
            [pg[pl.ds(g, PAGE_SIZE, stride=B_KV_HEADS), :].astype(BF16) for pg in pgs], axis=0)

    kcol = lax.broadcasted_iota(I32, (SUBLANES, chunk), 1)
    sel = chosen(sc_ref[...], s * chunk + kcol)
    sel = jnp.concatenate([sel] * B_GQA, axis=0)
    is_last = s == n_steps - 1
    ncol = lax.broadcasted_iota(I32, (SUBLANES, LANES), 1)
    seln = chosen(scn_ref[...], n_steps * chunk + ncol)
    seln = jnp.where(is_last & (scn_ref[...] > -jnp.inf), seln, -jnp.inf)
    seln = jnp.concatenate([seln] * B_GQA, axis=0)

    def partial(lg, v):
        m = jnp.maximum(jnp.max(lg, axis=-1, keepdims=True), NEG_INIT)
        p = jnp.exp(lg - m)
        return m, jnp.sum(p, axis=-1, keepdims=True), jnp.dot(p.astype(BF16), v, preferred_element_type=F32)

    sub = SAMPLE_SUB_PAGES * PAGE_SIZE
    nsub = PAGES_PER_STEP // SAMPLE_SUB_PAGES
    lane_sub = lax.broadcasted_iota(I32, (rows, sub), 1)
    for g in range(B_KV_HEADS):
        qg = q_ref[g]
        far = sconst_ref[g, :, 0:1]
        near = jnp.concatenate([jnp.zeros((rows, sub - PAGE_SIZE), F32), sbias_ref[g, :, :PAGE_SIZE]], axis=-1)
        parts = [(m_scr[g], l_scr[g], acc_scr[g])]
        for j in range(nsub):
            pj = slice(j * SAMPLE_SUB_PAGES, (j + 1) * SAMPLE_SUB_PAGES)
            lg = _dot_nt(qg, head_rows(kpages[pj], g)) * scale + sel[:, j * sub:(j + 1) * sub]
            if j == nsub - 1:
                lg = lg + jnp.where(is_last & (lane_sub >= sub - PAGE_SIZE), near, far)
            else:
                lg = lg + far
            parts.append(partial(lg, head_rows(vpages[pj], g)))
        kn = kvn_ref[:, g * B_HEAD_DIM:(g + 1) * B_HEAD_DIM]
        vn = kvn_ref[:, KV_WIDTH + g * B_HEAD_DIM:KV_WIDTH + (g + 1) * B_HEAD_DIM]
        parts.append(partial(_dot_nt(qg, kn) * scale + sbias_ref[g, :, PAGE_SIZE:] + seln, vn))
        m_new = parts[0][0]
        for m, _, _ in parts[1:]:
            m_new = jnp.maximum(m_new, m)
        l_new = jnp.zeros((rows, 1), F32)
        acc_new = jnp.zeros((rows, B_HEAD_DIM), F32)
        for m, l, acc in parts:
            w = jnp.exp(m - m_new)
            l_new = l_new + w * l
            acc_new = acc_new + w * acc
        m_scr[g] = m_new
        l_scr[g] = l_new
        acc_scr[g] = acc_new

    @pl.when(is_last)
    def _():
        for g in range(B_KV_HEADS):
            o_ref[g] = (acc_scr[g] / l_scr[g]).astype(BF16)


def _dsa_sample(p_s, tail_s, k_s, v_s, cache_k_l, cache_v_l, cache_idx_l, page_table, sbias, sconst,
                *, dec_batch, dec_seq):
    n_pages = page_table.shape[1]
    past = n_pages * PAGE_SIZE
    topk = min(TOPK_MAX, (past + dec_seq) // 4)
    n_steps = n_pages // PAGES_PER_STEP
    chunk = PAGES_PER_STEP * PAGE_SIZE
    rows = B_GQA * SUBLANES
    n_pool = cache_k_l.shape[0]

    assert SUBLANES % dec_seq == 0
    qi = p_s[:, COL_QI:COL_MAIN].reshape(dec_batch, dec_seq, IDX_HEADS, IDX_DIM).transpose(0, 2, 1, 3)
    qi = qi.reshape(dec_batch, IDX_HEADS * dec_seq, IDX_DIM)
    wi = tail_s[:, IDX_DIM:IDX_DIM + IDX_HEADS].reshape(dec_batch, dec_seq, IDX_HEADS).transpose(0, 2, 1)
    wi = wi.reshape(dec_batch, IDX_HEADS * dec_seq, 1)
    ki_new = tail_s[:, :IDX_DIM].astype(BF16).reshape(dec_batch, dec_seq, IDX_DIM)
    ki_new = jnp.pad(ki_new, ((0, 0), (0, LANES - dec_seq), (0, 0))).transpose(0, 2, 1)
    q = p_s[:, COL_Q:COL_K].reshape(dec_batch, dec_seq, B_KV_HEADS, B_GQA, B_HEAD_DIM)
    q = jnp.pad(q.transpose(0, 2, 3, 1, 4), ((0, 0), (0, 0), (0, 0), (0, SUBLANES - dec_seq), (0, 0)))
    q = q.reshape(dec_batch, B_KV_HEADS, rows, B_HEAD_DIM)
    kv_new = jnp.concatenate([k_s, v_s], axis=-1).astype(BF16).reshape(dec_batch, dec_seq, 2 * KV_WIDTH)
    kv_new = jnp.pad(kv_new, ((0, 0), (0, LANES - dec_seq), (0, 0)))

    cache_idx_t = jnp.swapaxes(cache_idx_l, 1, 2)
    stat_spec = pl.BlockSpec((None, SUBLANES, LANES), lambda b, pt: (b, 0, 0))
    sc, scn = pl.pallas_call(
        functools.partial(_dsa_sample_score_kernel, dec_seq=dec_seq, n_pages=n_pages, n_samples=dec_batch),
        grid_spec=pltpu.PrefetchScalarGridSpec(
            num_scalar_prefetch=1,
            grid=(dec_batch,),
            in_specs=[
                pl.BlockSpec(memory_space=pl.ANY),
                pl.BlockSpec((None, IDX_HEADS * dec_seq, IDX_DIM), lambda b, pt: (b, 0, 0)),
                pl.BlockSpec((None, IDX_HEADS * dec_seq, 1), lambda b, pt: (b, 0, 0)),
                pl.BlockSpec((None, IDX_DIM, LANES), lambda b, pt: (b, 0, 0)),
            ],
            out_specs=[pl.BlockSpec((None, SUBLANES, past), lambda b, pt: (b, 0, 0)), stat_spec],
            scratch_shapes=[
                pltpu.VMEM((2, n_pages, IDX_DIM, PAGE_SIZE), F32),
                pltpu.SemaphoreType.DMA((1, 2)),
            ],
        ),
        out_shape=[
            jax.ShapeDtypeStruct((dec_batch, SUBLANES, past), F32),
            jax.ShapeDtypeStruct((dec_batch, SUBLANES, LANES), F32),
        ],
        compiler_params=_cparams(("arbitrary",)),
        name="dsa_sample_score",
    )(page_table, cache_idx_t, qi, wi, ki_new)

    n_rows = dec_batch * SUBLANES
    sel_rows = math.gcd(n_rows, SELECT_ROWS)
    row_spec = pl.BlockSpec((sel_rows, LANES), lambda i: (i, 0))
    selv, selm = pl.pallas_call(
        functools.partial(_sample_select_kernel, topk=topk),
        grid=(n_rows // sel_rows,),
        in_specs=[pl.BlockSpec((sel_rows, past), lambda i: (i, 0)), row_spec],
        out_specs=[row_spec, row_spec],
        out_shape=[jax.ShapeDtypeStruct((n_rows, LANES), F32), jax.ShapeDtypeStruct((n_rows, LANES), I32)],
        compiler_params=_cparams(("arbitrary",)),
        name="sample_select",
    )(sc.reshape(n_rows, past), scn.reshape(n_rows, LANES))
    selv = selv.reshape(dec_batch, SUBLANES, LANES)
    selm = selm.reshape(dec_batch, SUBLANES, LANES)

    ck = cache_k_l.reshape(n_pool, PAGE_SIZE * B_KV_HEADS, B_HEAD_DIM)
    cv = cache_v_l.reshape(n_pool, PAGE_SIZE * B_KV_HEADS, B_HEAD_DIM)
    kv_buf = pltpu.VMEM((2, PAGES_PER_STEP, PAGE_SIZE * B_KV_HEADS, B_HEAD_DIM), F32)
    o = pl.pallas_call(
        functools.partial(_dsa_sample_attend_kernel, dec_seq=dec_seq, n_steps=n_steps, n_samples=dec_batch),
        grid_spec=pltpu.PrefetchScalarGridSpec(
            num_scalar_prefetch=1,
            grid=(dec_batch, n_steps),
            in_specs=[pl.BlockSpec(memory_space=pl.ANY), pl.BlockSpec(memory_space=pl.ANY)] + [
                pl.BlockSpec((None, B_KV_HEADS, rows, B_HEAD_DIM), lambda b, s, pt: (b, 0, 0, 0)),
                pl.BlockSpec((None, SUBLANES, chunk), lambda b, s, pt: (b, 0, s)),
                pl.BlockSpec((None, SUBLANES, LANES), lambda b, s, pt: (b, 0, 0)),
                pl.BlockSpec((None, SUBLANES, LANES), lambda b, s, pt: (b, 0, 0)),
                pl.BlockSpec((None, SUBLANES, LANES), lambda b, s, pt: (b, 0, 0)),
                pl.BlockSpec((None, LANES, 2 * KV_WIDTH), lambda b, s, pt: (b, 0, 0)),
                pl.BlockSpec(sbias.shape, lambda b, s, pt: (0, 0, 0)),
                pl.BlockSpec(sconst.shape, lambda b, s, pt: (0, 0, 0)),
            ],
            out_specs=pl.BlockSpec((None, B_KV_HEADS, rows, B_HEAD_DIM), lambda b, s, pt: (b, 0, 0, 0)),
            scratch_shapes=[
                kv_buf,
                kv_buf,
                pltpu.SemaphoreType.DMA((2, 2)),
                pltpu.VMEM((B_KV_HEADS, rows, 1), F32),
                pltpu.VMEM((B_KV_HEADS, rows, 1), F32),
                pltpu.VMEM((B_KV_HEADS, rows, B_HEAD_DIM), F32),
            ],
        ),
        out_shape=jax.ShapeDtypeStruct((dec_batch, B_KV_HEADS, rows, B_HEAD_DIM), BF16),
        compiler_params=_cparams(("arbitrary", "arbitrary")),
        name="dsa_sample_attend",
    )(page_table, ck, cv, q, sc, scn, selv, selm, kv_new, sbias, sconst)
    o = o.reshape(dec_batch, B_KV_HEADS, B_GQA, SUBLANES, B_HEAD_DIM)[:, :, :, :dec_seq].transpose(0, 3, 1, 2, 4)
    return o.reshape(dec_batch * dec_seq, B_WIDTH)


def _mixout_kernel(u_ref, vn_ref, b_ref, ws_ref, ab_ref, x_ref, w_ref, o_ref, mix_scr, *, tm):
    j = pl.program_id(1)

    @pl.when(j == 0)
    def _():
        row = lax.broadcasted_iota(I32, (A_CHUNK, A_CHUNK), 0)
        col = lax.broadcasted_iota(I32, (A_CHUNK, A_CHUNK), 1)
        for g in range(A_GROUPS):
            wsg = jnp.where(row >= col, ws_ref[g], 0.0).astype(BF16)
            bg = ab_ref[:, g:g + 1]
            cs = slice(g * A_CH, (g + 1) * A_CH)
            for r in range(tm // A_CHUNK):
                rs = slice(r * A_CHUNK, (r + 1) * A_CHUNK)
                sm = jnp.dot(wsg, vn_ref[rs, cs], preferred_element_type=F32) + bg
                mix_scr[rs, cs] = (u_ref[rs, cs].astype(F32) * sm).astype(BF16)
        mix_scr[:, A_WIDTH:] = b_ref[...]

    o_ref[...] = x_ref[...] + jnp.dot(mix_scr[...], w_ref[...].astype(BF16), preferred_element_type=F32)


def _mixout(p, b_out, ws_eff, ab_eff, x2d, w_out_l, *, tm, tn=1024):
    n = x2d.shape[0]
    return pl.pallas_call(
        functools.partial(_mixout_kernel, tm=tm),
        grid=(n // tm, D_MODEL // tn),
        in_specs=[
            pl.BlockSpec((tm, A_WIDTH), lambda i, j: (i, COL_U // A_WIDTH)),
            pl.BlockSpec((tm, A_WIDTH), lambda i, j: (i, COL_V // A_WIDTH)),
            pl.BlockSpec((tm, B_WIDTH), lambda i, j: (i, 0)),
            pl.BlockSpec((A_GROUPS, A_CHUNK, A_CHUNK), lambda i, j: (0, 0, 0)),
            pl.BlockSpec((A_CHUNK, A_GROUPS), lambda i, j: (0, 0)),
            pl.BlockSpec((tm, tn), lambda i, j: (i, j)),
            pl.BlockSpec((A_WIDTH + B_WIDTH, tn), lambda i, j: (0, j)),
        ],
        out_specs=pl.BlockSpec((tm, tn), lambda i, j: (i, j)),
        out_shape=jax.ShapeDtypeStruct((n, D_MODEL), F32),
        scratch_shapes=[pltpu.VMEM((tm, A_WIDTH + B_WIDTH), BF16)],
        compiler_params=_cparams(("arbitrary", "arbitrary")),
        name="mixout",
    )(p, p, b_out, ws_eff, ab_eff, x2d, w_out_l)


def _memkv_kernel(x_ref, wk_ref, wv_ref, kn_ref, mk_ref, mv_ref):
    x = x_ref[...].astype(BF16)
    tm = x.shape[0]
    mk = _head_norm(jnp.dot(x, wk_ref[...].astype(BF16), preferred_element_type=F32), kn_ref[...])
    mv = jnp.dot(x, wv_ref[...].astype(BF16), preferred_element_type=F32)
    for hd in range(MEM_HEADS):
        cs = slice(hd * MEM_HEAD_DIM, (hd + 1) * MEM_HEAD_DIM)
        mk_ref[pl.ds(hd, tm, stride=MEM_HEADS), :] = mk[:, cs]
        mv_ref[pl.ds(hd, tm, stride=MEM_HEADS), :] = mv[:, cs]


def _memkv(mem2d, wk, wv, knorm, *, tm=256):
    n = mem2d.shape[0]
    kn = jnp.tile(knorm, MEM_HEADS).reshape(1, MEM_WIDTH)
    return pl.pallas_call(
        _memkv_kernel,
        grid=(n // tm,),
        in_specs=[
            pl.BlockSpec((tm, D_MODEL), lambda i: (i, 0)),
            pl.BlockSpec((D_MODEL, MEM_WIDTH), lambda i: (0, 0)),
            pl.BlockSpec((D_MODEL, MEM_WIDTH), lambda i: (0, 0)),
            pl.BlockSpec((1, MEM_WIDTH), lambda i: (0, 0)),
        ],
        out_specs=[pl.BlockSpec((tm * MEM_HEADS, MEM_HEAD_DIM), lambda i: (i, 0))] * 2,
        out_shape=[jax.ShapeDtypeStruct((n * MEM_HEADS, MEM_HEAD_DIM), F32)] * 2,
        compiler_params=_cparams(("arbitrary",)),
        name="memkv",
    )(mem2d, wk, wv, kn)


def _memattn_kernel(h_ref, g_ref, wq_ref, qn_ref, mk_ref, mv_ref, wo_ref, gf_ref, wc_ref, bc_ref,
                    o_ref, n_ref, c_ref, *, rows_per_mem, mem_len):
    tm, m = h_ref.shape[0], mk_ref.shape[0] // MEM_HEADS
    wq = wq_ref[...].astype(BF16)
    wo = wo_ref[...].astype(BF16)
    mks = [mk_ref[pl.ds(hd, m, stride=MEM_HEADS), :].astype(BF16) for hd in range(MEM_HEADS)]
    mvs = [mv_ref[pl.ds(hd, m, stride=MEM_HEADS), :].astype(BF16) for hd in range(MEM_HEADS)]
    rc = min(tm, MEMATTN_ROW_CHUNK)
    for r in range(tm // rc):
        rs = slice(r * rc, (r + 1) * rc)
        h = h_ref[rs, :]
        n = _row_norm(h, g_ref[...]).astype(BF16)
        q = _head_norm(jnp.dot(n, wq, preferred_element_type=F32), qn_ref[...]).astype(BF16)
        if rows_per_mem is not None:
            row = r * rc + lax.broadcasted_iota(I32, (rc, m), 0)
            col = lax.broadcasted_iota(I32, (rc, m), 1)
            mask = jnp.where(row // rows_per_mem == col // mem_len, 0.0, -jnp.inf)
        outs = []
        for hd in range(MEM_HEADS):
            cs = slice(hd * MEM_HEAD_DIM, (hd + 1) * MEM_HEAD_DIM)
            lg = _dot_nt(q[:, cs], mks[hd]) * (MEM_HEAD_DIM ** -0.5)
            if rows_per_mem is not None:
                lg = lg + mask
            e = jnp.exp(lg - jnp.max(lg, axis=-1, keepdims=True))
            l = jnp.sum(e, axis=-1, keepdims=True)
            outs.append(jnp.dot(e.astype(BF16), mvs[hd], preferred_element_type=F32) / l)
        o = jnp.concatenate(outs, axis=-1).astype(BF16)
        h2 = h + jnp.dot(o, wo, preferred_element_type=F32)
        o_ref[rs, :] = h2
        n_ref[rs, :], c_ref[rs, :] = _route(h2, gf_ref[...], wc_ref[...], bc_ref[...])


def _memattn(h2d, gain, wq, qnorm, mk, mv, wo, gain_ffn, w_cat, b_cat, *, tm, mem_rows, tiles_per_mem,
             rows_per_mem, mem_len):
    n = h2d.shape[0]
    qn = jnp.tile(qnorm, MEM_HEADS).reshape(1, MEM_WIDTH)
    return pl.pallas_call(
        functools.partial(_memattn_kernel, rows_per_mem=rows_per_mem, mem_len=mem_len),
        grid=(n // tm,),
        in_specs=[
            pl.BlockSpec((tm, D_MODEL), lambda i: (i, 0)),
            pl.BlockSpec((1, D_MODEL), lambda i: (0, 0)),
            pl.BlockSpec((D_MODEL, MEM_WIDTH), lambda i: (0, 0)),
            pl.BlockSpec((1, MEM_WIDTH), lambda i: (0, 0)),
            pl.BlockSpec((mem_rows * MEM_HEADS, MEM_HEAD_DIM), lambda i: (i // tiles_per_mem, 0)),
            pl.BlockSpec((mem_rows * MEM_HEADS, MEM_HEAD_DIM), lambda i: (i // tiles_per_mem, 0)),
            pl.BlockSpec((MEM_WIDTH, D_MODEL), lambda i: (0, 0)),
            pl.BlockSpec((1, D_MODEL), lambda i: (0, 0)),
            pl.BlockSpec((D_MODEL, 2 * LANES), lambda i: (0, 0)),
            pl.BlockSpec((1, 2 * LANES), lambda i: (0, 0)),
        ],
        out_specs=[pl.BlockSpec((tm, D_MODEL), lambda i: (i, 0)), pl.BlockSpec((tm, D_MODEL), lambda i: (i, 0)),
                   pl.BlockSpec((tm, LANES), lambda i: (i, 0))],
        out_shape=[jax.ShapeDtypeStruct((n, D_MODEL), F32), jax.ShapeDtypeStruct((n, D_MODEL), BF16),
                   jax.ShapeDtypeStruct((n, LANES), F32)],
        compiler_params=_cparams(("arbitrary",)),
        name="memattn",
    )(h2d, gain.reshape(1, D_MODEL), wq, qn, mk, mv, wo, gain_ffn.reshape(1, D_MODEL), w_cat, b_cat)


def _route(h, gain, w_cat, b_cat):
    n = _row_norm(h, gain).astype(BF16)
    lg = jnp.dot(n, w_cat.astype(BF16), preferred_element_type=F32) + b_cat
    gl, el = lg[:, :LANES], lg[:, LANES:]
    lane = lax.broadcasted_iota(I32, gl.shape, 1)
    lanef = lane.astype(F32)

    def first_max_lane(v, vmax):
        return jnp.min(jnp.where(v == vmax, lanef, float(LANES)), axis=-1, keepdims=True).astype(I32)

    gl = jnp.where(lane < N_GROUPS, gl, -jnp.inf)
    ge = jnp.exp(gl - jnp.max(gl, axis=-1, keepdims=True))
    gp = ge / jnp.sum(ge, axis=-1, keepdims=True)
    g_gate = jnp.max(gp, axis=-1, keepdims=True)
    g_sel = first_max_lane(gp, g_gate)
    in_grp = (lane // EXPERTS_PER_GROUP == g_sel) & (lane < N_EXPERTS)
    el = jnp.where(in_grp, el, -jnp.inf)
    ee = jnp.exp(el - jnp.max(el, axis=-1, keepdims=True))
    ep = jnp.where(in_grp, ee / jnp.sum(ee, axis=-1, keepdims=True), -jnp.inf)
    w1 = jnp.max(ep, axis=-1, keepdims=True)
    i1 = first_max_lane(ep, w1)
    ep2 = jnp.where(lane == i1, -jnp.inf, ep)
    w2 = jnp.max(ep2, axis=-1, keepdims=True)
    i2 = first_max_lane(ep2, w2)
    tot = w1 + w2
    comb = jnp.where(lane == i1, w1 / tot * g_gate, jnp.where(lane == i2, w2 / tot * g_gate, 0.0))
    return n, jnp.where(lane == GSEL_LANE, g_sel.astype(F32), comb)


def _cast_kernel(x_ref, o_ref):
    o_ref[...] = x_ref[...].astype(BF16)


def _cast_bf16(w):
    ne, r, c = w.shape
    return pl.pallas_call(
        _cast_kernel,
        grid=(ne,),
        in_specs=[pl.BlockSpec((None, r, c), lambda e: (e, 0, 0))],
        out_specs=pl.BlockSpec((None, r, c), lambda e: (e, 0, 0)),
        out_shape=jax.ShapeDtypeStruct(w.shape, BF16),
        compiler_params=_cparams(("arbitrary",)),
        name="cast_bf16",
    )(w)


GSEL_LANE = N_EXPERTS
MOE_ROWS = 128
MOE_MAIN_SLACK = 0
MOE_FF_SPLIT = 2


def _moe_routed_kernel(n_ref, c_ref, h_hbm, wg_ref, wu_ref, wd_ref, o_ref,
                       tri_scr, key_scr, xc_scr, yc_scr, cw_scr, nch_ref, sem, *, tm):
    i, g, e, f = (pl.program_id(k) for k in range(4))
    main_rows = tm // N_GROUPS + MOE_MAIN_SLACK
    first_e = (e == 0) & (f == 0)
    last_e = (e == EXPERTS_PER_GROUP - 1) & (f == MOE_FF_SPLIT - 1)

    def onehot_rows(start, rows):
        want = (start + lax.broadcasted_iota(I32, (rows, 1), 0)).astype(F32)
        return jnp.where(key_scr[pl.ds(g, 1), :] == want, 1.0, 0.0).astype(BF16)

    def tail_start(c):
        return pl.multiple_of(main_rows + c * MOE_ROWS, math.gcd(main_rows, MOE_ROWS))

    @pl.when((i == 0) & (g == 0) & first_e)
    def _():
        r = lax.broadcasted_iota(I32, (tm, tm), 0)
        c = lax.broadcasted_iota(I32, (tm, tm), 1)
        tri_scr[...] = jnp.where(r < c, 1.0, 0.0).astype(BF16)

    def residual_copy():
        return pltpu.make_async_copy(h_hbm.at[pl.ds(pl.multiple_of(i * tm, tm), tm)], o_ref, sem)

    @pl.when((g == 0) & first_e)
    def _():
        residual_copy().start()
        lane = lax.broadcasted_iota(I32, (tm, LANES), 1).astype(F32)
        onehot = jnp.where(lane == c_ref[:, GSEL_LANE:GSEL_LANE + 1], 1.0, 0.0).astype(BF16)
        er = lax.broadcasted_iota(I32, (SUBLANES, LANES), 0)
        ec = lax.broadcasted_iota(I32, (SUBLANES, LANES), 1)
        eye = jnp.where(er == ec, 1.0, 0.0).astype(BF16)
        mask_t = _dot_nt(eye, onehot)
        rank_t = jnp.dot(mask_t.astype(BF16), tri_scr[...], preferred_element_type=F32)
        key_scr[...] = jnp.where(mask_t > 0.5, rank_t, -1.0)
        for gg in range(N_GROUPS):
            cnt = jnp.sum(mask_t[gg:gg + 1, :], axis=-1, keepdims=True)
            extra = jnp.ceil(jnp.maximum(cnt - main_rows, 0.0) * (1.0 / MOE_ROWS))
            nch_ref[gg] = jnp.sum(extra).astype(I32)

    ntail = nch_ref[g]

    @pl.when(first_e)
    def _():
        comb = c_ref[...]
        hi = comb.astype(BF16)
        r1 = comb - hi.astype(F32)
        mid = r1.astype(BF16)
        lo = (r1 - mid.astype(F32)).astype(BF16)

        def compact(start, rows):
            s = onehot_rows(start, rows)
            xc_scr[pl.ds(start, rows), :] = jnp.dot(s, n_ref[...], preferred_element_type=F32).astype(BF16)
            cw_scr[pl.ds(start, rows), :] = (jnp.dot(s, hi, preferred_element_type=F32)
                                             + jnp.dot(s, mid, preferred_element_type=F32)
                                             + jnp.dot(s, lo, preferred_element_type=F32))
            yc_scr[pl.ds(start, rows), :] = jnp.zeros((rows, D_MODEL), F32)

        compact(0, main_rows)
        lax.fori_loop(0, ntail, lambda c, carry: (compact(tail_start(c), MOE_ROWS), carry)[1], 0)

    wg = wg_ref[...].astype(BF16)
    wu = wu_ref[...].astype(BF16)
    wd = wd_ref[...].astype(BF16)
    ex = g * EXPERTS_PER_GROUP + e

    def expert(start, rows):
        x = xc_scr[pl.ds(start, rows), :]
        a = jnp.dot(x, wg, preferred_element_type=F32)
        u = jnp.dot(x, wu, preferred_element_type=F32)
        lane = lax.broadcasted_iota(I32, (rows, LANES), 1)
        cw = jnp.sum(jnp.where(lane == ex, cw_scr[pl.ds(start, rows), :], 0.0), axis=-1, keepdims=True)
        hid = (a * jax.nn.sigmoid(a) * u * cw).astype(BF16)
        yc_scr[pl.ds(start, rows), :] += jnp.dot(hid, wd, preferred_element_type=F32)

    expert(0, main_rows)
    lax.fori_loop(0, ntail, lambda c, carry: (expert(tail_start(c), MOE_ROWS), carry)[1], 0)

    @pl.when((g == 0) & last_e)
    def _():
        residual_copy().wait()

    @pl.when(last_e)
    def _():
        def scatter(start, rows):
            s = onehot_rows(start, rows)
            y = yc_scr[pl.ds(start, rows), :].astype(BF16)
            o_ref[...] += lax.dot_general(s, y, (((0,), (0,)), ((), ())), preferred_element_type=F32)

        scatter(0, main_rows)
        lax.fori_loop(0, ntail, lambda c, carry: (scatter(tail_start(c), MOE_ROWS), carry)[1], 0)


def _moe_routed(n3, comb, h2d, w_gate_l, w_up_l, w_down_l, *, tm):
    n = h2d.shape[0]
    ff = EXPERT_FF // MOE_FF_SPLIT

    def wmap(i, g, e, f):
        return (g * EXPERTS_PER_GROUP + e, 0, f)

    return pl.pallas_call(
        functools.partial(_moe_routed_kernel, tm=tm),
        grid=(n // tm, N_GROUPS, EXPERTS_PER_GROUP, MOE_FF_SPLIT),
        in_specs=[
            pl.BlockSpec((tm, D_MODEL), lambda i, g, e, f: (i, 0), pipeline_mode=pl.Buffered(1)),
            pl.BlockSpec((tm, LANES), lambda i, g, e, f: (i, 0), pipeline_mode=pl.Buffered(1)),
            pl.BlockSpec(memory_space=pl.ANY),
            pl.BlockSpec((None, D_MODEL, ff), wmap),
            pl.BlockSpec((None, D_MODEL, ff), wmap),
            pl.BlockSpec((None, ff, D_MODEL), lambda i, g, e, f: (g * EXPERTS_PER_GROUP + e, f, 0)),
        ],
        out_specs=pl.BlockSpec((tm, D_MODEL), lambda i, g, e, f: (i, 0)),
        out_shape=jax.ShapeDtypeStruct((n, D_MODEL), F32),
        scratch_shapes=[
            pltpu.VMEM((tm, tm), BF16),
            pltpu.VMEM((SUBLANES, tm), F32),
            pltpu.VMEM((tm + MOE_ROWS, D_MODEL), BF16),
            pltpu.VMEM((tm + MOE_ROWS, D_MODEL), F32),
            pltpu.VMEM((tm + MOE_ROWS, LANES), F32),
            pltpu.SMEM((N_GROUPS,), I32),
            pltpu.SemaphoreType.DMA(()),
        ],
        compiler_params=pltpu.CompilerParams(
            dimension_semantics=("arbitrary",) * 4, vmem_limit_bytes=MOE_VMEM_LIMIT),
        name="moe_routed",
    )(n3, comb, h2d, w_gate_l, w_up_l, w_down_l)


def _moe_kernel(n_ref, c_ref, h_ref, wg_ref, wu_ref, wd_ref, o_ref):
    e = pl.program_id(1)

    @pl.when(e == 0)
    def _():
        o_ref[...] = h_ref[...]

    x = n_ref[...]
    a = jnp.dot(x, wg_ref[...].astype(BF16), preferred_element_type=F32)
    u = jnp.dot(x, wu_ref[...].astype(BF16), preferred_element_type=F32)
    lane = lax.broadcasted_iota(I32, c_ref.shape, 1)
    c = jnp.sum(jnp.where(lane == e, c_ref[...], 0.0), axis=-1, keepdims=True)
    hid = (a * jax.nn.sigmoid(a) * u * c).astype(BF16)
    o_ref[...] += jnp.dot(hid, wd_ref[...].astype(BF16), preferred_element_type=F32)


def _moe(n3, comb, h2d, w_gate_l, w_up_l, w_down_l, *, tm):
    n = h2d.shape[0]
    return pl.pallas_call(
        _moe_kernel,
        grid=(n // tm, N_EXPERTS),
        in_specs=[
            pl.BlockSpec((tm, D_MODEL), lambda i, e: (i, 0)),
            pl.BlockSpec((tm, LANES), lambda i, e: (i, 0)),
            pl.BlockSpec((tm, D_MODEL), lambda i, e: (i, 0)),
            pl.BlockSpec((None, D_MODEL, EXPERT_FF), lambda i, e: (e, 0, 0)),
            pl.BlockSpec((None, D_MODEL, EXPERT_FF), lambda i, e: (e, 0, 0)),
            pl.BlockSpec((None, EXPERT_FF, D_MODEL), lambda i, e: (e, 0, 0)),
        ],
        out_specs=pl.BlockSpec((tm, D_MODEL), lambda i, e: (i, 0)),
        out_shape=jax.ShapeDtypeStruct((n, D_MODEL), F32),
        compiler_params=_cparams(("arbitrary", "arbitrary")),
        name="moe",
    )(n3, comb, h2d, w_gate_l, w_up_l, w_down_l)


def _post_mixer(p, b_out, ws_eff, ab_eff, x2d, mk, mv, lw, *, tm_mix, tm_mem, mem_rows, tiles_per_mem,
                rows_per_mem, mem_len, tm_moe):
    h1 = _mixout(p, b_out, ws_eff, ab_eff, x2d, lw["w_out"], tm=tm_mix)
    h2, n3, comb = _memattn(h1, lw["norm_mem"], lw["mem_wq"], lw["mem_qnorm"], mk, mv, lw["mem_wo"],
                            lw["norm_ffn"], lw["w_cat"], lw["b_cat"], tm=tm_mem, mem_rows=mem_rows,
                            tiles_per_mem=tiles_per_mem, rows_per_mem=rows_per_mem, mem_len=mem_len)
    moe = _moe_routed if tm_moe > N_GROUPS * MOE_ROWS else _moe
    return moe(n3, comb, h2, lw["w_gate"], lw["w_up"], lw["w_down"], tm=tm_moe)


def kernel(x_prompt, x_sample, cache_k, cache_v, cache_idx_k, cache_mem_k, cache_mem_v, page_table,
           mem_prompt, norm_mix, w_in, a_vnorm, a_ws, a_b, b_qnorm, b_knorm, rel_bias, w_out,
           norm_mem, mem_wq, mem_wk, mem_wv, mem_qnorm, mem_knorm, mem_wo, norm_ffn,
           w_group, b_group, w_router, b_router, w_gate, w_up, w_down):
    bp, t, d = x_prompt.shape
    bd, s, _ = x_sample.shape
    depth = w_in.shape[0]
    mem_len = mem_prompt.shape[1]
    assert d == D_MODEL and t % KEY_CHUNK == 0 and s <= 8 and (bd * s) % LANES == 0

    bias_pairs, sbias, sconst = _bias_tables(rel_bias)
    hp = x_prompt.reshape(bp * t, d)
    hs = x_sample.reshape(bd * s, d)
    outs = [[] for _ in range(9)]
    for l in range(depth):
        w_cat = jnp.zeros((d, 2 * LANES), F32)
        w_cat = w_cat.at[:, :N_GROUPS].set(w_group[l]).at[:, LANES:LANES + N_EXPERTS].set(w_router[l])
        b_cat = jnp.zeros((1, 2 * LANES), F32)
        b_cat = b_cat.at[0, :N_GROUPS].set(b_group[l]).at[0, LANES:LANES + N_EXPERTS].set(b_router[l])
        lw = dict(w_out=w_out[l], norm_mem=norm_mem[l], mem_wq=mem_wq[l], mem_qnorm=mem_qnorm[l],
                  mem_wo=mem_wo[l], norm_ffn=norm_ffn[l], w_cat=w_cat, b_cat=b_cat)

        p, k, v, tail = _proj(hp, norm_mix[l], w_in[l], a_vnorm[l], b_qnorm[l], b_knorm[l],
                              tm=1024, want_vn=False)
        later_weights = (w_gate[l], w_up[l], w_down[l], w_out[l][None])
        fuse_cast = (bp * (t // Q_BLOCK)) % N_EXPERTS == 0
        b_out, casted = _dsa_prompt(p, tail, bias_pairs, later_weights if fuse_cast else (), batch=bp, seq=t)
        if not fuse_cast:
            casted = tuple(_cast_bf16(w) for w in later_weights)
        lw.update(w_gate=casted[0], w_up=casted[1], w_down=casted[2], w_out=casted[3][0])
        mk, mv = _memkv(mem_prompt.reshape(bp * mem_len, d), mem_wk[l], mem_wv[l], mem_knorm[l])
        hp = _post_mixer(p, b_out, a_ws[l], a_b[l].T, hp, mk, mv, lw, tm_mix=1024, tm_mem=512,
                         mem_rows=mem_len, tiles_per_mem=t // 512, rows_per_mem=None, mem_len=mem_len,
                         tm_moe=1024)
        outs[0].append(k.reshape(bp, t, B_KV_HEADS, B_HEAD_DIM))
        outs[1].append(v.reshape(bp, t, B_KV_HEADS, B_HEAD_DIM))
        outs[2].append(tail[:, :IDX_DIM].reshape(bp, t, IDX_DIM))
        outs[3].append(mk.reshape(bp, mem_len, MEM_HEADS, MEM_HEAD_DIM))
        outs[4].append(mv.reshape(bp, mem_len, MEM_HEADS, MEM_HEAD_DIM))

        ns = bd * s
        p_s, k_s, v_s, tail_s, vn_s = _proj(hs, norm_mix[l], w_in[l], a_vnorm[l], b_qnorm[l], b_knorm[l],
                                            tm=ns, want_vn=True)
        b_out_s = _dsa_sample(p_s, tail_s, k_s.reshape(ns, KV_WIDTH), v_s.reshape(ns, KV_WIDTH),
                              cache_k[l], cache_v[l], cache_idx_k[l], page_table,
                              sbias, sconst, dec_batch=bd, dec_seq=s)
        ws_s = jnp.einsum("ab,gts->gatbs", jnp.eye(ns // s, dtype=F32), a_ws[l][:, :s, :s])
        ws_s = ws_s.reshape(A_GROUPS, ns, ns)
        ab_s = jnp.tile(a_b[l][:, :s].T, (ns // s, 1))
        mem_tile = 32
        hs = _post_mixer(p_s, b_out_s, ws_s, ab_s, hs,
                         cache_mem_k[l].reshape(bd * mem_len * MEM_HEADS, MEM_HEAD_DIM),
                         cache_mem_v[l].reshape(bd * mem_len * MEM_HEADS, MEM_HEAD_DIM), lw,
                         tm_mix=ns, tm_mem=mem_tile, mem_rows=mem_tile // s * mem_len, tiles_per_mem=1,
                         rows_per_mem=s, mem_len=mem_len, tm_moe=ns)
        outs[5].append(k_s.reshape(bd, s, B_KV_HEADS, B_HEAD_DIM))
        outs[6].append(v_s.reshape(bd, s, B_KV_HEADS, B_HEAD_DIM))
        outs[7].append(tail_s[:, :IDX_DIM].reshape(bd, s, IDX_DIM))
        outs[8].append(vn_s.reshape(bd, s, A_GROUPS, A_CH))
    return (hp.reshape(bp, t, d), hs.reshape(bd, s, d)) + tuple(jnp.stack(o) for o in outs)
```

```python
import functools
import math

import jax
import jax.numpy as jnp
from jax import lax
from jax.experimental import pallas as pl
from jax.experimental.pallas import tpu as pltpu

BF16 = jnp.bfloat16
F32 = jnp.float32
I32 = jnp.int32

D_MODEL = 2048
A_GROUPS = 8
A_CH = 128
A_WIDTH = A_GROUPS * A_CH
A_CHUNK = 128
B_HEADS = 8
B_HEAD_DIM = 128
B_KV_HEADS = 2
B_GQA = B_HEADS // B_KV_HEADS
B_WIDTH = B_HEADS * B_HEAD_DIM
KV_WIDTH = B_KV_HEADS * B_HEAD_DIM
IDX_HEADS = 16
IDX_DIM = 64
TOPK_MAX = 256
Q_BLOCK = 128
REL_BUCKETS = 32
REL_MAX_DIST = 128
MEM_HEADS = 4
MEM_HEAD_DIM = 128
MEM_WIDTH = MEM_HEADS * MEM_HEAD_DIM
N_GROUPS = 4
EXPERTS_PER_GROUP = 4
N_EXPERTS = N_GROUPS * EXPERTS_PER_GROUP
EXPERT_FF = 512
PAGE_SIZE = 128
EPS = 1e-6

COL_U = 0
COL_V = COL_U + A_WIDTH
COL_Q = COL_V + A_WIDTH
COL_K = COL_Q + B_WIDTH
COL_VV = COL_K + KV_WIDTH
COL_QI = COL_VV + KV_WIDTH
COL_MAIN = COL_QI + IDX_HEADS * IDX_DIM
TAIL_COLS = IDX_DIM + IDX_HEADS
LANES = 128
SUBLANES = 8
PROJ_TN = 512
PROJ_ROW_CHUNK = 256
MEMATTN_ROW_CHUNK = 512
KEY_CHUNK = 256
NEG_INIT = -1e30
VMEM_LIMIT = 56 * 1024 * 1024
MOE_VMEM_LIMIT = 60 * 1024 * 1024


def _cparams(sem):
    return pltpu.CompilerParams(dimension_semantics=sem, vmem_limit_bytes=VMEM_LIMIT)


def _gelu(x):
    return 0.5 * x * (1.0 + lax.erf(x * (2.0 ** -0.5)))


def _head_norm(a, gain, width=LANES):
    outs = []
    for c in range(a.shape[1] // width):
        blk = a[:, c * width:(c + 1) * width]
        ms = jnp.mean(blk * blk, axis=-1, keepdims=True)
        outs.append(blk * lax.rsqrt(ms + EPS) * gain[:, c * width:(c + 1) * width])
    return outs[0] if len(outs) == 1 else jnp.concatenate(outs, axis=-1)


def _row_norm(x, gain):
    ms = jnp.mean(x * x, axis=-1, keepdims=True)
    return x * lax.rsqrt(ms + EPS) * gain


def _dot_nt(a, b):
    return lax.dot_general(a, b, (((1,), (1,)), ((), ())), preferred_element_type=F32)


def _proj_kernel(x_ref, g_ref, w_ref, wt_ref, avn_ref, qn_ref, kn_ref, *rest, want_vn):
    if want_vn:
        p_ref, k_ref, v_ref, tail_ref, vn_ref, xn_scr = rest
    else:
        p_ref, k_ref, v_ref, tail_ref, xn_scr = rest
        vn_ref = None
    j = pl.program_id(1)
    ju = COL_V // PROJ_TN
    jv = COL_Q // PROJ_TN
    jq = COL_K // PROJ_TN
    jkv = COL_QI // PROJ_TN

    @pl.when(j == 0)
    def _():
        xn = _row_norm(x_ref[...], g_ref[...]).astype(BF16)
        xn_scr[...] = xn
        wrow = lax.broadcasted_iota(I32, wt_ref.shape, 0)
        wt = jnp.where(wrow < TAIL_COLS, wt_ref[...], 0.0).astype(BF16)
        t = _dot_nt(xn, wt)
        lane = lax.broadcasted_iota(I32, t.shape, 1)
        tail_ref[...] = jnp.where(lane >= IDX_DIM, t * (IDX_HEADS ** -0.5), t)

    tm = xn_scr.shape[0]
    rc = min(tm, PROJ_ROW_CHUNK)

    def row_chunks(epilogue):
        w = w_ref[...].astype(BF16)
        for r in range(tm // rc):
            rs = slice(r * rc, (r + 1) * rc)
            epilogue(_dot_nt(xn_scr[rs, :], w), rs)

    @pl.when(j < ju)
    def _():
        def epi(acc, rs):
            p_ref[rs, :] = _gelu(acc).astype(BF16)
        row_chunks(epi)

    @pl.when((j >= ju) & (j < jv))
    def _():
        def epi(acc, rs):
            vn = _head_norm(_gelu(acc), avn_ref[...])
            p_ref[rs, :] = vn.astype(BF16)
            if vn_ref is not None:
                vn_ref[rs, :] = vn
        row_chunks(epi)

    @pl.when((j >= jv) & (j < jq))
    def _():
        def epi(acc, rs):
            p_ref[rs, :] = _head_norm(acc, qn_ref[...]).astype(BF16)
        row_chunks(epi)

    @pl.when(j == jq)
    def _():
        def epi(acc, rs):
            k = _head_norm(acc[:, :KV_WIDTH], kn_ref[...])
            v = acc[:, KV_WIDTH:]
            for g in range(B_KV_HEADS):
                dst = pl.ds(rs.start * B_KV_HEADS + g, rs.stop - rs.start, stride=B_KV_HEADS)
                k_ref[dst, :] = k[:, g * B_HEAD_DIM:(g + 1) * B_HEAD_DIM]
                v_ref[dst, :] = v[:, g * B_HEAD_DIM:(g + 1) * B_HEAD_DIM]
            p_ref[rs, :] = jnp.concatenate([k, v], axis=-1).astype(BF16)
        row_chunks(epi)

    @pl.when(j >= jkv)
    def _():
        def epi(acc, rs):
            p_ref[rs, :] = acc.astype(BF16)
        row_chunks(epi)


def _proj(x2d, gain, w_in_l, a_vnorm_l, b_qnorm_l, b_knorm_l, *, tm, want_vn):
    n = x2d.shape[0]
    nj = COL_MAIN // PROJ_TN
    w_in_t = jnp.swapaxes(w_in_l, 0, 1)
    avn = a_vnorm_l.reshape(1, A_WIDTH)
    qn = jnp.tile(b_qnorm_l, PROJ_TN // B_HEAD_DIM).reshape(1, PROJ_TN)
    kn = jnp.tile(b_knorm_l, B_KV_HEADS).reshape(1, KV_WIDTH)
    ju = COL_V // PROJ_TN
    nv = A_WIDTH // PROJ_TN

    def vmap_(i, j):
        return (0, jnp.clip(j - ju, 0, nv - 1))

    in_specs = [
        pl.BlockSpec((tm, D_MODEL), lambda i, j: (i, 0)),
        pl.BlockSpec((1, D_MODEL), lambda i, j: (0, 0)),
        pl.BlockSpec((PROJ_TN, D_MODEL), lambda i, j: (j, 0)),
        pl.BlockSpec((LANES, D_MODEL), lambda i, j: (COL_MAIN // LANES, 0)),
        pl.BlockSpec((1, PROJ_TN), vmap_),
        pl.BlockSpec((1, PROJ_TN), lambda i, j: (0, 0)),
        pl.BlockSpec((1, KV_WIDTH), lambda i, j: (0, 0)),
    ]
    out_shape = [
        jax.ShapeDtypeStruct((n, COL_MAIN), BF16),
        jax.ShapeDtypeStruct((n * B_KV_HEADS, B_HEAD_DIM), F32),
        jax.ShapeDtypeStruct((n * B_KV_HEADS, B_HEAD_DIM), F32),
        jax.ShapeDtypeStruct((n, LANES), F32),
    ]
    out_specs = [
        pl.BlockSpec((tm, PROJ_TN), lambda i, j: (i, j)),
        pl.BlockSpec((tm * B_KV_HEADS, B_HEAD_DIM), lambda i, j: (i, 0)),
        pl.BlockSpec((tm * B_KV_HEADS, B_HEAD_DIM), lambda i, j: (i, 0)),
        pl.BlockSpec((tm, LANES), lambda i, j: (i, 0)),
    ]
    if want_vn:
        out_shape.append(jax.ShapeDtypeStruct((n, A_WIDTH), F32))
        out_specs.append(pl.BlockSpec((tm, PROJ_TN), lambda i, j: (i, jnp.clip(j - ju, 0, nv - 1))))
    return pl.pallas_call(
        functools.partial(_proj_kernel, want_vn=want_vn),
        grid=(n // tm, nj),
        in_specs=in_specs,
        out_specs=out_specs,
        out_shape=out_shape,
        scratch_shapes=[pltpu.VMEM((tm, D_MODEL), BF16)],
        compiler_params=_cparams(("arbitrary", "arbitrary")),
        name="proj",
    )(x2d, gain.reshape(1, D_MODEL), w_in_t, w_in_t, avn, qn, kn)


def _t5_bucket(d):
    max_exact = REL_BUCKETS // 2
    d = jnp.maximum(d, 0)
    ratio = jnp.log(jnp.maximum(d, 1).astype(F32) / max_exact) / math.log(REL_MAX_DIST / max_exact)
    large = jnp.minimum(max_exact + jnp.floor(ratio * (REL_BUCKETS - max_exact)).astype(I32), REL_BUCKETS - 1)
    return jnp.where(d < max_exact, d, large)


def _bias_lookup(bucket, rb_ref, h):
    acc = jnp.zeros(bucket.shape, F32)
    for r in range(REL_BUCKETS):
        acc = jnp.where(bucket == r, rb_ref[r, h], acc)
    return acc


def _bias_kernel(rb_ref, pair_ref, samp_ref, sconst_ref):
    q = lax.broadcasted_iota(I32, (Q_BLOCK, Q_BLOCK), 0)
    kc = lax.broadcasted_iota(I32, (Q_BLOCK, Q_BLOCK), 1)
    nvar = pair_ref.shape[0]
    rels = range(-(nvar - 1), 2)
    buckets = {r: _t5_bucket(q - kc - r * Q_BLOCK) for r in rels}
    for h in range(B_HEADS):
        g, hh = divmod(h, B_GQA)
        tiles = {r: _bias_lookup(buckets[r], rb_ref, h) for r in rels}
        for var in range(nvar):
            r0 = var - (nvar - 1)
            for t in range(2):
                pair_ref[var, g, hh * Q_BLOCK:(hh + 1) * Q_BLOCK, t * Q_BLOCK:(t + 1) * Q_BLOCK] = tiles[r0 + t]
    rows = B_GQA * SUBLANES
    row = lax.broadcasted_iota(I32, (rows, 2 * PAGE_SIZE), 0)
    col = lax.broadcasted_iota(I32, (rows, 2 * PAGE_SIZE), 1)
    t = row % SUBLANES
    d = jnp.where(col < PAGE_SIZE, t + PAGE_SIZE - col, t - (col - PAGE_SIZE))
    bucket = _t5_bucket(d)
    far = jnp.full((rows, LANES), REL_BUCKETS - 1, I32)
    rowc = lax.broadcasted_iota(I32, (rows, LANES), 0)
    for g in range(B_KV_HEADS):
        acc = jnp.zeros((rows, 2 * PAGE_SIZE), F32)
        accc = jnp.zeros((rows, LANES), F32)
        for hh in range(B_GQA):
            h = g * B_GQA + hh
            acc = jnp.where(row // SUBLANES == hh, _bias_lookup(bucket, rb_ref, h), acc)
            accc = jnp.where(rowc // SUBLANES == hh, _bias_lookup(far, rb_ref, h), accc)
        samp_ref[g] = acc
        sconst_ref[g] = accc


N_PAIR_VARIANTS = 4


def _bias_tables(rel_bias):
    rows = B_GQA * SUBLANES
    return pl.pallas_call(
        _bias_kernel,
        in_specs=[pl.BlockSpec(memory_space=pltpu.SMEM)],
        out_shape=[
            jax.ShapeDtypeStruct((N_PAIR_VARIANTS, B_KV_HEADS, B_GQA * Q_BLOCK, KEY_CHUNK), F32),
            jax.ShapeDtypeStruct((B_KV_HEADS, rows, 2 * PAGE_SIZE), F32),
            jax.ShapeDtypeStruct((B_KV_HEADS, rows, LANES), F32),
        ],
        name="bias_tables",
    )(rel_bias)


def _key_to_f32(key):
    bits = jnp.where(key < 0, key ^ I32(-2 ** 31), ~key)
    return lax.bitcast_convert_type(bits, F32)


def _kth_largest(count_ge, shape, k):
    def body(it, prefix):
        bit = 31 - it
        cand = prefix | lax.shift_left(I32(1), bit)
        cnt = count_ge(_key_to_f32(cand))
        return jnp.where(cnt >= k, cand, prefix)

    prefix = lax.fori_loop(0, 32, body, jnp.zeros(shape, I32))
    thr = _key_to_f32(prefix)
    return jnp.where((prefix & I32(-2 ** 23)) == 0, -jnp.inf, thr)


NO_INDEX_BOUND = 2 ** 30
TIE_REPAIR_ROUNDS = 2


def _topk_select(fold_sum, fold_min, shape, k, index_bits):
    thr = _kth_largest(lambda t: fold_sum(lambda sc, kp: sc >= t), shape, k)
    c_ge = fold_sum(lambda sc, kp: sc >= thr)
    no_bound = jnp.full(shape, NO_INDEX_BOUND, I32)

    def repair(_):
        v = thr
        strict = jnp.zeros(shape, F32)

        def in_set(sc, v, strict):
            return (sc > v) | ((sc == v) & (strict < 0.5))

        for _ in range(TIE_REPAIR_ROUNDS):
            c_set = fold_sum(lambda sc, kp: in_set(sc, v, strict))
            vmin = fold_min(lambda sc, kp: jnp.where(in_set(sc, v, strict), sc, jnp.inf))
            c_gt = fold_sum(lambda sc, kp: sc > vmin)
            drop = (c_set > k) & (c_gt >= k)
            v = jnp.where(drop, vmin, v)
            strict = jnp.where(drop, 1.0, strict)
        c_set = fold_sum(lambda sc, kp: in_set(sc, v, strict))
        vt = fold_min(lambda sc, kp: jnp.where(in_set(sc, v, strict), sc, jnp.inf))
        need = k - fold_sum(lambda sc, kp: sc > vt)

        def idx_body(it, m):
            cand = m | lax.shift_left(I32(1), index_bits - 1 - it)
            below = fold_sum(lambda sc, kp: (sc == vt) & (kp < cand))
            return jnp.where(below < need, cand, m)

        m = lax.fori_loop(0, index_bits, idx_body, jnp.zeros(shape, I32))
        tie = c_set > k
        v_out = jnp.where(tie, vt, v)
        m_out = jnp.where(tie, m, jnp.where(strict > 0.5, -1, no_bound))
        return v_out, m_out

    return lax.cond(jnp.max(c_ge) > k, repair, lambda _: (thr, no_bound), 0)


CHUNKS_PER_TRIP = 4


def _chunk_loop(n, body, init):
    def trip(j, c):
        for k in range(CHUNKS_PER_TRIP):
            c = body(CHUNKS_PER_TRIP * j + k, c)
        return c

    whole = n // CHUNKS_PER_TRIP
    carry = lax.fori_loop(0, whole, trip, init)
    return lax.fori_loop(whole * CHUNKS_PER_TRIP, n, body, carry)


def _dsa_prompt_kernel(q_ref, kv_ref, qia_ref, qib_ref, tailk_ref, tailq_ref, bias_ref, *rest, topk, n_cast):
    cast_in, rest = rest[:n_cast], rest[n_cast:]
    o_ref, rest = rest[0], rest[1:]
    cast_out, rest = rest[:n_cast], rest[n_cast:]
    kil_scr, kir_scr, sc_scr, lg_scr, m_scr, l_scr, acc_scr = rest
    i = pl.program_id(1)

    for src, dst in zip(cast_in, cast_out):
        dst[...] = src[...].astype(BF16)

    @pl.when(i == 0)
    def _():
        ki = tailk_ref[:, :IDX_DIM].astype(BF16)
        z = jnp.zeros_like(ki)
        kil_scr[...] = jnp.concatenate([ki, z], axis=-1)
        kir_scr[...] = jnp.concatenate([z, ki], axis=-1)

    wi = tailq_ref[:, IDX_DIM:IDX_DIM + IDX_HEADS] * (IDX_DIM ** -0.5)
    nk = (i + 2) // 2
    qpos = i * Q_BLOCK + lax.broadcasted_iota(I32, (Q_BLOCK, 1), 0)
    kloc = lax.broadcasted_iota(I32, (1, KEY_CHUNK), 1)
    heads_per_ref = qia_ref.shape[1] // IDX_DIM

    pairs = []
    for hp in range(IDX_HEADS // 2):
        ref = qia_ref if 2 * hp < heads_per_ref else qib_ref
        base = (2 * hp) % heads_per_ref * IDX_DIM
        pairs.append(ref[:, base:base + 2 * IDX_DIM])
    qstack = jnp.concatenate(pairs, axis=0)

    def score_chunk(c, carry):
        off = pl.multiple_of(c * KEY_CHUNK, KEY_CHUNK)
        acc = jnp.zeros((Q_BLOCK, KEY_CHUNK), F32)
        for side, k_scr in ((0, kil_scr), (1, kir_scr)):
            s = _dot_nt(qstack, k_scr[pl.ds(off, KEY_CHUNK), :])
            for hp in range(IDX_HEADS // 2):
                h = 2 * hp + side
                acc = acc + wi[:, h:h + 1] * jnp.maximum(s[hp * Q_BLOCK:(hp + 1) * Q_BLOCK], 0.0)
        kpos = c * KEY_CHUNK + kloc
        sc_scr[c] = jnp.where(kpos <= qpos, acc, -jnp.inf)
        return carry

    _chunk_loop(nk, score_chunk, 0)

    def fold_sum(pred):
        def body(c, cnt):
            m = jnp.where(pred(sc_scr[c], c * KEY_CHUNK + kloc), 1.0, 0.0)
            return cnt + m[:, :LANES] + m[:, LANES:]
        cnt = _chunk_loop(nk, body, jnp.zeros((Q_BLOCK, LANES), F32))
        return jnp.sum(cnt, axis=-1, keepdims=True)

    def fold_min(val):
        def body(c, acc):
            x = val(sc_scr[c], c * KEY_CHUNK + kloc)
            return jnp.minimum(acc, jnp.minimum(x[:, :LANES], x[:, LANES:]))
        acc = lax.fori_loop(0, nk, body, jnp.full((Q_BLOCK, LANES), jnp.inf, F32))
        return jnp.min(acc, axis=-1, keepdims=True)

    index_bits = (kv_ref.shape[0] - 1).bit_length()
    sel_v, sel_m = _topk_select(fold_sum, fold_min, (Q_BLOCK, 1), topk, index_bits)

    def mask_chunk(c, carry):
        kpos = c * KEY_CHUNK + kloc
        sc = sc_scr[c]
        chosen = (sc > sel_v) | ((sc == sel_v) & (kpos <= sel_m))
        sc_scr[c] = jnp.where(chosen & (kpos <= qpos), 0.0, -jnp.inf)
        return carry

    lax.fori_loop(0, nk, mask_chunk, 0)

    scale = B_HEAD_DIM ** -0.5
    rows = B_GQA * Q_BLOCK
    qgs = [jnp.concatenate(
        [q_ref[:, (g * B_GQA + hh) * B_HEAD_DIM:(g * B_GQA + hh + 1) * B_HEAD_DIM] for hh in range(B_GQA)],
        axis=0) for g in range(B_KV_HEADS)]
    m_scr[...] = jnp.full(m_scr.shape, -jnp.inf, F32)
    l_scr[...] = jnp.zeros(l_scr.shape, F32)
    acc_scr[...] = jnp.zeros(acc_scr.shape, F32)

    def logits_chunk(c, carry):
        off = pl.multiple_of(c * KEY_CHUNK, KEY_CHUNK)
        var = jnp.clip(2 * c - i, -(N_PAIR_VARIANTS - 1), 0) + (N_PAIR_VARIANTS - 1)
        sel = sc_scr[c]
        sel = jnp.concatenate([sel] * B_GQA, axis=0)
        for g in range(B_KV_HEADS):
            kc = kv_ref[pl.ds(off, KEY_CHUNK), g * B_HEAD_DIM:(g + 1) * B_HEAD_DIM]
            lg = _dot_nt(qgs[g], kc) * scale + bias_ref[var, g] + sel
            lg_scr[c, g] = lg
            m_scr[g] = jnp.maximum(m_scr[g], jnp.maximum(lg[:, :LANES], lg[:, LANES:]))
        return carry

    _chunk_loop(nk, logits_chunk, 0)
    for g in range(B_KV_HEADS):
        m = jnp.max(m_scr[g], axis=-1, keepdims=True)
        m_scr[g] = jnp.broadcast_to(m, (rows, LANES))

    def pv_chunk(c, carry):
        off = pl.multiple_of(c * KEY_CHUNK, KEY_CHUNK)
        for g in range(B_KV_HEADS):
            vc = kv_ref[pl.ds(off, KEY_CHUNK), KV_WIDTH + g * B_HEAD_DIM:KV_WIDTH + (g + 1) * B_HEAD_DIM]
            mb = m_scr[g]
            p = jnp.exp(lg_scr[c, g] - jnp.concatenate([mb, mb], axis=-1))
            l_scr[g] += p[:, :LANES] + p[:, LANES:]
            acc_scr[g] += jnp.dot(p.astype(BF16), vc, preferred_element_type=F32)
        return carry

    _chunk_loop(nk, pv_chunk, 0)
    for g in range(B_KV_HEADS):
        o = acc_scr[g] / jnp.sum(l_scr[g], axis=-1, keepdims=True)
        for hh in range(B_GQA):
            h = g * B_GQA + hh
            o_ref[:, h * B_HEAD_DIM:(h + 1) * B_HEAD_DIM] = o[hh * Q_BLOCK:(hh + 1) * Q_BLOCK].astype(BF16)


def _dsa_prompt(p, tail, bias_pairs, cast_weights, *, batch, seq):
    topk = min(TOPK_MAX, seq // 4)
    nblk = seq // Q_BLOCK
    qi_w = PROJ_TN
    steps = batch * nblk
    cast_specs, cast_shapes = [], []
    for w in cast_weights:
        ne, r, c = w.shape
        parts = steps // ne
        assert steps == ne * parts and r % (parts * 16) == 0
        spec = pl.BlockSpec((None, r // parts, c),
                            lambda b, i, parts=parts: ((b * nblk + i) // parts, (b * nblk + i) % parts, 0))
        cast_specs.append(spec)
        cast_shapes.append(jax.ShapeDtypeStruct(w.shape, BF16))
    outs = pl.pallas_call(
        functools.partial(_dsa_prompt_kernel, topk=topk, n_cast=len(cast_weights)),
        grid=(batch, nblk),
        in_specs=[
            pl.BlockSpec((Q_BLOCK, B_WIDTH), lambda b, i: (b * nblk + i, COL_Q // B_WIDTH)),
            pl.BlockSpec((seq, 2 * KV_WIDTH), lambda b, i: (b, COL_K // (2 * KV_WIDTH))),
            pl.BlockSpec((Q_BLOCK, qi_w), lambda b, i: (b * nblk + i, COL_QI // qi_w)),
            pl.BlockSpec((Q_BLOCK, qi_w), lambda b, i: (b * nblk + i, COL_QI // qi_w + 1)),
            pl.BlockSpec((seq, LANES), lambda b, i: (b, 0)),
            pl.BlockSpec((Q_BLOCK, LANES), lambda b, i: (b * nblk + i, 0)),
            pl.BlockSpec(bias_pairs.shape, lambda b, i: (0, 0, 0, 0)),
        ] + cast_specs,
        out_specs=[pl.BlockSpec((Q_BLOCK, B_WIDTH), lambda b, i: (b * nblk + i, 0))] + cast_specs,
        out_shape=[jax.ShapeDtypeStruct((batch * seq, B_WIDTH), BF16)] + cast_shapes,
        scratch_shapes=[
            pltpu.VMEM((seq, 2 * IDX_DIM), BF16),
            pltpu.VMEM((seq, 2 * IDX_DIM), BF16),
            pltpu.VMEM((seq // KEY_CHUNK, Q_BLOCK, KEY_CHUNK), F32),
            pltpu.VMEM((seq // KEY_CHUNK, B_KV_HEADS, B_GQA * Q_BLOCK, KEY_CHUNK), F32),
            pltpu.VMEM((B_KV_HEADS, B_GQA * Q_BLOCK, LANES), F32),
            pltpu.VMEM((B_KV_HEADS, B_GQA * Q_BLOCK, LANES), F32),
            pltpu.VMEM((B_KV_HEADS, B_GQA * Q_BLOCK, B_HEAD_DIM), F32),
        ],
        compiler_params=_cparams(("arbitrary", "arbitrary")),
        name="dsa_prompt",
    )(p, p, p, p, tail, tail, bias_pairs, *cast_weights)
    return outs[0], tuple(outs[1:])


PAGES_PER_STEP = 64
SAMPLE_SUB_PAGES = 32
SCORE_KEYS_PER_DOT = 2048
FOLD_VREGS = 16
SELECT_ROWS = 64


def _fetch_pages(pt_ref, sources, bufs, sem, step, n_steps_total, steps_per_sample, pages_per_step):
    slot = step % 2

    def copies(n, slot_, page_of):
        sample = n // steps_per_sample
        first = (n % steps_per_sample) * pages_per_step
        for r in range(pages_per_step):
            page = page_of(sample, first + r)
            for a, (src, buf) in enumerate(zip(sources, bufs)):
                yield pltpu.make_async_copy(src.at[page], buf.at[slot_, r], sem.at[a, slot_])

    def table(sample, idx):
        return pt_ref[sample, idx]

    @pl.when(step == 0)
    def _():
        for cp in copies(step, slot, table):
            cp.start()

    @pl.when(step + 1 < n_steps_total)
    def _():
        for cp in copies(step + 1, 1 - slot, table):
            cp.start()

    for cp in copies(step, slot, lambda sample, idx: 0):
        cp.wait()
    return slot


def _dsa_sample_score_kernel(pt_ref, idx_hbm, qi_ref, wi_ref, kin_ref, sc_ref, scn_ref,
                             page_buf, sem, *, dec_seq, n_pages, n_samples):
    slot = _fetch_pages(pt_ref, [idx_hbm], [page_buf], sem, pl.program_id(0), n_samples, 1, n_pages)
    pages = [page_buf.at[slot, r] for r in range(n_pages)]
    past = n_pages * PAGE_SIZE
    pages_per_dot = SCORE_KEYS_PER_DOT // PAGE_SIZE
    qi = qi_ref[...]
    wi = wi_ref[...] * (IDX_DIM ** -0.5)

    def scores(keys_t):
        r = jnp.maximum(jnp.dot(qi, keys_t, preferred_element_type=F32), 0.0) * wi
        acc = r[0:SUBLANES]
        for j in range(1, IDX_HEADS * dec_seq // SUBLANES):
            acc = acc + r[j * SUBLANES:(j + 1) * SUBLANES]
        shift = dec_seq
        while shift < SUBLANES:
            acc = acc + pltpu.roll(acc, shift, axis=0)
            shift *= 2
        return acc

    for d in range(n_pages // pages_per_dot):
        keys_t = jnp.concatenate(
            [pg[...].astype(BF16) for pg in pages[d * pages_per_dot:(d + 1) * pages_per_dot]], axis=1)
        sc_ref[:, d * SCORE_KEYS_PER_DOT:(d + 1) * SCORE_KEYS_PER_DOT] = scores(keys_t)

    row = lax.broadcasted_iota(I32, (SUBLANES, LANES), 0)
    col = lax.broadcasted_iota(I32, (SUBLANES, LANES), 1)
    scn_ref[...] = jnp.where((col <= row % dec_seq) & (col < dec_seq), scores(kin_ref[...]), -jnp.inf)


def _sample_select_kernel(sc_ref, scn_ref, selv_ref, selm_ref, *, topk):
    rows, past = sc_ref.shape
    col = lax.broadcasted_iota(I32, (1, LANES), 1)

    def pieces():
        yield scn_ref[...], past + col
        for w in range(past // LANES):
            yield sc_ref[:, w * LANES:(w + 1) * LANES], w * LANES + col

    ways = max(1, FOLD_VREGS // (rows // SUBLANES))

    def fold(term, combine):
        parts = [None] * ways
        for n, (sc, kp) in enumerate(pieces()):
            x = term(sc, kp)
            parts[n % ways] = x if parts[n % ways] is None else combine(parts[n % ways], x)
        parts = [p for p in parts if p is not None]
        while len(parts) > 1:
            parts = [combine(parts[j], parts[j + 1]) if j + 1 < len(parts) else parts[j]
                     for j in range(0, len(parts), 2)]
        return parts[0]

    def fold_sum(pred):
        cnt = fold(lambda sc, kp: jnp.where(pred(sc, kp), 1.0, 0.0), jnp.add)
        return jnp.sum(cnt, axis=-1, keepdims=True)

    def fold_min(val):
        return jnp.min(fold(val, jnp.minimum), axis=-1, keepdims=True)

    index_bits = (past + LANES - 1).bit_length()
    sel_v, sel_m = _topk_select(fold_sum, fold_min, (rows, 1), topk, index_bits)
    selv_ref[...] = jnp.broadcast_to(sel_v, (rows, LANES))
    selm_ref[...] = jnp.broadcast_to(sel_m, (rows, LANES))


def _dsa_sample_attend_kernel(pt_ref, ck_hbm, cv_hbm, q_ref, sc_ref, scn_ref, selv_ref, selm_ref, kvn_ref,
                              sbias_ref, sconst_ref, o_ref, kbuf, vbuf, sem, m_scr, l_scr, acc_scr,
                              *, dec_seq, n_steps, n_samples):
    s = pl.program_id(1)
    slot = _fetch_pages(pt_ref, [ck_hbm, cv_hbm], [kbuf, vbuf], sem, pl.program_id(0) * n_steps + s,
                        n_samples * n_steps, n_steps, PAGES_PER_STEP)
    kpages = [kbuf.at[slot, r] for r in range(PAGES_PER_STEP)]
    vpages = [vbuf.at[slot, r] for r in range(PAGES_PER_STEP)]
    chunk = PAGES_PER_STEP * PAGE_SIZE
    rows = B_GQA * SUBLANES
    scale = B_HEAD_DIM ** -0.5

    @pl.when(s == 0)
    def _():
        m_scr[...] = jnp.full(m_scr.shape, NEG_INIT, F32)
        l_scr[...] = jnp.zeros(l_scr.shape, F32)
        acc_scr[...] = jnp.zeros(acc_scr.shape, F32)

    sel_v = selv_ref[:, 0:1]
    sel_m = selm_ref[:, 0:1]

    def chosen(sc, kpos):
        return jnp.where((sc > sel_v) | ((sc == sel_v) & (kpos <= sel_m)), 0.0, -jnp.inf)

    def head_rows(pgs, g):
        return jnp.concatenate(
            [pg[pl.ds(g, PAGE_SIZE, stride=B_KV_HEADS), :].astype(BF16) for pg in pgs], axis=0)

    kcol = lax.broadcasted_iota(I32, (SUBLANES, chunk), 1)
    sel = chosen(sc_ref[...], s * chunk + kcol)
    sel = jnp.concatenate([sel] * B_GQA, axis=0)
    is_last = s == n_steps - 1
    ncol = lax.broadcasted_iota(I32, (SUBLANES, LANES), 1)
    seln = chosen(scn_ref[...], n_steps * chunk + ncol)
    seln = jnp.where(is_last & (scn_ref[...] > -jnp.inf), seln, -jnp.inf)
    seln = jnp.concatenate([seln] * B_GQA, axis=0)

    def partial(lg, v):
        m = jnp.maximum(jnp.max(lg, axis=-1, keepdims=True), NEG_INIT)
        p = jnp.exp(lg - m)
        return m, jnp.sum(p, axis=-1, keepdims=True), jnp.dot(p.astype(BF16), v, preferred_element_type=F32)

    sub = SAMPLE_SUB_PAGES * PAGE_SIZE
    nsub = PAGES_PER_STEP // SAMPLE_SUB_PAGES
    lane_sub = lax.broadcasted_iota(I32, (rows, sub), 1)
    for g in range(B_KV_HEADS):
        qg = q_ref[g]
        far = sconst_ref[g, :, 0:1]
        near = jnp.concatenate([jnp.zeros((rows, sub - PAGE_SIZE), F32), sbias_ref[g, :, :PAGE_SIZE]], axis=-1)
        parts = [(m_scr[g], l_scr[g], acc_scr[g])]
        for j in range(nsub):
            pj = slice(j * SAMPLE_SUB_PAGES, (j + 1) * SAMPLE_SUB_PAGES)
            lg = _dot_nt(qg, head_rows(kpages[pj], g)) * scale + sel[:, j * sub:(j + 1) * sub]
            if j == nsub - 1:
                lg = lg + jnp.where(is_last & (lane_sub >= sub - PAGE_SIZE), near, far)
            else:
                lg = lg + far
            parts.append(partial(lg, head_rows(vpages[pj], g)))
        kn = kvn_ref[:, g * B_HEAD_DIM:(g + 1) * B_HEAD_DIM]
        vn = kvn_ref[:, KV_WIDTH + g * B_HEAD_DIM:KV_WIDTH + (g + 1) * B_HEAD_DIM]
        parts.append(partial(_dot_nt(qg, kn) * scale + sbias_ref[g, :, PAGE_SIZE:] + seln, vn))
        m_new = parts[0][0]
        for m, _, _ in parts[1:]:
            m_new = jnp.maximum(m_new, m)
        l_new = jnp.zeros((rows, 1), F32)
        acc_new = jnp.zeros((rows, B_HEAD_DIM), F32)
        for m, l, acc in parts:
            w = jnp.exp(m - m_new)
            l_new = l_new + w * l
            acc_new = acc_new + w * acc
        m_scr[g] = m_new
        l_scr[g] = l_new
        acc_scr[g] = acc_new

    @pl.when(is_last)
    def _():
        for g in range(B_KV_HEADS):
            o_ref[g] = (acc_scr[g] / l_scr[g]).astype(BF16)


def _dsa_sample(p_s, tail_s, k_s, v_s, cache_k_l, cache_v_l, cache_idx_l, page_table, sbias, sconst,
                *, dec_batch, dec_seq):
    n_pages = page_table.shape[1]
    past = n_pages * PAGE_SIZE
    topk = min(TOPK_MAX, (past + dec_seq) // 4)
    n_steps = n_pages // PAGES_PER_STEP
    chunk = PAGES_PER_STEP * PAGE_SIZE
    rows = B_GQA * SUBLANES
    n_pool = cache_k_l.shape[0]

    assert SUBLANES % dec_seq == 0
    qi = p_s[:, COL_QI:COL_MAIN].reshape(dec_batch, dec_seq, IDX_HEADS, IDX_DIM).transpose(0, 2, 1, 3)
    qi = qi.reshape(dec_batch, IDX_HEADS * dec_seq, IDX_DIM)
    wi = tail_s[:, IDX_DIM:IDX_DIM + IDX_HEADS].reshape(dec_batch, dec_seq, IDX_HEADS).transpose(0, 2, 1)
    wi = wi.reshape(dec_batch, IDX_HEADS * dec_seq, 1)
    ki_new = tail_s[:, :IDX_DIM].astype(BF16).reshape(dec_batch, dec_seq, IDX_DIM)
    ki_new = jnp.pad(ki_new, ((0, 0), (0, LANES - dec_seq), (0, 0))).transpose(0, 2, 1)
    q = p_s[:, COL_Q:COL_K].reshape(dec_batch, dec_seq, B_KV_HEADS, B_GQA, B_HEAD_DIM)
    q = jnp.pad(q.transpose(0, 2, 3, 1, 4), ((0, 0), (0, 0), (0, 0), (0, SUBLANES - dec_seq), (0, 0)))
    q = q.reshape(dec_batch, B_KV_HEADS, rows, B_HEAD_DIM)
    kv_new = jnp.concatenate([k_s, v_s], axis=-1).astype(BF16).reshape(dec_batch, dec_seq, 2 * KV_WIDTH)
    kv_new = jnp.pad(kv_new, ((0, 0), (0, LANES - dec_seq), (0, 0)))

    cache_idx_t = jnp.swapaxes(cache_idx_l, 1, 2)
    stat_spec = pl.BlockSpec((None, SUBLANES, LANES), lambda b, pt: (b, 0, 0))
    sc, scn = pl.pallas_call(
        functools.partial(_dsa_sample_score_kernel, dec_seq=dec_seq, n_pages=n_pages, n_samples=dec_batch),
        grid_spec=pltpu.PrefetchScalarGridSpec(
            num_scalar_prefetch=1,
            grid=(dec_batch,),
            in_specs=[
                pl.BlockSpec(memory_space=pl.ANY),
                pl.BlockSpec((None, IDX_HEADS * dec_seq, IDX_DIM), lambda b, pt: (b, 0, 0)),
                pl.BlockSpec((None, IDX_HEADS * dec_seq, 1), lambda b, pt: (b, 0, 0)),
                pl.BlockSpec((None, IDX_DIM, LANES), lambda b, pt: (b, 0, 0)),
            ],
            out_specs=[pl.BlockSpec((None, SUBLANES, past), lambda b, pt: (b, 0, 0)), stat_spec],
            scratch_shapes=[
                pltpu.VMEM((2, n_pages, IDX_DIM, PAGE_SIZE), F32),
                pltpu.SemaphoreType.DMA((1, 2)),
            ],
        ),
        out_shape=[
            jax.ShapeDtypeStruct((dec_batch, SUBLANES, past), F32),
            jax.ShapeDtypeStruct((dec_batch, SUBLANES, LANES), F32),
        ],
        compiler_params=_cparams(("arbitrary",)),
        name="dsa_sample_score",
    )(page_table, cache_idx_t, qi, wi, ki_new)

    n_rows = dec_batch * SUBLANES
    sel_rows = math.gcd(n_rows, SELECT_ROWS)
    row_spec = pl.BlockSpec((sel_rows, LANES), lambda i: (i, 0))
    selv, selm = pl.pallas_call(
        functools.partial(_sample_select_kernel, topk=topk),
        grid=(n_rows // sel_rows,),
        in_specs=[pl.BlockSpec((sel_rows, past), lambda i: (i, 0)), row_spec],
        out_specs=[row_spec, row_spec],
        out_shape=[jax.ShapeDtypeStruct((n_rows, LANES), F32), jax.ShapeDtypeStruct((n_rows, LANES), I32)],
        compiler_params=_cparams(("arbitrary",)),
        name="sample_select",
    )(sc.reshape(n_rows, past), scn.reshape(n_rows, LANES))
    selv = selv.reshape(dec_batch, SUBLANES, LANES)
    selm = selm.reshape(dec_batch, SUBLANES, LANES)

    ck = cache_k_l.reshape(n_pool, PAGE_SIZE * B_KV_HEADS, B_HEAD_DIM)
    cv = cache_v_l.reshape(n_pool, PAGE_SIZE * B_KV_HEADS, B_HEAD_DIM)
    kv_buf = pltpu.VMEM((2, PAGES_PER_STEP, PAGE_SIZE * B_KV_HEADS, B_HEAD_DIM), F32)
    o = pl.pallas_call(
        functools.partial(_dsa_sample_attend_kernel, dec_seq=dec_seq, n_steps=n_steps, n_samples=dec_batch),
        grid_spec=pltpu.PrefetchScalarGridSpec(
            num_scalar_prefetch=1,
            grid=(dec_batch, n_steps),
            in_specs=[pl.BlockSpec(memory_space=pl.ANY), pl.BlockSpec(memory_space=pl.ANY)] + [
                pl.BlockSpec((None, B_KV_HEADS, rows, B_HEAD_DIM), lambda b, s, pt: (b, 0, 0, 0)),
                pl.BlockSpec((None, SUBLANES, chunk), lambda b, s, pt: (b, 0, s)),
                pl.BlockSpec((None, SUBLANES, LANES), lambda b, s, pt: (b, 0, 0)),
                pl.BlockSpec((None, SUBLANES, LANES), lambda b, s, pt: (b, 0, 0)),
                pl.BlockSpec((None, SUBLANES, LANES), lambda b, s, pt: (b, 0, 0)),
                pl.BlockSpec((None, LANES, 2 * KV_WIDTH), lambda b, s, pt: (b, 0, 0)),
                pl.BlockSpec(sbias.shape, lambda b, s, pt: (0, 0, 0)),
                pl.BlockSpec(sconst.shape, lambda b, s, pt: (0, 0, 0)),
            ],
            out_specs=pl.BlockSpec((None, B_KV_HEADS, rows, B_HEAD_DIM), lambda b, s, pt: (b, 0, 0, 0)),
            scratch_shapes=[
                kv_buf,
                kv_buf,
                pltpu.SemaphoreType.DMA((2, 2)),
                pltpu.VMEM((B_KV_HEADS, rows, 1), F32),
                pltpu.VMEM((B_KV_HEADS, rows, 1), F32),
                pltpu.VMEM((B_KV_HEADS, rows, B_HEAD_DIM), F32),
            ],
        ),
        out_shape=jax.ShapeDtypeStruct((dec_batch, B_KV_HEADS, rows, B_HEAD_DIM), BF16),
        compiler_params=_cparams(("arbitrary", "arbitrary")),
        name="dsa_sample_attend",
    )(page_table, ck, cv, q, sc, scn, selv, selm, kv_new, sbias, sconst)
    o = o.reshape(dec_batch, B_KV_HEADS, B_GQA, SUBLANES, B_HEAD_DIM)[:, :, :, :dec_seq].transpose(0, 3, 1, 2, 4)
    return o.reshape(dec_batch * dec_seq, B_WIDTH)


def _mixout_kernel(u_ref, vn_ref, b_ref, ws_ref, ab_ref, x_ref, w_ref, o_ref, mix_scr, *, tm):
    j = pl.program_id(1)

    @pl.when(j == 0)
    def _():
        row = lax.broadcasted_iota(I32, (A_CHUNK, A_CHUNK), 0)
        col = lax.broadcasted_iota(I32, (A_CHUNK, A_CHUNK), 1)
        for g in range(A_GROUPS):
            wsg = jnp.where(row >= col, ws_ref[g], 0.0).astype(BF16)
            bg = ab_ref[:, g:g + 1]
            cs = slice(g * A_CH, (g + 1) * A_CH)
            for r in range(tm // A_CHUNK):
                rs = slice(r * A_CHUNK, (r + 1) * A_CHUNK)
                sm = jnp.dot(wsg, vn_ref[rs, cs], preferred_element_type=F32) + bg
                mix_scr[rs, cs] = (u_ref[rs, cs].astype(F32) * sm).astype(BF16)
        mix_scr[:, A_WIDTH:] = b_ref[...]

    o_ref[...] = x_ref[...] + jnp.dot(mix_scr[...], w_ref[...].astype(BF16), preferred_element_type=F32)


def _mixout(p, b_out, ws_eff, ab_eff, x2d, w_out_l, *, tm, tn=1024):
    n = x2d.shape[0]
    return pl.pallas_call(
        functools.partial(_mixout_kernel, tm=tm),
        grid=(n // tm, D_MODEL // tn),
        in_specs=[
            pl.BlockSpec((tm, A_WIDTH), lambda i, j: (i, COL_U // A_WIDTH)),
            pl.BlockSpec((tm, A_WIDTH), lambda i, j: (i, COL_V // A_WIDTH)),
            pl.BlockSpec((tm, B_WIDTH), lambda i, j: (i, 0)),
            pl.BlockSpec((A_GROUPS, A_CHUNK, A_CHUNK), lambda i, j: (0, 0, 0)),
            pl.BlockSpec((A_CHUNK, A_GROUPS), lambda i, j: (0, 0)),
            pl.BlockSpec((tm, tn), lambda i, j: (i, j)),
            pl.BlockSpec((A_WIDTH + B_WIDTH, tn), lambda i, j: (0, j)),
        ],
        out_specs=pl.BlockSpec((tm, tn), lambda i, j: (i, j)),
        out_shape=jax.ShapeDtypeStruct((n, D_MODEL), F32),
        scratch_shapes=[pltpu.VMEM((tm, A_WIDTH + B_WIDTH), BF16)],
        compiler_params=_cparams(("arbitrary", "arbitrary")),
        name="mixout",
    )(p, p, b_out, ws_eff, ab_eff, x2d, w_out_l)


def _memkv_kernel(x_ref, wk_ref, wv_ref, kn_ref, mk_ref, mv_ref):
    x = x_ref[...].astype(BF16)
    tm = x.shape[0]
    mk = _head_norm(jnp.dot(x, wk_ref[...].astype(BF16), preferred_element_type=F32), kn_ref[...])
    mv = jnp.dot(x, wv_ref[...].astype(BF16), preferred_element_type=F32)
    for hd in range(MEM_HEADS):
        cs = slice(hd * MEM_HEAD_DIM, (hd + 1) * MEM_HEAD_DIM)
        mk_ref[pl.ds(hd, tm, stride=MEM_HEADS), :] = mk[:, cs]
        mv_ref[pl.ds(hd, tm, stride=MEM_HEADS), :] = mv[:, cs]


def _memkv(mem2d, wk, wv, knorm, *, tm=256):
    n = mem2d.shape[0]
    kn = jnp.tile(knorm, MEM_HEADS).reshape(1, MEM_WIDTH)
    return pl.pallas_call(
        _memkv_kernel,
        grid=(n // tm,),
        in_specs=[
            pl.BlockSpec((tm, D_MODEL), lambda i: (i, 0)),
            pl.BlockSpec((D_MODEL, MEM_WIDTH), lambda i: (0, 0)),
            pl.BlockSpec((D_MODEL, MEM_WIDTH), lambda i: (0, 0)),
            pl.BlockSpec((1, MEM_WIDTH), lambda i: (0, 0)),
        ],
        out_specs=[pl.BlockSpec((tm * MEM_HEADS, MEM_HEAD_DIM), lambda i: (i, 0))] * 2,
        out_shape=[jax.ShapeDtypeStruct((n * MEM_HEADS, MEM_HEAD_DIM), F32)] * 2,
        compiler_params=_cparams(("arbitrary",)),
        name="memkv",
    )(mem2d, wk, wv, kn)


def _memattn_kernel(h_ref, g_ref, wq_ref, qn_ref, mk_ref, mv_ref, wo_ref, gf_ref, wc_ref, bc_ref,
                    o_ref, n_ref, c_ref, *, rows_per_mem, mem_len):
    tm, m = h_ref.shape[0], mk_ref.shape[0] // MEM_HEADS
    wq = wq_ref[...].astype(BF16)
    wo = wo_ref[...].astype(BF16)
    mks = [mk_ref[pl.ds(hd, m, stride=MEM_HEADS), :].astype(BF16) for hd in range(MEM_HEADS)]
    mvs = [mv_ref[pl.ds(hd, m, stride=MEM_HEADS), :].astype(BF16) for hd in range(MEM_HEADS)]
    rc = min(tm, MEMATTN_ROW_CHUNK)
    for r in range(tm // rc):
        rs = slice(r * rc, (r + 1) * rc)
        h = h_ref[rs, :]
        n = _row_norm(h, g_ref[...]).astype(BF16)
        q = _head_norm(jnp.dot(n, wq, preferred_element_type=F32), qn_ref[...]).astype(BF16)
        if rows_per_mem is not None:
            row = r * rc + lax.broadcasted_iota(I32, (rc, m), 0)
            col = lax.broadcasted_iota(I32, (rc, m), 1)
            mask = jnp.where(row // rows_per_mem == col // mem_len, 0.0, -jnp.inf)
        outs = []
        for hd in range(MEM_HEADS):
            cs = slice(hd * MEM_HEAD_DIM, (hd + 1) * MEM_HEAD_DIM)
            lg = _dot_nt(q[:, cs], mks[hd]) * (MEM_HEAD_DIM ** -0.5)
            if rows_per_mem is not None:
                lg = lg + mask
            e = jnp.exp(lg - jnp.max(lg, axis=-1, keepdims=True))
            l = jnp.sum(e, axis=-1, keepdims=True)
            outs.append(jnp.dot(e.astype(BF16), mvs[hd], preferred_element_type=F32) / l)
        o = jnp.concatenate(outs, axis=-1).astype(BF16)
        h2 = h + jnp.dot(o, wo, preferred_element_type=F32)
        o_ref[rs, :] = h2
        n_ref[rs, :], c_ref[rs, :] = _route(h2, gf_ref[...], wc_ref[...], bc_ref[...])


def _memattn(h2d, gain, wq, qnorm, mk, mv, wo, gain_ffn, w_cat, b_cat, *, tm, mem_rows, tiles_per_mem,
             rows_per_mem, mem_len):
    n = h2d.shape[0]
    qn = jnp.tile(qnorm, MEM_HEADS).reshape(1, MEM_WIDTH)
    return pl.pallas_call(
        functools.partial(_memattn_kernel, rows_per_mem=rows_per_mem, mem_len=mem_len),
        grid=(n // tm,),
        in_specs=[
            pl.BlockSpec((tm, D_MODEL), lambda i: (i, 0)),
            pl.BlockSpec((1, D_MODEL), lambda i: (0, 0)),
            pl.BlockSpec((D_MODEL, MEM_WIDTH), lambda i: (0, 0)),
            pl.BlockSpec((1, MEM_WIDTH), lambda i: (0, 0)),
            pl.BlockSpec((mem_rows * MEM_HEADS, MEM_HEAD_DIM), lambda i: (i // tiles_per_mem, 0)),
            pl.BlockSpec((mem_rows * MEM_HEADS, MEM_HEAD_DIM), lambda i: (i // tiles_per_mem, 0)),
            pl.BlockSpec((MEM_WIDTH, D_MODEL), lambda i: (0, 0)),
            pl.BlockSpec((1, D_MODEL), lambda i: (0, 0)),
            pl.BlockSpec((D_MODEL, 2 * LANES), lambda i: (0, 0)),
            pl.BlockSpec((1, 2 * LANES), lambda i: (0, 0)),
        ],
        out_specs=[pl.BlockSpec((tm, D_MODEL), lambda i: (i, 0)), pl.BlockSpec((tm, D_MODEL), lambda i: (i, 0)),
                   pl.BlockSpec((tm, LANES), lambda i: (i, 0))],
        out_shape=[jax.ShapeDtypeStruct((n, D_MODEL), F32), jax.ShapeDtypeStruct((n, D_MODEL), BF16),
                   jax.ShapeDtypeStruct((n, LANES), F32)],
        compiler_params=_cparams(("arbitrary",)),
        name="memattn",
    )(h2d, gain.reshape(1, D_MODEL), wq, qn, mk, mv, wo, gain_ffn.reshape(1, D_MODEL), w_cat, b_cat)


def _route(h, gain, w_cat, b_cat):
    n = _row_norm(h, gain).astype(BF16)
    lg = jnp.dot(n, w_cat.astype(BF16), preferred_element_type=F32) + b_cat
    gl, el = lg[:, :LANES], lg[:, LANES:]
    lane = lax.broadcasted_iota(I32, gl.shape, 1)
    lanef = lane.astype(F32)

    def first_max_lane(v, vmax):
        return jnp.min(jnp.where(v == vmax, lanef, float(LANES)), axis=-1, keepdims=True).astype(I32)

    gl = jnp.where(lane < N_GROUPS, gl, -jnp.inf)
    ge = jnp.exp(gl - jnp.max(gl, axis=-1, keepdims=True))
    gp = ge / jnp.sum(ge, axis=-1, keepdims=True)
    g_gate = jnp.max(gp, axis=-1, keepdims=True)
    g_sel = first_max_lane(gp, g_gate)
    in_grp = (lane // EXPERTS_PER_GROUP == g_sel) & (lane < N_EXPERTS)
    el = jnp.where(in_grp, el, -jnp.inf)
    ee = jnp.exp(el - jnp.max(el, axis=-1, keepdims=True))
    ep = jnp.where(in_grp, ee / jnp.sum(ee, axis=-1, keepdims=True), -jnp.inf)
    w1 = jnp.max(ep, axis=-1, keepdims=True)
    i1 = first_max_lane(ep, w1)
    ep2 = jnp.where(lane == i1, -jnp.inf, ep)
    w2 = jnp.max(ep2, axis=-1, keepdims=True)
    i2 = first_max_lane(ep2, w2)
    tot = w1 + w2
    comb = jnp.where(lane == i1, w1 / tot * g_gate, jnp.where(lane == i2, w2 / tot * g_gate, 0.0))
    return n, jnp.where(lane == GSEL_LANE, g_sel.astype(F32), comb)


def _cast_kernel(x_ref, o_ref):
    o_ref[...] = x_ref[...].astype(BF16)


def _cast_bf16(w):
    ne, r, c = w.shape
    return pl.pallas_call(
        _cast_kernel,
        grid=(ne,),
        in_specs=[pl.BlockSpec((None, r, c), lambda e: (e, 0, 0))],
        out_specs=pl.BlockSpec((None, r, c), lambda e: (e, 0, 0)),
        out_shape=jax.ShapeDtypeStruct(w.shape, BF16),
        compiler_params=_cparams(("arbitrary",)),
        name="cast_bf16",
    )(w)


GSEL_LANE = N_EXPERTS
MOE_ROWS = 128
MOE_MAIN_SLACK = 0
MOE_FF_SPLIT = 1


def _moe_routed_kernel(n_ref, c_ref, h_hbm, wg_ref, wu_ref, wd_ref, o_ref,
                       tri_scr, key_scr, xc_scr, yc_scr, cw_scr, nch_ref, sem, *, tm):
    i, g, e, f = (pl.program_id(k) for k in range(4))
    main_rows = tm // N_GROUPS + MOE_MAIN_SLACK
    first_e = (e == 0) & (f == 0)
    last_e = (e == EXPERTS_PER_GROUP - 1) & (f == MOE_FF_SPLIT - 1)

    def onehot_rows(start, rows):
        want = (start + lax.broadcasted_iota(I32, (rows, 1), 0)).astype(F32)
        return jnp.where(key_scr[pl.ds(g, 1), :] == want, 1.0, 0.0).astype(BF16)

    def tail_start(c):
        return pl.multiple_of(main_rows + c * MOE_ROWS, math.gcd(main_rows, MOE_ROWS))

    @pl.when((i == 0) & (g == 0) & first_e)
    def _():
        r = lax.broadcasted_iota(I32, (tm, tm), 0)
        c = lax.broadcasted_iota(I32, (tm, tm), 1)
        tri_scr[...] = jnp.where(r < c, 1.0, 0.0).astype(BF16)

    def residual_copy():
        return pltpu.make_async_copy(h_hbm.at[pl.ds(pl.multiple_of(i * tm, tm), tm)], o_ref, sem)

    @pl.when((g == 0) & first_e)
    def _():
        residual_copy().start()
        lane = lax.broadcasted_iota(I32, (tm, LANES), 1).astype(F32)
        onehot = jnp.where(lane == c_ref[:, GSEL_LANE:GSEL_LANE + 1], 1.0, 0.0).astype(BF16)
        er = lax.broadcasted_iota(I32, (SUBLANES, LANES), 0)
        ec = lax.broadcasted_iota(I32, (SUBLANES, LANES), 1)
        eye = jnp.where(er == ec, 1.0, 0.0).astype(BF16)
        mask_t = _dot_nt(eye, onehot)
        rank_t = jnp.dot(mask_t.astype(BF16), tri_scr[...], preferred_element_type=F32)
        key_scr[...] = jnp.where(mask_t > 0.5, rank_t, -1.0)
        for gg in range(N_GROUPS):
            cnt = jnp.sum(mask_t[gg:gg + 1, :], axis=-1, keepdims=True)
            extra = jnp.ceil(jnp.maximum(cnt - main_rows, 0.0) * (1.0 / MOE_ROWS))
            nch_ref[gg] = jnp.sum(extra).astype(I32)

    ntail = nch_ref[g]

    @pl.when(first_e)
    def _():
        comb = c_ref[...]
        hi = comb.astype(BF16)
        r1 = comb - hi.astype(F32)
        mid = r1.astype(BF16)
        lo = (r1 - mid.astype(F32)).astype(BF16)

        def compact(start, rows):
            s = onehot_rows(start, rows)
            xc_scr[pl.ds(start, rows), :] = jnp.dot(s, n_ref[...], preferred_element_type=F32).astype(BF16)
            cw_scr[pl.ds(start, rows), :] = (jnp.dot(s, hi, preferred_element_type=F32)
                                             + jnp.dot(s, mid, preferred_element_type=F32)
                                             + jnp.dot(s, lo, preferred_element_type=F32))
            yc_scr[pl.ds(start, rows), :] = jnp.zeros((rows, D_MODEL), F32)

        compact(0, main_rows)
        lax.fori_loop(0, ntail, lambda c, carry: (compact(tail_start(c), MOE_ROWS), carry)[1], 0)

    wg = wg_ref[...].astype(BF16)
    wu = wu_ref[...].astype(BF16)
    wd = wd_ref[...].astype(BF16)
    ex = g * EXPERTS_PER_GROUP + e

    def expert(start, rows):
        x = xc_scr[pl.ds(start, rows), :]
        a = jnp.dot(x, wg, preferred_element_type=F32)
        u = jnp.dot(x, wu, preferred_element_type=F32)
        lane = lax.broadcasted_iota(I32, (rows, LANES), 1)
        cw = jnp.sum(jnp.where(lane == ex, cw_scr[pl.ds(start, rows), :], 0.0), axis=-1, keepdims=True)
        hid = (a * jax.nn.sigmoid(a) * u * cw).astype(BF16)
        yc_scr[pl.ds(start, rows), :] += jnp.dot(hid, wd, preferred_element_type=F32)

    expert(0, main_rows)
    lax.fori_loop(0, ntail, lambda c, carry: (expert(tail_start(c), MOE_ROWS), carry)[1], 0)

    @pl.when((g == 0) & last_e)
    def _():
        residual_copy().wait()

    @pl.when(last_e)
    def _():
        def scatter(start, rows):
            s = onehot_rows(start, rows)
            y = yc_scr[pl.ds(start, rows), :].astype(BF16)
            o_ref[...] += lax.dot_general(s, y, (((0,), (0,)), ((), ())), preferred_element_type=F32)

        scatter(0, main_rows)
        lax.fori_loop(0, ntail, lambda c, carry: (scatter(tail_start(c), MOE_ROWS), carry)[1], 0)


def _moe_routed(n3, comb, h2d, w_gate_l, w_up_l, w_down_l, *, tm):
    n = h2d.shape[0]
    ff = EXPERT_FF // MOE_FF_SPLIT

    def wmap(i, g, e, f):
        return (g * EXPERTS_PER_GROUP + e, 0, f)

    return pl.pallas_call(
        functools.partial(_moe_routed_kernel, tm=tm),
        grid=(n // tm, N_GROUPS, EXPERTS_PER_GROUP, MOE_FF_SPLIT),
        in_specs=[
            pl.BlockSpec((tm, D_MODEL), lambda i, g, e, f: (i, 0), pipeline_mode=pl.Buffered(1)),
            pl.BlockSpec((tm, LANES), lambda i, g, e, f: (i, 0), pipeline_mode=pl.Buffered(1)),
            pl.BlockSpec(memory_space=pl.ANY),
            pl.BlockSpec((None, D_MODEL, ff), wmap),
            pl.BlockSpec((None, D_MODEL, ff), wmap),
            pl.BlockSpec((None, ff, D_MODEL), lambda i, g, e, f: (g * EXPERTS_PER_GROUP + e, f, 0)),
        ],
        out_specs=pl.BlockSpec((tm, D_MODEL), lambda i, g, e, f: (i, 0)),
        out_shape=jax.ShapeDtypeStruct((n, D_MODEL), F32),
        scratch_shapes=[
            pltpu.VMEM((tm, tm), BF16),
            pltpu.VMEM((SUBLANES, tm), F32),
            pltpu.VMEM((tm + MOE_ROWS, D_MODEL), BF16),
            pltpu.VMEM((tm + MOE_ROWS, D_MODEL), F32),
            pltpu.VMEM((tm + MOE_ROWS, LANES), F32),
            pltpu.SMEM((N_GROUPS,), I32),
            pltpu.SemaphoreType.DMA(()),
        ],
        compiler_params=pltpu.CompilerParams(
            dimension_semantics=("arbitrary",) * 4, vmem_limit_bytes=MOE_VMEM_LIMIT),
        name="moe_routed",
    )(n3, comb, h2d, w_gate_l, w_up_l, w_down_l)


def _moe_kernel(n_ref, c_ref, h_ref, wg_ref, wu_ref, wd_ref, o_ref):
    e = pl.program_id(1)

    @pl.when(e == 0)
    def _():
        o_ref[...] = h_ref[...]

    x = n_ref[...]
    a = jnp.dot(x, wg_ref[...].astype(BF16), preferred_element_type=F32)
    u = jnp.dot(x, wu_ref[...].astype(BF16), preferred_element_type=F32)
    lane = lax.broadcasted_iota(I32, c_ref.shape, 1)
    c = jnp.sum(jnp.where(lane == e, c_ref[...], 0.0), axis=-1, keepdims=True)
    hid = (a * jax.nn.sigmoid(a) * u * c).astype(BF16)
    o_ref[...] += jnp.dot(hid, wd_ref[...].astype(BF16), preferred_element_type=F32)


def _moe(n3, comb, h2d, w_gate_l, w_up_l, w_down_l, *, tm):
    n = h2d.shape[0]
    return pl.pallas_call(
        _moe_kernel,
        grid=(n // tm, N_EXPERTS),
        in_specs=[
            pl.BlockSpec((tm, D_MODEL), lambda i, e: (i, 0)),
            pl.BlockSpec((tm, LANES), lambda i, e: (i, 0)),
            pl.BlockSpec((tm, D_MODEL), lambda i, e: (i, 0)),
            pl.BlockSpec((None, D_MODEL, EXPERT_FF), lambda i, e: (e, 0, 0)),
            pl.BlockSpec((None, D_MODEL, EXPERT_FF), lambda i, e: (e, 0, 0)),
            pl.BlockSpec((None, EXPERT_FF, D_MODEL), lambda i, e: (e, 0, 0)),
        ],
        out_specs=pl.BlockSpec((tm, D_MODEL), lambda i, e: (i, 0)),
        out_shape=jax.ShapeDtypeStruct((n, D_MODEL), F32),
        compiler_params=_cparams(("arbitrary", "arbitrary")),
        name="moe",
    )(n3, comb, h2d, w_gate_l, w_up_l, w_down_l)


def _post_mixer(p, b_out, ws_eff, ab_eff, x2d, mk, mv, lw, *, tm_mix, tm_mem, mem_rows, tiles_per_mem,
                rows_per_mem, mem_len, tm_moe):
    h1 = _mixout(p, b_out, ws_eff, ab_eff, x2d, lw["w_out"], tm=tm_mix)
    h2, n3, comb = _memattn(h1, lw["norm_mem"], lw["mem_wq"], lw["mem_qnorm"], mk, mv, lw["mem_wo"],
                            lw["norm_ffn"], lw["w_cat"], lw["b_cat"], tm=tm_mem, mem_rows=mem_rows,
                            tiles_per_mem=tiles_per_mem, rows_per_mem=rows_per_mem, mem_len=mem_len)
    moe = _moe_routed if tm_moe > N_GROUPS * MOE_ROWS else _moe
    return moe(n3, comb, h2, lw["w_gate"], lw["w_up"], lw["w_down"], tm=tm_moe)


def kernel(x_prompt, x_sample, cache_k, cache_v, cache_idx_k, cache_mem_k, cache_mem_v, page_table,
           mem_prompt, norm_mix, w_in, a_vnorm, a_ws, a_b, b_qnorm, b_knorm, rel_bias, w_out,
           norm_mem, mem_wq, mem_wk, mem_wv, mem_qnorm, mem_knorm, mem_wo, norm_ffn,
           w_group, b_group, w_router, b_router, w_gate, w_up, w_down):
    bp, t, d = x_prompt.shape
    bd, s, _ = x_sample.shape
    depth = w_in.shape[0]
    mem_len = mem_prompt.shape[1]
    assert d == D_MODEL and t % KEY_CHUNK == 0 and s <= 8 and (bd * s) % LANES == 0

    bias_pairs, sbias, sconst = _bias_tables(rel_bias)
    hp = x_prompt.reshape(bp * t, d)
    hs = x_sample.reshape(bd * s, d)
    outs = [[] for _ in range(9)]
    for l in range(depth):
        w_cat = jnp.zeros((d, 2 * LANES), F32)
        w_cat = w_cat.at[:, :N_GROUPS].set(w_group[l]).at[:, LANES:LANES + N_EXPERTS].set(w_router[l])
        b_cat = jnp.zeros((1, 2 * LANES), F32)
        b_cat = b_cat.at[0, :N_GROUPS].set(b_group[l]).at[0, LANES:LANES + N_EXPERTS].set(b_router[l])
        lw = dict(w_out=w_out[l], norm_mem=norm_mem[l], mem_wq=mem_wq[l], mem_qnorm=mem_qnorm[l],
                  mem_wo=mem_wo[l], norm_ffn=norm_ffn[l], w_cat=w_cat, b_cat=b_cat)

        p, k, v, tail = _proj(hp, norm_mix[l], w_in[l], a_vnorm[l], b_qnorm[l], b_knorm[l],
                              tm=1024, want_vn=False)
        later_weights = (w_gate[l], w_up[l], w_down[l], w_out[l][None])
        fuse_cast = (bp * (t // Q_BLOCK)) % N_EXPERTS == 0
        b_out, casted = _dsa_prompt(p, tail, bias_pairs, later_weights if fuse_cast else (), batch=bp, seq=t)
        if not fuse_cast:
            casted = tuple(_cast_bf16(w) for w in later_weights)
        lw.update(w_gate=casted[0], w_up=casted[1], w_down=casted[2], w_out=casted[3][0])
        mk, mv = _memkv(mem_prompt.reshape(bp * mem_len, d), mem_wk[l], mem_wv[l], mem_knorm[l])
        hp = _post_mixer(p, b_out, a_ws[l], a_b[l].T, hp, mk, mv, lw, tm_mix=1024, tm_mem=512,
                         mem_rows=mem_len, tiles_per_mem=t // 512, rows_per_mem=None, mem_len=mem_len,
                         tm_moe=1024)
        outs[0].append(k.reshape(bp, t, B_KV_HEADS, B_HEAD_DIM))
        outs[1].append(v.reshape(bp, t, B_KV_HEADS, B_HEAD_DIM))
        outs[2].append(tail[:, :IDX_DIM].reshape(bp, t, IDX_DIM))
        outs[3].append(mk.reshape(bp, mem_len, MEM_HEADS, MEM_HEAD_DIM))
        outs[4].append(mv.reshape(bp, mem_len, MEM_HEADS, MEM_HEAD_DIM))

        ns = bd * s
        p_s, k_s, v_s, tail_s, vn_s = _proj(hs, norm_mix[l], w_in[l], a_vnorm[l], b_qnorm[l], b_knorm[l],
                                            tm=ns, want_vn=True)
        b_out_s = _dsa_sample(p_s, tail_s, k_s.reshape(ns, KV_WIDTH), v_s.reshape(ns, KV_WIDTH),
                              cache_k[l], cache_v[l], cache_idx_k[l], page_table,
                              sbias, sconst, dec_batch=bd, dec_seq=s)
        ws_s = jnp.einsum("ab,gts->gatbs", jnp.eye(ns // s, dtype=F32), a_ws[l][:, :s, :s])
        ws_s = ws_s.reshape(A_GROUPS, ns, ns)
        ab_s = jnp.tile(a_b[l][:, :s].T, (ns // s, 1))
        mem_tile = 32
        hs = _post_mixer(p_s, b_out_s, ws_s, ab_s, hs,
                         cache_mem_k[l].reshape(bd * mem_len * MEM_HEADS, MEM_HEAD_DIM),
                         cache_mem_v[l].reshape(bd * mem_len * MEM_HEADS, MEM_HEAD_DIM), lw,
                         tm_mix=ns, tm_mem=mem_tile, mem_rows=mem_tile // s * mem_len, tiles_per_mem=1,
                         rows_per_mem=s, mem_len=mem_len, tm_moe=ns)
        outs[5].append(k_s.reshape(bd, s, B_KV_HEADS, B_HEAD_DIM))
        outs[6].append(v_s.reshape(bd, s, B_KV_HEADS, B_HEAD_DIM))
        outs[7].append(tail_s[:, :IDX_DIM].reshape(bd, s, IDX_DIM))
        outs[8].append(vn_s.reshape(bd, s, A_GROUPS, A_CH))
    return (hp.reshape(bp, t, d), hs.reshape(bd, s, d)) + tuple(jnp.stack(o) for o in outs)
```

```python
import functools
import math

import jax
import jax.numpy as jnp
from jax import lax
from jax.experimental import pallas as pl
from jax.experimental.pallas import tpu as pltpu

BF16 = jnp.bfloat16
F32 = jnp.float32
I32 = jnp.int32

D_MODEL = 2048
A_GROUPS = 8
A_CH = 128
A_WIDTH = A_GROUPS * A_CH
A_CHUNK = 128
B_HEADS = 8
B_HEAD_DIM = 128
B_KV_HEADS = 2
B_GQA = B_HEADS // B_KV_HEADS
B_WIDTH = B_HEADS * B_HEAD_DIM
KV_WIDTH = B_KV_HEADS * B_HEAD_DIM
IDX_HEADS = 16
IDX_DIM = 64
TOPK_MAX = 256
Q_BLOCK = 128
REL_BUCKETS = 32
REL_MAX_DIST = 128
MEM_HEADS = 4
MEM_HEAD_DIM = 128
MEM_WIDTH = MEM_HEADS * MEM_HEAD_DIM
N_GROUPS = 4
EXPERTS_PER_GROUP = 4
N_EXPERTS = N_GROUPS * EXPERTS_PER_GROUP
EXPERT_FF = 512
PAGE_SIZE = 128
EPS = 1e-6

COL_U = 0
COL_V = COL_U + A_WIDTH
COL_Q = COL_V + A_WIDTH
COL_K = COL_Q + B_WIDTH
COL_VV = COL_K + KV_WIDTH
COL_QI = COL_VV + KV_WIDTH
COL_MAIN = COL_QI + IDX_HEADS * IDX_DIM
TAIL_COLS = IDX_DIM + IDX_HEADS
LANES = 128
SUBLANES = 8
PROJ_TN = 512
PROJ_ROW_CHUNK = 256
MEMATTN_ROW_CHUNK = 512
KEY_CHUNK = 256
NEG_INIT = -1e30
VMEM_LIMIT = 56 * 1024 * 1024
MOE_VMEM_LIMIT = 60 * 1024 * 1024


def _cparams(sem):
    return pltpu.CompilerParams(dimension_semantics=sem, vmem_limit_bytes=VMEM_LIMIT)


def _gelu(x):
    return 0.5 * x * (1.0 + lax.erf(x * (2.0 ** -0.5)))


def _head_norm(a, gain, width=LANES):
    outs = []
    for c in range(a.shape[1] // width):
        blk = a[:, c * width:(c + 1) * width]
        ms = jnp.mean(blk * blk, axis=-1, keepdims=True)
        outs.append(blk * lax.rsqrt(ms + EPS) * gain[:, c * width:(c + 1) * width])
    return outs[0] if len(outs) == 1 else jnp.concatenate(outs, axis=-1)


def _row_norm(x, gain):
    ms = jnp.mean(x * x, axis=-1, keepdims=True)
    return x * lax.rsqrt(ms + EPS) * gain


def _dot_nt(a, b):
    return lax.dot_general(a, b, (((1,), (1,)), ((), ())), preferred_element_type=F32)


def _proj_kernel(x_ref, g_ref, w_ref, wt_ref, avn_ref, qn_ref, kn_ref, *rest, want_vn):
    if want_vn:
        p_ref, k_ref, v_ref, tail_ref, vn_ref, xn_scr = rest
    else:
        p_ref, k_ref, v_ref, tail_ref, xn_scr = rest
        vn_ref = None
    j = pl.program_id(1)
    ju = COL_V // PROJ_TN
    jv = COL_Q // PROJ_TN
    jq = COL_K // PROJ_TN
    jkv = COL_QI // PROJ_TN

    @pl.when(j == 0)
    def _():
        xn = _row_norm(x_ref[...], g_ref[...]).astype(BF16)
        xn_scr[...] = xn
        wrow = lax.broadcasted_iota(I32, wt_ref.shape, 0)
        wt = jnp.where(wrow < TAIL_COLS, wt_ref[...], 0.0).astype(BF16)
        t = _dot_nt(xn, wt)
        lane = lax.broadcasted_iota(I32, t.shape, 1)
        tail_ref[...] = jnp.where(lane >= IDX_DIM, t * (IDX_HEADS ** -0.5), t)

    tm = xn_scr.shape[0]
    rc = min(tm, PROJ_ROW_CHUNK)

    def row_chunks(epilogue):
        w = w_ref[...].astype(BF16)
        for r in range(tm // rc):
            rs = slice(r * rc, (r + 1) * rc)
            epilogue(_dot_nt(xn_scr[rs, :], w), rs)

    @pl.when(j < ju)
    def _():
        def epi(acc, rs):
            p_ref[rs, :] = _gelu(acc).astype(BF16)
        row_chunks(epi)

    @pl.when((j >= ju) & (j < jv))
    def _():
        def epi(acc, rs):
            vn = _head_norm(_gelu(acc), avn_ref[...])
            p_ref[rs, :] = vn.astype(BF16)
            if vn_ref is not None:
                vn_ref[rs, :] = vn
        row_chunks(epi)

    @pl.when((j >= jv) & (j < jq))
    def _():
        def epi(acc, rs):
            p_ref[rs, :] = _head_norm(acc, qn_ref[...]).astype(BF16)
        row_chunks(epi)

    @pl.when(j == jq)
    def _():
        def epi(acc, rs):
            k = _head_norm(acc[:, :KV_WIDTH], kn_ref[...])
            v = acc[:, KV_WIDTH:]
            for g in range(B_KV_HEADS):
                dst = pl.ds(rs.start * B_KV_HEADS + g, rs.stop - rs.start, stride=B_KV_HEADS)
                k_ref[dst, :] = k[:, g * B_HEAD_DIM:(g + 1) * B_HEAD_DIM]
                v_ref[dst, :] = v[:, g * B_HEAD_DIM:(g + 1) * B_HEAD_DIM]
            p_ref[rs, :] = jnp.concatenate([k, v], axis=-1).astype(BF16)
        row_chunks(epi)

    @pl.when(j >= jkv)
    def _():
        def epi(acc, rs):
            p_ref[rs, :] = acc.astype(BF16)
        row_chunks(epi)


def _proj(x2d, gain, w_in_l, a_vnorm_l, b_qnorm_l, b_knorm_l, *, tm, want_vn):
    n = x2d.shape[0]
    nj = COL_MAIN // PROJ_TN
    w_in_t = jnp.swapaxes(w_in_l, 0, 1)
    avn = a_vnorm_l.reshape(1, A_WIDTH)
    qn = jnp.tile(b_qnorm_l, PROJ_TN // B_HEAD_DIM).reshape(1, PROJ_TN)
    kn = jnp.tile(b_knorm_l, B_KV_HEADS).reshape(1, KV_WIDTH)
    ju = COL_V // PROJ_TN
    nv = A_WIDTH // PROJ_TN

    def vmap_(i, j):
        return (0, jnp.clip(j - ju, 0, nv - 1))

    in_specs = [
        pl.BlockSpec((tm, D_MODEL), lambda i, j: (i, 0)),
        pl.BlockSpec((1, D_MODEL), lambda i, j: (0, 0)),
        pl.BlockSpec((PROJ_TN, D_MODEL), lambda i, j: (j, 0)),
        pl.BlockSpec((LANES, D_MODEL), lambda i, j: (COL_MAIN // LANES, 0)),
        pl.BlockSpec((1, PROJ_TN), vmap_),
        pl.BlockSpec((1, PROJ_TN), lambda i, j: (0, 0)),
        pl.BlockSpec((1, KV_WIDTH), lambda i, j: (0, 0)),
    ]
    out_shape = [
        jax.ShapeDtypeStruct((n, COL_MAIN), BF16),
        jax.ShapeDtypeStruct((n * B_KV_HEADS, B_HEAD_DIM), F32),
        jax.ShapeDtypeStruct((n * B_KV_HEADS, B_HEAD_DIM), F32),
        jax.ShapeDtypeStruct((n, LANES), F32),
    ]
    out_specs = [
        pl.BlockSpec((tm, PROJ_TN), lambda i, j: (i, j)),
        pl.BlockSpec((tm * B_KV_HEADS, B_HEAD_DIM), lambda i, j: (i, 0)),
        pl.BlockSpec((tm * B_KV_HEADS, B_HEAD_DIM), lambda i, j: (i, 0)),
        pl.BlockSpec((tm, LANES), lambda i, j: (i, 0)),
    ]
    if want_vn:
        out_shape.append(jax.ShapeDtypeStruct((n, A_WIDTH), F32))
        out_specs.append(pl.BlockSpec((tm, PROJ_TN), lambda i, j: (i, jnp.clip(j - ju, 0, nv - 1))))
    return pl.pallas_call(
        functools.partial(_proj_kernel, want_vn=want_vn),
        grid=(n // tm, nj),
        in_specs=in_specs,
        out_specs=out_specs,
        out_shape=out_shape,
        scratch_shapes=[pltpu.VMEM((tm, D_MODEL), BF16)],
        compiler_params=_cparams(("arbitrary", "arbitrary")),
        name="proj",
    )(x2d, gain.reshape(1, D_MODEL), w_in_t, w_in_t, avn, qn, kn)


def _t5_bucket(d):
    max_exact = REL_BUCKETS // 2
    d = jnp.maximum(d, 0)
    ratio = jnp.log(jnp.maximum(d, 1).astype(F32) / max_exact) / math.log(REL_MAX_DIST / max_exact)
    large = jnp.minimum(max_exact + jnp.floor(ratio * (REL_BUCKETS - max_exact)).astype(I32), REL_BUCKETS - 1)
    return jnp.where(d < max_exact, d, large)


def _bias_lookup(bucket, rb_ref, h):
    acc = jnp.zeros(bucket.shape, F32)
    for r in range(REL_BUCKETS):
        acc = jnp.where(bucket == r, rb_ref[r, h], acc)
    return acc


def _bias_kernel(rb_ref, pair_ref, samp_ref, sconst_ref):
    q = lax.broadcasted_iota(I32, (Q_BLOCK, Q_BLOCK), 0)
    kc = lax.broadcasted_iota(I32, (Q_BLOCK, Q_BLOCK), 1)
    nvar = pair_ref.shape[0]
    rels = range(-(nvar - 1), 2)
    buckets = {r: _t5_bucket(q - kc - r * Q_BLOCK) for r in rels}
    for h in range(B_HEADS):
        g, hh = divmod(h, B_GQA)
        tiles = {r: _bias_lookup(buckets[r], rb_ref, h) for r in rels}
        for var in range(nvar):
            r0 = var - (nvar - 1)
            for t in range(2):
                pair_ref[var, g, hh * Q_BLOCK:(hh + 1) * Q_BLOCK, t * Q_BLOCK:(t + 1) * Q_BLOCK] = tiles[r0 + t]
    rows = B_GQA * SUBLANES
    row = lax.broadcasted_iota(I32, (rows, 2 * PAGE_SIZE), 0)
    col = lax.broadcasted_iota(I32, (rows, 2 * PAGE_SIZE), 1)
    t = row % SUBLANES
    d = jnp.where(col < PAGE_SIZE, t + PAGE_SIZE - col, t - (col - PAGE_SIZE))
    bucket = _t5_bucket(d)
    far = jnp.full((rows, LANES), REL_BUCKETS - 1, I32)
    rowc = lax.broadcasted_iota(I32, (rows, LANES), 0)
    for g in range(B_KV_HEADS):
        acc = jnp.zeros((rows, 2 * PAGE_SIZE), F32)
        accc = jnp.zeros((rows, LANES), F32)
        for hh in range(B_GQA):
            h = g * B_GQA + hh
            acc = jnp.where(row // SUBLANES == hh, _bias_lookup(bucket, rb_ref, h), acc)
            accc = jnp.where(rowc // SUBLANES == hh, _bias_lookup(far, rb_ref, h), accc)
        samp_ref[g] = acc
        sconst_ref[g] = accc


N_PAIR_VARIANTS = 4


def _bias_tables(rel_bias):
    rows = B_GQA * SUBLANES
    return pl.pallas_call(
        _bias_kernel,
        in_specs=[pl.BlockSpec(memory_space=pltpu.SMEM)],
        out_shape=[
            jax.ShapeDtypeStruct((N_PAIR_VARIANTS, B_KV_HEADS, B_GQA * Q_BLOCK, KEY_CHUNK), F32),
            jax.ShapeDtypeStruct((B_KV_HEADS, rows, 2 * PAGE_SIZE), F32),
            jax.ShapeDtypeStruct((B_KV_HEADS, rows, LANES), F32),
        ],
        name="bias_tables",
    )(rel_bias)


def _key_to_f32(key):
    bits = jnp.where(key < 0, key ^ I32(-2 ** 31), ~key)
    return lax.bitcast_convert_type(bits, F32)


def _kth_largest(count_ge, shape, k):
    def body(it, prefix):
        bit = 31 - it
        cand = prefix | lax.shift_left(I32(1), bit)
        cnt = count_ge(_key_to_f32(cand))
        return jnp.where(cnt >= k, cand, prefix)

    prefix = lax.fori_loop(0, 32, body, jnp.zeros(shape, I32))
    thr = _key_to_f32(prefix)
    return jnp.where((prefix & I32(-2 ** 23)) == 0, -jnp.inf, thr)


NO_INDEX_BOUND = 2 ** 30
TIE_REPAIR_ROUNDS = 2


def _topk_select(fold_sum, fold_min, shape, k, index_bits):
    thr = _kth_largest(lambda t: fold_sum(lambda sc, kp: sc >= t), shape, k)
    c_ge = fold_sum(lambda sc, kp: sc >= thr)
    no_bound = jnp.full(shape, NO_INDEX_BOUND, I32)

    def repair(_):
        v = thr
        strict = jnp.zeros(shape, F32)

        def in_set(sc, v, strict):
            return (sc > v) | ((sc == v) & (strict < 0.5))

        for _ in range(TIE_REPAIR_ROUNDS):
            c_set = fold_sum(lambda sc, kp: in_set(sc, v, strict))
            vmin = fold_min(lambda sc, kp: jnp.where(in_set(sc, v, strict), sc, jnp.inf))
            c_gt = fold_sum(lambda sc, kp: sc > vmin)
            drop = (c_set > k) & (c_gt >= k)
            v = jnp.where(drop, vmin, v)
            strict = jnp.where(drop, 1.0, strict)
        c_set = fold_sum(lambda sc, kp: in_set(sc, v, strict))
        vt = fold_min(lambda sc, kp: jnp.where(in_set(sc, v, strict), sc, jnp.inf))
        need = k - fold_sum(lambda sc, kp: sc > vt)

        def idx_body(it, m):
            cand = m | lax.shift_left(I32(1), index_bits - 1 - it)
            below = fold_sum(lambda sc, kp: (sc == vt) & (kp < cand))
            return jnp.where(below < need, cand, m)

        m = lax.fori_loop(0, index_bits, idx_body, jnp.zeros(shape, I32))
        tie = c_set > k
        v_out = jnp.where(tie, vt, v)
        m_out = jnp.where(tie, m, jnp.where(strict > 0.5, -1, no_bound))
        return v_out, m_out

    return lax.cond(jnp.max(c_ge) > k, repair, lambda _: (thr, no_bound), 0)


CHUNKS_PER_TRIP = 4


def _chunk_loop(n, body, init):
    def trip(j, c):
        for k in range(CHUNKS_PER_TRIP):
            c = body(CHUNKS_PER_TRIP * j + k, c)
        return c

    whole = n // CHUNKS_PER_TRIP
    carry = lax.fori_loop(0, whole, trip, init)
    return lax.fori_loop(whole * CHUNKS_PER_TRIP, n, body, carry)


def _dsa_prompt_kernel(q_ref, kv_ref, qia_ref, qib_ref, tailk_ref, tailq_ref, bias_ref, *rest, topk, n_cast):
    cast_in, rest = rest[:n_cast], rest[n_cast:]
    o_ref, rest = rest[0], rest[1:]
    cast_out, rest = rest[:n_cast], rest[n_cast:]
    kil_scr, kir_scr, sc_scr, lg_scr, m_scr, l_scr, acc_scr = rest
    i = pl.program_id(1)

    for src, dst in zip(cast_in, cast_out):
        dst[...] = src[...].astype(BF16)

    @pl.when(i == 0)
    def _():
        ki = tailk_ref[:, :IDX_DIM].astype(BF16)
        z = jnp.zeros_like(ki)
        kil_scr[...] = jnp.concatenate([ki, z], axis=-1)
        kir_scr[...] = jnp.concatenate([z, ki], axis=-1)

    wi = tailq_ref[:, IDX_DIM:IDX_DIM + IDX_HEADS] * (IDX_DIM ** -0.5)
    nk = (i + 2) // 2
    qpos = i * Q_BLOCK + lax.broadcasted_iota(I32, (Q_BLOCK, 1), 0)
    kloc = lax.broadcasted_iota(I32, (1, KEY_CHUNK), 1)
    heads_per_ref = qia_ref.shape[1] // IDX_DIM

    pairs = []
    for hp in range(IDX_HEADS // 2):
        ref = qia_ref if 2 * hp < heads_per_ref else qib_ref
        base = (2 * hp) % heads_per_ref * IDX_DIM
        pairs.append(ref[:, base:base + 2 * IDX_DIM])
    qstack = jnp.concatenate(pairs, axis=0)

    def score_chunk(c, carry):
        off = pl.multiple_of(c * KEY_CHUNK, KEY_CHUNK)
        acc = jnp.zeros((Q_BLOCK, KEY_CHUNK), F32)
        for side, k_scr in ((0, kil_scr), (1, kir_scr)):
            s = _dot_nt(qstack, k_scr[pl.ds(off, KEY_CHUNK), :])
            for hp in range(IDX_HEADS // 2):
                h = 2 * hp + side
                acc = acc + wi[:, h:h + 1] * jnp.maximum(s[hp * Q_BLOCK:(hp + 1) * Q_BLOCK], 0.0)
        kpos = c * KEY_CHUNK + kloc
        sc_scr[c] = jnp.where(kpos <= qpos, acc, -jnp.inf)
        return carry

    _chunk_loop(nk, score_chunk, 0)

    def fold_sum(pred):
        def body(c, cnt):
            m = jnp.where(pred(sc_scr[c], c * KEY_CHUNK + kloc), 1.0, 0.0)
            return cnt + m[:, :LANES] + m[:, LANES:]
        cnt = _chunk_loop(nk, body, jnp.zeros((Q_BLOCK, LANES), F32))
        return jnp.sum(cnt, axis=-1, keepdims=True)

    def fold_min(val):
        def body(c, acc):
            x = val(sc_scr[c], c * KEY_CHUNK + kloc)
            return jnp.minimum(acc, jnp.minimum(x[:, :LANES], x[:, LANES:]))
        acc = lax.fori_loop(0, nk, body, jnp.full((Q_BLOCK, LANES), jnp.inf, F32))
        return jnp.min(acc, axis=-1, keepdims=True)

    index_bits = (kv_ref.shape[0] - 1).bit_length()
    sel_v, sel_m = _topk_select(fold_sum, fold_min, (Q_BLOCK, 1), topk, index_bits)

    def mask_chunk(c, carry):
        kpos = c * KEY_CHUNK + kloc
        sc = sc_scr[c]
        chosen = (sc > sel_v) | ((sc == sel_v) & (kpos <= sel_m))
        sc_scr[c] = jnp.where(chosen & (kpos <= qpos), 0.0, -jnp.inf)
        return carry

    lax.fori_loop(0, nk, mask_chunk, 0)

    scale = B_HEAD_DIM ** -0.5
    rows = B_GQA * Q_BLOCK
    qgs = [jnp.concatenate(
        [q_ref[:, (g * B_GQA + hh) * B_HEAD_DIM:(g * B_GQA + hh + 1) * B_HEAD_DIM] for hh in range(B_GQA)],
        axis=0) for g in range(B_KV_HEADS)]
    m_scr[...] = jnp.full(m_scr.shape, -jnp.inf, F32)
    l_scr[...] = jnp.zeros(l_scr.shape, F32)
    acc_scr[...] = jnp.zeros(acc_scr.shape, F32)

    def logits_chunk(c, carry):
        off = pl.multiple_of(c * KEY_CHUNK, KEY_CHUNK)
        var = jnp.clip(2 * c - i, -(N_PAIR_VARIANTS - 1), 0) + (N_PAIR_VARIANTS - 1)
        sel = sc_scr[c]
        sel = jnp.concatenate([sel] * B_GQA, axis=0)
        for g in range(B_KV_HEADS):
            kc = kv_ref[pl.ds(off, KEY_CHUNK), g * B_HEAD_DIM:(g + 1) * B_HEAD_DIM]
            lg = _dot_nt(qgs[g], kc) * scale + bias_ref[var, g] + sel
            lg_scr[c, g] = lg
            m_scr[g] = jnp.maximum(m_scr[g], jnp.maximum(lg[:, :LANES], lg[:, LANES:]))
        return carry

    _chunk_loop(nk, logits_chunk, 0)
    for g in range(B_KV_HEADS):
        m = jnp.max(m_scr[g], axis=-1, keepdims=True)
        m_scr[g] = jnp.broadcast_to(m, (rows, LANES))

    def pv_chunk(c, carry):
        off = pl.multiple_of(c * KEY_CHUNK, KEY_CHUNK)
        for g in range(B_KV_HEADS):
            vc = kv_ref[pl.ds(off, KEY_CHUNK), KV_WIDTH + g * B_HEAD_DIM:KV_WIDTH + (g + 1) * B_HEAD_DIM]
            mb = m_scr[g]
            p = jnp.exp(lg_scr[c, g] - jnp.concatenate([mb, mb], axis=-1))
            l_scr[g] += p[:, :LANES] + p[:, LANES:]
            acc_scr[g] += jnp.dot(p.astype(BF16), vc, preferred_element_type=F32)
        return carry

    _chunk_loop(nk, pv_chunk, 0)
    for g in range(B_KV_HEADS):
        o = acc_scr[g] / jnp.sum(l_scr[g], axis=-1, keepdims=True)
        for hh in range(B_GQA):
            h = g * B_GQA + hh
            o_ref[:, h * B_HEAD_DIM:(h + 1) * B_HEAD_DIM] = o[hh * Q_BLOCK:(hh + 1) * Q_BLOCK].astype(BF16)


def _dsa_prompt(p, tail, bias_pairs, cast_weights, *, batch, seq):
    topk = min(TOPK_MAX, seq // 4)
    nblk = seq // Q_BLOCK
    qi_w = PROJ_TN
    steps = batch * nblk
    cast_specs, cast_shapes = [], []
    for w in cast_weights:
        ne, r, c = w.shape
        parts = steps // ne
        assert steps == ne * parts and r % (parts * 16) == 0
        spec = pl.BlockSpec((None, r // parts, c),
                            lambda b, i, parts=parts: ((b * nblk + i) // parts, (b * nblk + i) % parts, 0))
        cast_specs.append(spec)
        cast_shapes.append(jax.ShapeDtypeStruct(w.shape, BF16))
    outs = pl.pallas_call(
        functools.partial(_dsa_prompt_kernel, topk=topk, n_cast=len(cast_weights)),
        grid=(batch, nblk),
        in_specs=[
            pl.BlockSpec((Q_BLOCK, B_WIDTH), lambda b, i: (b * nblk + i, COL_Q // B_WIDTH)),
            pl.BlockSpec((seq, 2 * KV_WIDTH), lambda b, i: (b, COL_K // (2 * KV_WIDTH))),
            pl.BlockSpec((Q_BLOCK, qi_w), lambda b, i: (b * nblk + i, COL_QI // qi_w)),
            pl.BlockSpec((Q_BLOCK, qi_w), lambda b, i: (b * nblk + i, COL_QI // qi_w + 1)),
            pl.BlockSpec((seq, LANES), lambda b, i: (b, 0)),
            pl.BlockSpec((Q_BLOCK, LANES), lambda b, i: (b * nblk + i, 0)),
            pl.BlockSpec(bias_pairs.shape, lambda b, i: (0, 0, 0, 0)),
        ] + cast_specs,
        out_specs=[pl.BlockSpec((Q_BLOCK, B_WIDTH), lambda b, i: (b * nblk + i, 0))] + cast_specs,
        out_shape=[jax.ShapeDtypeStruct((batch * seq, B_WIDTH), BF16)] + cast_shapes,
        scratch_shapes=[
            pltpu.VMEM((seq, 2 * IDX_DIM), BF16),
            pltpu.VMEM((seq, 2 * IDX_DIM), BF16),
            pltpu.VMEM((seq // KEY_CHUNK, Q_BLOCK, KEY_CHUNK), F32),
            pltpu.VMEM((seq // KEY_CHUNK, B_KV_HEADS, B_GQA * Q_BLOCK, KEY_CHUNK), F32),
            pltpu.VMEM((B_KV_HEADS, B_GQA * Q_BLOCK, LANES), F32),
            pltpu.VMEM((B_KV_HEADS, B_GQA * Q_BLOCK, LANES), F32),
            pltpu.VMEM((B_KV_HEADS, B_GQA * Q_BLOCK, B_HEAD_DIM), F32),
        ],
        compiler_params=_cparams(("arbitrary", "arbitrary")),
        name="dsa_prompt",
    )(p, p, p, p, tail, tail, bias_pairs, *cast_weights)
    return outs[0], tuple(outs[1:])


PAGES_PER_STEP = 64
SAMPLE_SUB_PAGES = 32
SCORE_KEYS_PER_DOT = 2048
FOLD_VREGS = 16
SELECT_ROWS = 64


def _fetch_pages(pt_ref, sources, bufs, sem, step, n_steps_total, steps_per_sample, pages_per_step):
    slot = step % 2

    def copies(n, slot_, page_of):
        sample = n // steps_per_sample
        first = (n % steps_per_sample) * pages_per_step
        for r in range(pages_per_step):
            page = page_of(sample, first + r)
            for a, (src, buf) in enumerate(zip(sources, bufs)):
                yield pltpu.make_async_copy(src.at[page], buf.at[slot_, r], sem.at[a, slot_])

    def table(sample, idx):
        return pt_ref[sample, idx]

    @pl.when(step == 0)
    def _():
        for cp in copies(step, slot, table):
            cp.start()

    @pl.when(step + 1 < n_steps_total)
    def _():
        for cp in copies(step + 1, 1 - slot, table):
            cp.start()

    for cp in copies(step, slot, lambda sample, idx: 0):
        cp.wait()
    return slot


def _dsa_sample_score_kernel(pt_ref, idx_hbm, qi_ref, wi_ref, kin_ref, sc_ref, scn_ref,
                             page_buf, sem, *, dec_seq, n_pages, n_samples):
    slot = _fetch_pages(pt_ref, [idx_hbm], [page_buf], sem, pl.program_id(0), n_samples, 1, n_pages)
    pages = [page_buf.at[slot, r] for r in range(n_pages)]
    past = n_pages * PAGE_SIZE
    pages_per_dot = SCORE_KEYS_PER_DOT // PAGE_SIZE
    qi = qi_ref[...]
    wi = wi_ref[...] * (IDX_DIM ** -0.5)

    def scores(keys_t):
        r = jnp.maximum(jnp.dot(qi, keys_t, preferred_element_type=F32), 0.0) * wi
        acc = r[0:SUBLANES]
        for j in range(1, IDX_HEADS * dec_seq // SUBLANES):
            acc = acc + r[j * SUBLANES:(j + 1) * SUBLANES]
        shift = dec_seq
        while shift < SUBLANES:
            acc = acc + pltpu.roll(acc, shift, axis=0)
            shift *= 2
        return acc

    for d in range(n_pages // pages_per_dot):
        keys_t = jnp.concatenate(
            [pg[...].astype(BF16) for pg in pages[d * pages_per_dot:(d + 1) * pages_per_dot]], axis=1)
        sc_ref[:, d * SCORE_KEYS_PER_DOT:(d + 1) * SCORE_KEYS_PER_DOT] = scores(keys_t)

    row = lax.broadcasted_iota(I32, (SUBLANES, LANES), 0)
    col = lax.broadcasted_iota(I32, (SUBLANES, LANES), 1)
    scn_ref[...] = jnp.where((col <= row % dec_seq) & (col < dec_seq), scores(kin_ref[...]), -jnp.inf)


def _sample_select_kernel(sc_ref, scn_ref, selv_ref, selm_ref, *, topk):
    rows, past = sc_ref.shape
    col = lax.broadcasted_iota(I32, (1, LANES), 1)

    def pieces():
        yield scn_ref[...], past + col
        for w in range(past // LANES):
            yield sc_ref[:, w * LANES:(w + 1) * LANES], w * LANES + col

    ways = max(1, FOLD_VREGS // (rows // SUBLANES))

    def fold(term, combine):
        parts = [None] * ways
        for n, (sc, kp) in enumerate(pieces()):
            x = term(sc, kp)
            parts[n % ways] = x if parts[n % ways] is None else combine(parts[n % ways], x)
        parts = [p for p in parts if p is not None]
        while len(parts) > 1:
            parts = [combine(parts[j], parts[j + 1]) if j + 1 < len(parts) else parts[j]
                     for j in range(0, len(parts), 2)]
        return parts[0]

    def fold_sum(pred):
        cnt = fold(lambda sc, kp: jnp.where(pred(sc, kp), 1.0, 0.0), jnp.add)
        return jnp.sum(cnt, axis=-1, keepdims=True)

    def fold_min(val):
        return jnp.min(fold(val, jnp.minimum), axis=-1, keepdims=True)

    index_bits = (past + LANES - 1).bit_length()
    sel_v, sel_m = _topk_select(fold_sum, fold_min, (rows, 1), topk, index_bits)
    selv_ref[...] = jnp.broadcast_to(sel_v, (rows, LANES))
    selm_ref[...] = jnp.broadcast_to(sel_m, (rows, LANES))


def _dsa_sample_attend_kernel(pt_ref, ck_hbm, cv_hbm, q_ref, sc_ref, scn_ref, selv_ref, selm_ref, kvn_ref,
                              sbias_ref, sconst_ref, o_ref, kbuf, vbuf, sem, m_scr, l_scr, acc_scr,
                              *, dec_seq, n_steps, n_samples):
    s = pl.program_id(1)
    slot = _fetch_pages(pt_ref, [ck_hbm, cv_hbm], [kbuf, vbuf], sem, pl.program_id(0) * n_steps + s,
                        n_samples * n_steps, n_steps, PAGES_PER_STEP)
    kpages = [kbuf.at[slot, r] for r in range(PAGES_PER_STEP)]
    vpages = [vbuf.at[slot, r] for r in range(PAGES_PER_STEP)]
    chunk = PAGES_PER_STEP * PAGE_SIZE
    rows = B_GQA * SUBLANES
    scale = B_HEAD_DIM ** -0.5

    @pl.when(s == 0)
    def _():
        m_scr[...] = jnp.full(m_scr.shape, NEG_INIT, F32)
        l_scr[...] = jnp.zeros(l_scr.shape, F32)
        acc_scr[...] = jnp.zeros(acc_scr.shape, F32)

    sel_v = selv_ref[:, 0:1]
    sel_m = selm_ref[:, 0:1]

    def chosen(sc, kpos):
        return jnp.where((sc > sel_v) | ((sc == sel_v) & (kpos <= sel_m)), 0.0, -jnp.inf)

    def head_rows(pgs, g):
        return jnp.concatenate(
            [pg[pl.ds(g, PAGE_SIZE, stride=B_KV_HEADS), :].astype(BF16) for pg in pgs], axis=0)

    kcol = lax.broadcasted_iota(I32, (SUBLANES, chunk), 1)
    sel = chosen(sc_ref[...], s * chunk + kcol)
    sel = jnp.concatenate([sel] * B_GQA, axis=0)
    is_last = s == n_steps - 1
    ncol = lax.broadcasted_iota(I32, (SUBLANES, LANES), 1)
    seln = chosen(scn_ref[...], n_steps * chunk + ncol)
    seln = jnp.where(is_last & (scn_ref[...] > -jnp.inf), seln, -jnp.inf)
    seln = jnp.concatenate([seln] * B_GQA, axis=0)

    def partial(lg, v):
        m = jnp.maximum(jnp.max(lg, axis=-1, keepdims=True), NEG_INIT)
        p = jnp.exp(lg - m)
        return m, jnp.sum(p, axis=-1, keepdims=True), jnp.dot(p.astype(BF16), v, preferred_element_type=F32)

    sub = SAMPLE_SUB_PAGES * PAGE_SIZE
    nsub = PAGES_PER_STEP // SAMPLE_SUB_PAGES
    lane_sub = lax.broadcasted_iota(I32, (rows, sub), 1)
    for g in range(B_KV_HEADS):
        qg = q_ref[g]
        far = sconst_ref[g, :, 0:1]
        near = jnp.concatenate([jnp.zeros((rows, sub - PAGE_SIZE), F32), sbias_ref[g, :, :PAGE_SIZE]], axis=-1)
        parts = [(m_scr[g], l_scr[g], acc_scr[g])]
        for j in range(nsub):
            pj = slice(j * SAMPLE_SUB_PAGES, (j + 1) * SAMPLE_SUB_PAGES)
            lg = _dot_nt(qg, head_rows(kpages[pj], g)) * scale + sel[:, j * sub:(j + 1) * sub]
            if j == nsub - 1:
                lg = lg + jnp.where(is_last & (lane_sub >= sub - PAGE_SIZE), near, far)
            else:
                lg = lg + far
            parts.append(partial(lg, head_rows(vpages[pj], g)))
        kn = kvn_ref[:, g * B_HEAD_DIM:(g + 1) * B_HEAD_DIM]
        vn = kvn_ref[:, KV_WIDTH + g * B_HEAD_DIM:KV_WIDTH + (g + 1) * B_HEAD_DIM]
        parts.append(partial(_dot_nt(qg, kn) * scale + sbias_ref[g, :, PAGE_SIZE:] + seln, vn))
        m_new = parts[0][0]
        for m, _, _ in parts[1:]:
            m_new = jnp.maximum(m_new, m)
        l_new = jnp.zeros((rows, 1), F32)
        acc_new = jnp.zeros((rows, B_HEAD_DIM), F32)
        for m, l, acc in parts:
            w = jnp.exp(m - m_new)
            l_new = l_new + w * l
            acc_new = acc_new + w * acc
        m_scr[g] = m_new
        l_scr[g] = l_new
        acc_scr[g] = acc_new

    @pl.when(is_last)
    def _():
        for g in range(B_KV_HEADS):
            o_ref[g] = (acc_scr[g] / l_scr[g]).astype(BF16)


def _dsa_sample(p_s, tail_s, k_s, v_s, cache_k_l, cache_v_l, cache_idx_l, page_table, sbias, sconst,
                *, dec_batch, dec_seq):
    n_pages = page_table.shape[1]
    past = n_pages * PAGE_SIZE
    topk = min(TOPK_MAX, (past + dec_seq) // 4)
    n_steps = n_pages // PAGES_PER_STEP
    chunk = PAGES_PER_STEP * PAGE_SIZE
    rows = B_GQA * SUBLANES
    n_pool = cache_k_l.shape[0]

    assert SUBLANES % dec_seq == 0
    qi = p_s[:, COL_QI:COL_MAIN].reshape(dec_batch, dec_seq, IDX_HEADS, IDX_DIM).transpose(0, 2, 1, 3)
    qi = qi.reshape(dec_batch, IDX_HEADS * dec_seq, IDX_DIM)
    wi = tail_s[:, IDX_DIM:IDX_DIM + IDX_HEADS].reshape(dec_batch, dec_seq, IDX_HEADS).transpose(0, 2, 1)
    wi = wi.reshape(dec_batch, IDX_HEADS * dec_seq, 1)
    ki_new = tail_s[:, :IDX_DIM].astype(BF16).reshape(dec_batch, dec_seq, IDX_DIM)
    ki_new = jnp.pad(ki_new, ((0, 0), (0, LANES - dec_seq), (0, 0))).transpose(0, 2, 1)
    q = p_s[:, COL_Q:COL_K].reshape(dec_batch, dec_seq, B_KV_HEADS, B_GQA, B_HEAD_DIM)
    q = jnp.pad(q.transpose(0, 2, 3, 1, 4), ((0, 0), (0, 0), (0, 0), (0, SUBLANES - dec_seq), (0, 0)))
    q = q.reshape(dec_batch, B_KV_HEADS, rows, B_HEAD_DIM)
    kv_new = jnp.concatenate([k_s, v_s], axis=-1).astype(BF16).reshape(dec_batch, dec_seq, 2 * KV_WIDTH)
    kv_new = jnp.pad(kv_new, ((0, 0), (0, LANES - dec_seq), (0, 0)))

    cache_idx_t = jnp.swapaxes(cache_idx_l, 1, 2)
    stat_spec = pl.BlockSpec((None, SUBLANES, LANES), lambda b, pt: (b, 0, 0))
    sc, scn = pl.pallas_call(
        functools.partial(_dsa_sample_score_kernel, dec_seq=dec_seq, n_pages=n_pages, n_samples=dec_batch),
        grid_spec=pltpu.PrefetchScalarGridSpec(
            num_scalar_prefetch=1,
            grid=(dec_batch,),
            in_specs=[
                pl.BlockSpec(memory_space=pl.ANY),
                pl.BlockSpec((None, IDX_HEADS * dec_seq, IDX_DIM), lambda b, pt: (b, 0, 0)),
                pl.BlockSpec((None, IDX_HEADS * dec_seq, 1), lambda b, pt: (b, 0, 0)),
                pl.BlockSpec((None, IDX_DIM, LANES), lambda b, pt: (b, 0, 0)),
            ],
            out_specs=[pl.BlockSpec((None, SUBLANES, past), lambda b, pt: (b, 0, 0)), stat_spec],
            scratch_shapes=[
                pltpu.VMEM((2, n_pages, IDX_DIM, PAGE_SIZE), F32),
                pltpu.SemaphoreType.DMA((1, 2)),
            ],
        ),
        out_shape=[
            jax.ShapeDtypeStruct((dec_batch, SUBLANES, past), F32),
            jax.ShapeDtypeStruct((dec_batch, SUBLANES, LANES), F32),
        ],
        compiler_params=_cparams(("arbitrary",)),
        name="dsa_sample_score",
    )(page_table, cache_idx_t, qi, wi, ki_new)

    n_rows = dec_batch * SUBLANES
    sel_rows = math.gcd(n_rows, SELECT_ROWS)
    row_spec = pl.BlockSpec((sel_rows, LANES), lambda i: (i, 0))
    selv, selm = pl.pallas_call(
        functools.partial(_sample_select_kernel, topk=topk),
        grid=(n_rows // sel_rows,),
        in_specs=[pl.BlockSpec((sel_rows, past), lambda i: (i, 0)), row_spec],
        out_specs=[row_spec, row_spec],
        out_shape=[jax.ShapeDtypeStruct((n_rows, LANES), F32), jax.ShapeDtypeStruct((n_rows, LANES), I32)],
        compiler_params=_cparams(("arbitrary",)),
        name="sample_select",
    )(sc.reshape(n_rows, past), scn.reshape(n_rows, LANES))
    selv = selv.reshape(dec_batch, SUBLANES, LANES)
    selm = selm.reshape(dec_batch, SUBLANES, LANES)

    ck = cache_k_l.reshape(n_pool, PAGE_SIZE * B_KV_HEADS, B_HEAD_DIM)
    cv = cache_v_l.reshape(n_pool, PAGE_SIZE * B_KV_HEADS, B_HEAD_DIM)
    kv_buf = pltpu.VMEM((2, PAGES_PER_STEP, PAGE_SIZE * B_KV_HEADS, B_HEAD_DIM), F32)
    o = pl.pallas_call(
        functools.partial(_dsa_sample_attend_kernel, dec_seq=dec_seq, n_steps=n_steps, n_samples=dec_batch),
        grid_spec=pltpu.PrefetchScalarGridSpec(
            num_scalar_prefetch=1,
            grid=(dec_batch, n_steps),
            in_specs=[pl.BlockSpec(memory_space=pl.ANY), pl.BlockSpec(memory_space=pl.ANY)] + [
                pl.BlockSpec((None, B_KV_HEADS, rows, B_HEAD_DIM), lambda b, s, pt: (b, 0, 0, 0)),
                pl.BlockSpec((None, SUBLANES, chunk), lambda b, s, pt: (b, 0, s)),
                pl.BlockSpec((None, SUBLANES, LANES), lambda b, s, pt: (b, 0, 0)),
                pl.BlockSpec((None, SUBLANES, LANES), lambda b, s, pt: (b, 0, 0)),
                pl.BlockSpec((None, SUBLANES, LANES), lambda b, s, pt: (b, 0, 0)),
                pl.BlockSpec((None, LANES, 2 * KV_WIDTH), lambda b, s, pt: (b, 0, 0)),
                pl.BlockSpec(sbias.shape, lambda b, s, pt: (0, 0, 0)),
                pl.BlockSpec(sconst.shape, lambda b, s, pt: (0, 0, 0)),
            ],
            out_specs=pl.BlockSpec((None, B_KV_HEADS, rows, B_HEAD_DIM), lambda b, s, pt: (b, 0, 0, 0)),
            scratch_shapes=[
                kv_buf,
                kv_buf,
                pltpu.SemaphoreType.DMA((2, 2)),
                pltpu.VMEM((B_KV_HEADS, rows, 1), F32),
                pltpu.VMEM((B_KV_HEADS, rows, 1), F32),
                pltpu.VMEM((B_KV_HEADS, rows, B_HEAD_DIM), F32),
            ],
        ),
        out_shape=jax.ShapeDtypeStruct((dec_batch, B_KV_HEADS, rows, B_HEAD_DIM), BF16),
        compiler_params=_cparams(("arbitrary", "arbitrary")),
        name="dsa_sample_attend",
    )(page_table, ck, cv, q, sc, scn, selv, selm, kv_new, sbias, sconst)
    o = o.reshape(dec_batch, B_KV_HEADS, B_GQA, SUBLANES, B_HEAD_DIM)[:, :, :, :dec_seq].transpose(0, 3, 1, 2, 4)
    return o.reshape(dec_batch * dec_seq, B_WIDTH)


def _mixout_kernel(u_ref, vn_ref, b_ref, ws_ref, ab_ref, x_ref, w_ref, o_ref, mix_scr, *, tm):
    j = pl.program_id(1)

    @pl.when(j == 0)
    def _():
        row = lax.broadcasted_iota(I32, (A_CHUNK, A_CHUNK), 0)
        col = lax.broadcasted_iota(I32, (A_CHUNK, A_CHUNK), 1)
        for g in range(A_GROUPS):
            wsg = jnp.where(row >= col, ws_ref[g], 0.0).astype(BF16)
            bg = ab_ref[:, g:g + 1]
            cs = slice(g * A_CH, (g + 1) * A_CH)
            for r in range(tm // A_CHUNK):
                rs = slice(r * A_CHUNK, (r + 1) * A_CHUNK)
                sm = jnp.dot(wsg, vn_ref[rs, cs], preferred_element_type=F32) + bg
                mix_scr[rs, cs] = (u_ref[rs, cs].astype(F32) * sm).astype(BF16)
        mix_scr[:, A_WIDTH:] = b_ref[...]

    o_ref[...] = x_ref[...] + jnp.dot(mix_scr[...], w_ref[...].astype(BF16), preferred_element_type=F32)


def _mixout(p, b_out, ws_eff, ab_eff, x2d, w_out_l, *, tm, tn=D_MODEL):
    n = x2d.shape[0]
    return pl.pallas_call(
        functools.partial(_mixout_kernel, tm=tm),
        grid=(n // tm, D_MODEL // tn),
        in_specs=[
            pl.BlockSpec((tm, A_WIDTH), lambda i, j: (i, COL_U // A_WIDTH)),
            pl.BlockSpec((tm, A_WIDTH), lambda i, j: (i, COL_V // A_WIDTH)),
            pl.BlockSpec((tm, B_WIDTH), lambda i, j: (i, 0)),
            pl.BlockSpec((A_GROUPS, A_CHUNK, A_CHUNK), lambda i, j: (0, 0, 0)),
            pl.BlockSpec((A_CHUNK, A_GROUPS), lambda i, j: (0, 0)),
            pl.BlockSpec((tm, tn), lambda i, j: (i, j)),
            pl.BlockSpec((A_WIDTH + B_WIDTH, tn), lambda i, j: (0, j)),
        ],
        out_specs=pl.BlockSpec((tm, tn), lambda i, j: (i, j)),
        out_shape=jax.ShapeDtypeStruct((n, D_MODEL), F32),
        scratch_shapes=[pltpu.VMEM((tm, A_WIDTH + B_WIDTH), BF16)],
        compiler_params=_cparams(("arbitrary", "arbitrary")),
        name="mixout",
    )(p, p, b_out, ws_eff, ab_eff, x2d, w_out_l)


def _memkv_kernel(x_ref, wk_ref, wv_ref, kn_ref, mk_ref, mv_ref):
    x = x_ref[...].astype(BF16)
    tm = x.shape[0]
    mk = _head_norm(jnp.dot(x, wk_ref[...].astype(BF16), preferred_element_type=F32), kn_ref[...])
    mv = jnp.dot(x, wv_ref[...].astype(BF16), preferred_element_type=F32)
    for hd in range(MEM_HEADS):
        cs = slice(hd * MEM_HEAD_DIM, (hd + 1) * MEM_HEAD_DIM)
        mk_ref[pl.ds(hd, tm, stride=MEM_HEADS), :] = mk[:, cs]
        mv_ref[pl.ds(hd, tm, stride=MEM_HEADS), :] = mv[:, cs]


def _memkv(mem2d, wk, wv, knorm, *, tm=256):
    n = mem2d.shape[0]
    kn = jnp.tile(knorm, MEM_HEADS).reshape(1, MEM_WIDTH)
    return pl.pallas_call(
        _memkv_kernel,
        grid=(n // tm,),
        in_specs=[
            pl.BlockSpec((tm, D_MODEL), lambda i: (i, 0)),
            pl.BlockSpec((D_MODEL, MEM_WIDTH), lambda i: (0, 0)),
            pl.BlockSpec((D_MODEL, MEM_WIDTH), lambda i: (0, 0)),
            pl.BlockSpec((1, MEM_WIDTH), lambda i: (0, 0)),
        ],
        out_specs=[pl.BlockSpec((tm * MEM_HEADS, MEM_HEAD_DIM), lambda i: (i, 0))] * 2,
        out_shape=[jax.ShapeDtypeStruct((n * MEM_HEADS, MEM_HEAD_DIM), F32)] * 2,
        compiler_params=_cparams(("arbitrary",)),
        name="memkv",
    )(mem2d, wk, wv, kn)


def _memattn_kernel(h_ref, g_ref, wq_ref, qn_ref, mk_ref, mv_ref, wo_ref, gf_ref, wc_ref, bc_ref,
                    o_ref, n_ref, c_ref, *, rows_per_mem, mem_len):
    tm, m = h_ref.shape[0], mk_ref.shape[0] // MEM_HEADS
    wq = wq_ref[...].astype(BF16)
    wo = wo_ref[...].astype(BF16)
    mks = [mk_ref[pl.ds(hd, m, stride=MEM_HEADS), :].astype(BF16) for hd in range(MEM_HEADS)]
    mvs = [mv_ref[pl.ds(hd, m, stride=MEM_HEADS), :].astype(BF16) for hd in range(MEM_HEADS)]
    rc = min(tm, MEMATTN_ROW_CHUNK)
    for r in range(tm // rc):
        rs = slice(r * rc, (r + 1) * rc)
        h = h_ref[rs, :]
        n = _row_norm(h, g_ref[...]).astype(BF16)
        q = _head_norm(jnp.dot(n, wq, preferred_element_type=F32), qn_ref[...]).astype(BF16)
        if rows_per_mem is not None:
            row = r * rc + lax.broadcasted_iota(I32, (rc, m), 0)
            col = lax.broadcasted_iota(I32, (rc, m), 1)
            mask = jnp.where(row // rows_per_mem == col // mem_len, 0.0, -jnp.inf)
        outs = []
        for hd in range(MEM_HEADS):
            cs = slice(hd * MEM_HEAD_DIM, (hd + 1) * MEM_HEAD_DIM)
            lg = _dot_nt(q[:, cs], mks[hd]) * (MEM_HEAD_DIM ** -0.5)
            if rows_per_mem is not None:
                lg = lg + mask
            e = jnp.exp(lg - jnp.max(lg, axis=-1, keepdims=True))
            l = jnp.sum(e, axis=-1, keepdims=True)
            outs.append(jnp.dot(e.astype(BF16), mvs[hd], preferred_element_type=F32) / l)
        o = jnp.concatenate(outs, axis=-1).astype(BF16)
        h2 = h + jnp.dot(o, wo, preferred_element_type=F32)
        o_ref[rs, :] = h2
        n_ref[rs, :], c_ref[rs, :] = _route(h2, gf_ref[...], wc_ref[...], bc_ref[...])


def _memattn(h2d, gain, wq, qnorm, mk, mv, wo, gain_ffn, w_cat, b_cat, *, tm, mem_rows, tiles_per_mem,
             rows_per_mem, mem_len):
    n = h2d.shape[0]
    qn = jnp.tile(qnorm, MEM_HEADS).reshape(1, MEM_WIDTH)
    return pl.pallas_call(
        functools.partial(_memattn_kernel, rows_per_mem=rows_per_mem, mem_len=mem_len),
        grid=(n // tm,),
        in_specs=[
            pl.BlockSpec((tm, D_MODEL), lambda i: (i, 0)),
            pl.BlockSpec((1, D_MODEL), lambda i: (0, 0)),
            pl.BlockSpec((D_MODEL, MEM_WIDTH), lambda i: (0, 0)),
            pl.BlockSpec((1, MEM_WIDTH), lambda i: (0, 0)),
            pl.BlockSpec((mem_rows * MEM_HEADS, MEM_HEAD_DIM), lambda i: (i // tiles_per_mem, 0)),
            pl.BlockSpec((mem_rows * MEM_HEADS, MEM_HEAD_DIM), lambda i: (i // tiles_per_mem, 0)),
            pl.BlockSpec((MEM_WIDTH, D_MODEL), lambda i: (0, 0)),
            pl.BlockSpec((1, D_MODEL), lambda i: (0, 0)),
            pl.BlockSpec((D_MODEL, 2 * LANES), lambda i: (0, 0)),
            pl.BlockSpec((1, 2 * LANES), lambda i: (0, 0)),
        ],
        out_specs=[pl.BlockSpec((tm, D_MODEL), lambda i: (i, 0)), pl.BlockSpec((tm, D_MODEL), lambda i: (i, 0)),
                   pl.BlockSpec((tm, LANES), lambda i: (i, 0))],
        out_shape=[jax.ShapeDtypeStruct((n, D_MODEL), F32), jax.ShapeDtypeStruct((n, D_MODEL), BF16),
                   jax.ShapeDtypeStruct((n, LANES), F32)],
        compiler_params=_cparams(("arbitrary",)),
        name="memattn",
    )(h2d, gain.reshape(1, D_MODEL), wq, qn, mk, mv, wo, gain_ffn.reshape(1, D_MODEL), w_cat, b_cat)


def _route(h, gain, w_cat, b_cat):
    n = _row_norm(h, gain).astype(BF16)
    lg = jnp.dot(n, w_cat.astype(BF16), preferred_element_type=F32) + b_cat
    gl, el = lg[:, :LANES], lg[:, LANES:]
    lane = lax.broadcasted_iota(I32, gl.shape, 1)
    lanef = lane.astype(F32)

    def first_max_lane(v, vmax):
        return jnp.min(jnp.where(v == vmax, lanef, float(LANES)), axis=-1, keepdims=True).astype(I32)

    gl = jnp.where(lane < N_GROUPS, gl, -jnp.inf)
    ge = jnp.exp(gl - jnp.max(gl, axis=-1, keepdims=True))
    gp = ge / jnp.sum(ge, axis=-1, keepdims=True)
    g_gate = jnp.max(gp, axis=-1, keepdims=True)
    g_sel = first_max_lane(gp, g_gate)
    in_grp = (lane // EXPERTS_PER_GROUP == g_sel) & (lane < N_EXPERTS)
    el = jnp.where(in_grp, el, -jnp.inf)
    ee = jnp.exp(el - jnp.max(el, axis=-1, keepdims=True))
    ep = jnp.where(in_grp, ee / jnp.sum(ee, axis=-1, keepdims=True), -jnp.inf)
    w1 = jnp.max(ep, axis=-1, keepdims=True)
    i1 = first_max_lane(ep, w1)
    ep2 = jnp.where(lane == i1, -jnp.inf, ep)
    w2 = jnp.max(ep2, axis=-1, keepdims=True)
    i2 = first_max_lane(ep2, w2)
    tot = w1 + w2
    comb = jnp.where(lane == i1, w1 / tot * g_gate, jnp.where(lane == i2, w2 / tot * g_gate, 0.0))
    return n, jnp.where(lane == GSEL_LANE, g_sel.astype(F32), comb)


def _cast_kernel(x_ref, o_ref):
    o_ref[...] = x_ref[...].astype(BF16)


def _cast_bf16(w):
    ne, r, c = w.shape
    return pl.pallas_call(
        _cast_kernel,
        grid=(ne,),
        in_specs=[pl.BlockSpec((None, r, c), lambda e: (e, 0, 0))],
        out_specs=pl.BlockSpec((None, r, c), lambda e: (e, 0, 0)),
        out_shape=jax.ShapeDtypeStruct(w.shape, BF16),
        compiler_params=_cparams(("arbitrary",)),
        name="cast_bf16",
    )(w)


GSEL_LANE = N_EXPERTS
MOE_ROWS = 128
MOE_MAIN_SLACK = 0
MOE_FF_SPLIT = 1


def _moe_routed_kernel(n_ref, c_ref, h_hbm, wg_ref, wu_ref, wd_ref, o_ref,
                       tri_scr, key_scr, xc_scr, yc_scr, cw_scr, nch_ref, sem, *, tm):
    i, g, e, f = (pl.program_id(k) for k in range(4))
    main_rows = tm // N_GROUPS + MOE_MAIN_SLACK
    first_e = (e == 0) & (f == 0)
    last_e = (e == EXPERTS_PER_GROUP - 1) & (f == MOE_FF_SPLIT - 1)

    def onehot_rows(start, rows):
        want = (start + lax.broadcasted_iota(I32, (rows, 1), 0)).astype(F32)
        return jnp.where(key_scr[pl.ds(g, 1), :] == want, 1.0, 0.0).astype(BF16)

    def tail_start(c):
        return pl.multiple_of(main_rows + c * MOE_ROWS, math.gcd(main_rows, MOE_ROWS))

    @pl.when((i == 0) & (g == 0) & first_e)
    def _():
        r = lax.broadcasted_iota(I32, (tm, tm), 0)
        c = lax.broadcasted_iota(I32, (tm, tm), 1)
        tri_scr[...] = jnp.where(r < c, 1.0, 0.0).astype(BF16)

    def residual_copy():
        return pltpu.make_async_copy(h_hbm.at[pl.ds(pl.multiple_of(i * tm, tm), tm)], o_ref, sem)

    @pl.when((g == 0) & first_e)
    def _():
        residual_copy().start()
        lane = lax.broadcasted_iota(I32, (tm, LANES), 1).astype(F32)
        onehot = jnp.where(lane == c_ref[:, GSEL_LANE:GSEL_LANE + 1], 1.0, 0.0).astype(BF16)
        er = lax.broadcasted_iota(I32, (SUBLANES, LANES), 0)
        ec = lax.broadcasted_iota(I32, (SUBLANES, LANES), 1)
        eye = jnp.where(er == ec, 1.0, 0.0).astype(BF16)
        mask_t = _dot_nt(eye, onehot)
        rank_t = jnp.dot(mask_t.astype(BF16), tri_scr[...], preferred_element_type=F32)
        key_scr[...] = jnp.where(mask_t > 0.5, rank_t, -1.0)
        for gg in range(N_GROUPS):
            cnt = jnp.sum(mask_t[gg:gg + 1, :], axis=-1, keepdims=True)
            extra = jnp.ceil(jnp.maximum(cnt - main_rows, 0.0) * (1.0 / MOE_ROWS))
            nch_ref[gg] = jnp.sum(extra).astype(I32)

    ntail = nch_ref[g]

    @pl.when(first_e)
    def _():
        comb = c_ref[...]
        hi = comb.astype(BF16)
        r1 = comb - hi.astype(F32)
        mid = r1.astype(BF16)
        lo = (r1 - mid.astype(F32)).astype(BF16)

        def compact(start, rows):
            s = onehot_rows(start, rows)
            xc_scr[pl.ds(start, rows), :] = jnp.dot(s, n_ref[...], preferred_element_type=F32).astype(BF16)
            cw_scr[pl.ds(start, rows), :] = (jnp.dot(s, hi, preferred_element_type=F32)
                                             + jnp.dot(s, mid, preferred_element_type=F32)
                                             + jnp.dot(s, lo, preferred_element_type=F32))
            yc_scr[pl.ds(start, rows), :] = jnp.zeros((rows, D_MODEL), F32)

        compact(0, main_rows)
        lax.fori_loop(0, ntail, lambda c, carry: (compact(tail_start(c), MOE_ROWS), carry)[1], 0)

    wg = wg_ref[...].astype(BF16)
    wu = wu_ref[...].astype(BF16)
    wd = wd_ref[...].astype(BF16)
    ex = g * EXPERTS_PER_GROUP + e

    def expert(start, rows):
        x = xc_scr[pl.ds(start, rows), :]
        a = jnp.dot(x, wg, preferred_element_type=F32)
        u = jnp.dot(x, wu, preferred_element_type=F32)
        lane = lax.broadcasted_iota(I32, (rows, LANES), 1)
        cw = jnp.sum(jnp.where(lane == ex, cw_scr[pl.ds(start, rows), :], 0.0), axis=-1, keepdims=True)
        hid = (a * jax.nn.sigmoid(a) * u * cw).astype(BF16)
        yc_scr[pl.ds(start, rows), :] += jnp.dot(hid, wd, preferred_element_type=F32)

    expert(0, main_rows)
    lax.fori_loop(0, ntail, lambda c, carry: (expert(tail_start(c), MOE_ROWS), carry)[1], 0)

    @pl.when((g == 0) & last_e)
    def _():
        residual_copy().wait()

    @pl.when(last_e)
    def _():
        def scatter(start, rows):
            s = onehot_rows(start, rows)
            y = yc_scr[pl.ds(start, rows), :].astype(BF16)
            o_ref[...] += lax.dot_general(s, y, (((0,), (0,)), ((), ())), preferred_element_type=F32)

        scatter(0, main_rows)
        lax.fori_loop(0, ntail, lambda c, carry: (scatter(tail_start(c), MOE_ROWS), carry)[1], 0)


def _moe_routed(n3, comb, h2d, w_gate_l, w_up_l, w_down_l, *, tm):
    n = h2d.shape[0]
    ff = EXPERT_FF // MOE_FF_SPLIT

    def wmap(i, g, e, f):
        return (g * EXPERTS_PER_GROUP + e, 0, f)

    return pl.pallas_call(
        functools.partial(_moe_routed_kernel, tm=tm),
        grid=(n // tm, N_GROUPS, EXPERTS_PER_GROUP, MOE_FF_SPLIT),
        in_specs=[
            pl.BlockSpec((tm, D_MODEL), lambda i, g, e, f: (i, 0), pipeline_mode=pl.Buffered(1)),
            pl.BlockSpec((tm, LANES), lambda i, g, e, f: (i, 0), pipeline_mode=pl.Buffered(1)),
            pl.BlockSpec(memory_space=pl.ANY),
            pl.BlockSpec((None, D_MODEL, ff), wmap),
            pl.BlockSpec((None, D_MODEL, ff), wmap),
            pl.BlockSpec((None, ff, D_MODEL), lambda i, g, e, f: (g * EXPERTS_PER_GROUP + e, f, 0)),
        ],
        out_specs=pl.BlockSpec((tm, D_MODEL), lambda i, g, e, f: (i, 0)),
        out_shape=jax.ShapeDtypeStruct((n, D_MODEL), F32),
        scratch_shapes=[
            pltpu.VMEM((tm, tm), BF16),
            pltpu.VMEM((SUBLANES, tm), F32),
            pltpu.VMEM((tm + MOE_ROWS, D_MODEL), BF16),
            pltpu.VMEM((tm + MOE_ROWS, D_MODEL), F32),
            pltpu.VMEM((tm + MOE_ROWS, LANES), F32),
            pltpu.SMEM((N_GROUPS,), I32),
            pltpu.SemaphoreType.DMA(()),
        ],
        compiler_params=pltpu.CompilerParams(
            dimension_semantics=("arbitrary",) * 4, vmem_limit_bytes=MOE_VMEM_LIMIT),
        name="moe_routed",
    )(n3, comb, h2d, w_gate_l, w_up_l, w_down_l)


def _moe_kernel(n_ref, c_ref, h_ref, wg_ref, wu_ref, wd_ref, o_ref):
    e = pl.program_id(1)

    @pl.when(e == 0)
    def _():
        o_ref[...] = h_ref[...]

    x = n_ref[...]
    a = jnp.dot(x, wg_ref[...].astype(BF16), preferred_element_type=F32)
    u = jnp.dot(x, wu_ref[...].astype(BF16), preferred_element_type=F32)
    lane = lax.broadcasted_iota(I32, c_ref.shape, 1)
    c = jnp.sum(jnp.where(lane == e, c_ref[...], 0.0), axis=-1, keepdims=True)
    hid = (a * jax.nn.sigmoid(a) * u * c).astype(BF16)
    o_ref[...] += jnp.dot(hid, wd_ref[...].astype(BF16), preferred_element_type=F32)


def _moe(n3, comb, h2d, w_gate_l, w_up_l, w_down_l, *, tm):
    n = h2d.shape[0]
    return pl.pallas_call(
        _moe_kernel,
        grid=(n // tm, N_EXPERTS),
        in_specs=[
            pl.BlockSpec((tm, D_MODEL), lambda i, e: (i, 0)),
            pl.BlockSpec((tm, LANES), lambda i, e: (i, 0)),
            pl.BlockSpec((tm, D_MODEL), lambda i, e: (i, 0)),
            pl.BlockSpec((None, D_MODEL, EXPERT_FF), lambda i, e: (e, 0, 0)),
            pl.BlockSpec((None, D_MODEL, EXPERT_FF), lambda i, e: (e, 0, 0)),
            pl.BlockSpec((None, EXPERT_FF, D_MODEL), lambda i, e: (e, 0, 0)),
        ],
        out_specs=pl.BlockSpec((tm, D_MODEL), lambda i, e: (i, 0)),
        out_shape=jax.ShapeDtypeStruct((n, D_MODEL), F32),
        compiler_params=_cparams(("arbitrary", "arbitrary")),
        name="moe",
    )(n3, comb, h2d, w_gate_l, w_up_l, w_down_l)


def _post_mixer(p, b_out, ws_eff, ab_eff, x2d, mk, mv, lw, *, tm_mix, tm_mem, mem_rows, tiles_per_mem,
                rows_per_mem, mem_len, tm_moe):
    h1 = _mixout(p, b_out, ws_eff, ab_eff, x2d, lw["w_out"], tm=tm_mix)
    h2, n3, comb = _memattn(h1, lw["norm_mem"], lw["mem_wq"], lw["mem_qnorm"], mk, mv, lw["mem_wo"],
                            lw["norm_ffn"], lw["w_cat"], lw["b_cat"], tm=tm_mem, mem_rows=mem_rows,
                            tiles_per_mem=tiles_per_mem, rows_per_mem=rows_per_mem, mem_len=mem_len)
    moe = _moe_routed if tm_moe > N_GROUPS * MOE_ROWS else _moe
    return moe(n3, comb, h2, lw["w_gate"], lw["w_up"], lw["w_down"], tm=tm_moe)


def kernel(x_prompt, x_sample, cache_k, cache_v, cache_idx_k, cache_mem_k, cache_mem_v, page_table,
           mem_prompt, norm_mix, w_in, a_vnorm, a_ws, a_b, b_qnorm, b_knorm, rel_bias, w_out,
           norm_mem, mem_wq, mem_wk, mem_wv, mem_qnorm, mem_knorm, mem_wo, norm_ffn,
           w_group, b_group, w_router, b_router, w_gate, w_up, w_down):
    bp, t, d = x_prompt.shape
    bd, s, _ = x_sample.shape
    depth = w_in.shape[0]
    mem_len = mem_prompt.shape[1]
    assert d == D_MODEL and t % KEY_CHUNK == 0 and s <= 8 and (bd * s) % LANES == 0

    bias_pairs, sbias, sconst = _bias_tables(rel_bias)
    hp = x_prompt.reshape(bp * t, d)
    hs = x_sample.reshape(bd * s, d)
    outs = [[] for _ in range(9)]
    for l in range(depth):
        w_cat = jnp.zeros((d, 2 * LANES), F32)
        w_cat = w_cat.at[:, :N_GROUPS].set(w_group[l]).at[:, LANES:LANES + N_EXPERTS].set(w_router[l])
        b_cat = jnp.zeros((1, 2 * LANES), F32)
        b_cat = b_cat.at[0, :N_GROUPS].set(b_group[l]).at[0, LANES:LANES + N_EXPERTS].set(b_router[l])
        lw = dict(w_out=w_out[l], norm_mem=norm_mem[l], mem_wq=mem_wq[l], mem_qnorm=mem_qnorm[l],
                  mem_wo=mem_wo[l], norm_ffn=norm_ffn[l], w_cat=w_cat, b_cat=b_cat)

        p, k, v, tail = _proj(hp, norm_mix[l], w_in[l], a_vnorm[l], b_qnorm[l], b_knorm[l],
                              tm=1024, want_vn=False)
        later_weights = (w_gate[l], w_up[l], w_down[l], w_out[l][None])
        fuse_cast = (bp * (t // Q_BLOCK)) % N_EXPERTS == 0
        b_out, casted = _dsa_prompt(p, tail, bias_pairs, later_weights if fuse_cast else (), batch=bp, seq=t)
        if not fuse_cast:
            casted = tuple(_cast_bf16(w) for w in later_weights)
        lw.update(w_gate=casted[0], w_up=casted[1], w_down=casted[2], w_out=casted[3][0])
        mk, mv = _memkv(mem_prompt.reshape(bp * mem_len, d), mem_wk[l], mem_wv[l], mem_knorm[l])
        hp = _post_mixer(p, b_out, a_ws[l], a_b[l].T, hp, mk, mv, lw, tm_mix=512, tm_mem=512,
                         mem_rows=mem_len, tiles_per_mem=t // 512, rows_per_mem=None, mem_len=mem_len,
                         tm_moe=1024)
        outs[0].append(k.reshape(bp, t, B_KV_HEADS, B_HEAD_DIM))
        outs[1].append(v.reshape(bp, t, B_KV_HEADS, B_HEAD_DIM))
        outs[2].append(tail[:, :IDX_DIM].reshape(bp, t, IDX_DIM))
        outs[3].append(mk.reshape(bp, mem_len, MEM_HEADS, MEM_HEAD_DIM))
        outs[4].append(mv.reshape(bp, mem_len, MEM_HEADS, MEM_HEAD_DIM))

        ns = bd * s
        p_s, k_s, v_s, tail_s, vn_s = _proj(hs, norm_mix[l], w_in[l], a_vnorm[l], b_qnorm[l], b_knorm[l],
                                            tm=ns, want_vn=True)
        b_out_s = _dsa_sample(p_s, tail_s, k_s.reshape(ns, KV_WIDTH), v_s.reshape(ns, KV_WIDTH),
                              cache_k[l], cache_v[l], cache_idx_k[l], page_table,
                              sbias, sconst, dec_batch=bd, dec_seq=s)
        ws_s = jnp.einsum("ab,gts->gatbs", jnp.eye(ns // s, dtype=F32), a_ws[l][:, :s, :s])
        ws_s = ws_s.reshape(A_GROUPS, ns, ns)
        ab_s = jnp.tile(a_b[l][:, :s].T, (ns // s, 1))
        mem_tile = 32
        hs = _post_mixer(p_s, b_out_s, ws_s, ab_s, hs,
                         cache_mem_k[l].reshape(bd * mem_len * MEM_HEADS, MEM_HEAD_DIM),
                         cache_mem_v[l].reshape(bd * mem_len * MEM_HEADS, MEM_HEAD_DIM), lw,
                         tm_mix=ns, tm_mem=mem_tile, mem_rows=mem_tile // s * mem_len, tiles_per_mem=1,
                         rows_per_mem=s, mem_len=mem_len, tm_moe=ns)
        outs[5].append(k_s.reshape(bd, s, B_KV_HEADS, B_HEAD_DIM))
        outs[6].append(v_s.reshape(bd, s, B_KV_HEADS, B_HEAD_DIM))
        outs[7].append(tail_s[:, :IDX_DIM].reshape(bd, s, IDX_DIM))
        outs[8].append(vn_s.reshape(bd, s, A_GROUPS, A_CH))
    return (hp.reshape(bp, t, d), hs.reshape(bd, s, d)) + tuple(jnp.stack(o) for o in outs)
```

```python
import functools
import math

import jax
import jax.numpy as jnp
from jax import lax
from jax.experimental import pallas as pl
from jax.experimental.pallas import tpu as pltpu

BF16 = jnp.bfloat16
F32 = jnp.float32
I32 = jnp.int32

D_MODEL = 2048
A_GROUPS = 8
A_CH = 128
A_WIDTH = A_GROUPS * A_CH
A_CHUNK = 128
B_HEADS = 8
B_HEAD_DIM = 128
B_KV_HEADS = 2
B_GQA = B_HEADS // B_KV_HEADS
B_WIDTH = B_HEADS * B_HEAD_DIM
KV_WIDTH = B_KV_HEADS * B_HEAD_DIM
IDX_HEADS = 16
IDX_DIM = 64
TOPK_MAX = 256
Q_BLOCK = 128
REL_BUCKETS = 32
REL_MAX_DIST = 128
MEM_HEADS = 4
MEM_HEAD_DIM = 128
MEM_WIDTH = MEM_HEADS * MEM_HEAD_DIM
N_GROUPS = 4
EXPERTS_PER_GROUP = 4
N_EXPERTS = N_GROUPS * EXPERTS_PER_GROUP
EXPERT_FF = 512
PAGE_SIZE = 128
EPS = 1e-6

COL_U = 0
COL_V = COL_U + A_WIDTH
COL_Q = COL_V + A_WIDTH
COL_K = COL_Q + B_WIDTH
COL_VV = COL_K + KV_WIDTH
COL_QI = COL_VV + KV_WIDTH
COL_MAIN = COL_QI + IDX_HEADS * IDX_DIM
TAIL_COLS = IDX_DIM + IDX_HEADS
LANES = 128
SUBLANES = 8
BF16_SUBLANES = 2 * SUBLANES
PROJ_TN = 512
PROJ_ROW_CHUNK = 256
MEMATTN_ROW_CHUNK = 512
KEY_CHUNK = 256
NEG_INIT = -1e30
VMEM_LIMIT = 56 * 1024 * 1024
MOE_VMEM_LIMIT = 60 * 1024 * 1024


def _cparams(sem):
    return pltpu.CompilerParams(dimension_semantics=sem, vmem_limit_bytes=VMEM_LIMIT)


def _gelu(x):
    return 0.5 * x * (1.0 + lax.erf(x * (2.0 ** -0.5)))


def _head_norm(a, gain, width=LANES):
    outs = []
    for c in range(a.shape[1] // width):
        blk = a[:, c * width:(c + 1) * width]
        ms = jnp.mean(blk * blk, axis=-1, keepdims=True)
        outs.append(blk * lax.rsqrt(ms + EPS) * gain[:, c * width:(c + 1) * width])
    return outs[0] if len(outs) == 1 else jnp.concatenate(outs, axis=-1)


def _row_norm(x, gain):
    ms = jnp.mean(x * x, axis=-1, keepdims=True)
    return x * lax.rsqrt(ms + EPS) * gain


def _dot_nt(a, b):
    return lax.dot_general(a, b, (((1,), (1,)), ((), ())), preferred_element_type=F32)


def _proj_kernel(x_ref, g_ref, w_ref, wt_ref, avn_ref, qn_ref, kn_ref, *rest, want_vn):
    if want_vn:
        p_ref, k_ref, v_ref, tail_ref, vn_ref, xn_scr = rest
    else:
        p_ref, k_ref, v_ref, tail_ref, xn_scr = rest
        vn_ref = None
    j = pl.program_id(1)
    ju = COL_V // PROJ_TN
    jv = COL_Q // PROJ_TN
    jq = COL_K // PROJ_TN
    jkv = COL_QI // PROJ_TN

    @pl.when(j == 0)
    def _():
        xn = _row_norm(x_ref[...], g_ref[...]).astype(BF16)
        xn_scr[...] = xn
        wrow = lax.broadcasted_iota(I32, wt_ref.shape, 0)
        wt = jnp.where(wrow < TAIL_COLS, wt_ref[...], 0.0).astype(BF16)
        t = _dot_nt(xn, wt)
        lane = lax.broadcasted_iota(I32, t.shape, 1)
        tail_ref[...] = jnp.where(lane >= IDX_DIM, t * (IDX_HEADS ** -0.5), t)

    tm = xn_scr.shape[0]
    rc = min(tm, PROJ_ROW_CHUNK)

    def row_chunks(epilogue):
        w = w_ref[...].astype(BF16)
        for r in range(tm // rc):
            rs = slice(r * rc, (r + 1) * rc)
            epilogue(_dot_nt(xn_scr[rs, :], w), rs)

    @pl.when(j < ju)
    def _():
        def epi(acc, rs):
            p_ref[rs, :] = _gelu(acc).astype(BF16)
        row_chunks(epi)

    @pl.when((j >= ju) & (j < jv))
    def _():
        def epi(acc, rs):
            vn = _head_norm(_gelu(acc), avn_ref[...])
            p_ref[rs, :] = vn.astype(BF16)
            if vn_ref is not None:
                vn_ref[rs, :] = vn
        row_chunks(epi)

    @pl.when((j >= jv) & (j < jq))
    def _():
        def epi(acc, rs):
            p_ref[rs, :] = _head_norm(acc, qn_ref[...]).astype(BF16)
        row_chunks(epi)

    @pl.when(j == jq)
    def _():
        def epi(acc, rs):
            k = _head_norm(acc[:, :KV_WIDTH], kn_ref[...])
            v = acc[:, KV_WIDTH:]
            for g in range(B_KV_HEADS):
                dst = pl.ds(rs.start * B_KV_HEADS + g, rs.stop - rs.start, stride=B_KV_HEADS)
                k_ref[dst, :] = k[:, g * B_HEAD_DIM:(g + 1) * B_HEAD_DIM]
                v_ref[dst, :] = v[:, g * B_HEAD_DIM:(g + 1) * B_HEAD_DIM]
            p_ref[rs, :] = jnp.concatenate([k, v], axis=-1).astype(BF16)
        row_chunks(epi)

    @pl.when(j >= jkv)
    def _():
        def epi(acc, rs):
            p_ref[rs, :] = acc.astype(BF16)
        row_chunks(epi)


def _proj(x2d, gain, w_in_l, a_vnorm_l, b_qnorm_l, b_knorm_l, *, tm, want_vn):
    n = x2d.shape[0]
    nj = COL_MAIN // PROJ_TN
    w_in_t = jnp.swapaxes(w_in_l, 0, 1)
    avn = a_vnorm_l.reshape(1, A_WIDTH)
    qn = jnp.tile(b_qnorm_l, PROJ_TN // B_HEAD_DIM).reshape(1, PROJ_TN)
    kn = jnp.tile(b_knorm_l, B_KV_HEADS).reshape(1, KV_WIDTH)
    ju = COL_V // PROJ_TN
    nv = A_WIDTH // PROJ_TN

    def vmap_(i, j):
        return (0, jnp.clip(j - ju, 0, nv - 1))

    in_specs = [
        pl.BlockSpec((tm, D_MODEL), lambda i, j: (i, 0)),
        pl.BlockSpec((1, D_MODEL), lambda i, j: (0, 0)),
        pl.BlockSpec((PROJ_TN, D_MODEL), lambda i, j: (j, 0)),
        pl.BlockSpec((LANES, D_MODEL), lambda i, j: (COL_MAIN // LANES, 0)),
        pl.BlockSpec((1, PROJ_TN), vmap_),
        pl.BlockSpec((1, PROJ_TN), lambda i, j: (0, 0)),
        pl.BlockSpec((1, KV_WIDTH), lambda i, j: (0, 0)),
    ]
    out_shape = [
        jax.ShapeDtypeStruct((n, COL_MAIN), BF16),
        jax.ShapeDtypeStruct((n * B_KV_HEADS, B_HEAD_DIM), F32),
        jax.ShapeDtypeStruct((n * B_KV_HEADS, B_HEAD_DIM), F32),
        jax.ShapeDtypeStruct((n, LANES), F32),
    ]
    out_specs = [
        pl.BlockSpec((tm, PROJ_TN), lambda i, j: (i, j)),
        pl.BlockSpec((tm * B_KV_HEADS, B_HEAD_DIM), lambda i, j: (i, 0)),
        pl.BlockSpec((tm * B_KV_HEADS, B_HEAD_DIM), lambda i, j: (i, 0)),
        pl.BlockSpec((tm, LANES), lambda i, j: (i, 0)),
    ]
    if want_vn:
        out_shape.append(jax.ShapeDtypeStruct((n, A_WIDTH), F32))
        out_specs.append(pl.BlockSpec((tm, PROJ_TN), lambda i, j: (i, jnp.clip(j - ju, 0, nv - 1))))
    return pl.pallas_call(
        functools.partial(_proj_kernel, want_vn=want_vn),
        grid=(n // tm, nj),
        in_specs=in_specs,
        out_specs=out_specs,
        out_shape=out_shape,
        scratch_shapes=[pltpu.VMEM((tm, D_MODEL), BF16)],
        compiler_params=_cparams(("arbitrary", "arbitrary")),
        name="proj",
    )(x2d, gain.reshape(1, D_MODEL), w_in_t, w_in_t, avn, qn, kn)


def _t5_bucket(d):
    max_exact = REL_BUCKETS // 2
    d = jnp.maximum(d, 0)
    ratio = jnp.log(jnp.maximum(d, 1).astype(F32) / max_exact) / math.log(REL_MAX_DIST / max_exact)
    large = jnp.minimum(max_exact + jnp.floor(ratio * (REL_BUCKETS - max_exact)).astype(I32), REL_BUCKETS - 1)
    return jnp.where(d < max_exact, d, large)


def _bias_lookup(bucket, rb_ref, h):
    acc = jnp.zeros(bucket.shape, F32)
    for r in range(REL_BUCKETS):
        acc = jnp.where(bucket == r, rb_ref[r, h], acc)
    return acc


def _bias_kernel(rb_ref, pair_ref, samp_ref, sconst_ref):
    q = lax.broadcasted_iota(I32, (Q_BLOCK, Q_BLOCK), 0)
    kc = lax.broadcasted_iota(I32, (Q_BLOCK, Q_BLOCK), 1)
    nvar = pair_ref.shape[0]
    rels = range(-(nvar - 1), 2)
    buckets = {r: _t5_bucket(q - kc - r * Q_BLOCK) for r in rels}
    for h in range(B_HEADS):
        g, hh = divmod(h, B_GQA)
        tiles = {r: _bias_lookup(buckets[r], rb_ref, h) for r in rels}
        for var in range(nvar):
            r0 = var - (nvar - 1)
            for t in range(2):
                pair_ref[var, g, hh * Q_BLOCK:(hh + 1) * Q_BLOCK, t * Q_BLOCK:(t + 1) * Q_BLOCK] = tiles[r0 + t]
    rows = B_GQA * SUBLANES
    row = lax.broadcasted_iota(I32, (rows, 2 * PAGE_SIZE), 0)
    col = lax.broadcasted_iota(I32, (rows, 2 * PAGE_SIZE), 1)
    t = row % SUBLANES
    d = jnp.where(col < PAGE_SIZE, t + PAGE_SIZE - col, t - (col - PAGE_SIZE))
    bucket = _t5_bucket(d)
    far = jnp.full((rows, LANES), REL_BUCKETS - 1, I32)
    rowc = lax.broadcasted_iota(I32, (rows, LANES), 0)
    for g in range(B_KV_HEADS):
        acc = jnp.zeros((rows, 2 * PAGE_SIZE), F32)
        accc = jnp.zeros((rows, LANES), F32)
        for hh in range(B_GQA):
            h = g * B_GQA + hh
            acc = jnp.where(row // SUBLANES == hh, _bias_lookup(bucket, rb_ref, h), acc)
            accc = jnp.where(rowc // SUBLANES == hh, _bias_lookup(far, rb_ref, h), accc)
        samp_ref[g] = acc
        sconst_ref[g] = accc


N_PAIR_VARIANTS = 4


def _bias_tables(rel_bias):
    rows = B_GQA * SUBLANES
    return pl.pallas_call(
        _bias_kernel,
        in_specs=[pl.BlockSpec(memory_space=pltpu.SMEM)],
        out_shape=[
            jax.ShapeDtypeStruct((N_PAIR_VARIANTS, B_KV_HEADS, B_GQA * Q_BLOCK, KEY_CHUNK), F32),
            jax.ShapeDtypeStruct((B_KV_HEADS, rows, 2 * PAGE_SIZE), F32),
            jax.ShapeDtypeStruct((B_KV_HEADS, rows, LANES), F32),
        ],
        name="bias_tables",
    )(rel_bias)


def _key_to_f32(key):
    bits = jnp.where(key < 0, key ^ I32(-2 ** 31), ~key)
    return lax.bitcast_convert_type(bits, F32)


def _kth_largest(count_ge, shape, k):
    def body(it, prefix):
        bit = 31 - it
        cand = prefix | lax.shift_left(I32(1), bit)
        cnt = count_ge(_key_to_f32(cand))
        return jnp.where(cnt >= k, cand, prefix)

    prefix = lax.fori_loop(0, 32, body, jnp.zeros(shape, I32))
    thr = _key_to_f32(prefix)
    return jnp.where((prefix & I32(-2 ** 23)) == 0, -jnp.inf, thr)


NO_INDEX_BOUND = 2 ** 30
TIE_REPAIR_ROUNDS = 2


def _topk_select(fold_sum, fold_min, shape, k, index_bits):
    thr = _kth_largest(lambda t: fold_sum(lambda sc, kp: sc >= t), shape, k)
    c_ge = fold_sum(lambda sc, kp: sc >= thr)
    no_bound = jnp.full(shape, NO_INDEX_BOUND, I32)

    def repair(_):
        v = thr
        strict = jnp.zeros(shape, F32)

        def in_set(sc, v, strict):
            return (sc > v) | ((sc == v) & (strict < 0.5))

        for _ in range(TIE_REPAIR_ROUNDS):
            c_set = fold_sum(lambda sc, kp: in_set(sc, v, strict))
            vmin = fold_min(lambda sc, kp: jnp.where(in_set(sc, v, strict), sc, jnp.inf))
            c_gt = fold_sum(lambda sc, kp: sc > vmin)
            drop = (c_set > k) & (c_gt >= k)
            v = jnp.where(drop, vmin, v)
            strict = jnp.where(drop, 1.0, strict)
        c_set = fold_sum(lambda sc, kp: in_set(sc, v, strict))
        vt = fold_min(lambda sc, kp: jnp.where(in_set(sc, v, strict), sc, jnp.inf))
        need = k - fold_sum(lambda sc, kp: sc > vt)

        def idx_body(it, m):
            cand = m | lax.shift_left(I32(1), index_bits - 1 - it)
            below = fold_sum(lambda sc, kp: (sc == vt) & (kp < cand))
            return jnp.where(below < need, cand, m)

        m = lax.fori_loop(0, index_bits, idx_body, jnp.zeros(shape, I32))
        tie = c_set > k
        v_out = jnp.where(tie, vt, v)
        m_out = jnp.where(tie, m, jnp.where(strict > 0.5, -1, no_bound))
        return v_out, m_out

    return lax.cond(jnp.max(c_ge) > k, repair, lambda _: (thr, no_bound), 0)


CHUNKS_PER_TRIP = 4


def _chunk_loop(n, body, init):
    def trip(j, c):
        for k in range(CHUNKS_PER_TRIP):
            c = body(CHUNKS_PER_TRIP * j + k, c)
        return c

    whole = n // CHUNKS_PER_TRIP
    carry = lax.fori_loop(0, whole, trip, init)
    return lax.fori_loop(whole * CHUNKS_PER_TRIP, n, body, carry)


def _dsa_prompt_kernel(q_ref, kv_ref, qia_ref, qib_ref, tailk_ref, tailq_ref, bias_ref, *rest, topk, n_cast):
    cast_in, rest = rest[:n_cast], rest[n_cast:]
    o_ref, rest = rest[0], rest[1:]
    cast_out, rest = rest[:n_cast], rest[n_cast:]
    kil_scr, kir_scr, sc_scr, lg_scr, m_scr, l_scr, acc_scr = rest
    i = pl.program_id(1)

    for src, dst in zip(cast_in, cast_out):
        dst[...] = src[...].astype(BF16)

    @pl.when(i == 0)
    def _():
        ki = tailk_ref[:, :IDX_DIM].astype(BF16)
        z = jnp.zeros_like(ki)
        kil_scr[...] = jnp.concatenate([ki, z], axis=-1)
        kir_scr[...] = jnp.concatenate([z, ki], axis=-1)

    wi = tailq_ref[:, IDX_DIM:IDX_DIM + IDX_HEADS] * (IDX_DIM ** -0.5)
    nk = (i + 2) // 2
    qpos = i * Q_BLOCK + lax.broadcasted_iota(I32, (Q_BLOCK, 1), 0)
    kloc = lax.broadcasted_iota(I32, (1, KEY_CHUNK), 1)
    heads_per_ref = qia_ref.shape[1] // IDX_DIM

    pairs = []
    for hp in range(IDX_HEADS // 2):
        ref = qia_ref if 2 * hp < heads_per_ref else qib_ref
        base = (2 * hp) % heads_per_ref * IDX_DIM
        pairs.append(ref[:, base:base + 2 * IDX_DIM])
    qstack = jnp.concatenate(pairs, axis=0)

    def score_chunk(c, carry):
        off = pl.multiple_of(c * KEY_CHUNK, KEY_CHUNK)
        acc = jnp.zeros((Q_BLOCK, KEY_CHUNK), F32)
        for side, k_scr in ((0, kil_scr), (1, kir_scr)):
            s = _dot_nt(qstack, k_scr[pl.ds(off, KEY_CHUNK), :])
            for hp in range(IDX_HEADS // 2):
                h = 2 * hp + side
                acc = acc + wi[:, h:h + 1] * jnp.maximum(s[hp * Q_BLOCK:(hp + 1) * Q_BLOCK], 0.0)
        kpos = c * KEY_CHUNK + kloc
        sc_scr[c] = jnp.where(kpos <= qpos, acc, -jnp.inf)
        return carry

    _chunk_loop(nk, score_chunk, 0)

    def fold_sum(pred):
        def body(c, cnt):
            m = jnp.where(pred(sc_scr[c], c * KEY_CHUNK + kloc), 1.0, 0.0)
            return cnt + m[:, :LANES] + m[:, LANES:]
        cnt = _chunk_loop(nk, body, jnp.zeros((Q_BLOCK, LANES), F32))
        return jnp.sum(cnt, axis=-1, keepdims=True)

    def fold_min(val):
        def body(c, acc):
            x = val(sc_scr[c], c * KEY_CHUNK + kloc)
            return jnp.minimum(acc, jnp.minimum(x[:, :LANES], x[:, LANES:]))
        acc = lax.fori_loop(0, nk, body, jnp.full((Q_BLOCK, LANES), jnp.inf, F32))
        return jnp.min(acc, axis=-1, keepdims=True)

    index_bits = (kv_ref.shape[0] - 1).bit_length()
    sel_v, sel_m = _topk_select(fold_sum, fold_min, (Q_BLOCK, 1), topk, index_bits)

    def mask_chunk(c, carry):
        kpos = c * KEY_CHUNK + kloc
        sc = sc_scr[c]
        chosen = (sc > sel_v) | ((sc == sel_v) & (kpos <= sel_m))
        sc_scr[c] = jnp.where(chosen & (kpos <= qpos), 0.0, -jnp.inf)
        return carry

    lax.fori_loop(0, nk, mask_chunk, 0)

    scale = B_HEAD_DIM ** -0.5
    rows = B_GQA * Q_BLOCK
    qgs = [jnp.concatenate(
        [q_ref[:, (g * B_GQA + hh) * B_HEAD_DIM:(g * B_GQA + hh + 1) * B_HEAD_DIM] for hh in range(B_GQA)],
        axis=0) for g in range(B_KV_HEADS)]
    m_scr[...] = jnp.full(m_scr.shape, -jnp.inf, F32)
    l_scr[...] = jnp.zeros(l_scr.shape, F32)
    acc_scr[...] = jnp.zeros(acc_scr.shape, F32)

    def logits_chunk(c, carry):
        off = pl.multiple_of(c * KEY_CHUNK, KEY_CHUNK)
        var = jnp.clip(2 * c - i, -(N_PAIR_VARIANTS - 1), 0) + (N_PAIR_VARIANTS - 1)
        sel = sc_scr[c]
        sel = jnp.concatenate([sel] * B_GQA, axis=0)
        for g in range(B_KV_HEADS):
            kc = kv_ref[pl.ds(off, KEY_CHUNK), g * B_HEAD_DIM:(g + 1) * B_HEAD_DIM]
            lg = _dot_nt(qgs[g], kc) * scale + bias_ref[var, g] + sel
            lg_scr[c, g] = lg
            m_scr[g] = jnp.maximum(m_scr[g], jnp.maximum(lg[:, :LANES], lg[:, LANES:]))
        return carry

    _chunk_loop(nk, logits_chunk, 0)
    for g in range(B_KV_HEADS):
        m = jnp.max(m_scr[g], axis=-1, keepdims=True)
        m_scr[g] = jnp.broadcast_to(m, (rows, LANES))

    def pv_chunk(c, carry):
        off = pl.multiple_of(c * KEY_CHUNK, KEY_CHUNK)
        for g in range(B_KV_HEADS):
            vc = kv_ref[pl.ds(off, KEY_CHUNK), KV_WIDTH + g * B_HEAD_DIM:KV_WIDTH + (g + 1) * B_HEAD_DIM]
            mb = m_scr[g]
            p = jnp.exp(lg_scr[c, g] - jnp.concatenate([mb, mb], axis=-1))
            l_scr[g] += p[:, :LANES] + p[:, LANES:]
            acc_scr[g] += jnp.dot(p.astype(BF16), vc, preferred_element_type=F32)
        return carry

    _chunk_loop(nk, pv_chunk, 0)
    for g in range(B_KV_HEADS):
        o = acc_scr[g] / jnp.sum(l_scr[g], axis=-1, keepdims=True)
        for hh in range(B_GQA):
            h = g * B_GQA + hh
            o_ref[:, h * B_HEAD_DIM:(h + 1) * B_HEAD_DIM] = o[hh * Q_BLOCK:(hh + 1) * Q_BLOCK].astype(BF16)


def _dsa_prompt(p, tail, bias_pairs, cast_weights, *, batch, seq):
    topk = min(TOPK_MAX, seq // 4)
    nblk = seq // Q_BLOCK
    qi_w = PROJ_TN
    steps = batch * nblk
    cast_specs, cast_shapes = [], []
    for w in cast_weights:
        ne, r, c = w.shape
        parts = steps // ne
        assert steps == ne * parts and r % (parts * BF16_SUBLANES) == 0
        spec = pl.BlockSpec((None, r // parts, c),
                            lambda b, i, parts=parts: ((b * nblk + i) // parts, (b * nblk + i) % parts, 0))
        cast_specs.append(spec)
        cast_shapes.append(jax.ShapeDtypeStruct(w.shape, BF16))
    outs = pl.pallas_call(
        functools.partial(_dsa_prompt_kernel, topk=topk, n_cast=len(cast_weights)),
        grid=(batch, nblk),
        in_specs=[
            pl.BlockSpec((Q_BLOCK, B_WIDTH), lambda b, i: (b * nblk + i, COL_Q // B_WIDTH)),
            pl.BlockSpec((seq, 2 * KV_WIDTH), lambda b, i: (b, COL_K // (2 * KV_WIDTH))),
            pl.BlockSpec((Q_BLOCK, qi_w), lambda b, i: (b * nblk + i, COL_QI // qi_w)),
            pl.BlockSpec((Q_BLOCK, qi_w), lambda b, i: (b * nblk + i, COL_QI // qi_w + 1)),
            pl.BlockSpec((seq, LANES), lambda b, i: (b, 0)),
            pl.BlockSpec((Q_BLOCK, LANES), lambda b, i: (b * nblk + i, 0)),
            pl.BlockSpec(bias_pairs.shape, lambda b, i: (0, 0, 0, 0)),
        ] + cast_specs,
        out_specs=[pl.BlockSpec((Q_BLOCK, B_WIDTH), lambda b, i: (b * nblk + i, 0))] + cast_specs,
        out_shape=[jax.ShapeDtypeStruct((batch * seq, B_WIDTH), BF16)] + cast_shapes,
        scratch_shapes=[
            pltpu.VMEM((seq, 2 * IDX_DIM), BF16),
            pltpu.VMEM((seq, 2 * IDX_DIM), BF16),
            pltpu.VMEM((seq // KEY_CHUNK, Q_BLOCK, KEY_CHUNK), F32),
            pltpu.VMEM((seq // KEY_CHUNK, B_KV_HEADS, B_GQA * Q_BLOCK, KEY_CHUNK), F32),
            pltpu.VMEM((B_KV_HEADS, B_GQA * Q_BLOCK, LANES), F32),
            pltpu.VMEM((B_KV_HEADS, B_GQA * Q_BLOCK, LANES), F32),
            pltpu.VMEM((B_KV_HEADS, B_GQA * Q_BLOCK, B_HEAD_DIM), F32),
        ],
        compiler_params=_cparams(("arbitrary", "arbitrary")),
        name="dsa_prompt",
    )(p, p, p, p, tail, tail, bias_pairs, *cast_weights)
    return outs[0], tuple(outs[1:])


PAGES_PER_STEP = 64
SAMPLE_SUB_PAGES = 32
SCORE_KEYS_PER_DOT = 2048
FOLD_VREGS = 16
SELECT_ROWS = 64


def _fetch_pages(pt_ref, sources, bufs, sem, step, n_steps_total, steps_per_sample, pages_per_step):
    slot = step % 2

    def copies(n, slot_, page_of):
        sample = n // steps_per_sample
        first = (n % steps_per_sample) * pages_per_step
        for r in range(pages_per_step):
            page = page_of(sample, first + r)
            for a, (src, buf) in enumerate(zip(sources, bufs)):
                yield pltpu.make_async_copy(src.at[page], buf.at[slot_, r], sem.at[a, slot_])

    def table(sample, idx):
        return pt_ref[sample, idx]

    @pl.when(step == 0)
    def _():
        for cp in copies(step, slot, table):
            cp.start()

    @pl.when(step + 1 < n_steps_total)
    def _():
        for cp in copies(step + 1, 1 - slot, table):
            cp.start()

    for cp in copies(step, slot, lambda sample, idx: 0):
        cp.wait()
    return slot


def _dsa_sample_score_kernel(pt_ref, idx_hbm, qi_ref, wi_ref, kin_ref, sc_ref, scn_ref,
                             page_buf, sem, *, dec_seq, n_pages, n_samples):
    slot = _fetch_pages(pt_ref, [idx_hbm], [page_buf], sem, pl.program_id(0), n_samples, 1, n_pages)
    pages = [page_buf.at[slot, r] for r in range(n_pages)]
    pages_per_dot = SCORE_KEYS_PER_DOT // PAGE_SIZE
    qi = qi_ref[...]
    wi = wi_ref[...] * (IDX_DIM ** -0.5)

    def scores(keys_t):
        r = jnp.maximum(jnp.dot(qi, keys_t, preferred_element_type=F32), 0.0) * wi
        acc = r[0:SUBLANES]
        for j in range(1, IDX_HEADS * dec_seq // SUBLANES):
            acc = acc + r[j * SUBLANES:(j + 1) * SUBLANES]
        shift = dec_seq
        while shift < SUBLANES:
            acc = acc + pltpu.roll(acc, shift, axis=0)
            shift *= 2
        return acc

    for d in range(n_pages // pages_per_dot):
        keys_t = jnp.concatenate(
            [pg[...].astype(BF16) for pg in pages[d * pages_per_dot:(d + 1) * pages_per_dot]], axis=1)
        sc_ref[:, d * SCORE_KEYS_PER_DOT:(d + 1) * SCORE_KEYS_PER_DOT] = scores(keys_t)

    row = lax.broadcasted_iota(I32, (SUBLANES, LANES), 0)
    col = lax.broadcasted_iota(I32, (SUBLANES, LANES), 1)
    scn_ref[...] = jnp.where((col <= row % dec_seq) & (col < dec_seq), scores(kin_ref[...]), -jnp.inf)


def _sample_select_kernel(sc_ref, scn_ref, selv_ref, selm_ref, *, topk):
    rows, past = sc_ref.shape
    col = lax.broadcasted_iota(I32, (1, LANES), 1)

    def pieces():
        yield scn_ref[...], past + col
        for w in range(past // LANES):
            yield sc_ref[:, w * LANES:(w + 1) * LANES], w * LANES + col

    ways = max(1, FOLD_VREGS // (rows // SUBLANES))

    def fold(term, combine):
        parts = [None] * ways
        for n, (sc, kp) in enumerate(pieces()):
            x = term(sc, kp)
            parts[n % ways] = x if parts[n % ways] is None else combine(parts[n % ways], x)
        parts = [p for p in parts if p is not None]
        while len(parts) > 1:
            parts = [combine(parts[j], parts[j + 1]) if j + 1 < len(parts) else parts[j]
                     for j in range(0, len(parts), 2)]
        return parts[0]

    def fold_sum(pred):
        cnt = fold(lambda sc, kp: jnp.where(pred(sc, kp), 1.0, 0.0), jnp.add)
        return jnp.sum(cnt, axis=-1, keepdims=True)

    def fold_min(val):
        return jnp.min(fold(val, jnp.minimum), axis=-1, keepdims=True)

    index_bits = (past + LANES - 1).bit_length()
    sel_v, sel_m = _topk_select(fold_sum, fold_min, (rows, 1), topk, index_bits)
    selv_ref[...] = jnp.broadcast_to(sel_v, (rows, LANES))
    selm_ref[...] = jnp.broadcast_to(sel_m, (rows, LANES))


def _dsa_sample_attend_kernel(pt_ref, ck_hbm, cv_hbm, q_ref, sc_ref, scn_ref, selv_ref, selm_ref, kvn_ref,
                              sbias_ref, sconst_ref, o_ref, kbuf, vbuf, sem, m_scr, l_scr, acc_scr,
                              *, dec_seq, n_steps, n_samples):
    s = pl.program_id(1)
    slot = _fetch_pages(pt_ref, [ck_hbm, cv_hbm], [kbuf, vbuf], sem, pl.program_id(0) * n_steps + s,
                        n_samples * n_steps, n_steps, PAGES_PER_STEP)
    kpages = [kbuf.at[slot, r] for r in range(PAGES_PER_STEP)]
    vpages = [vbuf.at[slot, r] for r in range(PAGES_PER_STEP)]
    chunk = PAGES_PER_STEP * PAGE_SIZE
    rows = B_GQA * SUBLANES
    scale = B_HEAD_DIM ** -0.5

    @pl.when(s == 0)
    def _():
        m_scr[...] = jnp.full(m_scr.shape, NEG_INIT, F32)
        l_scr[...] = jnp.zeros(l_scr.shape, F32)
        acc_scr[...] = jnp.zeros(acc_scr.shape, F32)

    sel_v = selv_ref[:, 0:1]
    sel_m = selm_ref[:, 0:1]

    def chosen(sc, kpos):
        return jnp.where((sc > sel_v) | ((sc == sel_v) & (kpos <= sel_m)), 0.0, -jnp.inf)

    def head_rows(pgs, g):
        return jnp.concatenate(
            [pg[pl.ds(g, PAGE_SIZE, stride=B_KV_HEADS), :].astype(BF16) for pg in pgs], axis=0)

    kcol = lax.broadcasted_iota(I32, (SUBLANES, chunk), 1)
    sel = chosen(sc_ref[...], s * chunk + kcol)
    sel = jnp.concatenate([sel] * B_GQA, axis=0)
    is_last = s == n_steps - 1
    ncol = lax.broadcasted_iota(I32, (SUBLANES, LANES), 1)
    seln = chosen(scn_ref[...], n_steps * chunk + ncol)
    seln = jnp.where(is_last & (scn_ref[...] > -jnp.inf), seln, -jnp.inf)
    seln = jnp.concatenate([seln] * B_GQA, axis=0)

    def partial(lg, v):
        m = jnp.maximum(jnp.max(lg, axis=-1, keepdims=True), NEG_INIT)
        p = jnp.exp(lg - m)
        return m, jnp.sum(p, axis=-1, keepdims=True), jnp.dot(p.astype(BF16), v, preferred_element_type=F32)

    sub = SAMPLE_SUB_PAGES * PAGE_SIZE
    nsub = PAGES_PER_STEP // SAMPLE_SUB_PAGES
    lane_sub = lax.broadcasted_iota(I32, (rows, sub), 1)
    for g in range(B_KV_HEADS):
        qg = q_ref[g]
        far = sconst_ref[g, :, 0:1]
        near = jnp.concatenate([jnp.zeros((rows, sub - PAGE_SIZE), F32), sbias_ref[g, :, :PAGE_SIZE]], axis=-1)
        parts = [(m_scr[g], l_scr[g], acc_scr[g])]
        for j in range(nsub):
            pj = slice(j * SAMPLE_SUB_PAGES, (j + 1) * SAMPLE_SUB_PAGES)
            lg = _dot_nt(qg, head_rows(kpages[pj], g)) * scale + sel[:, j * sub:(j + 1) * sub]
            if j == nsub - 1:
                lg = lg + jnp.where(is_last & (lane_sub >= sub - PAGE_SIZE), near, far)
            else:
                lg = lg + far
            parts.append(partial(lg, head_rows(vpages[pj], g)))
        kn = kvn_ref[:, g * B_HEAD_DIM:(g + 1) * B_HEAD_DIM]
        vn = kvn_ref[:, KV_WIDTH + g * B_HEAD_DIM:KV_WIDTH + (g + 1) * B_HEAD_DIM]
        parts.append(partial(_dot_nt(qg, kn) * scale + sbias_ref[g, :, PAGE_SIZE:] + seln, vn))
        m_new = parts[0][0]
        for m, _, _ in parts[1:]:
            m_new = jnp.maximum(m_new, m)
        l_new = jnp.zeros((rows, 1), F32)
        acc_new = jnp.zeros((rows, B_HEAD_DIM), F32)
        for m, l, acc in parts:
            w = jnp.exp(m - m_new)
            l_new = l_new + w * l
            acc_new = acc_new + w * acc
        m_scr[g] = m_new
        l_scr[g] = l_new
        acc_scr[g] = acc_new

    @pl.when(is_last)
    def _():
        for g in range(B_KV_HEADS):
            o_ref[g] = (acc_scr[g] / l_scr[g]).astype(BF16)


def _dsa_sample(p_s, tail_s, k_s, v_s, cache_k_l, cache_v_l, cache_idx_l, page_table, sbias, sconst,
                *, dec_batch, dec_seq):
    n_pages = page_table.shape[1]
    past = n_pages * PAGE_SIZE
    topk = min(TOPK_MAX, (past + dec_seq) // 4)
    n_steps = n_pages // PAGES_PER_STEP
    chunk = PAGES_PER_STEP * PAGE_SIZE
    rows = B_GQA * SUBLANES
    n_pool = cache_k_l.shape[0]

    assert SUBLANES % dec_seq == 0
    qi = p_s[:, COL_QI:COL_MAIN].reshape(dec_batch, dec_seq, IDX_HEADS, IDX_DIM).transpose(0, 2, 1, 3)
    qi = qi.reshape(dec_batch, IDX_HEADS * dec_seq, IDX_DIM)
    wi = tail_s[:, IDX_DIM:IDX_DIM + IDX_HEADS].reshape(dec_batch, dec_seq, IDX_HEADS).transpose(0, 2, 1)
    wi = wi.reshape(dec_batch, IDX_HEADS * dec_seq, 1)
    ki_new = tail_s[:, :IDX_DIM].astype(BF16).reshape(dec_batch, dec_seq, IDX_DIM)
    ki_new = jnp.pad(ki_new, ((0, 0), (0, LANES - dec_seq), (0, 0))).transpose(0, 2, 1)
    q = p_s[:, COL_Q:COL_K].reshape(dec_batch, dec_seq, B_KV_HEADS, B_GQA, B_HEAD_DIM)
    q = jnp.pad(q.transpose(0, 2, 3, 1, 4), ((0, 0), (0, 0), (0, 0), (0, SUBLANES - dec_seq), (0, 0)))
    q = q.reshape(dec_batch, B_KV_HEADS, rows, B_HEAD_DIM)
    kv_new = jnp.concatenate([k_s, v_s], axis=-1).astype(BF16).reshape(dec_batch, dec_seq, 2 * KV_WIDTH)
    kv_new = jnp.pad(kv_new, ((0, 0), (0, LANES - dec_seq), (0, 0)))

    cache_idx_t = jnp.swapaxes(cache_idx_l, 1, 2)
    stat_spec = pl.BlockSpec((None, SUBLANES, LANES), lambda b, pt: (b, 0, 0))
    sc, scn = pl.pallas_call(
        functools.partial(_dsa_sample_score_kernel, dec_seq=dec_seq, n_pages=n_pages, n_samples=dec_batch),
        grid_spec=pltpu.PrefetchScalarGridSpec(
            num_scalar_prefetch=1,
            grid=(dec_batch,),
            in_specs=[
                pl.BlockSpec(memory_space=pl.ANY),
                pl.BlockSpec((None, IDX_HEADS * dec_seq, IDX_DIM), lambda b, pt: (b, 0, 0)),
                pl.BlockSpec((None, IDX_HEADS * dec_seq, 1), lambda b, pt: (b, 0, 0)),
                pl.BlockSpec((None, IDX_DIM, LANES), lambda b, pt: (b, 0, 0)),
            ],
            out_specs=[pl.BlockSpec((None, SUBLANES, past), lambda b, pt: (b, 0, 0)), stat_spec],
            scratch_shapes=[
                pltpu.VMEM((2, n_pages, IDX_DIM, PAGE_SIZE), F32),
                pltpu.SemaphoreType.DMA((1, 2)),
            ],
        ),
        out_shape=[
            jax.ShapeDtypeStruct((dec_batch, SUBLANES, past), F32),
            jax.ShapeDtypeStruct((dec_batch, SUBLANES, LANES), F32),
        ],
        compiler_params=_cparams(("arbitrary",)),
        name="dsa_sample_score",
    )(page_table, cache_idx_t, qi, wi, ki_new)

    n_rows = dec_batch * SUBLANES
    sel_rows = math.gcd(n_rows, SELECT_ROWS)
    row_spec = pl.BlockSpec((sel_rows, LANES), lambda i: (i, 0))
    selv, selm = pl.pallas_call(
        functools.partial(_sample_select_kernel, topk=topk),
        grid=(n_rows // sel_rows,),
        in_specs=[pl.BlockSpec((sel_rows, past), lambda i: (i, 0)), row_spec],
        out_specs=[row_spec, row_spec],
        out_shape=[jax.ShapeDtypeStruct((n_rows, LANES), F32), jax.ShapeDtypeStruct((n_rows, LANES), I32)],
        compiler_params=_cparams(("arbitrary",)),
        name="sample_select",
    )(sc.reshape(n_rows, past), scn.reshape(n_rows, LANES))
    selv = selv.reshape(dec_batch, SUBLANES, LANES)
    selm = selm.reshape(dec_batch, SUBLANES, LANES)

    ck = cache_k_l.reshape(n_pool, PAGE_SIZE * B_KV_HEADS, B_HEAD_DIM)
    cv = cache_v_l.reshape(n_pool, PAGE_SIZE * B_KV_HEADS, B_HEAD_DIM)
    kv_buf = pltpu.VMEM((2, PAGES_PER_STEP, PAGE_SIZE * B_KV_HEADS, B_HEAD_DIM), F32)
    o = pl.pallas_call(
        functools.partial(_dsa_sample_attend_kernel, dec_seq=dec_seq, n_steps=n_steps, n_samples=dec_batch),
        grid_spec=pltpu.PrefetchScalarGridSpec(
            num_scalar_prefetch=1,
            grid=(dec_batch, n_steps),
            in_specs=[pl.BlockSpec(memory_space=pl.ANY), pl.BlockSpec(memory_space=pl.ANY)] + [
                pl.BlockSpec((None, B_KV_HEADS, rows, B_HEAD_DIM), lambda b, s, pt: (b, 0, 0, 0)),
                pl.BlockSpec((None, SUBLANES, chunk), lambda b, s, pt: (b, 0, s)),
                pl.BlockSpec((None, SUBLANES, LANES), lambda b, s, pt: (b, 0, 0)),
                pl.BlockSpec((None, SUBLANES, LANES), lambda b, s, pt: (b, 0, 0)),
                pl.BlockSpec((None, SUBLANES, LANES), lambda b, s, pt: (b, 0, 0)),
                pl.BlockSpec((None, LANES, 2 * KV_WIDTH), lambda b, s, pt: (b, 0, 0)),
                pl.BlockSpec(sbias.shape, lambda b, s, pt: (0, 0, 0)),
                pl.BlockSpec(sconst.shape, lambda b, s, pt: (0, 0, 0)),
            ],
            out_specs=pl.BlockSpec((None, B_KV_HEADS, rows, B_HEAD_DIM), lambda b, s, pt: (b, 0, 0, 0)),
            scratch_shapes=[
                kv_buf,
                kv_buf,
                pltpu.SemaphoreType.DMA((2, 2)),
                pltpu.VMEM((B_KV_HEADS, rows, 1), F32),
                pltpu.VMEM((B_KV_HEADS, rows, 1), F32),
                pltpu.VMEM((B_KV_HEADS, rows, B_HEAD_DIM), F32),
            ],
        ),
        out_shape=jax.ShapeDtypeStruct((dec_batch, B_KV_HEADS, rows, B_HEAD_DIM), BF16),
        compiler_params=_cparams(("arbitrary", "arbitrary")),
        name="dsa_sample_attend",
    )(page_table, ck, cv, q, sc, scn, selv, selm, kv_new, sbias, sconst)
    o = o.reshape(dec_batch, B_KV_HEADS, B_GQA, SUBLANES, B_HEAD_DIM)[:, :, :, :dec_seq].transpose(0, 3, 1, 2, 4)
    return o.reshape(dec_batch * dec_seq, B_WIDTH)


def _mixout_kernel(u_ref, vn_ref, b_ref, ws_ref, ab_ref, x_ref, w_ref, o_ref, mix_scr, *, tm):
    j = pl.program_id(1)

    @pl.when(j == 0)
    def _():
        row = lax.broadcasted_iota(I32, (A_CHUNK, A_CHUNK), 0)
        col = lax.broadcasted_iota(I32, (A_CHUNK, A_CHUNK), 1)
        for g in range(A_GROUPS):
            wsg = jnp.where(row >= col, ws_ref[g], 0.0).astype(BF16)
            bg = ab_ref[:, g:g + 1]
            cs = slice(g * A_CH, (g + 1) * A_CH)
            for r in range(tm // A_CHUNK):
                rs = slice(r * A_CHUNK, (r + 1) * A_CHUNK)
                sm = jnp.dot(wsg, vn_ref[rs, cs], preferred_element_type=F32) + bg
                mix_scr[rs, cs] = (u_ref[rs, cs].astype(F32) * sm).astype(BF16)
        mix_scr[:, A_WIDTH:] = b_ref[...]

    o_ref[...] = x_ref[...] + jnp.dot(mix_scr[...], w_ref[...].astype(BF16), preferred_element_type=F32)


def _mixout(p, b_out, ws_eff, ab_eff, x2d, w_out_l, *, tm, tn=D_MODEL):
    n = x2d.shape[0]
    return pl.pallas_call(
        functools.partial(_mixout_kernel, tm=tm),
        grid=(n // tm, D_MODEL // tn),
        in_specs=[
            pl.BlockSpec((tm, A_WIDTH), lambda i, j: (i, COL_U // A_WIDTH)),
            pl.BlockSpec((tm, A_WIDTH), lambda i, j: (i, COL_V // A_WIDTH)),
            pl.BlockSpec((tm, B_WIDTH), lambda i, j: (i, 0)),
            pl.BlockSpec((A_GROUPS, A_CHUNK, A_CHUNK), lambda i, j: (0, 0, 0)),
            pl.BlockSpec((A_CHUNK, A_GROUPS), lambda i, j: (0, 0)),
            pl.BlockSpec((tm, tn), lambda i, j: (i, j)),
            pl.BlockSpec((A_WIDTH + B_WIDTH, tn), lambda i, j: (0, j)),
        ],
        out_specs=pl.BlockSpec((tm, tn), lambda i, j: (i, j)),
        out_shape=jax.ShapeDtypeStruct((n, D_MODEL), F32),
        scratch_shapes=[pltpu.VMEM((tm, A_WIDTH + B_WIDTH), BF16)],
        compiler_params=_cparams(("arbitrary", "arbitrary")),
        name="mixout",
    )(p, p, b_out, ws_eff, ab_eff, x2d, w_out_l)


def _memkv_kernel(x_ref, wk_ref, wv_ref, kn_ref, mk_ref, mv_ref):
    x = x_ref[...].astype(BF16)
    tm = x.shape[0]
    mk = _head_norm(jnp.dot(x, wk_ref[...].astype(BF16), preferred_element_type=F32), kn_ref[...])
    mv = jnp.dot(x, wv_ref[...].astype(BF16), preferred_element_type=F32)
    for hd in range(MEM_HEADS):
        cs = slice(hd * MEM_HEAD_DIM, (hd + 1) * MEM_HEAD_DIM)
        mk_ref[pl.ds(hd, tm, stride=MEM_HEADS), :] = mk[:, cs]
        mv_ref[pl.ds(hd, tm, stride=MEM_HEADS), :] = mv[:, cs]


def _memkv(mem2d, wk, wv, knorm, *, tm=256):
    n = mem2d.shape[0]
    kn = jnp.tile(knorm, MEM_HEADS).reshape(1, MEM_WIDTH)
    return pl.pallas_call(
        _memkv_kernel,
        grid=(n // tm,),
        in_specs=[
            pl.BlockSpec((tm, D_MODEL), lambda i: (i, 0)),
            pl.BlockSpec((D_MODEL, MEM_WIDTH), lambda i: (0, 0)),
            pl.BlockSpec((D_MODEL, MEM_WIDTH), lambda i: (0, 0)),
            pl.BlockSpec((1, MEM_WIDTH), lambda i: (0, 0)),
        ],
        out_specs=[pl.BlockSpec((tm * MEM_HEADS, MEM_HEAD_DIM), lambda i: (i, 0))] * 2,
        out_shape=[jax.ShapeDtypeStruct((n * MEM_HEADS, MEM_HEAD_DIM), F32)] * 2,
        compiler_params=_cparams(("arbitrary",)),
        name="memkv",
    )(mem2d, wk, wv, kn)


def _memattn_kernel(h_ref, g_ref, wq_ref, qn_ref, mk_ref, mv_ref, wo_ref, gf_ref, wc_ref, bc_ref,
                    o_ref, n_ref, c_ref, *, rows_per_mem, mem_len):
    tm, m = h_ref.shape[0], mk_ref.shape[0] // MEM_HEADS
    wq = wq_ref[...].astype(BF16)
    wo = wo_ref[...].astype(BF16)
    mks = [mk_ref[pl.ds(hd, m, stride=MEM_HEADS), :].astype(BF16) for hd in range(MEM_HEADS)]
    mvs = [mv_ref[pl.ds(hd, m, stride=MEM_HEADS), :].astype(BF16) for hd in range(MEM_HEADS)]
    rc = min(tm, MEMATTN_ROW_CHUNK)
    for r in range(tm // rc):
        rs = slice(r * rc, (r + 1) * rc)
        h = h_ref[rs, :]
        n = _row_norm(h, g_ref[...]).astype(BF16)
        q = _head_norm(jnp.dot(n, wq, preferred_element_type=F32), qn_ref[...]).astype(BF16)
        if rows_per_mem is not None:
            row = r * rc + lax.broadcasted_iota(I32, (rc, m), 0)
            col = lax.broadcasted_iota(I32, (rc, m), 1)
            mask = jnp.where(row // rows_per_mem == col // mem_len, 0.0, -jnp.inf)
        outs = []
        for hd in range(MEM_HEADS):
            cs = slice(hd * MEM_HEAD_DIM, (hd + 1) * MEM_HEAD_DIM)
            lg = _dot_nt(q[:, cs], mks[hd]) * (MEM_HEAD_DIM ** -0.5)
            if rows_per_mem is not None:
                lg = lg + mask
            e = jnp.exp(lg - jnp.max(lg, axis=-1, keepdims=True))
            l = jnp.sum(e, axis=-1, keepdims=True)
            outs.append(jnp.dot(e.astype(BF16), mvs[hd], preferred_element_type=F32) / l)
        o = jnp.concatenate(outs, axis=-1).astype(BF16)
        h2 = h + jnp.dot(o, wo, preferred_element_type=F32)
        o_ref[rs, :] = h2
        n_ref[rs, :], c_ref[rs, :] = _route(h2, gf_ref[...], wc_ref[...], bc_ref[...])


def _memattn(h2d, gain, wq, qnorm, mk, mv, wo, gain_ffn, w_cat, b_cat, *, tm, mem_rows, tiles_per_mem,
             rows_per_mem, mem_len):
    n = h2d.shape[0]
    qn = jnp.tile(qnorm, MEM_HEADS).reshape(1, MEM_WIDTH)
    return pl.pallas_call(
        functools.partial(_memattn_kernel, rows_per_mem=rows_per_mem, mem_len=mem_len),
        grid=(n // tm,),
        in_specs=[
            pl.BlockSpec((tm, D_MODEL), lambda i: (i, 0)),
            pl.BlockSpec((1, D_MODEL), lambda i: (0, 0)),
            pl.BlockSpec((D_MODEL, MEM_WIDTH), lambda i: (0, 0)),
            pl.BlockSpec((1, MEM_WIDTH), lambda i: (0, 0)),
            pl.BlockSpec((mem_rows * MEM_HEADS, MEM_HEAD_DIM), lambda i: (i // tiles_per_mem, 0)),
            pl.BlockSpec((mem_rows * MEM_HEADS, MEM_HEAD_DIM), lambda i: (i // tiles_per_mem, 0)),
            pl.BlockSpec((MEM_WIDTH, D_MODEL), lambda i: (0, 0)),
            pl.BlockSpec((1, D_MODEL), lambda i: (0, 0)),
            pl.BlockSpec((D_MODEL, 2 * LANES), lambda i: (0, 0)),
            pl.BlockSpec((1, 2 * LANES), lambda i: (0, 0)),
        ],
        out_specs=[pl.BlockSpec((tm, D_MODEL), lambda i: (i, 0)), pl.BlockSpec((tm, D_MODEL), lambda i: (i, 0)),
                   pl.BlockSpec((tm, LANES), lambda i: (i, 0))],
        out_shape=[jax.ShapeDtypeStruct((n, D_MODEL), F32), jax.ShapeDtypeStruct((n, D_MODEL), BF16),
                   jax.ShapeDtypeStruct((n, LANES), F32)],
        compiler_params=_cparams(("arbitrary",)),
        name="memattn",
    )(h2d, gain.reshape(1, D_MODEL), wq, qn, mk, mv, wo, gain_ffn.reshape(1, D_MODEL), w_cat, b_cat)


def _route(h, gain, w_cat, b_cat):
    n = _row_norm(h, gain).astype(BF16)
    lg = jnp.dot(n, w_cat.astype(BF16), preferred_element_type=F32) + b_cat
    gl, el = lg[:, :LANES], lg[:, LANES:]
    lane = lax.broadcasted_iota(I32, gl.shape, 1)
    lanef = lane.astype(F32)

    def first_max_lane(v, vmax):
        return jnp.min(jnp.where(v == vmax, lanef, float(LANES)), axis=-1, keepdims=True).astype(I32)

    gl = jnp.where(lane < N_GROUPS, gl, -jnp.inf)
    ge = jnp.exp(gl - jnp.max(gl, axis=-1, keepdims=True))
    gp = ge / jnp.sum(ge, axis=-1, keepdims=True)
    g_gate = jnp.max(gp, axis=-1, keepdims=True)
    g_sel = first_max_lane(gp, g_gate)
    in_grp = (lane // EXPERTS_PER_GROUP == g_sel) & (lane < N_EXPERTS)
    el = jnp.where(in_grp, el, -jnp.inf)
    ee = jnp.exp(el - jnp.max(el, axis=-1, keepdims=True))
    ep = jnp.where(in_grp, ee / jnp.sum(ee, axis=-1, keepdims=True), -jnp.inf)
    w1 = jnp.max(ep, axis=-1, keepdims=True)
    i1 = first_max_lane(ep, w1)
    ep2 = jnp.where(lane == i1, -jnp.inf, ep)
    w2 = jnp.max(ep2, axis=-1, keepdims=True)
    i2 = first_max_lane(ep2, w2)
    tot = w1 + w2
    comb = jnp.where(lane == i1, w1 / tot * g_gate, jnp.where(lane == i2, w2 / tot * g_gate, 0.0))
    return n, jnp.where(lane == GSEL_LANE, g_sel.astype(F32), comb)


def _cast_kernel(x_ref, o_ref):
    o_ref[...] = x_ref[...].astype(BF16)


def _cast_bf16(w):
    ne, r, c = w.shape
    return pl.pallas_call(
        _cast_kernel,
        grid=(ne,),
        in_specs=[pl.BlockSpec((None, r, c), lambda e: (e, 0, 0))],
        out_specs=pl.BlockSpec((None, r, c), lambda e: (e, 0, 0)),
        out_shape=jax.ShapeDtypeStruct(w.shape, BF16),
        compiler_params=_cparams(("arbitrary",)),
        name="cast_bf16",
    )(w)


GSEL_LANE = N_EXPERTS
MOE_ROWS = 128
MOE_MAIN_SLACK = 0
MOE_FF_SPLIT = 1


def _moe_routed_kernel(n_ref, c_ref, h_hbm, wg_ref, wu_ref, wd_ref, o_ref,
                       tri_scr, key_scr, xc_scr, yc_scr, cw_scr, nch_ref, sem, *, tm):
    i, g, e, f = (pl.program_id(k) for k in range(4))
    main_rows = tm // N_GROUPS + MOE_MAIN_SLACK
    first_e = (e == 0) & (f == 0)
    last_e = (e == EXPERTS_PER_GROUP - 1) & (f == MOE_FF_SPLIT - 1)

    def onehot_rows(start, rows):
        want = (start + lax.broadcasted_iota(I32, (rows, 1), 0)).astype(F32)
        return jnp.where(key_scr[pl.ds(g, 1), :] == want, 1.0, 0.0).astype(BF16)

    def tail_start(c):
        return pl.multiple_of(main_rows + c * MOE_ROWS, math.gcd(main_rows, MOE_ROWS))

    @pl.when((i == 0) & (g == 0) & first_e)
    def _():
        r = lax.broadcasted_iota(I32, (tm, tm), 0)
        c = lax.broadcasted_iota(I32, (tm, tm), 1)
        tri_scr[...] = jnp.where(r < c, 1.0, 0.0).astype(BF16)

    def residual_copy():
        return pltpu.make_async_copy(h_hbm.at[pl.ds(pl.multiple_of(i * tm, tm), tm)], o_ref, sem)

    @pl.when((g == 0) & first_e)
    def _():
        residual_copy().start()
        lane = lax.broadcasted_iota(I32, (tm, LANES), 1).astype(F32)
        onehot = jnp.where(lane == c_ref[:, GSEL_LANE:GSEL_LANE + 1], 1.0, 0.0).astype(BF16)
        er = lax.broadcasted_iota(I32, (SUBLANES, LANES), 0)
        ec = lax.broadcasted_iota(I32, (SUBLANES, LANES), 1)
        eye = jnp.where(er == ec, 1.0, 0.0).astype(BF16)
        mask_t = _dot_nt(eye, onehot)
        rank_t = jnp.dot(mask_t.astype(BF16), tri_scr[...], preferred_element_type=F32)
        key_scr[...] = jnp.where(mask_t > 0.5, rank_t, -1.0)
        for gg in range(N_GROUPS):
            cnt = jnp.sum(mask_t[gg:gg + 1, :], axis=-1, keepdims=True)
            extra = jnp.ceil(jnp.maximum(cnt - main_rows, 0.0) * (1.0 / MOE_ROWS))
            nch_ref[gg] = jnp.sum(extra).astype(I32)

    ntail = nch_ref[g]

    @pl.when(first_e)
    def _():
        comb = c_ref[...]
        hi = comb.astype(BF16)
        r1 = comb - hi.astype(F32)
        mid = r1.astype(BF16)
        lo = (r1 - mid.astype(F32)).astype(BF16)

        def compact(start, rows):
            s = onehot_rows(start, rows)
            xc_scr[pl.ds(start, rows), :] = jnp.dot(s, n_ref[...], preferred_element_type=F32).astype(BF16)
            cw_scr[pl.ds(start, rows), :] = (jnp.dot(s, hi, preferred_element_type=F32)
                                             + jnp.dot(s, mid, preferred_element_type=F32)
                                             + jnp.dot(s, lo, preferred_element_type=F32))
            yc_scr[pl.ds(start, rows), :] = jnp.zeros((rows, D_MODEL), F32)

        compact(0, main_rows)
        lax.fori_loop(0, ntail, lambda c, carry: (compact(tail_start(c), MOE_ROWS), carry)[1], 0)

    wg = wg_ref[...].astype(BF16)
    wu = wu_ref[...].astype(BF16)
    wd = wd_ref[...].astype(BF16)
    ex = g * EXPERTS_PER_GROUP + e

    def expert(start, rows):
        x = xc_scr[pl.ds(start, rows), :]
        a = jnp.dot(x, wg, preferred_element_type=F32)
        u = jnp.dot(x, wu, preferred_element_type=F32)
        lane = lax.broadcasted_iota(I32, (rows, LANES), 1)
        cw = jnp.sum(jnp.where(lane == ex, cw_scr[pl.ds(start, rows), :], 0.0), axis=-1, keepdims=True)
        hid = (a * jax.nn.sigmoid(a) * u * cw).astype(BF16)
        yc_scr[pl.ds(start, rows), :] += jnp.dot(hid, wd, preferred_element_type=F32)

    expert(0, main_rows)
    lax.fori_loop(0, ntail, lambda c, carry: (expert(tail_start(c), MOE_ROWS), carry)[1], 0)

    @pl.when((g == 0) & last_e)
    def _():
        residual_copy().wait()

    @pl.when(last_e)
    def _():
        def scatter(start, rows):
            s = onehot_rows(start, rows)
            y = yc_scr[pl.ds(start, rows), :].astype(BF16)
            o_ref[...] += lax.dot_general(s, y, (((0,), (0,)), ((), ())), preferred_element_type=F32)

        scatter(0, main_rows)
        lax.fori_loop(0, ntail, lambda c, carry: (scatter(tail_start(c), MOE_ROWS), carry)[1], 0)


def _moe_routed(n3, comb, h2d, w_gate_l, w_up_l, w_down_l, *, tm):
    n = h2d.shape[0]
    ff = EXPERT_FF // MOE_FF_SPLIT

    def wmap(i, g, e, f):
        return (g * EXPERTS_PER_GROUP + e, 0, f)

    return pl.pallas_call(
        functools.partial(_moe_routed_kernel, tm=tm),
        grid=(n // tm, N_GROUPS, EXPERTS_PER_GROUP, MOE_FF_SPLIT),
        in_specs=[
            pl.BlockSpec((tm, D_MODEL), lambda i, g, e, f: (i, 0), pipeline_mode=pl.Buffered(1)),
            pl.BlockSpec((tm, LANES), lambda i, g, e, f: (i, 0), pipeline_mode=pl.Buffered(1)),
            pl.BlockSpec(memory_space=pl.ANY),
            pl.BlockSpec((None, D_MODEL, ff), wmap),
            pl.BlockSpec((None, D_MODEL, ff), wmap),
            pl.BlockSpec((None, ff, D_MODEL), lambda i, g, e, f: (g * EXPERTS_PER_GROUP + e, f, 0)),
        ],
        out_specs=pl.BlockSpec((tm, D_MODEL), lambda i, g, e, f: (i, 0)),
        out_shape=jax.ShapeDtypeStruct((n, D_MODEL), F32),
        scratch_shapes=[
            pltpu.VMEM((tm, tm), BF16),
            pltpu.VMEM((SUBLANES, tm), F32),
            pltpu.VMEM((tm + MOE_ROWS, D_MODEL), BF16),
            pltpu.VMEM((tm + MOE_ROWS, D_MODEL), F32),
            pltpu.VMEM((tm + MOE_ROWS, LANES), F32),
            pltpu.SMEM((N_GROUPS,), I32),
            pltpu.SemaphoreType.DMA(()),
        ],
        compiler_params=pltpu.CompilerParams(
            dimension_semantics=("arbitrary",) * 4, vmem_limit_bytes=MOE_VMEM_LIMIT),
        name="moe_routed",
    )(n3, comb, h2d, w_gate_l, w_up_l, w_down_l)


def _moe_kernel(n_ref, c_ref, h_ref, wg_ref, wu_ref, wd_ref, o_ref):
    e = pl.program_id(1)

    @pl.when(e == 0)
    def _():
        o_ref[...] = h_ref[...]

    x = n_ref[...]
    a = jnp.dot(x, wg_ref[...].astype(BF16), preferred_element_type=F32)
    u = jnp.dot(x, wu_ref[...].astype(BF16), preferred_element_type=F32)
    lane = lax.broadcasted_iota(I32, c_ref.shape, 1)
    c = jnp.sum(jnp.where(lane == e, c_ref[...], 0.0), axis=-1, keepdims=True)
    hid = (a * jax.nn.sigmoid(a) * u * c).astype(BF16)
    o_ref[...] += jnp.dot(hid, wd_ref[...].astype(BF16), preferred_element_type=F32)


def _moe(n3, comb, h2d, w_gate_l, w_up_l, w_down_l, *, tm):
    n = h2d.shape[0]
    return pl.pallas_call(
        _moe_kernel,
        grid=(n // tm, N_EXPERTS),
        in_specs=[
            pl.BlockSpec((tm, D_MODEL), lambda i, e: (i, 0)),
            pl.BlockSpec((tm, LANES), lambda i, e: (i, 0)),
            pl.BlockSpec((tm, D_MODEL), lambda i, e: (i, 0)),
            pl.BlockSpec((None, D_MODEL, EXPERT_FF), lambda i, e: (e, 0, 0)),
            pl.BlockSpec((None, D_MODEL, EXPERT_FF), lambda i, e: (e, 0, 0)),
            pl.BlockSpec((None, EXPERT_FF, D_MODEL), lambda i, e: (e, 0, 0)),
        ],
        out_specs=pl.BlockSpec((tm, D_MODEL), lambda i, e: (i, 0)),
        out_shape=jax.ShapeDtypeStruct((n, D_MODEL), F32),
        compiler_params=_cparams(("arbitrary", "arbitrary")),
        name="moe",
    )(n3, comb, h2d, w_gate_l, w_up_l, w_down_l)


def _post_mixer(p, b_out, ws_eff, ab_eff, x2d, mk, mv, lw, *, tm_mix, tm_mem, mem_rows, tiles_per_mem,
                rows_per_mem, mem_len, tm_moe):
    h1 = _mixout(p, b_out, ws_eff, ab_eff, x2d, lw["w_out"], tm=tm_mix)
    h2, n3, comb = _memattn(h1, lw["norm_mem"], lw["mem_wq"], lw["mem_qnorm"], mk, mv, lw["mem_wo"],
                            lw["norm_ffn"], lw["w_cat"], lw["b_cat"], tm=tm_mem, mem_rows=mem_rows,
                            tiles_per_mem=tiles_per_mem, rows_per_mem=rows_per_mem, mem_len=mem_len)
    moe = _moe_routed if tm_moe > N_GROUPS * MOE_ROWS else _moe
    return moe(n3, comb, h2, lw["w_gate"], lw["w_up"], lw["w_down"], tm=tm_moe)


def kernel(x_prompt, x_sample, cache_k, cache_v, cache_idx_k, cache_mem_k, cache_mem_v, page_table,
           mem_prompt, norm_mix, w_in, a_vnorm, a_ws, a_b, b_qnorm, b_knorm, rel_bias, w_out,
           norm_mem, mem_wq, mem_wk, mem_wv, mem_qnorm, mem_knorm, mem_wo, norm_ffn,
           w_group, b_group, w_router, b_router, w_gate, w_up, w_down):
    bp, t, d = x_prompt.shape
    bd, s, _ = x_sample.shape
    depth = w_in.shape[0]
    mem_len = mem_prompt.shape[1]
    assert d == D_MODEL and t % KEY_CHUNK == 0 and s <= 8 and (bd * s) % LANES == 0

    bias_pairs, sbias, sconst = _bias_tables(rel_bias)
    hp = x_prompt.reshape(bp * t, d)
    hs = x_sample.reshape(bd * s, d)
    outs = [[] for _ in range(9)]
    for l in range(depth):
        w_cat = jnp.zeros((d, 2 * LANES), F32)
        w_cat = w_cat.at[:, :N_GROUPS].set(w_group[l]).at[:, LANES:LANES + N_EXPERTS].set(w_router[l])
        b_cat = jnp.zeros((1, 2 * LANES), F32)
        b_cat = b_cat.at[0, :N_GROUPS].set(b_group[l]).at[0, LANES:LANES + N_EXPERTS].set(b_router[l])
        lw = dict(w_out=w_out[l], norm_mem=norm_mem[l], mem_wq=mem_wq[l], mem_qnorm=mem_qnorm[l],
                  mem_wo=mem_wo[l], norm_ffn=norm_ffn[l], w_cat=w_cat, b_cat=b_cat)

        p, k, v, tail = _proj(hp, norm_mix[l], w_in[l], a_vnorm[l], b_qnorm[l], b_knorm[l],
                              tm=1024, want_vn=False)
        later_weights = (w_gate[l], w_up[l], w_down[l], w_out[l][None])
        fuse_cast = (bp * (t // Q_BLOCK)) % N_EXPERTS == 0
        b_out, casted = _dsa_prompt(p, tail, bias_pairs, later_weights if fuse_cast else (), batch=bp, seq=t)
        if not fuse_cast:
            casted = tuple(_cast_bf16(w) for w in later_weights)
        lw.update(w_gate=casted[0], w_up=casted[1], w_down=casted[2], w_out=casted[3][0])
        mk, mv = _memkv(mem_prompt.reshape(bp * mem_len, d), mem_wk[l], mem_wv[l], mem_knorm[l])
        hp = _post_mixer(p, b_out, a_ws[l], a_b[l].T, hp, mk, mv, lw, tm_mix=512, tm_mem=512,
                         mem_rows=mem_len, tiles_per_mem=t // 512, rows_per_mem=None, mem_len=mem_len,
                         tm_moe=1024)
        outs[0].append(k.reshape(bp, t, B_KV_HEADS, B_HEAD_DIM))
        outs[1].append(v.reshape(bp, t, B_KV_HEADS, B_HEAD_DIM))
        outs[2].append(tail[:, :IDX_DIM].reshape(bp, t, IDX_DIM))
        outs[3].append(mk.reshape(bp, mem_len, MEM_HEADS, MEM_HEAD_DIM))
        outs[4].append(mv.reshape(bp, mem_len, MEM_HEADS, MEM_HEAD_DIM))

        ns = bd * s
        p_s, k_s, v_s, tail_s, vn_s = _proj(hs, norm_mix[l], w_in[l], a_vnorm[l], b_qnorm[l], b_knorm[l],
                                            tm=ns, want_vn=True)
        b_out_s = _dsa_sample(p_s, tail_s, k_s.reshape(ns, KV_WIDTH), v_s.reshape(ns, KV_WIDTH),
                              cache_k[l], cache_v[l], cache_idx_k[l], page_table,
                              sbias, sconst, dec_batch=bd, dec_seq=s)
        ws_s = jnp.einsum("ab,gts->gatbs", jnp.eye(ns // s, dtype=F32), a_ws[l][:, :s, :s])
        ws_s = ws_s.reshape(A_GROUPS, ns, ns)
        ab_s = jnp.tile(a_b[l][:, :s].T, (ns // s, 1))
        mem_tile = 32
        hs = _post_mixer(p_s, b_out_s, ws_s, ab_s, hs,
                         cache_mem_k[l].reshape(bd * mem_len * MEM_HEADS, MEM_HEAD_DIM),
                         cache_mem_v[l].reshape(bd * mem_len * MEM_HEADS, MEM_HEAD_DIM), lw,
                         tm_mix=ns, tm_mem=mem_tile, mem_rows=mem_tile // s * mem_len, tiles_per_mem=1,
                         rows_per_mem=s, mem_len=mem_len, tm_moe=ns)
        outs[5].append(k_s.reshape(bd, s, B_KV_HEADS, B_HEAD_DIM))
        outs[6].append(v_s.reshape(bd, s, B_KV_HEADS, B_HEAD_DIM))
        outs[7].append(tail_s[:, :IDX_DIM].reshape(bd, s, IDX_DIM))
        outs[8].append(vn_s.reshape(bd, s, A_GROUPS, A_CH))
    return (hp.reshape(bp, t, d), hs.reshape(bd, s, d)) + tuple(jnp.stack(o) for o in outs)
```

```python
import functools
import math

import jax
import jax.numpy as jnp
from jax import lax
from jax.experimental import pallas as pl
from jax.experimental.pallas import tpu as pltpu

BF16 = jnp.bfloat16
F32 = jnp.float32
I32 = jnp.int32

D_MODEL = 2048
A_GROUPS = 8
A_CH = 128
A_WIDTH = A_GROUPS * A_CH
A_CHUNK = 128
B_HEADS = 8
B_HEAD_DIM = 128
B_KV_HEADS = 2
B_GQA = B_HEADS // B_KV_HEADS
B_WIDTH = B_HEADS * B_HEAD_DIM
KV_WIDTH = B_KV_HEADS * B_HEAD_DIM
IDX_HEADS = 16
IDX_DIM = 64
TOPK_MAX = 256
Q_BLOCK = 128
REL_BUCKETS = 32
REL_MAX_DIST = 128
MEM_HEADS = 4
MEM_HEAD_DIM = 128
MEM_WIDTH = MEM_HEADS * MEM_HEAD_DIM
N_GROUPS = 4
EXPERTS_PER_GROUP = 4
N_EXPERTS = N_GROUPS * EXPERTS_PER_GROUP
EXPERT_FF = 512
PAGE_SIZE = 128
EPS = 1e-6

COL_U = 0
COL_V = COL_U + A_WIDTH
COL_Q = COL_V + A_WIDTH
COL_K = COL_Q + B_WIDTH
COL_VV = COL_K + KV_WIDTH
COL_QI = COL_VV + KV_WIDTH
COL_MAIN = COL_QI + IDX_HEADS * IDX_DIM
TAIL_COLS = IDX_DIM + IDX_HEADS
LANES = 128
SUBLANES = 8
BF16_SUBLANES = 2 * SUBLANES
PROJ_TN = 512
PROJ_ROW_CHUNK = 256
MEMATTN_ROW_CHUNK = 512
KEY_CHUNK = 256
NEG_INIT = -1e30
VMEM_LIMIT = 56 * 1024 * 1024
MOE_VMEM_LIMIT = 60 * 1024 * 1024


def _cparams(sem):
    return pltpu.CompilerParams(dimension_semantics=sem, vmem_limit_bytes=VMEM_LIMIT)


def _gelu(x):
    return 0.5 * x * (1.0 + lax.erf(x * (2.0 ** -0.5)))


def _head_norm(a, gain, width=LANES):
    outs = []
    for c in range(a.shape[1] // width):
        blk = a[:, c * width:(c + 1) * width]
        ms = jnp.mean(blk * blk, axis=-1, keepdims=True)
        outs.append(blk * lax.rsqrt(ms + EPS) * gain[:, c * width:(c + 1) * width])
    return outs[0] if len(outs) == 1 else jnp.concatenate(outs, axis=-1)


def _row_norm(x, gain):
    ms = jnp.mean(x * x, axis=-1, keepdims=True)
    return x * lax.rsqrt(ms + EPS) * gain


def _dot_nt(a, b):
    return lax.dot_general(a, b, (((1,), (1,)), ((), ())), preferred_element_type=F32)


def _proj_kernel(x_ref, g_ref, w_ref, wt_ref, avn_ref, qn_ref, kn_ref, *rest, want_vn):
    if want_vn:
        p_ref, k_ref, v_ref, tail_ref, vn_ref, xn_scr = rest
    else:
        p_ref, k_ref, v_ref, tail_ref, xn_scr = rest
        vn_ref = None
    j = pl.program_id(1)
    ju = COL_V // PROJ_TN
    jv = COL_Q // PROJ_TN
    jq = COL_K // PROJ_TN
    jkv = COL_QI // PROJ_TN

    @pl.when(j == 0)
    def _():
        xn = _row_norm(x_ref[...], g_ref[...]).astype(BF16)
        xn_scr[...] = xn
        wrow = lax.broadcasted_iota(I32, wt_ref.shape, 0)
        wt = jnp.where(wrow < TAIL_COLS, wt_ref[...], 0.0).astype(BF16)
        t = _dot_nt(xn, wt)
        lane = lax.broadcasted_iota(I32, t.shape, 1)
        tail_ref[...] = jnp.where(lane >= IDX_DIM, t * (IDX_HEADS ** -0.5), t)

    tm = xn_scr.shape[0]
    rc = min(tm, PROJ_ROW_CHUNK)

    def row_chunks(epilogue):
        w = w_ref[...].astype(BF16)
        for r in range(tm // rc):
            rs = slice(r * rc, (r + 1) * rc)
            epilogue(_dot_nt(xn_scr[rs, :], w), rs)

    @pl.when(j < ju)
    def _():
        def epi(acc, rs):
            p_ref[rs, :] = _gelu(acc).astype(BF16)
        row_chunks(epi)

    @pl.when((j >= ju) & (j < jv))
    def _():
        def epi(acc, rs):
            vn = _head_norm(_gelu(acc), avn_ref[...])
            p_ref[rs, :] = vn.astype(BF16)
            if vn_ref is not None:
                vn_ref[rs, :] = vn
        row_chunks(epi)

    @pl.when((j >= jv) & (j < jq))
    def _():
        def epi(acc, rs):
            p_ref[rs, :] = _head_norm(acc, qn_ref[...]).astype(BF16)
        row_chunks(epi)

    @pl.when(j == jq)
    def _():
        def epi(acc, rs):
            k = _head_norm(acc[:, :KV_WIDTH], kn_ref[...])
            v = acc[:, KV_WIDTH:]
            for g in range(B_KV_HEADS):
                dst = pl.ds(rs.start * B_KV_HEADS + g, rs.stop - rs.start, stride=B_KV_HEADS)
                k_ref[dst, :] = k[:, g * B_HEAD_DIM:(g + 1) * B_HEAD_DIM]
                v_ref[dst, :] = v[:, g * B_HEAD_DIM:(g + 1) * B_HEAD_DIM]
            p_ref[rs, :] = jnp.concatenate([k, v], axis=-1).astype(BF16)
        row_chunks(epi)

    @pl.when(j >= jkv)
    def _():
        def epi(acc, rs):
            p_ref[rs, :] = acc.astype(BF16)
        row_chunks(epi)


def _proj(x2d, gain, w_in_l, a_vnorm_l, b_qnorm_l, b_knorm_l, *, tm, want_vn):
    n = x2d.shape[0]
    nj = COL_MAIN // PROJ_TN
    w_in_t = jnp.swapaxes(w_in_l, 0, 1)
    avn = a_vnorm_l.reshape(1, A_WIDTH)
    qn = jnp.tile(b_qnorm_l, PROJ_TN // B_HEAD_DIM).reshape(1, PROJ_TN)
    kn = jnp.tile(b_knorm_l, B_KV_HEADS).reshape(1, KV_WIDTH)
    ju = COL_V // PROJ_TN
    nv = A_WIDTH // PROJ_TN

    def vmap_(i, j):
        return (0, jnp.clip(j - ju, 0, nv - 1))

    in_specs = [
        pl.BlockSpec((tm, D_MODEL), lambda i, j: (i, 0)),
        pl.BlockSpec((1, D_MODEL), lambda i, j: (0, 0)),
        pl.BlockSpec((PROJ_TN, D_MODEL), lambda i, j: (j, 0)),
        pl.BlockSpec((LANES, D_MODEL), lambda i, j: (COL_MAIN // LANES, 0)),
        pl.BlockSpec((1, PROJ_TN), vmap_),
        pl.BlockSpec((1, PROJ_TN), lambda i, j: (0, 0)),
        pl.BlockSpec((1, KV_WIDTH), lambda i, j: (0, 0)),
    ]
    out_shape = [
        jax.ShapeDtypeStruct((n, COL_MAIN), BF16),
        jax.ShapeDtypeStruct((n * B_KV_HEADS, B_HEAD_DIM), F32),
        jax.ShapeDtypeStruct((n * B_KV_HEADS, B_HEAD_DIM), F32),
        jax.ShapeDtypeStruct((n, LANES), F32),
    ]
    out_specs = [
        pl.BlockSpec((tm, PROJ_TN), lambda i, j: (i, j)),
        pl.BlockSpec((tm * B_KV_HEADS, B_HEAD_DIM), lambda i, j: (i, 0)),
        pl.BlockSpec((tm * B_KV_HEADS, B_HEAD_DIM), lambda i, j: (i, 0)),
        pl.BlockSpec((tm, LANES), lambda i, j: (i, 0)),
    ]
    if want_vn:
        out_shape.append(jax.ShapeDtypeStruct((n, A_WIDTH), F32))
        out_specs.append(pl.BlockSpec((tm, PROJ_TN), lambda i, j: (i, jnp.clip(j - ju, 0, nv - 1))))
    return pl.pallas_call(
        functools.partial(_proj_kernel, want_vn=want_vn),
        grid=(n // tm, nj),
        in_specs=in_specs,
        out_specs=out_specs,
        out_shape=out_shape,
        scratch_shapes=[pltpu.VMEM((tm, D_MODEL), BF16)],
        compiler_params=_cparams(("arbitrary", "arbitrary")),
        name="proj",
    )(x2d, gain.reshape(1, D_MODEL), w_in_t, w_in_t, avn, qn, kn)


def _t5_bucket(d):
    max_exact = REL_BUCKETS // 2
    d = jnp.maximum(d, 0)
    ratio = jnp.log(jnp.maximum(d, 1).astype(F32) / max_exact) / math.log(REL_MAX_DIST / max_exact)
    large = jnp.minimum(max_exact + jnp.floor(ratio * (REL_BUCKETS - max_exact)).astype(I32), REL_BUCKETS - 1)
    return jnp.where(d < max_exact, d, large)


def _bias_lookup(bucket, rb_ref, h):
    acc = jnp.zeros(bucket.shape, F32)
    for r in range(REL_BUCKETS):
        acc = jnp.where(bucket == r, rb_ref[r, h], acc)
    return acc


def _bias_kernel(rb_ref, pair_ref, samp_ref, sconst_ref):
    q = lax.broadcasted_iota(I32, (Q_BLOCK, Q_BLOCK), 0)
    kc = lax.broadcasted_iota(I32, (Q_BLOCK, Q_BLOCK), 1)
    nvar = pair_ref.shape[0]
    rels = range(-(nvar - 1), 2)
    buckets = {r: _t5_bucket(q - kc - r * Q_BLOCK) for r in rels}
    for h in range(B_HEADS):
        g, hh = divmod(h, B_GQA)
        tiles = {r: _bias_lookup(buckets[r], rb_ref, h) for r in rels}
        for var in range(nvar):
            r0 = var - (nvar - 1)
            for t in range(2):
                pair_ref[var, g, hh * Q_BLOCK:(hh + 1) * Q_BLOCK, t * Q_BLOCK:(t + 1) * Q_BLOCK] = tiles[r0 + t]
    rows = B_GQA * SUBLANES
    row = lax.broadcasted_iota(I32, (rows, 2 * PAGE_SIZE), 0)
    col = lax.broadcasted_iota(I32, (rows, 2 * PAGE_SIZE), 1)
    t = row % SUBLANES
    d = jnp.where(col < PAGE_SIZE, t + PAGE_SIZE - col, t - (col - PAGE_SIZE))
    bucket = _t5_bucket(d)
    far = jnp.full((rows, LANES), REL_BUCKETS - 1, I32)
    rowc = lax.broadcasted_iota(I32, (rows, LANES), 0)
    for g in range(B_KV_HEADS):
        acc = jnp.zeros((rows, 2 * PAGE_SIZE), F32)
        accc = jnp.zeros((rows, LANES), F32)
        for hh in range(B_GQA):
            h = g * B_GQA + hh
            acc = jnp.where(row // SUBLANES == hh, _bias_lookup(bucket, rb_ref, h), acc)
            accc = jnp.where(rowc // SUBLANES == hh, _bias_lookup(far, rb_ref, h), accc)
        samp_ref[g] = acc
        sconst_ref[g] = accc


N_PAIR_VARIANTS = 4


def _bias_tables(rel_bias):
    rows = B_GQA * SUBLANES
    return pl.pallas_call(
        _bias_kernel,
        in_specs=[pl.BlockSpec(memory_space=pltpu.SMEM)],
        out_shape=[
            jax.ShapeDtypeStruct((N_PAIR_VARIANTS, B_KV_HEADS, B_GQA * Q_BLOCK, KEY_CHUNK), F32),
            jax.ShapeDtypeStruct((B_KV_HEADS, rows, 2 * PAGE_SIZE), F32),
            jax.ShapeDtypeStruct((B_KV_HEADS, rows, LANES), F32),
        ],
        name="bias_tables",
    )(rel_bias)


def _key_to_f32(key):
    bits = jnp.where(key < 0, key ^ I32(-2 ** 31), ~key)
    return lax.bitcast_convert_type(bits, F32)


def _kth_largest(count_ge, shape, k):
    def body(it, prefix):
        bit = 31 - it
        cand = prefix | lax.shift_left(I32(1), bit)
        cnt = count_ge(_key_to_f32(cand))
        return jnp.where(cnt >= k, cand, prefix)

    prefix = lax.fori_loop(0, 32, body, jnp.zeros(shape, I32))
    thr = _key_to_f32(prefix)
    return jnp.where((prefix & I32(-2 ** 23)) == 0, -jnp.inf, thr)


NO_INDEX_BOUND = 2 ** 30
TIE_REPAIR_ROUNDS = 2


def _topk_select(fold_sum, fold_min, shape, k, index_bits):
    thr = _kth_largest(lambda t: fold_sum(lambda sc, kp: sc >= t), shape, k)
    c_ge = fold_sum(lambda sc, kp: sc >= thr)
    no_bound = jnp.full(shape, NO_INDEX_BOUND, I32)

    def repair(_):
        v = thr
        strict = jnp.zeros(shape, F32)

        def in_set(sc, v, strict):
            return (sc > v) | ((sc == v) & (strict < 0.5))

        for _ in range(TIE_REPAIR_ROUNDS):
            c_set = fold_sum(lambda sc, kp: in_set(sc, v, strict))
            vmin = fold_min(lambda sc, kp: jnp.where(in_set(sc, v, strict), sc, jnp.inf))
            c_gt = fold_sum(lambda sc, kp: sc > vmin)
            drop = (c_set > k) & (c_gt >= k)
            v = jnp.where(drop, vmin, v)
            strict = jnp.where(drop, 1.0, strict)
        c_set = fold_sum(lambda sc, kp: in_set(sc, v, strict))
        vt = fold_min(lambda sc, kp: jnp.where(in_set(sc, v, strict), sc, jnp.inf))
        need = k - fold_sum(lambda sc, kp: sc > vt)

        def idx_body(it, m):
            cand = m | lax.shift_left(I32(1), index_bits - 1 - it)
            below = fold_sum(lambda sc, kp: (sc == vt) & (kp < cand))
            return jnp.where(below < need, cand, m)

        m = lax.fori_loop(0, index_bits, idx_body, jnp.zeros(shape, I32))
        tie = c_set > k
        v_out = jnp.where(tie, vt, v)
        m_out = jnp.where(tie, m, jnp.where(strict > 0.5, -1, no_bound))
        return v_out, m_out

    return lax.cond(jnp.max(c_ge) > k, repair, lambda _: (thr, no_bound), 0)


CHUNKS_PER_TRIP = 4


def _chunk_loop(n, body, init):
    def trip(j, c):
        for k in range(CHUNKS_PER_TRIP):
            c = body(CHUNKS_PER_TRIP * j + k, c)
        return c

    whole = n // CHUNKS_PER_TRIP
    carry = lax.fori_loop(0, whole, trip, init)
    return lax.fori_loop(whole * CHUNKS_PER_TRIP, n, body, carry)


def _dsa_prompt_kernel(q_ref, kv_ref, qia_ref, qib_ref, tailk_ref, tailq_ref, bias_ref, *rest, topk, n_cast):
    cast_in, rest = rest[:n_cast], rest[n_cast:]
    o_ref, rest = rest[0], rest[1:]
    cast_out, rest = rest[:n_cast], rest[n_cast:]
    kil_scr, kir_scr, sc_scr, lg_scr, m_scr, l_scr, acc_scr = rest
    i = pl.program_id(1)

    for src, dst in zip(cast_in, cast_out):
        dst[...] = src[...].astype(BF16)

    @pl.when(i == 0)
    def _():
        ki = tailk_ref[:, :IDX_DIM].astype(BF16)
        z = jnp.zeros_like(ki)
        kil_scr[...] = jnp.concatenate([ki, z], axis=-1)
        kir_scr[...] = jnp.concatenate([z, ki], axis=-1)

    wi = tailq_ref[:, IDX_DIM:IDX_DIM + IDX_HEADS] * (IDX_DIM ** -0.5)
    nk = (i + 2) // 2
    qpos = i * Q_BLOCK + lax.broadcasted_iota(I32, (Q_BLOCK, 1), 0)
    kloc = lax.broadcasted_iota(I32, (1, KEY_CHUNK), 1)
    heads_per_ref = qia_ref.shape[1] // IDX_DIM

    pairs = []
    for hp in range(IDX_HEADS // 2):
        ref = qia_ref if 2 * hp < heads_per_ref else qib_ref
        base = (2 * hp) % heads_per_ref * IDX_DIM
        pairs.append(ref[:, base:base + 2 * IDX_DIM])
    qstack = jnp.concatenate(pairs, axis=0)

    def score_chunk(c, carry):
        off = pl.multiple_of(c * KEY_CHUNK, KEY_CHUNK)
        acc = jnp.zeros((Q_BLOCK, KEY_CHUNK), F32)
        for side, k_scr in ((0, kil_scr), (1, kir_scr)):
            s = _dot_nt(qstack, k_scr[pl.ds(off, KEY_CHUNK), :])
            for hp in range(IDX_HEADS // 2):
                h = 2 * hp + side
                acc = acc + wi[:, h:h + 1] * jnp.maximum(s[hp * Q_BLOCK:(hp + 1) * Q_BLOCK], 0.0)
        kpos = c * KEY_CHUNK + kloc
        sc_scr[c] = jnp.where(kpos <= qpos, acc, -jnp.inf)
        return carry

    _chunk_loop(nk, score_chunk, 0)

    def fold_sum(pred):
        def body(c, cnt):
            m = jnp.where(pred(sc_scr[c], c * KEY_CHUNK + kloc), 1.0, 0.0)
            return cnt + m[:, :LANES] + m[:, LANES:]
        cnt = _chunk_loop(nk, body, jnp.zeros((Q_BLOCK, LANES), F32))
        return jnp.sum(cnt, axis=-1, keepdims=True)

    def fold_min(val):
        def body(c, acc):
            x = val(sc_scr[c], c * KEY_CHUNK + kloc)
            return jnp.minimum(acc, jnp.minimum(x[:, :LANES], x[:, LANES:]))
        acc = lax.fori_loop(0, nk, body, jnp.full((Q_BLOCK, LANES), jnp.inf, F32))
        return jnp.min(acc, axis=-1, keepdims=True)

    index_bits = (kv_ref.shape[0] - 1).bit_length()
    sel_v, sel_m = _topk_select(fold_sum, fold_min, (Q_BLOCK, 1), topk, index_bits)

    def mask_chunk(c, carry):
        kpos = c * KEY_CHUNK + kloc
        sc = sc_scr[c]
        chosen = (sc > sel_v) | ((sc == sel_v) & (kpos <= sel_m))
        sc_scr[c] = jnp.where(chosen & (kpos <= qpos), 0.0, -jnp.inf)
        return carry

    lax.fori_loop(0, nk, mask_chunk, 0)

    scale = B_HEAD_DIM ** -0.5
    rows = B_GQA * Q_BLOCK
    qgs = [jnp.concatenate(
        [q_ref[:, (g * B_GQA + hh) * B_HEAD_DIM:(g * B_GQA + hh + 1) * B_HEAD_DIM] for hh in range(B_GQA)],
        axis=0) for g in range(B_KV_HEADS)]
    m_scr[...] = jnp.full(m_scr.shape, -jnp.inf, F32)
    l_scr[...] = jnp.zeros(l_scr.shape, F32)
    acc_scr[...] = jnp.zeros(acc_scr.shape, F32)

    def logits_chunk(c, carry):
        off = pl.multiple_of(c * KEY_CHUNK, KEY_CHUNK)
        var = jnp.clip(2 * c - i, -(N_PAIR_VARIANTS - 1), 0) + (N_PAIR_VARIANTS - 1)
        sel = sc_scr[c]
        sel = jnp.concatenate([sel] * B_GQA, axis=0)
        for g in range(B_KV_HEADS):
            kc = kv_ref[pl.ds(off, KEY_CHUNK), g * B_HEAD_DIM:(g + 1) * B_HEAD_DIM]
            lg = _dot_nt(qgs[g], kc) * scale + bias_ref[var, g] + sel
            lg_scr[c, g] = lg
            m_scr[g] = jnp.maximum(m_scr[g], jnp.maximum(lg[:, :LANES], lg[:, LANES:]))
        return carry

    _chunk_loop(nk, logits_chunk, 0)
    for g in range(B_KV_HEADS):
        m = jnp.max(m_scr[g], axis=-1, keepdims=True)
        m_scr[g] = jnp.broadcast_to(m, (rows, LANES))

    def pv_chunk(c, carry):
        off = pl.multiple_of(c * KEY_CHUNK, KEY_CHUNK)
        for g in range(B_KV_HEADS):
            vc = kv_ref[pl.ds(off, KEY_CHUNK), KV_WIDTH + g * B_HEAD_DIM:KV_WIDTH + (g + 1) * B_HEAD_DIM]
            mb = m_scr[g]
            p = jnp.exp(lg_scr[c, g] - jnp.concatenate([mb, mb], axis=-1))
            l_scr[g] += p[:, :LANES] + p[:, LANES:]
            acc_scr[g] += jnp.dot(p.astype(BF16), vc, preferred_element_type=F32)
        return carry

    _chunk_loop(nk, pv_chunk, 0)
    for g in range(B_KV_HEADS):
        o = acc_scr[g] / jnp.sum(l_scr[g], axis=-1, keepdims=True)
        for hh in range(B_GQA):
            h = g * B_GQA + hh
            o_ref[:, h * B_HEAD_DIM:(h + 1) * B_HEAD_DIM] = o[hh * Q_BLOCK:(hh + 1) * Q_BLOCK].astype(BF16)


def _dsa_prompt(p, tail, bias_pairs, cast_weights, *, batch, seq):
    topk = min(TOPK_MAX, seq // 4)
    nblk = seq // Q_BLOCK
    qi_w = PROJ_TN
    steps = batch * nblk
    cast_specs, cast_shapes = [], []
    for w in cast_weights:
        ne, r, c = w.shape
        parts = steps // ne
        assert steps == ne * parts and r % (parts * BF16_SUBLANES) == 0
        spec = pl.BlockSpec((None, r // parts, c),
                            lambda b, i, parts=parts: ((b * nblk + i) // parts, (b * nblk + i) % parts, 0))
        cast_specs.append(spec)
        cast_shapes.append(jax.ShapeDtypeStruct(w.shape, BF16))
    outs = pl.pallas_call(
        functools.partial(_dsa_prompt_kernel, topk=topk, n_cast=len(cast_weights)),
        grid=(batch, nblk),
        in_specs=[
            pl.BlockSpec((Q_BLOCK, B_WIDTH), lambda b, i: (b * nblk + i, COL_Q // B_WIDTH)),
            pl.BlockSpec((seq, 2 * KV_WIDTH), lambda b, i: (b, COL_K // (2 * KV_WIDTH))),
            pl.BlockSpec((Q_BLOCK, qi_w), lambda b, i: (b * nblk + i, COL_QI // qi_w)),
            pl.BlockSpec((Q_BLOCK, qi_w), lambda b, i: (b * nblk + i, COL_QI // qi_w + 1)),
            pl.BlockSpec((seq, LANES), lambda b, i: (b, 0)),
            pl.BlockSpec((Q_BLOCK, LANES), lambda b, i: (b * nblk + i, 0)),
            pl.BlockSpec(bias_pairs.shape, lambda b, i: (0, 0, 0, 0)),
        ] + cast_specs,
        out_specs=[pl.BlockSpec((Q_BLOCK, B_WIDTH), lambda b, i: (b * nblk + i, 0))] + cast_specs,
        out_shape=[jax.ShapeDtypeStruct((batch * seq, B_WIDTH), BF16)] + cast_shapes,
        scratch_shapes=[
            pltpu.VMEM((seq, 2 * IDX_DIM), BF16),
            pltpu.VMEM((seq, 2 * IDX_DIM), BF16),
            pltpu.VMEM((seq // KEY_CHUNK, Q_BLOCK, KEY_CHUNK), F32),
            pltpu.VMEM((seq // KEY_CHUNK, B_KV_HEADS, B_GQA * Q_BLOCK, KEY_CHUNK), F32),
            pltpu.VMEM((B_KV_HEADS, B_GQA * Q_BLOCK, LANES), F32),
            pltpu.VMEM((B_KV_HEADS, B_GQA * Q_BLOCK, LANES), F32),
            pltpu.VMEM((B_KV_HEADS, B_GQA * Q_BLOCK, B_HEAD_DIM), F32),
        ],
        compiler_params=_cparams(("arbitrary", "arbitrary")),
        name="dsa_prompt",
    )(p, p, p, p, tail, tail, bias_pairs, *cast_weights)
    return outs[0], tuple(outs[1:])


PAGES_PER_STEP = 64
SAMPLE_SUB_PAGES = 32
SCORE_KEYS_PER_DOT = 2048
FOLD_VREGS = 16
SELECT_ROWS = 64


def _fetch_pages(pt_ref, sources, bufs, sem, step, n_steps_total, steps_per_sample, pages_per_step):
    slot = step % 2

    def copies(n, slot_, page_of):
        sample = n // steps_per_sample
        first = (n % steps_per_sample) * pages_per_step
        for r in range(pages_per_step):
            page = page_of(sample, first + r)
            for a, (src, buf) in enumerate(zip(sources, bufs)):
                yield pltpu.make_async_copy(src.at[page], buf.at[slot_, r], sem.at[a, slot_])

    def table(sample, idx):
        return pt_ref[sample, idx]

    @pl.when(step == 0)
    def _():
        for cp in copies(step, slot, table):
            cp.start()

    @pl.when(step + 1 < n_steps_total)
    def _():
        for cp in copies(step + 1, 1 - slot, table):
            cp.start()

    for cp in copies(step, slot, lambda sample, idx: 0):
        cp.wait()
    return slot


def _dsa_sample_score_kernel(pt_ref, idx_hbm, qi_ref, wi_ref, kin_ref, sc_ref, scn_ref,
                             page_buf, sem, *, dec_seq, n_pages, n_samples):
    slot = _fetch_pages(pt_ref, [idx_hbm], [page_buf], sem, pl.program_id(0), n_samples, 1, n_pages)
    pages = [page_buf.at[slot, r] for r in range(n_pages)]
    pages_per_dot = SCORE_KEYS_PER_DOT // PAGE_SIZE
    qi = qi_ref[...]
    wi = wi_ref[...] * (IDX_DIM ** -0.5)

    def scores(keys_t):
        r = jnp.maximum(jnp.dot(qi, keys_t, preferred_element_type=F32), 0.0) * wi
        acc = r[0:SUBLANES]
        for j in range(1, IDX_HEADS * dec_seq // SUBLANES):
            acc = acc + r[j * SUBLANES:(j + 1) * SUBLANES]
        shift = dec_seq
        while shift < SUBLANES:
            acc = acc + pltpu.roll(acc, shift, axis=0)
            shift *= 2
        return acc

    for d in range(n_pages // pages_per_dot):
        keys_t = jnp.concatenate(
            [pg[...].astype(BF16) for pg in pages[d * pages_per_dot:(d + 1) * pages_per_dot]], axis=1)
        sc_ref[:, d * SCORE_KEYS_PER_DOT:(d + 1) * SCORE_KEYS_PER_DOT] = scores(keys_t)

    row = lax.broadcasted_iota(I32, (SUBLANES, LANES), 0)
    col = lax.broadcasted_iota(I32, (SUBLANES, LANES), 1)
    scn_ref[...] = jnp.where((col <= row % dec_seq) & (col < dec_seq), scores(kin_ref[...]), -jnp.inf)


def _sample_select_kernel(sc_ref, scn_ref, selv_ref, selm_ref, *, topk):
    rows, past = sc_ref.shape
    col = lax.broadcasted_iota(I32, (1, LANES), 1)

    def pieces():
        yield scn_ref[...], past + col
        for w in range(past // LANES):
            yield sc_ref[:, w * LANES:(w + 1) * LANES], w * LANES + col

    ways = max(1, FOLD_VREGS // (rows // SUBLANES))

    def fold(term, combine):
        parts = [None] * ways
        for n, (sc, kp) in enumerate(pieces()):
            x = term(sc, kp)
            parts[n % ways] = x if parts[n % ways] is None else combine(parts[n % ways], x)
        parts = [p for p in parts if p is not None]
        while len(parts) > 1:
            parts = [combine(parts[j], parts[j + 1]) if j + 1 < len(parts) else parts[j]
                     for j in range(0, len(parts), 2)]
        return parts[0]

    def fold_sum(pred):
        cnt = fold(lambda sc, kp: jnp.where(pred(sc, kp), 1.0, 0.0), jnp.add)
        return jnp.sum(cnt, axis=-1, keepdims=True)

    def fold_min(val):
        return jnp.min(fold(val, jnp.minimum), axis=-1, keepdims=True)

    index_bits = (past + LANES - 1).bit_length()
    sel_v, sel_m = _topk_select(fold_sum, fold_min, (rows, 1), topk, index_bits)
    selv_ref[...] = jnp.broadcast_to(sel_v, (rows, LANES))
    selm_ref[...] = jnp.broadcast_to(sel_m, (rows, LANES))


def _dsa_sample_attend_kernel(pt_ref, ck_hbm, cv_hbm, q_ref, sc_ref, scn_ref, selv_ref, selm_ref, kvn_ref,
                              sbias_ref, sconst_ref, o_ref, kbuf, vbuf, sem, m_scr, l_scr, acc_scr,
                              *, dec_seq, n_steps, n_samples):
    s = pl.program_id(1)
    slot = _fetch_pages(pt_ref, [ck_hbm, cv_hbm], [kbuf, vbuf], sem, pl.program_id(0) * n_steps + s,
                        n_samples * n_steps, n_steps, PAGES_PER_STEP)
    kpages = [kbuf.at[slot, r] for r in range(PAGES_PER_STEP)]
    vpages = [vbuf.at[slot, r] for r in range(PAGES_PER_STEP)]
    chunk = PAGES_PER_STEP * PAGE_SIZE
    rows = B_GQA * SUBLANES
    scale = B_HEAD_DIM ** -0.5

    @pl.when(s == 0)
    def _():
        m_scr[...] = jnp.full(m_scr.shape, NEG_INIT, F32)
        l_scr[...] = jnp.zeros(l_scr.shape, F32)
        acc_scr[...] = jnp.zeros(acc_scr.shape, F32)

    sel_v = selv_ref[:, 0:1]
    sel_m = selm_ref[:, 0:1]

    def chosen(sc, kpos):
        return jnp.where((sc > sel_v) | ((sc == sel_v) & (kpos <= sel_m)), 0.0, -jnp.inf)

    def head_rows(pgs, g):
        return jnp.concatenate(
            [pg[pl.ds(g, PAGE_SIZE, stride=B_KV_HEADS), :].astype(BF16) for pg in pgs], axis=0)

    kcol = lax.broadcasted_iota(I32, (SUBLANES, chunk), 1)
    sel = chosen(sc_ref[...], s * chunk + kcol)
    sel = jnp.concatenate([sel] * B_GQA, axis=0)
    is_last = s == n_steps - 1
    ncol = lax.broadcasted_iota(I32, (SUBLANES, LANES), 1)
    seln = chosen(scn_ref[...], n_steps * chunk + ncol)
    seln = jnp.where(is_last & (scn_ref[...] > -jnp.inf), seln, -jnp.inf)
    seln = jnp.concatenate([seln] * B_GQA, axis=0)

    def partial(lg, v):
        m = jnp.maximum(jnp.max(lg, axis=-1, keepdims=True), NEG_INIT)
        p = jnp.exp(lg - m)
        return m, jnp.sum(p, axis=-1, keepdims=True), jnp.dot(p.astype(BF16), v, preferred_element_type=F32)

    sub = SAMPLE_SUB_PAGES * PAGE_SIZE
    nsub = PAGES_PER_STEP // SAMPLE_SUB_PAGES
    lane_sub = lax.broadcasted_iota(I32, (rows, sub), 1)
    for g in range(B_KV_HEADS):
        qg = q_ref[g]
        far = sconst_ref[g, :, 0:1]
        near = jnp.concatenate([jnp.zeros((rows, sub - PAGE_SIZE), F32), sbias_ref[g, :, :PAGE_SIZE]], axis=-1)
        parts = [(m_scr[g], l_scr[g], acc_scr[g])]
        for j in range(nsub):
            pj = slice(j * SAMPLE_SUB_PAGES, (j + 1) * SAMPLE_SUB_PAGES)
            lg = _dot_nt(qg, head_rows(kpages[pj], g)) * scale + sel[:, j * sub:(j + 1) * sub]
            if j == nsub - 1:
                lg = lg + jnp.where(is_last & (lane_sub >= sub - PAGE_SIZE), near, far)
            else:
                lg = lg + far
            parts.append(partial(lg, head_rows(vpages[pj], g)))
        kn = kvn_ref[:, g * B_HEAD_DIM:(g + 1) * B_HEAD_DIM]
        vn = kvn_ref[:, KV_WIDTH + g * B_HEAD_DIM:KV_WIDTH + (g + 1) * B_HEAD_DIM]
        parts.append(partial(_dot_nt(qg, kn) * scale + sbias_ref[g, :, PAGE_SIZE:] + seln, vn))
        m_new = parts[0][0]
        for m, _, _ in parts[1:]:
            m_new = jnp.maximum(m_new, m)
        l_new = jnp.zeros((rows, 1), F32)
        acc_new = jnp.zeros((rows, B_HEAD_DIM), F32)
        for m, l, acc in parts:
            w = jnp.exp(m - m_new)
            l_new = l_new + w * l
            acc_new = acc_new + w * acc
        m_scr[g] = m_new
        l_scr[g] = l_new
        acc_scr[g] = acc_new

    @pl.when(is_last)
    def _():
        for g in range(B_KV_HEADS):
            o_ref[g] = (acc_scr[g] / l_scr[g]).astype(BF16)


def _dsa_sample(p_s, tail_s, k_s, v_s, cache_k_l, cache_v_l, cache_idx_l, page_table, sbias, sconst,
                *, dec_batch, dec_seq):
    n_pages = page_table.shape[1]
    past = n_pages * PAGE_SIZE
    topk = min(TOPK_MAX, (past + dec_seq) // 4)
    n_steps = n_pages // PAGES_PER_STEP
    chunk = PAGES_PER_STEP * PAGE_SIZE
    rows = B_GQA * SUBLANES
    n_pool = cache_k_l.shape[0]

    assert SUBLANES % dec_seq == 0
    qi = p_s[:, COL_QI:COL_MAIN].reshape(dec_batch, dec_seq, IDX_HEADS, IDX_DIM).transpose(0, 2, 1, 3)
    qi = qi.reshape(dec_batch, IDX_HEADS * dec_seq, IDX_DIM)
    wi = tail_s[:, IDX_DIM:IDX_DIM + IDX_HEADS].reshape(dec_batch, dec_seq, IDX_HEADS).transpose(0, 2, 1)
    wi = wi.reshape(dec_batch, IDX_HEADS * dec_seq, 1)
    ki_new = tail_s[:, :IDX_DIM].astype(BF16).reshape(dec_batch, dec_seq, IDX_DIM)
    ki_new = jnp.pad(ki_new, ((0, 0), (0, LANES - dec_seq), (0, 0))).transpose(0, 2, 1)
    q = p_s[:, COL_Q:COL_K].reshape(dec_batch, dec_seq, B_KV_HEADS, B_GQA, B_HEAD_DIM)
    q = jnp.pad(q.transpose(0, 2, 3, 1, 4), ((0, 0), (0, 0), (0, 0), (0, SUBLANES - dec_seq), (0, 0)))
    q = q.reshape(dec_batch, B_KV_HEADS, rows, B_HEAD_DIM)
    kv_new = jnp.concatenate([k_s, v_s], axis=-1).astype(BF16).reshape(dec_batch, dec_seq, 2 * KV_WIDTH)
    kv_new = jnp.pad(kv_new, ((0, 0), (0, LANES - dec_seq), (0, 0)))

    cache_idx_t = jnp.swapaxes(cache_idx_l, 1, 2)
    stat_spec = pl.BlockSpec((None, SUBLANES, LANES), lambda b, pt: (b, 0, 0))
    sc, scn = pl.pallas_call(
        functools.partial(_dsa_sample_score_kernel, dec_seq=dec_seq, n_pages=n_pages, n_samples=dec_batch),
        grid_spec=pltpu.PrefetchScalarGridSpec(
            num_scalar_prefetch=1,
            grid=(dec_batch,),
            in_specs=[
                pl.BlockSpec(memory_space=pl.ANY),
                pl.BlockSpec((None, IDX_HEADS * dec_seq, IDX_DIM), lambda b, pt: (b, 0, 0)),
                pl.BlockSpec((None, IDX_HEADS * dec_seq, 1), lambda b, pt: (b, 0, 0)),
                pl.BlockSpec((None, IDX_DIM, LANES), lambda b, pt: (b, 0, 0)),
            ],
            out_specs=[pl.BlockSpec((None, SUBLANES, past), lambda b, pt: (b, 0, 0)), stat_spec],
            scratch_shapes=[
                pltpu.VMEM((2, n_pages, IDX_DIM, PAGE_SIZE), F32),
                pltpu.SemaphoreType.DMA((1, 2)),
            ],
        ),
        out_shape=[
            jax.ShapeDtypeStruct((dec_batch, SUBLANES, past), F32),
            jax.ShapeDtypeStruct((dec_batch, SUBLANES, LANES), F32),
        ],
        compiler_params=_cparams(("arbitrary",)),
        name="dsa_sample_score",
    )(page_table, cache_idx_t, qi, wi, ki_new)

    n_rows = dec_batch * SUBLANES
    sel_rows = math.gcd(n_rows, SELECT_ROWS)
    row_spec = pl.BlockSpec((sel_rows, LANES), lambda i: (i, 0))
    selv, selm = pl.pallas_call(
        functools.partial(_sample_select_kernel, topk=topk),
        grid=(n_rows // sel_rows,),
        in_specs=[pl.BlockSpec((sel_rows, past), lambda i: (i, 0)), row_spec],
        out_specs=[row_spec, row_spec],
        out_shape=[jax.ShapeDtypeStruct((n_rows, LANES), F32), jax.ShapeDtypeStruct((n_rows, LANES), I32)],
        compiler_params=_cparams(("arbitrary",)),
        name="sample_select",
    )(sc.reshape(n_rows, past), scn.reshape(n_rows, LANES))
    selv = selv.reshape(dec_batch, SUBLANES, LANES)
    selm = selm.reshape(dec_batch, SUBLANES, LANES)

    ck = cache_k_l.reshape(n_pool, PAGE_SIZE * B_KV_HEADS, B_HEAD_DIM)
    cv = cache_v_l.reshape(n_pool, PAGE_SIZE * B_KV_HEADS, B_HEAD_DIM)
    kv_buf = pltpu.VMEM((2, PAGES_PER_STEP, PAGE_SIZE * B_KV_HEADS, B_HEAD_DIM), F32)
    o = pl.pallas_call(
        functools.partial(_dsa_sample_attend_kernel, dec_seq=dec_seq, n_steps=n_steps, n_samples=dec_batch),
        grid_spec=pltpu.PrefetchScalarGridSpec(
            num_scalar_prefetch=1,
            grid=(dec_batch, n_steps),
            in_specs=[pl.BlockSpec(memory_space=pl.ANY), pl.BlockSpec(memory_space=pl.ANY)] + [
                pl.BlockSpec((None, B_KV_HEADS, rows, B_HEAD_DIM), lambda b, s, pt: (b, 0, 0, 0)),
                pl.BlockSpec((None, SUBLANES, chunk), lambda b, s, pt: (b, 0, s)),
                pl.BlockSpec((None, SUBLANES, LANES), lambda b, s, pt: (b, 0, 0)),
                pl.BlockSpec((None, SUBLANES, LANES), lambda b, s, pt: (b, 0, 0)),
                pl.BlockSpec((None, SUBLANES, LANES), lambda b, s, pt: (b, 0, 0)),
                pl.BlockSpec((None, LANES, 2 * KV_WIDTH), lambda b, s, pt: (b, 0, 0)),
                pl.BlockSpec(sbias.shape, lambda b, s, pt: (0, 0, 0)),
                pl.BlockSpec(sconst.shape, lambda b, s, pt: (0, 0, 0)),
            ],
            out_specs=pl.BlockSpec((None, B_KV_HEADS, rows, B_HEAD_DIM), lambda b, s, pt: (b, 0, 0, 0)),
            scratch_shapes=[
                kv_buf,
                kv_buf,
                pltpu.SemaphoreType.DMA((2, 2)),
                pltpu.VMEM((B_KV_HEADS, rows, 1), F32),
                pltpu.VMEM((B_KV_HEADS, rows, 1), F32),
                pltpu.VMEM((B_KV_HEADS, rows, B_HEAD_DIM), F32),
            ],
        ),
        out_shape=jax.ShapeDtypeStruct((dec_batch, B_KV_HEADS, rows, B_HEAD_DIM), BF16),
        compiler_params=_cparams(("arbitrary", "arbitrary")),
        name="dsa_sample_attend",
    )(page_table, ck, cv, q, sc, scn, selv, selm, kv_new, sbias, sconst)
    o = o.reshape(dec_batch, B_KV_HEADS, B_GQA, SUBLANES, B_HEAD_DIM)[:, :, :, :dec_seq].transpose(0, 3, 1, 2, 4)
    return o.reshape(dec_batch * dec_seq, B_WIDTH)


def _mixout_kernel(u_ref, vn_ref, b_ref, ws_ref, ab_ref, x_ref, w_ref, o_ref, mix_scr, *, tm):
    j = pl.program_id(1)

    @pl.when(j == 0)
    def _():
        row = lax.broadcasted_iota(I32, (A_CHUNK, A_CHUNK), 0)
        col = lax.broadcasted_iota(I32, (A_CHUNK, A_CHUNK), 1)
        for g in range(A_GROUPS):
            wsg = jnp.where(row >= col, ws_ref[g], 0.0).astype(BF16)
            bg = ab_ref[:, g:g + 1]
            cs = slice(g * A_CH, (g + 1) * A_CH)
            for r in range(tm // A_CHUNK):
                rs = slice(r * A_CHUNK, (r + 1) * A_CHUNK)
                sm = jnp.dot(wsg, vn_ref[rs, cs], preferred_element_type=F32) + bg
                mix_scr[rs, cs] = (u_ref[rs, cs].astype(F32) * sm).astype(BF16)
        mix_scr[:, A_WIDTH:] = b_ref[...]

    o_ref[...] = x_ref[...] + jnp.dot(mix_scr[...], w_ref[...].astype(BF16), preferred_element_type=F32)


def _mixout(p, b_out, ws_eff, ab_eff, x2d, w_out_l, *, tm, tn=D_MODEL):
    n = x2d.shape[0]
    return pl.pallas_call(
        functools.partial(_mixout_kernel, tm=tm),
        grid=(n // tm, D_MODEL // tn),
        in_specs=[
            pl.BlockSpec((tm, A_WIDTH), lambda i, j: (i, COL_U // A_WIDTH)),
            pl.BlockSpec((tm, A_WIDTH), lambda i, j: (i, COL_V // A_WIDTH)),
            pl.BlockSpec((tm, B_WIDTH), lambda i, j: (i, 0)),
            pl.BlockSpec((A_GROUPS, A_CHUNK, A_CHUNK), lambda i, j: (0, 0, 0)),
            pl.BlockSpec((A_CHUNK, A_GROUPS), lambda i, j: (0, 0)),
            pl.BlockSpec((tm, tn), lambda i, j: (i, j)),
            pl.BlockSpec((A_WIDTH + B_WIDTH, tn), lambda i, j: (0, j)),
        ],
        out_specs=pl.BlockSpec((tm, tn), lambda i, j: (i, j)),
        out_shape=jax.ShapeDtypeStruct((n, D_MODEL), F32),
        scratch_shapes=[pltpu.VMEM((tm, A_WIDTH + B_WIDTH), BF16)],
        compiler_params=_cparams(("arbitrary", "arbitrary")),
        name="mixout",
    )(p, p, b_out, ws_eff, ab_eff, x2d, w_out_l)


def _memkv_kernel(x_ref, wk_ref, wv_ref, kn_ref, mk_ref, mv_ref):
    x = x_ref[...].astype(BF16)
    tm = x.shape[0]
    mk = _head_norm(jnp.dot(x, wk_ref[...].astype(BF16), preferred_element_type=F32), kn_ref[...])
    mv = jnp.dot(x, wv_ref[...].astype(BF16), preferred_element_type=F32)
    for hd in range(MEM_HEADS):
        cs = slice(hd * MEM_HEAD_DIM, (hd + 1) * MEM_HEAD_DIM)
        mk_ref[pl.ds(hd, tm, stride=MEM_HEADS), :] = mk[:, cs]
        mv_ref[pl.ds(hd, tm, stride=MEM_HEADS), :] = mv[:, cs]


def _memkv(mem2d, wk, wv, knorm, *, tm=256):
    n = mem2d.shape[0]
    kn = jnp.tile(knorm, MEM_HEADS).reshape(1, MEM_WIDTH)
    return pl.pallas_call(
        _memkv_kernel,
        grid=(n // tm,),
        in_specs=[
            pl.BlockSpec((tm, D_MODEL), lambda i: (i, 0)),
            pl.BlockSpec((D_MODEL, MEM_WIDTH), lambda i: (0, 0)),
            pl.BlockSpec((D_MODEL, MEM_WIDTH), lambda i: (0, 0)),
            pl.BlockSpec((1, MEM_WIDTH), lambda i: (0, 0)),
        ],
        out_specs=[pl.BlockSpec((tm * MEM_HEADS, MEM_HEAD_DIM), lambda i: (i, 0))] * 2,
        out_shape=[jax.ShapeDtypeStruct((n * MEM_HEADS, MEM_HEAD_DIM), F32)] * 2,
        compiler_params=_cparams(("arbitrary",)),
        name="memkv",
    )(mem2d, wk, wv, kn)


def _memattn_kernel(h_ref, g_ref, wq_ref, qn_ref, mk_ref, mv_ref, wo_ref, gf_ref, wc_ref, bc_ref,
                    o_ref, n_ref, c_ref, *, rows_per_mem, mem_len):
    tm, m = h_ref.shape[0], mk_ref.shape[0] // MEM_HEADS
    wq = wq_ref[...].astype(BF16)
    wo = wo_ref[...].astype(BF16)
    mks = [mk_ref[pl.ds(hd, m, stride=MEM_HEADS), :].astype(BF16) for hd in range(MEM_HEADS)]
    mvs = [mv_ref[pl.ds(hd, m, stride=MEM_HEADS), :].astype(BF16) for hd in range(MEM_HEADS)]
    rc = min(tm, MEMATTN_ROW_CHUNK)
    for r in range(tm // rc):
        rs = slice(r * rc, (r + 1) * rc)
        h = h_ref[rs, :]
        n = _row_norm(h, g_ref[...]).astype(BF16)
        q = _head_norm(jnp.dot(n, wq, preferred_element_type=F32), qn_ref[...]).astype(BF16)
        if rows_per_mem is not None:
            row = r * rc + lax.broadcasted_iota(I32, (rc, m), 0)
            col = lax.broadcasted_iota(I32, (rc, m), 1)
            mask = jnp.where(row // rows_per_mem == col // mem_len, 0.0, -jnp.inf)
        outs = []
        for hd in range(MEM_HEADS):
            cs = slice(hd * MEM_HEAD_DIM, (hd + 1) * MEM_HEAD_DIM)
            lg = _dot_nt(q[:, cs], mks[hd]) * (MEM_HEAD_DIM ** -0.5)
            if rows_per_mem is not None:
                lg = lg + mask
            e = jnp.exp(lg - jnp.max(lg, axis=-1, keepdims=True))
            l = jnp.sum(e, axis=-1, keepdims=True)
            outs.append(jnp.dot(e.astype(BF16), mvs[hd], preferred_element_type=F32) / l)
        o = jnp.concatenate(outs, axis=-1).astype(BF16)
        h2 = h + jnp.dot(o, wo, preferred_element_type=F32)
        o_ref[rs, :] = h2
        n_ref[rs, :], c_ref[rs, :] = _route(h2, gf_ref[...], wc_ref[...], bc_ref[...])


def _memattn(h2d, gain, wq, qnorm, mk, mv, wo, gain_ffn, w_cat, b_cat, *, tm, mem_rows, tiles_per_mem,
             rows_per_mem, mem_len):
    n = h2d.shape[0]
    qn = jnp.tile(qnorm, MEM_HEADS).reshape(1, MEM_WIDTH)
    return pl.pallas_call(
        functools.partial(_memattn_kernel, rows_per_mem=rows_per_mem, mem_len=mem_len),
        grid=(n // tm,),
        in_specs=[
            pl.BlockSpec((tm, D_MODEL), lambda i: (i, 0)),
            pl.BlockSpec((1, D_MODEL), lambda i: (0, 0)),
            pl.BlockSpec((D_MODEL, MEM_WIDTH), lambda i: (0, 0)),
            pl.BlockSpec((1, MEM_WIDTH), lambda i: (0, 0)),
            pl.BlockSpec((mem_rows * MEM_HEADS, MEM_HEAD_DIM), lambda i: (i // tiles_per_mem, 0)),
            pl.BlockSpec((mem_rows * MEM_HEADS, MEM_HEAD_DIM), lambda i: (i // tiles_per_mem, 0)),
            pl.BlockSpec((MEM_WIDTH, D_MODEL), lambda i: (0, 0)),
            pl.BlockSpec((1, D_MODEL), lambda i: (0, 0)),
            pl.BlockSpec((D_MODEL, 2 * LANES), lambda i: (0, 0)),
            pl.BlockSpec((1, 2 * LANES), lambda i: (0, 0)),
        ],
        out_specs=[pl.BlockSpec((tm, D_MODEL), lambda i: (i, 0)), pl.BlockSpec((tm, D_MODEL), lambda i: (i, 0)),
                   pl.BlockSpec((tm, LANES), lambda i: (i, 0))],
        out_shape=[jax.ShapeDtypeStruct((n, D_MODEL), F32), jax.ShapeDtypeStruct((n, D_MODEL), BF16),
                   jax.ShapeDtypeStruct((n, LANES), F32)],
        compiler_params=_cparams(("arbitrary",)),
        name="memattn",
    )(h2d, gain.reshape(1, D_MODEL), wq, qn, mk, mv, wo, gain_ffn.reshape(1, D_MODEL), w_cat, b_cat)


def _route(h, gain, w_cat, b_cat):
    n = _row_norm(h, gain).astype(BF16)
    lg = jnp.dot(n, w_cat.astype(BF16), preferred_element_type=F32) + b_cat
    gl, el = lg[:, :LANES], lg[:, LANES:]
    lane = lax.broadcasted_iota(I32, gl.shape, 1)
    lanef = lane.astype(F32)

    def first_max_lane(v, vmax):
        return jnp.min(jnp.where(v == vmax, lanef, float(LANES)), axis=-1, keepdims=True).astype(I32)

    gl = jnp.where(lane < N_GROUPS, gl, -jnp.inf)
    ge = jnp.exp(gl - jnp.max(gl, axis=-1, keepdims=True))
    gp = ge / jnp.sum(ge, axis=-1, keepdims=True)
    g_gate = jnp.max(gp, axis=-1, keepdims=True)
    g_sel = first_max_lane(gp, g_gate)
    in_grp = (lane // EXPERTS_PER_GROUP == g_sel) & (lane < N_EXPERTS)
    el = jnp.where(in_grp, el, -jnp.inf)
    ee = jnp.exp(el - jnp.max(el, axis=-1, keepdims=True))
    ep = jnp.where(in_grp, ee / jnp.sum(ee, axis=-1, keepdims=True), -jnp.inf)
    w1 = jnp.max(ep, axis=-1, keepdims=True)
    i1 = first_max_lane(ep, w1)
    ep2 = jnp.where(lane == i1, -jnp.inf, ep)
    w2 = jnp.max(ep2, axis=-1, keepdims=True)
    i2 = first_max_lane(ep2, w2)
    tot = w1 + w2
    comb = jnp.where(lane == i1, w1 / tot * g_gate, jnp.where(lane == i2, w2 / tot * g_gate, 0.0))
    return n, jnp.where(lane == GSEL_LANE, g_sel.astype(F32), comb)


def _cast_kernel(x_ref, o_ref):
    o_ref[...] = x_ref[...].astype(BF16)


def _cast_bf16(w):
    ne, r, c = w.shape
    return pl.pallas_call(
        _cast_kernel,
        grid=(ne,),
        in_specs=[pl.BlockSpec((None, r, c), lambda e: (e, 0, 0))],
        out_specs=pl.BlockSpec((None, r, c), lambda e: (e, 0, 0)),
        out_shape=jax.ShapeDtypeStruct(w.shape, BF16),
        compiler_params=_cparams(("arbitrary",)),
        name="cast_bf16",
    )(w)


GSEL_LANE = N_EXPERTS
MOE_ROWS = 64
MOE_MAIN_SLACK = 0
MOE_FF_SPLIT = 1


def _moe_routed_kernel(n_ref, c_ref, h_hbm, wg_ref, wu_ref, wd_ref, o_ref,
                       tri_scr, key_scr, xc_scr, yc_scr, cw_scr, nch_ref, sem, *, tm):
    i, g, e, f = (pl.program_id(k) for k in range(4))
    main_rows = tm // N_GROUPS + MOE_MAIN_SLACK
    first_e = (e == 0) & (f == 0)
    last_e = (e == EXPERTS_PER_GROUP - 1) & (f == MOE_FF_SPLIT - 1)

    def onehot_rows(start, rows):
        want = (start + lax.broadcasted_iota(I32, (rows, 1), 0)).astype(F32)
        return jnp.where(key_scr[pl.ds(g, 1), :] == want, 1.0, 0.0).astype(BF16)

    def tail_start(c):
        return pl.multiple_of(main_rows + c * MOE_ROWS, math.gcd(main_rows, MOE_ROWS))

    @pl.when((i == 0) & (g == 0) & first_e)
    def _():
        r = lax.broadcasted_iota(I32, (tm, tm), 0)
        c = lax.broadcasted_iota(I32, (tm, tm), 1)
        tri_scr[...] = jnp.where(r < c, 1.0, 0.0).astype(BF16)

    def residual_copy():
        return pltpu.make_async_copy(h_hbm.at[pl.ds(pl.multiple_of(i * tm, tm), tm)], o_ref, sem)

    @pl.when((g == 0) & first_e)
    def _():
        residual_copy().start()
        lane = lax.broadcasted_iota(I32, (tm, LANES), 1).astype(F32)
        onehot = jnp.where(lane == c_ref[:, GSEL_LANE:GSEL_LANE + 1], 1.0, 0.0).astype(BF16)
        er = lax.broadcasted_iota(I32, (SUBLANES, LANES), 0)
        ec = lax.broadcasted_iota(I32, (SUBLANES, LANES), 1)
        eye = jnp.where(er == ec, 1.0, 0.0).astype(BF16)
        mask_t = _dot_nt(eye, onehot)
        rank_t = jnp.dot(mask_t.astype(BF16), tri_scr[...], preferred_element_type=F32)
        key_scr[...] = jnp.where(mask_t > 0.5, rank_t, -1.0)
        for gg in range(N_GROUPS):
            cnt = jnp.sum(mask_t[gg:gg + 1, :], axis=-1, keepdims=True)
            extra = jnp.ceil(jnp.maximum(cnt - main_rows, 0.0) * (1.0 / MOE_ROWS))
            nch_ref[gg] = jnp.sum(extra).astype(I32)

    ntail = nch_ref[g]

    @pl.when(first_e)
    def _():
        comb = c_ref[...]
        hi = comb.astype(BF16)
        r1 = comb - hi.astype(F32)
        mid = r1.astype(BF16)
        lo = (r1 - mid.astype(F32)).astype(BF16)

        def compact(start, rows):
            s = onehot_rows(start, rows)
            xc_scr[pl.ds(start, rows), :] = jnp.dot(s, n_ref[...], preferred_element_type=F32).astype(BF16)
            cw_scr[pl.ds(start, rows), :] = (jnp.dot(s, hi, preferred_element_type=F32)
                                             + jnp.dot(s, mid, preferred_element_type=F32)
                                             + jnp.dot(s, lo, preferred_element_type=F32))
            yc_scr[pl.ds(start, rows), :] = jnp.zeros((rows, D_MODEL), F32)

        compact(0, main_rows)
        lax.fori_loop(0, ntail, lambda c, carry: (compact(tail_start(c), MOE_ROWS), carry)[1], 0)

    wg = wg_ref[...].astype(BF16)
    wu = wu_ref[...].astype(BF16)
    wd = wd_ref[...].astype(BF16)
    ex = g * EXPERTS_PER_GROUP + e

    def expert(start, rows):
        x = xc_scr[pl.ds(start, rows), :]
        a = jnp.dot(x, wg, preferred_element_type=F32)
        u = jnp.dot(x, wu, preferred_element_type=F32)
        lane = lax.broadcasted_iota(I32, (rows, LANES), 1)
        cw = jnp.sum(jnp.where(lane == ex, cw_scr[pl.ds(start, rows), :], 0.0), axis=-1, keepdims=True)
        hid = (a * jax.nn.sigmoid(a) * u * cw).astype(BF16)
        yc_scr[pl.ds(start, rows), :] += jnp.dot(hid, wd, preferred_element_type=F32)

    expert(0, main_rows)
    lax.fori_loop(0, ntail, lambda c, carry: (expert(tail_start(c), MOE_ROWS), carry)[1], 0)

    @pl.when((g == 0) & last_e)
    def _():
        residual_copy().wait()

    @pl.when(last_e)
    def _():
        def scatter(start, rows):
            s = onehot_rows(start, rows)
            y = yc_scr[pl.ds(start, rows), :].astype(BF16)
            o_ref[...] += lax.dot_general(s, y, (((0,), (0,)), ((), ())), preferred_element_type=F32)

        scatter(0, main_rows)
        lax.fori_loop(0, ntail, lambda c, carry: (scatter(tail_start(c), MOE_ROWS), carry)[1], 0)


def _moe_routed(n3, comb, h2d, w_gate_l, w_up_l, w_down_l, *, tm):
    n = h2d.shape[0]
    ff = EXPERT_FF // MOE_FF_SPLIT

    def wmap(i, g, e, f):
        return (g * EXPERTS_PER_GROUP + e, 0, f)

    return pl.pallas_call(
        functools.partial(_moe_routed_kernel, tm=tm),
        grid=(n // tm, N_GROUPS, EXPERTS_PER_GROUP, MOE_FF_SPLIT),
        in_specs=[
            pl.BlockSpec((tm, D_MODEL), lambda i, g, e, f: (i, 0), pipeline_mode=pl.Buffered(1)),
            pl.BlockSpec((tm, LANES), lambda i, g, e, f: (i, 0), pipeline_mode=pl.Buffered(1)),
            pl.BlockSpec(memory_space=pl.ANY),
            pl.BlockSpec((None, D_MODEL, ff), wmap),
            pl.BlockSpec((None, D_MODEL, ff), wmap),
            pl.BlockSpec((None, ff, D_MODEL), lambda i, g, e, f: (g * EXPERTS_PER_GROUP + e, f, 0)),
        ],
        out_specs=pl.BlockSpec((tm, D_MODEL), lambda i, g, e, f: (i, 0)),
        out_shape=jax.ShapeDtypeStruct((n, D_MODEL), F32),
        scratch_shapes=[
            pltpu.VMEM((tm, tm), BF16),
            pltpu.VMEM((SUBLANES, tm), F32),
            pltpu.VMEM((tm + MOE_ROWS, D_MODEL), BF16),
            pltpu.VMEM((tm + MOE_ROWS, D_MODEL), F32),
            pltpu.VMEM((tm + MOE_ROWS, LANES), F32),
            pltpu.SMEM((N_GROUPS,), I32),
            pltpu.SemaphoreType.DMA(()),
        ],
        compiler_params=pltpu.CompilerParams(
            dimension_semantics=("arbitrary",) * 4, vmem_limit_bytes=MOE_VMEM_LIMIT),
        name="moe_routed",
    )(n3, comb, h2d, w_gate_l, w_up_l, w_down_l)


def _moe_kernel(n_ref, c_ref, h_ref, wg_ref, wu_ref, wd_ref, o_ref):
    e = pl.program_id(1)

    @pl.when(e == 0)
    def _():
        o_ref[...] = h_ref[...]

    x = n_ref[...]
    a = jnp.dot(x, wg_ref[...].astype(BF16), preferred_element_type=F32)
    u = jnp.dot(x, wu_ref[...].astype(BF16), preferred_element_type=F32)
    lane = lax.broadcasted_iota(I32, c_ref.shape, 1)
    c = jnp.sum(jnp.where(lane == e, c_ref[...], 0.0), axis=-1, keepdims=True)
    hid = (a * jax.nn.sigmoid(a) * u * c).astype(BF16)
    o_ref[...] += jnp.dot(hid, wd_ref[...].astype(BF16), preferred_element_type=F32)


def _moe(n3, comb, h2d, w_gate_l, w_up_l, w_down_l, *, tm):
    n = h2d.shape[0]
    return pl.pallas_call(
        _moe_kernel,
        grid=(n // tm, N_EXPERTS),
        in_specs=[
            pl.BlockSpec((tm, D_MODEL), lambda i, e: (i, 0)),
            pl.BlockSpec((tm, LANES), lambda i, e: (i, 0)),
            pl.BlockSpec((tm, D_MODEL), lambda i, e: (i, 0)),
            pl.BlockSpec((None, D_MODEL, EXPERT_FF), lambda i, e: (e, 0, 0)),
            pl.BlockSpec((None, D_MODEL, EXPERT_FF), lambda i, e: (e, 0, 0)),
            pl.BlockSpec((None, EXPERT_FF, D_MODEL), lambda i, e: (e, 0, 0)),
        ],
        out_specs=pl.BlockSpec((tm, D_MODEL), lambda i, e: (i, 0)),
        out_shape=jax.ShapeDtypeStruct((n, D_MODEL), F32),
        compiler_params=_cparams(("arbitrary", "arbitrary")),
        name="moe",
    )(n3, comb, h2d, w_gate_l, w_up_l, w_down_l)


def _post_mixer(p, b_out, ws_eff, ab_eff, x2d, mk, mv, lw, *, tm_mix, tm_mem, mem_rows, tiles_per_mem,
                rows_per_mem, mem_len, tm_moe):
    h1 = _mixout(p, b_out, ws_eff, ab_eff, x2d, lw["w_out"], tm=tm_mix)
    h2, n3, comb = _memattn(h1, lw["norm_mem"], lw["mem_wq"], lw["mem_qnorm"], mk, mv, lw["mem_wo"],
                            lw["norm_ffn"], lw["w_cat"], lw["b_cat"], tm=tm_mem, mem_rows=mem_rows,
                            tiles_per_mem=tiles_per_mem, rows_per_mem=rows_per_mem, mem_len=mem_len)
    moe = _moe_routed if tm_moe > N_GROUPS * MOE_ROWS else _moe
    return moe(n3, comb, h2, lw["w_gate"], lw["w_up"], lw["w_down"], tm=tm_moe)


def kernel(x_prompt, x_sample, cache_k, cache_v, cache_idx_k, cache_mem_k, cache_mem_v, page_table,
           mem_prompt, norm_mix, w_in, a_vnorm, a_ws, a_b, b_qnorm, b_knorm, rel_bias, w_out,
           norm_mem, mem_wq, mem_wk, mem_wv, mem_qnorm, mem_knorm, mem_wo, norm_ffn,
           w_group, b_group, w_router, b_router, w_gate, w_up, w_down):
    bp, t, d = x_prompt.shape
    bd, s, _ = x_sample.shape
    depth = w_in.shape[0]
    mem_len = mem_prompt.shape[1]
    assert d == D_MODEL and t % KEY_CHUNK == 0 and s <= 8 and (bd * s) % LANES == 0

    bias_pairs, sbias, sconst = _bias_tables(rel_bias)
    hp = x_prompt.reshape(bp * t, d)
    hs = x_sample.reshape(bd * s, d)
    outs = [[] for _ in range(9)]
    for l in range(depth):
        w_cat = jnp.zeros((d, 2 * LANES), F32)
        w_cat = w_cat.at[:, :N_GROUPS].set(w_group[l]).at[:, LANES:LANES + N_EXPERTS].set(w_router[l])
        b_cat = jnp.zeros((1, 2 * LANES), F32)
        b_cat = b_cat.at[0, :N_GROUPS].set(b_group[l]).at[0, LANES:LANES + N_EXPERTS].set(b_router[l])
        lw = dict(w_out=w_out[l], norm_mem=norm_mem[l], mem_wq=mem_wq[l], mem_qnorm=mem_qnorm[l],
                  mem_wo=mem_wo[l], norm_ffn=norm_ffn[l], w_cat=w_cat, b_cat=b_cat)

        p, k, v, tail = _proj(hp, norm_mix[l], w_in[l], a_vnorm[l], b_qnorm[l], b_knorm[l],
                              tm=1024, want_vn=False)
        later_weights = (w_gate[l], w_up[l], w_down[l], w_out[l][None])
        fuse_cast = (bp * (t // Q_BLOCK)) % N_EXPERTS == 0
        b_out, casted = _dsa_prompt(p, tail, bias_pairs, later_weights if fuse_cast else (), batch=bp, seq=t)
        if not fuse_cast:
            casted = tuple(_cast_bf16(w) for w in later_weights)
        lw.update(w_gate=casted[0], w_up=casted[1], w_down=casted[2], w_out=casted[3][0])
        mk, mv = _memkv(mem_prompt.reshape(bp * mem_len, d), mem_wk[l], mem_wv[l], mem_knorm[l])
        hp = _post_mixer(p, b_out, a_ws[l], a_b[l].T, hp, mk, mv, lw, tm_mix=512, tm_mem=512,
                         mem_rows=mem_len, tiles_per_mem=t // 512, rows_per_mem=None, mem_len=mem_len,
                         tm_moe=1024)
        outs[0].append(k.reshape(bp, t, B_KV_HEADS, B_HEAD_DIM))
        outs[1].append(v.reshape(bp, t, B_KV_HEADS, B_HEAD_DIM))
        outs[2].append(tail[:, :IDX_DIM].reshape(bp, t, IDX_DIM))
        outs[3].append(mk.reshape(bp, mem_len, MEM_HEADS, MEM_HEAD_DIM))
        outs[4].append(mv.reshape(bp, mem_len, MEM_HEADS, MEM_HEAD_DIM))

        ns = bd * s
        p_s, k_s, v_s, tail_s, vn_s = _proj(hs, norm_mix[l], w_in[l], a_vnorm[l], b_qnorm[l], b_knorm[l],
                                            tm=ns, want_vn=True)
        b_out_s = _dsa_sample(p_s, tail_s, k_s.reshape(ns, KV_WIDTH), v_s.reshape(ns, KV_WIDTH),
                              cache_k[l], cache_v[l], cache_idx_k[l], page_table,
                              sbias, sconst, dec_batch=bd, dec_seq=s)
        ws_s = jnp.einsum("ab,gts->gatbs", jnp.eye(ns // s, dtype=F32), a_ws[l][:, :s, :s])
        ws_s = ws_s.reshape(A_GROUPS, ns, ns)
        ab_s = jnp.tile(a_b[l][:, :s].T, (ns // s, 1))
        mem_tile = 32
        hs = _post_mixer(p_s, b_out_s, ws_s, ab_s, hs,
                         cache_mem_k[l].reshape(bd * mem_len * MEM_HEADS, MEM_HEAD_DIM),
                         cache_mem_v[l].reshape(bd * mem_len * MEM_HEADS, MEM_HEAD_DIM), lw,
                         tm_mix=ns, tm_mem=mem_tile, mem_rows=mem_tile // s * mem_len, tiles_per_mem=1,
                         rows_per_mem=s, mem_len=mem_len, tm_moe=ns)
        outs[5].append(k_s.reshape(bd, s, B_KV_HEADS, B_HEAD_DIM))
        outs[6].append(v_s.reshape(bd, s, B_KV_HEADS, B_HEAD_DIM))
        outs[7].append(tail_s[:, :IDX_DIM].reshape(bd, s, IDX_DIM))
        outs[8].append(vn_s.reshape(bd, s, A_GROUPS, A_CH))
    return (hp.reshape(bp, t, d), hs.reshape(bd, s, d)) + tuple(jnp.stack(o) for o in outs)
```

```python
import functools
import math

import jax
import jax.numpy as jnp
from jax import lax
from jax.experimental import pallas as pl
from jax.experimental.pallas import tpu as pltpu

BF16 = jnp.bfloat16
F32 = jnp.float32
I32 = jnp.int32

D_MODEL = 2048
A_GROUPS = 8
A_CH = 128
A_WIDTH = A_GROUPS * A_CH
A_CHUNK = 128
B_HEADS = 8
B_HEAD_DIM = 128
B_KV_HEADS = 2
B_GQA = B_HEADS // B_KV_HEADS
B_WIDTH = B_HEADS * B_HEAD_DIM
KV_WIDTH = B_KV_HEADS * B_HEAD_DIM
IDX_HEADS = 16
IDX_DIM = 64
TOPK_MAX = 256
Q_BLOCK = 128
REL_BUCKETS = 32
REL_MAX_DIST = 128
MEM_HEADS = 4
MEM_HEAD_DIM = 128
MEM_WIDTH = MEM_HEADS * MEM_HEAD_DIM
N_GROUPS = 4
EXPERTS_PER_GROUP = 4
N_EXPERTS = N_GROUPS * EXPERTS_PER_GROUP
EXPERT_FF = 512
PAGE_SIZE = 128
EPS = 1e-6

COL_U = 0
COL_V = COL_U + A_WIDTH
COL_Q = COL_V + A_WIDTH
COL_K = COL_Q + B_WIDTH
COL_VV = COL_K + KV_WIDTH
COL_QI = COL_VV + KV_WIDTH
COL_MAIN = COL_QI + IDX_HEADS * IDX_DIM
TAIL_COLS = IDX_DIM + IDX_HEADS
LANES = 128
SUBLANES = 8
BF16_SUBLANES = 2 * SUBLANES
PROJ_TN = 512
PROJ_ROW_CHUNK = 256
MEMATTN_ROW_CHUNK = 512
KEY_CHUNK = 256
NEG_INIT = -1e30
VMEM_LIMIT = 56 * 1024 * 1024
MOE_VMEM_LIMIT = 60 * 1024 * 1024


def _cparams(sem):
    return pltpu.CompilerParams(dimension_semantics=sem, vmem_limit_bytes=VMEM_LIMIT)


def _gelu(x):
    return 0.5 * x * (1.0 + lax.erf(x * (2.0 ** -0.5)))


def _head_norm(a, gain, width=LANES):
    outs = []
    for c in range(a.shape[1] // width):
        blk = a[:, c * width:(c + 1) * width]
        ms = jnp.mean(blk * blk, axis=-1, keepdims=True)
        outs.append(blk * lax.rsqrt(ms + EPS) * gain[:, c * width:(c + 1) * width])
    return outs[0] if len(outs) == 1 else jnp.concatenate(outs, axis=-1)


def _row_norm(x, gain):
    ms = jnp.mean(x * x, axis=-1, keepdims=True)
    return x * lax.rsqrt(ms + EPS) * gain


def _dot_nt(a, b):
    return lax.dot_general(a, b, (((1,), (1,)), ((), ())), preferred_element_type=F32)


def _proj_kernel(x_ref, g_ref, w_ref, wt_ref, avn_ref, qn_ref, kn_ref, *rest, want_vn):
    if want_vn:
        p_ref, k_ref, v_ref, tail_ref, vn_ref, xn_scr = rest
    else:
        p_ref, k_ref, v_ref, tail_ref, xn_scr = rest
        vn_ref = None
    j = pl.program_id(1)
    ju = COL_V // PROJ_TN
    jv = COL_Q // PROJ_TN
    jq = COL_K // PROJ_TN
    jkv = COL_QI // PROJ_TN

    @pl.when(j == 0)
    def _():
        xn = _row_norm(x_ref[...], g_ref[...]).astype(BF16)
        xn_scr[...] = xn
        wrow = lax.broadcasted_iota(I32, wt_ref.shape, 0)
        wt = jnp.where(wrow < TAIL_COLS, wt_ref[...], 0.0).astype(BF16)
        t = _dot_nt(xn, wt)
        lane = lax.broadcasted_iota(I32, t.shape, 1)
        tail_ref[...] = jnp.where(lane >= IDX_DIM, t * (IDX_HEADS ** -0.5), t)

    tm = xn_scr.shape[0]
    rc = min(tm, PROJ_ROW_CHUNK)

    def row_chunks(epilogue):
        w = w_ref[...].astype(BF16)
        for r in range(tm // rc):
            rs = slice(r * rc, (r + 1) * rc)
            epilogue(_dot_nt(xn_scr[rs, :], w), rs)

    @pl.when(j < ju)
    def _():
        def epi(acc, rs):
            p_ref[rs, :] = _gelu(acc).astype(BF16)
        row_chunks(epi)

    @pl.when((j >= ju) & (j < jv))
    def _():
        def epi(acc, rs):
            vn = _head_norm(_gelu(acc), avn_ref[...])
            p_ref[rs, :] = vn.astype(BF16)
            if vn_ref is not None:
                vn_ref[rs, :] = vn
        row_chunks(epi)

    @pl.when((j >= jv) & (j < jq))
    def _():
        def epi(acc, rs):
            p_ref[rs, :] = _head_norm(acc, qn_ref[...]).astype(BF16)
        row_chunks(epi)

    @pl.when(j == jq)
    def _():
        def epi(acc, rs):
            k = _head_norm(acc[:, :KV_WIDTH], kn_ref[...])
            v = acc[:, KV_WIDTH:]
            for g in range(B_KV_HEADS):
                dst = pl.ds(rs.start * B_KV_HEADS + g, rs.stop - rs.start, stride=B_KV_HEADS)
                k_ref[dst, :] = k[:, g * B_HEAD_DIM:(g + 1) * B_HEAD_DIM]
                v_ref[dst, :] = v[:, g * B_HEAD_DIM:(g + 1) * B_HEAD_DIM]
            p_ref[rs, :] = jnp.concatenate([k, v], axis=-1).astype(BF16)
        row_chunks(epi)

    @pl.when(j >= jkv)
    def _():
        def epi(acc, rs):
            p_ref[rs, :] = acc.astype(BF16)
        row_chunks(epi)


def _proj(x2d, gain, w_in_l, a_vnorm_l, b_qnorm_l, b_knorm_l, *, tm, want_vn):
    n = x2d.shape[0]
    nj = COL_MAIN // PROJ_TN
    w_in_t = jnp.swapaxes(w_in_l, 0, 1)
    avn = a_vnorm_l.reshape(1, A_WIDTH)
    qn = jnp.tile(b_qnorm_l, PROJ_TN // B_HEAD_DIM).reshape(1, PROJ_TN)
    kn = jnp.tile(b_knorm_l, B_KV_HEADS).reshape(1, KV_WIDTH)
    ju = COL_V // PROJ_TN
    nv = A_WIDTH // PROJ_TN

    def vmap_(i, j):
        return (0, jnp.clip(j - ju, 0, nv - 1))

    in_specs = [
        pl.BlockSpec((tm, D_MODEL), lambda i, j: (i, 0)),
        pl.BlockSpec((1, D_MODEL), lambda i, j: (0, 0)),
        pl.BlockSpec((PROJ_TN, D_MODEL), lambda i, j: (j, 0)),
        pl.BlockSpec((LANES, D_MODEL), lambda i, j: (COL_MAIN // LANES, 0)),
        pl.BlockSpec((1, PROJ_TN), vmap_),
        pl.BlockSpec((1, PROJ_TN), lambda i, j: (0, 0)),
        pl.BlockSpec((1, KV_WIDTH), lambda i, j: (0, 0)),
    ]
    out_shape = [
        jax.ShapeDtypeStruct((n, COL_MAIN), BF16),
        jax.ShapeDtypeStruct((n * B_KV_HEADS, B_HEAD_DIM), F32),
        jax.ShapeDtypeStruct((n * B_KV_HEADS, B_HEAD_DIM), F32),
        jax.ShapeDtypeStruct((n, LANES), F32),
    ]
    out_specs = [
        pl.BlockSpec((tm, PROJ_TN), lambda i, j: (i, j)),
        pl.BlockSpec((tm * B_KV_HEADS, B_HEAD_DIM), lambda i, j: (i, 0)),
        pl.BlockSpec((tm * B_KV_HEADS, B_HEAD_DIM), lambda i, j: (i, 0)),
        pl.BlockSpec((tm, LANES), lambda i, j: (i, 0)),
    ]
    if want_vn:
        out_shape.append(jax.ShapeDtypeStruct((n, A_WIDTH), F32))
        out_specs.append(pl.BlockSpec((tm, PROJ_TN), lambda i, j: (i, jnp.clip(j - ju, 0, nv - 1))))
    return pl.pallas_call(
        functools.partial(_proj_kernel, want_vn=want_vn),
        grid=(n // tm, nj),
        in_specs=in_specs,
        out_specs=out_specs,
        out_shape=out_shape,
        scratch_shapes=[pltpu.VMEM((tm, D_MODEL), BF16)],
        compiler_params=_cparams(("arbitrary", "arbitrary")),
        name="proj",
    )(x2d, gain.reshape(1, D_MODEL), w_in_t, w_in_t, avn, qn, kn)


def _t5_bucket(d):
    max_exact = REL_BUCKETS // 2
    d = jnp.maximum(d, 0)
    ratio = jnp.log(jnp.maximum(d, 1).astype(F32) / max_exact) / math.log(REL_MAX_DIST / max_exact)
    large = jnp.minimum(max_exact + jnp.floor(ratio * (REL_BUCKETS - max_exact)).astype(I32), REL_BUCKETS - 1)
    return jnp.where(d < max_exact, d, large)


def _bias_lookup(bucket, rb_ref, h):
    acc = jnp.zeros(bucket.shape, F32)
    for r in range(REL_BUCKETS):
        acc = jnp.where(bucket == r, rb_ref[r, h], acc)
    return acc


def _bias_kernel(rb_ref, pair_ref, samp_ref, sconst_ref):
    q = lax.broadcasted_iota(I32, (Q_BLOCK, Q_BLOCK), 0)
    kc = lax.broadcasted_iota(I32, (Q_BLOCK, Q_BLOCK), 1)
    nvar = pair_ref.shape[0]
    rels = range(-(nvar - 1), 2)
    buckets = {r: _t5_bucket(q - kc - r * Q_BLOCK) for r in rels}
    for h in range(B_HEADS):
        g, hh = divmod(h, B_GQA)
        tiles = {r: _bias_lookup(buckets[r], rb_ref, h) for r in rels}
        for var in range(nvar):
            r0 = var - (nvar - 1)
            for t in range(2):
                pair_ref[var, g, hh * Q_BLOCK:(hh + 1) * Q_BLOCK, t * Q_BLOCK:(t + 1) * Q_BLOCK] = tiles[r0 + t]
    rows = B_GQA * SUBLANES
    row = lax.broadcasted_iota(I32, (rows, 2 * PAGE_SIZE), 0)
    col = lax.broadcasted_iota(I32, (rows, 2 * PAGE_SIZE), 1)
    t = row % SUBLANES
    d = jnp.where(col < PAGE_SIZE, t + PAGE_SIZE - col, t - (col - PAGE_SIZE))
    bucket = _t5_bucket(d)
    far = jnp.full((rows, LANES), REL_BUCKETS - 1, I32)
    rowc = lax.broadcasted_iota(I32, (rows, LANES), 0)
    for g in range(B_KV_HEADS):
        acc = jnp.zeros((rows, 2 * PAGE_SIZE), F32)
        accc = jnp.zeros((rows, LANES), F32)
        for hh in range(B_GQA):
            h = g * B_GQA + hh
            acc = jnp.where(row // SUBLANES == hh, _bias_lookup(bucket, rb_ref, h), acc)
            accc = jnp.where(rowc // SUBLANES == hh, _bias_lookup(far, rb_ref, h), accc)
        samp_ref[g] = acc
        sconst_ref[g] = accc


N_PAIR_VARIANTS = 4


def _bias_tables(rel_bias):
    rows = B_GQA * SUBLANES
    return pl.pallas_call(
        _bias_kernel,
        in_specs=[pl.BlockSpec(memory_space=pltpu.SMEM)],
        out_shape=[
            jax.ShapeDtypeStruct((N_PAIR_VARIANTS, B_KV_HEADS, B_GQA * Q_BLOCK, KEY_CHUNK), F32),
            jax.ShapeDtypeStruct((B_KV_HEADS, rows, 2 * PAGE_SIZE), F32),
            jax.ShapeDtypeStruct((B_KV_HEADS, rows, LANES), F32),
        ],
        name="bias_tables",
    )(rel_bias)


def _key_to_f32(key):
    bits = jnp.where(key < 0, key ^ I32(-2 ** 31), ~key)
    return lax.bitcast_convert_type(bits, F32)


def _kth_largest(count_ge, shape, k):
    def body(it, prefix):
        bit = 31 - it
        cand = prefix | lax.shift_left(I32(1), bit)
        cnt = count_ge(_key_to_f32(cand))
        return jnp.where(cnt >= k, cand, prefix)

    prefix = lax.fori_loop(0, 32, body, jnp.zeros(shape, I32))
    thr = _key_to_f32(prefix)
    return jnp.where((prefix & I32(-2 ** 23)) == 0, -jnp.inf, thr)


NO_INDEX_BOUND = 2 ** 30
TIE_REPAIR_ROUNDS = 2


def _topk_select(fold_sum, fold_min, shape, k, index_bits):
    thr = _kth_largest(lambda t: fold_sum(lambda sc, kp: sc >= t), shape, k)
    c_ge = fold_sum(lambda sc, kp: sc >= thr)
    no_bound = jnp.full(shape, NO_INDEX_BOUND, I32)

    def repair(_):
        v = thr
        strict = jnp.zeros(shape, F32)

        def in_set(sc, v, strict):
            return (sc > v) | ((sc == v) & (strict < 0.5))

        for _ in range(TIE_REPAIR_ROUNDS):
            c_set = fold_sum(lambda sc, kp: in_set(sc, v, strict))
            vmin = fold_min(lambda sc, kp: jnp.where(in_set(sc, v, strict), sc, jnp.inf))
            c_gt = fold_sum(lambda sc, kp: sc > vmin)
            drop = (c_set > k) & (c_gt >= k)
            v = jnp.where(drop, vmin, v)
            strict = jnp.where(drop, 1.0, strict)
        c_set = fold_sum(lambda sc, kp: in_set(sc, v, strict))
        vt = fold_min(lambda sc, kp: jnp.where(in_set(sc, v, strict), sc, jnp.inf))
        need = k - fold_sum(lambda sc, kp: sc > vt)

        def idx_body(it, m):
            cand = m | lax.shift_left(I32(1), index_bits - 1 - it)
            below = fold_sum(lambda sc, kp: (sc == vt) & (kp < cand))
            return jnp.where(below < need, cand, m)

        m = lax.fori_loop(0, index_bits, idx_body, jnp.zeros(shape, I32))
        tie = c_set > k
        v_out = jnp.where(tie, vt, v)
        m_out = jnp.where(tie, m, jnp.where(strict > 0.5, -1, no_bound))
        return v_out, m_out

    return lax.cond(jnp.max(c_ge) > k, repair, lambda _: (thr, no_bound), 0)


CHUNKS_PER_TRIP = 4


def _chunk_loop(n, body, init):
    def trip(j, c):
        for k in range(CHUNKS_PER_TRIP):
            c = body(CHUNKS_PER_TRIP * j + k, c)
        return c

    whole = n // CHUNKS_PER_TRIP
    carry = lax.fori_loop(0, whole, trip, init)
    done = whole * CHUNKS_PER_TRIP
    rem = n - done
    carry = lax.cond(rem >= 2, lambda c: body(done + 1, body(done, c)), lambda c: c, carry)
    last = done + 2 * (rem // 2)
    return lax.fori_loop(last, n, body, carry)


def _dsa_prompt_kernel(q_ref, kv_ref, qia_ref, qib_ref, tailk_ref, tailq_ref, bias_ref, *rest, topk, n_cast):
    cast_in, rest = rest[:n_cast], rest[n_cast:]
    o_ref, rest = rest[0], rest[1:]
    cast_out, rest = rest[:n_cast], rest[n_cast:]
    kil_scr, kir_scr, sc_scr, lg_scr, m_scr, l_scr, acc_scr = rest
    i = pl.program_id(1)

    for src, dst in zip(cast_in, cast_out):
        dst[...] = src[...].astype(BF16)

    @pl.when(i == 0)
    def _():
        ki = tailk_ref[:, :IDX_DIM].astype(BF16)
        z = jnp.zeros_like(ki)
        kil_scr[...] = jnp.concatenate([ki, z], axis=-1)
        kir_scr[...] = jnp.concatenate([z, ki], axis=-1)

    wi = tailq_ref[:, IDX_DIM:IDX_DIM + IDX_HEADS] * (IDX_DIM ** -0.5)
    nk = (i + 2) // 2
    qpos = i * Q_BLOCK + lax.broadcasted_iota(I32, (Q_BLOCK, 1), 0)
    kloc = lax.broadcasted_iota(I32, (1, KEY_CHUNK), 1)
    heads_per_ref = qia_ref.shape[1] // IDX_DIM

    pairs = []
    for hp in range(IDX_HEADS // 2):
        ref = qia_ref if 2 * hp < heads_per_ref else qib_ref
        base = (2 * hp) % heads_per_ref * IDX_DIM
        pairs.append(ref[:, base:base + 2 * IDX_DIM])
    qstack = jnp.concatenate(pairs, axis=0)

    def score_chunk(c, carry):
        off = pl.multiple_of(c * KEY_CHUNK, KEY_CHUNK)
        acc = jnp.zeros((Q_BLOCK, KEY_CHUNK), F32)
        for side, k_scr in ((0, kil_scr), (1, kir_scr)):
            s = _dot_nt(qstack, k_scr[pl.ds(off, KEY_CHUNK), :])
            for hp in range(IDX_HEADS // 2):
                h = 2 * hp + side
                acc = acc + wi[:, h:h + 1] * jnp.maximum(s[hp * Q_BLOCK:(hp + 1) * Q_BLOCK], 0.0)
        kpos = c * KEY_CHUNK + kloc
        sc_scr[c] = jnp.where(kpos <= qpos, acc, -jnp.inf)
        return carry

    _chunk_loop(nk, score_chunk, 0)

    def fold_sum(pred):
        def body(c, cnt):
            m = jnp.where(pred(sc_scr[c], c * KEY_CHUNK + kloc), 1.0, 0.0)
            return cnt + m[:, :LANES] + m[:, LANES:]
        cnt = _chunk_loop(nk, body, jnp.zeros((Q_BLOCK, LANES), F32))
        return jnp.sum(cnt, axis=-1, keepdims=True)

    def fold_min(val):
        def body(c, acc):
            x = val(sc_scr[c], c * KEY_CHUNK + kloc)
            return jnp.minimum(acc, jnp.minimum(x[:, :LANES], x[:, LANES:]))
        acc = lax.fori_loop(0, nk, body, jnp.full((Q_BLOCK, LANES), jnp.inf, F32))
        return jnp.min(acc, axis=-1, keepdims=True)

    index_bits = (kv_ref.shape[0] - 1).bit_length()
    sel_v, sel_m = _topk_select(fold_sum, fold_min, (Q_BLOCK, 1), topk, index_bits)

    def mask_chunk(c, carry):
        kpos = c * KEY_CHUNK + kloc
        sc = sc_scr[c]
        chosen = (sc > sel_v) | ((sc == sel_v) & (kpos <= sel_m))
        sc_scr[c] = jnp.where(chosen & (kpos <= qpos), 0.0, -jnp.inf)
        return carry

    lax.fori_loop(0, nk, mask_chunk, 0)

    scale = B_HEAD_DIM ** -0.5
    rows = B_GQA * Q_BLOCK
    qgs = [jnp.concatenate(
        [q_ref[:, (g * B_GQA + hh) * B_HEAD_DIM:(g * B_GQA + hh + 1) * B_HEAD_DIM] for hh in range(B_GQA)],
        axis=0) for g in range(B_KV_HEADS)]
    m_scr[...] = jnp.full(m_scr.shape, -jnp.inf, F32)
    l_scr[...] = jnp.zeros(l_scr.shape, F32)
    acc_scr[...] = jnp.zeros(acc_scr.shape, F32)

    def logits_chunk(c, carry):
        off = pl.multiple_of(c * KEY_CHUNK, KEY_CHUNK)
        var = jnp.clip(2 * c - i, -(N_PAIR_VARIANTS - 1), 0) + (N_PAIR_VARIANTS - 1)
        sel = sc_scr[c]
        sel = jnp.concatenate([sel] * B_GQA, axis=0)
        for g in range(B_KV_HEADS):
            kc = kv_ref[pl.ds(off, KEY_CHUNK), g * B_HEAD_DIM:(g + 1) * B_HEAD_DIM]
            lg = _dot_nt(qgs[g], kc) * scale + bias_ref[var, g] + sel
            lg_scr[c, g] = lg
            m_scr[g] = jnp.maximum(m_scr[g], jnp.maximum(lg[:, :LANES], lg[:, LANES:]))
        return carry

    _chunk_loop(nk, logits_chunk, 0)
    for g in range(B_KV_HEADS):
        m = jnp.max(m_scr[g], axis=-1, keepdims=True)
        m_scr[g] = jnp.broadcast_to(m, (rows, LANES))

    def pv_chunk(c, carry):
        off = pl.multiple_of(c * KEY_CHUNK, KEY_CHUNK)
        for g in range(B_KV_HEADS):
            vc = kv_ref[pl.ds(off, KEY_CHUNK), KV_WIDTH + g * B_HEAD_DIM:KV_WIDTH + (g + 1) * B_HEAD_DIM]
            mb = m_scr[g]
            p = jnp.exp(lg_scr[c, g] - jnp.concatenate([mb, mb], axis=-1))
            l_scr[g] += p[:, :LANES] + p[:, LANES:]
            acc_scr[g] += jnp.dot(p.astype(BF16), vc, preferred_element_type=F32)
        return carry

    _chunk_loop(nk, pv_chunk, 0)
    for g in range(B_KV_HEADS):
        o = acc_scr[g] / jnp.sum(l_scr[g], axis=-1, keepdims=True)
        for hh in range(B_GQA):
            h = g * B_GQA + hh
            o_ref[:, h * B_HEAD_DIM:(h + 1) * B_HEAD_DIM] = o[hh * Q_BLOCK:(hh + 1) * Q_BLOCK].astype(BF16)


def _dsa_prompt(p, tail, bias_pairs, cast_weights, *, batch, seq):
    topk = min(TOPK_MAX, seq // 4)
    nblk = seq // Q_BLOCK
    qi_w = PROJ_TN
    steps = batch * nblk
    cast_specs, cast_shapes = [], []
    for w in cast_weights:
        ne, r, c = w.shape
        parts = steps // ne
        assert steps == ne * parts and r % (parts * BF16_SUBLANES) == 0
        spec = pl.BlockSpec((None, r // parts, c),
                            lambda b, i, parts=parts: ((b * nblk + i) // parts, (b * nblk + i) % parts, 0))
        cast_specs.append(spec)
        cast_shapes.append(jax.ShapeDtypeStruct(w.shape, BF16))
    outs = pl.pallas_call(
        functools.partial(_dsa_prompt_kernel, topk=topk, n_cast=len(cast_weights)),
        grid=(batch, nblk),
        in_specs=[
            pl.BlockSpec((Q_BLOCK, B_WIDTH), lambda b, i: (b * nblk + i, COL_Q // B_WIDTH)),
            pl.BlockSpec((seq, 2 * KV_WIDTH), lambda b, i: (b, COL_K // (2 * KV_WIDTH))),
            pl.BlockSpec((Q_BLOCK, qi_w), lambda b, i: (b * nblk + i, COL_QI // qi_w)),
            pl.BlockSpec((Q_BLOCK, qi_w), lambda b, i: (b * nblk + i, COL_QI // qi_w + 1)),
            pl.BlockSpec((seq, LANES), lambda b, i: (b, 0)),
            pl.BlockSpec((Q_BLOCK, LANES), lambda b, i: (b * nblk + i, 0)),
            pl.BlockSpec(bias_pairs.shape, lambda b, i: (0, 0, 0, 0)),
        ] + cast_specs,
        out_specs=[pl.BlockSpec((Q_BLOCK, B_WIDTH), lambda b, i: (b * nblk + i, 0))] + cast_specs,
        out_shape=[jax.ShapeDtypeStruct((batch * seq, B_WIDTH), BF16)] + cast_shapes,
        scratch_shapes=[
            pltpu.VMEM((seq, 2 * IDX_DIM), BF16),
            pltpu.VMEM((seq, 2 * IDX_DIM), BF16),
            pltpu.VMEM((seq // KEY_CHUNK, Q_BLOCK, KEY_CHUNK), F32),
            pltpu.VMEM((seq // KEY_CHUNK, B_KV_HEADS, B_GQA * Q_BLOCK, KEY_CHUNK), F32),
            pltpu.VMEM((B_KV_HEADS, B_GQA * Q_BLOCK, LANES), F32),
            pltpu.VMEM((B_KV_HEADS, B_GQA * Q_BLOCK, LANES), F32),
            pltpu.VMEM((B_KV_HEADS, B_GQA * Q_BLOCK, B_HEAD_DIM), F32),
        ],
        compiler_params=_cparams(("arbitrary", "arbitrary")),
        name="dsa_prompt",
    )(p, p, p, p, tail, tail, bias_pairs, *cast_weights)
    return outs[0], tuple(outs[1:])


PAGES_PER_STEP = 64
SAMPLE_SUB_PAGES = 32
SCORE_KEYS_PER_DOT = 2048
FOLD_VREGS = 16
SELECT_ROWS = 64


def _fetch_pages(pt_ref, sources, bufs, sem, step, n_steps_total, steps_per_sample, pages_per_step):
    slot = step % 2

    def copies(n, slot_, page_of):
        sample = n // steps_per_sample
        first = (n % steps_per_sample) * pages_per_step
        for r in range(pages_per_step):
            page = page_of(sample, first + r)
            for a, (src, buf) in enumerate(zip(sources, bufs)):
                yield pltpu.make_async_copy(src.at[page], buf.at[slot_, r], sem.at[a, slot_])

    def table(sample, idx):
        return pt_ref[sample, idx]

    @pl.when(step == 0)
    def _():
        for cp in copies(step, slot, table):
            cp.start()

    @pl.when(step + 1 < n_steps_total)
    def _():
        for cp in copies(step + 1, 1 - slot, table):
            cp.start()

    for cp in copies(step, slot, lambda sample, idx: 0):
        cp.wait()
    return slot


def _dsa_sample_score_kernel(pt_ref, idx_hbm, qi_ref, wi_ref, kin_ref, sc_ref, scn_ref,
                             page_buf, sem, *, dec_seq, n_pages, n_samples):
    slot = _fetch_pages(pt_ref, [idx_hbm], [page_buf], sem, pl.program_id(0), n_samples, 1, n_pages)
    pages = [page_buf.at[slot, r] for r in range(n_pages)]
    pages_per_dot = SCORE_KEYS_PER_DOT // PAGE_SIZE
    qi = qi_ref[...]
    wi = wi_ref[...] * (IDX_DIM ** -0.5)

    def scores(keys_t):
        r = jnp.maximum(jnp.dot(qi, keys_t, preferred_element_type=F32), 0.0) * wi
        acc = r[0:SUBLANES]
        for j in range(1, IDX_HEADS * dec_seq // SUBLANES):
            acc = acc + r[j * SUBLANES:(j + 1) * SUBLANES]
        shift = dec_seq
        while shift < SUBLANES:
            acc = acc + pltpu.roll(acc, shift, axis=0)
            shift *= 2
        return acc

    for d in range(n_pages // pages_per_dot):
        keys_t = jnp.concatenate(
            [pg[...].astype(BF16) for pg in pages[d * pages_per_dot:(d + 1) * pages_per_dot]], axis=1)
        sc_ref[:, d * SCORE_KEYS_PER_DOT:(d + 1) * SCORE_KEYS_PER_DOT] = scores(keys_t)

    row = lax.broadcasted_iota(I32, (SUBLANES, LANES), 0)
    col = lax.broadcasted_iota(I32, (SUBLANES, LANES), 1)
    scn_ref[...] = jnp.where((col <= row % dec_seq) & (col < dec_seq), scores(kin_ref[...]), -jnp.inf)


def _sample_select_kernel(sc_ref, scn_ref, selv_ref, selm_ref, *, topk):
    rows, past = sc_ref.shape
    col = lax.broadcasted_iota(I32, (1, LANES), 1)

    def pieces():
        yield scn_ref[...], past + col
        for w in range(past // LANES):
            yield sc_ref[:, w * LANES:(w + 1) * LANES], w * LANES + col

    ways = max(1, FOLD_VREGS // (rows // SUBLANES))

    def fold(term, combine):
        parts = [None] * ways
        for n, (sc, kp) in enumerate(pieces()):
            x = term(sc, kp)
            parts[n % ways] = x if parts[n % ways] is None else combine(parts[n % ways], x)
        parts = [p for p in parts if p is not None]
        while len(parts) > 1:
            parts = [combine(parts[j], parts[j + 1]) if j + 1 < len(parts) else parts[j]
                     for j in range(0, len(parts), 2)]
        return parts[0]

    def fold_sum(pred):
        cnt = fold(lambda sc, kp: jnp.where(pred(sc, kp), 1.0, 0.0), jnp.add)
        return jnp.sum(cnt, axis=-1, keepdims=True)

    def fold_min(val):
        return jnp.min(fold(val, jnp.minimum), axis=-1, keepdims=True)

    index_bits = (past + LANES - 1).bit_length()
    sel_v, sel_m = _topk_select(fold_sum, fold_min, (rows, 1), topk, index_bits)
    selv_ref[...] = jnp.broadcast_to(sel_v, (rows, LANES))
    selm_ref[...] = jnp.broadcast_to(sel_m, (rows, LANES))


def _dsa_sample_attend_kernel(pt_ref, ck_hbm, cv_hbm, q_ref, sc_ref, scn_ref, selv_ref, selm_ref, kvn_ref,
                              sbias_ref, sconst_ref, o_ref, kbuf, vbuf, sem, m_scr, l_scr, acc_scr,
                              *, dec_seq, n_steps, n_samples):
    s = pl.program_id(1)
    slot = _fetch_pages(pt_ref, [ck_hbm, cv_hbm], [kbuf, vbuf], sem, pl.program_id(0) * n_steps + s,
                        n_samples * n_steps, n_steps, PAGES_PER_STEP)
    kpages = [kbuf.at[slot, r] for r in range(PAGES_PER_STEP)]
    vpages = [vbuf.at[slot, r] for r in range(PAGES_PER_STEP)]
    chunk = PAGES_PER_STEP * PAGE_SIZE
    rows = B_GQA * SUBLANES
    scale = B_HEAD_DIM ** -0.5

    @pl.when(s == 0)
    def _():
        m_scr[...] = jnp.full(m_scr.shape, NEG_INIT, F32)
        l_scr[...] = jnp.zeros(l_scr.shape, F32)
        acc_scr[...] = jnp.zeros(acc_scr.shape, F32)

    sel_v = selv_ref[:, 0:1]
    sel_m = selm_ref[:, 0:1]

    def chosen(sc, kpos):
        return jnp.where((sc > sel_v) | ((sc == sel_v) & (kpos <= sel_m)), 0.0, -jnp.inf)

    def head_rows(pgs, g):
        return jnp.concatenate(
            [pg[pl.ds(g, PAGE_SIZE, stride=B_KV_HEADS), :].astype(BF16) for pg in pgs], axis=0)

    kcol = lax.broadcasted_iota(I32, (SUBLANES, chunk), 1)
    sel = chosen(sc_ref[...], s * chunk + kcol)
    sel = jnp.concatenate([sel] * B_GQA, axis=0)
    is_last = s == n_steps - 1
    ncol = lax.broadcasted_iota(I32, (SUBLANES, LANES), 1)
    seln = chosen(scn_ref[...], n_steps * chunk + ncol)
    seln = jnp.where(is_last & (scn_ref[...] > -jnp.inf), seln, -jnp.inf)
    seln = jnp.concatenate([seln] * B_GQA, axis=0)

    def partial(lg, v):
        m = jnp.maximum(jnp.max(lg, axis=-1, keepdims=True), NEG_INIT)
        p = jnp.exp(lg - m)
        return m, jnp.sum(p, axis=-1, keepdims=True), jnp.dot(p.astype(BF16), v, preferred_element_type=F32)

    sub = SAMPLE_SUB_PAGES * PAGE_SIZE
    nsub = PAGES_PER_STEP // SAMPLE_SUB_PAGES
    lane_sub = lax.broadcasted_iota(I32, (rows, sub), 1)
    for g in range(B_KV_HEADS):
        qg = q_ref[g]
        far = sconst_ref[g, :, 0:1]
        near = jnp.concatenate([jnp.zeros((rows, sub - PAGE_SIZE), F32), sbias_ref[g, :, :PAGE_SIZE]], axis=-1)
        parts = [(m_scr[g], l_scr[g], acc_scr[g])]
        for j in range(nsub):
            pj = slice(j * SAMPLE_SUB_PAGES, (j + 1) * SAMPLE_SUB_PAGES)
            lg = _dot_nt(qg, head_rows(kpages[pj], g)) * scale + sel[:, j * sub:(j + 1) * sub]
            if j == nsub - 1:
                lg = lg + jnp.where(is_last & (lane_sub >= sub - PAGE_SIZE), near, far)
            else:
                lg = lg + far
            parts.append(partial(lg, head_rows(vpages[pj], g)))
        kn = kvn_ref[:, g * B_HEAD_DIM:(g + 1) * B_HEAD_DIM]
        vn = kvn_ref[:, KV_WIDTH + g * B_HEAD_DIM:KV_WIDTH + (g + 1) * B_HEAD_DIM]
        parts.append(partial(_dot_nt(qg, kn) * scale + sbias_ref[g, :, PAGE_SIZE:] + seln, vn))
        m_new = parts[0][0]
        for m, _, _ in parts[1:]:
            m_new = jnp.maximum(m_new, m)
        l_new = jnp.zeros((rows, 1), F32)
        acc_new = jnp.zeros((rows, B_HEAD_DIM), F32)
        for m, l, acc in parts:
            w = jnp.exp(m - m_new)
            l_new = l_new + w * l
            acc_new = acc_new + w * acc
        m_scr[g] = m_new
        l_scr[g] = l_new
        acc_scr[g] = acc_new

    @pl.when(is_last)
    def _():
        for g in range(B_KV_HEADS):
            o_ref[g] = (acc_scr[g] / l_scr[g]).astype(BF16)


def _dsa_sample(p_s, tail_s, k_s, v_s, cache_k_l, cache_v_l, cache_idx_l, page_table, sbias, sconst,
                *, dec_batch, dec_seq):
    n_pages = page_table.shape[1]
    past = n_pages * PAGE_SIZE
    topk = min(TOPK_MAX, (past + dec_seq) // 4)
    n_steps = n_pages // PAGES_PER_STEP
    chunk = PAGES_PER_STEP * PAGE_SIZE
    rows = B_GQA * SUBLANES
    n_pool = cache_k_l.shape[0]

    assert SUBLANES % dec_seq == 0
    qi = p_s[:, COL_QI:COL_MAIN].reshape(dec_batch, dec_seq, IDX_HEADS, IDX_DIM).transpose(0, 2, 1, 3)
    qi = qi.reshape(dec_batch, IDX_HEADS * dec_seq, IDX_DIM)
    wi = tail_s[:, IDX_DIM:IDX_DIM + IDX_HEADS].reshape(dec_batch, dec_seq, IDX_HEADS).transpose(0, 2, 1)
    wi = wi.reshape(dec_batch, IDX_HEADS * dec_seq, 1)
    ki_new = tail_s[:, :IDX_DIM].astype(BF16).reshape(dec_batch, dec_seq, IDX_DIM)
    ki_new = jnp.pad(ki_new, ((0, 0), (0, LANES - dec_seq), (0, 0))).transpose(0, 2, 1)
    q = p_s[:, COL_Q:COL_K].reshape(dec_batch, dec_seq, B_KV_HEADS, B_GQA, B_HEAD_DIM)
    q = jnp.pad(q.transpose(0, 2, 3, 1, 4), ((0, 0), (0, 0), (0, 0), (0, SUBLANES - dec_seq), (0, 0)))
    q = q.reshape(dec_batch, B_KV_HEADS, rows, B_HEAD_DIM)
    kv_new = jnp.concatenate([k_s, v_s], axis=-1).astype(BF16).reshape(dec_batch, dec_seq, 2 * KV_WIDTH)
    kv_new = jnp.pad(kv_new, ((0, 0), (0, LANES - dec_seq), (0, 0)))

    cache_idx_t = jnp.swapaxes(cache_idx_l, 1, 2)
    stat_spec = pl.BlockSpec((None, SUBLANES, LANES), lambda b, pt: (b, 0, 0))
    sc, scn = pl.pallas_call(
        functools.partial(_dsa_sample_score_kernel, dec_seq=dec_seq, n_pages=n_pages, n_samples=dec_batch),
        grid_spec=pltpu.PrefetchScalarGridSpec(
            num_scalar_prefetch=1,
            grid=(dec_batch,),
            in_specs=[
                pl.BlockSpec(memory_space=pl.ANY),
                pl.BlockSpec((None, IDX_HEADS * dec_seq, IDX_DIM), lambda b, pt: (b, 0, 0)),
                pl.BlockSpec((None, IDX_HEADS * dec_seq, 1), lambda b, pt: (b, 0, 0)),
                pl.BlockSpec((None, IDX_DIM, LANES), lambda b, pt: (b, 0, 0)),
            ],
            out_specs=[pl.BlockSpec((None, SUBLANES, past), lambda b, pt: (b, 0, 0)), stat_spec],
            scratch_shapes=[
                pltpu.VMEM((2, n_pages, IDX_DIM, PAGE_SIZE), F32),
                pltpu.SemaphoreType.DMA((1, 2)),
            ],
        ),
        out_shape=[
            jax.ShapeDtypeStruct((dec_batch, SUBLANES, past), F32),
            jax.ShapeDtypeStruct((dec_batch, SUBLANES, LANES), F32),
        ],
        compiler_params=_cparams(("arbitrary",)),
        name="dsa_sample_score",
    )(page_table, cache_idx_t, qi, wi, ki_new)

    n_rows = dec_batch * SUBLANES
    sel_rows = math.gcd(n_rows, SELECT_ROWS)
    row_spec = pl.BlockSpec((sel_rows, LANES), lambda i: (i, 0))
    selv, selm = pl.pallas_call(
        functools.partial(_sample_select_kernel, topk=topk),
        grid=(n_rows // sel_rows,),
        in_specs=[pl.BlockSpec((sel_rows, past), lambda i: (i, 0)), row_spec],
        out_specs=[row_spec, row_spec],
        out_shape=[jax.ShapeDtypeStruct((n_rows, LANES), F32), jax.ShapeDtypeStruct((n_rows, LANES), I32)],
        compiler_params=_cparams(("arbitrary",)),
        name="sample_select",
    )(sc.reshape(n_rows, past), scn.reshape(n_rows, LANES))
    selv = selv.reshape(dec_batch, SUBLANES, LANES)
    selm = selm.reshape(dec_batch, SUBLANES, LANES)

    ck = cache_k_l.reshape(n_pool, PAGE_SIZE * B_KV_HEADS, B_HEAD_DIM)
    cv = cache_v_l.reshape(n_pool, PAGE_SIZE * B_KV_HEADS, B_HEAD_DIM)
    kv_buf = pltpu.VMEM((2, PAGES_PER_STEP, PAGE_SIZE * B_KV_HEADS, B_HEAD_DIM), F32)
    o = pl.pallas_call(
        functools.partial(_dsa_sample_attend_kernel, dec_seq=dec_seq, n_steps=n_steps, n_samples=dec_batch),
        grid_spec=pltpu.PrefetchScalarGridSpec(
            num_scalar_prefetch=1,
            grid=(dec_batch, n_steps),
            in_specs=[pl.BlockSpec(memory_space=pl.ANY), pl.BlockSpec(memory_space=pl.ANY)] + [
                pl.BlockSpec((None, B_KV_HEADS, rows, B_HEAD_DIM), lambda b, s, pt: (b, 0, 0, 0)),
                pl.BlockSpec((None, SUBLANES, chunk), lambda b, s, pt: (b, 0, s)),
                pl.BlockSpec((None, SUBLANES, LANES), lambda b, s, pt: (b, 0, 0)),
                pl.BlockSpec((None, SUBLANES, LANES), lambda b, s, pt: (b, 0, 0)),
                pl.BlockSpec((None, SUBLANES, LANES), lambda b, s, pt: (b, 0, 0)),
                pl.BlockSpec((None, LANES, 2 * KV_WIDTH), lambda b, s, pt: (b, 0, 0)),
                pl.BlockSpec(sbias.shape, lambda b, s, pt: (0, 0, 0)),
                pl.BlockSpec(sconst.shape, lambda b, s, pt: (0, 0, 0)),
            ],
            out_specs=pl.BlockSpec((None, B_KV_HEADS, rows, B_HEAD_DIM), lambda b, s, pt: (b, 0, 0, 0)),
            scratch_shapes=[
                kv_buf,
                kv_buf,
                pltpu.SemaphoreType.DMA((2, 2)),
                pltpu.VMEM((B_KV_HEADS, rows, 1), F32),
                pltpu.VMEM((B_KV_HEADS, rows, 1), F32),
                pltpu.VMEM((B_KV_HEADS, rows, B_HEAD_DIM), F32),
            ],
        ),
        out_shape=jax.ShapeDtypeStruct((dec_batch, B_KV_HEADS, rows, B_HEAD_DIM), BF16),
        compiler_params=_cparams(("arbitrary", "arbitrary")),
        name="dsa_sample_attend",
    )(page_table, ck, cv, q, sc, scn, selv, selm, kv_new, sbias, sconst)
    o = o.reshape(dec_batch, B_KV_HEADS, B_GQA, SUBLANES, B_HEAD_DIM)[:, :, :, :dec_seq].transpose(0, 3, 1, 2, 4)
    return o.reshape(dec_batch * dec_seq, B_WIDTH)


def _mixout_kernel(u_ref, vn_ref, b_ref, ws_ref, ab_ref, x_ref, w_ref, o_ref, mix_scr, *, tm):
    j = pl.program_id(1)

    @pl.when(j == 0)
    def _():
        row = lax.broadcasted_iota(I32, (A_CHUNK, A_CHUNK), 0)
        col = lax.broadcasted_iota(I32, (A_CHUNK, A_CHUNK), 1)
        for g in range(A_GROUPS):
            wsg = jnp.where(row >= col, ws_ref[g], 0.0).astype(BF16)
            bg = ab_ref[:, g:g + 1]
            cs = slice(g * A_CH, (g + 1) * A_CH)
            for r in range(tm // A_CHUNK):
                rs = slice(r * A_CHUNK, (r + 1) * A_CHUNK)
                sm = jnp.dot(wsg, vn_ref[rs, cs], preferred_element_type=F32) + bg
                mix_scr[rs, cs] = (u_ref[rs, cs].astype(F32) * sm).astype(BF16)
        mix_scr[:, A_WIDTH:] = b_ref[...]

    o_ref[...] = x_ref[...] + jnp.dot(mix_scr[...], w_ref[...].astype(BF16), preferred_element_type=F32)


def _mixout(p, b_out, ws_eff, ab_eff, x2d, w_out_l, *, tm, tn=D_MODEL):
    n = x2d.shape[0]
    return pl.pallas_call(
        functools.partial(_mixout_kernel, tm=tm),
        grid=(n // tm, D_MODEL // tn),
        in_specs=[
            pl.BlockSpec((tm, A_WIDTH), lambda i, j: (i, COL_U // A_WIDTH)),
            pl.BlockSpec((tm, A_WIDTH), lambda i, j: (i, COL_V // A_WIDTH)),
            pl.BlockSpec((tm, B_WIDTH), lambda i, j: (i, 0)),
            pl.BlockSpec((A_GROUPS, A_CHUNK, A_CHUNK), lambda i, j: (0, 0, 0)),
            pl.BlockSpec((A_CHUNK, A_GROUPS), lambda i, j: (0, 0)),
            pl.BlockSpec((tm, tn), lambda i, j: (i, j)),
            pl.BlockSpec((A_WIDTH + B_WIDTH, tn), lambda i, j: (0, j)),
        ],
        out_specs=pl.BlockSpec((tm, tn), lambda i, j: (i, j)),
        out_shape=jax.ShapeDtypeStruct((n, D_MODEL), F32),
        scratch_shapes=[pltpu.VMEM((tm, A_WIDTH + B_WIDTH), BF16)],
        compiler_params=_cparams(("arbitrary", "arbitrary")),
        name="mixout",
    )(p, p, b_out, ws_eff, ab_eff, x2d, w_out_l)


def _memkv_kernel(x_ref, wk_ref, wv_ref, kn_ref, mk_ref, mv_ref):
    x = x_ref[...].astype(BF16)
    tm = x.shape[0]
    mk = _head_norm(jnp.dot(x, wk_ref[...].astype(BF16), preferred_element_type=F32), kn_ref[...])
    mv = jnp.dot(x, wv_ref[...].astype(BF16), preferred_element_type=F32)
    for hd in range(MEM_HEADS):
        cs = slice(hd * MEM_HEAD_DIM, (hd + 1) * MEM_HEAD_DIM)
        mk_ref[pl.ds(hd, tm, stride=MEM_HEADS), :] = mk[:, cs]
        mv_ref[pl.ds(hd, tm, stride=MEM_HEADS), :] = mv[:, cs]


def _memkv(mem2d, wk, wv, knorm, *, tm=256):
    n = mem2d.shape[0]
    kn = jnp.tile(knorm, MEM_HEADS).reshape(1, MEM_WIDTH)
    return pl.pallas_call(
        _memkv_kernel,
        grid=(n // tm,),
        in_specs=[
            pl.BlockSpec((tm, D_MODEL), lambda i: (i, 0)),
            pl.BlockSpec((D_MODEL, MEM_WIDTH), lambda i: (0, 0)),
            pl.BlockSpec((D_MODEL, MEM_WIDTH), lambda i: (0, 0)),
            pl.BlockSpec((1, MEM_WIDTH), lambda i: (0, 0)),
        ],
        out_specs=[pl.BlockSpec((tm * MEM_HEADS, MEM_HEAD_DIM), lambda i: (i, 0))] * 2,
        out_shape=[jax.ShapeDtypeStruct((n * MEM_HEADS, MEM_HEAD_DIM), F32)] * 2,
        compiler_params=_cparams(("arbitrary",)),
        name="memkv",
    )(mem2d, wk, wv, kn)


def _memattn_kernel(h_ref, g_ref, wq_ref, qn_ref, mk_ref, mv_ref, wo_ref, gf_ref, wc_ref, bc_ref,
                    o_ref, n_ref, c_ref, *, rows_per_mem, mem_len):
    tm, m = h_ref.shape[0], mk_ref.shape[0] // MEM_HEADS
    wq = wq_ref[...].astype(BF16)
    wo = wo_ref[...].astype(BF16)
    mks = [mk_ref[pl.ds(hd, m, stride=MEM_HEADS), :].astype(BF16) for hd in range(MEM_HEADS)]
    mvs = [mv_ref[pl.ds(hd, m, stride=MEM_HEADS), :].astype(BF16) for hd in range(MEM_HEADS)]
    rc = min(tm, MEMATTN_ROW_CHUNK)
    for r in range(tm // rc):
        rs = slice(r * rc, (r + 1) * rc)
        h = h_ref[rs, :]
        n = _row_norm(h, g_ref[...]).astype(BF16)
        q = _head_norm(jnp.dot(n, wq, preferred_element_type=F32), qn_ref[...]).astype(BF16)
        if rows_per_mem is not None:
            row = r * rc + lax.broadcasted_iota(I32, (rc, m), 0)
            col = lax.broadcasted_iota(I32, (rc, m), 1)
            mask = jnp.where(row // rows_per_mem == col // mem_len, 0.0, -jnp.inf)
        outs = []
        for hd in range(MEM_HEADS):
            cs = slice(hd * MEM_HEAD_DIM, (hd + 1) * MEM_HEAD_DIM)
            lg = _dot_nt(q[:, cs], mks[hd]) * (MEM_HEAD_DIM ** -0.5)
            if rows_per_mem is not None:
                lg = lg + mask
            e = jnp.exp(lg - jnp.max(lg, axis=-1, keepdims=True))
            l = jnp.sum(e, axis=-1, keepdims=True)
            outs.append(jnp.dot(e.astype(BF16), mvs[hd], preferred_element_type=F32) / l)
        o = jnp.concatenate(outs, axis=-1).astype(BF16)
        h2 = h + jnp.dot(o, wo, preferred_element_type=F32)
        o_ref[rs, :] = h2
        n_ref[rs, :], c_ref[rs, :] = _route(h2, gf_ref[...], wc_ref[...], bc_ref[...])


def _memattn(h2d, gain, wq, qnorm, mk, mv, wo, gain_ffn, w_cat, b_cat, *, tm, mem_rows, tiles_per_mem,
             rows_per_mem, mem_len):
    n = h2d.shape[0]
    qn = jnp.tile(qnorm, MEM_HEADS).reshape(1, MEM_WIDTH)
    return pl.pallas_call(
        functools.partial(_memattn_kernel, rows_per_mem=rows_per_mem, mem_len=mem_len),
        grid=(n // tm,),
        in_specs=[
            pl.BlockSpec((tm, D_MODEL), lambda i: (i, 0)),
            pl.BlockSpec((1, D_MODEL), lambda i: (0, 0)),
            pl.BlockSpec((D_MODEL, MEM_WIDTH), lambda i: (0, 0)),
            pl.BlockSpec((1, MEM_WIDTH), lambda i: (0, 0)),
            pl.BlockSpec((mem_rows * MEM_HEADS, MEM_HEAD_DIM), lambda i: (i // tiles_per_mem, 0)),
            pl.BlockSpec((mem_rows * MEM_HEADS, MEM_HEAD_DIM), lambda i: (i // tiles_per_mem, 0)),
            pl.BlockSpec((MEM_WIDTH, D_MODEL), lambda i: (0, 0)),
            pl.BlockSpec((1, D_MODEL), lambda i: (0, 0)),
            pl.BlockSpec((D_MODEL, 2 * LANES), lambda i: (0, 0)),
            pl.BlockSpec((1, 2 * LANES), lambda i: (0, 0)),
        ],
        out_specs=[pl.BlockSpec((tm, D_MODEL), lambda i: (i, 0)), pl.BlockSpec((tm, D_MODEL), lambda i: (i, 0)),
                   pl.BlockSpec((tm, LANES), lambda i: (i, 0))],
        out_shape=[jax.ShapeDtypeStruct((n, D_MODEL), F32), jax.ShapeDtypeStruct((n, D_MODEL), BF16),
                   jax.ShapeDtypeStruct((n, LANES), F32)],
        compiler_params=_cparams(("arbitrary",)),
        name="memattn",
    )(h2d, gain.reshape(1, D_MODEL), wq, qn, mk, mv, wo, gain_ffn.reshape(1, D_MODEL), w_cat, b_cat)


def _route(h, gain, w_cat, b_cat):
    n = _row_norm(h, gain).astype(BF16)
    lg = jnp.dot(n, w_cat.astype(BF16), preferred_element_type=F32) + b_cat
    gl, el = lg[:, :LANES], lg[:, LANES:]
    lane = lax.broadcasted_iota(I32, gl.shape, 1)
    lanef = lane.astype(F32)

    def first_max_lane(v, vmax):
        return jnp.min(jnp.where(v == vmax, lanef, float(LANES)), axis=-1, keepdims=True).astype(I32)

    gl = jnp.where(lane < N_GROUPS, gl, -jnp.inf)
    ge = jnp.exp(gl - jnp.max(gl, axis=-1, keepdims=True))
    gp = ge / jnp.sum(ge, axis=-1, keepdims=True)
    g_gate = jnp.max(gp, axis=-1, keepdims=True)
    g_sel = first_max_lane(gp, g_gate)
    in_grp = (lane // EXPERTS_PER_GROUP == g_sel) & (lane < N_EXPERTS)
    el = jnp.where(in_grp, el, -jnp.inf)
    ee = jnp.exp(el - jnp.max(el, axis=-1, keepdims=True))
    ep = jnp.where(in_grp, ee / jnp.sum(ee, axis=-1, keepdims=True), -jnp.inf)
    w1 = jnp.max(ep, axis=-1, keepdims=True)
    i1 = first_max_lane(ep, w1)
    ep2 = jnp.where(lane == i1, -jnp.inf, ep)
    w2 = jnp.max(ep2, axis=-1, keepdims=True)
    i2 = first_max_lane(ep2, w2)
    tot = w1 + w2
    comb = jnp.where(lane == i1, w1 / tot * g_gate, jnp.where(lane == i2, w2 / tot * g_gate, 0.0))
    return n, jnp.where(lane == GSEL_LANE, g_sel.astype(F32), comb)


def _cast_kernel(x_ref, o_ref):
    o_ref[...] = x_ref[...].astype(BF16)


def _cast_bf16(w):
    ne, r, c = w.shape
    return pl.pallas_call(
        _cast_kernel,
        grid=(ne,),
        in_specs=[pl.BlockSpec((None, r, c), lambda e: (e, 0, 0))],
        out_specs=pl.BlockSpec((None, r, c), lambda e: (e, 0, 0)),
        out_shape=jax.ShapeDtypeStruct(w.shape, BF16),
        compiler_params=_cparams(("arbitrary",)),
        name="cast_bf16",
    )(w)


GSEL_LANE = N_EXPERTS
MOE_ROWS = 128
MOE_MAIN_SLACK = 0
MOE_FF_SPLIT = 1


def _moe_routed_kernel(n_ref, c_ref, h_hbm, wg_ref, wu_ref, wd_ref, o_ref,
                       tri_scr, key_scr, xc_scr, yc_scr, cw_scr, nch_ref, sem, *, tm):
    i, g, e, f = (pl.program_id(k) for k in range(4))
    main_rows = tm // N_GROUPS + MOE_MAIN_SLACK
    first_e = (e == 0) & (f == 0)
    last_e = (e == EXPERTS_PER_GROUP - 1) & (f == MOE_FF_SPLIT - 1)

    def onehot_rows(start, rows):
        want = (start + lax.broadcasted_iota(I32, (rows, 1), 0)).astype(F32)
        return jnp.where(key_scr[pl.ds(g, 1), :] == want, 1.0, 0.0).astype(BF16)

    def tail_start(c):
        return pl.multiple_of(main_rows + c * MOE_ROWS, math.gcd(main_rows, MOE_ROWS))

    @pl.when((i == 0) & (g == 0) & first_e)
    def _():
        r = lax.broadcasted_iota(I32, (tm, tm), 0)
        c = lax.broadcasted_iota(I32, (tm, tm), 1)
        tri_scr[...] = jnp.where(r < c, 1.0, 0.0).astype(BF16)

    def residual_copy():
        return pltpu.make_async_copy(h_hbm.at[pl.ds(pl.multiple_of(i * tm, tm), tm)], o_ref, sem)

    @pl.when((g == 0) & first_e)
    def _():
        residual_copy().start()
        lane = lax.broadcasted_iota(I32, (tm, LANES), 1).astype(F32)
        onehot = jnp.where(lane == c_ref[:, GSEL_LANE:GSEL_LANE + 1], 1.0, 0.0).astype(BF16)
        er = lax.broadcasted_iota(I32, (SUBLANES, LANES), 0)
        ec = lax.broadcasted_iota(I32, (SUBLANES, LANES), 1)
        eye = jnp.where(er == ec, 1.0, 0.0).astype(BF16)
        mask_t = _dot_nt(eye, onehot)
        rank_t = jnp.dot(mask_t.astype(BF16), tri_scr[...], preferred_element_type=F32)
        key_scr[...] = jnp.where(mask_t > 0.5, rank_t, -1.0)
        for gg in range(N_GROUPS):
            cnt = jnp.sum(mask_t[gg:gg + 1, :], axis=-1, keepdims=True)
            extra = jnp.ceil(jnp.maximum(cnt - main_rows, 0.0) * (1.0 / MOE_ROWS))
            nch_ref[gg] = jnp.sum(extra).astype(I32)

    ntail = nch_ref[g]

    @pl.when(first_e)
    def _():
        comb = c_ref[...]
        hi = comb.astype(BF16)
        r1 = comb - hi.astype(F32)
        mid = r1.astype(BF16)
        lo = (r1 - mid.astype(F32)).astype(BF16)

        def compact(start, rows):
            s = onehot_rows(start, rows)
            xc_scr[pl.ds(start, rows), :] = jnp.dot(s, n_ref[...], preferred_element_type=F32).astype(BF16)
            cw_scr[pl.ds(start, rows), :] = (jnp.dot(s, hi, preferred_element_type=F32)
                                             + jnp.dot(s, mid, preferred_element_type=F32)
                                             + jnp.dot(s, lo, preferred_element_type=F32))
            yc_scr[pl.ds(start, rows), :] = jnp.zeros((rows, D_MODEL), F32)

        compact(0, main_rows)
        lax.fori_loop(0, ntail, lambda c, carry: (compact(tail_start(c), MOE_ROWS), carry)[1], 0)

    wg = wg_ref[...].astype(BF16)
    wu = wu_ref[...].astype(BF16)
    wd = wd_ref[...].astype(BF16)
    ex = g * EXPERTS_PER_GROUP + e

    def expert(start, rows):
        x = xc_scr[pl.ds(start, rows), :]
        a = jnp.dot(x, wg, preferred_element_type=F32)
        u = jnp.dot(x, wu, preferred_element_type=F32)
        lane = lax.broadcasted_iota(I32, (rows, LANES), 1)
        cw = jnp.sum(jnp.where(lane == ex, cw_scr[pl.ds(start, rows), :], 0.0), axis=-1, keepdims=True)
        hid = (a * jax.nn.sigmoid(a) * u * cw).astype(BF16)
        yc_scr[pl.ds(start, rows), :] += jnp.dot(hid, wd, preferred_element_type=F32)

    expert(0, main_rows)
    lax.fori_loop(0, ntail, lambda c, carry: (expert(tail_start(c), MOE_ROWS), carry)[1], 0)

    @pl.when((g == 0) & last_e)
    def _():
        residual_copy().wait()

    @pl.when(last_e)
    def _():
        def scatter(start, rows):
            s = onehot_rows(start, rows)
            y = yc_scr[pl.ds(start, rows), :].astype(BF16)
            o_ref[...] += lax.dot_general(s, y, (((0,), (0,)), ((), ())), preferred_element_type=F32)

        scatter(0, main_rows)
        lax.fori_loop(0, ntail, lambda c, carry: (scatter(tail_start(c), MOE_ROWS), carry)[1], 0)


def _moe_routed(n3, comb, h2d, w_gate_l, w_up_l, w_down_l, *, tm):
    n = h2d.shape[0]
    ff = EXPERT_FF // MOE_FF_SPLIT

    def wmap(i, g, e, f):
        return (g * EXPERTS_PER_GROUP + e, 0, f)

    return pl.pallas_call(
        functools.partial(_moe_routed_kernel, tm=tm),
        grid=(n // tm, N_GROUPS, EXPERTS_PER_GROUP, MOE_FF_SPLIT),
        in_specs=[
            pl.BlockSpec((tm, D_MODEL), lambda i, g, e, f: (i, 0), pipeline_mode=pl.Buffered(1)),
            pl.BlockSpec((tm, LANES), lambda i, g, e, f: (i, 0), pipeline_mode=pl.Buffered(1)),
            pl.BlockSpec(memory_space=pl.ANY),
            pl.BlockSpec((None, D_MODEL, ff), wmap),
            pl.BlockSpec((None, D_MODEL, ff), wmap),
            pl.BlockSpec((None, ff, D_MODEL), lambda i, g, e, f: (g * EXPERTS_PER_GROUP + e, f, 0)),
        ],
        out_specs=pl.BlockSpec((tm, D_MODEL), lambda i, g, e, f: (i, 0)),
        out_shape=jax.ShapeDtypeStruct((n, D_MODEL), F32),
        scratch_shapes=[
            pltpu.VMEM((tm, tm), BF16),
            pltpu.VMEM((SUBLANES, tm), F32),
            pltpu.VMEM((tm + MOE_ROWS, D_MODEL), BF16),
            pltpu.VMEM((tm + MOE_ROWS, D_MODEL), F32),
            pltpu.VMEM((tm + MOE_ROWS, LANES), F32),
            pltpu.SMEM((N_GROUPS,), I32),
            pltpu.SemaphoreType.DMA(()),
        ],
        compiler_params=pltpu.CompilerParams(
            dimension_semantics=("arbitrary",) * 4, vmem_limit_bytes=MOE_VMEM_LIMIT),
        name="moe_routed",
    )(n3, comb, h2d, w_gate_l, w_up_l, w_down_l)


def _moe_kernel(n_ref, c_ref, h_ref, wg_ref, wu_ref, wd_ref, o_ref):
    e = pl.program_id(1)

    @pl.when(e == 0)
    def _():
        o_ref[...] = h_ref[...]

    x = n_ref[...]
    a = jnp.dot(x, wg_ref[...].astype(BF16), preferred_element_type=F32)
    u = jnp.dot(x, wu_ref[...].astype(BF16), preferred_element_type=F32)
    lane = lax.broadcasted_iota(I32, c_ref.shape, 1)
    c = jnp.sum(jnp.where(lane == e, c_ref[...], 0.0), axis=-1, keepdims=True)
    hid = (a * jax.nn.sigmoid(a) * u * c).astype(BF16)
    o_ref[...] += jnp.dot(hid, wd_ref[...].astype(BF16), preferred_element_type=F32)


def _moe(n3, comb, h2d, w_gate_l, w_up_l, w_down_l, *, tm):
    n = h2d.shape[0]
    return pl.pallas_call(
        _moe_kernel,
        grid=(n // tm, N_EXPERTS),
        in_specs=[
            pl.BlockSpec((tm, D_MODEL), lambda i, e: (i, 0)),
            pl.BlockSpec((tm, LANES), lambda i, e: (i, 0)),
            pl.BlockSpec((tm, D_MODEL), lambda i, e: (i, 0)),
            pl.BlockSpec((None, D_MODEL, EXPERT_FF), lambda i, e: (e, 0, 0)),
            pl.BlockSpec((None, D_MODEL, EXPERT_FF), lambda i, e: (e, 0, 0)),
            pl.BlockSpec((None, EXPERT_FF, D_MODEL), lambda i, e: (e, 0, 0)),
        ],
        out_specs=pl.BlockSpec((tm, D_MODEL), lambda i, e: (i, 0)),
        out_shape=jax.ShapeDtypeStruct((n, D_MODEL), F32),
        compiler_params=_cparams(("arbitrary", "arbitrary")),
        name="moe",
    )(n3, comb, h2d, w_gate_l, w_up_l, w_down_l)


def _post_mixer(p, b_out, ws_eff, ab_eff, x2d, mk, mv, lw, *, tm_mix, tm_mem, mem_rows, tiles_per_mem,
                rows_per_mem, mem_len, tm_moe):
    h1 = _mixout(p, b_out, ws_eff, ab_eff, x2d, lw["w_out"], tm=tm_mix)
    h2, n3, comb = _memattn(h1, lw["norm_mem"], lw["mem_wq"], lw["mem_qnorm"], mk, mv, lw["mem_wo"],
                            lw["norm_ffn"], lw["w_cat"], lw["b_cat"], tm=tm_mem, mem_rows=mem_rows,
                            tiles_per_mem=tiles_per_mem, rows_per_mem=rows_per_mem, mem_len=mem_len)
    moe = _moe_routed if tm_moe > N_GROUPS * MOE_ROWS else _moe
    return moe(n3, comb, h2, lw["w_gate"], lw["w_up"], lw["w_down"], tm=tm_moe)


def kernel(x_prompt, x_sample, cache_k, cache_v, cache_idx_k, cache_mem_k, cache_mem_v, page_table,
           mem_prompt, norm_mix, w_in, a_vnorm, a_ws, a_b, b_qnorm, b_knorm, rel_bias, w_out,
           norm_mem, mem_wq, mem_wk, mem_wv, mem_qnorm, mem_knorm, mem_wo, norm_ffn,
           w_group, b_group, w_router, b_router, w_gate, w_up, w_down):
    bp, t, d = x_prompt.shape
    bd, s, _ = x_sample.shape
    depth = w_in.shape[0]
    mem_len = mem_prompt.shape[1]
    assert d == D_MODEL and t % KEY_CHUNK == 0 and s <= 8 and (bd * s) % LANES == 0

    bias_pairs, sbias, sconst = _bias_tables(rel_bias)
    hp = x_prompt.reshape(bp * t, d)
    hs = x_sample.reshape(bd * s, d)
    outs = [[] for _ in range(9)]
    for l in range(depth):
        w_cat = jnp.zeros((d, 2 * LANES), F32)
        w_cat = w_cat.at[:, :N_GROUPS].set(w_group[l]).at[:, LANES:LANES + N_EXPERTS].set(w_router[l])
        b_cat = jnp.zeros((1, 2 * LANES), F32)
        b_cat = b_cat.at[0, :N_GROUPS].set(b_group[l]).at[0, LANES:LANES + N_EXPERTS].set(b_router[l])
        lw = dict(w_out=w_out[l], norm_mem=norm_mem[l], mem_wq=mem_wq[l], mem_qnorm=mem_qnorm[l],
                  mem_wo=mem_wo[l], norm_ffn=norm_ffn[l], w_cat=w_cat, b_cat=b_cat)

        p, k, v, tail = _proj(hp, norm_mix[l], w_in[l], a_vnorm[l], b_qnorm[l], b_knorm[l],
                              tm=1024, want_vn=False)
        later_weights = (w_gate[l], w_up[l], w_down[l], w_out[l][None])
        fuse_cast = (bp * (t // Q_BLOCK)) % N_EXPERTS == 0
        b_out, casted = _dsa_prompt(p, tail, bias_pairs, later_weights if fuse_cast else (), batch=bp, seq=t)
        if not fuse_cast:
            casted = tuple(_cast_bf16(w) for w in later_weights)
        lw.update(w_gate=casted[0], w_up=casted[1], w_down=casted[2], w_out=casted[3][0])
        mk, mv = _memkv(mem_prompt.reshape(bp * mem_len, d), mem_wk[l], mem_wv[l], mem_knorm[l])
        hp = _post_mixer(p, b_out, a_ws[l], a_b[l].T, hp, mk, mv, lw, tm_mix=512, tm_mem=512,
                         mem_rows=mem_len, tiles_per_mem=t // 512, rows_per_mem=None, mem_len=mem_len,
                         tm_moe=1024)
        outs[0].append(k.reshape(bp, t, B_KV_HEADS, B_HEAD_DIM))
        outs[1].append(v.reshape(bp, t, B_KV_HEADS, B_HEAD_DIM))
        outs[2].append(tail[:, :IDX_DIM].reshape(bp, t, IDX_DIM))
        outs[3].append(mk.reshape(bp, mem_len, MEM_HEADS, MEM_HEAD_DIM))
        outs[4].append(mv.reshape(bp, mem_len, MEM_HEADS, MEM_HEAD_DIM))

        ns = bd * s
        p_s, k_s, v_s, tail_s, vn_s = _proj(hs, norm_mix[l], w_in[l], a_vnorm[l], b_qnorm[l], b_knorm[l],
                                            tm=ns, want_vn=True)
        b_out_s = _dsa_sample(p_s, tail_s, k_s.reshape(ns, KV_WIDTH), v_s.reshape(ns, KV_WIDTH),
                              cache_k[l], cache_v[l], cache_idx_k[l], page_table,
                              sbias, sconst, dec_batch=bd, dec_seq=s)
        ws_s = jnp.einsum("ab,gts->gatbs", jnp.eye(ns // s, dtype=F32), a_ws[l][:, :s, :s])
        ws_s = ws_s.reshape(A_GROUPS, ns, ns)
        ab_s = jnp.tile(a_b[l][:, :s].T, (ns // s, 1))
        mem_tile = 32
        hs = _post_mixer(p_s, b_out_s, ws_s, ab_s, hs,
                         cache_mem_k[l].reshape(bd * mem_len * MEM_HEADS, MEM_HEAD_DIM),
                         cache_mem_v[l].reshape(bd * mem_len * MEM_HEADS, MEM_HEAD_DIM), lw,
                         tm_mix=ns, tm_mem=mem_tile, mem_rows=mem_tile // s * mem_len, tiles_per_mem=1,
                         rows_per_mem=s, mem_len=mem_len, tm_moe=ns)
        outs[5].append(k_s.reshape(bd, s, B_KV_HEADS, B_HEAD_DIM))
        outs[6].append(v_s.reshape(bd, s, B_KV_HEADS, B_HEAD_DIM))
        outs[7].append(tail_s[:, :IDX_DIM].reshape(bd, s, IDX_DIM))
        outs[8].append(vn_s.reshape(bd, s, A_GROUPS, A_CH))
    return (hp.reshape(bp, t, d), hs.reshape(bd, s, d)) + tuple(jnp.stack(o) for o in outs)
```
